```python
import jax, jax.numpy as jnp
from jax import lax
import numpy as np

D_MODEL = 1024
BATCH = 8
SEQ = 8192
DEPTH = 4

CHUNK = 64
N_EVEN = (DEPTH + 1) // 2
N_ODD = DEPTH // 2
D_FF = 4 * D_MODEL
RMS_EPS = 1e-6
RET_WIDTH = D_MODEL // 2
RET_HEADS = 4
RET_HEAD_DIM = RET_WIDTH // RET_HEADS
RET_ROPE_BASE = 10000.0
GN_EPS = 1e-5
POOL_WIDTH = D_MODEL - RET_WIDTH
POOL_WINDOWS = (2, 4, 8, 16)
POOL_GROUPS = len(POOL_WINDOWS)
POOL_GROUP_DIM = POOL_WIDTH // POOL_GROUPS
AB_IN_WIDTH = 4 * RET_WIDTH + POOL_WIDTH
AB_OUT_WIDTH = RET_WIDTH + POOL_WIDTH
ATT_HEADS = 16
ATT_HEAD_DIM = D_MODEL // ATT_HEADS
LEFT_CHUNKS = 8
BAND = (LEFT_CHUNKS + 1) * CHUNK
REL_CLIP = 128
N_REL = 2 * REL_CLIP + 1
NEG_INF = -1e30

kernel_name = "hybrid_retention_pool_chunkattn_trunk"


def rms_norm(x, g):
    xf = x.astype(jnp.float32)
    y = xf * lax.rsqrt(jnp.mean(xf * xf, axis=-1, keepdims=True) + RMS_EPS)
    return (y * g.astype(jnp.float32)).astype(x.dtype)


def rotary(x):
    S, d = x.shape[1], x.shape[-1]
    inv_freq = 1.0 / (RET_ROPE_BASE ** jnp.linspace(0.0, 1.0, d // 2, dtype=jnp.float32))
    ang = jnp.arange(S, dtype=jnp.float32)[:, None] * inv_freq[None, :]
    cos = jnp.cos(ang)[None, :, None, :]
    sin = jnp.sin(ang)[None, :, None, :]
    x1, x2 = x[..., 0::2], x[..., 1::2]
    return jnp.stack([x1 * cos - x2 * sin, x1 * sin + x2 * cos], axis=-1).reshape(x.shape)


def retention(q, k, v):
    B, S, H, d = q.shape
    nc = S // CHUNK
    log_g = jnp.log1p(-jnp.power(2.0, -5.0 - jnp.arange(H, dtype=jnp.float32)))
    pos = jnp.arange(CHUNK, dtype=jnp.float32)
    intra_decay = jnp.exp(jnp.abs(pos[:, None] - pos[None, :])[None] * log_g[:, None, None])
    q_decay = jnp.exp((pos[:, None] + 1.0) * log_g[None, :])
    k_decay = jnp.exp((CHUNK - 1.0 - pos[:, None]) * log_g[None, :])
    chunk_decay = jnp.exp(CHUNK * log_g)

    qc = q.reshape(B, nc, CHUNK, H, d)
    kc = k.reshape(B, nc, CHUNK, H, d)
    vc = v.reshape(B, nc, CHUNK, H, d)
    scores = jnp.einsum('bcnhd,bcmhd->bchnm', qc, kc) * intra_decay
    intra = jnp.einsum('bchnm,bcmhe->bcnhe', scores, vc)

    def step(state, inp):
        q_i, k_i, v_i = inp
        cross = jnp.einsum('bnhd,bhde->bnhe', q_i * q_decay[:, :, None], state)
        state = state * chunk_decay[:, None, None] + jnp.einsum(
            'bmhd,bmhe->bhde', k_i * k_decay[:, :, None], v_i)
        return state, cross

    xs = (jnp.moveaxis(qc, 1, 0), jnp.moveaxis(kc, 1, 0), jnp.moveaxis(vc, 1, 0))
    state0 = jnp.zeros((B, H, d, d), jnp.float32)
    _, cross = lax.scan(step, state0, xs)
    out = intra + jnp.moveaxis(cross, 0, 1)
    return out.reshape(B, S, H, d)


def head_group_norm(o, g):
    B, S, H, d = o.shape
    mu = jnp.mean(o, axis=-1, keepdims=True)
    var = jnp.mean(jnp.square(o - mu), axis=-1, keepdims=True)
    y = (o - mu) * lax.rsqrt(var + GN_EPS)
    return y.reshape(B, S, H * d) * g


def multiscale_pool(p, w_group, scale):
    B, S, _ = p.shape
    pf = p.astype(jnp.float32).reshape(B, S, POOL_GROUPS, POOL_GROUP_DIM)
    csum = lax.cumsum(pf, axis=1)
    t = jnp.arange(S)
    outs = []
    for gi, w in enumerate(POOL_WINDOWS):
        c = csum[:, :, gi]
        lagged = jnp.pad(c, ((0, 0), (w, 0), (0, 0)))[:, :S]
        count = jnp.minimum(t + 1, w).astype(jnp.float32)[None, :, None]
        outs.append((c - lagged) / count - pf[:, :, gi])
    pooled = jnp.stack(outs, axis=2)
    mixed = jnp.einsum('bsgc,gce->bsge', pooled, w_group.astype(jnp.float32))
    return (mixed.reshape(B, S, POOL_WIDTH) * scale.astype(jnp.float32)).astype(p.dtype)


def retention_pool_mixer(h, w_in, gn_gain, w_pool, pool_scale, w_out):
    B, S, _ = h.shape
    z = h @ w_in
    q, k, v, g, p = jnp.split(
        z, [RET_WIDTH, 2 * RET_WIDTH, 3 * RET_WIDTH, 4 * RET_WIDTH], axis=-1)

    def heads(t):
        return t.astype(jnp.float32).reshape(B, S, RET_HEADS, RET_HEAD_DIM)

    qh = rotary(heads(q))
    kh = rotary(heads(k)) * (RET_HEAD_DIM ** -0.5)
    o = retention(qh, kh, heads(v))
    o = head_group_norm(o, gn_gain.astype(jnp.float32))
    ret_out = (jax.nn.silu(g.astype(jnp.float32)) * o).astype(h.dtype)
    pool_out = multiscale_pool(p, w_pool, pool_scale)
    return jnp.concatenate([ret_out, pool_out], axis=-1) @ w_out


def chunk_attention(h, w_qkv, rel_bias, w_out):
    B, S, _ = h.shape
    nc = S // CHUNK
    q, k, v = jnp.split(h @ w_qkv, 3, axis=-1)
    q = q.reshape(B, S, ATT_HEADS, ATT_HEAD_DIM) * (ATT_HEAD_DIM ** -0.5)
    pad = ((0, 0), (LEFT_CHUNKS * CHUNK, 0), (0, 0), (0, 0))
    k = jnp.pad(k.reshape(B, S, ATT_HEADS, ATT_HEAD_DIM), pad)
    v = jnp.pad(v.reshape(B, S, ATT_HEADS, ATT_HEAD_DIM), pad)
    n_idx = jnp.arange(CHUNK)[:, None]
    j_idx = jnp.arange(BAND)[None, :]
    rel = jnp.clip(n_idx + LEFT_CHUNKS * CHUNK - j_idx, -REL_CLIP, REL_CLIP) + REL_CLIP
    bias = rel_bias.astype(jnp.float32)[:, rel]
    band_pos = jnp.arange(BAND)

    def one_chunk(c):
        q_c = lax.dynamic_slice_in_dim(q, c * CHUNK, CHUNK, axis=1)
        k_b = lax.dynamic_slice_in_dim(k, c * CHUNK, BAND, axis=1)
        v_b = lax.dynamic_slice_in_dim(v, c * CHUNK, BAND, axis=1)
        s = jnp.einsum('bnhd,bjhd->bhnj', q_c, k_b).astype(jnp.float32) + bias
        valid = band_pos >= (LEFT_CHUNKS - c) * CHUNK
        s = jnp.where(valid, s, NEG_INF)
        pr = jax.nn.softmax(s, axis=-1).astype(v_b.dtype)
        return jnp.einsum('bhnj,bjhd->bnhd', pr, v_b)

    o = lax.map(one_chunk, jnp.arange(nc))
    o = jnp.moveaxis(o, 0, 1).reshape(B, S, D_MODEL)
    return o @ w_out


def squared_relu_mlp(h, w1, w2):
    return jnp.square(jax.nn.relu(h @ w1)) @ w2


def _fwd_setup_inputs(seed: int = 0) -> dict:
    key = jax.random.key(seed)
    ks = jax.random.split(key, 16)
    f32 = jnp.float32

    def nrm(k, shape, scale):
        return jax.random.normal(k, shape, f32) * scale

    return {
        "x": nrm(ks[0], (BATCH, SEQ, D_MODEL), 1.0),
        "mix_norm": 1.0 + nrm(ks[1], (DEPTH, D_MODEL), 0.05),
        "ffn_norm": 1.0 + nrm(ks[2], (DEPTH, D_MODEL), 0.05),
        "w_ffn_in": nrm(ks[3], (DEPTH, D_MODEL, D_FF), D_MODEL ** -0.5),
        "w_ffn_out": nrm(ks[4], (DEPTH, D_FF, D_MODEL), D_FF ** -0.5),
        "ab_w_in": nrm(ks[5], (N_EVEN, D_MODEL, AB_IN_WIDTH), D_MODEL ** -0.5),
        "ab_gn_gain": 1.0 + nrm(ks[6], (N_EVEN, RET_WIDTH), 0.05),
        "ab_w_pool": nrm(ks[7], (N_EVEN, POOL_GROUPS, POOL_GROUP_DIM, POOL_GROUP_DIM), POOL_GROUP_DIM ** -0.5),
        "ab_pool_scale": 1.0 + nrm(ks[8], (N_EVEN, POOL_WIDTH), 0.1),
        "ab_w_out": nrm(ks[9], (N_EVEN, AB_OUT_WIDTH, D_MODEL), AB_OUT_WIDTH ** -0.5),
        "c_w_qkv": nrm(ks[10], (N_ODD, D_MODEL, 3 * D_MODEL), D_MODEL ** -0.5),
        "c_rel_bias": nrm(ks[11], (N_ODD, ATT_HEADS, N_REL), 0.5),
        "c_w_out": nrm(ks[12], (N_ODD, D_MODEL, D_MODEL), D_MODEL ** -0.5),
        "final_norm": 1.0 + nrm(ks[13], (D_MODEL,), 0.05),
    }


def _fwd_reference(x, mix_norm, ffn_norm, w_ffn_in, w_ffn_out, ab_w_in, ab_gn_gain,
              ab_w_pool, ab_pool_scale, ab_w_out, c_w_qkv, c_rel_bias, c_w_out,
              final_norm):
    for layer in range(DEPTH):
        h = rms_norm(x, mix_norm[layer])
        i = layer // 2
        if layer % 2 == 0:
            x = x + retention_pool_mixer(h, ab_w_in[i], ab_gn_gain[i], ab_w_pool[i],
                                         ab_pool_scale[i], ab_w_out[i])
        else:
            x = x + chunk_attention(h, c_w_qkv[i], c_rel_bias[i], c_w_out[i])
        x = x + squared_relu_mlp(rms_norm(x, ffn_norm[layer]), w_ffn_in[layer], w_ffn_out[layer])
    return rms_norm(x, final_norm)


import jax as _jax
import jax.numpy as _jnp

TWIN_FORMAT = 'train_step'
FWD_PARAMS = ['x', 'mix_norm', 'ffn_norm', 'w_ffn_in', 'w_ffn_out', 'ab_w_in', 'ab_gn_gain', 'ab_w_pool', 'ab_pool_scale', 'ab_w_out', 'c_w_qkv', 'c_rel_bias', 'c_w_out', 'final_norm']
TWIN_WEIGHTS = ['mix_norm', 'ffn_norm', 'w_ffn_in', 'w_ffn_out', 'ab_w_in', 'ab_gn_gain', 'ab_w_pool', 'ab_pool_scale', 'ab_w_out', 'c_w_qkv', 'c_rel_bias', 'c_w_out', 'final_norm']
TWIN_DIFF_INPUT = 'x'
TWIN_INPUTS = ['x', 'mix_norm', 'ffn_norm', 'w_ffn_in', 'w_ffn_out', 'ab_w_in', 'ab_gn_gain', 'ab_w_pool', 'ab_pool_scale', 'ab_w_out', 'c_w_qkv', 'c_rel_bias', 'c_w_out', 'final_norm', 'loss_target', 'm_mix_norm', 'm_ffn_norm', 'm_w_ffn_in', 'm_w_ffn_out', 'm_ab_w_in', 'm_ab_gn_gain', 'm_ab_w_pool', 'm_ab_pool_scale', 'm_ab_w_out', 'm_c_w_qkv', 'm_c_rel_bias', 'm_c_w_out', 'm_final_norm', 'v_mix_norm', 'v_ffn_norm', 'v_w_ffn_in', 'v_w_ffn_out', 'v_ab_w_in', 'v_ab_gn_gain', 'v_ab_w_pool', 'v_ab_pool_scale', 'v_ab_w_out', 'v_c_w_qkv', 'v_c_rel_bias', 'v_c_w_out', 'v_final_norm']
TWIN_OUTPUTS = ['loss', 'grad_x', 'grad_mix_norm', 'grad_ffn_norm', 'grad_w_ffn_in', 'grad_w_ffn_out', 'grad_ab_w_in', 'grad_ab_gn_gain', 'grad_ab_w_pool', 'grad_ab_pool_scale', 'grad_ab_w_out', 'grad_c_w_qkv', 'grad_c_rel_bias', 'grad_c_w_out', 'grad_final_norm', 'delta_mix_norm', 'delta_ffn_norm', 'delta_w_ffn_in', 'delta_w_ffn_out', 'delta_ab_w_in', 'delta_ab_gn_gain', 'delta_ab_w_pool', 'delta_ab_pool_scale', 'delta_ab_w_out', 'delta_c_w_qkv', 'delta_c_rel_bias', 'delta_c_w_out', 'delta_final_norm', 'new_m_mix_norm', 'new_m_ffn_norm', 'new_m_w_ffn_in', 'new_m_w_ffn_out', 'new_m_ab_w_in', 'new_m_ab_gn_gain', 'new_m_ab_w_pool', 'new_m_ab_pool_scale', 'new_m_ab_w_out', 'new_m_c_w_qkv', 'new_m_c_rel_bias', 'new_m_c_w_out', 'new_m_final_norm', 'new_v_mix_norm', 'new_v_ffn_norm', 'new_v_w_ffn_in', 'new_v_w_ffn_out', 'new_v_ab_w_in', 'new_v_ab_gn_gain', 'new_v_ab_w_pool', 'new_v_ab_pool_scale', 'new_v_ab_w_out', 'new_v_c_w_qkv', 'new_v_c_rel_bias', 'new_v_c_w_out', 'new_v_final_norm']
TWIN_LEAF_KINDS = {'loss': 'loss', 'grad_x': 'grad_x', 'grad_mix_norm': 'grad_w', 'grad_ffn_norm': 'grad_w', 'grad_w_ffn_in': 'grad_w', 'grad_w_ffn_out': 'grad_w', 'grad_ab_w_in': 'grad_w', 'grad_ab_gn_gain': 'grad_w', 'grad_ab_w_pool': 'grad_w', 'grad_ab_pool_scale': 'grad_w', 'grad_ab_w_out': 'grad_w', 'grad_c_w_qkv': 'grad_w', 'grad_c_rel_bias': 'grad_w', 'grad_c_w_out': 'grad_w', 'grad_final_norm': 'grad_w', 'delta_mix_norm': 'delta_w', 'delta_ffn_norm': 'delta_w', 'delta_w_ffn_in': 'delta_w', 'delta_w_ffn_out': 'delta_w', 'delta_ab_w_in': 'delta_w', 'delta_ab_gn_gain': 'delta_w', 'delta_ab_w_pool': 'delta_w', 'delta_ab_pool_scale': 'delta_w', 'delta_ab_w_out': 'delta_w', 'delta_c_w_qkv': 'delta_w', 'delta_c_rel_bias': 'delta_w', 'delta_c_w_out': 'delta_w', 'delta_final_norm': 'delta_w', 'new_m_mix_norm': 'new_m', 'new_m_ffn_norm': 'new_m', 'new_m_w_ffn_in': 'new_m', 'new_m_w_ffn_out': 'new_m', 'new_m_ab_w_in': 'new_m', 'new_m_ab_gn_gain': 'new_m', 'new_m_ab_w_pool': 'new_m', 'new_m_ab_pool_scale': 'new_m', 'new_m_ab_w_out': 'new_m', 'new_m_c_w_qkv': 'new_m', 'new_m_c_rel_bias': 'new_m', 'new_m_c_w_out': 'new_m', 'new_m_final_norm': 'new_m', 'new_v_mix_norm': 'new_v', 'new_v_ffn_norm': 'new_v', 'new_v_w_ffn_in': 'new_v', 'new_v_w_ffn_out': 'new_v', 'new_v_ab_w_in': 'new_v', 'new_v_ab_gn_gain': 'new_v', 'new_v_ab_w_pool': 'new_v', 'new_v_ab_pool_scale': 'new_v', 'new_v_ab_w_out': 'new_v', 'new_v_c_w_qkv': 'new_v', 'new_v_c_rel_bias': 'new_v', 'new_v_c_w_out': 'new_v', 'new_v_final_norm': 'new_v'}


def _forward(args):
    return _fwd_reference(*[args[k] for k in FWD_PARAMS])


def _output_shape():
    out = _jax.eval_shape(lambda: _forward(_fwd_setup_inputs(0)))
    return out.shape, out.dtype

N_MICROBATCH = 1
ADAM_LR = 0.001
ADAM_B1 = 0.9
ADAM_B2 = 0.999
ADAM_EPS = 1e-08
ADAM_WD = 0.01
ADAM_STEP = 10
PER_EXAMPLE_BATCH_AXIS = {'x': 0, 'loss_target': 0}
SHARED_INPUTS = []
_WEIGHT_DTYPES = {'mix_norm': _jnp.float32, 'ffn_norm': _jnp.float32, 'w_ffn_in': _jnp.float32, 'w_ffn_out': _jnp.float32, 'ab_w_in': _jnp.float32, 'ab_gn_gain': _jnp.float32, 'ab_w_pool': _jnp.float32, 'ab_pool_scale': _jnp.float32, 'ab_w_out': _jnp.float32, 'c_w_qkv': _jnp.float32, 'c_rel_bias': _jnp.float32, 'c_w_out': _jnp.float32, 'final_norm': _jnp.float32}
MOMENT_SCALE = {'mix_norm': 2.112126e-01, 'ffn_norm': 2.243311e-01, 'w_ffn_in': 1.087019e-01, 'w_ffn_out': 3.664618e-01, 'ab_w_in': 1.353198e-01, 'ab_gn_gain': 1.259752e-01, 'ab_w_pool': 1.748354e-01, 'ab_pool_scale': 1.685966e-01, 'ab_w_out': 1.495308e-01, 'c_w_qkv': 1.299564e-01, 'c_rel_bias': 9.384993e-03, 'c_w_out': 2.314463e-01, 'final_norm': 6.585736e+01}


def _to_microbatches(a, axis):
    t = _jnp.moveaxis(a, axis, 0)
    t = t.reshape((N_MICROBATCH, t.shape[0] // N_MICROBATCH) + t.shape[1:])
    return _jnp.moveaxis(t, 1, axis + 1)


def setup_inputs(seed: int = 0) -> dict:
    inp = _fwd_setup_inputs(seed)
    key = _jax.random.fold_in(_jax.random.key(seed), 7919)
    shape, _ = _output_shape()
    out = dict(inp)
    out["loss_target"] = _jax.random.normal(_jax.random.fold_in(key, 0), shape, _jnp.float32)
    for i, name in enumerate(TWIN_WEIGHTS):
        w = inp[name].astype(_jnp.float32)
        if MOMENT_SCALE is None:
            s = _jnp.sqrt(_jnp.mean(_jnp.square(w)) + 1e-30)
        else:
            s = MOMENT_SCALE[name]
        km, kv = _jax.random.split(_jax.random.fold_in(key, i + 1))
        out[name] = w
        out["m_" + name] = s * _jax.random.normal(km, w.shape, _jnp.float32)
        out["v_" + name] = (s * s) * _jax.random.uniform(kv, w.shape, _jnp.float32, 0.5, 1.5)
    if N_MICROBATCH > 1:
        for name, axis in PER_EXAMPLE_BATCH_AXIS.items():
            out[name] = _to_microbatches(out[name], axis)
    return {'x': out['x'], 'mix_norm': out['mix_norm'], 'ffn_norm': out['ffn_norm'], 'w_ffn_in': out['w_ffn_in'], 'w_ffn_out': out['w_ffn_out'], 'ab_w_in': out['ab_w_in'], 'ab_gn_gain': out['ab_gn_gain'], 'ab_w_pool': out['ab_w_pool'], 'ab_pool_scale': out['ab_pool_scale'], 'ab_w_out': out['ab_w_out'], 'c_w_qkv': out['c_w_qkv'], 'c_rel_bias': out['c_rel_bias'], 'c_w_out': out['c_w_out'], 'final_norm': out['final_norm'], 'loss_target': out['loss_target'], 'm_mix_norm': out['m_mix_norm'], 'm_ffn_norm': out['m_ffn_norm'], 'm_w_ffn_in': out['m_w_ffn_in'], 'm_w_ffn_out': out['m_w_ffn_out'], 'm_ab_w_in': out['m_ab_w_in'], 'm_ab_gn_gain': out['m_ab_gn_gain'], 'm_ab_w_pool': out['m_ab_w_pool'], 'm_ab_pool_scale': out['m_ab_pool_scale'], 'm_ab_w_out': out['m_ab_w_out'], 'm_c_w_qkv': out['m_c_w_qkv'], 'm_c_rel_bias': out['m_c_rel_bias'], 'm_c_w_out': out['m_c_w_out'], 'm_final_norm': out['m_final_norm'], 'v_mix_norm': out['v_mix_norm'], 'v_ffn_norm': out['v_ffn_norm'], 'v_w_ffn_in': out['v_w_ffn_in'], 'v_w_ffn_out': out['v_w_ffn_out'], 'v_ab_w_in': out['v_ab_w_in'], 'v_ab_gn_gain': out['v_ab_gn_gain'], 'v_ab_w_pool': out['v_ab_w_pool'], 'v_ab_pool_scale': out['v_ab_pool_scale'], 'v_ab_w_out': out['v_ab_w_out'], 'v_c_w_qkv': out['v_c_w_qkv'], 'v_c_rel_bias': out['v_c_rel_bias'], 'v_c_w_out': out['v_c_w_out'], 'v_final_norm': out['v_final_norm']}


def _loss(weights, diff, rest, loss_target):
    with _jax.named_scope("forward"):
        args = {**rest, TWIN_DIFF_INPUT: diff, **{k: w.astype(_WEIGHT_DTYPES[k]) for k, w in weights.items()}}
        y = _forward(args)
    with _jax.named_scope("loss_head"):
        err = _jnp.square(y.astype(_jnp.float32) - loss_target)
        return 0.5 * _jnp.sum(_jnp.mean(err, axis=-1)) if err.ndim else 0.5 * err


def _adamw(w, g, m, v):
    m = ADAM_B1 * m + (1.0 - ADAM_B1) * g
    v = ADAM_B2 * v + (1.0 - ADAM_B2) * _jnp.square(g)
    m_hat = m / (1.0 - ADAM_B1 ** ADAM_STEP)
    v_hat = v / (1.0 - ADAM_B2 ** ADAM_STEP)
    delta = -ADAM_LR * (m_hat / (_jnp.sqrt(v_hat) + ADAM_EPS) + ADAM_WD * w)
    return delta, m, v


def reference(x, mix_norm, ffn_norm, w_ffn_in, w_ffn_out, ab_w_in, ab_gn_gain, ab_w_pool, ab_pool_scale, ab_w_out, c_w_qkv, c_rel_bias, c_w_out, final_norm, loss_target, m_mix_norm, m_ffn_norm, m_w_ffn_in, m_w_ffn_out, m_ab_w_in, m_ab_gn_gain, m_ab_w_pool, m_ab_pool_scale, m_ab_w_out, m_c_w_qkv, m_c_rel_bias, m_c_w_out, m_final_norm, v_mix_norm, v_ffn_norm, v_w_ffn_in, v_w_ffn_out, v_ab_w_in, v_ab_gn_gain, v_ab_w_pool, v_ab_pool_scale, v_ab_w_out, v_c_w_qkv, v_c_rel_bias, v_c_w_out, v_final_norm):
    given = dict(x=x, mix_norm=mix_norm, ffn_norm=ffn_norm, w_ffn_in=w_ffn_in, w_ffn_out=w_ffn_out, ab_w_in=ab_w_in, ab_gn_gain=ab_gn_gain, ab_w_pool=ab_w_pool, ab_pool_scale=ab_pool_scale, ab_w_out=ab_w_out, c_w_qkv=c_w_qkv, c_rel_bias=c_rel_bias, c_w_out=c_w_out, final_norm=final_norm, loss_target=loss_target, m_mix_norm=m_mix_norm, m_ffn_norm=m_ffn_norm, m_w_ffn_in=m_w_ffn_in, m_w_ffn_out=m_w_ffn_out, m_ab_w_in=m_ab_w_in, m_ab_gn_gain=m_ab_gn_gain, m_ab_w_pool=m_ab_w_pool, m_ab_pool_scale=m_ab_pool_scale, m_ab_w_out=m_ab_w_out, m_c_w_qkv=m_c_w_qkv, m_c_rel_bias=m_c_rel_bias, m_c_w_out=m_c_w_out, m_final_norm=m_final_norm, v_mix_norm=v_mix_norm, v_ffn_norm=v_ffn_norm, v_w_ffn_in=v_w_ffn_in, v_w_ffn_out=v_w_ffn_out, v_ab_w_in=v_ab_w_in, v_ab_gn_gain=v_ab_gn_gain, v_ab_w_pool=v_ab_w_pool, v_ab_pool_scale=v_ab_pool_scale, v_ab_w_out=v_ab_w_out, v_c_w_qkv=v_c_w_qkv, v_c_rel_bias=v_c_rel_bias, v_c_w_out=v_c_w_out, v_final_norm=v_final_norm)
    weights = {n: given[n] for n in TWIN_WEIGHTS}
    shared = {n: given[n] for n in SHARED_INPUTS}
    per_example = {n: given[n] for n in ['x']}
    grad_fn = _jax.value_and_grad(_loss, argnums=(0, 1))

    def one_microbatch(ex, loss_target):
        ex = dict(ex)
        diff = ex.pop(TWIN_DIFF_INPUT)
        return grad_fn(weights, diff, {**shared, **ex}, loss_target)

    if N_MICROBATCH == 1:
        loss, (grad_w, grad_x) = one_microbatch(per_example, given["loss_target"])
    else:
        def body(carry, xs):
            loss_sum, grad_sum = carry
            l_k, (gw_k, gx_k) = one_microbatch(xs[0], xs[1])
            with _jax.named_scope("update"):
                return (loss_sum + l_k, _jax.tree.map(_jnp.add, grad_sum, gw_k)), gx_k

        init = (_jnp.zeros((), _jnp.float32), _jax.tree.map(_jnp.zeros_like, weights))
        (loss, grad_w), grad_x = _jax.lax.scan(body, init, (per_example, given["loss_target"]))
    with _jax.named_scope("update"):
        delta_w, new_m, new_v = {}, {}, {}
        for n in TWIN_WEIGHTS:
            delta_w[n], new_m[n], new_v[n] = _adamw(weights[n], grad_w[n], given["m_" + n], given["v_" + n])
    return (loss, grad_x, *[grad_w[n] for n in TWIN_WEIGHTS], *[delta_w[n] for n in TWIN_WEIGHTS],
            *[new_m[n] for n in TWIN_WEIGHTS], *[new_v[n] for n in TWIN_WEIGHTS])
```

```python
import functools

import jax
import jax.numpy as jnp
from jax import lax
from jax.experimental import pallas as pl
from jax.experimental.pallas import tpu as pltpu

F32 = jnp.float32
BF16 = jnp.bfloat16
SDS = jax.ShapeDtypeStruct
MESH_AXES = ("x", "y", "c")
N_DEV = 8

D_MODEL = 1024
DEPTH = 4
CHUNK = 64
D_FF = 4 * D_MODEL
RMS_EPS = 1e-6
RET_WIDTH = 512
RET_HEADS = 4
RET_HEAD_DIM = 128
RET_ROPE_BASE = 10000.0
GN_EPS = 1e-5
POOL_WIDTH = 512
POOL_WINDOWS = (2, 4, 8, 16)
POOL_HALO = 16
AB_IN_WIDTH = 4 * RET_WIDTH + POOL_WIDTH
ATT_HEADS = 16
ATT_HEAD_DIM = 64
LEFT_CHUNKS = 8
REL_CLIP = 128
N_REL = 2 * REL_CLIP + 1
NEG_INF = -1e30

ADAM_LR = 0.001
ADAM_B1 = 0.9
ADAM_B2 = 0.999
ADAM_EPS = 1e-08
ADAM_WD = 0.01
ADAM_STEP = 10

TOKEN_TILE = 512
ATT_Q_TILE = 256
ATT_K_TILE = 3 * ATT_Q_TILE
ATT_DIAG = 1024
REL_PAD = 384
VMEM_LIMIT_MB = 56

NT = (((1,), (1,)), ((), ()))
TN = (((0,), (0,)), ((), ()))


def _params(semantics, **kw):
    return pltpu.CompilerParams(dimension_semantics=semantics,
                                vmem_limit_bytes=VMEM_LIMIT_MB * 2 ** 20, **kw)


def _dot(a, b, dims=None):
    if dims is None:
        return jnp.dot(a, b, preferred_element_type=F32)
    return lax.dot_general(a, b, dims, preferred_element_type=F32)


def _bf(v):
    return v.astype(BF16)


def _tile(n, t):
    return min(n, t)


def _norm_mm(x, gain, w, tn, z_dtype, relu2, name):
    s, d = x.shape
    nj = w.shape[0]
    tm = _tile(s, TOKEN_TILE)

    def body(x_ref, g_ref, w_ref, h_ref, z_ref, *a_ref):
        @pl.when(pl.program_id(1) == 0)
        def _():
            xv = x_ref[...]
            r = lax.rsqrt(jnp.mean(xv * xv, axis=-1, keepdims=True) + RMS_EPS)
            h_ref[...] = _bf(xv * r * g_ref[...])

        z = _dot(h_ref[...], w_ref[...])
        z_ref[...] = z.astype(z_ref.dtype)
        if relu2:
            a_ref[0][...] = _bf(jnp.square(jnp.maximum(z, 0.0)))

    out_shape = [SDS((s, d), BF16), SDS((s, nj * tn), z_dtype)]
    out_specs = [pl.BlockSpec((tm, d), lambda i, j: (i, 0)), pl.BlockSpec((tm, tn), lambda i, j: (i, j))]
    if relu2:
        out_shape.append(SDS((s, nj * tn), BF16))
        out_specs.append(pl.BlockSpec((tm, tn), lambda i, j: (i, j)))
    return pl.pallas_call(
        body, name=name, grid=(s // tm, nj), out_shape=out_shape,
        in_specs=[pl.BlockSpec((tm, d), lambda i, j: (i, 0)),
                  pl.BlockSpec((1, d), lambda i, j: (0, 0)),
                  pl.BlockSpec((None, d, tn), lambda i, j: (j, 0, 0))],
        out_specs=out_specs,
        compiler_params=_params(("parallel", "arbitrary")),
    )(x, gain, w)


def _mm_res(parts, w, res, name):
    s, d = res.shape
    tm = _tile(s, TOKEN_TILE)
    widths = [p.shape[1] for p in parts]

    def body(*refs):
        a_refs = refs[:len(parts)]
        w_ref, res_ref, o_ref = refs[len(parts):]
        acc = res_ref[...]
        off = 0
        for a_ref, k in zip(a_refs, widths):
            acc = acc + _dot(a_ref[...], w_ref[off:off + k, :])
            off += k
        o_ref[...] = acc

    return pl.pallas_call(
        body, name=name, grid=(s // tm,), out_shape=SDS((s, d), F32),
        in_specs=[pl.BlockSpec((tm, k), lambda i: (i, 0)) for k in widths]
        + [pl.BlockSpec(w.shape, lambda i: (0, 0)), pl.BlockSpec((tm, d), lambda i: (i, 0))],
        out_specs=pl.BlockSpec((tm, d), lambda i: (i, 0)),
        compiler_params=_params(("parallel",)),
    )(*parts, w, res)


def _mm_nt_rows(dy, w, z, tk, name):
    s, d = dy.shape
    k = w.shape[0]
    tm = _tile(s, TOKEN_TILE)

    def body(dy_ref, w_ref, *rest):
        o_ref = rest[-1]
        da = _dot(_bf(dy_ref[...]), w_ref[...], NT)
        if z is not None:
            da = da * (2.0 * jnp.maximum(rest[0][...].astype(F32), 0.0))
        o_ref[...] = _bf(da)

    in_specs = [pl.BlockSpec((tm, d), lambda i, j: (i, 0)), pl.BlockSpec((tk, d), lambda i, j: (j, 0))]
    args = [dy, w]
    if z is not None:
        in_specs.append(pl.BlockSpec((tm, tk), lambda i, j: (i, j)))
        args.append(z)
    return pl.pallas_call(
        body, name=name, grid=(s // tm, k // tk), out_shape=SDS((s, k), BF16),
        in_specs=in_specs, out_specs=pl.BlockSpec((tm, tk), lambda i, j: (i, j)),
        compiler_params=_params(("parallel", "parallel")),
    )(*args)


def _mm_nt_normbwd(dz, w, x, gain, dres, name):
    s, d = x.shape
    nj, _, nc = w.shape
    tm = _tile(s, TOKEN_TILE)

    def body(dz_ref, w_ref, x_ref, g_ref, dres_ref, dx_ref, dg_ref):
        dh = _dot(dz_ref[:, 0:nc], w_ref[0], NT)
        for j in range(1, nj):
            dh = dh + _dot(dz_ref[:, j * nc:(j + 1) * nc], w_ref[j], NT)
        xv = x_ref[...]
        r = lax.rsqrt(jnp.mean(xv * xv, axis=-1, keepdims=True) + RMS_EPS)
        xn = xv * r

        @pl.when(pl.program_id(0) == 0)
        def _():
            dg_ref[...] = jnp.zeros_like(dg_ref)

        dg_ref[...] += jnp.sum(dh * xn, axis=0, keepdims=True)
        dxh = dh * g_ref[...]
        dx_ref[...] = dres_ref[...] + r * (dxh - xn * jnp.mean(dxh * xn, axis=-1, keepdims=True))

    return pl.pallas_call(
        body, name=name, grid=(s // tm,), out_shape=[SDS((s, d), F32), SDS((1, d), F32)],
        in_specs=[pl.BlockSpec((tm, nj * nc), lambda i: (i, 0)),
                  pl.BlockSpec((nj, d, nc), lambda i: (0, 0, 0)),
                  pl.BlockSpec((tm, d), lambda i: (i, 0)),
                  pl.BlockSpec((1, d), lambda i: (0, 0)),
                  pl.BlockSpec((tm, d), lambda i: (i, 0))],
        out_specs=[pl.BlockSpec((tm, d), lambda i: (i, 0)), pl.BlockSpec((1, d), lambda i: (0, 0))],
        compiler_params=_params(("arbitrary",)),
    )(dz, w, x, gain, dres)


def _mm_tn(a, b, ka, nb, nj, a_tiled, name):
    s = a.shape[0]
    tm = _tile(s, TOKEN_TILE)
    nm = s // tm

    def body(a_ref, b_ref, o_ref, acc):
        m = pl.program_id(1)

        @pl.when(m == 0)
        def _():
            acc[...] = jnp.zeros_like(acc)

        acc[...] += _dot(_bf(a_ref[...]), _bf(b_ref[...]), TN)

        @pl.when(m == nm - 1)
        def _():
            o_ref[...] = acc[...].astype(o_ref.dtype)

    return pl.pallas_call(
        body, name=name, grid=(nj, nm), out_shape=SDS((nj, ka, nb), BF16),
        in_specs=[pl.BlockSpec((tm, ka), (lambda j, m: (m, j)) if a_tiled else (lambda j, m: (m, 0))),
                  pl.BlockSpec((tm, nb), (lambda j, m: (m, 0)) if a_tiled else (lambda j, m: (m, j)))],
        out_specs=pl.BlockSpec((None, ka, nb), lambda j, m: (j, 0, 0)),
        scratch_shapes=[pltpu.VMEM((ka, nb), F32)],
        compiler_params=_params(("parallel", "arbitrary")),
    )(a, b)


def _final_loss(x, gain, target, name):
    s, d = x.shape
    tm = _tile(s, TOKEN_TILE)

    def body(x_ref, g_ref, t_ref, loss_ref, dx_ref, dg_ref):
        @pl.when(pl.program_id(0) == 0)
        def _():
            loss_ref[...] = jnp.zeros_like(loss_ref)
            dg_ref[...] = jnp.zeros_like(dg_ref)

        xv = x_ref[...]
        r = lax.rsqrt(jnp.mean(xv * xv, axis=-1, keepdims=True) + RMS_EPS)
        xn = xv * r
        err = xn * g_ref[...] - t_ref[...]
        loss_ref[...] += (0.5 / d) * jnp.sum(err * err)
        dy = err * (1.0 / d)
        dg_ref[...] += jnp.sum(dy * xn, axis=0, keepdims=True)
        dxh = dy * g_ref[...]
        dx_ref[...] = r * (dxh - xn * jnp.mean(dxh * xn, axis=-1, keepdims=True))

    return pl.pallas_call(
        body, name=name, grid=(s // tm,),
        out_shape=[SDS((8, 128), F32), SDS((s, d), F32), SDS((1, d), F32)],
        in_specs=[pl.BlockSpec((tm, d), lambda i: (i, 0)), pl.BlockSpec((1, d), lambda i: (0, 0)),
                  pl.BlockSpec((tm, d), lambda i: (i, 0))],
        out_specs=[pl.BlockSpec((8, 128), lambda i: (0, 0)), pl.BlockSpec((tm, d), lambda i: (i, 0)),
                   pl.BlockSpec((1, d), lambda i: (0, 0))],
        compiler_params=_params(("arbitrary",)),
    )(x, gain, target)


def _retention_tables(s):
    half = RET_HEAD_DIM // 2
    inv_freq = 1.0 / (RET_ROPE_BASE ** jnp.linspace(0.0, 1.0, half, dtype=F32))
    ang = jnp.arange(s, dtype=F32)[:, None] * inv_freq[None, :]
    cos, sin = jnp.cos(ang), jnp.sin(ang)
    cos_e = jnp.repeat(cos, 2, axis=-1)
    sin_s = jnp.stack([-sin, sin], axis=-1).reshape(s, RET_HEAD_DIM)
    log_g = jnp.log1p(-jnp.power(2.0, -5.0 - jnp.arange(RET_HEADS, dtype=F32)))
    pos = jnp.arange(CHUNK, dtype=F32)
    dmat = jnp.exp(jnp.abs(pos[:, None] - pos[None, :])[None] * log_g[:, None, None])
    qdec = jnp.exp((pos[None, :] + 1.0) * log_g[:, None])
    kdec = jnp.exp((CHUNK - 1.0 - pos[None, :]) * log_g[:, None])
    lam = jnp.exp(CHUNK * log_g)
    wide = (RET_HEADS, CHUNK, RET_HEAD_DIM)
    return dict(cos=cos_e, sin=sin_s, dmat=dmat,
                qdec=jnp.broadcast_to(qdec[:, :, None], wide),
                kdec=jnp.broadcast_to(kdec[:, :, None], wide),
                lam=jnp.broadcast_to(lam[:, None, None], (RET_HEADS, RET_HEAD_DIM, RET_HEAD_DIM)))


def _swap_pairs(t):
    lane = lax.broadcasted_iota(jnp.int32, t.shape, 1)
    return jnp.where(lane % 2 == 0, pltpu.roll(t, RET_HEAD_DIM - 1, 1), pltpu.roll(t, 1, 1))


def _head(h):
    return slice(h * RET_HEAD_DIM, (h + 1) * RET_HEAD_DIM)


def _ret_common_specs(tb, blk):
    zs = [pl.BlockSpec((tb, RET_WIDTH), functools.partial(lambda j, i: (blk(i), j), j)) for j in range(4)]
    tabs = [pl.BlockSpec((tb, RET_HEAD_DIM), lambda i: (blk(i), 0))] * 2
    consts = [pl.BlockSpec((1, RET_WIDTH), lambda i: (0, 0)),
              pl.BlockSpec((RET_HEADS, CHUNK, CHUNK), lambda i: (0, 0, 0)),
              pl.BlockSpec((RET_HEADS, CHUNK, RET_HEAD_DIM), lambda i: (0, 0, 0)),
              pl.BlockSpec((RET_HEADS, CHUNK, RET_HEAD_DIM), lambda i: (0, 0, 0)),
              pl.BlockSpec((RET_HEADS, RET_HEAD_DIM, RET_HEAD_DIM), lambda i: (0, 0, 0))]
    return zs + tabs + consts


def _ret_fwd(z, tabs, gn_gain, name):
    s = z.shape[0]
    tb = _tile(s, TOKEN_TILE)
    ncb = tb // CHUNK
    scale = RET_HEAD_DIM ** -0.5

    def body(q_ref, k_ref, v_ref, g_ref, cos_ref, sin_ref, gain_ref, dm_ref, qd_ref, kd_ref, lam_ref,
             o_ref, st_ref, ret_ref, s_scr, qr_scr, kr_scr):
        @pl.when(pl.program_id(0) == 0)
        def _():
            s_scr[...] = jnp.zeros_like(s_scr)

        cosv, sinv = cos_ref[...], sin_ref[...]
        for h in range(RET_HEADS):
            qh, kh = q_ref[:, _head(h)], k_ref[:, _head(h)]
            qr_scr[:, _head(h)] = qh * cosv + _swap_pairs(qh) * sinv
            kr_scr[:, _head(h)] = (kh * cosv + _swap_pairs(kh) * sinv) * scale

        def chunk(c, carry):
            rows = pl.ds(pl.multiple_of(c * CHUNK, CHUNK), CHUNK)
            for h in range(RET_HEADS):
                qc, kc, vc = qr_scr[rows, _head(h)], kr_scr[rows, _head(h)], v_ref[rows, _head(h)]
                a = _dot(_bf(qc), _bf(kc), NT) * dm_ref[h]
                st = s_scr[h]
                st_ref[c, h] = st
                o_ref[rows, _head(h)] = _dot(_bf(a), _bf(vc)) + _dot(_bf(qc * qd_ref[h]), _bf(st))
                s_scr[h] = st * lam_ref[h] + _dot(_bf(kc * kd_ref[h]), _bf(vc), TN)
            return carry

        lax.fori_loop(0, ncb, chunk, 0)
        for h in range(RET_HEADS):
            o = o_ref[:, _head(h)]
            mu = jnp.mean(o, axis=-1, keepdims=True)
            oc = o - mu
            y = oc * lax.rsqrt(jnp.mean(oc * oc, axis=-1, keepdims=True) + GN_EPS) * gain_ref[:, _head(h)]
            g = g_ref[:, _head(h)]
            ret_ref[:, _head(h)] = _bf(g / (1.0 + jnp.exp(-g)) * y)

    nc = s // CHUNK
    return pl.pallas_call(
        body, name=name, grid=(s // tb,),
        out_shape=[SDS((s, RET_WIDTH), F32), SDS((nc, RET_HEADS, RET_HEAD_DIM, RET_HEAD_DIM), F32),
                   SDS((s, RET_WIDTH), BF16)],
        in_specs=_ret_common_specs(tb, lambda i: i),
        out_specs=[pl.BlockSpec((tb, RET_WIDTH), lambda i: (i, 0)),
                   pl.BlockSpec((ncb, RET_HEADS, RET_HEAD_DIM, RET_HEAD_DIM), lambda i: (i, 0, 0, 0)),
                   pl.BlockSpec((tb, RET_WIDTH), lambda i: (i, 0))],
        scratch_shapes=[pltpu.VMEM((RET_HEADS, RET_HEAD_DIM, RET_HEAD_DIM), F32),
                        pltpu.VMEM((tb, RET_WIDTH), F32), pltpu.VMEM((tb, RET_WIDTH), F32)],
        compiler_params=_params(("arbitrary",)),
    )(z, z, z, z, tabs["cos"], tabs["sin"], gn_gain, tabs["dmat"], tabs["qdec"], tabs["kdec"], tabs["lam"])


def _ret_bwd(z, tabs, gn_gain, o_pre, states, du, name):
    s = z.shape[0]
    tb = _tile(s, TOKEN_TILE)
    ncb = tb // CHUNK
    nblk = s // tb
    scale = RET_HEAD_DIM ** -0.5
    rev = lambda i: nblk - 1 - i

    def body(q_ref, k_ref, v_ref, g_ref, cos_ref, sin_ref, gain_ref, dm_ref, qd_ref, kd_ref, lam_ref,
             o_ref, st_ref, dret_ref, dz_ref, dgain_ref, g_scr, qr_scr, kr_scr, do_scr, dq_scr, dk_scr):
        @pl.when(pl.program_id(0) == 0)
        def _():
            g_scr[...] = jnp.zeros_like(g_scr)
            dgain_ref[...] = jnp.zeros_like(dgain_ref)

        cosv, sinv = cos_ref[...], sin_ref[...]
        for h in range(RET_HEADS):
            hs = _head(h)
            qh, kh = q_ref[:, hs], k_ref[:, hs]
            qr_scr[:, hs] = qh * cosv + _swap_pairs(qh) * sinv
            kr_scr[:, hs] = (kh * cosv + _swap_pairs(kh) * sinv) * scale
            o = o_ref[:, hs]
            mu = jnp.mean(o, axis=-1, keepdims=True)
            oc = o - mu
            rstd = lax.rsqrt(jnp.mean(oc * oc, axis=-1, keepdims=True) + GN_EPS)
            yh = oc * rstd
            gain = gain_ref[:, hs]
            g = g_ref[:, hs]
            sg = 1.0 / (1.0 + jnp.exp(-g))
            dret = dret_ref[:, hs].astype(F32)
            dy = dret * (g * sg)
            dz_ref[:, 3 * RET_WIDTH + h * RET_HEAD_DIM:3 * RET_WIDTH + (h + 1) * RET_HEAD_DIM] = _bf(
                dret * (yh * gain) * (sg * (1.0 + g * (1.0 - sg))))
            dgain_ref[:, hs] += jnp.sum(dy * yh, axis=0, keepdims=True)
            dyh = dy * gain
            do_scr[:, hs] = rstd * (dyh - jnp.mean(dyh, axis=-1, keepdims=True)
                                    - yh * jnp.mean(dyh * yh, axis=-1, keepdims=True))

        def chunk(cc, carry):
            c = ncb - 1 - cc
            rows = pl.ds(pl.multiple_of(c * CHUNK, CHUNK), CHUNK)
            for h in range(RET_HEADS):
                hs = _head(h)
                qc, kc, vc, doc = _bf(qr_scr[rows, hs]), _bf(kr_scr[rows, hs]), _bf(v_ref[rows, hs]), _bf(do_scr[rows, hs])
                qdc, kdc = qd_ref[h], kd_ref[h]
                st, gs = _bf(st_ref[c, h]), g_scr[h]
                gsb = _bf(gs)
                dm = dm_ref[h]
                p = _bf(_dot(qc, kc, NT) * dm)
                da = _bf(_dot(doc, vc, NT) * dm)
                kt = _bf(kr_scr[rows, hs] * kdc)
                qt = _bf(qr_scr[rows, hs] * qdc)
                dz_ref[rows, 2 * RET_WIDTH + h * RET_HEAD_DIM:2 * RET_WIDTH + (h + 1) * RET_HEAD_DIM] = _bf(
                    _dot(p, doc, TN) + _dot(kt, gsb))
                dq_scr[rows, hs] = _dot(da, kc) + _dot(doc, st, NT) * qdc
                dk_scr[rows, hs] = _dot(da, qc, TN) + _dot(vc, gsb, NT) * kdc
                g_scr[h] = gs * lam_ref[h] + _dot(qt, doc, TN)
            return carry

        lax.fori_loop(0, ncb, chunk, 0)
        for h in range(RET_HEADS):
            hs = _head(h)
            dq, dk = dq_scr[:, hs], dk_scr[:, hs]
            dz_ref[:, h * RET_HEAD_DIM:(h + 1) * RET_HEAD_DIM] = _bf(dq * cosv - _swap_pairs(dq) * sinv)
            dz_ref[:, RET_WIDTH + h * RET_HEAD_DIM:RET_WIDTH + (h + 1) * RET_HEAD_DIM] = _bf(
                (dk * cosv - _swap_pairs(dk) * sinv) * scale)

    return pl.pallas_call(
        body, name=name, grid=(nblk,),
        out_shape=[SDS((s, 4 * RET_WIDTH), BF16), SDS((1, RET_WIDTH), F32)],
        in_specs=_ret_common_specs(tb, rev)
        + [pl.BlockSpec((tb, RET_WIDTH), lambda i: (rev(i), 0)),
           pl.BlockSpec((ncb, RET_HEADS, RET_HEAD_DIM, RET_HEAD_DIM), lambda i: (rev(i), 0, 0, 0)),
           pl.BlockSpec((tb, RET_WIDTH), lambda i: (rev(i), 0))],
        out_specs=[pl.BlockSpec((tb, 4 * RET_WIDTH), lambda i: (rev(i), 0)),
                   pl.BlockSpec((1, RET_WIDTH), lambda i: (0, 0))],
        scratch_shapes=[pltpu.VMEM((RET_HEADS, RET_HEAD_DIM, RET_HEAD_DIM), F32)]
        + [pltpu.VMEM((tb, RET_WIDTH), F32)] * 5,
        compiler_params=_params(("arbitrary",)),
    )(z, z, z, z, tabs["cos"], tabs["sin"], gn_gain, tabs["dmat"], tabs["qdec"], tabs["kdec"], tabs["lam"],
      o_pre, states, du)


POOL_COL = 4 * RET_WIDTH // POOL_WIDTH


def _pooled(cur, prev, t0):
    tm = cur.shape[0]
    xx = jnp.concatenate([prev, cur], axis=0)
    sums = {1: xx}
    w = 1
    while w < POOL_WINDOWS[-1]:
        sums[2 * w] = sums[w] + pltpu.roll(sums[w], w, 0)
        w *= 2
    t = t0 + lax.broadcasted_iota(jnp.int32, (tm, 128), 0)
    outs = []
    for gi, w in enumerate(POOL_WINDOWS):
        cols = slice(gi * 128, (gi + 1) * 128)
        cnt = jnp.minimum(t + 1, w).astype(F32)
        outs.append(sums[w][POOL_HALO:, cols] / cnt - cur[:, cols])
    return outs


def _pool_fwd(z, w_pool, scale, name):
    s = z.shape[0]
    tm = _tile(s, TOKEN_TILE)
    hb = tm // POOL_HALO

    def body(p_ref, prev_ref, w_ref, sc_ref, o_ref):
        i = pl.program_id(0)
        prev = jnp.where(i > 0, prev_ref[...], 0.0)
        pooled = _pooled(p_ref[...], prev, i * tm)
        for gi in range(len(POOL_WINDOWS)):
            cols = slice(gi * 128, (gi + 1) * 128)
            o_ref[:, cols] = _bf(_dot(_bf(pooled[gi]), _bf(w_ref[gi])) * sc_ref[:, cols])

    return pl.pallas_call(
        body, name=name, grid=(s // tm,), out_shape=SDS((s, POOL_WIDTH), BF16),
        in_specs=[pl.BlockSpec((tm, POOL_WIDTH), lambda i: (i, POOL_COL)),
                  pl.BlockSpec((POOL_HALO, POOL_WIDTH), lambda i: (jnp.maximum(i * hb - 1, 0), POOL_COL)),
                  pl.BlockSpec(w_pool.shape, lambda i: (0, 0, 0)),
                  pl.BlockSpec((1, POOL_WIDTH), lambda i: (0, 0))],
        out_specs=pl.BlockSpec((tm, POOL_WIDTH), lambda i: (i, 0)),
        compiler_params=_params(("parallel",)),
    )(z, z, w_pool, scale)


def _pool_bwd(z, w_pool, scale, du, name):
    s = z.shape[0]
    tm = _tile(s, TOKEN_TILE)
    hb = tm // POOL_HALO
    nblk = s // tm
    last_halo = s // POOL_HALO - 1

    def body(p_ref, prev_ref, w_ref, sc_ref, do_ref, don_ref, dp_ref, dw_ref, dsc_ref):
        i = pl.program_id(0)

        @pl.when(i == 0)
        def _():
            dw_ref[...] = jnp.zeros_like(dw_ref)
            dsc_ref[...] = jnp.zeros_like(dsc_ref)

        prev = jnp.where(i > 0, prev_ref[...], 0.0)
        pooled = _pooled(p_ref[...], prev, i * tm)
        dout = do_ref[...].astype(F32)
        dout_next = jnp.where(i < nblk - 1, don_ref[...].astype(F32), 0.0)
        sc = sc_ref[...]
        dmix = jnp.concatenate([dout * sc, dout_next * sc], axis=0)
        n = tm + POOL_HALO
        t = i * tm + lax.broadcasted_iota(jnp.int32, (n, 128), 0)
        for gi, w in enumerate(POOL_WINDOWS):
            cols = slice(gi * 128, (gi + 1) * 128)
            wg = _bf(w_ref[gi])
            pg = _bf(pooled[gi])
            dsc_ref[:, cols] += jnp.sum(dout[:, cols] * _dot(pg, wg), axis=0, keepdims=True)
            dw_ref[gi] += _dot(pg, _bf(dmix[:tm, cols]), TN)
            dpool = _dot(_bf(dmix[:, cols]), wg, NT)
            acc = dpool / jnp.minimum(t + 1, w).astype(F32)
            step = 1
            while step < w:
                acc = acc + pltpu.roll(acc, n - step, 0)
                step *= 2
            dp_ref[:, cols] = _bf(acc[:tm] - dpool[:tm])

    return pl.pallas_call(
        body, name=name, grid=(nblk,),
        out_shape=[SDS((s, POOL_WIDTH), BF16), SDS(w_pool.shape, F32), SDS((1, POOL_WIDTH), F32)],
        in_specs=[pl.BlockSpec((tm, POOL_WIDTH), lambda i: (i, POOL_COL)),
                  pl.BlockSpec((POOL_HALO, POOL_WIDTH), lambda i: (jnp.maximum(i * hb - 1, 0), POOL_COL)),
                  pl.BlockSpec(w_pool.shape, lambda i: (0, 0, 0)),
                  pl.BlockSpec((1, POOL_WIDTH), lambda i: (0, 0)),
                  pl.BlockSpec((tm, POOL_WIDTH), lambda i: (i, 1)),
                  pl.BlockSpec((POOL_HALO, POOL_WIDTH), lambda i: (jnp.minimum((i + 1) * hb, last_halo), 1))],
        out_specs=[pl.BlockSpec((tm, POOL_WIDTH), lambda i: (i, 0)),
                   pl.BlockSpec(w_pool.shape, lambda i: (0, 0, 0)),
                   pl.BlockSpec((1, POOL_WIDTH), lambda i: (0, 0))],
        compiler_params=_params(("arbitrary",)),
    )(z, z, w_pool, scale, du, du)


def _rel_onehot():
    r = lax.broadcasted_iota(jnp.int32, (REL_PAD, ATT_DIAG), 0)
    c = lax.broadcasted_iota(jnp.int32, (REL_PAD, ATT_DIAG), 1)
    rel = jnp.where(c < ATT_K_TILE, jnp.clip(LEFT_CHUNKS * CHUNK - c, -REL_CLIP, REL_CLIP) + REL_CLIP,
                    2 * REL_CLIP)
    return (rel == r).astype(BF16)


def _split3(v):
    hi = _bf(v)
    r1 = v - hi.astype(F32)
    mid = _bf(r1)
    return hi, mid, _bf(r1 - mid.astype(F32))


def _skew(v, sign):
    row = lax.broadcasted_iota(jnp.int32, v.shape, 0)
    bit = 1
    while bit < ATT_Q_TILE:
        shift = bit if sign > 0 else ATT_DIAG - bit
        v = jnp.where((row & bit) != 0, pltpu.roll(v, shift, 1), v)
        bit *= 2
    return v


def _attn_bias(rel_bias, name):
    def body(t_ref, o_ref):
        oh = _rel_onehot()
        base = sum(_dot(part, oh) for part in _split3(t_ref[0]))
        full = _skew(jnp.broadcast_to(base[0:1], (ATT_Q_TILE, ATT_DIAG)), +1)[:, :ATT_K_TILE]
        qc = lax.broadcasted_iota(jnp.int32, full.shape, 0) // CHUNK
        kc = lax.broadcasted_iota(jnp.int32, full.shape, 1) // CHUNK
        o_ref[0] = jnp.where((kc >= qc) & (kc <= qc + LEFT_CHUNKS), full, NEG_INF)

    t8 = jnp.broadcast_to(rel_bias[:, None, :], (ATT_HEADS, 8, REL_PAD))
    return pl.pallas_call(
        body, name=name, grid=(ATT_HEADS,), out_shape=SDS((ATT_HEADS, ATT_Q_TILE, ATT_K_TILE), F32),
        in_specs=[pl.BlockSpec((1, 8, REL_PAD), lambda h: (h, 0, 0))],
        out_specs=pl.BlockSpec((1, ATT_Q_TILE, ATT_K_TILE), lambda h: (h, 0, 0)),
        compiler_params=_params(("parallel",)),
    )(t8)


def _attn_dbias(dbias, name):
    def body(d_ref, o_ref):
        pad = jnp.zeros((ATT_Q_TILE, ATT_DIAG - ATT_K_TILE), F32)
        diag = _skew(jnp.concatenate([d_ref[0], pad], axis=1), -1)
        col = jnp.sum(diag, axis=0, keepdims=True)
        oh = _rel_onehot()
        col8 = jnp.broadcast_to(col, (8, ATT_DIAG))
        o_ref[0] = sum(_dot(part, oh, NT) for part in _split3(col8))

    out = pl.pallas_call(
        body, name=name, grid=(ATT_HEADS,), out_shape=SDS((ATT_HEADS, 8, REL_PAD), F32),
        in_specs=[pl.BlockSpec((1, ATT_Q_TILE, ATT_K_TILE), lambda h: (h, 0, 0))],
        out_specs=pl.BlockSpec((1, 8, REL_PAD), lambda h: (h, 0, 0)),
        compiler_params=_params(("parallel",)),
    )(dbias)
    return out[:, 0, :]


def _attn_specs():
    hp = ATT_HEADS // 2
    kv = lambda off, back: pl.BlockSpec((ATT_Q_TILE, 128), lambda p, i: (jnp.maximum(i - back, 0), off + p))
    return ([pl.BlockSpec((ATT_Q_TILE, 128), lambda p, i: (i, p))]
            + [kv(hp, 2), kv(hp, 1), kv(hp, 0)] + [kv(2 * hp, 2), kv(2 * hp, 1), kv(2 * hp, 0)]
            + [pl.BlockSpec((2, ATT_Q_TILE, ATT_K_TILE), lambda p, i: (p, 0, 0))])


def _attn_probs(q, k, bias, i):
    sc = _dot(q, k, NT) + bias
    kpos = (i - 2) * ATT_Q_TILE + lax.broadcasted_iota(jnp.int32, sc.shape, 1)
    sc = jnp.where(kpos >= 0, sc, NEG_INF)
    e = jnp.exp(sc - jnp.max(sc, axis=-1, keepdims=True))
    return e / jnp.sum(e, axis=-1, keepdims=True)


def _attn_fwd(z, bias, name):
    s = z.shape[0]

    def body(q_ref, k0, k1, k2, v0, v1, v2, b_ref, o_ref):
        i = pl.program_id(1)
        for hh in range(2):
            cs = slice(hh * ATT_HEAD_DIM, (hh + 1) * ATT_HEAD_DIM)
            q = _bf(q_ref[:, cs].astype(F32) * ATT_HEAD_DIM ** -0.5)
            k = jnp.concatenate([k0[:, cs], k1[:, cs], k2[:, cs]], axis=0)
            v = jnp.concatenate([v0[:, cs], v1[:, cs], v2[:, cs]], axis=0)
            p = _attn_probs(q, k, b_ref[hh], i)
            o_ref[:, cs] = _bf(_dot(_bf(p), v))

    return pl.pallas_call(
        body, name=name, grid=(ATT_HEADS // 2, s // ATT_Q_TILE), out_shape=SDS((s, D_MODEL), BF16),
        in_specs=_attn_specs(),
        out_specs=pl.BlockSpec((ATT_Q_TILE, 128), lambda p, i: (i, p)),
        compiler_params=_params(("parallel", "parallel")),
    )(z, z, z, z, z, z, z, bias)


def _attn_bwd(z, bias, o, do, name):
    s = z.shape[0]

    def body(q_ref, k0, k1, k2, v0, v1, v2, b_ref, o_ref, do_ref, dq_ref, dk_ref, dv_ref, db_ref):
        i = pl.program_id(1)

        @pl.when(i == 0)
        def _():
            db_ref[...] = jnp.zeros_like(db_ref)

        for hh in range(2):
            cs = slice(hh * ATT_HEAD_DIM, (hh + 1) * ATT_HEAD_DIM)
            q = _bf(q_ref[:, cs].astype(F32) * ATT_HEAD_DIM ** -0.5)
            k = jnp.concatenate([k0[:, cs], k1[:, cs], k2[:, cs]], axis=0)
            v = jnp.concatenate([v0[:, cs], v1[:, cs], v2[:, cs]], axis=0)
            p = _attn_probs(q, k, b_ref[hh], i)
            doh = do_ref[:, cs]
            delta = jnp.sum(doh.astype(F32) * o_ref[:, cs].astype(F32), axis=-1, keepdims=True)
            ds = p * (_dot(doh, v, NT) - delta)
            db_ref[hh] += ds
            dsb = _bf(ds)
            dq_ref[:, cs] = _bf(_dot(dsb, k) * ATT_HEAD_DIM ** -0.5)
            dk = _dot(dsb, q, TN)
            dv = _dot(_bf(p), doh, TN)
            for t in range(3):
                rows = slice(t * ATT_Q_TILE, (t + 1) * ATT_Q_TILE)
                dk_ref[t, :, cs] = _bf(dk[rows])
                dv_ref[t, :, cs] = _bf(dv[rows])

    tile = pl.BlockSpec((ATT_Q_TILE, 128), lambda p, i: (i, p))
    part = pl.BlockSpec((3, ATT_Q_TILE, 128), lambda p, i: (0, i, p))
    return pl.pallas_call(
        body, name=name, grid=(ATT_HEADS // 2, s // ATT_Q_TILE),
        out_shape=[SDS((s, D_MODEL), BF16), SDS((3, s, D_MODEL), BF16), SDS((3, s, D_MODEL), BF16),
                   SDS((ATT_HEADS, ATT_Q_TILE, ATT_K_TILE), F32)],
        in_specs=_attn_specs() + [tile, tile],
        out_specs=[tile, part, part, pl.BlockSpec((2, ATT_Q_TILE, ATT_K_TILE), lambda p, i: (p, 0, 0))],
        compiler_params=_params(("parallel", "arbitrary")),
    )(z, z, z, z, z, z, z, bias, o, do)


def _attn_combine(dq, dkp, dvp, name):
    s = dq.shape[0]
    nq = s // ATT_Q_TILE

    def body(dq_ref, k2, k1, k0, v2, v1, v0, o_ref):
        i = pl.program_id(0)

        def total(a, b, c):
            t = a[0].astype(F32)
            t = t + jnp.where(i + 1 < nq, b[0].astype(F32), 0.0)
            return _bf(t + jnp.where(i + 2 < nq, c[0].astype(F32), 0.0))

        o_ref[:, 0:D_MODEL] = dq_ref[...]
        o_ref[:, D_MODEL:2 * D_MODEL] = total(k2, k1, k0)
        o_ref[:, 2 * D_MODEL:3 * D_MODEL] = total(v2, v1, v0)

    slot = lambda t: pl.BlockSpec((1, ATT_Q_TILE, D_MODEL), lambda i: (t, jnp.minimum(i + 2 - t, nq - 1), 0))
    return pl.pallas_call(
        body, name=name, grid=(nq,), out_shape=SDS((s, 3 * D_MODEL), BF16),
        in_specs=[pl.BlockSpec((ATT_Q_TILE, D_MODEL), lambda i: (i, 0)), slot(2), slot(1), slot(0),
                  slot(2), slot(1), slot(0)],
        out_specs=pl.BlockSpec((ATT_Q_TILE, 3 * D_MODEL), lambda i: (i, 0)),
        compiler_params=_params(("parallel",)),
    )(dq, dkp, dkp, dkp, dvp, dvp, dvp)


def _local_step(x, target, wts, small):
    s = x.shape[0]
    tabs = _retention_tables(s)
    saved = []
    for layer in range(DEPTH):
        i = layer // 2
        sv = {"x0": x}
        g_mix = small["mix_norm"][layer:layer + 1]
        if layer % 2 == 0:
            sv["h1"], sv["z"] = _norm_mm(x, g_mix, wts["ab_w_in"][i], AB_IN_WIDTH, F32, False, "ab_in_fwd")
            gn = small["ab_gn_gain"][i:i + 1]
            sv["o_pre"], sv["states"], ret = _ret_fwd(sv["z"], tabs, gn, "ret_fwd")
            pool = _pool_fwd(sv["z"], small["ab_w_pool"][i], small["ab_pool_scale"][i:i + 1], "pool_fwd")
            sv["u"] = (ret, pool)
            x = _mm_res([ret, pool], wts["ab_w_out"][i], x, "ab_out_fwd")
        else:
            sv["h1"], sv["z"] = _norm_mm(x, g_mix, wts["c_w_qkv"][i], 3 * D_MODEL // N_DEV, BF16, False, "qkv_fwd")
            rb = jnp.pad(small["c_rel_bias"][i], ((0, 0), (0, REL_PAD - N_REL)))
            sv["bias"] = _attn_bias(rb, "attn_bias")
            sv["o"] = _attn_fwd(sv["z"], sv["bias"], "attn_fwd")
            x = _mm_res([sv["o"]], wts["c_w_out"][i], x, "c_out_fwd")
        sv["x1"] = x
        sv["h2"], sv["z1"], sv["a"] = _norm_mm(x, small["ffn_norm"][layer:layer + 1], wts["w_ffn_in"][layer],
                                               D_FF // N_DEV, BF16, True, "ffn_in_fwd")
        x = _mm_res([sv["a"]], wts["w_ffn_out"][layer], x, "ffn_out_fwd")
        saved.append(sv)

    loss, dx, d_final = _final_loss(x, small["final_norm"][None, :], target, "final_loss")

    gw = {k: [None] * len(v) for k, v in wts.items()}
    gs = {k: [None] * DEPTH for k in ("mix_norm", "ffn_norm")}
    for k in ("ab_gn_gain", "ab_w_pool", "ab_pool_scale", "c_rel_bias"):
        gs[k] = [None] * (DEPTH // 2)
    gs["final_norm"] = d_final[0]
    for layer in reversed(range(DEPTH)):
        i = layer // 2
        sv = saved[layer]
        w2 = wts["w_ffn_out"][layer]
        dz1 = _mm_nt_rows(dx, w2, sv["z1"], 1024, "ffn_out_bwd")
        gw["w_ffn_out"][layer] = _mm_tn(sv["a"], dx, D_FF // N_DEV, D_MODEL, N_DEV, True, "ffn_out_dw")
        gw["w_ffn_in"][layer] = _mm_tn(sv["h2"], dz1, D_MODEL, D_FF // N_DEV, N_DEV, False, "ffn_in_dw")
        dx, dg = _mm_nt_normbwd(dz1, wts["w_ffn_in"][layer], sv["x1"], small["ffn_norm"][layer:layer + 1], dx,
                                "ffn_in_bwd")
        gs["ffn_norm"][layer] = dg[0]
        g_mix = small["mix_norm"][layer:layer + 1]
        if layer % 2 == 0:
            w_out = wts["ab_w_out"][i]
            du = _mm_nt_rows(dx, w_out, None, D_MODEL, "mix_out_bwd")
            ret, pool = sv["u"]
            u = jnp.concatenate([ret, pool], axis=1)
            gw["ab_w_out"][i] = _mm_tn(u, dx, D_MODEL // N_DEV, D_MODEL, N_DEV, True, "mix_out_dw")
            gn = small["ab_gn_gain"][i:i + 1]
            dz_ret, dgn = _ret_bwd(sv["z"], tabs, gn, sv["o_pre"], sv["states"], du, "ret_bwd")
            dp, dwp, dsc = _pool_bwd(sv["z"], small["ab_w_pool"][i], small["ab_pool_scale"][i:i + 1], du, "pool_bwd")
            gs["ab_gn_gain"][i], gs["ab_w_pool"][i], gs["ab_pool_scale"][i] = dgn[0], dwp, dsc[0]
            dz = jnp.concatenate([dz_ret, dp], axis=1)
            gw["ab_w_in"][i] = _mm_tn(sv["h1"], dz, D_MODEL, 512, AB_IN_WIDTH // 512, False, "ab_in_dw")
            dx, dg = _mm_nt_normbwd(dz, wts["ab_w_in"][i], sv["x0"], g_mix, dx, "ab_in_bwd")
        else:
            w_out = wts["c_w_out"][i]
            do = _mm_nt_rows(dx, w_out, None, D_MODEL, "mix_out_bwd")
            gw["c_w_out"][i] = _mm_tn(sv["o"], dx, D_MODEL // N_DEV, D_MODEL, N_DEV, True, "mix_out_dw")
            dq, dkp, dvp, dbias = _attn_bwd(sv["z"], sv["bias"], sv["o"], do, "attn_bwd")
            gs["c_rel_bias"][i] = _attn_dbias(dbias, "attn_dbias")[:, :N_REL]
            dz = _attn_combine(dq, dkp, dvp, "attn_combine")
            gw["c_w_qkv"][i] = _mm_tn(sv["h1"], dz, D_MODEL, 3 * D_MODEL // N_DEV, N_DEV, False, "qkv_dw")
            dx, dg = _mm_nt_normbwd(dz, wts["c_w_qkv"][i], sv["x0"], g_mix, dx, "qkv_bwd")
        gs["mix_norm"][layer] = dg[0]
    gsmall = {k: (jnp.stack(v) if isinstance(v, list) else v) for k, v in gs.items()}
    return loss, dx, gw, gsmall


BIG = ("w_ffn_in", "w_ffn_out", "ab_w_in", "ab_w_out", "c_w_qkv", "c_w_out")
SMALL = ("mix_norm", "ffn_norm", "ab_gn_gain", "ab_w_pool", "ab_pool_scale", "c_rel_bias", "final_norm")
ANY = pl.BlockSpec(memory_space=pl.ANY)


def _position():
    x, y, c = (lax.axis_index(a) for a in MESH_AXES)
    return x, y, c


def _cast_bf16(w, name):
    nl, r, c = w.shape
    tr = _tile(r, 512)
    def body(i_ref, o_ref):
        o_ref[...] = _bf(i_ref[...])

    return pl.pallas_call(
        body, name=name, grid=(nl, r // tr), out_shape=SDS(w.shape, BF16),
        in_specs=[pl.BlockSpec((None, tr, c), lambda l, i: (l, i, 0))],
        out_specs=pl.BlockSpec((None, tr, c), lambda l, i: (l, i, 0)),
        compiler_params=_params(("parallel", "parallel")),
    )(w)


def _all_gather(shards):
    items = [(a, l) for a, sh in enumerate(shards) for l in range(sh.shape[0])]
    n_in, n = len(shards), len(items)

    def body(*refs):
        ins, outs = refs[:n_in], refs[n_in:n_in + n]
        send_sems, recv_sems, local_sems = refs[n_in + n:]
        x, y, c = _position()
        me, sibling = (x, y, c), (x, y, 1 - c)
        chips = [(1 - x, y), (x, 1 - y), (1 - x, 1 - y)]

        def slot(out, pos):
            return out.at[4 * pos[0] + 2 * pos[1] + pos[2]]

        def copy(k, rel, block, to, src=None):
            out = outs[k]
            return pltpu.make_async_remote_copy(
                src_ref=slot(out, block) if src is None else src, dst_ref=slot(out, block),
                send_sem=send_sems.at[k, rel], recv_sem=recv_sems.at[k, rel],
                device_id=to, device_id_type=pl.DeviceIdType.MESH)

        started, local = [], []
        for k, (a, l) in enumerate(items):
            src = ins[a].at[l]
            mine = pltpu.make_async_copy(src, slot(outs[k], me), local_sems.at[k])
            mine.start()
            local.append(mine)
            first = [copy(k, 0, me, sibling, src=src)]
            first += [copy(k, 1 + j, me, (*chip, c), src=src) for j, chip in enumerate(chips)]
            for cp in first:
                cp.start()
            started += first
        for k in range(n):
            for j, chip in enumerate(chips):
                copy(k, 1 + j, (*chip, c), me).wait_recv()
                fwd = copy(k, 4 + j, (*chip, c), sibling)
                fwd.start()
                started.append(fwd)
        for k in range(n):
            copy(k, 0, sibling, me).wait_recv()
            for j, chip in enumerate(chips):
                copy(k, 4 + j, (*chip, 1 - c), me).wait_recv()
        for cp in started:
            cp.wait_send()
        for cp in local:
            cp.wait()

    out_shape = [SDS((N_DEV,) + shards[a].shape[1:], BF16) for a, _ in items]
    outs = pl.pallas_call(
        body, name="all_gather_weights", out_shape=out_shape,
        in_specs=[ANY] * n_in, out_specs=[ANY] * n,
        scratch_shapes=[pltpu.SemaphoreType.DMA((n, 7)), pltpu.SemaphoreType.DMA((n, 7)),
                        pltpu.SemaphoreType.DMA((n,))],
    )(*shards)
    res, k = [], 0
    for sh in shards:
        res.append(outs[k:k + sh.shape[0]])
        k += sh.shape[0]
    return res


def _exchange(grads, small):
    items = [(a, l) for a, g in enumerate(grads) for l in range(len(g))]
    n = len(items)
    flat = [grads[a][l] for a, l in items]
    flips = [(fx, fy, fc) for fx in (0, 1) for fy in (0, 1) for fc in (0, 1)][1:]

    def body(*refs):
        ins, small_in = refs[:n], refs[n]
        outs, small_out = refs[n + 1:n + 1 + len(grads)], refs[n + 1 + len(grads)]
        send_sems, recv_sems, local_sems = refs[n + 2 + len(grads):]
        x, y, c = _position()
        me = 4 * x + 2 * y + c
        copies = []
        for k in range(n + 1):
            if k < n:
                a, l = items[k]
                own_src, own_dst = ins[k].at[me], outs[a].at[me, l]
            else:
                own_src, own_dst = small_in, small_out.at[me]
            own = pltpu.make_async_copy(own_src, own_dst, local_sems.at[k])
            own.start()
            copies.append(own)
            for rel, (fx, fy, fc) in enumerate(flips):
                px, py, pc = (1 - x if fx else x), (1 - y if fy else y), (1 - c if fc else c)
                peer = 4 * px + 2 * py + pc
                if k < n:
                    src, dst, land = ins[k].at[peer], outs[a].at[me, l], outs[a].at[peer, l]
                else:
                    src, dst, land = small_in, small_out.at[me], small_out.at[peer]
                cp = pltpu.make_async_remote_copy(
                    src_ref=src, dst_ref=dst, send_sem=send_sems.at[k, rel], recv_sem=recv_sems.at[k, rel],
                    device_id=(px, py, pc), device_id_type=pl.DeviceIdType.MESH)
                cp.start()
                arrival = pltpu.make_async_remote_copy(
                    src_ref=src, dst_ref=land, send_sem=send_sems.at[k, rel], recv_sem=recv_sems.at[k, rel],
                    device_id=(px, py, pc), device_id_type=pl.DeviceIdType.MESH)
                copies.append((cp, arrival))
        for cp in copies:
            if isinstance(cp, tuple):
                cp[0].wait_send()
                cp[1].wait_recv()
            else:
                cp.wait()

    out_shape = [SDS((N_DEV, len(g)) + g[0].shape[1:], BF16) for g in grads] + [SDS((N_DEV,) + small.shape, F32)]
    outs = pl.pallas_call(
        body, name="exchange_grads", out_shape=out_shape,
        in_specs=[ANY] * (n + 1), out_specs=[ANY] * (len(grads) + 1),
        scratch_shapes=[pltpu.SemaphoreType.DMA((n + 1, 7)), pltpu.SemaphoreType.DMA((n + 1, 7)),
                        pltpu.SemaphoreType.DMA((n + 1,))],
    )(*flat, small)
    return outs[:-1], outs[-1]


def _adamw_math(g, w, m, v):
    m2 = ADAM_B1 * m + (1.0 - ADAM_B1) * g
    v2 = ADAM_B2 * v + (1.0 - ADAM_B2) * jnp.square(g)
    m_hat = m2 / (1.0 - ADAM_B1 ** ADAM_STEP)
    v_hat = v2 / (1.0 - ADAM_B2 ** ADAM_STEP)
    delta = -ADAM_LR * (m_hat / (jnp.sqrt(v_hat) + ADAM_EPS) + ADAM_WD * w)
    return delta, m2, v2


def _adamw(recv, w, m, v, name):
    nl, r, c = w.shape
    tr = _tile(r, 256)

    def body(g_ref, w_ref, m_ref, v_ref, go_ref, d_ref, mo_ref, vo_ref):
        g = g_ref[0].astype(F32)
        for p in range(1, N_DEV):
            g = g + g_ref[p].astype(F32)
        go_ref[...] = g
        d_ref[...], mo_ref[...], vo_ref[...] = _adamw_math(g, w_ref[...], m_ref[...], v_ref[...])

    blk = pl.BlockSpec((None, tr, c), lambda l, i: (l, i, 0))
    return pl.pallas_call(
        body, name=name, grid=(nl, r // tr), out_shape=[SDS(w.shape, F32)] * 4,
        in_specs=[pl.BlockSpec((N_DEV, None, tr, c), lambda l, i: (0, l, i, 0)), blk, blk, blk],
        out_specs=[blk] * 4,
        compiler_params=_params(("parallel", "parallel")),
    )(recv, w, m, v)


def _adamw_small(recv, w, m, v, name):
    def body(g_ref, w_ref, m_ref, v_ref, go_ref, d_ref, mo_ref, vo_ref):
        g = g_ref[0]
        for p in range(1, N_DEV):
            g = g + g_ref[p]
        go_ref[...] = g
        d_ref[...], mo_ref[...], vo_ref[...] = _adamw_math(g, w_ref[...], m_ref[...], v_ref[...])

    return pl.pallas_call(body, name=name, out_shape=[SDS(w.shape, F32)] * 4,
                          compiler_params=_params(None))(recv, w, m, v)


def _pack_small(tree):
    parts = []
    for k in SMALL:
        flat = tree[k].reshape(-1)
        rows = -(-flat.shape[0] // 1024) * 8
        parts.append(jnp.pad(flat, (0, rows * 128 - flat.shape[0])).reshape(rows, 128))
    return jnp.concatenate(parts, axis=0)


def _unpack_small(packed, like):
    out, row = {}, 0
    for k in SMALL:
        size = like[k].size
        rows = -(-size // 1024) * 8
        out[k] = packed[row:row + rows].reshape(-1)[:size].reshape(like[k].shape)
        row += rows
    return out


def _to_shard_major(name, g):
    if name == "ab_w_in":
        nj, ka, nb = g.shape
        full = jnp.transpose(g, (1, 0, 2)).reshape(ka, N_DEV, nj * nb // N_DEV)
        return jnp.transpose(full, (1, 0, 2))
    return g


def _from_gathered(name, g):
    if name in ("w_ffn_out", "ab_w_out", "c_w_out"):
        return g.reshape(g.shape[0] * g.shape[1], g.shape[2])
    if name == "ab_w_in":
        return jnp.transpose(g, (1, 0, 2)).reshape(1, g.shape[1], N_DEV * g.shape[2])
    return g


def kernel(x, mix_norm, ffn_norm, w_ffn_in, w_ffn_out, ab_w_in, ab_gn_gain, ab_w_pool, ab_pool_scale, ab_w_out, c_w_qkv, c_rel_bias, c_w_out, final_norm, loss_target, m_mix_norm, m_ffn_norm, m_w_ffn_in, m_w_ffn_out, m_ab_w_in, m_ab_gn_gain, m_ab_w_pool, m_ab_pool_scale, m_ab_w_out, m_c_w_qkv, m_c_rel_bias, m_c_w_out, m_final_norm, v_mix_norm, v_ffn_norm, v_w_ffn_in, v_w_ffn_out, v_ab_w_in, v_ab_gn_gain, v_ab_w_pool, v_ab_pool_scale, v_ab_w_out, v_c_w_qkv, v_c_rel_bias, v_c_w_out, v_final_norm):
    args = dict(locals())
    weights = {k: args[k] for k in BIG + SMALL}
    moments_m = {k: args["m_" + k] for k in BIG + SMALL}
    moments_v = {k: args["v_" + k] for k in BIG + SMALL}

    shards = [_cast_bf16(weights[k], "cast_" + k) for k in BIG]
    gathered = _all_gather(shards)
    wts = {k: [_from_gathered(k, g) for g in gl] for k, gl in zip(BIG, gathered)}
    small = {k: weights[k] for k in SMALL}

    loss, dx, gw, gsmall = _local_step(x[0], loss_target[0], wts, small)

    grads = [[_to_shard_major(k, g) for g in gw[k]] for k in BIG]
    recv, recv_small = _exchange(grads, _pack_small(gsmall))

    outs = {}
    for k, r in zip(BIG, recv):
        outs[k] = _adamw(r, weights[k], moments_m[k], moments_v[k], "adamw_" + k)
    packed = _adamw_small(recv_small, _pack_small(small), _pack_small(moments_m), _pack_small(moments_v),
                          "adamw_small")
    unpacked = [_unpack_small(p, small) for p in packed]
    for k in SMALL:
        outs[k] = [u[k] for u in unpacked]

    total = lax.psum(loss[0, 0], MESH_AXES)
    order = SMALL[:2] + BIG[:2] + ("ab_w_in", "ab_gn_gain", "ab_w_pool", "ab_pool_scale", "ab_w_out",
                                   "c_w_qkv", "c_rel_bias", "c_w_out", "final_norm")
    result = [total, dx[None]]
    for part in range(4):
        result += [outs[k][part] for k in order]
    return tuple(result)
```

```python
import functools

import jax
import jax.numpy as jnp
from jax import lax
from jax.experimental import pallas as pl
from jax.experimental.pallas import tpu as pltpu

F32 = jnp.float32
BF16 = jnp.bfloat16
SDS = jax.ShapeDtypeStruct
MESH_AXES = ("x", "y", "c")
N_DEV = 8

D_MODEL = 1024
DEPTH = 4
CHUNK = 64
D_FF = 4 * D_MODEL
RMS_EPS = 1e-6
RET_WIDTH = 512
RET_HEADS = 4
RET_HEAD_DIM = 128
RET_ROPE_BASE = 10000.0
GN_EPS = 1e-5
POOL_WIDTH = 512
POOL_WINDOWS = (2, 4, 8, 16)
POOL_HALO = 16
AB_IN_WIDTH = 4 * RET_WIDTH + POOL_WIDTH
ATT_HEADS = 16
ATT_HEAD_DIM = 64
LEFT_CHUNKS = 8
REL_CLIP = 128
N_REL = 2 * REL_CLIP + 1
NEG_INF = -1e30

ADAM_LR = 0.001
ADAM_B1 = 0.9
ADAM_B2 = 0.999
ADAM_EPS = 1e-08
ADAM_WD = 0.01
ADAM_STEP = 10

TOKEN_TILE = 512
ATT_Q_TILE = 256
ATT_K_TILE = 3 * ATT_Q_TILE
ATT_DIAG = 1024
REL_PAD = 384
VMEM_LIMIT_MB = 56

NT = (((1,), (1,)), ((), ()))
TN = (((0,), (0,)), ((), ()))


def _params(semantics, **kw):
    return pltpu.CompilerParams(dimension_semantics=semantics,
                                vmem_limit_bytes=VMEM_LIMIT_MB * 2 ** 20, **kw)


def _dot(a, b, dims=None):
    if dims is None:
        return jnp.dot(a, b, preferred_element_type=F32)
    return lax.dot_general(a, b, dims, preferred_element_type=F32)


def _bf(v):
    return v.astype(BF16)


def _tile(n, t):
    return min(n, t)


def _norm_mm(x, gain, w, tn, z_dtype, relu2, name):
    s, d = x.shape
    nj = w.shape[0]
    tm = _tile(s, TOKEN_TILE)

    def body(x_ref, g_ref, w_ref, h_ref, z_ref, *a_ref):
        xv = x_ref[...]
        r = lax.rsqrt(jnp.mean(xv * xv, axis=-1, keepdims=True) + RMS_EPS)
        h = _bf(xv * r * g_ref[...])
        h_ref[...] = h
        cw = tn if tn <= 512 else 512
        for j in range(nj):
            for c in range(0, tn, cw):
                z = _dot(h, w_ref[j, :, c:c + cw])
                cols = slice(j * tn + c, j * tn + c + cw)
                z_ref[:, cols] = z.astype(z_ref.dtype)
                if relu2:
                    a_ref[0][:, cols] = _bf(jnp.square(jnp.maximum(z, 0.0)))

    n = nj * tn
    out_shape = [SDS((s, d), BF16), SDS((s, n), z_dtype)]
    out_specs = [pl.BlockSpec((tm, d), lambda i: (i, 0)), pl.BlockSpec((tm, n), lambda i: (i, 0))]
    if relu2:
        out_shape.append(SDS((s, n), BF16))
        out_specs.append(pl.BlockSpec((tm, n), lambda i: (i, 0)))
    return pl.pallas_call(
        body, name=name, grid=(s // tm,), out_shape=out_shape,
        in_specs=[pl.BlockSpec((tm, d), lambda i: (i, 0)),
                  pl.BlockSpec((1, d), lambda i: (0, 0)),
                  pl.BlockSpec((nj, d, tn), lambda i: (0, 0, 0))],
        out_specs=out_specs,
        compiler_params=_params(("parallel",)),
    )(x, gain, w)


def _mm_res(parts, w, res, name):
    s, d = res.shape
    tm = _tile(s, TOKEN_TILE)
    widths = [p.shape[1] for p in parts]

    def body(*refs):
        a_refs = refs[:len(parts)]
        w_ref, res_ref, o_ref = refs[len(parts):]
        acc = res_ref[...]
        off = 0
        for a_ref, k in zip(a_refs, widths):
            acc = acc + _dot(a_ref[...], w_ref[off:off + k, :])
            off += k
        o_ref[...] = acc

    return pl.pallas_call(
        body, name=name, grid=(s // tm,), out_shape=SDS((s, d), F32),
        in_specs=[pl.BlockSpec((tm, k), lambda i: (i, 0)) for k in widths]
        + [pl.BlockSpec(w.shape, lambda i: (0, 0)), pl.BlockSpec((tm, d), lambda i: (i, 0))],
        out_specs=pl.BlockSpec((tm, d), lambda i: (i, 0)),
        compiler_params=_params(("parallel",)),
    )(*parts, w, res)


def _mm_nt_rows(dy, w, z, name):
    s, d = dy.shape
    k = w.shape[0]
    tm = _tile(s, TOKEN_TILE)
    tk = _tile(k, 1024)

    def body(dy_ref, w_ref, *rest):
        o_ref = rest[-1]
        dyb = _bf(dy_ref[...])
        for j in range(k // tk):
            cols = slice(j * tk, (j + 1) * tk)
            da = _dot(dyb, w_ref[cols, :], NT)
            if z is not None:
                da = da * (2.0 * jnp.maximum(rest[0][:, cols].astype(F32), 0.0))
            o_ref[:, cols] = _bf(da)

    in_specs = [pl.BlockSpec((tm, d), lambda i: (i, 0)), pl.BlockSpec((k, d), lambda i: (0, 0))]
    args = [dy, w]
    if z is not None:
        in_specs.append(pl.BlockSpec((tm, k), lambda i: (i, 0)))
        args.append(z)
    return pl.pallas_call(
        body, name=name, grid=(s // tm,), out_shape=SDS((s, k), BF16),
        in_specs=in_specs, out_specs=pl.BlockSpec((tm, k), lambda i: (i, 0)),
        compiler_params=_params(("parallel",)),
    )(*args)


def _mm_nt_normbwd(dz, w, x, gain, dres, name):
    s, d = x.shape
    nj, _, nc = w.shape
    tm = _tile(s, TOKEN_TILE)

    def body(dz_ref, w_ref, x_ref, g_ref, dres_ref, dx_ref, dg_ref):
        dh = _dot(dz_ref[:, 0:nc], w_ref[0], NT)
        for j in range(1, nj):
            dh = dh + _dot(dz_ref[:, j * nc:(j + 1) * nc], w_ref[j], NT)
        xv = x_ref[...]
        r = lax.rsqrt(jnp.mean(xv * xv, axis=-1, keepdims=True) + RMS_EPS)
        xn = xv * r

        @pl.when(pl.program_id(0) == 0)
        def _():
            dg_ref[...] = jnp.zeros_like(dg_ref)

        dg_ref[...] += jnp.sum(dh * xn, axis=0, keepdims=True)
        dxh = dh * g_ref[...]
        dx_ref[...] = dres_ref[...] + r * (dxh - xn * jnp.mean(dxh * xn, axis=-1, keepdims=True))

    return pl.pallas_call(
        body, name=name, grid=(s // tm,), out_shape=[SDS((s, d), F32), SDS((1, d), F32)],
        in_specs=[pl.BlockSpec((tm, nj * nc), lambda i: (i, 0)),
                  pl.BlockSpec((nj, d, nc), lambda i: (0, 0, 0)),
                  pl.BlockSpec((tm, d), lambda i: (i, 0)),
                  pl.BlockSpec((1, d), lambda i: (0, 0)),
                  pl.BlockSpec((tm, d), lambda i: (i, 0))],
        out_specs=[pl.BlockSpec((tm, d), lambda i: (i, 0)), pl.BlockSpec((1, d), lambda i: (0, 0))],
        compiler_params=_params(("arbitrary",)),
    )(dz, w, x, gain, dres)


def _mm_tn(a, b, ka, nb, a_tiled, split, name):
    s = a.shape[0]
    tm = _tile(s, 2 * TOKEN_TILE)
    nm = s // tm
    nj = a.shape[1] // ka if a_tiled else b.shape[1] // nb
    axis, parts = split
    pr, pc = (ka // parts, nb) if axis == 0 else (ka, nb // parts)

    def body(a_ref, b_ref, o_ref, acc):
        m = pl.program_id(1)

        @pl.when(m == 0)
        def _():
            acc[...] = jnp.zeros_like(acc)

        acc[...] += _dot(_bf(a_ref[...]), _bf(b_ref[...]), TN)

        @pl.when(m == nm - 1)
        def _():
            for q in range(parts):
                piece = acc[q * pr:(q + 1) * pr, :] if axis == 0 else acc[:, q * pc:(q + 1) * pc]
                o_ref[q] = piece.astype(o_ref.dtype)

    return pl.pallas_call(
        body, name=name, grid=(nj, nm), out_shape=SDS((nj * parts, pr, pc), BF16),
        in_specs=[pl.BlockSpec((tm, ka), (lambda j, m: (m, j)) if a_tiled else (lambda j, m: (m, 0))),
                  pl.BlockSpec((tm, nb), (lambda j, m: (m, 0)) if a_tiled else (lambda j, m: (m, j)))],
        out_specs=pl.BlockSpec((parts, pr, pc), lambda j, m: (j, 0, 0)),
        scratch_shapes=[pltpu.VMEM((ka, nb), F32)],
        compiler_params=_params(("parallel", "arbitrary")),
    )(a, b)


def _final_loss(x, gain, target, name):
    s, d = x.shape
    tm = _tile(s, TOKEN_TILE)

    def body(x_ref, g_ref, t_ref, loss_ref, dx_ref, dg_ref):
        @pl.when(pl.program_id(0) == 0)
        def _():
            loss_ref[...] = jnp.zeros_like(loss_ref)
            dg_ref[...] = jnp.zeros_like(dg_ref)

        xv = x_ref[...]
        r = lax.rsqrt(jnp.mean(xv * xv, axis=-1, keepdims=True) + RMS_EPS)
        xn = xv * r
        err = xn * g_ref[...] - t_ref[...]
        loss_ref[...] += (0.5 / d) * jnp.sum(err * err)
        dy = err * (1.0 / d)
        dg_ref[...] += jnp.sum(dy * xn, axis=0, keepdims=True)
        dxh = dy * g_ref[...]
        dx_ref[...] = r * (dxh - xn * jnp.mean(dxh * xn, axis=-1, keepdims=True))

    return pl.pallas_call(
        body, name=name, grid=(s // tm,),
        out_shape=[SDS((8, 128), F32), SDS((s, d), F32), SDS((1, d), F32)],
        in_specs=[pl.BlockSpec((tm, d), lambda i: (i, 0)), pl.BlockSpec((1, d), lambda i: (0, 0)),
                  pl.BlockSpec((tm, d), lambda i: (i, 0))],
        out_specs=[pl.BlockSpec((8, 128), lambda i: (0, 0)), pl.BlockSpec((tm, d), lambda i: (i, 0)),
                   pl.BlockSpec((1, d), lambda i: (0, 0))],
        compiler_params=_params(("arbitrary",)),
    )(x, gain, target)


def _retention_tables(s):
    half = RET_HEAD_DIM // 2
    inv_freq = 1.0 / (RET_ROPE_BASE ** jnp.linspace(0.0, 1.0, half, dtype=F32))
    ang = jnp.arange(s, dtype=F32)[:, None] * inv_freq[None, :]
    cos, sin = jnp.cos(ang), jnp.sin(ang)
    cos_e = jnp.repeat(cos, 2, axis=-1)
    sin_s = jnp.stack([-sin, sin], axis=-1).reshape(s, RET_HEAD_DIM)
    log_g = jnp.log1p(-jnp.power(2.0, -5.0 - jnp.arange(RET_HEADS, dtype=F32)))
    pos = jnp.arange(CHUNK, dtype=F32)
    dmat = jnp.exp(jnp.abs(pos[:, None] - pos[None, :])[None] * log_g[:, None, None])
    qdec = jnp.exp((pos[None, :] + 1.0) * log_g[:, None])
    kdec = jnp.exp((CHUNK - 1.0 - pos[None, :]) * log_g[:, None])
    lam = jnp.exp(CHUNK * log_g)
    wide = (RET_HEADS, CHUNK, RET_HEAD_DIM)
    return dict(cos=cos_e, sin=sin_s, dmat=dmat,
                qdec=jnp.broadcast_to(qdec[:, :, None], wide),
                kdec=jnp.broadcast_to(kdec[:, :, None], wide),
                lam=jnp.broadcast_to(lam[:, None, None], (RET_HEADS, RET_HEAD_DIM, RET_HEAD_DIM)))


def _swap_pairs(t):
    lane = lax.broadcasted_iota(jnp.int32, t.shape, 1)
    return jnp.where(lane % 2 == 0, pltpu.roll(t, RET_HEAD_DIM - 1, 1), pltpu.roll(t, 1, 1))


def _head(h):
    return slice(h * RET_HEAD_DIM, (h + 1) * RET_HEAD_DIM)


def _ret_common_specs(tb, blk):
    zs = [pl.BlockSpec((tb, RET_WIDTH), functools.partial(lambda j, i: (blk(i), j), j)) for j in range(4)]
    tabs = [pl.BlockSpec((tb, RET_HEAD_DIM), lambda i: (blk(i), 0))] * 2
    consts = [pl.BlockSpec((1, RET_WIDTH), lambda i: (0, 0)),
              pl.BlockSpec((RET_HEADS, CHUNK, CHUNK), lambda i: (0, 0, 0)),
              pl.BlockSpec((RET_HEADS, CHUNK, RET_HEAD_DIM), lambda i: (0, 0, 0)),
              pl.BlockSpec((RET_HEADS, CHUNK, RET_HEAD_DIM), lambda i: (0, 0, 0)),
              pl.BlockSpec((RET_HEADS, RET_HEAD_DIM, RET_HEAD_DIM), lambda i: (0, 0, 0))]
    return zs + tabs + consts


def _ret_fwd(z, tabs, gn_gain, name):
    s = z.shape[0]
    tb = _tile(s, TOKEN_TILE)
    ncb = tb // CHUNK
    scale = RET_HEAD_DIM ** -0.5

    def body(q_ref, k_ref, v_ref, g_ref, cos_ref, sin_ref, gain_ref, dm_ref, qd_ref, kd_ref, lam_ref,
             o_ref, st_ref, ret_ref, s_scr, qr_scr, kr_scr):
        @pl.when(pl.program_id(0) == 0)
        def _():
            s_scr[...] = jnp.zeros_like(s_scr)

        cosv, sinv = cos_ref[...], sin_ref[...]
        for h in range(RET_HEADS):
            qh, kh = q_ref[:, _head(h)], k_ref[:, _head(h)]
            qr_scr[:, _head(h)] = qh * cosv + _swap_pairs(qh) * sinv
            kr_scr[:, _head(h)] = (kh * cosv + _swap_pairs(kh) * sinv) * scale

        def chunk(c, carry):
            rows = pl.ds(pl.multiple_of(c * CHUNK, CHUNK), CHUNK)
            for h in range(RET_HEADS):
                qc, kc, vc = qr_scr[rows, _head(h)], kr_scr[rows, _head(h)], v_ref[rows, _head(h)]
                a = _dot(_bf(qc), _bf(kc), NT) * dm_ref[h]
                st = s_scr[h]
                st_ref[c, h] = st
                o_ref[rows, _head(h)] = _dot(_bf(a), _bf(vc)) + _dot(_bf(qc * qd_ref[h]), _bf(st))
                s_scr[h] = st * lam_ref[h] + _dot(_bf(kc * kd_ref[h]), _bf(vc), TN)
            return carry

        lax.fori_loop(0, ncb, chunk, 0)
        for h in range(RET_HEADS):
            o = o_ref[:, _head(h)]
            mu = jnp.mean(o, axis=-1, keepdims=True)
            oc = o - mu
            y = oc * lax.rsqrt(jnp.mean(oc * oc, axis=-1, keepdims=True) + GN_EPS) * gain_ref[:, _head(h)]
            g = g_ref[:, _head(h)]
            ret_ref[:, _head(h)] = _bf(g / (1.0 + jnp.exp(-g)) * y)

    nc = s // CHUNK
    return pl.pallas_call(
        body, name=name, grid=(s // tb,),
        out_shape=[SDS((s, RET_WIDTH), F32), SDS((nc, RET_HEADS, RET_HEAD_DIM, RET_HEAD_DIM), F32),
                   SDS((s, RET_WIDTH), BF16)],
        in_specs=_ret_common_specs(tb, lambda i: i),
        out_specs=[pl.BlockSpec((tb, RET_WIDTH), lambda i: (i, 0)),
                   pl.BlockSpec((ncb, RET_HEADS, RET_HEAD_DIM, RET_HEAD_DIM), lambda i: (i, 0, 0, 0)),
                   pl.BlockSpec((tb, RET_WIDTH), lambda i: (i, 0))],
        scratch_shapes=[pltpu.VMEM((RET_HEADS, RET_HEAD_DIM, RET_HEAD_DIM), F32),
                        pltpu.VMEM((tb, RET_WIDTH), F32), pltpu.VMEM((tb, RET_WIDTH), F32)],
        compiler_params=_params(("arbitrary",)),
    )(z, z, z, z, tabs["cos"], tabs["sin"], gn_gain, tabs["dmat"], tabs["qdec"], tabs["kdec"], tabs["lam"])


def _ret_bwd(z, tabs, gn_gain, o_pre, states, du, name):
    s = z.shape[0]
    tb = _tile(s, TOKEN_TILE)
    ncb = tb // CHUNK
    nblk = s // tb
    scale = RET_HEAD_DIM ** -0.5
    rev = lambda i: nblk - 1 - i

    def body(q_ref, k_ref, v_ref, g_ref, cos_ref, sin_ref, gain_ref, dm_ref, qd_ref, kd_ref, lam_ref,
             o_ref, st_ref, dret_ref, dz_ref, dgain_ref, g_scr, qr_scr, kr_scr, do_scr, dq_scr, dk_scr):
        @pl.when(pl.program_id(0) == 0)
        def _():
            g_scr[...] = jnp.zeros_like(g_scr)
            dgain_ref[...] = jnp.zeros_like(dgain_ref)

        cosv, sinv = cos_ref[...], sin_ref[...]
        for h in range(RET_HEADS):
            hs = _head(h)
            qh, kh = q_ref[:, hs], k_ref[:, hs]
            qr_scr[:, hs] = qh * cosv + _swap_pairs(qh) * sinv
            kr_scr[:, hs] = (kh * cosv + _swap_pairs(kh) * sinv) * scale
            o = o_ref[:, hs]
            mu = jnp.mean(o, axis=-1, keepdims=True)
            oc = o - mu
            rstd = lax.rsqrt(jnp.mean(oc * oc, axis=-1, keepdims=True) + GN_EPS)
            yh = oc * rstd
            gain = gain_ref[:, hs]
            g = g_ref[:, hs]
            sg = 1.0 / (1.0 + jnp.exp(-g))
            dret = dret_ref[:, hs].astype(F32)
            dy = dret * (g * sg)
            dz_ref[:, 3 * RET_WIDTH + h * RET_HEAD_DIM:3 * RET_WIDTH + (h + 1) * RET_HEAD_DIM] = _bf(
                dret * (yh * gain) * (sg * (1.0 + g * (1.0 - sg))))
            dgain_ref[:, hs] += jnp.sum(dy * yh, axis=0, keepdims=True)
            dyh = dy * gain
            do_scr[:, hs] = rstd * (dyh - jnp.mean(dyh, axis=-1, keepdims=True)
                                    - yh * jnp.mean(dyh * yh, axis=-1, keepdims=True))

        def chunk(cc, carry):
            c = ncb - 1 - cc
            rows = pl.ds(pl.multiple_of(c * CHUNK, CHUNK), CHUNK)
            for h in range(RET_HEADS):
                hs = _head(h)
                qc, kc, vc, doc = _bf(qr_scr[rows, hs]), _bf(kr_scr[rows, hs]), _bf(v_ref[rows, hs]), _bf(do_scr[rows, hs])
                qdc, kdc = qd_ref[h], kd_ref[h]
                st, gs = _bf(st_ref[c, h]), g_scr[h]
                gsb = _bf(gs)
                dm = dm_ref[h]
                p = _bf(_dot(qc, kc, NT) * dm)
                da = _bf(_dot(doc, vc, NT) * dm)
                kt = _bf(kr_scr[rows, hs] * kdc)
                qt = _bf(qr_scr[rows, hs] * qdc)
                dz_ref[rows, 2 * RET_WIDTH + h * RET_HEAD_DIM:2 * RET_WIDTH + (h + 1) * RET_HEAD_DIM] = _bf(
                    _dot(p, doc, TN) + _dot(kt, gsb))
                dq_scr[rows, hs] = _dot(da, kc) + _dot(doc, st, NT) * qdc
                dk_scr[rows, hs] = _dot(da, qc, TN) + _dot(vc, gsb, NT) * kdc
                g_scr[h] = gs * lam_ref[h] + _dot(qt, doc, TN)
            return carry

        lax.fori_loop(0, ncb, chunk, 0)
        for h in range(RET_HEADS):
            hs = _head(h)
            dq, dk = dq_scr[:, hs], dk_scr[:, hs]
            dz_ref[:, h * RET_HEAD_DIM:(h + 1) * RET_HEAD_DIM] = _bf(dq * cosv - _swap_pairs(dq) * sinv)
            dz_ref[:, RET_WIDTH + h * RET_HEAD_DIM:RET_WIDTH + (h + 1) * RET_HEAD_DIM] = _bf(
                (dk * cosv - _swap_pairs(dk) * sinv) * scale)

    return pl.pallas_call(
        body, name=name, grid=(nblk,),
        out_shape=[SDS((s, 4 * RET_WIDTH), BF16), SDS((1, RET_WIDTH), F32)],
        in_specs=_ret_common_specs(tb, rev)
        + [pl.BlockSpec((tb, RET_WIDTH), lambda i: (rev(i), 0)),
           pl.BlockSpec((ncb, RET_HEADS, RET_HEAD_DIM, RET_HEAD_DIM), lambda i: (rev(i), 0, 0, 0)),
           pl.BlockSpec((tb, RET_WIDTH), lambda i: (rev(i), 0))],
        out_specs=[pl.BlockSpec((tb, 4 * RET_WIDTH), lambda i: (rev(i), 0)),
                   pl.BlockSpec((1, RET_WIDTH), lambda i: (0, 0))],
        scratch_shapes=[pltpu.VMEM((RET_HEADS, RET_HEAD_DIM, RET_HEAD_DIM), F32)]
        + [pltpu.VMEM((tb, RET_WIDTH), F32)] * 5,
        compiler_params=_params(("arbitrary",)),
    )(z, z, z, z, tabs["cos"], tabs["sin"], gn_gain, tabs["dmat"], tabs["qdec"], tabs["kdec"], tabs["lam"],
      o_pre, states, du)


POOL_COL = 4 * RET_WIDTH // POOL_WIDTH


def _pooled(cur, prev, t0):
    tm = cur.shape[0]
    xx = jnp.concatenate([prev, cur], axis=0)
    sums = {1: xx}
    w = 1
    while w < POOL_WINDOWS[-1]:
        sums[2 * w] = sums[w] + pltpu.roll(sums[w], w, 0)
        w *= 2
    t = t0 + lax.broadcasted_iota(jnp.int32, (tm, 128), 0)
    outs = []
    for gi, w in enumerate(POOL_WINDOWS):
        cols = slice(gi * 128, (gi + 1) * 128)
        cnt = jnp.minimum(t + 1, w).astype(F32)
        outs.append(sums[w][POOL_HALO:, cols] / cnt - cur[:, cols])
    return outs


def _pool_fwd(z, w_pool, scale, name):
    s = z.shape[0]
    tm = _tile(s, TOKEN_TILE)
    hb = tm // POOL_HALO

    def body(p_ref, prev_ref, w_ref, sc_ref, o_ref):
        i = pl.program_id(0)
        prev = jnp.where(i > 0, prev_ref[...], 0.0)
        pooled = _pooled(p_ref[...], prev, i * tm)
        for gi in range(len(POOL_WINDOWS)):
            cols = slice(gi * 128, (gi + 1) * 128)
            o_ref[:, cols] = _bf(_dot(_bf(pooled[gi]), _bf(w_ref[gi])) * sc_ref[:, cols])

    return pl.pallas_call(
        body, name=name, grid=(s // tm,), out_shape=SDS((s, POOL_WIDTH), BF16),
        in_specs=[pl.BlockSpec((tm, POOL_WIDTH), lambda i: (i, POOL_COL)),
                  pl.BlockSpec((POOL_HALO, POOL_WIDTH), lambda i: (jnp.maximum(i * hb - 1, 0), POOL_COL)),
                  pl.BlockSpec(w_pool.shape, lambda i: (0, 0, 0)),
                  pl.BlockSpec((1, POOL_WIDTH), lambda i: (0, 0))],
        out_specs=pl.BlockSpec((tm, POOL_WIDTH), lambda i: (i, 0)),
        compiler_params=_params(("parallel",)),
    )(z, z, w_pool, scale)


def _pool_bwd(z, w_pool, scale, du, name):
    s = z.shape[0]
    tm = _tile(s, TOKEN_TILE)
    hb = tm // POOL_HALO
    nblk = s // tm
    last_halo = s // POOL_HALO - 1

    def body(p_ref, prev_ref, w_ref, sc_ref, do_ref, don_ref, dp_ref, dw_ref, dsc_ref):
        i = pl.program_id(0)

        @pl.when(i == 0)
        def _():
            dw_ref[...] = jnp.zeros_like(dw_ref)
            dsc_ref[...] = jnp.zeros_like(dsc_ref)

        prev = jnp.where(i > 0, prev_ref[...], 0.0)
        pooled = _pooled(p_ref[...], prev, i * tm)
        dout = do_ref[...].astype(F32)
        dout_next = jnp.where(i < nblk - 1, don_ref[...].astype(F32), 0.0)
        sc = sc_ref[...]
        dmix = jnp.concatenate([dout * sc, dout_next * sc], axis=0)
        n = tm + POOL_HALO
        t = i * tm + lax.broadcasted_iota(jnp.int32, (n, 128), 0)
        for gi, w in enumerate(POOL_WINDOWS):
            cols = slice(gi * 128, (gi + 1) * 128)
            wg = _bf(w_ref[gi])
            pg = _bf(pooled[gi])
            dsc_ref[:, cols] += jnp.sum(dout[:, cols] * _dot(pg, wg), axis=0, keepdims=True)
            dw_ref[gi] += _dot(pg, _bf(dmix[:tm, cols]), TN)
            dpool = _dot(_bf(dmix[:, cols]), wg, NT)
            acc = dpool / jnp.minimum(t + 1, w).astype(F32)
            step = 1
            while step < w:
                acc = acc + pltpu.roll(acc, n - step, 0)
                step *= 2
            dp_ref[:, cols] = _bf(acc[:tm] - dpool[:tm])

    return pl.pallas_call(
        body, name=name, grid=(nblk,),
        out_shape=[SDS((s, POOL_WIDTH), BF16), SDS(w_pool.shape, F32), SDS((1, POOL_WIDTH), F32)],
        in_specs=[pl.BlockSpec((tm, POOL_WIDTH), lambda i: (i, POOL_COL)),
                  pl.BlockSpec((POOL_HALO, POOL_WIDTH), lambda i: (jnp.maximum(i * hb - 1, 0), POOL_COL)),
                  pl.BlockSpec(w_pool.shape, lambda i: (0, 0, 0)),
                  pl.BlockSpec((1, POOL_WIDTH), lambda i: (0, 0)),
                  pl.BlockSpec((tm, POOL_WIDTH), lambda i: (i, 1)),
                  pl.BlockSpec((POOL_HALO, POOL_WIDTH), lambda i: (jnp.minimum((i + 1) * hb, last_halo), 1))],
        out_specs=[pl.BlockSpec((tm, POOL_WIDTH), lambda i: (i, 0)),
                   pl.BlockSpec(w_pool.shape, lambda i: (0, 0, 0)),
                   pl.BlockSpec((1, POOL_WIDTH), lambda i: (0, 0))],
        compiler_params=_params(("arbitrary",)),
    )(z, z, w_pool, scale, du, du)


def _rel_onehot():
    r = lax.broadcasted_iota(jnp.int32, (REL_PAD, ATT_DIAG), 0)
    c = lax.broadcasted_iota(jnp.int32, (REL_PAD, ATT_DIAG), 1)
    rel = jnp.where(c < ATT_K_TILE, jnp.clip(LEFT_CHUNKS * CHUNK - c, -REL_CLIP, REL_CLIP) + REL_CLIP,
                    2 * REL_CLIP)
    return (rel == r).astype(BF16)


def _split3(v):
    hi = _bf(v)
    r1 = v - hi.astype(F32)
    mid = _bf(r1)
    return hi, mid, _bf(r1 - mid.astype(F32))


def _skew(v, sign):
    row = lax.broadcasted_iota(jnp.int32, v.shape, 0)
    bit = 1
    while bit < ATT_Q_TILE:
        shift = bit if sign > 0 else ATT_DIAG - bit
        v = jnp.where((row & bit) != 0, pltpu.roll(v, shift, 1), v)
        bit *= 2
    return v


def _attn_bias(rel_bias, name):
    def body(t_ref, o_ref):
        oh = _rel_onehot()
        base = sum(_dot(part, oh) for part in _split3(t_ref[0]))
        full = _skew(jnp.broadcast_to(base[0:1], (ATT_Q_TILE, ATT_DIAG)), +1)[:, :ATT_K_TILE]
        qc = lax.broadcasted_iota(jnp.int32, full.shape, 0) // CHUNK
        kc = lax.broadcasted_iota(jnp.int32, full.shape, 1) // CHUNK
        o_ref[0] = jnp.where((kc >= qc) & (kc <= qc + LEFT_CHUNKS), full, NEG_INF)

    t8 = jnp.broadcast_to(rel_bias[:, None, :], (ATT_HEADS, 8, REL_PAD))
    return pl.pallas_call(
        body, name=name, grid=(ATT_HEADS,), out_shape=SDS((ATT_HEADS, ATT_Q_TILE, ATT_K_TILE), F32),
        in_specs=[pl.BlockSpec((1, 8, REL_PAD), lambda h: (h, 0, 0))],
        out_specs=pl.BlockSpec((1, ATT_Q_TILE, ATT_K_TILE), lambda h: (h, 0, 0)),
        compiler_params=_params(("parallel",)),
    )(t8)


def _attn_dbias(dbias, name):
    def body(d_ref, o_ref):
        pad = jnp.zeros((ATT_Q_TILE, ATT_DIAG - ATT_K_TILE), F32)
        diag = _skew(jnp.concatenate([d_ref[0], pad], axis=1), -1)
        col = jnp.sum(diag, axis=0, keepdims=True)
        oh = _rel_onehot()
        col8 = jnp.broadcast_to(col, (8, ATT_DIAG))
        o_ref[0] = sum(_dot(part, oh, NT) for part in _split3(col8))

    out = pl.pallas_call(
        body, name=name, grid=(ATT_HEADS,), out_shape=SDS((ATT_HEADS, 8, REL_PAD), F32),
        in_specs=[pl.BlockSpec((1, ATT_Q_TILE, ATT_K_TILE), lambda h: (h, 0, 0))],
        out_specs=pl.BlockSpec((1, 8, REL_PAD), lambda h: (h, 0, 0)),
        compiler_params=_params(("parallel",)),
    )(dbias)
    return out[:, 0, :]


def _attn_specs():
    hp = ATT_HEADS // 2
    kv = lambda off, back: pl.BlockSpec((ATT_Q_TILE, 128), lambda p, i: (jnp.maximum(i - back, 0), off + p))
    return ([pl.BlockSpec((ATT_Q_TILE, 128), lambda p, i: (i, p))]
            + [kv(hp, 2), kv(hp, 1), kv(hp, 0)] + [kv(2 * hp, 2), kv(2 * hp, 1), kv(2 * hp, 0)]
            + [pl.BlockSpec((2, ATT_Q_TILE, ATT_K_TILE), lambda p, i: (p, 0, 0))])


def _attn_probs(q, k, bias, i):
    sc = _dot(q, k, NT) + bias
    kpos = (i - 2) * ATT_Q_TILE + lax.broadcasted_iota(jnp.int32, sc.shape, 1)
    sc = jnp.where(kpos >= 0, sc, NEG_INF)
    e = jnp.exp(sc - jnp.max(sc, axis=-1, keepdims=True))
    return e / jnp.sum(e, axis=-1, keepdims=True)


def _attn_fwd(z, bias, name):
    s = z.shape[0]

    def body(q_ref, k0, k1, k2, v0, v1, v2, b_ref, o_ref):
        i = pl.program_id(1)
        for hh in range(2):
            cs = slice(hh * ATT_HEAD_DIM, (hh + 1) * ATT_HEAD_DIM)
            q = _bf(q_ref[:, cs].astype(F32) * ATT_HEAD_DIM ** -0.5)
            k = jnp.concatenate([k0[:, cs], k1[:, cs], k2[:, cs]], axis=0)
            v = jnp.concatenate([v0[:, cs], v1[:, cs], v2[:, cs]], axis=0)
            p = _attn_probs(q, k, b_ref[hh], i)
            o_ref[:, cs] = _bf(_dot(_bf(p), v))

    return pl.pallas_call(
        body, name=name, grid=(ATT_HEADS // 2, s // ATT_Q_TILE), out_shape=SDS((s, D_MODEL), BF16),
        in_specs=_attn_specs(),
        out_specs=pl.BlockSpec((ATT_Q_TILE, 128), lambda p, i: (i, p)),
        compiler_params=_params(("parallel", "parallel")),
    )(z, z, z, z, z, z, z, bias)


def _attn_bwd(z, bias, o, do, name):
    s = z.shape[0]

    def body(q_ref, k0, k1, k2, v0, v1, v2, b_ref, o_ref, do_ref, dq_ref, dk_ref, dv_ref, db_ref):
        i = pl.program_id(1)

        @pl.when(i == 0)
        def _():
            db_ref[...] = jnp.zeros_like(db_ref)

        for hh in range(2):
            cs = slice(hh * ATT_HEAD_DIM, (hh + 1) * ATT_HEAD_DIM)
            q = _bf(q_ref[:, cs].astype(F32) * ATT_HEAD_DIM ** -0.5)
            k = jnp.concatenate([k0[:, cs], k1[:, cs], k2[:, cs]], axis=0)
            v = jnp.concatenate([v0[:, cs], v1[:, cs], v2[:, cs]], axis=0)
            p = _attn_probs(q, k, b_ref[hh], i)
            doh = do_ref[:, cs]
            delta = jnp.sum(doh.astype(F32) * o_ref[:, cs].astype(F32), axis=-1, keepdims=True)
            ds = p * (_dot(doh, v, NT) - delta)
            db_ref[hh] += ds
            dsb = _bf(ds)
            dq_ref[:, cs] = _bf(_dot(dsb, k) * ATT_HEAD_DIM ** -0.5)
            dk = _dot(dsb, q, TN)
            dv = _dot(_bf(p), doh, TN)
            for t in range(3):
                rows = slice(t * ATT_Q_TILE, (t + 1) * ATT_Q_TILE)
                dk_ref[t, :, cs] = _bf(dk[rows])
                dv_ref[t, :, cs] = _bf(dv[rows])

    tile = pl.BlockSpec((ATT_Q_TILE, 128), lambda p, i: (i, p))
    part = pl.BlockSpec((3, ATT_Q_TILE, 128), lambda p, i: (0, i, p))
    return pl.pallas_call(
        body, name=name, grid=(ATT_HEADS // 2, s // ATT_Q_TILE),
        out_shape=[SDS((s, D_MODEL), BF16), SDS((3, s, D_MODEL), BF16), SDS((3, s, D_MODEL), BF16),
                   SDS((ATT_HEADS, ATT_Q_TILE, ATT_K_TILE), F32)],
        in_specs=_attn_specs() + [tile, tile],
        out_specs=[tile, part, part, pl.BlockSpec((2, ATT_Q_TILE, ATT_K_TILE), lambda p, i: (p, 0, 0))],
        compiler_params=_params(("parallel", "arbitrary")),
    )(z, z, z, z, z, z, z, bias, o, do)


def _attn_combine(dq, dkp, dvp, name):
    s = dq.shape[0]
    nq = s // ATT_Q_TILE

    def body(dq_ref, k2, k1, k0, v2, v1, v0, o_ref):
        i = pl.program_id(0)

        def total(a, b, c):
            t = a[0].astype(F32)
            t = t + jnp.where(i + 1 < nq, b[0].astype(F32), 0.0)
            return _bf(t + jnp.where(i + 2 < nq, c[0].astype(F32), 0.0))

        o_ref[:, 0:D_MODEL] = dq_ref[...]
        o_ref[:, D_MODEL:2 * D_MODEL] = total(k2, k1, k0)
        o_ref[:, 2 * D_MODEL:3 * D_MODEL] = total(v2, v1, v0)

    slot = lambda t: pl.BlockSpec((1, ATT_Q_TILE, D_MODEL), lambda i: (t, jnp.minimum(i + 2 - t, nq - 1), 0))
    return pl.pallas_call(
        body, name=name, grid=(nq,), out_shape=SDS((s, 3 * D_MODEL), BF16),
        in_specs=[pl.BlockSpec((ATT_Q_TILE, D_MODEL), lambda i: (i, 0)), slot(2), slot(1), slot(0),
                  slot(2), slot(1), slot(0)],
        out_specs=pl.BlockSpec((ATT_Q_TILE, 3 * D_MODEL), lambda i: (i, 0)),
        compiler_params=_params(("parallel",)),
    )(dq, dkp, dkp, dkp, dvp, dvp, dvp)


FWD_GROUPS = (
    (("ab_w_in", 0),),
    (("ab_w_out", 0), ("w_ffn_in", 0), ("w_ffn_out", 0)),
    (("c_w_qkv", 0), ("c_w_out", 0), ("w_ffn_in", 1), ("w_ffn_out", 1)),
    (("ab_w_in", 1), ("ab_w_out", 1), ("w_ffn_in", 2), ("w_ffn_out", 2)),
    (("c_w_qkv", 1), ("c_w_out", 1), ("w_ffn_in", 3), ("w_ffn_out", 3)),
)


def _local_step(x, target, small, comm):
    s = x.shape[0]
    tabs = _retention_tables(s)
    saved, w = [], {}
    for layer in range(DEPTH):
        i = layer // 2
        sv = {"x0": x}
        g_mix = small["mix_norm"][layer:layer + 1]
        if layer % 2 == 0:
            if layer == 0:
                w.update(comm.weights(0, x))
            sv["h1"], sv["z"] = _norm_mm(x, g_mix, w["ab_w_in", i], AB_IN_WIDTH, F32, False, "ab_in_fwd")
            gn = small["ab_gn_gain"][i:i + 1]
            sv["o_pre"], sv["states"], ret = _ret_fwd(sv["z"], tabs, gn, "ret_fwd")
            pool = _pool_fwd(sv["z"], small["ab_w_pool"][i], small["ab_pool_scale"][i:i + 1], "pool_fwd")
            sv["u"] = (ret, pool)
            if layer == 0:
                w.update(comm.weights(1, pool))
            x = _mm_res([ret, pool], w["ab_w_out", i], x, "ab_out_fwd")
        else:
            sv["h1"], sv["z"] = _norm_mm(x, g_mix, w["c_w_qkv", i], 3 * D_MODEL // N_DEV, BF16, False, "qkv_fwd")
            rb = jnp.pad(small["c_rel_bias"][i], ((0, 0), (0, REL_PAD - N_REL)))
            sv["bias"] = _attn_bias(rb, "attn_bias")
            sv["o"] = _attn_fwd(sv["z"], sv["bias"], "attn_fwd")
            x = _mm_res([sv["o"]], w["c_w_out", i], x, "c_out_fwd")
        sv["x1"] = x
        sv["h2"], sv["z1"], sv["a"] = _norm_mm(x, small["ffn_norm"][layer:layer + 1], w["w_ffn_in", layer],
                                               D_FF // N_DEV, BF16, True, "ffn_in_fwd")
        x = _mm_res([sv["a"]], w["w_ffn_out", layer], x, "ffn_out_fwd")
        saved.append(sv)
        comm.layer_done(layer, x)
        if layer + 1 < DEPTH:
            w.update(comm.weights(layer + 2, x))

    loss, dx, d_final = _final_loss(x, small["final_norm"][None, :], target, "final_loss")

    gs = {k: [None] * DEPTH for k in ("mix_norm", "ffn_norm")}
    for k in ("ab_gn_gain", "ab_w_pool", "ab_pool_scale", "c_rel_bias"):
        gs[k] = [None] * (DEPTH // 2)
    gs["final_norm"] = d_final[0]
    tok = jnp.zeros((), F32)
    for layer in reversed(range(DEPTH)):
        i = layer // 2
        sv = saved[layer]
        dz1 = _mm_nt_rows(dx, w["w_ffn_out", layer], sv["z1"], "ffn_out_bwd")
        gw = {("w_ffn_out", layer): _mm_tn(sv["a"], dx, 1024, D_MODEL, True, (0, 2), "ffn_out_dw"),
              ("w_ffn_in", layer): _mm_tn(sv["h2"], dz1, D_MODEL, 1024, False, (1, 2), "ffn_in_dw")}
        dx, dg = _mm_nt_normbwd(dz1, w["w_ffn_in", layer], sv["x1"], small["ffn_norm"][layer:layer + 1] + tok, dx,
                                "ffn_in_bwd")
        gs["ffn_norm"][layer] = dg[0]
        if layer == 0:
            tok = comm.send(gw, dx)
            gw = {}
        g_mix = small["mix_norm"][layer:layer + 1] + tok
        if layer % 2 == 0:
            du = _mm_nt_rows(dx, w["ab_w_out", i], None, "mix_out_bwd")
            ret, pool = sv["u"]
            u = jnp.concatenate([ret, pool], axis=1)
            gw["ab_w_out", i] = _mm_tn(u, dx, D_MODEL, D_MODEL, True, (0, N_DEV), "mix_out_dw")
            gn = small["ab_gn_gain"][i:i + 1]
            dz_ret, dgn = _ret_bwd(sv["z"], tabs, gn, sv["o_pre"], sv["states"], du, "ret_bwd")
            dp, dwp, dsc = _pool_bwd(sv["z"], small["ab_w_pool"][i], small["ab_pool_scale"][i:i + 1], du, "pool_bwd")
            gs["ab_gn_gain"][i], gs["ab_w_pool"][i], gs["ab_pool_scale"][i] = dgn[0], dwp, dsc[0]
            dz = jnp.concatenate([dz_ret, dp], axis=1)
            gw["ab_w_in", i] = _to_shard_major(_mm_tn(sv["h1"], dz, D_MODEL, AB_IN_WIDTH // 2, False, (1, 1), "ab_in_dw"))
            dx, dg = _mm_nt_normbwd(dz, w["ab_w_in", i], sv["x0"], g_mix, dx, "ab_in_bwd")
        else:
            do = _mm_nt_rows(dx, w["c_w_out", i], None, "mix_out_bwd")
            gw["c_w_out", i] = _mm_tn(sv["o"], dx, D_MODEL, D_MODEL, True, (0, N_DEV), "mix_out_dw")
            dq, dkp, dvp, dbias = _attn_bwd(sv["z"], sv["bias"], sv["o"], do, "attn_bwd")
            gs["c_rel_bias"][i] = _attn_dbias(dbias, "attn_dbias")[:, :N_REL]
            dz = _attn_combine(dq, dkp, dvp, "attn_combine")
            gw["c_w_qkv", i] = _mm_tn(sv["h1"], dz, D_MODEL, 768, False, (1, 2), "qkv_dw")
            dx, dg = _mm_nt_normbwd(dz, w["c_w_qkv", i], sv["x0"], g_mix, dx, "qkv_bwd")
        gs["mix_norm"][layer] = dg[0]
        if layer > 0:
            tok = comm.send(gw, dx)
    gsmall = {k: (jnp.stack(v) if isinstance(v, list) else v) for k, v in gs.items()}
    return loss, dx, gw, gsmall


BIG = ("w_ffn_in", "w_ffn_out", "ab_w_in", "ab_w_out", "c_w_qkv", "c_w_out")
SMALL = ("mix_norm", "ffn_norm", "ab_gn_gain", "ab_w_pool", "ab_pool_scale", "c_rel_bias", "final_norm")
HBM = pl.BlockSpec(memory_space=pltpu.HBM)
SEM = pl.BlockSpec(memory_space=pltpu.SEMAPHORE)
ANY = pl.BlockSpec(memory_space=pl.ANY)
N_PEERS = N_DEV - 1
FLIPS = [(fx, fy, fc) for fx in (0, 1) for fy in (0, 1) for fc in (0, 1)][1:]


def _peers():
    x, y, c = (lax.axis_index(a) for a in MESH_AXES)
    peers = []
    for fx, fy, fc in FLIPS:
        px, py, pc = (1 - x if fx else x), (1 - y if fy else y), (1 - c if fc else c)
        peers.append(((px, py, pc), 4 * px + 2 * py + pc))
    return 4 * x + 2 * y + c, peers


def _remote(src, dst, sems, j, rel, pos):
    n = j * N_PEERS + rel
    return pltpu.make_async_remote_copy(src_ref=src, dst_ref=dst, send_sem=sems[0].at[n], recv_sem=sems[1].at[n],
                                        device_id=pos, device_id_type=pl.DeviceIdType.MESH)


def _send_start(srcs, by_slot, groups, after, name):
    n, ng = len(srcs), len(groups)
    lands = [lax.empty((N_DEV,) + (a.shape[1:] if by_slot else a.shape), a.dtype) for a in srcs]

    def body(*refs):
        src_refs, land_refs = refs[:n], refs[n:2 * n]
        sem_refs = refs[2 * n + 1:2 * n + 1 + 2 * ng]
        token, local_sems = refs[4 * n + 1 + 2 * ng], refs[4 * n + 2 + 2 * ng]
        me, peers = _peers()
        local = []
        for g, members in enumerate(groups):
            for j, k in enumerate(members):
                own = pltpu.make_async_copy(src_refs[k].at[me] if by_slot else src_refs[k], land_refs[k].at[me],
                                            local_sems.at[k])
                own.start()
                local.append(own)
                for rel, (pos, slot) in enumerate(peers):
                    _remote(src_refs[k].at[slot] if by_slot else src_refs[k], land_refs[k].at[me],
                            sem_refs[2 * g:2 * g + 2], j, rel, pos).start()
        for own in local:
            own.wait()
        token[...] = jnp.zeros_like(token)

    out_shape = []
    for members in groups:
        out_shape += [pltpu.SemaphoreType.DMA((len(members) * N_PEERS,))] * 2
    out_shape += [pltpu.HBM(a.shape, a.dtype) for a in list(srcs) + lands] + [SDS((8, 128), F32)]
    outs = pl.pallas_call(
        body, name=name, out_shape=out_shape,
        in_specs=[HBM] * (2 * n) + [ANY],
        out_specs=[SEM] * (2 * ng) + [HBM] * (2 * n) + [pl.BlockSpec(memory_space=pltpu.VMEM)],
        input_output_aliases={i: 2 * ng + i for i in range(2 * n)},
        scratch_shapes=[pltpu.SemaphoreType.DMA((n,))],
        compiler_params=pltpu.CompilerParams(has_side_effects=pltpu.SideEffectType.DATAFLOW_SIDE_EFFECTING),
    )(*[pltpu.with_memory_space_constraint(a, pltpu.HBM) for a in list(srcs) + lands], after)
    sems = [tuple(outs[2 * g:2 * g + 2]) for g in range(ng)]
    return sems, outs[2 * ng:2 * ng + n], outs[2 * ng + n:2 * ng + 2 * n], outs[-1][0, 0]


def _send_wait(sems, srcs, lands, by_slot, after, name):
    n = len(srcs)

    def body(*refs):
        src_refs, land_refs, sem_refs = refs[:n], refs[n:2 * n], refs[2 * n:2 * n + 2]
        _, peers = _peers()
        for j in range(n):
            for rel, (pos, slot) in enumerate(peers):
                cp = _remote(src_refs[j].at[slot] if by_slot else src_refs[j], land_refs[j].at[slot], sem_refs, j, rel, pos)
                cp.wait_send()
                cp.wait_recv()

    outs = pl.pallas_call(
        body, name=name, out_shape=[pltpu.HBM(a.shape, a.dtype) for a in list(srcs) + list(lands)],
        in_specs=[HBM] * (2 * n) + [SEM, SEM, ANY], out_specs=[HBM] * (2 * n),
        input_output_aliases={i: i for i in range(2 * n)},
        compiler_params=pltpu.CompilerParams(has_side_effects=pltpu.SideEffectType.DATAFLOW_SIDE_EFFECTING),
    )(*srcs, *lands, *sems, after)
    return outs[n:]


def _cast_bf16(w, name):
    nl, r, c = w.shape
    tr = _tile(r, 512)

    def body(i_ref, *o_refs):
        for l in range(nl):
            o_refs[l][...] = _bf(i_ref[l])

    return pl.pallas_call(
        body, name=name, grid=(r // tr,), out_shape=[SDS((r, c), BF16)] * nl,
        in_specs=[pl.BlockSpec((nl, tr, c), lambda i: (0, i, 0))],
        out_specs=[pl.BlockSpec((tr, c), lambda i: (i, 0))] * nl,
        compiler_params=_params(("parallel",)),
    )(w)


def _to_shard_major(g):
    nj, ka, nb = g.shape
    full = jnp.transpose(g, (1, 0, 2)).reshape(ka, N_DEV, nj * nb // N_DEV)
    return jnp.transpose(full, (1, 0, 2))


def _from_gathered(name, g):
    if name in ("w_ffn_out", "ab_w_out", "c_w_out"):
        return g.reshape(g.shape[0] * g.shape[1], g.shape[2])
    if name == "ab_w_in":
        return jnp.transpose(g, (1, 0, 2)).reshape(1, g.shape[1], N_DEV * g.shape[2])
    return g


class _Comm:
    def __init__(self, shards):
        self.shards = shards
        self.pending = {}
        self.sent = []

    def _start_gather(self, group_ids, after):
        keys = [FWD_GROUPS[g] for g in group_ids]
        flat = [k for ks in keys for k in ks]
        index, groups = 0, []
        for ks in keys:
            groups.append(list(range(index, index + len(ks))))
            index += len(ks)
        sems, srcs, lands, _ = _send_start([self.shards[k] for k in flat], False, groups, after,
                                           "gather_start_%d" % group_ids[0])
        for g, members, sem in zip(group_ids, groups, sems):
            self.pending[g] = (sem, [srcs[k] for k in members], [lands[k] for k in members])

    def weights(self, g, after):
        if g == 0:
            self._start_gather((0, 1, 2), after)
        sem, srcs, lands = self.pending.pop(g)
        got = _send_wait(sem, srcs, lands, False, after, "gather_wait_%d" % g)
        return {key: _from_gathered(key[0], arr) for key, arr in zip(FWD_GROUPS[g], got)}

    def layer_done(self, layer, x):
        if layer == 0:
            self._start_gather((3, 4), x)

    def send(self, grads, after):
        keys = list(grads)
        sems, srcs, lands, tok = _send_start([grads[k] for k in keys], True, [list(range(len(keys)))], after,
                                             "grad_start_%d" % len(self.sent))
        self.sent.append([keys, sems[0], srcs, lands, True])
        return tok

    def send_small(self, packed, after):
        sems, srcs, lands, _ = _send_start([packed], False, [[0]], after, "small_start")
        self.sent.append([["small"], sems[0], srcs, lands, False])

    def received(self, after):
        got = {}
        for n, (keys, sem, srcs, lands, by_slot) in enumerate(self.sent):
            for key, arr in zip(keys, _send_wait(sem, srcs, lands, by_slot, after, "grad_wait_%d" % n)):
                got[key] = arr
        return got


def _adamw_math(g, w, m, v):
    m2 = ADAM_B1 * m + (1.0 - ADAM_B1) * g
    v2 = ADAM_B2 * v + (1.0 - ADAM_B2) * jnp.square(g)
    m_hat = m2 / (1.0 - ADAM_B1 ** ADAM_STEP)
    v_hat = v2 / (1.0 - ADAM_B2 ** ADAM_STEP)
    delta = -ADAM_LR * (m_hat / (jnp.sqrt(v_hat) + ADAM_EPS) + ADAM_WD * w)
    return delta, m2, v2


def _adamw(recv, w, m, v, name):
    nl, r, c = w.shape
    tr = _tile(r, 256)

    def body(*refs):
        g_refs = refs[:nl]
        w_ref, m_ref, v_ref, go_ref, d_ref, mo_ref, vo_ref = refs[nl:]
        for l in range(nl):
            @pl.when(pl.program_id(0) == l)
            def _():
                g = g_refs[l][0].astype(F32)
                for p in range(1, N_DEV):
                    g = g + g_refs[l][p].astype(F32)
                go_ref[...] = g
                d_ref[...], mo_ref[...], vo_ref[...] = _adamw_math(g, w_ref[...], m_ref[...], v_ref[...])

    def recv_spec(l):
        return pl.BlockSpec((N_DEV, tr, c), lambda layer, i: (0, jnp.where(layer == l, i, 0), 0))

    blk = pl.BlockSpec((None, tr, c), lambda l, i: (l, i, 0))
    return pl.pallas_call(
        body, name=name, grid=(nl, r // tr), out_shape=[SDS(w.shape, F32)] * 4,
        in_specs=[recv_spec(l) for l in range(nl)] + [blk, blk, blk],
        out_specs=[blk] * 4,
        compiler_params=_params(("arbitrary", "arbitrary")),
    )(*recv, w, m, v)


def _adamw_small(recv, w, m, v, name):
    def body(g_ref, w_ref, m_ref, v_ref, go_ref, d_ref, mo_ref, vo_ref):
        g = g_ref[0]
        for p in range(1, N_DEV):
            g = g + g_ref[p]
        go_ref[...] = g
        d_ref[...], mo_ref[...], vo_ref[...] = _adamw_math(g, w_ref[...], m_ref[...], v_ref[...])

    return pl.pallas_call(body, name=name, out_shape=[SDS(w.shape, F32)] * 4,
                          compiler_params=_params(None))(recv, w, m, v)


def _pack_small(tree):
    parts = []
    for k in SMALL:
        flat = tree[k].reshape(-1)
        rows = -(-flat.shape[0] // 1024) * 8
        parts.append(jnp.pad(flat, (0, rows * 128 - flat.shape[0])).reshape(rows, 128))
    return jnp.concatenate(parts, axis=0)


def _unpack_small(packed, like):
    out, row = {}, 0
    for k in SMALL:
        size = like[k].size
        rows = -(-size // 1024) * 8
        out[k] = packed[row:row + rows].reshape(-1)[:size].reshape(like[k].shape)
        row += rows
    return out


def kernel(x, mix_norm, ffn_norm, w_ffn_in, w_ffn_out, ab_w_in, ab_gn_gain, ab_w_pool, ab_pool_scale, ab_w_out, c_w_qkv, c_rel_bias, c_w_out, final_norm, loss_target, m_mix_norm, m_ffn_norm, m_w_ffn_in, m_w_ffn_out, m_ab_w_in, m_ab_gn_gain, m_ab_w_pool, m_ab_pool_scale, m_ab_w_out, m_c_w_qkv, m_c_rel_bias, m_c_w_out, m_final_norm, v_mix_norm, v_ffn_norm, v_w_ffn_in, v_w_ffn_out, v_ab_w_in, v_ab_gn_gain, v_ab_w_pool, v_ab_pool_scale, v_ab_w_out, v_c_w_qkv, v_c_rel_bias, v_c_w_out, v_final_norm):
    args = dict(locals())
    weights = {k: args[k] for k in BIG + SMALL}
    moments_m = {k: args["m_" + k] for k in BIG + SMALL}
    moments_v = {k: args["v_" + k] for k in BIG + SMALL}

    small = {k: weights[k] for k in SMALL}

    shards = {}
    for k in BIG:
        for l, sh in enumerate(_cast_bf16(weights[k], "cast_" + k)):
            shards[k, l] = sh
    comm = _Comm(shards)
    loss, dx, last_grads, gsmall = _local_step(x[0], loss_target[0], small, comm)
    comm.send(last_grads, dx)
    comm.send_small(_pack_small(gsmall), dx)
    recv = comm.received(dx)

    outs = {}
    for k in BIG:
        layers = [recv[k, l] for l in range(weights[k].shape[0])]
        outs[k] = _adamw(layers, weights[k], moments_m[k], moments_v[k], "adamw_" + k)
    packed = _adamw_small(recv["small"], _pack_small(small), _pack_small(moments_m), _pack_small(moments_v),
                          "adamw_small")
    unpacked = [_unpack_small(p, small) for p in packed]
    for k in SMALL:
        outs[k] = [u[k] for u in unpacked]

    total = lax.psum(loss[0, 0], MESH_AXES)
    order = SMALL[:2] + BIG[:2] + ("ab_w_in", "ab_gn_gain", "ab_w_pool", "ab_pool_scale", "ab_w_out",
                                   "c_w_qkv", "c_rel_bias", "c_w_out", "final_norm")
    result = [total, dx[None]]
    for part in range(4):
        result += [outs[k][part] for k in order]
    return tuple(result)
```

```python
import functools

import jax
import jax.numpy as jnp
from jax import lax
from jax.experimental import pallas as pl
from jax.experimental.pallas import tpu as pltpu

F32 = jnp.float32
BF16 = jnp.bfloat16
SDS = jax.ShapeDtypeStruct
MESH_AXES = ("x", "y", "c")
N_DEV = 8

D_MODEL = 1024
DEPTH = 4
CHUNK = 64
D_FF = 4 * D_MODEL
RMS_EPS = 1e-6
RET_WIDTH = 512
RET_HEADS = 4
RET_HEAD_DIM = 128
RET_ROPE_BASE = 10000.0
GN_EPS = 1e-5
POOL_WIDTH = 512
POOL_WINDOWS = (2, 4, 8, 16)
POOL_HALO = 16
AB_IN_WIDTH = 4 * RET_WIDTH + POOL_WIDTH
ATT_HEADS = 16
ATT_HEAD_DIM = 64
LEFT_CHUNKS = 8
REL_CLIP = 128
N_REL = 2 * REL_CLIP + 1
NEG_INF = -1e30

ADAM_LR = 0.001
ADAM_B1 = 0.9
ADAM_B2 = 0.999
ADAM_EPS = 1e-08
ADAM_WD = 0.01
ADAM_STEP = 10

TOKEN_TILE = 512
ATT_Q_TILE = 128
ATT_BACK = LEFT_CHUNKS * CHUNK // ATT_Q_TILE
ATT_K_TILE = (ATT_BACK + 1) * ATT_Q_TILE
ATT_PAIRS = 4
ATT_DIAG = 1024
REL_PAD = 384
VMEM_LIMIT_MB = 56

NT = (((1,), (1,)), ((), ()))
TN = (((0,), (0,)), ((), ()))


def _params(semantics, **kw):
    return pltpu.CompilerParams(dimension_semantics=semantics,
                                vmem_limit_bytes=VMEM_LIMIT_MB * 2 ** 20, **kw)


def _dot(a, b, dims=None):
    if dims is None:
        return jnp.dot(a, b, preferred_element_type=F32)
    return lax.dot_general(a, b, dims, preferred_element_type=F32)


def _bf(v):
    return v.astype(BF16)


def _tile(n, t):
    return min(n, t)


def _norm_mm(x, gain, w, tn, z_dtype, relu2, name):
    s, d = x.shape
    nj = w.shape[0]
    tm = _tile(s, TOKEN_TILE)

    def body(x_ref, g_ref, w_ref, h_ref, z_ref, *a_ref):
        xv = x_ref[...]
        r = lax.rsqrt(jnp.mean(xv * xv, axis=-1, keepdims=True) + RMS_EPS)
        h = _bf(xv * r * g_ref[...])
        h_ref[...] = h
        cw = tn if tn <= 512 else 512
        for j in range(nj):
            for c in range(0, tn, cw):
                z = _dot(h, w_ref[j, :, c:c + cw])
                cols = slice(j * tn + c, j * tn + c + cw)
                z_ref[:, cols] = z.astype(z_ref.dtype)
                if relu2:
                    a_ref[0][:, cols] = _bf(jnp.square(jnp.maximum(z, 0.0)))

    n = nj * tn
    out_shape = [SDS((s, d), BF16), SDS((s, n), z_dtype)]
    out_specs = [pl.BlockSpec((tm, d), lambda i: (i, 0)), pl.BlockSpec((tm, n), lambda i: (i, 0))]
    if relu2:
        out_shape.append(SDS((s, n), BF16))
        out_specs.append(pl.BlockSpec((tm, n), lambda i: (i, 0)))
    return pl.pallas_call(
        body, name=name, grid=(s // tm,), out_shape=out_shape,
        in_specs=[pl.BlockSpec((tm, d), lambda i: (i, 0)),
                  pl.BlockSpec((1, d), lambda i: (0, 0)),
                  pl.BlockSpec((nj, d, tn), lambda i: (0, 0, 0))],
        out_specs=out_specs,
        compiler_params=_params(("parallel",)),
    )(x, gain, w)


def _mm_res(parts, w, res, name):
    s, d = res.shape
    tm = _tile(s, TOKEN_TILE)
    widths = [p.shape[1] for p in parts]

    def body(*refs):
        a_refs = refs[:len(parts)]
        w_ref, res_ref, o_ref = refs[len(parts):]
        acc = res_ref[...]
        off = 0
        for a_ref, k in zip(a_refs, widths):
            acc = acc + _dot(a_ref[...], w_ref[off:off + k, :])
            off += k
        o_ref[...] = acc

    return pl.pallas_call(
        body, name=name, grid=(s // tm,), out_shape=SDS((s, d), F32),
        in_specs=[pl.BlockSpec((tm, k), lambda i: (i, 0)) for k in widths]
        + [pl.BlockSpec(w.shape, lambda i: (0, 0)), pl.BlockSpec((tm, d), lambda i: (i, 0))],
        out_specs=pl.BlockSpec((tm, d), lambda i: (i, 0)),
        compiler_params=_params(("parallel",)),
    )(*parts, w, res)


def _mm_nt_rows(dy, w, z, name):
    s, d = dy.shape
    k = w.shape[0]
    tm = _tile(s, TOKEN_TILE)
    tk = _tile(k, 1024)

    def body(dy_ref, w_ref, *rest):
        o_ref = rest[-1]
        dyb = _bf(dy_ref[...])
        for j in range(k // tk):
            cols = slice(j * tk, (j + 1) * tk)
            da = _dot(dyb, w_ref[cols, :], NT)
            if z is not None:
                da = da * (2.0 * jnp.maximum(rest[0][:, cols].astype(F32), 0.0))
            o_ref[:, cols] = _bf(da)

    in_specs = [pl.BlockSpec((tm, d), lambda i: (i, 0)), pl.BlockSpec((k, d), lambda i: (0, 0))]
    args = [dy, w]
    if z is not None:
        in_specs.append(pl.BlockSpec((tm, k), lambda i: (i, 0)))
        args.append(z)
    return pl.pallas_call(
        body, name=name, grid=(s // tm,), out_shape=SDS((s, k), BF16),
        in_specs=in_specs, out_specs=pl.BlockSpec((tm, k), lambda i: (i, 0)),
        compiler_params=_params(("parallel",)),
    )(*args)


def _mm_nt_normbwd(dz, w, x, gain, dres, name):
    s, d = x.shape
    nj, _, nc = w.shape
    tm = _tile(s, TOKEN_TILE)

    def body(dz_ref, w_ref, x_ref, g_ref, dres_ref, dx_ref, dg_ref):
        dh = _dot(dz_ref[:, 0:nc], w_ref[0], NT)
        for j in range(1, nj):
            dh = dh + _dot(dz_ref[:, j * nc:(j + 1) * nc], w_ref[j], NT)
        xv = x_ref[...]
        r = lax.rsqrt(jnp.mean(xv * xv, axis=-1, keepdims=True) + RMS_EPS)
        xn = xv * r

        @pl.when(pl.program_id(0) == 0)
        def _():
            dg_ref[...] = jnp.zeros_like(dg_ref)

        dg_ref[...] += jnp.sum(dh * xn, axis=0, keepdims=True)
        dxh = dh * g_ref[...]
        dx_ref[...] = dres_ref[...] + r * (dxh - xn * jnp.mean(dxh * xn, axis=-1, keepdims=True))

    return pl.pallas_call(
        body, name=name, grid=(s // tm,), out_shape=[SDS((s, d), F32), SDS((1, d), F32)],
        in_specs=[pl.BlockSpec((tm, nj * nc), lambda i: (i, 0)),
                  pl.BlockSpec((nj, d, nc), lambda i: (0, 0, 0)),
                  pl.BlockSpec((tm, d), lambda i: (i, 0)),
                  pl.BlockSpec((1, d), lambda i: (0, 0)),
                  pl.BlockSpec((tm, d), lambda i: (i, 0))],
        out_specs=[pl.BlockSpec((tm, d), lambda i: (i, 0)), pl.BlockSpec((1, d), lambda i: (0, 0))],
        compiler_params=_params(("arbitrary",)),
    )(dz, w, x, gain, dres)


def _mm_tn(a, b, ka, nb, a_tiled, split, name):
    s = a.shape[0]
    tm = _tile(s, 2 * TOKEN_TILE)
    nm = s // tm
    nj = a.shape[1] // ka if a_tiled else b.shape[1] // nb
    axis, parts = split
    pr, pc = (ka // parts, nb) if axis == 0 else (ka, nb // parts)

    def body(a_ref, b_ref, o_ref, acc):
        m = pl.program_id(1)

        @pl.when(m == 0)
        def _():
            acc[...] = jnp.zeros_like(acc)

        acc[...] += _dot(_bf(a_ref[...]), _bf(b_ref[...]), TN)

        @pl.when(m == nm - 1)
        def _():
            for q in range(parts):
                piece = acc[q * pr:(q + 1) * pr, :] if axis == 0 else acc[:, q * pc:(q + 1) * pc]
                o_ref[q] = piece.astype(o_ref.dtype)

    return pl.pallas_call(
        body, name=name, grid=(nj, nm), out_shape=SDS((nj * parts, pr, pc), BF16),
        in_specs=[pl.BlockSpec((tm, ka), (lambda j, m: (m, j)) if a_tiled else (lambda j, m: (m, 0))),
                  pl.BlockSpec((tm, nb), (lambda j, m: (m, 0)) if a_tiled else (lambda j, m: (m, j)))],
        out_specs=pl.BlockSpec((parts, pr, pc), lambda j, m: (j, 0, 0)),
        scratch_shapes=[pltpu.VMEM((ka, nb), F32)],
        compiler_params=_params(("parallel", "arbitrary")),
    )(a, b)


def _final_loss(x, gain, target, name):
    s, d = x.shape
    tm = _tile(s, TOKEN_TILE)

    def body(x_ref, g_ref, t_ref, loss_ref, dx_ref, dg_ref):
        @pl.when(pl.program_id(0) == 0)
        def _():
            loss_ref[...] = jnp.zeros_like(loss_ref)
            dg_ref[...] = jnp.zeros_like(dg_ref)

        xv = x_ref[...]
        r = lax.rsqrt(jnp.mean(xv * xv, axis=-1, keepdims=True) + RMS_EPS)
        xn = xv * r
        err = xn * g_ref[...] - t_ref[...]
        loss_ref[...] += (0.5 / d) * jnp.sum(err * err)
        dy = err * (1.0 / d)
        dg_ref[...] += jnp.sum(dy * xn, axis=0, keepdims=True)
        dxh = dy * g_ref[...]
        dx_ref[...] = r * (dxh - xn * jnp.mean(dxh * xn, axis=-1, keepdims=True))

    return pl.pallas_call(
        body, name=name, grid=(s // tm,),
        out_shape=[SDS((8, 128), F32), SDS((s, d), F32), SDS((1, d), F32)],
        in_specs=[pl.BlockSpec((tm, d), lambda i: (i, 0)), pl.BlockSpec((1, d), lambda i: (0, 0)),
                  pl.BlockSpec((tm, d), lambda i: (i, 0))],
        out_specs=[pl.BlockSpec((8, 128), lambda i: (0, 0)), pl.BlockSpec((tm, d), lambda i: (i, 0)),
                   pl.BlockSpec((1, d), lambda i: (0, 0))],
        compiler_params=_params(("arbitrary",)),
    )(x, gain, target)


def _retention_tables(s):
    half = RET_HEAD_DIM // 2
    inv_freq = 1.0 / (RET_ROPE_BASE ** jnp.linspace(0.0, 1.0, half, dtype=F32))
    ang = jnp.arange(s, dtype=F32)[:, None] * inv_freq[None, :]
    cos, sin = jnp.cos(ang), jnp.sin(ang)
    cos_e = jnp.repeat(cos, 2, axis=-1)
    sin_s = jnp.stack([-sin, sin], axis=-1).reshape(s, RET_HEAD_DIM)
    log_g = jnp.log1p(-jnp.power(2.0, -5.0 - jnp.arange(RET_HEADS, dtype=F32)))
    pos = jnp.arange(CHUNK, dtype=F32)
    dmat = jnp.exp(jnp.abs(pos[:, None] - pos[None, :])[None] * log_g[:, None, None])
    qdec = jnp.exp((pos[None, :] + 1.0) * log_g[:, None])
    kdec = jnp.exp((CHUNK - 1.0 - pos[None, :]) * log_g[:, None])
    lam = jnp.exp(CHUNK * log_g)
    wide = (RET_HEADS, CHUNK, RET_HEAD_DIM)
    return dict(cos=cos_e, sin=sin_s, dmat=dmat,
                qdec=jnp.broadcast_to(qdec[:, :, None], wide),
                kdec=jnp.broadcast_to(kdec[:, :, None], wide),
                lam=jnp.broadcast_to(lam[:, None, None], (RET_HEADS, RET_HEAD_DIM, RET_HEAD_DIM)))


def _swap_pairs(t):
    lane = lax.broadcasted_iota(jnp.int32, t.shape, 1)
    return jnp.where(lane % 2 == 0, pltpu.roll(t, RET_HEAD_DIM - 1, 1), pltpu.roll(t, 1, 1))


def _head(h):
    return slice(h * RET_HEAD_DIM, (h + 1) * RET_HEAD_DIM)


def _ret_common_specs(tb, blk):
    zs = [pl.BlockSpec((tb, RET_WIDTH), functools.partial(lambda j, i: (blk(i), j), j)) for j in range(4)]
    tabs = [pl.BlockSpec((tb, RET_HEAD_DIM), lambda i: (blk(i), 0))] * 2
    consts = [pl.BlockSpec((1, RET_WIDTH), lambda i: (0, 0)),
              pl.BlockSpec((RET_HEADS, CHUNK, CHUNK), lambda i: (0, 0, 0)),
              pl.BlockSpec((RET_HEADS, CHUNK, RET_HEAD_DIM), lambda i: (0, 0, 0)),
              pl.BlockSpec((RET_HEADS, CHUNK, RET_HEAD_DIM), lambda i: (0, 0, 0)),
              pl.BlockSpec((RET_HEADS, RET_HEAD_DIM, RET_HEAD_DIM), lambda i: (0, 0, 0))]
    return zs + tabs + consts


def _ret_fwd(z, tabs, gn_gain, name):
    s = z.shape[0]
    tb = _tile(s, TOKEN_TILE)
    ncb = tb // CHUNK
    scale = RET_HEAD_DIM ** -0.5

    def body(q_ref, k_ref, v_ref, g_ref, cos_ref, sin_ref, gain_ref, dm_ref, qd_ref, kd_ref, lam_ref,
             o_ref, st_ref, ret_ref, s_scr, qr_scr, kr_scr):
        @pl.when(pl.program_id(0) == 0)
        def _():
            s_scr[...] = jnp.zeros_like(s_scr)

        cosv, sinv = cos_ref[...], sin_ref[...]
        for h in range(RET_HEADS):
            qh, kh = q_ref[:, _head(h)], k_ref[:, _head(h)]
            qr_scr[:, _head(h)] = qh * cosv + _swap_pairs(qh) * sinv
            kr_scr[:, _head(h)] = (kh * cosv + _swap_pairs(kh) * sinv) * scale

        def chunk(c, carry):
            rows = pl.ds(pl.multiple_of(c * CHUNK, CHUNK), CHUNK)
            for h in range(RET_HEADS):
                qc, kc, vc = qr_scr[rows, _head(h)], kr_scr[rows, _head(h)], v_ref[rows, _head(h)]
                a = _dot(_bf(qc), _bf(kc), NT) * dm_ref[h]
                st = s_scr[h]
                st_ref[c, h] = st
                o_ref[rows, _head(h)] = _dot(_bf(a), _bf(vc)) + _dot(_bf(qc * qd_ref[h]), _bf(st))
                s_scr[h] = st * lam_ref[h] + _dot(_bf(kc * kd_ref[h]), _bf(vc), TN)
            return carry

        lax.fori_loop(0, ncb, chunk, 0)
        for h in range(RET_HEADS):
            o = o_ref[:, _head(h)]
            mu = jnp.mean(o, axis=-1, keepdims=True)
            oc = o - mu
            y = oc * lax.rsqrt(jnp.mean(oc * oc, axis=-1, keepdims=True) + GN_EPS) * gain_ref[:, _head(h)]
            g = g_ref[:, _head(h)]
            ret_ref[:, _head(h)] = _bf(g / (1.0 + jnp.exp(-g)) * y)

    nc = s // CHUNK
    return pl.pallas_call(
        body, name=name, grid=(s // tb,),
        out_shape=[SDS((s, RET_WIDTH), F32), SDS((nc, RET_HEADS, RET_HEAD_DIM, RET_HEAD_DIM), F32),
                   SDS((s, RET_WIDTH), BF16)],
        in_specs=_ret_common_specs(tb, lambda i: i),
        out_specs=[pl.BlockSpec((tb, RET_WIDTH), lambda i: (i, 0)),
                   pl.BlockSpec((ncb, RET_HEADS, RET_HEAD_DIM, RET_HEAD_DIM), lambda i: (i, 0, 0, 0)),
                   pl.BlockSpec((tb, RET_WIDTH), lambda i: (i, 0))],
        scratch_shapes=[pltpu.VMEM((RET_HEADS, RET_HEAD_DIM, RET_HEAD_DIM), F32),
                        pltpu.VMEM((tb, RET_WIDTH), F32), pltpu.VMEM((tb, RET_WIDTH), F32)],
        compiler_params=_params(("arbitrary",)),
    )(z, z, z, z, tabs["cos"], tabs["sin"], gn_gain, tabs["dmat"], tabs["qdec"], tabs["kdec"], tabs["lam"])


def _ret_bwd(z, tabs, gn_gain, o_pre, states, du, name):
    s = z.shape[0]
    tb = _tile(s, TOKEN_TILE)
    ncb = tb // CHUNK
    nblk = s // tb
    scale = RET_HEAD_DIM ** -0.5
    rev = lambda i: nblk - 1 - i

    def body(q_ref, k_ref, v_ref, g_ref, cos_ref, sin_ref, gain_ref, dm_ref, qd_ref, kd_ref, lam_ref,
             o_ref, st_ref, dret_ref, dz_ref, dgain_ref, g_scr, qr_scr, kr_scr, do_scr, dq_scr, dk_scr):
        @pl.when(pl.program_id(0) == 0)
        def _():
            g_scr[...] = jnp.zeros_like(g_scr)
            dgain_ref[...] = jnp.zeros_like(dgain_ref)

        cosv, sinv = cos_ref[...], sin_ref[...]
        for h in range(RET_HEADS):
            hs = _head(h)
            qh, kh = q_ref[:, hs], k_ref[:, hs]
            qr_scr[:, hs] = qh * cosv + _swap_pairs(qh) * sinv
            kr_scr[:, hs] = (kh * cosv + _swap_pairs(kh) * sinv) * scale
            o = o_ref[:, hs]
            mu = jnp.mean(o, axis=-1, keepdims=True)
            oc = o - mu
            rstd = lax.rsqrt(jnp.mean(oc * oc, axis=-1, keepdims=True) + GN_EPS)
            yh = oc * rstd
            gain = gain_ref[:, hs]
            g = g_ref[:, hs]
            sg = 1.0 / (1.0 + jnp.exp(-g))
            dret = dret_ref[:, hs].astype(F32)
            dy = dret * (g * sg)
            dz_ref[:, 3 * RET_WIDTH + h * RET_HEAD_DIM:3 * RET_WIDTH + (h + 1) * RET_HEAD_DIM] = _bf(
                dret * (yh * gain) * (sg * (1.0 + g * (1.0 - sg))))
            dgain_ref[:, hs] += jnp.sum(dy * yh, axis=0, keepdims=True)
            dyh = dy * gain
            do_scr[:, hs] = rstd * (dyh - jnp.mean(dyh, axis=-1, keepdims=True)
                                    - yh * jnp.mean(dyh * yh, axis=-1, keepdims=True))

        def chunk(cc, carry):
            c = ncb - 1 - cc
            rows = pl.ds(pl.multiple_of(c * CHUNK, CHUNK), CHUNK)
            for h in range(RET_HEADS):
                hs = _head(h)
                qc, kc, vc, doc = _bf(qr_scr[rows, hs]), _bf(kr_scr[rows, hs]), _bf(v_ref[rows, hs]), _bf(do_scr[rows, hs])
                qdc, kdc = qd_ref[h], kd_ref[h]
                st, gs = _bf(st_ref[c, h]), g_scr[h]
                gsb = _bf(gs)
                dm = dm_ref[h]
                p = _bf(_dot(qc, kc, NT) * dm)
                da = _bf(_dot(doc, vc, NT) * dm)
                kt = _bf(kr_scr[rows, hs] * kdc)
                qt = _bf(qr_scr[rows, hs] * qdc)
                dz_ref[rows, 2 * RET_WIDTH + h * RET_HEAD_DIM:2 * RET_WIDTH + (h + 1) * RET_HEAD_DIM] = _bf(
                    _dot(p, doc, TN) + _dot(kt, gsb))
                dq_scr[rows, hs] = _dot(da, kc) + _dot(doc, st, NT) * qdc
                dk_scr[rows, hs] = _dot(da, qc, TN) + _dot(vc, gsb, NT) * kdc
                g_scr[h] = gs * lam_ref[h] + _dot(qt, doc, TN)
            return carry

        lax.fori_loop(0, ncb, chunk, 0)
        for h in range(RET_HEADS):
            hs = _head(h)
            dq, dk = dq_scr[:, hs], dk_scr[:, hs]
            dz_ref[:, h * RET_HEAD_DIM:(h + 1) * RET_HEAD_DIM] = _bf(dq * cosv - _swap_pairs(dq) * sinv)
            dz_ref[:, RET_WIDTH + h * RET_HEAD_DIM:RET_WIDTH + (h + 1) * RET_HEAD_DIM] = _bf(
                (dk * cosv - _swap_pairs(dk) * sinv) * scale)

    return pl.pallas_call(
        body, name=name, grid=(nblk,),
        out_shape=[SDS((s, 4 * RET_WIDTH), BF16), SDS((1, RET_WIDTH), F32)],
        in_specs=_ret_common_specs(tb, rev)
        + [pl.BlockSpec((tb, RET_WIDTH), lambda i: (rev(i), 0)),
           pl.BlockSpec((ncb, RET_HEADS, RET_HEAD_DIM, RET_HEAD_DIM), lambda i: (rev(i), 0, 0, 0)),
           pl.BlockSpec((tb, RET_WIDTH), lambda i: (rev(i), 0))],
        out_specs=[pl.BlockSpec((tb, 4 * RET_WIDTH), lambda i: (rev(i), 0)),
                   pl.BlockSpec((1, RET_WIDTH), lambda i: (0, 0))],
        scratch_shapes=[pltpu.VMEM((RET_HEADS, RET_HEAD_DIM, RET_HEAD_DIM), F32)]
        + [pltpu.VMEM((tb, RET_WIDTH), F32)] * 5,
        compiler_params=_params(("arbitrary",)),
    )(z, z, z, z, tabs["cos"], tabs["sin"], gn_gain, tabs["dmat"], tabs["qdec"], tabs["kdec"], tabs["lam"],
      o_pre, states, du)


POOL_COL = 4 * RET_WIDTH // POOL_WIDTH


def _pooled(cur, prev, t0):
    tm = cur.shape[0]
    xx = jnp.concatenate([prev, cur], axis=0)
    sums = {1: xx}
    w = 1
    while w < POOL_WINDOWS[-1]:
        sums[2 * w] = sums[w] + pltpu.roll(sums[w], w, 0)
        w *= 2
    t = t0 + lax.broadcasted_iota(jnp.int32, (tm, 128), 0)
    outs = []
    for gi, w in enumerate(POOL_WINDOWS):
        cols = slice(gi * 128, (gi + 1) * 128)
        cnt = jnp.minimum(t + 1, w).astype(F32)
        outs.append(sums[w][POOL_HALO:, cols] / cnt - cur[:, cols])
    return outs


def _pool_fwd(z, w_pool, scale, name):
    s = z.shape[0]
    tm = _tile(s, TOKEN_TILE)
    hb = tm // POOL_HALO

    def body(p_ref, prev_ref, w_ref, sc_ref, o_ref):
        i = pl.program_id(0)
        prev = jnp.where(i > 0, prev_ref[...], 0.0)
        pooled = _pooled(p_ref[...], prev, i * tm)
        for gi in range(len(POOL_WINDOWS)):
            cols = slice(gi * 128, (gi + 1) * 128)
            o_ref[:, cols] = _bf(_dot(_bf(pooled[gi]), _bf(w_ref[gi])) * sc_ref[:, cols])

    return pl.pallas_call(
        body, name=name, grid=(s // tm,), out_shape=SDS((s, POOL_WIDTH), BF16),
        in_specs=[pl.BlockSpec((tm, POOL_WIDTH), lambda i: (i, POOL_COL)),
                  pl.BlockSpec((POOL_HALO, POOL_WIDTH), lambda i: (jnp.maximum(i * hb - 1, 0), POOL_COL)),
                  pl.BlockSpec(w_pool.shape, lambda i: (0, 0, 0)),
                  pl.BlockSpec((1, POOL_WIDTH), lambda i: (0, 0))],
        out_specs=pl.BlockSpec((tm, POOL_WIDTH), lambda i: (i, 0)),
        compiler_params=_params(("parallel",)),
    )(z, z, w_pool, scale)


def _pool_bwd(z, w_pool, scale, du, name):
    s = z.shape[0]
    tm = _tile(s, TOKEN_TILE)
    hb = tm // POOL_HALO
    nblk = s // tm
    last_halo = s // POOL_HALO - 1

    def body(p_ref, prev_ref, w_ref, sc_ref, do_ref, don_ref, dp_ref, dw_ref, dsc_ref):
        i = pl.program_id(0)

        @pl.when(i == 0)
        def _():
            dw_ref[...] = jnp.zeros_like(dw_ref)
            dsc_ref[...] = jnp.zeros_like(dsc_ref)

        prev = jnp.where(i > 0, prev_ref[...], 0.0)
        pooled = _pooled(p_ref[...], prev, i * tm)
        dout = do_ref[...].astype(F32)
        dout_next = jnp.where(i < nblk - 1, don_ref[...].astype(F32), 0.0)
        sc = sc_ref[...]
        dmix = jnp.concatenate([dout * sc, dout_next * sc], axis=0)
        n = tm + POOL_HALO
        t = i * tm + lax.broadcasted_iota(jnp.int32, (n, 128), 0)
        for gi, w in enumerate(POOL_WINDOWS):
            cols = slice(gi * 128, (gi + 1) * 128)
            wg = _bf(w_ref[gi])
            pg = _bf(pooled[gi])
            dsc_ref[:, cols] += jnp.sum(dout[:, cols] * _dot(pg, wg), axis=0, keepdims=True)
            dw_ref[gi] += _dot(pg, _bf(dmix[:tm, cols]), TN)
            dpool = _dot(_bf(dmix[:, cols]), wg, NT)
            acc = dpool / jnp.minimum(t + 1, w).astype(F32)
            step = 1
            while step < w:
                acc = acc + pltpu.roll(acc, n - step, 0)
                step *= 2
            dp_ref[:, cols] = _bf(acc[:tm] - dpool[:tm])

    return pl.pallas_call(
        body, name=name, grid=(nblk,),
        out_shape=[SDS((s, POOL_WIDTH), BF16), SDS(w_pool.shape, F32), SDS((1, POOL_WIDTH), F32)],
        in_specs=[pl.BlockSpec((tm, POOL_WIDTH), lambda i: (i, POOL_COL)),
                  pl.BlockSpec((POOL_HALO, POOL_WIDTH), lambda i: (jnp.maximum(i * hb - 1, 0), POOL_COL)),
                  pl.BlockSpec(w_pool.shape, lambda i: (0, 0, 0)),
                  pl.BlockSpec((1, POOL_WIDTH), lambda i: (0, 0)),
                  pl.BlockSpec((tm, POOL_WIDTH), lambda i: (i, 1)),
                  pl.BlockSpec((POOL_HALO, POOL_WIDTH), lambda i: (jnp.minimum((i + 1) * hb, last_halo), 1))],
        out_specs=[pl.BlockSpec((tm, POOL_WIDTH), lambda i: (i, 0)),
                   pl.BlockSpec(w_pool.shape, lambda i: (0, 0, 0)),
                   pl.BlockSpec((1, POOL_WIDTH), lambda i: (0, 0))],
        compiler_params=_params(("arbitrary",)),
    )(z, z, w_pool, scale, du, du)


def _rel_onehot():
    r = lax.broadcasted_iota(jnp.int32, (REL_PAD, ATT_DIAG), 0)
    c = lax.broadcasted_iota(jnp.int32, (REL_PAD, ATT_DIAG), 1)
    rel = jnp.where(c < ATT_K_TILE, jnp.clip(LEFT_CHUNKS * CHUNK - c, -REL_CLIP, REL_CLIP) + REL_CLIP,
                    2 * REL_CLIP)
    return (rel == r).astype(BF16)


def _split3(v):
    hi = _bf(v)
    r1 = v - hi.astype(F32)
    mid = _bf(r1)
    return hi, mid, _bf(r1 - mid.astype(F32))


def _skew(v, sign):
    row = lax.broadcasted_iota(jnp.int32, v.shape, 0)
    bit = 1
    while bit < ATT_Q_TILE:
        shift = bit if sign > 0 else ATT_DIAG - bit
        v = jnp.where((row & bit) != 0, pltpu.roll(v, shift, 1), v)
        bit *= 2
    return v


def _attn_bias(rel_bias, name):
    def body(t_ref, o_ref):
        oh = _rel_onehot()
        base = sum(_dot(part, oh) for part in _split3(t_ref[0]))
        full = _skew(jnp.broadcast_to(base[0:1], (ATT_Q_TILE, ATT_DIAG)), +1)[:, :ATT_K_TILE]
        qc = lax.broadcasted_iota(jnp.int32, full.shape, 0) // CHUNK
        kc = lax.broadcasted_iota(jnp.int32, full.shape, 1) // CHUNK
        o_ref[0] = jnp.where((kc >= qc) & (kc <= qc + LEFT_CHUNKS), full, NEG_INF)

    t8 = jnp.broadcast_to(rel_bias[:, None, :], (ATT_HEADS, 8, REL_PAD))
    return pl.pallas_call(
        body, name=name, grid=(ATT_HEADS,), out_shape=SDS((ATT_HEADS, ATT_Q_TILE, ATT_K_TILE), F32),
        in_specs=[pl.BlockSpec((1, 8, REL_PAD), lambda h: (h, 0, 0))],
        out_specs=pl.BlockSpec((1, ATT_Q_TILE, ATT_K_TILE), lambda h: (h, 0, 0)),
        compiler_params=_params(("parallel",)),
    )(t8)


def _attn_dbias(dbias, name):
    def body(d_ref, o_ref):
        pad = jnp.zeros((ATT_Q_TILE, ATT_DIAG - ATT_K_TILE), F32)
        diag = _skew(jnp.concatenate([d_ref[0], pad], axis=1), -1)
        col = jnp.sum(diag, axis=0, keepdims=True)
        oh = _rel_onehot()
        col8 = jnp.broadcast_to(col, (8, ATT_DIAG))
        o_ref[0] = sum(_dot(part, oh, NT) for part in _split3(col8))

    out = pl.pallas_call(
        body, name=name, grid=(ATT_HEADS,), out_shape=SDS((ATT_HEADS, 8, REL_PAD), F32),
        in_specs=[pl.BlockSpec((1, ATT_Q_TILE, ATT_K_TILE), lambda h: (h, 0, 0))],
        out_specs=pl.BlockSpec((1, 8, REL_PAD), lambda h: (h, 0, 0)),
        compiler_params=_params(("parallel",)),
    )(dbias)
    return out[:, 0, :]


ATT_WIDTH = 128 * ATT_PAIRS
ATT_GROUPS = D_MODEL // ATT_WIDTH


def _attn_specs(nq):
    def tile(off, back):
        return pl.BlockSpec((ATT_Q_TILE, ATT_WIDTH),
                            lambda g, i: (jnp.maximum(jnp.minimum(i, nq - 1) - back, 0), off + g))

    backs = [ATT_BACK - b for b in range(ATT_BACK + 1)]
    return ([tile(0, 0)] + [tile(ATT_GROUPS, b) for b in backs] + [tile(2 * ATT_GROUPS, b) for b in backs]
            + [pl.BlockSpec((2 * ATT_PAIRS, ATT_Q_TILE, ATT_K_TILE), lambda g, i: (g, 0, 0))])


def _attn_weights(qh, k2, bias, i, masked):
    sc = _dot(qh, k2, NT) + bias
    if masked:
        kpos = (i - ATT_BACK) * ATT_Q_TILE + lax.broadcasted_iota(jnp.int32, sc.shape, 1)
        sc = jnp.where(kpos >= 0, sc, NEG_INF)
    e = jnp.exp(sc - jnp.max(sc, axis=-1, keepdims=True))
    return e, 1.0 / jnp.sum(e, axis=-1, keepdims=True)


def _first_head():
    return lax.broadcasted_iota(jnp.int32, (ATT_Q_TILE, 128), 1) < ATT_HEAD_DIM


def _pair_operands(q_ref, k_refs, v_refs, pp):
    cols = slice(pp * 128, (pp + 1) * 128)
    q2 = q_ref[:, cols] * ATT_HEAD_DIM ** -0.5
    k2 = jnp.concatenate([r[:, cols] for r in k_refs], axis=0)
    v2 = jnp.concatenate([r[:, cols] for r in v_refs], axis=0)
    return cols, q2, k2, v2


def _attn_fwd(z, bias, name):
    s = z.shape[0]
    nq = s // ATT_Q_TILE
    nt = ATT_BACK + 1

    def body(q_ref, *rest):
        k_refs, v_refs, (b_ref, o_ref) = rest[:nt], rest[nt:2 * nt], rest[2 * nt:]
        i = pl.program_id(1)
        first = _first_head()

        def compute(masked):
            for pp in range(ATT_PAIRS):
                cols, q2, k2, v2 = _pair_operands(q_ref, k_refs, v_refs, pp)
                outs = []
                for hh in range(2):
                    qh = jnp.where(first if hh == 0 else ~first, q2, 0)
                    e, inv = _attn_weights(qh, k2, b_ref[2 * pp + hh], i, masked)
                    outs.append(_dot(_bf(e), v2) * inv)
                o_ref[:, cols] = _bf(jnp.where(first, outs[0], outs[1]))

        pl.when(i < ATT_BACK)(lambda: compute(True))
        pl.when(i >= ATT_BACK)(lambda: compute(False))

    return pl.pallas_call(
        body, name=name, grid=(ATT_GROUPS, nq), out_shape=SDS((s, D_MODEL), BF16),
        in_specs=_attn_specs(nq),
        out_specs=pl.BlockSpec((ATT_Q_TILE, ATT_WIDTH), lambda g, i: (i, g)),
        compiler_params=_params(("parallel", "parallel")),
    )(*([z] * (1 + 2 * nt)), bias)


def _attn_bwd(z, bias, o, do, name):
    s = z.shape[0]
    nq = s // ATT_Q_TILE
    nt = ATT_BACK + 1

    def body(q_ref, *rest):
        k_refs, v_refs = rest[:nt], rest[nt:2 * nt]
        b_ref, o_ref, do_ref, dq_ref, dk_ref, dv_ref, db_ref, dk_acc, dv_acc = rest[2 * nt:]
        i = pl.program_id(1)
        first = _first_head()

        @pl.when(i == 0)
        def _():
            db_ref[...] = jnp.zeros_like(db_ref)
            dk_acc[...] = jnp.zeros_like(dk_acc)
            dv_acc[...] = jnp.zeros_like(dv_acc)

        def compute(masked):
            for pp in range(ATT_PAIRS):
                cols, q2, k2, v2 = _pair_operands(q_ref, k_refs, v_refs, pp)
                do2 = do_ref[:, cols].astype(F32)
                prod = do2 * o_ref[:, cols].astype(F32)
                dqs, dk, dv = [], None, None
                for hh in range(2):
                    mine = first if hh == 0 else ~first
                    qh = jnp.where(mine, q2, 0)
                    e, inv = _attn_weights(qh, k2, b_ref[2 * pp + hh], i, masked)
                    delta = jnp.sum(jnp.where(mine, prod, 0.0), axis=-1, keepdims=True) * inv
                    doh = _bf(jnp.where(mine, do2 * inv, 0.0))
                    ds = e * (_dot(doh, v2, NT) - delta)
                    db_ref[2 * pp + hh] += ds
                    dsb = _bf(ds)
                    dqs.append(_dot(dsb, k2))
                    dkh, dvh = _dot(dsb, qh, TN), _dot(_bf(e), doh, TN)
                    dk, dv = (dkh, dvh) if hh == 0 else (dk + dkh, dv + dvh)
                dq_ref[:, cols] = _bf(jnp.where(first, dqs[0], dqs[1]) * ATT_HEAD_DIM ** -0.5)
                for b in range(nt):
                    slot = (i + b + 1) % nt
                    rows = slice(b * ATT_Q_TILE, (b + 1) * ATT_Q_TILE)
                    if b < ATT_BACK:
                        dk_acc[slot, :, cols] += dk[rows]
                        dv_acc[slot, :, cols] += dv[rows]
                    else:
                        dk_acc[slot, :, cols] = dk[rows]
                        dv_acc[slot, :, cols] = dv[rows]

        pl.when(i < ATT_BACK)(lambda: compute(True))
        pl.when((i >= ATT_BACK) & (i < nq))(lambda: compute(False))
        done = (i + 1) % nt
        dk_ref[...] = _bf(dk_acc[done])
        dv_ref[...] = _bf(dv_acc[done])

    tile = pl.BlockSpec((ATT_Q_TILE, ATT_WIDTH), lambda g, i: (jnp.minimum(i, nq - 1), g))
    late = pl.BlockSpec((ATT_Q_TILE, ATT_WIDTH), lambda g, i: (jnp.maximum(i - ATT_BACK, 0), g))
    ring = pltpu.VMEM((nt, ATT_Q_TILE, ATT_WIDTH), F32)
    return pl.pallas_call(
        body, name=name, grid=(ATT_GROUPS, nq + ATT_BACK),
        out_shape=[SDS((s, D_MODEL), BF16)] * 3 + [SDS((ATT_HEADS, ATT_Q_TILE, ATT_K_TILE), F32)],
        in_specs=_attn_specs(nq) + [tile, tile],
        out_specs=[tile, late, late, pl.BlockSpec((2 * ATT_PAIRS, ATT_Q_TILE, ATT_K_TILE), lambda g, i: (g, 0, 0))],
        scratch_shapes=[ring, ring],
        compiler_params=_params(("parallel", "arbitrary")),
    )(*([z] * (1 + 2 * nt)), bias, o, do)


FWD_GROUPS = (
    (("ab_w_in", 0),),
    (("ab_w_out", 0), ("w_ffn_in", 0), ("w_ffn_out", 0)),
    (("c_w_qkv", 0), ("c_w_out", 0), ("w_ffn_in", 1), ("w_ffn_out", 1)),
    (("ab_w_in", 1), ("ab_w_out", 1), ("w_ffn_in", 2), ("w_ffn_out", 2)),
    (("c_w_qkv", 1), ("c_w_out", 1), ("w_ffn_in", 3), ("w_ffn_out", 3)),
)


def _local_step(x, target, small, comm):
    s = x.shape[0]
    tabs = _retention_tables(s)
    saved, w = [], {}
    for layer in range(DEPTH):
        i = layer // 2
        sv = {"x0": x}
        g_mix = small["mix_norm"][layer:layer + 1]
        if layer % 2 == 0:
            if layer == 0:
                w.update(comm.weights(0, x))
            sv["h1"], sv["z"] = _norm_mm(x, g_mix, w["ab_w_in", i], AB_IN_WIDTH, F32, False, "ab_in_fwd")
            gn = small["ab_gn_gain"][i:i + 1]
            sv["o_pre"], sv["states"], ret = _ret_fwd(sv["z"], tabs, gn, "ret_fwd")
            pool = _pool_fwd(sv["z"], small["ab_w_pool"][i], small["ab_pool_scale"][i:i + 1], "pool_fwd")
            sv["u"] = (ret, pool)
            if layer == 0:
                w.update(comm.weights(1, pool))
            x = _mm_res([ret, pool], w["ab_w_out", i], x, "ab_out_fwd")
        else:
            sv["h1"], sv["z"] = _norm_mm(x, g_mix, w["c_w_qkv", i], 3 * D_MODEL // N_DEV, BF16, False, "qkv_fwd")
            rb = jnp.pad(small["c_rel_bias"][i], ((0, 0), (0, REL_PAD - N_REL)))
            sv["bias"] = _attn_bias(rb, "attn_bias")
            sv["o"] = _attn_fwd(sv["z"], sv["bias"], "attn_fwd")
            x = _mm_res([sv["o"]], w["c_w_out", i], x, "c_out_fwd")
        sv["x1"] = x
        sv["h2"], sv["z1"], sv["a"] = _norm_mm(x, small["ffn_norm"][layer:layer + 1], w["w_ffn_in", layer],
                                               D_FF // N_DEV, BF16, True, "ffn_in_fwd")
        x = _mm_res([sv["a"]], w["w_ffn_out", layer], x, "ffn_out_fwd")
        saved.append(sv)
        comm.layer_done(layer, x)
        if layer + 1 < DEPTH:
            w.update(comm.weights(layer + 2, x))

    loss, dx, d_final = _final_loss(x, small["final_norm"][None, :], target, "final_loss")

    gs = {k: [None] * DEPTH for k in ("mix_norm", "ffn_norm")}
    for k in ("ab_gn_gain", "ab_w_pool", "ab_pool_scale", "c_rel_bias"):
        gs[k] = [None] * (DEPTH // 2)
    gs["final_norm"] = d_final[0]
    tok = jnp.zeros((), F32)
    for layer in reversed(range(DEPTH)):
        i = layer // 2
        sv = saved[layer]
        dz1 = _mm_nt_rows(dx, w["w_ffn_out", layer], sv["z1"], "ffn_out_bwd")
        gw = {("w_ffn_out", layer): _mm_tn(sv["a"], dx, 1024, D_MODEL, True, (0, 2), "ffn_out_dw"),
              ("w_ffn_in", layer): _mm_tn(sv["h2"], dz1, D_MODEL, 1024, False, (1, 2), "ffn_in_dw")}
        dx, dg = _mm_nt_normbwd(dz1, w["w_ffn_in", layer], sv["x1"], small["ffn_norm"][layer:layer + 1] + tok, dx,
                                "ffn_in_bwd")
        gs["ffn_norm"][layer] = dg[0]
        if layer == 0:
            tok = comm.send(gw, dx)
            gw = {}
        g_mix = small["mix_norm"][layer:layer + 1] + tok
        if layer % 2 == 0:
            du = _mm_nt_rows(dx, w["ab_w_out", i], None, "mix_out_bwd")
            ret, pool = sv["u"]
            u = jnp.concatenate([ret, pool], axis=1)
            gw["ab_w_out", i] = _mm_tn(u, dx, D_MODEL, D_MODEL, True, (0, N_DEV), "mix_out_dw")
            gn = small["ab_gn_gain"][i:i + 1]
            dz_ret, dgn = _ret_bwd(sv["z"], tabs, gn, sv["o_pre"], sv["states"], du, "ret_bwd")
            dp, dwp, dsc = _pool_bwd(sv["z"], small["ab_w_pool"][i], small["ab_pool_scale"][i:i + 1], du, "pool_bwd")
            gs["ab_gn_gain"][i], gs["ab_w_pool"][i], gs["ab_pool_scale"][i] = dgn[0], dwp, dsc[0]
            dz = jnp.concatenate([dz_ret, dp], axis=1)
            gw["ab_w_in", i] = _to_shard_major(_mm_tn(sv["h1"], dz, D_MODEL, AB_IN_WIDTH // 2, False, (1, 1), "ab_in_dw"))
            dx, dg = _mm_nt_normbwd(dz, w["ab_w_in", i], sv["x0"], g_mix, dx, "ab_in_bwd")
        else:
            do = _mm_nt_rows(dx, w["c_w_out", i], None, "mix_out_bwd")
            gw["c_w_out", i] = _mm_tn(sv["o"], dx, D_MODEL, D_MODEL, True, (0, N_DEV), "mix_out_dw")
            dq, dk, dv, dbias = _attn_bwd(sv["z"], sv["bias"], sv["o"], do, "attn_bwd")
            gs["c_rel_bias"][i] = _attn_dbias(dbias, "attn_dbias")[:, :N_REL]
            dz = jnp.concatenate([dq, dk, dv], axis=1)
            gw["c_w_qkv", i] = _mm_tn(sv["h1"], dz, D_MODEL, 768, False, (1, 2), "qkv_dw")
            dx, dg = _mm_nt_normbwd(dz, w["c_w_qkv", i], sv["x0"], g_mix, dx, "qkv_bwd")
        gs["mix_norm"][layer] = dg[0]
        if layer > 0:
            tok = comm.send(gw, dx)
    gsmall = {k: (jnp.stack(v) if isinstance(v, list) else v) for k, v in gs.items()}
    return loss, dx, gw, gsmall


BIG = ("w_ffn_in", "w_ffn_out", "ab_w_in", "ab_w_out", "c_w_qkv", "c_w_out")
SMALL = ("mix_norm", "ffn_norm", "ab_gn_gain", "ab_w_pool", "ab_pool_scale", "c_rel_bias", "final_norm")
HBM = pl.BlockSpec(memory_space=pltpu.HBM)
SEM = pl.BlockSpec(memory_space=pltpu.SEMAPHORE)
ANY = pl.BlockSpec(memory_space=pl.ANY)
N_PEERS = N_DEV - 1
FLIPS = [(fx, fy, fc) for fx in (0, 1) for fy in (0, 1) for fc in (0, 1)][1:]


def _peers():
    x, y, c = (lax.axis_index(a) for a in MESH_AXES)
    peers = []
    for fx, fy, fc in FLIPS:
        px, py, pc = (1 - x if fx else x), (1 - y if fy else y), (1 - c if fc else c)
        peers.append(((px, py, pc), 4 * px + 2 * py + pc))
    return 4 * x + 2 * y + c, peers


def _remote(src, dst, sems, j, rel, pos):
    n = j * N_PEERS + rel
    return pltpu.make_async_remote_copy(src_ref=src, dst_ref=dst, send_sem=sems[0].at[n], recv_sem=sems[1].at[n],
                                        device_id=pos, device_id_type=pl.DeviceIdType.MESH)


def _send_start(srcs, by_slot, groups, after, name):
    n, ng = len(srcs), len(groups)
    lands = [lax.empty((N_DEV,) + (a.shape[1:] if by_slot else a.shape), a.dtype) for a in srcs]

    def body(*refs):
        src_refs, land_refs = refs[:n], refs[n:2 * n]
        sem_refs = refs[2 * n + 1:2 * n + 1 + 2 * ng]
        token, local_sems = refs[4 * n + 1 + 2 * ng], refs[4 * n + 2 + 2 * ng]
        me, peers = _peers()
        local = []
        for k in range(n):
            own = pltpu.make_async_copy(src_refs[k].at[me] if by_slot else src_refs[k], land_refs[k].at[me],
                                        local_sems.at[k])
            own.start()
            local.append(own)
        for own in local:
            own.wait()
        for g, members in enumerate(groups):
            for j, k in enumerate(members):
                for rel, (pos, slot) in enumerate(peers):
                    _remote(src_refs[k].at[slot] if by_slot else src_refs[k], land_refs[k].at[me],
                            sem_refs[2 * g:2 * g + 2], j, rel, pos).start()
        token[...] = jnp.zeros_like(token)

    out_shape = []
    for members in groups:
        out_shape += [pltpu.SemaphoreType.DMA((len(members) * N_PEERS,))] * 2
    out_shape += [pltpu.HBM(a.shape, a.dtype) for a in list(srcs) + lands] + [SDS((8, 128), F32)]
    outs = pl.pallas_call(
        body, name=name, out_shape=out_shape,
        in_specs=[HBM] * (2 * n) + [ANY],
        out_specs=[SEM] * (2 * ng) + [HBM] * (2 * n) + [pl.BlockSpec(memory_space=pltpu.VMEM)],
        input_output_aliases={i: 2 * ng + i for i in range(2 * n)},
        scratch_shapes=[pltpu.SemaphoreType.DMA((n,))],
        compiler_params=pltpu.CompilerParams(has_side_effects=pltpu.SideEffectType.DATAFLOW_SIDE_EFFECTING),
    )(*[pltpu.with_memory_space_constraint(a, pltpu.HBM) for a in list(srcs) + lands], after)
    sems = [tuple(outs[2 * g:2 * g + 2]) for g in range(ng)]
    return sems, outs[2 * ng:2 * ng + n], outs[2 * ng + n:2 * ng + 2 * n], outs[-1][0, 0]


def _send_wait(sems, srcs, lands, by_slot, after, name):
    n = len(srcs)

    def body(*refs):
        src_refs, land_refs, sem_refs = refs[:n], refs[n:2 * n], refs[2 * n:2 * n + 2]
        _, peers = _peers()
        for j in range(n):
            for rel, (pos, slot) in enumerate(peers):
                cp = _remote(src_refs[j].at[slot] if by_slot else src_refs[j], land_refs[j].at[slot], sem_refs, j, rel, pos)
                cp.wait_send()
                cp.wait_recv()

    outs = pl.pallas_call(
        body, name=name, out_shape=[pltpu.HBM(a.shape, a.dtype) for a in list(srcs) + list(lands)],
        in_specs=[HBM] * (2 * n) + [SEM, SEM, ANY], out_specs=[HBM] * (2 * n),
        input_output_aliases={i: i for i in range(2 * n)},
        compiler_params=pltpu.CompilerParams(has_side_effects=pltpu.SideEffectType.DATAFLOW_SIDE_EFFECTING),
    )(*srcs, *lands, *sems, after)
    return outs[n:]


def _cast_bf16(w, name):
    nl, r, c = w.shape
    tr = _tile(r, 512)

    def body(i_ref, *o_refs):
        for l in range(nl):
            o_refs[l][...] = _bf(i_ref[l])

    return pl.pallas_call(
        body, name=name, grid=(r // tr,), out_shape=[SDS((r, c), BF16)] * nl,
        in_specs=[pl.BlockSpec((nl, tr, c), lambda i: (0, i, 0))],
        out_specs=[pl.BlockSpec((tr, c), lambda i: (i, 0))] * nl,
        compiler_params=_params(("parallel",)),
    )(w)


def _to_shard_major(g):
    nj, ka, nb = g.shape
    full = jnp.transpose(g, (1, 0, 2)).reshape(ka, N_DEV, nj * nb // N_DEV)
    return jnp.transpose(full, (1, 0, 2))


def _from_gathered(name, g):
    if name in ("w_ffn_out", "ab_w_out", "c_w_out"):
        return g.reshape(g.shape[0] * g.shape[1], g.shape[2])
    if name == "ab_w_in":
        return jnp.transpose(g, (1, 0, 2)).reshape(1, g.shape[1], N_DEV * g.shape[2])
    return g


class _Comm:
    def __init__(self, shards):
        self.shards = shards
        self.pending = {}
        self.sent = []

    def _start_gather(self, group_ids, after):
        keys = [FWD_GROUPS[g] for g in group_ids]
        flat = [k for ks in keys for k in ks]
        index, groups = 0, []
        for ks in keys:
            groups.append(list(range(index, index + len(ks))))
            index += len(ks)
        sems, srcs, lands, _ = _send_start([self.shards[k] for k in flat], False, groups, after,
                                           "gather_start_%d" % group_ids[0])
        for g, members, sem in zip(group_ids, groups, sems):
            self.pending[g] = (sem, [srcs[k] for k in members], [lands[k] for k in members])

    def weights(self, g, after):
        if g == 0:
            self._start_gather((0, 1, 2), after)
        sem, srcs, lands = self.pending.pop(g)
        got = _send_wait(sem, srcs, lands, False, after, "gather_wait_%d" % g)
        return {key: _from_gathered(key[0], arr) for key, arr in zip(FWD_GROUPS[g], got)}

    def layer_done(self, layer, x):
        if layer == 0:
            self._start_gather((3, 4), x)

    def send(self, grads, after):
        keys = list(grads)
        sems, srcs, lands, tok = _send_start([grads[k] for k in keys], True, [list(range(len(keys)))], after,
                                             "grad_start_%d" % len(self.sent))
        self.sent.append([keys, sems[0], srcs, lands, True])
        return tok

    def send_small(self, packed, after):
        sems, srcs, lands, _ = _send_start([packed], False, [[0]], after, "small_start")
        self.sent.append([["small"], sems[0], srcs, lands, False])

    def received(self, after):
        got = {}
        for n, (keys, sem, srcs, lands, by_slot) in enumerate(self.sent):
            for key, arr in zip(keys, _send_wait(sem, srcs, lands, by_slot, after, "grad_wait_%d" % n)):
                got[key] = arr
        return got


def _adamw_math(g, w, m, v):
    m2 = ADAM_B1 * m + (1.0 - ADAM_B1) * g
    v2 = ADAM_B2 * v + (1.0 - ADAM_B2) * jnp.square(g)
    m_hat = m2 / (1.0 - ADAM_B1 ** ADAM_STEP)
    v_hat = v2 / (1.0 - ADAM_B2 ** ADAM_STEP)
    delta = -ADAM_LR * (m_hat / (jnp.sqrt(v_hat) + ADAM_EPS) + ADAM_WD * w)
    return delta, m2, v2


def _adamw(recv, w, m, v, name):
    nl, r, c = w.shape
    tr = _tile(r, 256)

    def body(*refs):
        g_refs = refs[:nl]
        w_ref, m_ref, v_ref, go_ref, d_ref, mo_ref, vo_ref = refs[nl:]
        for l in range(nl):
            @pl.when(pl.program_id(0) == l)
            def _():
                g = g_refs[l][0].astype(F32)
                for p in range(1, N_DEV):
                    g = g + g_refs[l][p].astype(F32)
                go_ref[...] = g
                d_ref[...], mo_ref[...], vo_ref[...] = _adamw_math(g, w_ref[...], m_ref[...], v_ref[...])

    def recv_spec(l):
        return pl.BlockSpec((N_DEV, tr, c), lambda layer, i: (0, jnp.where(layer == l, i, 0), 0))

    blk = pl.BlockSpec((None, tr, c), lambda l, i: (l, i, 0))
    return pl.pallas_call(
        body, name=name, grid=(nl, r // tr), out_shape=[SDS(w.shape, F32)] * 4,
        in_specs=[recv_spec(l) for l in range(nl)] + [blk, blk, blk],
        out_specs=[blk] * 4,
        compiler_params=_params(("arbitrary", "arbitrary")),
    )(*recv, w, m, v)


def _adamw_small(recv, w, m, v, name):
    def body(g_ref, w_ref, m_ref, v_ref, go_ref, d_ref, mo_ref, vo_ref):
        g = g_ref[0]
        for p in range(1, N_DEV):
            g = g + g_ref[p]
        go_ref[...] = g
        d_ref[...], mo_ref[...], vo_ref[...] = _adamw_math(g, w_ref[...], m_ref[...], v_ref[...])

    return pl.pallas_call(body, name=name, out_shape=[SDS(w.shape, F32)] * 4,
                          compiler_params=_params(None))(recv, w, m, v)


def _pack_small(tree):
    parts = []
    for k in SMALL:
        flat = tree[k].reshape(-1)
        rows = -(-flat.shape[0] // 1024) * 8
        parts.append(jnp.pad(flat, (0, rows * 128 - flat.shape[0])).reshape(rows, 128))
    return jnp.concatenate(parts, axis=0)


def _unpack_small(packed, like):
    out, row = {}, 0
    for k in SMALL:
        size = like[k].size
        rows = -(-size // 1024) * 8
        out[k] = packed[row:row + rows].reshape(-1)[:size].reshape(like[k].shape)
        row += rows
    return out


def kernel(x, mix_norm, ffn_norm, w_ffn_in, w_ffn_out, ab_w_in, ab_gn_gain, ab_w_pool, ab_pool_scale, ab_w_out, c_w_qkv, c_rel_bias, c_w_out, final_norm, loss_target, m_mix_norm, m_ffn_norm, m_w_ffn_in, m_w_ffn_out, m_ab_w_in, m_ab_gn_gain, m_ab_w_pool, m_ab_pool_scale, m_ab_w_out, m_c_w_qkv, m_c_rel_bias, m_c_w_out, m_final_norm, v_mix_norm, v_ffn_norm, v_w_ffn_in, v_w_ffn_out, v_ab_w_in, v_ab_gn_gain, v_ab_w_pool, v_ab_pool_scale, v_ab_w_out, v_c_w_qkv, v_c_rel_bias, v_c_w_out, v_final_norm):
    args = dict(locals())
    weights = {k: args[k] for k in BIG + SMALL}
    moments_m = {k: args["m_" + k] for k in BIG + SMALL}
    moments_v = {k: args["v_" + k] for k in BIG + SMALL}

    small = {k: weights[k] for k in SMALL}

    shards = {}
    for k in BIG:
        for l, sh in enumerate(_cast_bf16(weights[k], "cast_" + k)):
            shards[k, l] = sh
    comm = _Comm(shards)
    loss, dx, last_grads, gsmall = _local_step(x[0], loss_target[0], small, comm)
    comm.send(last_grads, dx)
    comm.send_small(_pack_small(gsmall), dx)
    recv = comm.received(dx)

    outs = {}
    for k in BIG:
        layers = [recv[k, l] for l in range(weights[k].shape[0])]
        outs[k] = _adamw(layers, weights[k], moments_m[k], moments_v[k], "adamw_" + k)
    packed = _adamw_small(recv["small"], _pack_small(small), _pack_small(moments_m), _pack_small(moments_v),
                          "adamw_small")
    unpacked = [_unpack_small(p, small) for p in packed]
    for k in SMALL:
        outs[k] = [u[k] for u in unpacked]

    total = lax.psum(loss[0, 0], MESH_AXES)
    order = SMALL[:2] + BIG[:2] + ("ab_w_in", "ab_gn_gain", "ab_w_pool", "ab_pool_scale", "ab_w_out",
                                   "c_w_qkv", "c_rel_bias", "c_w_out", "final_norm")
    result = [total, dx[None]]
    for part in range(4):
        result += [outs[k][part] for k in order]
    return tuple(result)
```

```python
import functools

import jax
import jax.numpy as jnp
from jax import lax
from jax.experimental import pallas as pl
from jax.experimental.pallas import tpu as pltpu
from jax.experimental.pallas import tpu_sc as plsc

F32 = jnp.float32
BF16 = jnp.bfloat16
SDS = jax.ShapeDtypeStruct
MESH_AXES = ("x", "y", "c")
N_DEV = 8

D_MODEL = 1024
DEPTH = 4
CHUNK = 64
D_FF = 4 * D_MODEL
RMS_EPS = 1e-6
RET_WIDTH = 512
RET_HEADS = 4
RET_HEAD_DIM = 128
RET_ROPE_BASE = 10000.0
GN_EPS = 1e-5
POOL_WIDTH = 512
POOL_WINDOWS = (2, 4, 8, 16)
POOL_HALO = 16
AB_IN_WIDTH = 4 * RET_WIDTH + POOL_WIDTH
ATT_HEADS = 16
ATT_HEAD_DIM = 64
LEFT_CHUNKS = 8
REL_CLIP = 128
N_REL = 2 * REL_CLIP + 1
NEG_INF = -1e30

ADAM_LR = 0.001
ADAM_B1 = 0.9
ADAM_B2 = 0.999
ADAM_EPS = 1e-08
ADAM_WD = 0.01
ADAM_STEP = 10

TOKEN_TILE = 512
ATT_Q_TILE = 256
ATT_BACK = LEFT_CHUNKS * CHUNK // ATT_Q_TILE
ATT_K_TILE = (ATT_BACK + 1) * ATT_Q_TILE
ATT_PAIRS = 4
ATT_DIAG = 1024
REL_PAD = 384
VMEM_LIMIT_MB = 56

NT = (((1,), (1,)), ((), ()))
TN = (((0,), (0,)), ((), ()))


def _params(semantics, **kw):
    return pltpu.CompilerParams(dimension_semantics=semantics,
                                vmem_limit_bytes=VMEM_LIMIT_MB * 2 ** 20, **kw)


def _dot(a, b, dims=None):
    if dims is None:
        return jnp.dot(a, b, preferred_element_type=F32)
    return lax.dot_general(a, b, dims, preferred_element_type=F32)


def _bf(v):
    return v.astype(BF16)


def _tile(n, t):
    return min(n, t)


def _norm_mm(x, gain, w, tn, z_dtype, relu2, name):
    s, d = x.shape
    nj = w.shape[0]
    tm = _tile(s, TOKEN_TILE)

    def body(x_ref, g_ref, w_ref, h_ref, z_ref, *a_ref):
        xv = x_ref[...]
        r = lax.rsqrt(jnp.mean(xv * xv, axis=-1, keepdims=True) + RMS_EPS)
        h = _bf(xv * r * g_ref[...])
        h_ref[...] = h
        cw = tn if tn <= 512 else 512
        for j in range(nj):
            for c in range(0, tn, cw):
                z = _dot(h, w_ref[j, :, c:c + cw])
                cols = slice(j * tn + c, j * tn + c + cw)
                z_ref[:, cols] = z.astype(z_ref.dtype)
                if relu2:
                    a_ref[0][:, cols] = _bf(jnp.square(jnp.maximum(z, 0.0)))

    n = nj * tn
    out_shape = [SDS((s, d), BF16), SDS((s, n), z_dtype)]
    out_specs = [pl.BlockSpec((tm, d), lambda i: (i, 0)), pl.BlockSpec((tm, n), lambda i: (i, 0))]
    if relu2:
        out_shape.append(SDS((s, n), BF16))
        out_specs.append(pl.BlockSpec((tm, n), lambda i: (i, 0)))
    return pl.pallas_call(
        body, name=name, grid=(s // tm,), out_shape=out_shape,
        in_specs=[pl.BlockSpec((tm, d), lambda i: (i, 0)),
                  pl.BlockSpec((1, d), lambda i: (0, 0)),
                  pl.BlockSpec((nj, d, tn), lambda i: (0, 0, 0))],
        out_specs=out_specs,
        compiler_params=_params(("parallel",)),
    )(x, gain, w)


def _mm_res(parts, w, res, name):
    s, d = res.shape
    tm = _tile(s, TOKEN_TILE)
    widths = [p.shape[1] for p in parts]

    def body(*refs):
        a_refs = refs[:len(parts)]
        w_ref, res_ref, o_ref = refs[len(parts):]
        acc = res_ref[...]
        off = 0
        for a_ref, k in zip(a_refs, widths):
            acc = acc + _dot(a_ref[...], w_ref[off:off + k, :])
            off += k
        o_ref[...] = acc

    return pl.pallas_call(
        body, name=name, grid=(s // tm,), out_shape=SDS((s, d), F32),
        in_specs=[pl.BlockSpec((tm, k), lambda i: (i, 0)) for k in widths]
        + [pl.BlockSpec(w.shape, lambda i: (0, 0)), pl.BlockSpec((tm, d), lambda i: (i, 0))],
        out_specs=pl.BlockSpec((tm, d), lambda i: (i, 0)),
        compiler_params=_params(("parallel",)),
    )(*parts, w, res)


def _mm_nt_rows(dy, w, z, name):
    s, d = dy.shape
    k = w.shape[0]
    tm = _tile(s, TOKEN_TILE)
    tk = _tile(k, 1024)

    def body(dy_ref, w_ref, *rest):
        o_ref = rest[-1]
        dyb = _bf(dy_ref[...])
        for j in range(k // tk):
            cols = slice(j * tk, (j + 1) * tk)
            da = _dot(dyb, w_ref[cols, :], NT)
            if z is not None:
                da = da * (2.0 * jnp.maximum(rest[0][:, cols].astype(F32), 0.0))
            o_ref[:, cols] = _bf(da)

    in_specs = [pl.BlockSpec((tm, d), lambda i: (i, 0)), pl.BlockSpec((k, d), lambda i: (0, 0))]
    args = [dy, w]
    if z is not None:
        in_specs.append(pl.BlockSpec((tm, k), lambda i: (i, 0)))
        args.append(z)
    return pl.pallas_call(
        body, name=name, grid=(s // tm,), out_shape=SDS((s, k), BF16),
        in_specs=in_specs, out_specs=pl.BlockSpec((tm, k), lambda i: (i, 0)),
        compiler_params=_params(("parallel",)),
    )(*args)


def _mm_nt_normbwd(dz, w, x, gain, dres, name):
    s, d = x.shape
    nj, _, nc = w.shape
    tm = _tile(s, TOKEN_TILE)

    def body(dz_ref, w_ref, x_ref, g_ref, dres_ref, dx_ref, dg_ref):
        dh = _dot(dz_ref[:, 0:nc], w_ref[0], NT)
        for j in range(1, nj):
            dh = dh + _dot(dz_ref[:, j * nc:(j + 1) * nc], w_ref[j], NT)
        xv = x_ref[...]
        r = lax.rsqrt(jnp.mean(xv * xv, axis=-1, keepdims=True) + RMS_EPS)
        xn = xv * r

        @pl.when(pl.program_id(0) == 0)
        def _():
            dg_ref[...] = jnp.zeros_like(dg_ref)

        dg_ref[...] += jnp.sum(dh * xn, axis=0, keepdims=True)
        dxh = dh * g_ref[...]
        dx_ref[...] = dres_ref[...] + r * (dxh - xn * jnp.mean(dxh * xn, axis=-1, keepdims=True))

    return pl.pallas_call(
        body, name=name, grid=(s // tm,), out_shape=[SDS((s, d), F32), SDS((1, d), F32)],
        in_specs=[pl.BlockSpec((tm, nj * nc), lambda i: (i, 0)),
                  pl.BlockSpec((nj, d, nc), lambda i: (0, 0, 0)),
                  pl.BlockSpec((tm, d), lambda i: (i, 0)),
                  pl.BlockSpec((1, d), lambda i: (0, 0)),
                  pl.BlockSpec((tm, d), lambda i: (i, 0))],
        out_specs=[pl.BlockSpec((tm, d), lambda i: (i, 0)), pl.BlockSpec((1, d), lambda i: (0, 0))],
        compiler_params=_params(("arbitrary",)),
    )(dz, w, x, gain, dres)


def _mm_tn(a, b, ka, nb, a_tiled, split, name, after=None):
    s = a.shape[0]
    tm = _tile(s, 2 * TOKEN_TILE)
    nm = s // tm
    nj = a.shape[1] // ka if a_tiled else b.shape[1] // nb
    axis, parts = split
    pr, pc = (ka // parts, nb) if axis == 0 else (ka, nb // parts)

    def body(a_ref, b_ref, *rest):
        o_ref, acc = rest[-2:]
        m = pl.program_id(1)

        @pl.when(m == 0)
        def _():
            acc[...] = jnp.zeros_like(acc)

        acc[...] += _dot(_bf(a_ref[...]), _bf(b_ref[...]), TN)

        @pl.when(m == nm - 1)
        def _():
            for q in range(parts):
                piece = acc[q * pr:(q + 1) * pr, :] if axis == 0 else acc[:, q * pc:(q + 1) * pc]
                o_ref[q] = piece.astype(o_ref.dtype)

    return pl.pallas_call(
        body, name=name, grid=(nj, nm), out_shape=SDS((nj * parts, pr, pc), BF16),
        in_specs=[pl.BlockSpec((tm, ka), (lambda j, m: (m, j)) if a_tiled else (lambda j, m: (m, 0))),
                  pl.BlockSpec((tm, nb), (lambda j, m: (m, 0)) if a_tiled else (lambda j, m: (m, j)))]
        + [pl.BlockSpec(memory_space=pl.ANY)] * (after is not None),
        out_specs=pl.BlockSpec((parts, pr, pc), lambda j, m: (j, 0, 0)),
        scratch_shapes=[pltpu.VMEM((ka, nb), F32)],
        compiler_params=_params(("parallel", "arbitrary")),
    )(a, b, *([] if after is None else [after]))


def _final_loss(x, gain, target, name):
    s, d = x.shape
    tm = _tile(s, TOKEN_TILE)

    def body(x_ref, g_ref, t_ref, loss_ref, dx_ref, dg_ref):
        @pl.when(pl.program_id(0) == 0)
        def _():
            loss_ref[...] = jnp.zeros_like(loss_ref)
            dg_ref[...] = jnp.zeros_like(dg_ref)

        xv = x_ref[...]
        r = lax.rsqrt(jnp.mean(xv * xv, axis=-1, keepdims=True) + RMS_EPS)
        xn = xv * r
        err = xn * g_ref[...] - t_ref[...]
        loss_ref[...] += (0.5 / d) * jnp.sum(err * err)
        dy = err * (1.0 / d)
        dg_ref[...] += jnp.sum(dy * xn, axis=0, keepdims=True)
        dxh = dy * g_ref[...]
        dx_ref[...] = r * (dxh - xn * jnp.mean(dxh * xn, axis=-1, keepdims=True))

    return pl.pallas_call(
        body, name=name, grid=(s // tm,),
        out_shape=[SDS((8, 128), F32), SDS((s, d), F32), SDS((1, d), F32)],
        in_specs=[pl.BlockSpec((tm, d), lambda i: (i, 0)), pl.BlockSpec((1, d), lambda i: (0, 0)),
                  pl.BlockSpec((tm, d), lambda i: (i, 0))],
        out_specs=[pl.BlockSpec((8, 128), lambda i: (0, 0)), pl.BlockSpec((tm, d), lambda i: (i, 0)),
                   pl.BlockSpec((1, d), lambda i: (0, 0))],
        compiler_params=_params(("arbitrary",)),
    )(x, gain, target)


def _retention_tables(s):
    half = RET_HEAD_DIM // 2
    inv_freq = 1.0 / (RET_ROPE_BASE ** jnp.linspace(0.0, 1.0, half, dtype=F32))
    ang = jnp.arange(s, dtype=F32)[:, None] * inv_freq[None, :]
    cos, sin = jnp.cos(ang), jnp.sin(ang)
    cos_e = jnp.repeat(cos, 2, axis=-1)
    sin_s = jnp.stack([-sin, sin], axis=-1).reshape(s, RET_HEAD_DIM)
    log_g = jnp.log1p(-jnp.power(2.0, -5.0 - jnp.arange(RET_HEADS, dtype=F32)))
    pos = jnp.arange(CHUNK, dtype=F32)
    dmat = jnp.exp(jnp.abs(pos[:, None] - pos[None, :])[None] * log_g[:, None, None])
    qdec = jnp.exp((pos[None, :] + 1.0) * log_g[:, None])
    kdec = jnp.exp((CHUNK - 1.0 - pos[None, :]) * log_g[:, None])
    lam = jnp.exp(CHUNK * log_g)
    wide = (RET_HEADS, CHUNK, RET_HEAD_DIM)
    return dict(cos=cos_e, sin=sin_s, dmat=dmat,
                qdec=jnp.broadcast_to(qdec[:, :, None], wide),
                kdec=jnp.broadcast_to(kdec[:, :, None], wide),
                lam=jnp.broadcast_to(lam[:, None, None], (RET_HEADS, RET_HEAD_DIM, RET_HEAD_DIM)))


def _swap_pairs(t):
    lane = lax.broadcasted_iota(jnp.int32, t.shape, 1)
    return jnp.where(lane % 2 == 0, pltpu.roll(t, RET_HEAD_DIM - 1, 1), pltpu.roll(t, 1, 1))


def _head(h):
    return slice(h * RET_HEAD_DIM, (h + 1) * RET_HEAD_DIM)


def _ret_common_specs(tb, blk):
    zs = [pl.BlockSpec((tb, RET_WIDTH), functools.partial(lambda j, i: (blk(i), j), j)) for j in range(4)]
    tabs = [pl.BlockSpec((tb, RET_HEAD_DIM), lambda i: (blk(i), 0))] * 2
    consts = [pl.BlockSpec((1, RET_WIDTH), lambda i: (0, 0)),
              pl.BlockSpec((RET_HEADS, CHUNK, CHUNK), lambda i: (0, 0, 0)),
              pl.BlockSpec((RET_HEADS, CHUNK, RET_HEAD_DIM), lambda i: (0, 0, 0)),
              pl.BlockSpec((RET_HEADS, CHUNK, RET_HEAD_DIM), lambda i: (0, 0, 0)),
              pl.BlockSpec((RET_HEADS, RET_HEAD_DIM, RET_HEAD_DIM), lambda i: (0, 0, 0))]
    return zs + tabs + consts


def _ret_fwd(z, tabs, gn_gain, name):
    s = z.shape[0]
    tb = _tile(s, TOKEN_TILE)
    ncb = tb // CHUNK
    scale = RET_HEAD_DIM ** -0.5

    def body(q_ref, k_ref, v_ref, g_ref, cos_ref, sin_ref, gain_ref, dm_ref, qd_ref, kd_ref, lam_ref,
             o_ref, st_ref, ret_ref, s_scr, qr_scr, kr_scr):
        @pl.when(pl.program_id(0) == 0)
        def _():
            s_scr[...] = jnp.zeros_like(s_scr)

        cosv, sinv = cos_ref[...], sin_ref[...]
        for h in range(RET_HEADS):
            qh, kh = q_ref[:, _head(h)], k_ref[:, _head(h)]
            qr_scr[:, _head(h)] = qh * cosv + _swap_pairs(qh) * sinv
            kr_scr[:, _head(h)] = (kh * cosv + _swap_pairs(kh) * sinv) * scale

        def chunk(c, carry):
            rows = pl.ds(pl.multiple_of(c * CHUNK, CHUNK), CHUNK)
            for h in range(RET_HEADS):
                qc, kc, vc = qr_scr[rows, _head(h)], kr_scr[rows, _head(h)], v_ref[rows, _head(h)]
                a = _dot(_bf(qc), _bf(kc), NT) * dm_ref[h]
                st = s_scr[h]
                st_ref[c, h] = st
                o_ref[rows, _head(h)] = _dot(_bf(a), _bf(vc)) + _dot(_bf(qc * qd_ref[h]), _bf(st))
                s_scr[h] = st * lam_ref[h] + _dot(_bf(kc * kd_ref[h]), _bf(vc), TN)
            return carry

        lax.fori_loop(0, ncb, chunk, 0)
        for h in range(RET_HEADS):
            o = o_ref[:, _head(h)]
            mu = jnp.mean(o, axis=-1, keepdims=True)
            oc = o - mu
            y = oc * lax.rsqrt(jnp.mean(oc * oc, axis=-1, keepdims=True) + GN_EPS) * gain_ref[:, _head(h)]
            g = g_ref[:, _head(h)]
            ret_ref[:, _head(h)] = _bf(g / (1.0 + jnp.exp(-g)) * y)

    nc = s // CHUNK
    return pl.pallas_call(
        body, name=name, grid=(s // tb,),
        out_shape=[SDS((s, RET_WIDTH), F32), SDS((nc, RET_HEADS, RET_HEAD_DIM, RET_HEAD_DIM), F32),
                   SDS((s, RET_WIDTH), BF16)],
        in_specs=_ret_common_specs(tb, lambda i: i),
        out_specs=[pl.BlockSpec((tb, RET_WIDTH), lambda i: (i, 0)),
                   pl.BlockSpec((ncb, RET_HEADS, RET_HEAD_DIM, RET_HEAD_DIM), lambda i: (i, 0, 0, 0)),
                   pl.BlockSpec((tb, RET_WIDTH), lambda i: (i, 0))],
        scratch_shapes=[pltpu.VMEM((RET_HEADS, RET_HEAD_DIM, RET_HEAD_DIM), F32),
                        pltpu.VMEM((tb, RET_WIDTH), F32), pltpu.VMEM((tb, RET_WIDTH), F32)],
        compiler_params=_params(("arbitrary",)),
    )(z, z, z, z, tabs["cos"], tabs["sin"], gn_gain, tabs["dmat"], tabs["qdec"], tabs["kdec"], tabs["lam"])


def _ret_bwd(z, tabs, gn_gain, o_pre, states, du, name):
    s = z.shape[0]
    tb = _tile(s, TOKEN_TILE)
    ncb = tb // CHUNK
    nblk = s // tb
    scale = RET_HEAD_DIM ** -0.5
    rev = lambda i: nblk - 1 - i

    def body(q_ref, k_ref, v_ref, g_ref, cos_ref, sin_ref, gain_ref, dm_ref, qd_ref, kd_ref, lam_ref,
             o_ref, st_ref, dret_ref, dz_ref, dgain_ref, g_scr, qr_scr, kr_scr, do_scr, dq_scr, dk_scr):
        @pl.when(pl.program_id(0) == 0)
        def _():
            g_scr[...] = jnp.zeros_like(g_scr)
            dgain_ref[...] = jnp.zeros_like(dgain_ref)

        cosv, sinv = cos_ref[...], sin_ref[...]
        for h in range(RET_HEADS):
            hs = _head(h)
            qh, kh = q_ref[:, hs], k_ref[:, hs]
            qr_scr[:, hs] = qh * cosv + _swap_pairs(qh) * sinv
            kr_scr[:, hs] = (kh * cosv + _swap_pairs(kh) * sinv) * scale
            o = o_ref[:, hs]
            mu = jnp.mean(o, axis=-1, keepdims=True)
            oc = o - mu
            rstd = lax.rsqrt(jnp.mean(oc * oc, axis=-1, keepdims=True) + GN_EPS)
            yh = oc * rstd
            gain = gain_ref[:, hs]
            g = g_ref[:, hs]
            sg = 1.0 / (1.0 + jnp.exp(-g))
            dret = dret_ref[:, hs].astype(F32)
            dy = dret * (g * sg)
            dz_ref[:, 3 * RET_WIDTH + h * RET_HEAD_DIM:3 * RET_WIDTH + (h + 1) * RET_HEAD_DIM] = _bf(
                dret * (yh * gain) * (sg * (1.0 + g * (1.0 - sg))))
            dgain_ref[:, hs] += jnp.sum(dy * yh, axis=0, keepdims=True)
            dyh = dy * gain
            do_scr[:, hs] = rstd * (dyh - jnp.mean(dyh, axis=-1, keepdims=True)
                                    - yh * jnp.mean(dyh * yh, axis=-1, keepdims=True))

        def chunk(cc, carry):
            c = ncb - 1 - cc
            rows = pl.ds(pl.multiple_of(c * CHUNK, CHUNK), CHUNK)
            for h in range(RET_HEADS):
                hs = _head(h)
                qc, kc, vc, doc = _bf(qr_scr[rows, hs]), _bf(kr_scr[rows, hs]), _bf(v_ref[rows, hs]), _bf(do_scr[rows, hs])
                qdc, kdc = qd_ref[h], kd_ref[h]
                st, gs = _bf(st_ref[c, h]), g_scr[h]
                gsb = _bf(gs)
                dm = dm_ref[h]
                p = _bf(_dot(qc, kc, NT) * dm)
                da = _bf(_dot(doc, vc, NT) * dm)
                kt = _bf(kr_scr[rows, hs] * kdc)
                qt = _bf(qr_scr[rows, hs] * qdc)
                dz_ref[rows, 2 * RET_WIDTH + h * RET_HEAD_DIM:2 * RET_WIDTH + (h + 1) * RET_HEAD_DIM] = _bf(
                    _dot(p, doc, TN) + _dot(kt, gsb))
                dq_scr[rows, hs] = _dot(da, kc) + _dot(doc, st, NT) * qdc
                dk_scr[rows, hs] = _dot(da, qc, TN) + _dot(vc, gsb, NT) * kdc
                g_scr[h] = gs * lam_ref[h] + _dot(qt, doc, TN)
            return carry

        lax.fori_loop(0, ncb, chunk, 0)
        for h in range(RET_HEADS):
            hs = _head(h)
            dq, dk = dq_scr[:, hs], dk_scr[:, hs]
            dz_ref[:, h * RET_HEAD_DIM:(h + 1) * RET_HEAD_DIM] = _bf(dq * cosv - _swap_pairs(dq) * sinv)
            dz_ref[:, RET_WIDTH + h * RET_HEAD_DIM:RET_WIDTH + (h + 1) * RET_HEAD_DIM] = _bf(
                (dk * cosv - _swap_pairs(dk) * sinv) * scale)

    return pl.pallas_call(
        body, name=name, grid=(nblk,),
        out_shape=[SDS((s, 4 * RET_WIDTH), BF16), SDS((1, RET_WIDTH), F32)],
        in_specs=_ret_common_specs(tb, rev)
        + [pl.BlockSpec((tb, RET_WIDTH), lambda i: (rev(i), 0)),
           pl.BlockSpec((ncb, RET_HEADS, RET_HEAD_DIM, RET_HEAD_DIM), lambda i: (rev(i), 0, 0, 0)),
           pl.BlockSpec((tb, RET_WIDTH), lambda i: (rev(i), 0))],
        out_specs=[pl.BlockSpec((tb, 4 * RET_WIDTH), lambda i: (rev(i), 0)),
                   pl.BlockSpec((1, RET_WIDTH), lambda i: (0, 0))],
        scratch_shapes=[pltpu.VMEM((RET_HEADS, RET_HEAD_DIM, RET_HEAD_DIM), F32)]
        + [pltpu.VMEM((tb, RET_WIDTH), F32)] * 5,
        compiler_params=_params(("arbitrary",)),
    )(z, z, z, z, tabs["cos"], tabs["sin"], gn_gain, tabs["dmat"], tabs["qdec"], tabs["kdec"], tabs["lam"],
      o_pre, states, du)


POOL_COL = 4 * RET_WIDTH // POOL_WIDTH


def _pooled(cur, prev, t0):
    tm = cur.shape[0]
    xx = jnp.concatenate([prev, cur], axis=0)
    sums = {1: xx}
    w = 1
    while w < POOL_WINDOWS[-1]:
        sums[2 * w] = sums[w] + pltpu.roll(sums[w], w, 0)
        w *= 2
    t = t0 + lax.broadcasted_iota(jnp.int32, (tm, 128), 0)
    outs = []
    for gi, w in enumerate(POOL_WINDOWS):
        cols = slice(gi * 128, (gi + 1) * 128)
        cnt = jnp.minimum(t + 1, w).astype(F32)
        outs.append(sums[w][POOL_HALO:, cols] / cnt - cur[:, cols])
    return outs


def _pool_fwd(z, w_pool, scale, name):
    s = z.shape[0]
    tm = _tile(s, TOKEN_TILE)
    hb = tm // POOL_HALO

    def body(p_ref, prev_ref, w_ref, sc_ref, o_ref):
        i = pl.program_id(0)
        prev = jnp.where(i > 0, prev_ref[...], 0.0)
        pooled = _pooled(p_ref[...], prev, i * tm)
        for gi in range(len(POOL_WINDOWS)):
            cols = slice(gi * 128, (gi + 1) * 128)
            o_ref[:, cols] = _bf(_dot(_bf(pooled[gi]), _bf(w_ref[gi])) * sc_ref[:, cols])

    return pl.pallas_call(
        body, name=name, grid=(s // tm,), out_shape=SDS((s, POOL_WIDTH), BF16),
        in_specs=[pl.BlockSpec((tm, POOL_WIDTH), lambda i: (i, POOL_COL)),
                  pl.BlockSpec((POOL_HALO, POOL_WIDTH), lambda i: (jnp.maximum(i * hb - 1, 0), POOL_COL)),
                  pl.BlockSpec(w_pool.shape, lambda i: (0, 0, 0)),
                  pl.BlockSpec((1, POOL_WIDTH), lambda i: (0, 0))],
        out_specs=pl.BlockSpec((tm, POOL_WIDTH), lambda i: (i, 0)),
        compiler_params=_params(("parallel",)),
    )(z, z, w_pool, scale)


def _pool_bwd(z, w_pool, scale, du, name):
    s = z.shape[0]
    tm = _tile(s, TOKEN_TILE)
    hb = tm // POOL_HALO
    nblk = s // tm
    last_halo = s // POOL_HALO - 1

    def body(p_ref, prev_ref, w_ref, sc_ref, do_ref, don_ref, dp_ref, dw_ref, dsc_ref):
        i = pl.program_id(0)

        @pl.when(i == 0)
        def _():
            dw_ref[...] = jnp.zeros_like(dw_ref)
            dsc_ref[...] = jnp.zeros_like(dsc_ref)

        prev = jnp.where(i > 0, prev_ref[...], 0.0)
        pooled = _pooled(p_ref[...], prev, i * tm)
        dout = do_ref[...].astype(F32)
        dout_next = jnp.where(i < nblk - 1, don_ref[...].astype(F32), 0.0)
        sc = sc_ref[...]
        dmix = jnp.concatenate([dout * sc, dout_next * sc], axis=0)
        n = tm + POOL_HALO
        t = i * tm + lax.broadcasted_iota(jnp.int32, (n, 128), 0)
        for gi, w in enumerate(POOL_WINDOWS):
            cols = slice(gi * 128, (gi + 1) * 128)
            wg = _bf(w_ref[gi])
            pg = _bf(pooled[gi])
            dsc_ref[:, cols] += jnp.sum(dout[:, cols] * _dot(pg, wg), axis=0, keepdims=True)
            dw_ref[gi] += _dot(pg, _bf(dmix[:tm, cols]), TN)
            dpool = _dot(_bf(dmix[:, cols]), wg, NT)
            acc = dpool / jnp.minimum(t + 1, w).astype(F32)
            step = 1
            while step < w:
                acc = acc + pltpu.roll(acc, n - step, 0)
                step *= 2
            dp_ref[:, cols] = _bf(acc[:tm] - dpool[:tm])

    return pl.pallas_call(
        body, name=name, grid=(nblk,),
        out_shape=[SDS((s, POOL_WIDTH), BF16), SDS(w_pool.shape, F32), SDS((1, POOL_WIDTH), F32)],
        in_specs=[pl.BlockSpec((tm, POOL_WIDTH), lambda i: (i, POOL_COL)),
                  pl.BlockSpec((POOL_HALO, POOL_WIDTH), lambda i: (jnp.maximum(i * hb - 1, 0), POOL_COL)),
                  pl.BlockSpec(w_pool.shape, lambda i: (0, 0, 0)),
                  pl.BlockSpec((1, POOL_WIDTH), lambda i: (0, 0)),
                  pl.BlockSpec((tm, POOL_WIDTH), lambda i: (i, 1)),
                  pl.BlockSpec((POOL_HALO, POOL_WIDTH), lambda i: (jnp.minimum((i + 1) * hb, last_halo), 1))],
        out_specs=[pl.BlockSpec((tm, POOL_WIDTH), lambda i: (i, 0)),
                   pl.BlockSpec(w_pool.shape, lambda i: (0, 0, 0)),
                   pl.BlockSpec((1, POOL_WIDTH), lambda i: (0, 0))],
        compiler_params=_params(("arbitrary",)),
    )(z, z, w_pool, scale, du, du)


def _rel_onehot():
    r = lax.broadcasted_iota(jnp.int32, (REL_PAD, ATT_DIAG), 0)
    c = lax.broadcasted_iota(jnp.int32, (REL_PAD, ATT_DIAG), 1)
    rel = jnp.where(c < ATT_K_TILE, jnp.clip(LEFT_CHUNKS * CHUNK - c, -REL_CLIP, REL_CLIP) + REL_CLIP,
                    2 * REL_CLIP)
    return (rel == r).astype(BF16)


def _split3(v):
    hi = _bf(v)
    r1 = v - hi.astype(F32)
    mid = _bf(r1)
    return hi, mid, _bf(r1 - mid.astype(F32))


def _skew(v, sign):
    row = lax.broadcasted_iota(jnp.int32, v.shape, 0)
    bit = 1
    while bit < ATT_Q_TILE:
        shift = bit if sign > 0 else ATT_DIAG - bit
        v = jnp.where((row & bit) != 0, pltpu.roll(v, shift, 1), v)
        bit *= 2
    return v


def _attn_bias(rel_bias, name):
    def body(t_ref, o_ref):
        oh = _rel_onehot()
        base = sum(_dot(part, oh) for part in _split3(t_ref[0]))
        full = _skew(jnp.broadcast_to(base[0:1], (ATT_Q_TILE, ATT_DIAG)), +1)[:, :ATT_K_TILE]
        qc = lax.broadcasted_iota(jnp.int32, full.shape, 0) // CHUNK
        kc = lax.broadcasted_iota(jnp.int32, full.shape, 1) // CHUNK
        o_ref[0] = jnp.where((kc >= qc) & (kc <= qc + LEFT_CHUNKS), full, NEG_INF)

    t8 = jnp.broadcast_to(rel_bias[:, None, :], (ATT_HEADS, 8, REL_PAD))
    return pl.pallas_call(
        body, name=name, grid=(ATT_HEADS,), out_shape=SDS((ATT_HEADS, ATT_Q_TILE, ATT_K_TILE), F32),
        in_specs=[pl.BlockSpec((1, 8, REL_PAD), lambda h: (h, 0, 0))],
        out_specs=pl.BlockSpec((1, ATT_Q_TILE, ATT_K_TILE), lambda h: (h, 0, 0)),
        compiler_params=_params(("parallel",)),
    )(t8)


def _attn_dbias(dbias, name):
    def body(d_ref, o_ref):
        pad = jnp.zeros((ATT_Q_TILE, ATT_DIAG - ATT_K_TILE), F32)
        diag = _skew(jnp.concatenate([d_ref[0], pad], axis=1), -1)
        col = jnp.sum(diag, axis=0, keepdims=True)
        oh = _rel_onehot()
        col8 = jnp.broadcast_to(col, (8, ATT_DIAG))
        o_ref[0] = sum(_dot(part, oh, NT) for part in _split3(col8))

    out = pl.pallas_call(
        body, name=name, grid=(ATT_HEADS,), out_shape=SDS((ATT_HEADS, 8, REL_PAD), F32),
        in_specs=[pl.BlockSpec((1, ATT_Q_TILE, ATT_K_TILE), lambda h: (h, 0, 0))],
        out_specs=pl.BlockSpec((1, 8, REL_PAD), lambda h: (h, 0, 0)),
        compiler_params=_params(("parallel",)),
    )(dbias)
    return out[:, 0, :]


ATT_WIDTH = 128 * ATT_PAIRS
ATT_GROUPS = D_MODEL // ATT_WIDTH


def _attn_specs(nq):
    def tile(off, back):
        return pl.BlockSpec((ATT_Q_TILE, ATT_WIDTH),
                            lambda g, i: (jnp.maximum(jnp.minimum(i, nq - 1) - back, 0), off + g))

    backs = [ATT_BACK - b for b in range(ATT_BACK + 1)]
    return ([tile(0, 0)] + [tile(ATT_GROUPS, b) for b in backs] + [tile(2 * ATT_GROUPS, b) for b in backs]
            + [pl.BlockSpec((2 * ATT_PAIRS, ATT_Q_TILE, ATT_K_TILE), lambda g, i: (g, 0, 0))])


def _attn_weights(qh, k2, bias, i, masked):
    sc = _dot(qh, k2, NT) + bias
    if masked:
        kpos = (i - ATT_BACK) * ATT_Q_TILE + lax.broadcasted_iota(jnp.int32, sc.shape, 1)
        sc = jnp.where(kpos >= 0, sc, NEG_INF)
    e = jnp.exp(sc - jnp.max(sc, axis=-1, keepdims=True))
    return e, 1.0 / jnp.sum(e, axis=-1, keepdims=True)


def _first_head():
    return lax.broadcasted_iota(jnp.int32, (ATT_Q_TILE, 128), 1) < ATT_HEAD_DIM


def _pair_operands(q_ref, k_refs, v_refs, pp):
    cols = slice(pp * 128, (pp + 1) * 128)
    q2 = q_ref[:, cols] * ATT_HEAD_DIM ** -0.5
    k2 = jnp.concatenate([r[:, cols] for r in k_refs], axis=0)
    v2 = jnp.concatenate([r[:, cols] for r in v_refs], axis=0)
    return cols, q2, k2, v2


def _attn_fwd(z, bias, name):
    s = z.shape[0]
    nq = s // ATT_Q_TILE
    nt = ATT_BACK + 1

    def body(q_ref, *rest):
        k_refs, v_refs, (b_ref, o_ref) = rest[:nt], rest[nt:2 * nt], rest[2 * nt:]
        i = pl.program_id(1)
        first = _first_head()

        def compute(masked):
            for pp in range(ATT_PAIRS):
                cols, q2, k2, v2 = _pair_operands(q_ref, k_refs, v_refs, pp)
                outs = []
                for hh in range(2):
                    qh = jnp.where(first if hh == 0 else ~first, q2, 0)
                    e, inv = _attn_weights(qh, k2, b_ref[2 * pp + hh], i, masked)
                    outs.append(_dot(_bf(e), v2) * inv)
                o_ref[:, cols] = _bf(jnp.where(first, outs[0], outs[1]))

        pl.when(i < ATT_BACK)(lambda: compute(True))
        pl.when(i >= ATT_BACK)(lambda: compute(False))

    return pl.pallas_call(
        body, name=name, grid=(ATT_GROUPS, nq), out_shape=SDS((s, D_MODEL), BF16),
        in_specs=_attn_specs(nq),
        out_specs=pl.BlockSpec((ATT_Q_TILE, ATT_WIDTH), lambda g, i: (i, g)),
        compiler_params=_params(("parallel", "parallel")),
    )(*([z] * (1 + 2 * nt)), bias)


def _attn_bwd(z, bias, o, do, name):
    s = z.shape[0]
    nq = s // ATT_Q_TILE
    nt = ATT_BACK + 1

    def body(q_ref, *rest):
        k_refs, v_refs = rest[:nt], rest[nt:2 * nt]
        b_ref, o_ref, do_ref, dq_ref, dk_ref, dv_ref, db_ref, dk_acc, dv_acc = rest[2 * nt:]
        i = pl.program_id(1)
        first = _first_head()

        @pl.when(i == 0)
        def _():
            db_ref[...] = jnp.zeros_like(db_ref)
            dk_acc[...] = jnp.zeros_like(dk_acc)
            dv_acc[...] = jnp.zeros_like(dv_acc)

        def compute(masked):
            for pp in range(ATT_PAIRS):
                cols, q2, k2, v2 = _pair_operands(q_ref, k_refs, v_refs, pp)
                do2 = do_ref[:, cols].astype(F32)
                prod = do2 * o_ref[:, cols].astype(F32)
                dqs, dk, dv = [], None, None
                for hh in range(2):
                    mine = first if hh == 0 else ~first
                    qh = jnp.where(mine, q2, 0)
                    e, inv = _attn_weights(qh, k2, b_ref[2 * pp + hh], i, masked)
                    delta = jnp.sum(jnp.where(mine, prod, 0.0), axis=-1, keepdims=True) * inv
                    doh = _bf(jnp.where(mine, do2 * inv, 0.0))
                    ds = e * (_dot(doh, v2, NT) - delta)
                    db_ref[2 * pp + hh] += ds
                    dsb = _bf(ds)
                    dqs.append(_dot(dsb, k2))
                    dkh, dvh = _dot(dsb, qh, TN), _dot(_bf(e), doh, TN)
                    dk, dv = (dkh, dvh) if hh == 0 else (dk + dkh, dv + dvh)
                dq_ref[:, cols] = _bf(jnp.where(first, dqs[0], dqs[1]) * ATT_HEAD_DIM ** -0.5)
                for b in range(nt):
                    slot = (i + b + 1) % nt
                    rows = slice(b * ATT_Q_TILE, (b + 1) * ATT_Q_TILE)
                    if b < ATT_BACK:
                        dk_acc[slot, :, cols] += dk[rows]
                        dv_acc[slot, :, cols] += dv[rows]
                    else:
                        dk_acc[slot, :, cols] = dk[rows]
                        dv_acc[slot, :, cols] = dv[rows]

        pl.when(i < ATT_BACK)(lambda: compute(True))
        pl.when((i >= ATT_BACK) & (i < nq))(lambda: compute(False))
        done = (i + 1) % nt
        dk_ref[...] = _bf(dk_acc[done])
        dv_ref[...] = _bf(dv_acc[done])

    tile = pl.BlockSpec((ATT_Q_TILE, ATT_WIDTH), lambda g, i: (jnp.minimum(i, nq - 1), g))
    late = pl.BlockSpec((ATT_Q_TILE, ATT_WIDTH), lambda g, i: (jnp.maximum(i - ATT_BACK, 0), g))
    ring = pltpu.VMEM((nt, ATT_Q_TILE, ATT_WIDTH), F32)
    return pl.pallas_call(
        body, name=name, grid=(ATT_GROUPS, nq + ATT_BACK),
        out_shape=[SDS((s, D_MODEL), BF16)] * 3 + [SDS((ATT_HEADS, ATT_Q_TILE, ATT_K_TILE), F32)],
        in_specs=_attn_specs(nq) + [tile, tile],
        out_specs=[tile, late, late, pl.BlockSpec((2 * ATT_PAIRS, ATT_Q_TILE, ATT_K_TILE), lambda g, i: (g, 0, 0))],
        scratch_shapes=[ring, ring],
        compiler_params=_params(("parallel", "arbitrary")),
    )(*([z] * (1 + 2 * nt)), bias, o, do)


FWD_GROUPS = (
    (("ab_w_in", 0),),
    (("ab_w_out", 0), ("w_ffn_in", 0)),
    (("w_ffn_out", 0),),
    (("c_w_qkv", 0),),
    (("c_w_out", 0), ("w_ffn_in", 1), ("w_ffn_out", 1), ("ab_w_in", 1), ("ab_w_out", 1), ("w_ffn_in", 2), ("w_ffn_out", 2)),
    (("c_w_qkv", 1), ("c_w_out", 1), ("w_ffn_in", 3), ("w_ffn_out", 3)),
)


def _local_step(x, target, small, comm):
    s = x.shape[0]
    tabs = _retention_tables(s)
    saved, w = [], comm.weight
    for layer in range(DEPTH):
        i = layer // 2
        sv = {"x0": x}
        g_mix = small["mix_norm"][layer:layer + 1]
        if layer % 2 == 0:
            sv["h1"], sv["z"] = _norm_mm(x, g_mix, w("ab_w_in", i, x), AB_IN_WIDTH, F32, False, "ab_in_fwd")
            gn = small["ab_gn_gain"][i:i + 1]
            sv["o_pre"], sv["states"], ret = _ret_fwd(sv["z"], tabs, gn, "ret_fwd")
            pool = _pool_fwd(sv["z"], small["ab_w_pool"][i], small["ab_pool_scale"][i:i + 1], "pool_fwd")
            sv["u"] = (ret, pool)
            x = _mm_res([ret, pool], w("ab_w_out", i, ret), x, "ab_out_fwd")
        else:
            sv["h1"], sv["z"] = _norm_mm(x, g_mix, w("c_w_qkv", i, x), 3 * D_MODEL // N_DEV, BF16, False, "qkv_fwd")
            rb = jnp.pad(small["c_rel_bias"][i], ((0, 0), (0, REL_PAD - N_REL)))
            sv["bias"] = _attn_bias(rb, "attn_bias")
            sv["o"] = _attn_fwd(sv["z"], sv["bias"], "attn_fwd")
            x = _mm_res([sv["o"]], w("c_w_out", i, sv["o"]), x, "c_out_fwd")
        sv["x1"] = x
        sv["h2"], sv["z1"], sv["a"] = _norm_mm(x, small["ffn_norm"][layer:layer + 1], w("w_ffn_in", layer, x),
                                               D_FF // N_DEV, BF16, True, "ffn_in_fwd")
        x = _mm_res([sv["a"]], w("w_ffn_out", layer, sv["a"]), x, "ffn_out_fwd")
        saved.append(sv)

    loss, dx, d_final = _final_loss(x, small["final_norm"][None, :], target, "final_loss")

    gs = {k: [None] * DEPTH for k in ("mix_norm", "ffn_norm")}
    for k in ("ab_gn_gain", "ab_w_pool", "ab_pool_scale", "c_rel_bias"):
        gs[k] = [None] * (DEPTH // 2)
    gs["final_norm"] = d_final[0]
    tok = jnp.zeros((), F32)
    for layer in reversed(range(DEPTH)):
        i = layer // 2
        sv = saved[layer]
        dz1 = _mm_nt_rows(dx, w("w_ffn_out", layer), sv["z1"], "ffn_out_bwd")
        gw = {("w_ffn_out", layer): _mm_tn(sv["a"], dx, 1024, D_MODEL, True, (0, 2), "ffn_out_dw"),
              ("w_ffn_in", layer): _mm_tn(sv["h2"], dz1, D_MODEL, 1024, False, (1, 2), "ffn_in_dw",
                                          comm.after() if layer == 0 else None)}
        dx, dg = _mm_nt_normbwd(dz1, w("w_ffn_in", layer), sv["x1"], small["ffn_norm"][layer:layer + 1] + tok, dx,
                                "ffn_in_bwd")
        gs["ffn_norm"][layer] = dg[0]
        if layer == 0:
            tok = comm.send(gw)
            gw = {}
        g_mix = small["mix_norm"][layer:layer + 1] + tok
        if layer % 2 == 0:
            du = _mm_nt_rows(dx, w("ab_w_out", i), None, "mix_out_bwd")
            ret, pool = sv["u"]
            u = jnp.concatenate([ret, pool], axis=1)
            gw["ab_w_out", i] = _mm_tn(u, dx, D_MODEL, D_MODEL, True, (0, N_DEV), "mix_out_dw")
            gn = small["ab_gn_gain"][i:i + 1]
            dz_ret, dgn = _ret_bwd(sv["z"], tabs, gn, sv["o_pre"], sv["states"], du, "ret_bwd")
            dp, dwp, dsc = _pool_bwd(sv["z"], small["ab_w_pool"][i], small["ab_pool_scale"][i:i + 1], du, "pool_bwd")
            gs["ab_gn_gain"][i], gs["ab_w_pool"][i], gs["ab_pool_scale"][i] = dgn[0], dwp, dsc[0]
            dz = jnp.concatenate([dz_ret, dp], axis=1)
            gw["ab_w_in", i] = _to_shard_major(_mm_tn(sv["h1"], dz, D_MODEL, AB_IN_WIDTH // 2, False, (1, 1), "ab_in_dw",
                                                      comm.after()))
            dx, dg = _mm_nt_normbwd(dz, w("ab_w_in", i), sv["x0"], g_mix, dx, "ab_in_bwd")
        else:
            do = _mm_nt_rows(dx, w("c_w_out", i), None, "mix_out_bwd")
            gw["c_w_out", i] = _mm_tn(sv["o"], dx, D_MODEL, D_MODEL, True, (0, N_DEV), "mix_out_dw")
            dq, dk, dv, dbias = _attn_bwd(sv["z"], sv["bias"], sv["o"], do, "attn_bwd")
            gs["c_rel_bias"][i] = _attn_dbias(dbias, "attn_dbias")[:, :N_REL]
            dz = jnp.concatenate([dq, dk, dv], axis=1)
            gw["c_w_qkv", i] = _mm_tn(sv["h1"], dz, D_MODEL, 768, False, (1, 2), "qkv_dw", comm.after())
            dx, dg = _mm_nt_normbwd(dz, w("c_w_qkv", i), sv["x0"], g_mix, dx, "qkv_bwd")
        gs["mix_norm"][layer] = dg[0]
        if layer > 0:
            tok = comm.send(gw)
    gsmall = {k: (jnp.stack(v) if isinstance(v, list) else v) for k, v in gs.items()}
    return loss, dx, gw, gsmall


BIG = ("w_ffn_in", "w_ffn_out", "ab_w_in", "ab_w_out", "c_w_qkv", "c_w_out")
SMALL = ("mix_norm", "ffn_norm", "ab_gn_gain", "ab_w_pool", "ab_pool_scale", "c_rel_bias", "final_norm")
N_PEERS = N_DEV - 1
FLIPS = [(fx, fy, fc) for fx in (0, 1) for fy in (0, 1) for fc in (0, 1)][1:]


def _peers():
    x, y, c = (lax.axis_index(a) for a in MESH_AXES)
    peers = []
    for fx, fy, fc in FLIPS:
        px, py, pc = (1 - x if fx else x), (1 - y if fy else y), (1 - c if fc else c)
        peers.append(((px, py, pc), 4 * px + 2 * py + pc))
    return 4 * x + 2 * y + c, peers


def _exchange(srcs, by_slot, name, collective_id):
    n = len(srcs)
    src_refs = [jax.new_ref(a, memory_space=pltpu.MemorySpace.HBM) for a in srcs]
    land_refs = [jax.empty_ref(SDS((N_DEV,) + (a.shape[1:] if slotted else a.shape), a.dtype),
                               memory_space=pltpu.MemorySpace.HBM) for a, slotted in zip(srcs, by_slot)]

    @pl.kernel(mesh=plsc.ScalarSubcoreMesh(axis_name="sequencer", num_cores=1), name=name,
               scratch_types=(pltpu.SemaphoreType.DMA((n * N_PEERS,)), pltpu.SemaphoreType.DMA((n * N_PEERS,)),
                              pltpu.SemaphoreType.DMA((n,))),
               compiler_params=pltpu.CompilerParams(collective_id=collective_id))
    def launch(send_sems, recv_sems, local_sems):
        me, peers = _peers()
        barrier = pltpu.get_barrier_semaphore()
        for pos, _ in peers:
            pl.semaphore_signal(barrier, inc=1, device_id=pos, device_id_type=pl.DeviceIdType.MESH)
        pl.semaphore_wait(barrier, N_PEERS)
        waits = []
        for k in range(n):
            own = pltpu.make_async_copy(src_refs[k].at[me] if by_slot[k] else src_refs[k], land_refs[k].at[me],
                                        local_sems.at[k])
            own.start()
            waits.append(own.wait)
            for rel, (pos, slot) in enumerate(peers):
                src = src_refs[k].at[slot] if by_slot[k] else src_refs[k]
                sems = dict(send_sem=send_sems.at[k * N_PEERS + rel], recv_sem=recv_sems.at[k * N_PEERS + rel],
                            device_id=pos, device_id_type=pl.DeviceIdType.MESH)
                send = pltpu.make_async_remote_copy(src_ref=src, dst_ref=land_refs[k].at[me], **sems)
                send.start()
                arrival = pltpu.make_async_remote_copy(src_ref=src, dst_ref=land_refs[k].at[slot], **sems)
                waits += [send.wait_send, arrival.wait_recv]
        for wait in waits:
            wait()

    launch()
    return [r[...] for r in land_refs]


def _cast_group(weights, keys, token, name):
    def body(*refs):
        n = len(keys)
        for i_ref, o_ref in zip(refs[:n], refs[n + 1:]):
            o_ref[...] = _bf(i_ref[...])

    def layer_spec(shape, l):
        return pl.BlockSpec((None,) + shape[1:], lambda i: (l, 0, 0))

    whole = lambda shape: pl.BlockSpec(shape, lambda i: (0, 0))
    ins = [weights[k] for k, _ in keys]
    return pl.pallas_call(
        body, name=name, grid=(1,), out_shape=[SDS(w.shape[1:], BF16) for w in ins],
        in_specs=[layer_spec(w.shape, l) for w, (_, l) in zip(ins, keys)] + [pl.BlockSpec(memory_space=pl.ANY)],
        out_specs=[whole(w.shape[1:]) for w in ins],
        compiler_params=_params(("arbitrary",)),
    )(*ins, token)


def _to_shard_major(g):
    nj, ka, nb = g.shape
    full = jnp.transpose(g, (1, 0, 2)).reshape(ka, N_DEV, nj * nb // N_DEV)
    return jnp.transpose(full, (1, 0, 2))


def _from_gathered(name, g):
    if name in ("w_ffn_out", "ab_w_out", "c_w_out"):
        return g.reshape(g.shape[0] * g.shape[1], g.shape[2])
    if name == "ab_w_in":
        return jnp.transpose(g, (1, 0, 2)).reshape(1, g.shape[1], N_DEV * g.shape[2])
    return g


class _Comm:
    def __init__(self, weights):
        self.weights_f32 = weights
        self.gathered = {}
        self.got = {}
        self.calls = 0
        self.ended = None
        self.opened = -1

    def _exchange(self, srcs, by_slot, name):
        self.calls += 1
        got = _exchange(srcs, by_slot, name, self.calls)
        self.ended = got[0][(0,) * got[0].ndim].astype(F32) * 0.0
        return got

    def _gather(self, group, at):
        keys = FWD_GROUPS[group]
        token = jnp.zeros((8, 128), F32) + at[(0,) * at.ndim].astype(F32) * 0.0 + (0.0 if self.ended is None else self.ended)
        shards = _cast_group(self.weights_f32, keys, token, "cast_%d" % group)
        got = self._exchange(shards, [False] * len(keys), "gather_%d" % group)
        self.gathered.update((k, _from_gathered(k[0], arr)) for k, arr in zip(keys, got))

    def weight(self, name, layer, at=None):
        if (name, layer) not in self.gathered:
            self._gather(0, at)
        group = next(g for g, keys in enumerate(FWD_GROUPS) if (name, layer) in keys)
        if group == self.opened + 1:
            self.opened = group
            if group + 1 < len(FWD_GROUPS):
                self._gather(group + 1, at)
        return self.gathered[name, layer]

    def after(self):
        return jnp.zeros((8, 128), F32) + self.ended

    def send(self, grads, small=None):
        keys = list(grads)
        srcs = [grads[k] for k in keys] + ([] if small is None else [small])
        got = self._exchange(srcs, [True] * len(keys) + [False] * (small is not None), "scatter_%d" % self.calls)
        self.got.update(zip(keys + ["small"], got))
        return sum(g[0, 0, 0].astype(F32) * 0.0 for g in grads.values())

    def received(self):
        return self.got


def _adamw_math(g, w, m, v):
    m2 = ADAM_B1 * m + (1.0 - ADAM_B1) * g
    v2 = ADAM_B2 * v + (1.0 - ADAM_B2) * jnp.square(g)
    m_hat = m2 / (1.0 - ADAM_B1 ** ADAM_STEP)
    v_hat = v2 / (1.0 - ADAM_B2 ** ADAM_STEP)
    delta = -ADAM_LR * (m_hat / (jnp.sqrt(v_hat) + ADAM_EPS) + ADAM_WD * w)
    return delta, m2, v2


def _adamw(recv, w, m, v, name):
    nl, r, c = w.shape
    tr = _tile(r, 256)

    def body(*refs):
        g_refs = refs[:nl]
        w_ref, m_ref, v_ref, go_ref, d_ref, mo_ref, vo_ref = refs[nl:]
        for l in range(nl):
            @pl.when(pl.program_id(0) == l)
            def _():
                g = g_refs[l][0].astype(F32)
                for p in range(1, N_DEV):
                    g = g + g_refs[l][p].astype(F32)
                go_ref[...] = g
                d_ref[...], mo_ref[...], vo_ref[...] = _adamw_math(g, w_ref[...], m_ref[...], v_ref[...])

    def recv_spec(l):
        return pl.BlockSpec((N_DEV, tr, c), lambda layer, i: (0, jnp.where(layer == l, i, 0), 0))

    blk = pl.BlockSpec((None, tr, c), lambda l, i: (l, i, 0))
    return pl.pallas_call(
        body, name=name, grid=(nl, r // tr), out_shape=[SDS(w.shape, F32)] * 4,
        in_specs=[recv_spec(l) for l in range(nl)] + [blk, blk, blk],
        out_specs=[blk] * 4,
        compiler_params=_params(("arbitrary", "arbitrary")),
    )(*recv, w, m, v)


def _adamw_small(recv, w, m, v, name):
    def body(g_ref, w_ref, m_ref, v_ref, go_ref, d_ref, mo_ref, vo_ref):
        g = g_ref[0]
        for p in range(1, N_DEV):
            g = g + g_ref[p]
        go_ref[...] = g
        d_ref[...], mo_ref[...], vo_ref[...] = _adamw_math(g, w_ref[...], m_ref[...], v_ref[...])

    return pl.pallas_call(body, name=name, out_shape=[SDS(w.shape, F32)] * 4,
                          compiler_params=_params(None))(recv, w, m, v)


def _pack_small(tree):
    parts = []
    for k in SMALL:
        flat = tree[k].reshape(-1)
        rows = -(-flat.shape[0] // 1024) * 8
        parts.append(jnp.pad(flat, (0, rows * 128 - flat.shape[0])).reshape(rows, 128))
    return jnp.concatenate(parts, axis=0)


def _unpack_small(packed, like):
    out, row = {}, 0
    for k in SMALL:
        size = like[k].size
        rows = -(-size // 1024) * 8
        out[k] = packed[row:row + rows].reshape(-1)[:size].reshape(like[k].shape)
        row += rows
    return out


def kernel(x, mix_norm, ffn_norm, w_ffn_in, w_ffn_out, ab_w_in, ab_gn_gain, ab_w_pool, ab_pool_scale, ab_w_out, c_w_qkv, c_rel_bias, c_w_out, final_norm, loss_target, m_mix_norm, m_ffn_norm, m_w_ffn_in, m_w_ffn_out, m_ab_w_in, m_ab_gn_gain, m_ab_w_pool, m_ab_pool_scale, m_ab_w_out, m_c_w_qkv, m_c_rel_bias, m_c_w_out, m_final_norm, v_mix_norm, v_ffn_norm, v_w_ffn_in, v_w_ffn_out, v_ab_w_in, v_ab_gn_gain, v_ab_w_pool, v_ab_pool_scale, v_ab_w_out, v_c_w_qkv, v_c_rel_bias, v_c_w_out, v_final_norm):
    args = dict(locals())
    weights = {k: args[k] for k in BIG + SMALL}
    moments_m = {k: args["m_" + k] for k in BIG + SMALL}
    moments_v = {k: args["v_" + k] for k in BIG + SMALL}

    small = {k: weights[k] for k in SMALL}

    comm = _Comm(weights)
    loss, dx, last_grads, gsmall = _local_step(x[0], loss_target[0], small, comm)
    comm.send(last_grads, _pack_small(gsmall))
    recv = comm.received()

    outs = {}
    for k in BIG:
        layers = [recv[k, l] for l in range(weights[k].shape[0])]
        outs[k] = _adamw(layers, weights[k], moments_m[k], moments_v[k], "adamw_" + k)
    packed = _adamw_small(recv["small"], _pack_small(small), _pack_small(moments_m), _pack_small(moments_v),
                          "adamw_small")
    unpacked = [_unpack_small(p, small) for p in packed]
    for k in SMALL:
        outs[k] = [u[k] for u in unpacked]

    total = lax.psum(loss[0, 0], MESH_AXES)
    order = SMALL[:2] + BIG[:2] + ("ab_w_in", "ab_gn_gain", "ab_w_pool", "ab_pool_scale", "ab_w_out",
                                   "c_w_qkv", "c_rel_bias", "c_w_out", "final_norm")
    result = [total, dx[None]]
    for part in range(4):
        result += [outs[k][part] for k in order]
    return tuple(result)
```

```python
import functools

import jax
import jax.numpy as jnp
from jax import lax
from jax.experimental import pallas as pl
from jax.experimental.pallas import tpu as pltpu
from jax.experimental.pallas import tpu_sc as plsc

F32 = jnp.float32
BF16 = jnp.bfloat16
SDS = jax.ShapeDtypeStruct
MESH_AXES = ("x", "y", "c")
N_DEV = 8

D_MODEL = 1024
DEPTH = 4
CHUNK = 64
D_FF = 4 * D_MODEL
RMS_EPS = 1e-6
RET_WIDTH = 512
RET_HEADS = 4
RET_HEAD_DIM = 128
RET_ROPE_BASE = 10000.0
GN_EPS = 1e-5
POOL_WIDTH = 512
POOL_WINDOWS = (2, 4, 8, 16)
POOL_HALO = 16
AB_IN_WIDTH = 4 * RET_WIDTH + POOL_WIDTH
ATT_HEADS = 16
ATT_HEAD_DIM = 64
LEFT_CHUNKS = 8
REL_CLIP = 128
N_REL = 2 * REL_CLIP + 1
NEG_INF = -1e30

ADAM_LR = 0.001
ADAM_B1 = 0.9
ADAM_B2 = 0.999
ADAM_EPS = 1e-08
ADAM_WD = 0.01
ADAM_STEP = 10

TOKEN_TILE = 512
ATT_Q_TILE = 256
ATT_BACK = LEFT_CHUNKS * CHUNK // ATT_Q_TILE
ATT_K_TILE = (ATT_BACK + 1) * ATT_Q_TILE
ATT_PAIRS = 4
ATT_DIAG = 1024
REL_PAD = 384
VMEM_LIMIT_MB = 56

NT = (((1,), (1,)), ((), ()))
TN = (((0,), (0,)), ((), ()))


def _params(semantics, **kw):
    return pltpu.CompilerParams(dimension_semantics=semantics,
                                vmem_limit_bytes=VMEM_LIMIT_MB * 2 ** 20, **kw)


def _dot(a, b, dims=None):
    if dims is None:
        return jnp.dot(a, b, preferred_element_type=F32)
    return lax.dot_general(a, b, dims, preferred_element_type=F32)


def _bf(v):
    return v.astype(BF16)


def _tile(n, t):
    return min(n, t)


def _norm_mm(x, gain, w, tn, z_dtype, relu2, name):
    s, d = x.shape
    nj = w.shape[0]
    tm = _tile(s, TOKEN_TILE)

    def body(x_ref, g_ref, w_ref, h_ref, z_ref, *a_ref):
        xv = x_ref[...]
        r = lax.rsqrt(jnp.mean(xv * xv, axis=-1, keepdims=True) + RMS_EPS)
        h = _bf(xv * r * g_ref[...])
        h_ref[...] = h
        cw = tn if tn <= 512 else 512
        for j in range(nj):
            for c in range(0, tn, cw):
                z = _dot(h, w_ref[j, :, c:c + cw])
                cols = slice(j * tn + c, j * tn + c + cw)
                z_ref[:, cols] = z.astype(z_ref.dtype)
                if relu2:
                    a_ref[0][:, cols] = _bf(jnp.square(jnp.maximum(z, 0.0)))

    n = nj * tn
    out_shape = [SDS((s, d), BF16), SDS((s, n), z_dtype)]
    out_specs = [pl.BlockSpec((tm, d), lambda i: (i, 0)), pl.BlockSpec((tm, n), lambda i: (i, 0))]
    if relu2:
        out_shape.append(SDS((s, n), BF16))
        out_specs.append(pl.BlockSpec((tm, n), lambda i: (i, 0)))
    return pl.pallas_call(
        body, name=name, grid=(s // tm,), out_shape=out_shape,
        in_specs=[pl.BlockSpec((tm, d), lambda i: (i, 0)),
                  pl.BlockSpec((1, d), lambda i: (0, 0)),
                  pl.BlockSpec((nj, d, tn), lambda i: (0, 0, 0))],
        out_specs=out_specs,
        compiler_params=_params(("parallel",)),
    )(x, gain, w)


def _mm_res(parts, w, res, name):
    s, d = res.shape
    tm = _tile(s, TOKEN_TILE)
    widths = [p.shape[1] for p in parts]

    def body(*refs):
        a_refs = refs[:len(parts)]
        w_ref, res_ref, o_ref = refs[len(parts):]
        acc = res_ref[...]
        off = 0
        for a_ref, k in zip(a_refs, widths):
            acc = acc + _dot(a_ref[...], w_ref[off:off + k, :])
            off += k
        o_ref[...] = acc

    return pl.pallas_call(
        body, name=name, grid=(s // tm,), out_shape=SDS((s, d), F32),
        in_specs=[pl.BlockSpec((tm, k), lambda i: (i, 0)) for k in widths]
        + [pl.BlockSpec(w.shape, lambda i: (0, 0)), pl.BlockSpec((tm, d), lambda i: (i, 0))],
        out_specs=pl.BlockSpec((tm, d), lambda i: (i, 0)),
        compiler_params=_params(("parallel",)),
    )(*parts, w, res)


def _mm_nt_rows(dy, w, z, name):
    s, d = dy.shape
    k = w.shape[0]
    tm = _tile(s, TOKEN_TILE)
    tk = _tile(k, 1024)

    def body(dy_ref, w_ref, *rest):
        o_ref = rest[-1]
        dyb = _bf(dy_ref[...])
        for j in range(k // tk):
            cols = slice(j * tk, (j + 1) * tk)
            da = _dot(dyb, w_ref[cols, :], NT)
            if z is not None:
                da = da * (2.0 * jnp.maximum(rest[0][:, cols].astype(F32), 0.0))
            o_ref[:, cols] = _bf(da)

    in_specs = [pl.BlockSpec((tm, d), lambda i: (i, 0)), pl.BlockSpec((k, d), lambda i: (0, 0))]
    args = [dy, w]
    if z is not None:
        in_specs.append(pl.BlockSpec((tm, k), lambda i: (i, 0)))
        args.append(z)
    return pl.pallas_call(
        body, name=name, grid=(s // tm,), out_shape=SDS((s, k), BF16),
        in_specs=in_specs, out_specs=pl.BlockSpec((tm, k), lambda i: (i, 0)),
        compiler_params=_params(("parallel",)),
    )(*args)


def _mm_nt_normbwd(dz, w, x, gain, dres, name):
    s, d = x.shape
    nj, _, nc = w.shape
    tm = _tile(s, TOKEN_TILE)

    def body(dz_ref, w_ref, x_ref, g_ref, dres_ref, dx_ref, dg_ref):
        dh = _dot(dz_ref[:, 0:nc], w_ref[0], NT)
        for j in range(1, nj):
            dh = dh + _dot(dz_ref[:, j * nc:(j + 1) * nc], w_ref[j], NT)
        xv = x_ref[...]
        r = lax.rsqrt(jnp.mean(xv * xv, axis=-1, keepdims=True) + RMS_EPS)
        xn = xv * r

        @pl.when(pl.program_id(0) == 0)
        def _():
            dg_ref[...] = jnp.zeros_like(dg_ref)

        dg_ref[...] += jnp.sum(dh * xn, axis=0, keepdims=True)
        dxh = dh * g_ref[...]
        dx_ref[...] = dres_ref[...] + r * (dxh - xn * jnp.mean(dxh * xn, axis=-1, keepdims=True))

    return pl.pallas_call(
        body, name=name, grid=(s // tm,), out_shape=[SDS((s, d), F32), SDS((1, d), F32)],
        in_specs=[pl.BlockSpec((tm, nj * nc), lambda i: (i, 0)),
                  pl.BlockSpec((nj, d, nc), lambda i: (0, 0, 0)),
                  pl.BlockSpec((tm, d), lambda i: (i, 0)),
                  pl.BlockSpec((1, d), lambda i: (0, 0)),
                  pl.BlockSpec((tm, d), lambda i: (i, 0))],
        out_specs=[pl.BlockSpec((tm, d), lambda i: (i, 0)), pl.BlockSpec((1, d), lambda i: (0, 0))],
        compiler_params=_params(("arbitrary",)),
    )(dz, w, x, gain, dres)


def _mm_tn(a, b, ka, nb, a_tiled, split, name, after=None):
    a_parts = list(a) if isinstance(a, (list, tuple)) else [a]
    s = a_parts[0].shape[0]
    tm = _tile(s, 2 * TOKEN_TILE)
    nm = s // tm
    nj = a_parts[0].shape[1] // ka if a_tiled and len(a_parts) == 1 else (1 if a_tiled else b.shape[1] // nb)
    axis, parts = split
    pr, pc = (ka // parts, nb) if axis == 0 else (ka, nb // parts)

    def body(*refs):
        a_refs, b_ref = refs[:len(a_parts)], refs[len(a_parts)]
        o_ref, acc = refs[-2:]
        m = pl.program_id(1)

        @pl.when(m == 0)
        def _():
            acc[...] = jnp.zeros_like(acc)

        av = a_refs[0][...] if len(a_refs) == 1 else jnp.concatenate([r[...] for r in a_refs], axis=1)
        acc[...] += _dot(_bf(av), _bf(b_ref[...]), TN)

        @pl.when(m == nm - 1)
        def _():
            for q in range(parts):
                piece = acc[q * pr:(q + 1) * pr, :] if axis == 0 else acc[:, q * pc:(q + 1) * pc]
                o_ref[q] = piece.astype(o_ref.dtype)

    return pl.pallas_call(
        body, name=name, grid=(nj, nm), out_shape=SDS((nj * parts, pr, pc), BF16),
        in_specs=([pl.BlockSpec((tm, ka), (lambda j, m: (m, j)) if a_tiled else (lambda j, m: (m, 0)))]
                  if len(a_parts) == 1 else [pl.BlockSpec((tm, p.shape[1]), lambda j, m: (m, 0)) for p in a_parts])
        + [pl.BlockSpec((tm, nb), (lambda j, m: (m, 0)) if a_tiled else (lambda j, m: (m, j)))]
        + [pl.BlockSpec(memory_space=pl.ANY)] * (after is not None),
        out_specs=pl.BlockSpec((parts, pr, pc), lambda j, m: (j, 0, 0)),
        scratch_shapes=[pltpu.VMEM((ka, nb), F32)],
        compiler_params=_params(("parallel", "arbitrary")),
    )(*a_parts, b, *([] if after is None else [after]))


def _final_loss(x, gain, target, name):
    s, d = x.shape
    tm = _tile(s, TOKEN_TILE)

    def body(x_ref, g_ref, t_ref, loss_ref, dx_ref, dg_ref):
        @pl.when(pl.program_id(0) == 0)
        def _():
            loss_ref[...] = jnp.zeros_like(loss_ref)
            dg_ref[...] = jnp.zeros_like(dg_ref)

        xv = x_ref[...]
        r = lax.rsqrt(jnp.mean(xv * xv, axis=-1, keepdims=True) + RMS_EPS)
        xn = xv * r
        err = xn * g_ref[...] - t_ref[...]
        loss_ref[...] += (0.5 / d) * jnp.sum(err * err)
        dy = err * (1.0 / d)
        dg_ref[...] += jnp.sum(dy * xn, axis=0, keepdims=True)
        dxh = dy * g_ref[...]
        dx_ref[...] = r * (dxh - xn * jnp.mean(dxh * xn, axis=-1, keepdims=True))

    return pl.pallas_call(
        body, name=name, grid=(s // tm,),
        out_shape=[SDS((8, 128), F32), SDS((s, d), F32), SDS((1, d), F32)],
        in_specs=[pl.BlockSpec((tm, d), lambda i: (i, 0)), pl.BlockSpec((1, d), lambda i: (0, 0)),
                  pl.BlockSpec((tm, d), lambda i: (i, 0))],
        out_specs=[pl.BlockSpec((8, 128), lambda i: (0, 0)), pl.BlockSpec((tm, d), lambda i: (i, 0)),
                   pl.BlockSpec((1, d), lambda i: (0, 0))],
        compiler_params=_params(("arbitrary",)),
    )(x, gain, target)


def _retention_tables(s):
    half = RET_HEAD_DIM // 2
    inv_freq = 1.0 / (RET_ROPE_BASE ** jnp.linspace(0.0, 1.0, half, dtype=F32))
    ang = jnp.arange(s, dtype=F32)[:, None] * inv_freq[None, :]
    cos, sin = jnp.cos(ang), jnp.sin(ang)
    cos_e = jnp.repeat(cos, 2, axis=-1)
    sin_s = jnp.stack([-sin, sin], axis=-1).reshape(s, RET_HEAD_DIM)
    log_g = jnp.log1p(-jnp.power(2.0, -5.0 - jnp.arange(RET_HEADS, dtype=F32)))
    pos = jnp.arange(CHUNK, dtype=F32)
    dmat = jnp.exp(jnp.abs(pos[:, None] - pos[None, :])[None] * log_g[:, None, None])
    qdec = jnp.exp((pos[None, :] + 1.0) * log_g[:, None])
    kdec = jnp.exp((CHUNK - 1.0 - pos[None, :]) * log_g[:, None])
    lam = jnp.exp(CHUNK * log_g)
    wide = (RET_HEADS, CHUNK, RET_HEAD_DIM)
    return dict(cos=cos_e, sin=sin_s, dmat=dmat,
                qdec=jnp.broadcast_to(qdec[:, :, None], wide),
                kdec=jnp.broadcast_to(kdec[:, :, None], wide),
                lam=jnp.broadcast_to(lam[:, None, None], (RET_HEADS, RET_HEAD_DIM, RET_HEAD_DIM)))


def _swap_pairs(t):
    lane = lax.broadcasted_iota(jnp.int32, t.shape, 1)
    return jnp.where(lane % 2 == 0, pltpu.roll(t, RET_HEAD_DIM - 1, 1), pltpu.roll(t, 1, 1))


def _head(h):
    return slice(h * RET_HEAD_DIM, (h + 1) * RET_HEAD_DIM)


def _ret_common_specs(tb, blk):
    zs = [pl.BlockSpec((tb, RET_WIDTH), functools.partial(lambda j, i: (blk(i), j), j)) for j in range(4)]
    tabs = [pl.BlockSpec((tb, RET_HEAD_DIM), lambda i: (blk(i), 0))] * 2
    consts = [pl.BlockSpec((1, RET_WIDTH), lambda i: (0, 0)),
              pl.BlockSpec((RET_HEADS, CHUNK, CHUNK), lambda i: (0, 0, 0)),
              pl.BlockSpec((RET_HEADS, CHUNK, RET_HEAD_DIM), lambda i: (0, 0, 0)),
              pl.BlockSpec((RET_HEADS, CHUNK, RET_HEAD_DIM), lambda i: (0, 0, 0)),
              pl.BlockSpec((RET_HEADS, RET_HEAD_DIM, RET_HEAD_DIM), lambda i: (0, 0, 0))]
    return zs + tabs + consts


def _ret_fwd(z, tabs, gn_gain, name):
    s = z.shape[0]
    tb = _tile(s, TOKEN_TILE)
    ncb = tb // CHUNK
    scale = RET_HEAD_DIM ** -0.5

    def body(q_ref, k_ref, v_ref, g_ref, cos_ref, sin_ref, gain_ref, dm_ref, qd_ref, kd_ref, lam_ref,
             o_ref, st_ref, ret_ref, s_scr, qr_scr, kr_scr):
        @pl.when(pl.program_id(0) == 0)
        def _():
            s_scr[...] = jnp.zeros_like(s_scr)

        cosv, sinv = cos_ref[...], sin_ref[...]
        for h in range(RET_HEADS):
            qh, kh = q_ref[:, _head(h)], k_ref[:, _head(h)]
            qr_scr[:, _head(h)] = qh * cosv + _swap_pairs(qh) * sinv
            kr_scr[:, _head(h)] = (kh * cosv + _swap_pairs(kh) * sinv) * scale

        def chunk(c, carry):
            rows = pl.ds(pl.multiple_of(c * CHUNK, CHUNK), CHUNK)
            for h in range(RET_HEADS):
                qc, kc, vc = qr_scr[rows, _head(h)], kr_scr[rows, _head(h)], v_ref[rows, _head(h)]
                a = _dot(_bf(qc), _bf(kc), NT) * dm_ref[h]
                st = s_scr[h]
                st_ref[c, h] = st
                o_ref[rows, _head(h)] = _dot(_bf(a), _bf(vc)) + _dot(_bf(qc * qd_ref[h]), _bf(st))
                s_scr[h] = st * lam_ref[h] + _dot(_bf(kc * kd_ref[h]), _bf(vc), TN)
            return carry

        lax.fori_loop(0, ncb, chunk, 0)
        for h in range(RET_HEADS):
            o = o_ref[:, _head(h)]
            mu = jnp.mean(o, axis=-1, keepdims=True)
            oc = o - mu
            y = oc * lax.rsqrt(jnp.mean(oc * oc, axis=-1, keepdims=True) + GN_EPS) * gain_ref[:, _head(h)]
            g = g_ref[:, _head(h)]
            ret_ref[:, _head(h)] = _bf(g / (1.0 + jnp.exp(-g)) * y)

    nc = s // CHUNK
    return pl.pallas_call(
        body, name=name, grid=(s // tb,),
        out_shape=[SDS((s, RET_WIDTH), F32), SDS((nc, RET_HEADS, RET_HEAD_DIM, RET_HEAD_DIM), F32),
                   SDS((s, RET_WIDTH), BF16)],
        in_specs=_ret_common_specs(tb, lambda i: i),
        out_specs=[pl.BlockSpec((tb, RET_WIDTH), lambda i: (i, 0)),
                   pl.BlockSpec((ncb, RET_HEADS, RET_HEAD_DIM, RET_HEAD_DIM), lambda i: (i, 0, 0, 0)),
                   pl.BlockSpec((tb, RET_WIDTH), lambda i: (i, 0))],
        scratch_shapes=[pltpu.VMEM((RET_HEADS, RET_HEAD_DIM, RET_HEAD_DIM), F32),
                        pltpu.VMEM((tb, RET_WIDTH), F32), pltpu.VMEM((tb, RET_WIDTH), F32)],
        compiler_params=_params(("arbitrary",)),
    )(z, z, z, z, tabs["cos"], tabs["sin"], gn_gain, tabs["dmat"], tabs["qdec"], tabs["kdec"], tabs["lam"])


def _ret_bwd(z, tabs, gn_gain, o_pre, states, du, name):
    s = z.shape[0]
    tb = _tile(s, TOKEN_TILE)
    ncb = tb // CHUNK
    nblk = s // tb
    scale = RET_HEAD_DIM ** -0.5
    rev = lambda i: nblk - 1 - i

    def body(q_ref, k_ref, v_ref, g_ref, cos_ref, sin_ref, gain_ref, dm_ref, qd_ref, kd_ref, lam_ref,
             o_ref, st_ref, dret_ref, dz_ref, dgain_ref, g_scr, qr_scr, kr_scr, do_scr, dq_scr, dk_scr):
        @pl.when(pl.program_id(0) == 0)
        def _():
            g_scr[...] = jnp.zeros_like(g_scr)
            dgain_ref[...] = jnp.zeros_like(dgain_ref)

        cosv, sinv = cos_ref[...], sin_ref[...]
        for h in range(RET_HEADS):
            hs = _head(h)
            qh, kh = q_ref[:, hs], k_ref[:, hs]
            qr_scr[:, hs] = qh * cosv + _swap_pairs(qh) * sinv
            kr_scr[:, hs] = (kh * cosv + _swap_pairs(kh) * sinv) * scale
            o = o_ref[:, hs]
            mu = jnp.mean(o, axis=-1, keepdims=True)
            oc = o - mu
            rstd = lax.rsqrt(jnp.mean(oc * oc, axis=-1, keepdims=True) + GN_EPS)
            yh = oc * rstd
            gain = gain_ref[:, hs]
            g = g_ref[:, hs]
            sg = 1.0 / (1.0 + jnp.exp(-g))
            dret = dret_ref[:, hs].astype(F32)
            dy = dret * (g * sg)
            dz_ref[:, 3 * RET_WIDTH + h * RET_HEAD_DIM:3 * RET_WIDTH + (h + 1) * RET_HEAD_DIM] = _bf(
                dret * (yh * gain) * (sg * (1.0 + g * (1.0 - sg))))
            dgain_ref[:, hs] += jnp.sum(dy * yh, axis=0, keepdims=True)
            dyh = dy * gain
            do_scr[:, hs] = rstd * (dyh - jnp.mean(dyh, axis=-1, keepdims=True)
                                    - yh * jnp.mean(dyh * yh, axis=-1, keepdims=True))

        def chunk(cc, carry):
            c = ncb - 1 - cc
            rows = pl.ds(pl.multiple_of(c * CHUNK, CHUNK), CHUNK)
            for h in range(RET_HEADS):
                hs = _head(h)
                qc, kc, vc, doc = _bf(qr_scr[rows, hs]), _bf(kr_scr[rows, hs]), _bf(v_ref[rows, hs]), _bf(do_scr[rows, hs])
                qdc, kdc = qd_ref[h], kd_ref[h]
                st, gs = _bf(st_ref[c, h]), g_scr[h]
                gsb = _bf(gs)
                dm = dm_ref[h]
                p = _bf(_dot(qc, kc, NT) * dm)
                da = _bf(_dot(doc, vc, NT) * dm)
                kt = _bf(kr_scr[rows, hs] * kdc)
                qt = _bf(qr_scr[rows, hs] * qdc)
                dz_ref[rows, 2 * RET_WIDTH + h * RET_HEAD_DIM:2 * RET_WIDTH + (h + 1) * RET_HEAD_DIM] = _bf(
                    _dot(p, doc, TN) + _dot(kt, gsb))
                dq_scr[rows, hs] = _dot(da, kc) + _dot(doc, st, NT) * qdc
                dk_scr[rows, hs] = _dot(da, qc, TN) + _dot(vc, gsb, NT) * kdc
                g_scr[h] = gs * lam_ref[h] + _dot(qt, doc, TN)
            return carry

        lax.fori_loop(0, ncb, chunk, 0)
        for h in range(RET_HEADS):
            hs = _head(h)
            dq, dk = dq_scr[:, hs], dk_scr[:, hs]
            dz_ref[:, h * RET_HEAD_DIM:(h + 1) * RET_HEAD_DIM] = _bf(dq * cosv - _swap_pairs(dq) * sinv)
            dz_ref[:, RET_WIDTH + h * RET_HEAD_DIM:RET_WIDTH + (h + 1) * RET_HEAD_DIM] = _bf(
                (dk * cosv - _swap_pairs(dk) * sinv) * scale)

    return pl.pallas_call(
        body, name=name, grid=(nblk,),
        out_shape=[SDS((s, AB_IN_WIDTH), BF16), SDS((1, RET_WIDTH), F32)],
        in_specs=_ret_common_specs(tb, rev)
        + [pl.BlockSpec((tb, RET_WIDTH), lambda i: (rev(i), 0)),
           pl.BlockSpec((ncb, RET_HEADS, RET_HEAD_DIM, RET_HEAD_DIM), lambda i: (rev(i), 0, 0, 0)),
           pl.BlockSpec((tb, RET_WIDTH), lambda i: (rev(i), 0))],
        out_specs=[pl.BlockSpec((tb, 4 * RET_WIDTH), lambda i: (rev(i), 0)),
                   pl.BlockSpec((1, RET_WIDTH), lambda i: (0, 0))],
        scratch_shapes=[pltpu.VMEM((RET_HEADS, RET_HEAD_DIM, RET_HEAD_DIM), F32)]
        + [pltpu.VMEM((tb, RET_WIDTH), F32)] * 5,
        compiler_params=_params(("arbitrary",)),
    )(z, z, z, z, tabs["cos"], tabs["sin"], gn_gain, tabs["dmat"], tabs["qdec"], tabs["kdec"], tabs["lam"],
      o_pre, states, du)


POOL_COL = 4 * RET_WIDTH // POOL_WIDTH


def _pooled(cur, prev, t0):
    tm = cur.shape[0]
    xx = jnp.concatenate([prev, cur], axis=0)
    sums = {1: xx}
    w = 1
    while w < POOL_WINDOWS[-1]:
        sums[2 * w] = sums[w] + pltpu.roll(sums[w], w, 0)
        w *= 2
    t = t0 + lax.broadcasted_iota(jnp.int32, (tm, 128), 0)
    outs = []
    for gi, w in enumerate(POOL_WINDOWS):
        cols = slice(gi * 128, (gi + 1) * 128)
        cnt = jnp.minimum(t + 1, w).astype(F32)
        outs.append(sums[w][POOL_HALO:, cols] / cnt - cur[:, cols])
    return outs


def _pool_fwd(z, w_pool, scale, name):
    s = z.shape[0]
    tm = _tile(s, TOKEN_TILE)
    hb = tm // POOL_HALO

    def body(p_ref, prev_ref, w_ref, sc_ref, o_ref):
        i = pl.program_id(0)
        prev = jnp.where(i > 0, prev_ref[...], 0.0)
        pooled = _pooled(p_ref[...], prev, i * tm)
        for gi in range(len(POOL_WINDOWS)):
            cols = slice(gi * 128, (gi + 1) * 128)
            o_ref[:, cols] = _bf(_dot(_bf(pooled[gi]), _bf(w_ref[gi])) * sc_ref[:, cols])

    return pl.pallas_call(
        body, name=name, grid=(s // tm,), out_shape=SDS((s, POOL_WIDTH), BF16),
        in_specs=[pl.BlockSpec((tm, POOL_WIDTH), lambda i: (i, POOL_COL)),
                  pl.BlockSpec((POOL_HALO, POOL_WIDTH), lambda i: (jnp.maximum(i * hb - 1, 0), POOL_COL)),
                  pl.BlockSpec(w_pool.shape, lambda i: (0, 0, 0)),
                  pl.BlockSpec((1, POOL_WIDTH), lambda i: (0, 0))],
        out_specs=pl.BlockSpec((tm, POOL_WIDTH), lambda i: (i, 0)),
        compiler_params=_params(("parallel",)),
    )(z, z, w_pool, scale)


def _pool_bwd(z, w_pool, scale, du, dz, name):
    s = z.shape[0]
    tm = _tile(s, TOKEN_TILE)
    hb = tm // POOL_HALO
    nblk = s // tm
    last_halo = s // POOL_HALO - 1

    def body(p_ref, prev_ref, w_ref, sc_ref, do_ref, don_ref, dz_ref, dp_ref, dw_ref, dsc_ref):
        i = pl.program_id(0)

        @pl.when(i == 0)
        def _():
            dw_ref[...] = jnp.zeros_like(dw_ref)
            dsc_ref[...] = jnp.zeros_like(dsc_ref)

        prev = jnp.where(i > 0, prev_ref[...], 0.0)
        pooled = _pooled(p_ref[...], prev, i * tm)
        dout = do_ref[...].astype(F32)
        dout_next = jnp.where(i < nblk - 1, don_ref[...].astype(F32), 0.0)
        sc = sc_ref[...]
        dmix = jnp.concatenate([dout * sc, dout_next * sc], axis=0)
        n = tm + POOL_HALO
        t = i * tm + lax.broadcasted_iota(jnp.int32, (n, 128), 0)
        for gi, w in enumerate(POOL_WINDOWS):
            cols = slice(gi * 128, (gi + 1) * 128)
            wg = _bf(w_ref[gi])
            pg = _bf(pooled[gi])
            dsc_ref[:, cols] += jnp.sum(dout[:, cols] * _dot(pg, wg), axis=0, keepdims=True)
            dw_ref[gi] += _dot(pg, _bf(dmix[:tm, cols]), TN)
            dpool = _dot(_bf(dmix[:, cols]), wg, NT)
            acc = dpool / jnp.minimum(t + 1, w).astype(F32)
            step = 1
            while step < w:
                acc = acc + pltpu.roll(acc, n - step, 0)
                step *= 2
            dp_ref[:, cols] = _bf(acc[:tm] - dpool[:tm])

    return pl.pallas_call(
        body, name=name, grid=(nblk,),
        out_shape=[SDS(dz.shape, BF16), SDS(w_pool.shape, F32), SDS((1, POOL_WIDTH), F32)],
        in_specs=[pl.BlockSpec((tm, POOL_WIDTH), lambda i: (i, POOL_COL)),
                  pl.BlockSpec((POOL_HALO, POOL_WIDTH), lambda i: (jnp.maximum(i * hb - 1, 0), POOL_COL)),
                  pl.BlockSpec(w_pool.shape, lambda i: (0, 0, 0)),
                  pl.BlockSpec((1, POOL_WIDTH), lambda i: (0, 0)),
                  pl.BlockSpec((tm, POOL_WIDTH), lambda i: (i, 1)),
                  pl.BlockSpec((POOL_HALO, POOL_WIDTH), lambda i: (jnp.minimum((i + 1) * hb, last_halo), 1)),
                  pl.BlockSpec(memory_space=pl.ANY)],
        out_specs=[pl.BlockSpec((tm, POOL_WIDTH), lambda i: (i, POOL_COL)),
                   pl.BlockSpec(w_pool.shape, lambda i: (0, 0, 0)),
                   pl.BlockSpec((1, POOL_WIDTH), lambda i: (0, 0))],
        input_output_aliases={6: 0},
        compiler_params=_params(("arbitrary",)),
    )(z, z, w_pool, scale, du, du, dz)


def _rel_onehot():
    r = lax.broadcasted_iota(jnp.int32, (REL_PAD, ATT_DIAG), 0)
    c = lax.broadcasted_iota(jnp.int32, (REL_PAD, ATT_DIAG), 1)
    rel = jnp.where(c < ATT_K_TILE, jnp.clip(LEFT_CHUNKS * CHUNK - c, -REL_CLIP, REL_CLIP) + REL_CLIP,
                    2 * REL_CLIP)
    return (rel == r).astype(BF16)


def _split3(v):
    hi = _bf(v)
    r1 = v - hi.astype(F32)
    mid = _bf(r1)
    return hi, mid, _bf(r1 - mid.astype(F32))


def _skew(v, sign):
    row = lax.broadcasted_iota(jnp.int32, v.shape, 0)
    bit = 1
    while bit < ATT_Q_TILE:
        shift = bit if sign > 0 else ATT_DIAG - bit
        v = jnp.where((row & bit) != 0, pltpu.roll(v, shift, 1), v)
        bit *= 2
    return v


def _attn_bias(rel_bias, name):
    def body(t_ref, o_ref):
        oh = _rel_onehot()
        base = sum(_dot(part, oh) for part in _split3(t_ref[0]))
        full = _skew(jnp.broadcast_to(base[0:1], (ATT_Q_TILE, ATT_DIAG)), +1)[:, :ATT_K_TILE]
        qc = lax.broadcasted_iota(jnp.int32, full.shape, 0) // CHUNK
        kc = lax.broadcasted_iota(jnp.int32, full.shape, 1) // CHUNK
        o_ref[0] = jnp.where((kc >= qc) & (kc <= qc + LEFT_CHUNKS), full, NEG_INF)

    t8 = jnp.broadcast_to(rel_bias[:, None, :], (ATT_HEADS, 8, REL_PAD))
    return pl.pallas_call(
        body, name=name, grid=(ATT_HEADS,), out_shape=SDS((ATT_HEADS, ATT_Q_TILE, ATT_K_TILE), F32),
        in_specs=[pl.BlockSpec((1, 8, REL_PAD), lambda h: (h, 0, 0))],
        out_specs=pl.BlockSpec((1, ATT_Q_TILE, ATT_K_TILE), lambda h: (h, 0, 0)),
        compiler_params=_params(("parallel",)),
    )(t8)


def _attn_dbias(dbias, name):
    def body(d_ref, o_ref):
        pad = jnp.zeros((ATT_Q_TILE, ATT_DIAG - ATT_K_TILE), F32)
        diag = _skew(jnp.concatenate([d_ref[0], pad], axis=1), -1)
        col = jnp.sum(diag, axis=0, keepdims=True)
        oh = _rel_onehot()
        col8 = jnp.broadcast_to(col, (8, ATT_DIAG))
        o_ref[0] = sum(_dot(part, oh, NT) for part in _split3(col8))

    out = pl.pallas_call(
        body, name=name, grid=(ATT_HEADS,), out_shape=SDS((ATT_HEADS, 8, REL_PAD), F32),
        in_specs=[pl.BlockSpec((1, ATT_Q_TILE, ATT_K_TILE), lambda h: (h, 0, 0))],
        out_specs=pl.BlockSpec((1, 8, REL_PAD), lambda h: (h, 0, 0)),
        compiler_params=_params(("parallel",)),
    )(dbias)
    return out[:, 0, :]


ATT_WIDTH = 128 * ATT_PAIRS
ATT_GROUPS = D_MODEL // ATT_WIDTH


def _attn_specs(nq):
    def tile(off, back):
        return pl.BlockSpec((ATT_Q_TILE, ATT_WIDTH),
                            lambda g, i: (jnp.maximum(jnp.minimum(i, nq - 1) - back, 0), off + g))

    backs = [ATT_BACK - b for b in range(ATT_BACK + 1)]
    return ([tile(0, 0)] + [tile(ATT_GROUPS, b) for b in backs] + [tile(2 * ATT_GROUPS, b) for b in backs]
            + [pl.BlockSpec((2 * ATT_PAIRS, ATT_Q_TILE, ATT_K_TILE), lambda g, i: (g, 0, 0))])


def _attn_weights(qh, k2, bias, i, masked):
    sc = _dot(qh, k2, NT) + bias
    if masked:
        kpos = (i - ATT_BACK) * ATT_Q_TILE + lax.broadcasted_iota(jnp.int32, sc.shape, 1)
        sc = jnp.where(kpos >= 0, sc, NEG_INF)
    e = jnp.exp(sc - jnp.max(sc, axis=-1, keepdims=True))
    return e, 1.0 / jnp.sum(e, axis=-1, keepdims=True)


def _first_head():
    return lax.broadcasted_iota(jnp.int32, (ATT_Q_TILE, 128), 1) < ATT_HEAD_DIM


def _pair_operands(q_ref, k_refs, v_refs, pp):
    cols = slice(pp * 128, (pp + 1) * 128)
    q2 = q_ref[:, cols] * ATT_HEAD_DIM ** -0.5
    k2 = jnp.concatenate([r[:, cols] for r in k_refs], axis=0)
    v2 = jnp.concatenate([r[:, cols] for r in v_refs], axis=0)
    return cols, q2, k2, v2


def _attn_fwd(z, bias, name):
    s = z.shape[0]
    nq = s // ATT_Q_TILE
    nt = ATT_BACK + 1

    def body(q_ref, *rest):
        k_refs, v_refs, (b_ref, o_ref) = rest[:nt], rest[nt:2 * nt], rest[2 * nt:]
        i = pl.program_id(1)
        first = _first_head()

        def compute(masked):
            for pp in range(ATT_PAIRS):
                cols, q2, k2, v2 = _pair_operands(q_ref, k_refs, v_refs, pp)
                outs = []
                for hh in range(2):
                    qh = jnp.where(first if hh == 0 else ~first, q2, 0)
                    e, inv = _attn_weights(qh, k2, b_ref[2 * pp + hh], i, masked)
                    outs.append(_dot(_bf(e), v2) * inv)
                o_ref[:, cols] = _bf(jnp.where(first, outs[0], outs[1]))

        pl.when(i < ATT_BACK)(lambda: compute(True))
        pl.when(i >= ATT_BACK)(lambda: compute(False))

    return pl.pallas_call(
        body, name=name, grid=(ATT_GROUPS, nq), out_shape=SDS((s, D_MODEL), BF16),
        in_specs=_attn_specs(nq),
        out_specs=pl.BlockSpec((ATT_Q_TILE, ATT_WIDTH), lambda g, i: (i, g)),
        compiler_params=_params(("parallel", "parallel")),
    )(*([z] * (1 + 2 * nt)), bias)


def _attn_bwd(z, bias, o, do, name):
    s = z.shape[0]
    nq = s // ATT_Q_TILE
    nt = ATT_BACK + 1

    def body(q_ref, *rest):
        k_refs, v_refs = rest[:nt], rest[nt:2 * nt]
        b_ref, o_ref, do_ref, dq_ref, dk_ref, dv_ref, db_ref, dk_acc, dv_acc = rest[2 * nt:]
        i = pl.program_id(1)
        first = _first_head()

        @pl.when(i == 0)
        def _():
            db_ref[...] = jnp.zeros_like(db_ref)
            dk_acc[...] = jnp.zeros_like(dk_acc)
            dv_acc[...] = jnp.zeros_like(dv_acc)

        def compute(masked):
            for pp in range(ATT_PAIRS):
                cols, q2, k2, v2 = _pair_operands(q_ref, k_refs, v_refs, pp)
                do2 = do_ref[:, cols].astype(F32)
                prod = do2 * o_ref[:, cols].astype(F32)
                dqs, dk, dv = [], None, None
                for hh in range(2):
                    mine = first if hh == 0 else ~first
                    qh = jnp.where(mine, q2, 0)
                    e, inv = _attn_weights(qh, k2, b_ref[2 * pp + hh], i, masked)
                    delta = jnp.sum(jnp.where(mine, prod, 0.0), axis=-1, keepdims=True) * inv
                    doh = _bf(jnp.where(mine, do2 * inv, 0.0))
                    ds = e * (_dot(doh, v2, NT) - delta)
                    db_ref[2 * pp + hh] += ds
                    dsb = _bf(ds)
                    dqs.append(_dot(dsb, k2))
                    dkh, dvh = _dot(dsb, qh, TN), _dot(_bf(e), doh, TN)
                    dk, dv = (dkh, dvh) if hh == 0 else (dk + dkh, dv + dvh)
                dq_ref[:, cols] = _bf(jnp.where(first, dqs[0], dqs[1]) * ATT_HEAD_DIM ** -0.5)
                for b in range(nt):
                    slot = (i + b + 1) % nt
                    rows = slice(b * ATT_Q_TILE, (b + 1) * ATT_Q_TILE)
                    if b < ATT_BACK:
                        dk_acc[slot, :, cols] += dk[rows]
                        dv_acc[slot, :, cols] += dv[rows]
                    else:
                        dk_acc[slot, :, cols] = dk[rows]
                        dv_acc[slot, :, cols] = dv[rows]

        pl.when(i < ATT_BACK)(lambda: compute(True))
        pl.when((i >= ATT_BACK) & (i < nq))(lambda: compute(False))
        done = (i + 1) % nt
        dk_ref[...] = _bf(dk_acc[done])
        dv_ref[...] = _bf(dv_acc[done])

    tile = pl.BlockSpec((ATT_Q_TILE, ATT_WIDTH), lambda g, i: (jnp.minimum(i, nq - 1), g))
    late = pl.BlockSpec((ATT_Q_TILE, ATT_WIDTH), lambda g, i: (jnp.maximum(i - ATT_BACK, 0), g))
    ring = pltpu.VMEM((nt, ATT_Q_TILE, ATT_WIDTH), F32)
    return pl.pallas_call(
        body, name=name, grid=(ATT_GROUPS, nq + ATT_BACK),
        out_shape=[SDS((s, D_MODEL), BF16)] * 3 + [SDS((ATT_HEADS, ATT_Q_TILE, ATT_K_TILE), F32)],
        in_specs=_attn_specs(nq) + [tile, tile],
        out_specs=[tile, late, late, pl.BlockSpec((2 * ATT_PAIRS, ATT_Q_TILE, ATT_K_TILE), lambda g, i: (g, 0, 0))],
        scratch_shapes=[ring, ring],
        compiler_params=_params(("parallel", "arbitrary")),
    )(*([z] * (1 + 2 * nt)), bias, o, do)


FWD_GROUPS = (
    (("ab_w_in", 0),),
    (("ab_w_out", 0), ("w_ffn_in", 0)),
    (("w_ffn_out", 0),),
    (("c_w_qkv", 0),),
    (("c_w_out", 0), ("w_ffn_in", 1), ("w_ffn_out", 1), ("ab_w_in", 1)),
    (("ab_w_out", 1), ("w_ffn_in", 2), ("w_ffn_out", 2), ("c_w_qkv", 1)),
    (("c_w_out", 1), ("w_ffn_in", 3), ("w_ffn_out", 3)),
)


def _local_step(x, target, small, comm):
    s = x.shape[0]
    tabs = _retention_tables(s)
    saved, w = [], comm.weight
    for layer in range(DEPTH):
        i = layer // 2
        sv = {"x0": x}
        g_mix = small["mix_norm"][layer:layer + 1]
        if layer % 2 == 0:
            sv["h1"], sv["z"] = _norm_mm(x, g_mix, w("ab_w_in", i, x), AB_IN_WIDTH, F32, False, "ab_in_fwd")
            gn = small["ab_gn_gain"][i:i + 1]
            sv["o_pre"], sv["states"], ret = _ret_fwd(sv["z"], tabs, gn, "ret_fwd")
            pool = _pool_fwd(sv["z"], small["ab_w_pool"][i], small["ab_pool_scale"][i:i + 1], "pool_fwd")
            sv["u"] = (ret, pool)
            x = _mm_res([ret, pool], w("ab_w_out", i, ret), x, "ab_out_fwd")
        else:
            sv["h1"], sv["z"] = _norm_mm(x, g_mix, w("c_w_qkv", i, x), 3 * D_MODEL // N_DEV, BF16, False, "qkv_fwd")
            rb = jnp.pad(small["c_rel_bias"][i], ((0, 0), (0, REL_PAD - N_REL)))
            sv["bias"] = _attn_bias(rb, "attn_bias")
            sv["o"] = _attn_fwd(sv["z"], sv["bias"], "attn_fwd")
            x = _mm_res([sv["o"]], w("c_w_out", i, sv["o"]), x, "c_out_fwd")
        sv["x1"] = x
        sv["h2"], sv["z1"], sv["a"] = _norm_mm(x, small["ffn_norm"][layer:layer + 1], w("w_ffn_in", layer, x),
                                               D_FF // N_DEV, BF16, True, "ffn_in_fwd")
        x = _mm_res([sv["a"]], w("w_ffn_out", layer, sv["a"]), x, "ffn_out_fwd")
        saved.append(sv)

    loss, dx, d_final = _final_loss(x, small["final_norm"][None, :], target, "final_loss")

    gs = {k: [None] * DEPTH for k in ("mix_norm", "ffn_norm")}
    for k in ("ab_gn_gain", "ab_w_pool", "ab_pool_scale", "c_rel_bias"):
        gs[k] = [None] * (DEPTH // 2)
    gs["final_norm"] = d_final[0]
    tok = jnp.zeros((), F32)
    for layer in reversed(range(DEPTH)):
        i = layer // 2
        sv = saved[layer]
        dz1 = _mm_nt_rows(dx, w("w_ffn_out", layer), sv["z1"], "ffn_out_bwd")
        gw = {("w_ffn_out", layer): _mm_tn(sv["a"], dx, 1024, D_MODEL, True, (0, 2), "ffn_out_dw"),
              ("w_ffn_in", layer): _mm_tn(sv["h2"], dz1, D_MODEL, 1024, False, (1, 2), "ffn_in_dw",
                                          comm.after() if layer == 0 else None)}
        dx, dg = _mm_nt_normbwd(dz1, w("w_ffn_in", layer), sv["x1"], small["ffn_norm"][layer:layer + 1] + tok, dx,
                                "ffn_in_bwd")
        gs["ffn_norm"][layer] = dg[0]
        if layer == 0:
            tok = comm.send(gw)
            gw = {}
        g_mix = small["mix_norm"][layer:layer + 1] + tok
        if layer % 2 == 0:
            du = _mm_nt_rows(dx, w("ab_w_out", i), None, "mix_out_bwd")
            gw["ab_w_out", i] = _mm_tn(sv["u"], dx, D_MODEL, D_MODEL, True, (0, N_DEV), "mix_out_dw")
            gn = small["ab_gn_gain"][i:i + 1]
            dz, dgn = _ret_bwd(sv["z"], tabs, gn, sv["o_pre"], sv["states"], du, "ret_bwd")
            dz, dwp, dsc = _pool_bwd(sv["z"], small["ab_w_pool"][i], small["ab_pool_scale"][i:i + 1], du, dz, "pool_bwd")
            gs["ab_gn_gain"][i], gs["ab_w_pool"][i], gs["ab_pool_scale"][i] = dgn[0], dwp, dsc[0]
            gw["ab_w_in", i] = _to_shard_major(_mm_tn(sv["h1"], dz, D_MODEL, AB_IN_WIDTH // 2, False, (1, 1), "ab_in_dw",
                                                      comm.after()))
            dx, dg = _mm_nt_normbwd(dz, w("ab_w_in", i), sv["x0"], g_mix, dx, "ab_in_bwd")
        else:
            do = _mm_nt_rows(dx, w("c_w_out", i), None, "mix_out_bwd")
            gw["c_w_out", i] = _mm_tn(sv["o"], dx, D_MODEL, D_MODEL, True, (0, N_DEV), "mix_out_dw")
            dq, dk, dv, dbias = _attn_bwd(sv["z"], sv["bias"], sv["o"], do, "attn_bwd")
            gs["c_rel_bias"][i] = _attn_dbias(dbias, "attn_dbias")[:, :N_REL]
            dz = jnp.concatenate([dq, dk, dv], axis=1)
            gw["c_w_qkv", i] = _mm_tn(sv["h1"], dz, D_MODEL, 768, False, (1, 2), "qkv_dw", comm.after())
            dx, dg = _mm_nt_normbwd(dz, w("c_w_qkv", i), sv["x0"], g_mix, dx, "qkv_bwd")
        gs["mix_norm"][layer] = dg[0]
        if layer > 0:
            tok = comm.send(gw)
    gsmall = {k: (jnp.stack(v) if isinstance(v, list) else v) for k, v in gs.items()}
    return loss, dx, gw, gsmall


BIG = ("w_ffn_in", "w_ffn_out", "ab_w_in", "ab_w_out", "c_w_qkv", "c_w_out")
SMALL = ("mix_norm", "ffn_norm", "ab_gn_gain", "ab_w_pool", "ab_pool_scale", "c_rel_bias", "final_norm")
N_PEERS = N_DEV - 1
FLIPS = [(fx, fy, fc) for fx in (0, 1) for fy in (0, 1) for fc in (0, 1)][1:]


def _peers():
    x, y, c = (lax.axis_index(a) for a in MESH_AXES)
    peers = []
    for fx, fy, fc in FLIPS:
        px, py, pc = (1 - x if fx else x), (1 - y if fy else y), (1 - c if fc else c)
        peers.append(((px, py, pc), 4 * px + 2 * py + pc))
    return 4 * x + 2 * y + c, peers


def _exchange(srcs, by_slot, name, collective_id):
    n = len(srcs)
    src_refs = [jax.new_ref(a, memory_space=pltpu.MemorySpace.HBM) for a in srcs]
    land_refs = [jax.empty_ref(SDS((N_DEV,) + (a.shape[1:] if slotted else a.shape), a.dtype),
                               memory_space=pltpu.MemorySpace.HBM) for a, slotted in zip(srcs, by_slot)]

    @pl.kernel(mesh=plsc.ScalarSubcoreMesh(axis_name="sequencer", num_cores=1), name=name,
               scratch_types=(pltpu.SemaphoreType.DMA((n * N_PEERS,)), pltpu.SemaphoreType.DMA((n * N_PEERS,)),
                              pltpu.SemaphoreType.DMA((n,))),
               compiler_params=pltpu.CompilerParams(collective_id=collective_id))
    def launch(send_sems, recv_sems, local_sems):
        me, peers = _peers()
        barrier = pltpu.get_barrier_semaphore()
        for pos, _ in peers:
            pl.semaphore_signal(barrier, inc=1, device_id=pos, device_id_type=pl.DeviceIdType.MESH)
        pl.semaphore_wait(barrier, N_PEERS)
        waits = []
        for k in range(n):
            own = pltpu.make_async_copy(src_refs[k].at[me] if by_slot[k] else src_refs[k], land_refs[k].at[me],
                                        local_sems.at[k])
            own.start()
            waits.append(own.wait)
            for rel, (pos, slot) in enumerate(peers):
                src = src_refs[k].at[slot] if by_slot[k] else src_refs[k]
                sems = dict(send_sem=send_sems.at[k * N_PEERS + rel], recv_sem=recv_sems.at[k * N_PEERS + rel],
                            device_id=pos, device_id_type=pl.DeviceIdType.MESH)
                send = pltpu.make_async_remote_copy(src_ref=src, dst_ref=land_refs[k].at[me], **sems)
                send.start()
                arrival = pltpu.make_async_remote_copy(src_ref=src, dst_ref=land_refs[k].at[slot], **sems)
                waits += [send.wait_send, arrival.wait_recv]
        for wait in waits:
            wait()

    launch()
    return [r[...] for r in land_refs]


def _cast_group(weights, keys, token, name):
    def body(*refs):
        n = len(keys)
        for i_ref, o_ref in zip(refs[:n], refs[n + 1:]):
            o_ref[...] = _bf(i_ref[...])

    def layer_spec(shape, l):
        return pl.BlockSpec((None,) + shape[1:], lambda i: (l, 0, 0))

    whole = lambda shape: pl.BlockSpec(shape, lambda i: (0, 0))
    ins = [weights[k] for k, _ in keys]
    return pl.pallas_call(
        body, name=name, grid=(1,), out_shape=[SDS(w.shape[1:], BF16) for w in ins],
        in_specs=[layer_spec(w.shape, l) for w, (_, l) in zip(ins, keys)] + [pl.BlockSpec(memory_space=pl.ANY)],
        out_specs=[whole(w.shape[1:]) for w in ins],
        compiler_params=_params(("arbitrary",)),
    )(*ins, token)


def _to_shard_major(g):
    nj, ka, nb = g.shape
    full = jnp.transpose(g, (1, 0, 2)).reshape(ka, N_DEV, nj * nb // N_DEV)
    return jnp.transpose(full, (1, 0, 2))


def _from_gathered(name, g):
    if name in ("w_ffn_out", "ab_w_out", "c_w_out"):
        return g.reshape(g.shape[0] * g.shape[1], g.shape[2])
    if name == "ab_w_in":
        return jnp.transpose(g, (1, 0, 2)).reshape(1, g.shape[1], N_DEV * g.shape[2])
    return g


class _Comm:
    def __init__(self, weights):
        self.weights_f32 = weights
        self.gathered = {}
        self.got = {}
        self.calls = 0
        self.ended = None
        self.opened = -1

    def _exchange(self, srcs, by_slot, name):
        self.calls += 1
        got = _exchange(srcs, by_slot, name, self.calls)
        self.ended = got[0][(0,) * got[0].ndim].astype(F32) * 0.0
        return got

    def _gather(self, group, at):
        keys = FWD_GROUPS[group]
        token = jnp.zeros((8, 128), F32) + at[(0,) * at.ndim].astype(F32) * 0.0 + (0.0 if self.ended is None else self.ended)
        shards = _cast_group(self.weights_f32, keys, token, "cast_%d" % group)
        got = self._exchange(shards, [False] * len(keys), "gather_%d" % group)
        self.gathered.update((k, _from_gathered(k[0], arr)) for k, arr in zip(keys, got))

    def weight(self, name, layer, at=None):
        if (name, layer) not in self.gathered:
            self._gather(0, at)
        group = next(g for g, keys in enumerate(FWD_GROUPS) if (name, layer) in keys)
        if group == self.opened + 1:
            self.opened = group
            if group + 1 < len(FWD_GROUPS):
                self._gather(group + 1, at)
        return self.gathered[name, layer]

    def after(self):
        return jnp.zeros((8, 128), F32) + self.ended

    def send(self, grads, shared=None):
        shared = shared or {}
        keys = list(grads) + list(shared)
        srcs = list(grads.values()) + list(shared.values())
        got = self._exchange(srcs, [True] * len(grads) + [False] * len(shared), "scatter_%d" % self.calls)
        self.got.update(zip(keys, got))
        return sum(g[0, 0, 0].astype(F32) * 0.0 for g in grads.values())

    def received(self):
        return self.got


def _adamw_math(g, w, m, v):
    m2 = ADAM_B1 * m + (1.0 - ADAM_B1) * g
    v2 = ADAM_B2 * v + (1.0 - ADAM_B2) * jnp.square(g)
    m_hat = m2 / (1.0 - ADAM_B1 ** ADAM_STEP)
    v_hat = v2 / (1.0 - ADAM_B2 ** ADAM_STEP)
    delta = -ADAM_LR * (m_hat / (jnp.sqrt(v_hat) + ADAM_EPS) + ADAM_WD * w)
    return delta, m2, v2


def _adamw(recv, w, m, v, name):
    nl, r, c = w.shape
    tr = _tile(r, 256)

    def body(*refs):
        g_refs = refs[:nl]
        w_ref, m_ref, v_ref, go_ref, d_ref, mo_ref, vo_ref = refs[nl:]
        for l in range(nl):
            @pl.when(pl.program_id(0) == l)
            def _():
                g = g_refs[l][0].astype(F32)
                for p in range(1, N_DEV):
                    g = g + g_refs[l][p].astype(F32)
                go_ref[...] = g
                d_ref[...], mo_ref[...], vo_ref[...] = _adamw_math(g, w_ref[...], m_ref[...], v_ref[...])

    def recv_spec(l):
        return pl.BlockSpec((N_DEV, tr, c), lambda layer, i: (0, jnp.where(layer == l, i, 0), 0))

    blk = pl.BlockSpec((None, tr, c), lambda l, i: (l, i, 0))
    return pl.pallas_call(
        body, name=name, grid=(nl, r // tr), out_shape=[SDS(w.shape, F32)] * 4,
        in_specs=[recv_spec(l) for l in range(nl)] + [blk, blk, blk],
        out_specs=[blk] * 4,
        compiler_params=_params(("arbitrary", "arbitrary")),
    )(*recv, w, m, v)


def _adamw_small(recv, loss_parts, w, m, v, name):
    n = len(w)

    def total(ref):
        t = ref[0]
        for p in range(1, N_DEV):
            t = t + ref[p]
        return t

    def body(*refs):
        g_refs, loss_ref = refs[:n], refs[n]
        w_refs, m_refs, v_refs = refs[n + 1:2 * n + 1], refs[2 * n + 1:3 * n + 1], refs[3 * n + 1:4 * n + 1]
        outs = refs[4 * n + 1:]
        for i in range(n):
            g = total(g_refs[i])
            outs[4 * i][...] = g
            outs[4 * i + 1][...], outs[4 * i + 2][...], outs[4 * i + 3][...] = _adamw_math(
                g, w_refs[i][...], m_refs[i][...], v_refs[i][...])
        outs[4 * n][...] = total(loss_ref)

    out_shape = [SDS(p.shape, F32) for p in w for _ in range(4)] + [SDS(loss_parts.shape[1:], F32)]
    outs = pl.pallas_call(body, name=name, out_shape=out_shape,
                          compiler_params=_params(None))(*recv, loss_parts, *w, *m, *v)
    return [outs[4 * i:4 * i + 4] for i in range(n)], outs[-1]


def kernel(x, mix_norm, ffn_norm, w_ffn_in, w_ffn_out, ab_w_in, ab_gn_gain, ab_w_pool, ab_pool_scale, ab_w_out, c_w_qkv, c_rel_bias, c_w_out, final_norm, loss_target, m_mix_norm, m_ffn_norm, m_w_ffn_in, m_w_ffn_out, m_ab_w_in, m_ab_gn_gain, m_ab_w_pool, m_ab_pool_scale, m_ab_w_out, m_c_w_qkv, m_c_rel_bias, m_c_w_out, m_final_norm, v_mix_norm, v_ffn_norm, v_w_ffn_in, v_w_ffn_out, v_ab_w_in, v_ab_gn_gain, v_ab_w_pool, v_ab_pool_scale, v_ab_w_out, v_c_w_qkv, v_c_rel_bias, v_c_w_out, v_final_norm):
    args = dict(locals())
    weights = {k: args[k] for k in BIG + SMALL}
    moments_m = {k: args["m_" + k] for k in BIG + SMALL}
    moments_v = {k: args["v_" + k] for k in BIG + SMALL}

    small = {k: weights[k] for k in SMALL}
    rows = lambda a: a.reshape(1, -1) if a.ndim == 1 else a

    comm = _Comm(weights)
    loss, dx, last_grads, gsmall = _local_step(x[0], loss_target[0], small, comm)
    comm.send(last_grads, {**{k: rows(gsmall[k]) for k in SMALL}, "loss": loss})
    recv = comm.received()

    outs = {}
    for k in BIG:
        layers = [recv[k, l] for l in range(weights[k].shape[0])]
        outs[k] = _adamw(layers, weights[k], moments_m[k], moments_v[k], "adamw_" + k)
    updated, total = _adamw_small([recv[k] for k in SMALL], recv["loss"], [rows(small[k]) for k in SMALL],
                                  [rows(moments_m[k]) for k in SMALL], [rows(moments_v[k]) for k in SMALL], "adamw_small")
    for k, parts in zip(SMALL, updated):
        outs[k] = [p.reshape(small[k].shape) for p in parts]

    order = SMALL[:2] + BIG[:2] + ("ab_w_in", "ab_gn_gain", "ab_w_pool", "ab_pool_scale", "ab_w_out",
                                   "c_w_qkv", "c_rel_bias", "c_w_out", "final_norm")
    result = [total[0, 0], dx[None]]
    for part in range(4):
        result += [outs[k][part] for k in order]
    return tuple(result)
```

```python
import functools

import jax
import jax.numpy as jnp
from jax import lax
from jax.experimental import pallas as pl
from jax.experimental.pallas import tpu as pltpu
from jax.experimental.pallas import tpu_sc as plsc

F32 = jnp.float32
BF16 = jnp.bfloat16
SDS = jax.ShapeDtypeStruct
MESH_AXES = ("x", "y", "c")
N_DEV = 8

D_MODEL = 1024
DEPTH = 4
CHUNK = 64
D_FF = 4 * D_MODEL
RMS_EPS = 1e-6
RET_WIDTH = 512
RET_HEADS = 4
RET_HEAD_DIM = 128
RET_ROPE_BASE = 10000.0
GN_EPS = 1e-5
POOL_WIDTH = 512
POOL_WINDOWS = (2, 4, 8, 16)
POOL_HALO = 16
AB_IN_WIDTH = 4 * RET_WIDTH + POOL_WIDTH
ATT_HEADS = 16
ATT_HEAD_DIM = 64
LEFT_CHUNKS = 8
REL_CLIP = 128
N_REL = 2 * REL_CLIP + 1
NEG_INF = -1e30

ADAM_LR = 0.001
ADAM_B1 = 0.9
ADAM_B2 = 0.999
ADAM_EPS = 1e-08
ADAM_WD = 0.01
ADAM_STEP = 10

TOKEN_TILE = 512
ATT_Q_TILE = 256
ATT_BACK = LEFT_CHUNKS * CHUNK // ATT_Q_TILE
ATT_K_TILE = (ATT_BACK + 1) * ATT_Q_TILE
ATT_PAIRS = 4
ATT_DIAG = 1024
REL_PAD = 384
VMEM_LIMIT_MB = 56

NT = (((1,), (1,)), ((), ()))
TN = (((0,), (0,)), ((), ()))


def _params(semantics, **kw):
    return pltpu.CompilerParams(dimension_semantics=semantics,
                                vmem_limit_bytes=VMEM_LIMIT_MB * 2 ** 20, **kw)


def _dot(a, b, dims=None):
    if dims is None:
        return jnp.dot(a, b, preferred_element_type=F32)
    return lax.dot_general(a, b, dims, preferred_element_type=F32)


def _bf(v):
    return v.astype(BF16)


def _tile(n, t):
    return min(n, t)


def _norm_mm(x, gain, w, tn, z_dtype, relu2, name):
    s, d = x.shape
    nj = w.shape[0]
    tm = _tile(s, TOKEN_TILE)

    def body(x_ref, g_ref, w_ref, h_ref, z_ref, *a_ref):
        xv = x_ref[...]
        r = lax.rsqrt(jnp.mean(xv * xv, axis=-1, keepdims=True) + RMS_EPS)
        h = _bf(xv * r * g_ref[...])
        h_ref[...] = h
        cw = tn if tn <= 512 else 512
        for j in range(nj):
            for c in range(0, tn, cw):
                z = _dot(h, w_ref[j, :, c:c + cw])
                cols = slice(j * tn + c, j * tn + c + cw)
                z_ref[:, cols] = z.astype(z_ref.dtype)
                if relu2:
                    a_ref[0][:, cols] = _bf(jnp.square(jnp.maximum(z, 0.0)))

    n = nj * tn
    out_shape = [SDS((s, d), BF16), SDS((s, n), z_dtype)]
    out_specs = [pl.BlockSpec((tm, d), lambda i: (i, 0)), pl.BlockSpec((tm, n), lambda i: (i, 0))]
    if relu2:
        out_shape.append(SDS((s, n), BF16))
        out_specs.append(pl.BlockSpec((tm, n), lambda i: (i, 0)))
    return pl.pallas_call(
        body, name=name, grid=(s // tm,), out_shape=out_shape,
        in_specs=[pl.BlockSpec((tm, d), lambda i: (i, 0)),
                  pl.BlockSpec((1, d), lambda i: (0, 0)),
                  pl.BlockSpec((nj, d, tn), lambda i: (0, 0, 0))],
        out_specs=out_specs,
        compiler_params=_params(("parallel",)),
    )(x, gain, w)


def _mm_res(parts, w, res, name):
    s, d = res.shape
    tm = _tile(s, TOKEN_TILE)
    widths = [p.shape[1] for p in parts]

    def body(*refs):
        a_refs = refs[:len(parts)]
        w_ref, res_ref, o_ref = refs[len(parts):]
        acc = res_ref[...]
        off = 0
        for a_ref, k in zip(a_refs, widths):
            acc = acc + _dot(a_ref[...], w_ref[off:off + k, :])
            off += k
        o_ref[...] = acc

    return pl.pallas_call(
        body, name=name, grid=(s // tm,), out_shape=SDS((s, d), F32),
        in_specs=[pl.BlockSpec((tm, k), lambda i: (i, 0)) for k in widths]
        + [pl.BlockSpec(w.shape, lambda i: (0, 0)), pl.BlockSpec((tm, d), lambda i: (i, 0))],
        out_specs=pl.BlockSpec((tm, d), lambda i: (i, 0)),
        compiler_params=_params(("parallel",)),
    )(*parts, w, res)


def _mm_nt_rows(dy, w, z, name):
    s, d = dy.shape
    k = w.shape[0]
    tm = _tile(s, TOKEN_TILE)
    tk = _tile(k, 1024)

    def body(dy_ref, w_ref, *rest):
        o_ref = rest[-1]
        dyb = _bf(dy_ref[...])
        for j in range(k // tk):
            cols = slice(j * tk, (j + 1) * tk)
            da = _dot(dyb, w_ref[cols, :], NT)
            if z is not None:
                da = da * (2.0 * jnp.maximum(rest[0][:, cols].astype(F32), 0.0))
            o_ref[:, cols] = _bf(da)

    in_specs = [pl.BlockSpec((tm, d), lambda i: (i, 0)), pl.BlockSpec((k, d), lambda i: (0, 0))]
    args = [dy, w]
    if z is not None:
        in_specs.append(pl.BlockSpec((tm, k), lambda i: (i, 0)))
        args.append(z)
    return pl.pallas_call(
        body, name=name, grid=(s // tm,), out_shape=SDS((s, k), BF16),
        in_specs=in_specs, out_specs=pl.BlockSpec((tm, k), lambda i: (i, 0)),
        compiler_params=_params(("parallel",)),
    )(*args)


def _mm_nt_normbwd(dz, w, x, gain, dres, name):
    s, d = x.shape
    nj, _, nc = w.shape
    tm = _tile(s, TOKEN_TILE)

    def body(dz_ref, w_ref, x_ref, g_ref, dres_ref, dx_ref, dg_ref):
        dh = _dot(dz_ref[:, 0:nc], w_ref[0], NT)
        for j in range(1, nj):
            dh = dh + _dot(dz_ref[:, j * nc:(j + 1) * nc], w_ref[j], NT)
        xv = x_ref[...]
        r = lax.rsqrt(jnp.mean(xv * xv, axis=-1, keepdims=True) + RMS_EPS)
        xn = xv * r

        @pl.when(pl.program_id(0) == 0)
        def _():
            dg_ref[...] = jnp.zeros_like(dg_ref)

        dg_ref[...] += jnp.sum(dh * xn, axis=0, keepdims=True)
        dxh = dh * g_ref[...]
        dx_ref[...] = dres_ref[...] + r * (dxh - xn * jnp.mean(dxh * xn, axis=-1, keepdims=True))

    return pl.pallas_call(
        body, name=name, grid=(s // tm,), out_shape=[SDS((s, d), F32), SDS((1, d), F32)],
        in_specs=[pl.BlockSpec((tm, nj * nc), lambda i: (i, 0)),
                  pl.BlockSpec((nj, d, nc), lambda i: (0, 0, 0)),
                  pl.BlockSpec((tm, d), lambda i: (i, 0)),
                  pl.BlockSpec((1, d), lambda i: (0, 0)),
                  pl.BlockSpec((tm, d), lambda i: (i, 0))],
        out_specs=[pl.BlockSpec((tm, d), lambda i: (i, 0)), pl.BlockSpec((1, d), lambda i: (0, 0))],
        compiler_params=_params(("arbitrary",)),
    )(dz, w, x, gain, dres)


def _mm_tn(a, b, ka, nb, a_tiled, split, name, after=None):
    a_parts = list(a) if isinstance(a, (list, tuple)) else [a]
    s = a_parts[0].shape[0]
    tm = _tile(s, 2 * TOKEN_TILE)
    nm = s // tm
    nj = a_parts[0].shape[1] // ka if a_tiled and len(a_parts) == 1 else (1 if a_tiled else b.shape[1] // nb)
    axis, parts = split
    pr, pc = (ka // parts, nb) if axis == 0 else (ka, nb // parts)

    def body(*refs):
        a_refs, b_ref = refs[:len(a_parts)], refs[len(a_parts)]
        o_ref, acc = refs[-2:]
        m = pl.program_id(1)

        @pl.when(m == 0)
        def _():
            acc[...] = jnp.zeros_like(acc)

        av = a_refs[0][...] if len(a_refs) == 1 else jnp.concatenate([r[...] for r in a_refs], axis=1)
        acc[...] += _dot(_bf(av), _bf(b_ref[...]), TN)

        @pl.when(m == nm - 1)
        def _():
            for q in range(parts):
                piece = acc[q * pr:(q + 1) * pr, :] if axis == 0 else acc[:, q * pc:(q + 1) * pc]
                o_ref[q] = piece.astype(o_ref.dtype)

    return pl.pallas_call(
        body, name=name, grid=(nj, nm), out_shape=SDS((nj * parts, pr, pc), BF16),
        in_specs=([pl.BlockSpec((tm, ka), (lambda j, m: (m, j)) if a_tiled else (lambda j, m: (m, 0)))]
                  if len(a_parts) == 1 else [pl.BlockSpec((tm, p.shape[1]), lambda j, m: (m, 0)) for p in a_parts])
        + [pl.BlockSpec((tm, nb), (lambda j, m: (m, 0)) if a_tiled else (lambda j, m: (m, j)))]
        + [pl.BlockSpec(memory_space=pl.ANY)] * (after is not None),
        out_specs=pl.BlockSpec((parts, pr, pc), lambda j, m: (j, 0, 0)),
        scratch_shapes=[pltpu.VMEM((ka, nb), F32)],
        compiler_params=_params(("parallel", "arbitrary")),
    )(*a_parts, b, *([] if after is None else [after]))


def _final_loss(x, gain, target, name):
    s, d = x.shape
    tm = _tile(s, TOKEN_TILE)

    def body(x_ref, g_ref, t_ref, loss_ref, dx_ref, dg_ref):
        @pl.when(pl.program_id(0) == 0)
        def _():
            loss_ref[...] = jnp.zeros_like(loss_ref)
            dg_ref[...] = jnp.zeros_like(dg_ref)

        xv = x_ref[...]
        r = lax.rsqrt(jnp.mean(xv * xv, axis=-1, keepdims=True) + RMS_EPS)
        xn = xv * r
        err = xn * g_ref[...] - t_ref[...]
        loss_ref[...] += (0.5 / d) * jnp.sum(err * err)
        dy = err * (1.0 / d)
        dg_ref[...] += jnp.sum(dy * xn, axis=0, keepdims=True)
        dxh = dy * g_ref[...]
        dx_ref[...] = r * (dxh - xn * jnp.mean(dxh * xn, axis=-1, keepdims=True))

    return pl.pallas_call(
        body, name=name, grid=(s // tm,),
        out_shape=[SDS((8, 128), F32), SDS((s, d), F32), SDS((1, d), F32)],
        in_specs=[pl.BlockSpec((tm, d), lambda i: (i, 0)), pl.BlockSpec((1, d), lambda i: (0, 0)),
                  pl.BlockSpec((tm, d), lambda i: (i, 0))],
        out_specs=[pl.BlockSpec((8, 128), lambda i: (0, 0)), pl.BlockSpec((tm, d), lambda i: (i, 0)),
                   pl.BlockSpec((1, d), lambda i: (0, 0))],
        compiler_params=_params(("arbitrary",)),
    )(x, gain, target)


def _retention_tables(s):
    half = RET_HEAD_DIM // 2
    inv_freq = 1.0 / (RET_ROPE_BASE ** jnp.linspace(0.0, 1.0, half, dtype=F32))
    ang = jnp.arange(s, dtype=F32)[:, None] * inv_freq[None, :]
    cos, sin = jnp.cos(ang), jnp.sin(ang)
    cos_e = jnp.repeat(cos, 2, axis=-1)
    sin_s = jnp.stack([-sin, sin], axis=-1).reshape(s, RET_HEAD_DIM)
    log_g = jnp.log1p(-jnp.power(2.0, -5.0 - jnp.arange(RET_HEADS, dtype=F32)))
    pos = jnp.arange(CHUNK, dtype=F32)
    dmat = jnp.exp(jnp.abs(pos[:, None] - pos[None, :])[None] * log_g[:, None, None])
    qdec = jnp.exp((pos[None, :] + 1.0) * log_g[:, None])
    kdec = jnp.exp((CHUNK - 1.0 - pos[None, :]) * log_g[:, None])
    lam = jnp.exp(CHUNK * log_g)
    wide = (RET_HEADS, CHUNK, RET_HEAD_DIM)
    return dict(cos=cos_e, sin=sin_s, dmat=dmat,
                qdec=jnp.broadcast_to(qdec[:, :, None], wide),
                kdec=jnp.broadcast_to(kdec[:, :, None], wide),
                lam=jnp.broadcast_to(lam[:, None, None], (RET_HEADS, RET_HEAD_DIM, RET_HEAD_DIM)))


def _swap_pairs(t):
    lane = lax.broadcasted_iota(jnp.int32, t.shape, 1)
    return jnp.where(lane % 2 == 0, pltpu.roll(t, RET_HEAD_DIM - 1, 1), pltpu.roll(t, 1, 1))


def _head(h):
    return slice(h * RET_HEAD_DIM, (h + 1) * RET_HEAD_DIM)


def _ret_common_specs(tb, blk):
    zs = [pl.BlockSpec((tb, RET_WIDTH), functools.partial(lambda j, i: (blk(i), j), j)) for j in range(4)]
    tabs = [pl.BlockSpec((tb, RET_HEAD_DIM), lambda i: (blk(i), 0))] * 2
    consts = [pl.BlockSpec((1, RET_WIDTH), lambda i: (0, 0)),
              pl.BlockSpec((RET_HEADS, CHUNK, CHUNK), lambda i: (0, 0, 0)),
              pl.BlockSpec((RET_HEADS, CHUNK, RET_HEAD_DIM), lambda i: (0, 0, 0)),
              pl.BlockSpec((RET_HEADS, CHUNK, RET_HEAD_DIM), lambda i: (0, 0, 0)),
              pl.BlockSpec((RET_HEADS, RET_HEAD_DIM, RET_HEAD_DIM), lambda i: (0, 0, 0))]
    return zs + tabs + consts


def _ret_fwd(z, tabs, gn_gain, name):
    s = z.shape[0]
    tb = _tile(s, TOKEN_TILE)
    ncb = tb // CHUNK
    scale = RET_HEAD_DIM ** -0.5

    def body(q_ref, k_ref, v_ref, g_ref, cos_ref, sin_ref, gain_ref, dm_ref, qd_ref, kd_ref, lam_ref,
             o_ref, st_ref, ret_ref, s_scr, qr_scr, kr_scr):
        @pl.when(pl.program_id(0) == 0)
        def _():
            s_scr[...] = jnp.zeros_like(s_scr)

        cosv, sinv = cos_ref[...], sin_ref[...]
        for h in range(RET_HEADS):
            qh, kh = q_ref[:, _head(h)], k_ref[:, _head(h)]
            qr_scr[:, _head(h)] = qh * cosv + _swap_pairs(qh) * sinv
            kr_scr[:, _head(h)] = (kh * cosv + _swap_pairs(kh) * sinv) * scale

        def chunk(c, carry):
            rows = pl.ds(pl.multiple_of(c * CHUNK, CHUNK), CHUNK)
            for h in range(RET_HEADS):
                qc, kc, vc = qr_scr[rows, _head(h)], kr_scr[rows, _head(h)], v_ref[rows, _head(h)]
                a = _dot(_bf(qc), _bf(kc), NT) * dm_ref[h]
                st = s_scr[h]
                st_ref[c, h] = st
                o_ref[rows, _head(h)] = _dot(_bf(a), _bf(vc)) + _dot(_bf(qc * qd_ref[h]), _bf(st))
                s_scr[h] = st * lam_ref[h] + _dot(_bf(kc * kd_ref[h]), _bf(vc), TN)
            return carry

        lax.fori_loop(0, ncb, chunk, 0, unroll=4)
        for h in range(RET_HEADS):
            o = o_ref[:, _head(h)]
            mu = jnp.mean(o, axis=-1, keepdims=True)
            oc = o - mu
            y = oc * lax.rsqrt(jnp.mean(oc * oc, axis=-1, keepdims=True) + GN_EPS) * gain_ref[:, _head(h)]
            g = g_ref[:, _head(h)]
            ret_ref[:, _head(h)] = _bf(g / (1.0 + jnp.exp(-g)) * y)

    nc = s // CHUNK
    return pl.pallas_call(
        body, name=name, grid=(s // tb,),
        out_shape=[SDS((s, RET_WIDTH), F32), SDS((nc, RET_HEADS, RET_HEAD_DIM, RET_HEAD_DIM), F32),
                   SDS((s, RET_WIDTH), BF16)],
        in_specs=_ret_common_specs(tb, lambda i: i),
        out_specs=[pl.BlockSpec((tb, RET_WIDTH), lambda i: (i, 0)),
                   pl.BlockSpec((ncb, RET_HEADS, RET_HEAD_DIM, RET_HEAD_DIM), lambda i: (i, 0, 0, 0)),
                   pl.BlockSpec((tb, RET_WIDTH), lambda i: (i, 0))],
        scratch_shapes=[pltpu.VMEM((RET_HEADS, RET_HEAD_DIM, RET_HEAD_DIM), F32),
                        pltpu.VMEM((tb, RET_WIDTH), F32), pltpu.VMEM((tb, RET_WIDTH), F32)],
        compiler_params=_params(("arbitrary",)),
    )(z, z, z, z, tabs["cos"], tabs["sin"], gn_gain, tabs["dmat"], tabs["qdec"], tabs["kdec"], tabs["lam"])


def _ret_bwd(z, tabs, gn_gain, o_pre, states, du, name):
    s = z.shape[0]
    tb = _tile(s, TOKEN_TILE)
    ncb = tb // CHUNK
    nblk = s // tb
    scale = RET_HEAD_DIM ** -0.5
    rev = lambda i: nblk - 1 - i

    def body(q_ref, k_ref, v_ref, g_ref, cos_ref, sin_ref, gain_ref, dm_ref, qd_ref, kd_ref, lam_ref,
             o_ref, st_ref, dret_ref, dz_ref, dgain_ref, g_scr, qr_scr, kr_scr, do_scr, dq_scr, dk_scr):
        @pl.when(pl.program_id(0) == 0)
        def _():
            g_scr[...] = jnp.zeros_like(g_scr)
            dgain_ref[...] = jnp.zeros_like(dgain_ref)

        cosv, sinv = cos_ref[...], sin_ref[...]
        for h in range(RET_HEADS):
            hs = _head(h)
            qh, kh = q_ref[:, hs], k_ref[:, hs]
            qr_scr[:, hs] = qh * cosv + _swap_pairs(qh) * sinv
            kr_scr[:, hs] = (kh * cosv + _swap_pairs(kh) * sinv) * scale
            o = o_ref[:, hs]
            mu = jnp.mean(o, axis=-1, keepdims=True)
            oc = o - mu
            rstd = lax.rsqrt(jnp.mean(oc * oc, axis=-1, keepdims=True) + GN_EPS)
            yh = oc * rstd
            gain = gain_ref[:, hs]
            g = g_ref[:, hs]
            sg = 1.0 / (1.0 + jnp.exp(-g))
            dret = dret_ref[:, hs].astype(F32)
            dy = dret * (g * sg)
            dz_ref[:, 3 * RET_WIDTH + h * RET_HEAD_DIM:3 * RET_WIDTH + (h + 1) * RET_HEAD_DIM] = _bf(
                dret * (yh * gain) * (sg * (1.0 + g * (1.0 - sg))))
            dgain_ref[:, hs] += jnp.sum(dy * yh, axis=0, keepdims=True)
            dyh = dy * gain
            do_scr[:, hs] = rstd * (dyh - jnp.mean(dyh, axis=-1, keepdims=True)
                                    - yh * jnp.mean(dyh * yh, axis=-1, keepdims=True))

        def chunk(cc, carry):
            c = ncb - 1 - cc
            rows = pl.ds(pl.multiple_of(c * CHUNK, CHUNK), CHUNK)
            for h in range(RET_HEADS):
                hs = _head(h)
                qc, kc, vc, doc = _bf(qr_scr[rows, hs]), _bf(kr_scr[rows, hs]), _bf(v_ref[rows, hs]), _bf(do_scr[rows, hs])
                qdc, kdc = qd_ref[h], kd_ref[h]
                st, gs = _bf(st_ref[c, h]), g_scr[h]
                gsb = _bf(gs)
                dm = dm_ref[h]
                p = _bf(_dot(qc, kc, NT) * dm)
                da = _bf(_dot(doc, vc, NT) * dm)
                kt = _bf(kr_scr[rows, hs] * kdc)
                qt = _bf(qr_scr[rows, hs] * qdc)
                dz_ref[rows, 2 * RET_WIDTH + h * RET_HEAD_DIM:2 * RET_WIDTH + (h + 1) * RET_HEAD_DIM] = _bf(
                    _dot(p, doc, TN) + _dot(kt, gsb))
                dq_scr[rows, hs] = _dot(da, kc) + _dot(doc, st, NT) * qdc
                dk_scr[rows, hs] = _dot(da, qc, TN) + _dot(vc, gsb, NT) * kdc
                g_scr[h] = gs * lam_ref[h] + _dot(qt, doc, TN)
            return carry

        lax.fori_loop(0, ncb, chunk, 0, unroll=4)
        for h in range(RET_HEADS):
            hs = _head(h)
            dq, dk = dq_scr[:, hs], dk_scr[:, hs]
            dz_ref[:, h * RET_HEAD_DIM:(h + 1) * RET_HEAD_DIM] = _bf(dq * cosv - _swap_pairs(dq) * sinv)
            dz_ref[:, RET_WIDTH + h * RET_HEAD_DIM:RET_WIDTH + (h + 1) * RET_HEAD_DIM] = _bf(
                (dk * cosv - _swap_pairs(dk) * sinv) * scale)

    return pl.pallas_call(
        body, name=name, grid=(nblk,),
        out_shape=[SDS((s, AB_IN_WIDTH), BF16), SDS((1, RET_WIDTH), F32)],
        in_specs=_ret_common_specs(tb, rev)
        + [pl.BlockSpec((tb, RET_WIDTH), lambda i: (rev(i), 0)),
           pl.BlockSpec((ncb, RET_HEADS, RET_HEAD_DIM, RET_HEAD_DIM), lambda i: (rev(i), 0, 0, 0)),
           pl.BlockSpec((tb, RET_WIDTH), lambda i: (rev(i), 0))],
        out_specs=[pl.BlockSpec((tb, 4 * RET_WIDTH), lambda i: (rev(i), 0)),
                   pl.BlockSpec((1, RET_WIDTH), lambda i: (0, 0))],
        scratch_shapes=[pltpu.VMEM((RET_HEADS, RET_HEAD_DIM, RET_HEAD_DIM), F32)]
        + [pltpu.VMEM((tb, RET_WIDTH), F32)] * 5,
        compiler_params=_params(("arbitrary",)),
    )(z, z, z, z, tabs["cos"], tabs["sin"], gn_gain, tabs["dmat"], tabs["qdec"], tabs["kdec"], tabs["lam"],
      o_pre, states, du)


POOL_COL = 4 * RET_WIDTH // POOL_WIDTH


def _pooled(cur, prev, t0):
    tm = cur.shape[0]
    xx = jnp.concatenate([prev, cur], axis=0)
    sums = {1: xx}
    w = 1
    while w < POOL_WINDOWS[-1]:
        sums[2 * w] = sums[w] + pltpu.roll(sums[w], w, 0)
        w *= 2
    t = t0 + lax.broadcasted_iota(jnp.int32, (tm, 128), 0)
    outs = []
    for gi, w in enumerate(POOL_WINDOWS):
        cols = slice(gi * 128, (gi + 1) * 128)
        cnt = jnp.minimum(t + 1, w).astype(F32)
        outs.append(sums[w][POOL_HALO:, cols] / cnt - cur[:, cols])
    return outs


def _pool_fwd(z, w_pool, scale, name):
    s = z.shape[0]
    tm = _tile(s, TOKEN_TILE)
    hb = tm // POOL_HALO

    def body(p_ref, prev_ref, w_ref, sc_ref, o_ref):
        i = pl.program_id(0)
        prev = jnp.where(i > 0, prev_ref[...], 0.0)
        pooled = _pooled(p_ref[...], prev, i * tm)
        for gi in range(len(POOL_WINDOWS)):
            cols = slice(gi * 128, (gi + 1) * 128)
            o_ref[:, cols] = _bf(_dot(_bf(pooled[gi]), _bf(w_ref[gi])) * sc_ref[:, cols])

    return pl.pallas_call(
        body, name=name, grid=(s // tm,), out_shape=SDS((s, POOL_WIDTH), BF16),
        in_specs=[pl.BlockSpec((tm, POOL_WIDTH), lambda i: (i, POOL_COL)),
                  pl.BlockSpec((POOL_HALO, POOL_WIDTH), lambda i: (jnp.maximum(i * hb - 1, 0), POOL_COL)),
                  pl.BlockSpec(w_pool.shape, lambda i: (0, 0, 0)),
                  pl.BlockSpec((1, POOL_WIDTH), lambda i: (0, 0))],
        out_specs=pl.BlockSpec((tm, POOL_WIDTH), lambda i: (i, 0)),
        compiler_params=_params(("parallel",)),
    )(z, z, w_pool, scale)


def _pool_bwd(z, w_pool, scale, du, dz, name):
    s = z.shape[0]
    tm = _tile(s, TOKEN_TILE)
    hb = tm // POOL_HALO
    nblk = s // tm
    last_halo = s // POOL_HALO - 1

    def body(p_ref, prev_ref, w_ref, sc_ref, do_ref, don_ref, dz_ref, dp_ref, dw_ref, dsc_ref):
        i = pl.program_id(0)

        @pl.when(i == 0)
        def _():
            dw_ref[...] = jnp.zeros_like(dw_ref)
            dsc_ref[...] = jnp.zeros_like(dsc_ref)

        prev = jnp.where(i > 0, prev_ref[...], 0.0)
        pooled = _pooled(p_ref[...], prev, i * tm)
        dout = do_ref[...].astype(F32)
        dout_next = jnp.where(i < nblk - 1, don_ref[...].astype(F32), 0.0)
        sc = sc_ref[...]
        dmix = jnp.concatenate([dout * sc, dout_next * sc], axis=0)
        n = tm + POOL_HALO
        t = i * tm + lax.broadcasted_iota(jnp.int32, (n, 128), 0)
        for gi, w in enumerate(POOL_WINDOWS):
            cols = slice(gi * 128, (gi + 1) * 128)
            wg = _bf(w_ref[gi])
            pg = _bf(pooled[gi])
            dsc_ref[:, cols] += jnp.sum(dout[:, cols] * _dot(pg, wg), axis=0, keepdims=True)
            dw_ref[gi] += _dot(pg, _bf(dmix[:tm, cols]), TN)
            dpool = _dot(_bf(dmix[:, cols]), wg, NT)
            acc = dpool / jnp.minimum(t + 1, w).astype(F32)
            step = 1
            while step < w:
                acc = acc + pltpu.roll(acc, n - step, 0)
                step *= 2
            dp_ref[:, cols] = _bf(acc[:tm] - dpool[:tm])

    return pl.pallas_call(
        body, name=name, grid=(nblk,),
        out_shape=[SDS(dz.shape, BF16), SDS(w_pool.shape, F32), SDS((1, POOL_WIDTH), F32)],
        in_specs=[pl.BlockSpec((tm, POOL_WIDTH), lambda i: (i, POOL_COL)),
                  pl.BlockSpec((POOL_HALO, POOL_WIDTH), lambda i: (jnp.maximum(i * hb - 1, 0), POOL_COL)),
                  pl.BlockSpec(w_pool.shape, lambda i: (0, 0, 0)),
                  pl.BlockSpec((1, POOL_WIDTH), lambda i: (0, 0)),
                  pl.BlockSpec((tm, POOL_WIDTH), lambda i: (i, 1)),
                  pl.BlockSpec((POOL_HALO, POOL_WIDTH), lambda i: (jnp.minimum((i + 1) * hb, last_halo), 1)),
                  pl.BlockSpec(memory_space=pl.ANY)],
        out_specs=[pl.BlockSpec((tm, POOL_WIDTH), lambda i: (i, POOL_COL)),
                   pl.BlockSpec(w_pool.shape, lambda i: (0, 0, 0)),
                   pl.BlockSpec((1, POOL_WIDTH), lambda i: (0, 0))],
        input_output_aliases={6: 0},
        compiler_params=_params(("arbitrary",)),
    )(z, z, w_pool, scale, du, du, dz)


def _rel_onehot():
    r = lax.broadcasted_iota(jnp.int32, (REL_PAD, ATT_DIAG), 0)
    c = lax.broadcasted_iota(jnp.int32, (REL_PAD, ATT_DIAG), 1)
    rel = jnp.where(c < ATT_K_TILE, jnp.clip(LEFT_CHUNKS * CHUNK - c, -REL_CLIP, REL_CLIP) + REL_CLIP,
                    2 * REL_CLIP)
    return (rel == r).astype(BF16)


def _split3(v):
    hi = _bf(v)
    r1 = v - hi.astype(F32)
    mid = _bf(r1)
    return hi, mid, _bf(r1 - mid.astype(F32))


def _skew(v, sign):
    row = lax.broadcasted_iota(jnp.int32, v.shape, 0)
    bit = 1
    while bit < ATT_Q_TILE:
        shift = bit if sign > 0 else ATT_DIAG - bit
        v = jnp.where((row & bit) != 0, pltpu.roll(v, shift, 1), v)
        bit *= 2
    return v


def _attn_bias(rel_bias, name):
    def body(t_ref, o_ref):
        oh = _rel_onehot()
        base = sum(_dot(part, oh) for part in _split3(t_ref[0]))
        full = _skew(jnp.broadcast_to(base[0:1], (ATT_Q_TILE, ATT_DIAG)), +1)[:, :ATT_K_TILE]
        qc = lax.broadcasted_iota(jnp.int32, full.shape, 0) // CHUNK
        kc = lax.broadcasted_iota(jnp.int32, full.shape, 1) // CHUNK
        o_ref[0] = jnp.where((kc >= qc) & (kc <= qc + LEFT_CHUNKS), full, NEG_INF)

    t8 = jnp.broadcast_to(rel_bias[:, None, :], (ATT_HEADS, 8, REL_PAD))
    return pl.pallas_call(
        body, name=name, grid=(ATT_HEADS,), out_shape=SDS((ATT_HEADS, ATT_Q_TILE, ATT_K_TILE), F32),
        in_specs=[pl.BlockSpec((1, 8, REL_PAD), lambda h: (h, 0, 0))],
        out_specs=pl.BlockSpec((1, ATT_Q_TILE, ATT_K_TILE), lambda h: (h, 0, 0)),
        compiler_params=_params(("parallel",)),
    )(t8)


def _attn_dbias(dbias, name):
    def body(d_ref, o_ref):
        pad = jnp.zeros((ATT_Q_TILE, ATT_DIAG - ATT_K_TILE), F32)
        diag = _skew(jnp.concatenate([d_ref[0], pad], axis=1), -1)
        col = jnp.sum(diag, axis=0, keepdims=True)
        oh = _rel_onehot()
        col8 = jnp.broadcast_to(col, (8, ATT_DIAG))
        o_ref[0] = sum(_dot(part, oh, NT) for part in _split3(col8))

    out = pl.pallas_call(
        body, name=name, grid=(ATT_HEADS,), out_shape=SDS((ATT_HEADS, 8, REL_PAD), F32),
        in_specs=[pl.BlockSpec((1, ATT_Q_TILE, ATT_K_TILE), lambda h: (h, 0, 0))],
        out_specs=pl.BlockSpec((1, 8, REL_PAD), lambda h: (h, 0, 0)),
        compiler_params=_params(("parallel",)),
    )(dbias)
    return out[:, 0, :]


ATT_WIDTH = 128 * ATT_PAIRS
ATT_GROUPS = D_MODEL // ATT_WIDTH


def _attn_specs(nq):
    def tile(off, back):
        return pl.BlockSpec((ATT_Q_TILE, ATT_WIDTH),
                            lambda g, i: (jnp.maximum(jnp.minimum(i, nq - 1) - back, 0), off + g))

    backs = [ATT_BACK - b for b in range(ATT_BACK + 1)]
    return ([tile(0, 0)] + [tile(ATT_GROUPS, b) for b in backs] + [tile(2 * ATT_GROUPS, b) for b in backs]
            + [pl.BlockSpec((2 * ATT_PAIRS, ATT_Q_TILE, ATT_K_TILE), lambda g, i: (g, 0, 0))])


def _attn_weights(qh, k2, bias, i, masked):
    sc = _dot(qh, k2, NT) + bias
    if masked:
        kpos = (i - ATT_BACK) * ATT_Q_TILE + lax.broadcasted_iota(jnp.int32, sc.shape, 1)
        sc = jnp.where(kpos >= 0, sc, NEG_INF)
    e = jnp.exp(sc - jnp.max(sc, axis=-1, keepdims=True))
    return e, 1.0 / jnp.sum(e, axis=-1, keepdims=True)


def _first_head():
    return lax.broadcasted_iota(jnp.int32, (ATT_Q_TILE, 128), 1) < ATT_HEAD_DIM


def _head_rows(hh):
    return slice(hh * ATT_HEAD_DIM, (hh + 1) * ATT_HEAD_DIM)


def _pair_operands(q_ref, k_refs, v_refs, pp):
    cols = slice(pp * 128, (pp + 1) * 128)
    q2 = q_ref[:, cols] * ATT_HEAD_DIM ** -0.5
    k2 = jnp.concatenate([r[:, cols] for r in k_refs], axis=0)
    v2 = jnp.concatenate([r[:, cols] for r in v_refs], axis=0)
    return cols, q2, k2, v2


def _attn_fwd(z, bias, name):
    s = z.shape[0]
    nq = s // ATT_Q_TILE
    nt = ATT_BACK + 1

    def body(q_ref, *rest):
        k_refs, v_refs, (b_ref, o_ref) = rest[:nt], rest[nt:2 * nt], rest[2 * nt:]
        i = pl.program_id(1)
        first = _first_head()

        def compute(masked):
            for pp in range(ATT_PAIRS):
                cols, q2, k2, v2 = _pair_operands(q_ref, k_refs, v_refs, pp)
                outs = []
                for hh in range(2):
                    qh = jnp.where(first if hh == 0 else ~first, q2, 0)
                    e, inv = _attn_weights(qh, k2, b_ref[2 * pp + hh], i, masked)
                    outs.append(_dot(_bf(e), v2) * inv)
                o_ref[:, cols] = _bf(jnp.where(first, outs[0], outs[1]))

        pl.when(i < ATT_BACK)(lambda: compute(True))
        pl.when(i >= ATT_BACK)(lambda: compute(False))

    return pl.pallas_call(
        body, name=name, grid=(ATT_GROUPS, nq), out_shape=SDS((s, D_MODEL), BF16),
        in_specs=_attn_specs(nq),
        out_specs=pl.BlockSpec((ATT_Q_TILE, ATT_WIDTH), lambda g, i: (i, g)),
        compiler_params=_params(("parallel", "parallel")),
    )(*([z] * (1 + 2 * nt)), bias)


def _attn_bwd(z, bias, o, do, name):
    s = z.shape[0]
    nq = s // ATT_Q_TILE
    nt = ATT_BACK + 1

    def body(q_ref, *rest):
        k_refs, v_refs = rest[:nt], rest[nt:2 * nt]
        b_ref, o_ref, do_ref, dq_ref, dk_ref, dv_ref, db_ref, dk_acc, dv_acc = rest[2 * nt:]
        i = pl.program_id(1)
        first = _first_head()

        @pl.when(i == 0)
        def _():
            db_ref[...] = jnp.zeros_like(db_ref)
            dk_acc[...] = jnp.zeros_like(dk_acc)
            dv_acc[...] = jnp.zeros_like(dv_acc)

        def compute(masked):
            for pp in range(ATT_PAIRS):
                cols, q2, k2, v2 = _pair_operands(q_ref, k_refs, v_refs, pp)
                do2 = do_ref[:, cols].astype(F32)
                prod = do2 * o_ref[:, cols].astype(F32)
                q_t, k_t = q2.T, k2.T
                dq_t, dk_t, dv_t = [], [], []
                for hh in range(2):
                    mine = first if hh == 0 else ~first
                    rows = _head_rows(hh)
                    qh = jnp.where(mine, q2, 0)
                    e, inv = _attn_weights(qh, k2, b_ref[2 * pp + hh], i, masked)
                    delta = jnp.sum(jnp.where(mine, prod, 0.0), axis=-1, keepdims=True) * inv
                    doh = _bf(jnp.where(mine, do2 * inv, 0.0))
                    ds = e * (_dot(doh, v2, NT) - delta)
                    db_ref[2 * pp + hh] += ds
                    dsb = _bf(ds)
                    dq_t.append(_dot(k_t[rows], dsb, NT))
                    dk_t.append(_dot(q_t[rows], dsb))
                    dv_t.append(_dot(doh.T[rows], _bf(e)))
                dq_ref[:, cols] = _bf(jnp.concatenate(dq_t, axis=0).T * ATT_HEAD_DIM ** -0.5)
                dk, dv = jnp.concatenate(dk_t, axis=0).T, jnp.concatenate(dv_t, axis=0).T
                for b in range(nt):
                    slot = (i + b + 1) % nt
                    rows = slice(b * ATT_Q_TILE, (b + 1) * ATT_Q_TILE)
                    if b < ATT_BACK:
                        dk_acc[slot, :, cols] += dk[rows]
                        dv_acc[slot, :, cols] += dv[rows]
                    else:
                        dk_acc[slot, :, cols] = dk[rows]
                        dv_acc[slot, :, cols] = dv[rows]

        pl.when(i < ATT_BACK)(lambda: compute(True))
        pl.when((i >= ATT_BACK) & (i < nq))(lambda: compute(False))
        done = (i + 1) % nt
        dk_ref[...] = _bf(dk_acc[done])
        dv_ref[...] = _bf(dv_acc[done])

    tile = pl.BlockSpec((ATT_Q_TILE, ATT_WIDTH), lambda g, i: (jnp.minimum(i, nq - 1), g))
    late = pl.BlockSpec((ATT_Q_TILE, ATT_WIDTH), lambda g, i: (jnp.maximum(i - ATT_BACK, 0), g))
    ring = pltpu.VMEM((nt, ATT_Q_TILE, ATT_WIDTH), F32)
    return pl.pallas_call(
        body, name=name, grid=(ATT_GROUPS, nq + ATT_BACK),
        out_shape=[SDS((s, D_MODEL), BF16)] * 3 + [SDS((ATT_HEADS, ATT_Q_TILE, ATT_K_TILE), F32)],
        in_specs=_attn_specs(nq) + [tile, tile],
        out_specs=[tile, late, late, pl.BlockSpec((2 * ATT_PAIRS, ATT_Q_TILE, ATT_K_TILE), lambda g, i: (g, 0, 0))],
        scratch_shapes=[ring, ring],
        compiler_params=_params(("parallel", "arbitrary")),
    )(*([z] * (1 + 2 * nt)), bias, o, do)


FWD_GROUPS = (
    (("ab_w_in", 0),),
    (("ab_w_out", 0), ("w_ffn_in", 0)),
    (("w_ffn_out", 0),),
    (("c_w_qkv", 0),),
    (("c_w_out", 0), ("w_ffn_in", 1), ("w_ffn_out", 1), ("ab_w_in", 1)),
    (("ab_w_out", 1), ("w_ffn_in", 2), ("w_ffn_out", 2), ("c_w_qkv", 1)),
    (("c_w_out", 1), ("w_ffn_in", 3), ("w_ffn_out", 3)),
)


def _local_step(x, target, small, comm):
    s = x.shape[0]
    tabs = _retention_tables(s)
    saved, w = [], comm.weight
    for layer in range(DEPTH):
        i = layer // 2
        sv = {"x0": x}
        g_mix = small["mix_norm"][layer:layer + 1]
        if layer % 2 == 0:
            sv["h1"], sv["z"] = _norm_mm(x, g_mix, w("ab_w_in", i, x), AB_IN_WIDTH, F32, False, "ab_in_fwd")
            gn = small["ab_gn_gain"][i:i + 1]
            sv["o_pre"], sv["states"], ret = _ret_fwd(sv["z"], tabs, gn, "ret_fwd")
            pool = _pool_fwd(sv["z"], small["ab_w_pool"][i], small["ab_pool_scale"][i:i + 1], "pool_fwd")
            sv["u"] = (ret, pool)
            x = _mm_res([ret, pool], w("ab_w_out", i, ret), x, "ab_out_fwd")
        else:
            sv["h1"], sv["z"] = _norm_mm(x, g_mix, w("c_w_qkv", i, x), 3 * D_MODEL, BF16, False, "qkv_fwd")
            rb = jnp.pad(small["c_rel_bias"][i], ((0, 0), (0, REL_PAD - N_REL)))
            sv["bias"] = _attn_bias(rb, "attn_bias")
            sv["o"] = _attn_fwd(sv["z"], sv["bias"], "attn_fwd")
            x = _mm_res([sv["o"]], w("c_w_out", i, sv["o"]), x, "c_out_fwd")
        sv["x1"] = x
        sv["h2"], sv["z1"], sv["a"] = _norm_mm(x, small["ffn_norm"][layer:layer + 1], w("w_ffn_in", layer, x),
                                               D_FF // N_DEV, BF16, True, "ffn_in_fwd")
        x = _mm_res([sv["a"]], w("w_ffn_out", layer, sv["a"]), x, "ffn_out_fwd")
        saved.append(sv)

    loss, dx, d_final = _final_loss(x, small["final_norm"][None, :], target, "final_loss")

    gs = {k: [None] * DEPTH for k in ("mix_norm", "ffn_norm")}
    for k in ("ab_gn_gain", "ab_w_pool", "ab_pool_scale", "c_rel_bias"):
        gs[k] = [None] * (DEPTH // 2)
    gs["final_norm"] = d_final[0]
    tok = jnp.zeros((), F32)
    for layer in reversed(range(DEPTH)):
        i = layer // 2
        sv = saved[layer]
        dz1 = _mm_nt_rows(dx, w("w_ffn_out", layer), sv["z1"], "ffn_out_bwd")
        gw = {("w_ffn_out", layer): _mm_tn(sv["a"], dx, 1024, D_MODEL, True, (0, 2), "ffn_out_dw"),
              ("w_ffn_in", layer): _mm_tn(sv["h2"], dz1, D_MODEL, 1024, False, (1, 2), "ffn_in_dw",
                                          comm.after() if layer == 0 else None)}
        dx, dg = _mm_nt_normbwd(dz1, w("w_ffn_in", layer), sv["x1"], small["ffn_norm"][layer:layer + 1] + tok, dx,
                                "ffn_in_bwd")
        gs["ffn_norm"][layer] = dg[0]
        if layer == 0:
            tok = comm.send(gw)
            gw = {}
        g_mix = small["mix_norm"][layer:layer + 1] + tok
        if layer % 2 == 0:
            du = _mm_nt_rows(dx, w("ab_w_out", i), None, "mix_out_bwd")
            gw["ab_w_out", i] = _mm_tn(sv["u"], dx, D_MODEL, D_MODEL, True, (0, N_DEV), "mix_out_dw")
            gn = small["ab_gn_gain"][i:i + 1]
            dz, dgn = _ret_bwd(sv["z"], tabs, gn, sv["o_pre"], sv["states"], du, "ret_bwd")
            dz, dwp, dsc = _pool_bwd(sv["z"], small["ab_w_pool"][i], small["ab_pool_scale"][i:i + 1], du, dz, "pool_bwd")
            gs["ab_gn_gain"][i], gs["ab_w_pool"][i], gs["ab_pool_scale"][i] = dgn[0], dwp, dsc[0]
            gw["ab_w_in", i] = _to_shard_major(_mm_tn(sv["h1"], dz, D_MODEL, AB_IN_WIDTH // 2, False, (1, 1), "ab_in_dw",
                                                      comm.after()))
            dx, dg = _mm_nt_normbwd(dz, w("ab_w_in", i), sv["x0"], g_mix, dx, "ab_in_bwd")
        else:
            do = _mm_nt_rows(dx, w("c_w_out", i), None, "mix_out_bwd")
            gw["c_w_out", i] = _mm_tn(sv["o"], dx, D_MODEL, D_MODEL, True, (0, N_DEV), "mix_out_dw")
            dq, dk, dv, dbias = _attn_bwd(sv["z"], sv["bias"], sv["o"], do, "attn_bwd")
            gs["c_rel_bias"][i] = _attn_dbias(dbias, "attn_dbias")[:, :N_REL]
            dz = jnp.concatenate([dq, dk, dv], axis=1)
            gw["c_w_qkv", i] = _mm_tn(sv["h1"], dz, D_MODEL, 768, False, (1, 2), "qkv_dw", comm.after())
            dx, dg = _mm_nt_normbwd(dz, w("c_w_qkv", i), sv["x0"], g_mix, dx, "qkv_bwd")
        gs["mix_norm"][layer] = dg[0]
        if layer > 0:
            tok = comm.send(gw)
    gsmall = {k: (jnp.stack(v) if isinstance(v, list) else v) for k, v in gs.items()}
    return loss, dx, gw, gsmall


BIG = ("w_ffn_in", "w_ffn_out", "ab_w_in", "ab_w_out", "c_w_qkv", "c_w_out")
SMALL = ("mix_norm", "ffn_norm", "ab_gn_gain", "ab_w_pool", "ab_pool_scale", "c_rel_bias", "final_norm")
N_PEERS = N_DEV - 1
FLIPS = [(fx, fy, fc) for fx in (0, 1) for fy in (0, 1) for fc in (0, 1)][1:]


def _peers():
    x, y, c = (lax.axis_index(a) for a in MESH_AXES)
    peers = []
    for fx, fy, fc in FLIPS:
        px, py, pc = (1 - x if fx else x), (1 - y if fy else y), (1 - c if fc else c)
        peers.append(((px, py, pc), 4 * px + 2 * py + pc))
    return 4 * x + 2 * y + c, peers


def _exchange(srcs, by_slot, name, collective_id):
    n = len(srcs)
    src_refs = [jax.new_ref(a, memory_space=pltpu.MemorySpace.HBM) for a in srcs]
    land_refs = [jax.empty_ref(SDS((N_DEV,) + (a.shape[1:] if slotted else a.shape), a.dtype),
                               memory_space=pltpu.MemorySpace.HBM) for a, slotted in zip(srcs, by_slot)]

    @pl.kernel(mesh=plsc.ScalarSubcoreMesh(axis_name="sequencer", num_cores=1), name=name,
               scratch_types=(pltpu.SemaphoreType.DMA((n * N_PEERS,)), pltpu.SemaphoreType.DMA((n * N_PEERS,)),
                              pltpu.SemaphoreType.DMA((n,))),
               compiler_params=pltpu.CompilerParams(collective_id=collective_id))
    def launch(send_sems, recv_sems, local_sems):
        me, peers = _peers()
        barrier = pltpu.get_barrier_semaphore()
        for pos, _ in peers:
            pl.semaphore_signal(barrier, inc=1, device_id=pos, device_id_type=pl.DeviceIdType.MESH)
        pl.semaphore_wait(barrier, N_PEERS)
        waits = []
        for k in range(n):
            own = pltpu.make_async_copy(src_refs[k].at[me] if by_slot[k] else src_refs[k], land_refs[k].at[me],
                                        local_sems.at[k])
            own.start()
            waits.append(own.wait)
            for rel, (pos, slot) in enumerate(peers):
                src = src_refs[k].at[slot] if by_slot[k] else src_refs[k]
                sems = dict(send_sem=send_sems.at[k * N_PEERS + rel], recv_sem=recv_sems.at[k * N_PEERS + rel],
                            device_id=pos, device_id_type=pl.DeviceIdType.MESH)
                send = pltpu.make_async_remote_copy(src_ref=src, dst_ref=land_refs[k].at[me], **sems)
                send.start()
                arrival = pltpu.make_async_remote_copy(src_ref=src, dst_ref=land_refs[k].at[slot], **sems)
                waits += [send.wait_send, arrival.wait_recv]
        for wait in waits:
            wait()

    launch()
    return [r[...] for r in land_refs]


def _cast_group(weights, keys, token, name):
    def body(*refs):
        n = len(keys)
        for i_ref, o_ref in zip(refs[:n], refs[n + 1:]):
            o_ref[...] = _bf(i_ref[...])

    def layer_spec(shape, l):
        return pl.BlockSpec((None,) + shape[1:], lambda i: (l, 0, 0))

    whole = lambda shape: pl.BlockSpec(shape, lambda i: (0, 0))
    ins = [weights[k] for k, _ in keys]
    return pl.pallas_call(
        body, name=name, grid=(1,), out_shape=[SDS(w.shape[1:], BF16) for w in ins],
        in_specs=[layer_spec(w.shape, l) for w, (_, l) in zip(ins, keys)] + [pl.BlockSpec(memory_space=pl.ANY)],
        out_specs=[whole(w.shape[1:]) for w in ins],
        compiler_params=_params(("arbitrary",)),
    )(*ins, token)


def _to_shard_major(g):
    nj, ka, nb = g.shape
    full = jnp.transpose(g, (1, 0, 2)).reshape(ka, N_DEV, nj * nb // N_DEV)
    return jnp.transpose(full, (1, 0, 2))


def _from_gathered(name, g):
    if name in ("w_ffn_out", "ab_w_out", "c_w_out"):
        return g.reshape(g.shape[0] * g.shape[1], g.shape[2])
    if name in ("ab_w_in", "c_w_qkv"):
        return jnp.transpose(g, (1, 0, 2)).reshape(1, g.shape[1], N_DEV * g.shape[2])
    return g


class _Comm:
    def __init__(self, weights):
        self.weights_f32 = weights
        self.gathered = {}
        self.got = {}
        self.calls = 0
        self.ended = None
        self.opened = -1

    def _exchange(self, srcs, by_slot, name):
        self.calls += 1
        got = _exchange(srcs, by_slot, name, self.calls)
        self.ended = got[0][(0,) * got[0].ndim].astype(F32) * 0.0
        return got

    def _gather(self, group, at):
        keys = FWD_GROUPS[group]
        token = jnp.zeros((8, 128), F32) + at[(0,) * at.ndim].astype(F32) * 0.0 + (0.0 if self.ended is None else self.ended)
        shards = _cast_group(self.weights_f32, keys, token, "cast_%d" % group)
        got = self._exchange(shards, [False] * len(keys), "gather_%d" % group)
        self.gathered.update((k, _from_gathered(k[0], arr)) for k, arr in zip(keys, got))

    def weight(self, name, layer, at=None):
        if (name, layer) not in self.gathered:
            self._gather(0, at)
        group = next(g for g, keys in enumerate(FWD_GROUPS) if (name, layer) in keys)
        if group == self.opened + 1:
            self.opened = group
            if group + 1 < len(FWD_GROUPS):
                self._gather(group + 1, at)
        return self.gathered[name, layer]

    def after(self):
        return jnp.zeros((8, 128), F32) + self.ended

    def send(self, grads, shared=None):
        shared = shared or {}
        keys = list(grads) + list(shared)
        srcs = list(grads.values()) + list(shared.values())
        got = self._exchange(srcs, [True] * len(grads) + [False] * len(shared), "scatter_%d" % self.calls)
        self.got.update(zip(keys, got))
        return sum(g[0, 0, 0].astype(F32) * 0.0 for g in grads.values())

    def received(self):
        return self.got


def _adamw_math(g, w, m, v):
    m2 = ADAM_B1 * m + (1.0 - ADAM_B1) * g
    v2 = ADAM_B2 * v + (1.0 - ADAM_B2) * jnp.square(g)
    m_hat = m2 / (1.0 - ADAM_B1 ** ADAM_STEP)
    v_hat = v2 / (1.0 - ADAM_B2 ** ADAM_STEP)
    delta = -ADAM_LR * (m_hat / (jnp.sqrt(v_hat) + ADAM_EPS) + ADAM_WD * w)
    return delta, m2, v2


def _adamw(recv, w, m, v, name):
    nl, r, c = w.shape
    tr = _tile(r, 256)

    def body(*refs):
        g_refs = refs[:nl]
        w_ref, m_ref, v_ref, go_ref, d_ref, mo_ref, vo_ref = refs[nl:]
        for l in range(nl):
            @pl.when(pl.program_id(0) == l)
            def _():
                g = g_refs[l][0].astype(F32)
                for p in range(1, N_DEV):
                    g = g + g_refs[l][p].astype(F32)
                go_ref[...] = g
                d_ref[...], mo_ref[...], vo_ref[...] = _adamw_math(g, w_ref[...], m_ref[...], v_ref[...])

    def recv_spec(l):
        return pl.BlockSpec((N_DEV, tr, c), lambda layer, i: (0, jnp.where(layer == l, i, 0), 0))

    blk = pl.BlockSpec((None, tr, c), lambda l, i: (l, i, 0))
    return pl.pallas_call(
        body, name=name, grid=(nl, r // tr), out_shape=[SDS(w.shape, F32)] * 4,
        in_specs=[recv_spec(l) for l in range(nl)] + [blk, blk, blk],
        out_specs=[blk] * 4,
        compiler_params=_params(("arbitrary", "arbitrary")),
    )(*recv, w, m, v)


def _adamw_small(recv, loss_parts, w, m, v, name):
    n = len(w)

    def total(ref):
        t = ref[0]
        for p in range(1, N_DEV):
            t = t + ref[p]
        return t

    def body(*refs):
        g_refs, loss_ref = refs[:n], refs[n]
        w_refs, m_refs, v_refs = refs[n + 1:2 * n + 1], refs[2 * n + 1:3 * n + 1], refs[3 * n + 1:4 * n + 1]
        outs = refs[4 * n + 1:]
        for i in range(n):
            g = total(g_refs[i])
            outs[4 * i][...] = g
            outs[4 * i + 1][...], outs[4 * i + 2][...], outs[4 * i + 3][...] = _adamw_math(
                g, w_refs[i][...], m_refs[i][...], v_refs[i][...])
        outs[4 * n][...] = total(loss_ref)

    out_shape = [SDS(p.shape, F32) for p in w for _ in range(4)] + [SDS(loss_parts.shape[1:], F32)]
    outs = pl.pallas_call(body, name=name, out_shape=out_shape,
                          compiler_params=_params(None))(*recv, loss_parts, *w, *m, *v)
    return [outs[4 * i:4 * i + 4] for i in range(n)], outs[-1]


def kernel(x, mix_norm, ffn_norm, w_ffn_in, w_ffn_out, ab_w_in, ab_gn_gain, ab_w_pool, ab_pool_scale, ab_w_out, c_w_qkv, c_rel_bias, c_w_out, final_norm, loss_target, m_mix_norm, m_ffn_norm, m_w_ffn_in, m_w_ffn_out, m_ab_w_in, m_ab_gn_gain, m_ab_w_pool, m_ab_pool_scale, m_ab_w_out, m_c_w_qkv, m_c_rel_bias, m_c_w_out, m_final_norm, v_mix_norm, v_ffn_norm, v_w_ffn_in, v_w_ffn_out, v_ab_w_in, v_ab_gn_gain, v_ab_w_pool, v_ab_pool_scale, v_ab_w_out, v_c_w_qkv, v_c_rel_bias, v_c_w_out, v_final_norm):
    args = dict(locals())
    weights = {k: args[k] for k in BIG + SMALL}
    moments_m = {k: args["m_" + k] for k in BIG + SMALL}
    moments_v = {k: args["v_" + k] for k in BIG + SMALL}

    small = {k: weights[k] for k in SMALL}
    rows = lambda a: a.reshape(1, -1) if a.ndim == 1 else a

    comm = _Comm(weights)
    loss, dx, last_grads, gsmall = _local_step(x[0], loss_target[0], small, comm)
    comm.send(last_grads, {**{k: rows(gsmall[k]) for k in SMALL}, "loss": loss})
    recv = comm.received()

    outs = {}
    for k in BIG:
        layers = [recv[k, l] for l in range(weights[k].shape[0])]
        outs[k] = _adamw(layers, weights[k], moments_m[k], moments_v[k], "adamw_" + k)
    updated, total = _adamw_small([recv[k] for k in SMALL], recv["loss"], [rows(small[k]) for k in SMALL],
                                  [rows(moments_m[k]) for k in SMALL], [rows(moments_v[k]) for k in SMALL], "adamw_small")
    for k, parts in zip(SMALL, updated):
        outs[k] = [p.reshape(small[k].shape) for p in parts]

    order = SMALL[:2] + BIG[:2] + ("ab_w_in", "ab_gn_gain", "ab_w_pool", "ab_pool_scale", "ab_w_out",
                                   "c_w_qkv", "c_rel_bias", "c_w_out", "final_norm")
    result = [total[0, 0], dx[None]]
    for part in range(4):
        result += [outs[k][part] for k in order]
    return tuple(result)
```

```python
import functools

import jax
import jax.numpy as jnp
from jax import lax
from jax.experimental import pallas as pl
from jax.experimental.pallas import tpu as pltpu
from jax.experimental.pallas import tpu_sc as plsc

F32 = jnp.float32
BF16 = jnp.bfloat16
SDS = jax.ShapeDtypeStruct
MESH_AXES = ("x", "y", "c")
N_DEV = 8

D_MODEL = 1024
DEPTH = 4
CHUNK = 64
D_FF = 4 * D_MODEL
RMS_EPS = 1e-6
RET_WIDTH = 512
RET_HEADS = 4
RET_HEAD_DIM = 128
RET_ROPE_BASE = 10000.0
GN_EPS = 1e-5
POOL_WIDTH = 512
POOL_WINDOWS = (2, 4, 8, 16)
POOL_HALO = 16
AB_IN_WIDTH = 4 * RET_WIDTH + POOL_WIDTH
ATT_HEADS = 16
ATT_HEAD_DIM = 64
LEFT_CHUNKS = 8
REL_CLIP = 128
N_REL = 2 * REL_CLIP + 1
NEG_INF = -1e30

ADAM_LR = 0.001
ADAM_B1 = 0.9
ADAM_B2 = 0.999
ADAM_EPS = 1e-08
ADAM_WD = 0.01
ADAM_STEP = 10

TOKEN_TILE = 512
ATT_Q_TILE = 256
ATT_BACK = LEFT_CHUNKS * CHUNK // ATT_Q_TILE
ATT_K_TILE = (ATT_BACK + 1) * ATT_Q_TILE
ATT_PAIRS = 4
ATT_DIAG = 1024
REL_PAD = 384
VMEM_LIMIT_MB = 56

NT = (((1,), (1,)), ((), ()))
TN = (((0,), (0,)), ((), ()))


def _params(semantics, **kw):
    return pltpu.CompilerParams(dimension_semantics=semantics,
                                vmem_limit_bytes=VMEM_LIMIT_MB * 2 ** 20, **kw)


def _dot(a, b, dims=None):
    if dims is None:
        return jnp.dot(a, b, preferred_element_type=F32)
    return lax.dot_general(a, b, dims, preferred_element_type=F32)


def _bf(v):
    return v.astype(BF16)


def _tile(n, t):
    return min(n, t)


MXU_WIDTH = 256


def _mxu_group(nj, tn):
    return 2 if tn % MXU_WIDTH and (2 * tn) % MXU_WIDTH == 0 and nj % 2 == 0 else 1


def _w_tiles(w_ref, j, group):
    return w_ref[j] if group == 1 else jnp.concatenate([w_ref[j + t] for t in range(group)], axis=1)


def _w_cols(w_ref, j, group, c, width):
    return w_ref[j, :, c:c + width] if group == 1 else _w_tiles(w_ref, j, group)


def _norm_mm(x, gain, w, tn, z_dtype, relu2, name):
    s, d = x.shape
    nj = w.shape[0]
    tm = _tile(s, TOKEN_TILE)
    group = _mxu_group(nj, tn)

    def body(x_ref, g_ref, w_ref, h_ref, z_ref, *a_ref):
        xv = x_ref[...]
        r = lax.rsqrt(jnp.mean(xv * xv, axis=-1, keepdims=True) + RMS_EPS)
        h = _bf(xv * r * g_ref[...])
        h_ref[...] = h
        cw = tn if tn <= 512 else 512
        for j in range(0, nj, group):
            for c in range(0, tn, cw):
                z = _dot(h, _w_cols(w_ref, j, group, c, cw))
                cols = slice(j * tn + c, j * tn + c + group * cw)
                z_ref[:, cols] = z.astype(z_ref.dtype)
                if relu2:
                    a_ref[0][:, cols] = _bf(jnp.square(jnp.maximum(z, 0.0)))

    n = nj * tn
    out_shape = [SDS((s, d), BF16), SDS((s, n), z_dtype)]
    out_specs = [pl.BlockSpec((tm, d), lambda i: (i, 0)), pl.BlockSpec((tm, n), lambda i: (i, 0))]
    if relu2:
        out_shape.append(SDS((s, n), BF16))
        out_specs.append(pl.BlockSpec((tm, n), lambda i: (i, 0)))
    return pl.pallas_call(
        body, name=name, grid=(s // tm,), out_shape=out_shape,
        in_specs=[pl.BlockSpec((tm, d), lambda i: (i, 0)),
                  pl.BlockSpec((1, d), lambda i: (0, 0)),
                  pl.BlockSpec((nj, d, tn), lambda i: (0, 0, 0))],
        out_specs=out_specs,
        compiler_params=_params(("parallel",)),
    )(x, gain, w)


def _mm_res(parts, w, res, name):
    s, d = res.shape
    tm = _tile(s, TOKEN_TILE)
    widths = [p.shape[1] for p in parts]

    def body(*refs):
        a_refs = refs[:len(parts)]
        w_ref, res_ref, o_ref = refs[len(parts):]
        acc = res_ref[...]
        off = 0
        for a_ref, k in zip(a_refs, widths):
            acc = acc + _dot(a_ref[...], w_ref[off:off + k, :])
            off += k
        o_ref[...] = acc

    return pl.pallas_call(
        body, name=name, grid=(s // tm,), out_shape=SDS((s, d), F32),
        in_specs=[pl.BlockSpec((tm, k), lambda i: (i, 0)) for k in widths]
        + [pl.BlockSpec(w.shape, lambda i: (0, 0)), pl.BlockSpec((tm, d), lambda i: (i, 0))],
        out_specs=pl.BlockSpec((tm, d), lambda i: (i, 0)),
        compiler_params=_params(("parallel",)),
    )(*parts, w, res)


def _mm_nt_rows(dy, w, z, name):
    s, d = dy.shape
    k = w.shape[0]
    tm = _tile(s, TOKEN_TILE)
    tk = _tile(k, 1024)

    def body(dy_ref, w_ref, *rest):
        o_ref = rest[-1]
        dyb = _bf(dy_ref[...])
        for j in range(k // tk):
            cols = slice(j * tk, (j + 1) * tk)
            da = _dot(dyb, w_ref[cols, :], NT)
            if z is not None:
                da = da * (2.0 * jnp.maximum(rest[0][:, cols].astype(F32), 0.0))
            o_ref[:, cols] = _bf(da)

    in_specs = [pl.BlockSpec((tm, d), lambda i: (i, 0)), pl.BlockSpec((k, d), lambda i: (0, 0))]
    args = [dy, w]
    if z is not None:
        in_specs.append(pl.BlockSpec((tm, k), lambda i: (i, 0)))
        args.append(z)
    return pl.pallas_call(
        body, name=name, grid=(s // tm,), out_shape=SDS((s, k), BF16),
        in_specs=in_specs, out_specs=pl.BlockSpec((tm, k), lambda i: (i, 0)),
        compiler_params=_params(("parallel",)),
    )(*args)


def _mm_nt_normbwd(dz, w, x, gain, dres, name):
    s, d = x.shape
    nj, _, nc = w.shape
    tm = _tile(s, TOKEN_TILE)

    group = _mxu_group(nj, nc)

    def body(dz_ref, w_ref, x_ref, g_ref, dres_ref, dx_ref, dg_ref):
        dh = _dot(dz_ref[:, 0:group * nc], _w_tiles(w_ref, 0, group), NT)
        for j in range(group, nj, group):
            dh = dh + _dot(dz_ref[:, j * nc:(j + group) * nc], _w_tiles(w_ref, j, group), NT)
        xv = x_ref[...]
        r = lax.rsqrt(jnp.mean(xv * xv, axis=-1, keepdims=True) + RMS_EPS)
        xn = xv * r

        @pl.when(pl.program_id(0) == 0)
        def _():
            dg_ref[...] = jnp.zeros_like(dg_ref)

        dg_ref[...] += jnp.sum(dh * xn, axis=0, keepdims=True)
        dxh = dh * g_ref[...]
        dx_ref[...] = dres_ref[...] + r * (dxh - xn * jnp.mean(dxh * xn, axis=-1, keepdims=True))

    return pl.pallas_call(
        body, name=name, grid=(s // tm,), out_shape=[SDS((s, d), F32), SDS((1, d), F32)],
        in_specs=[pl.BlockSpec((tm, nj * nc), lambda i: (i, 0)),
                  pl.BlockSpec((nj, d, nc), lambda i: (0, 0, 0)),
                  pl.BlockSpec((tm, d), lambda i: (i, 0)),
                  pl.BlockSpec((1, d), lambda i: (0, 0)),
                  pl.BlockSpec((tm, d), lambda i: (i, 0))],
        out_specs=[pl.BlockSpec((tm, d), lambda i: (i, 0)), pl.BlockSpec((1, d), lambda i: (0, 0))],
        compiler_params=_params(("arbitrary",)),
    )(dz, w, x, gain, dres)


def _mm_tn(a, b, ka, nb, a_tiled, split, name, after=None):
    a_parts = list(a) if isinstance(a, (list, tuple)) else [a]
    s = a_parts[0].shape[0]
    tm = _tile(s, 2 * TOKEN_TILE)
    nm = s // tm
    nj = a_parts[0].shape[1] // ka if a_tiled and len(a_parts) == 1 else (1 if a_tiled else b.shape[1] // nb)
    axis, parts = split
    pr, pc = (ka // parts, nb) if axis == 0 else (ka, nb // parts)

    def body(*refs):
        a_refs, b_ref = refs[:len(a_parts)], refs[len(a_parts)]
        o_ref, acc = refs[-2:]
        m = pl.program_id(1)

        @pl.when(m == 0)
        def _():
            acc[...] = jnp.zeros_like(acc)

        av = a_refs[0][...] if len(a_refs) == 1 else jnp.concatenate([r[...] for r in a_refs], axis=1)
        acc[...] += _dot(_bf(av), _bf(b_ref[...]), TN)

        @pl.when(m == nm - 1)
        def _():
            for q in range(parts):
                piece = acc[q * pr:(q + 1) * pr, :] if axis == 0 else acc[:, q * pc:(q + 1) * pc]
                o_ref[q] = piece.astype(o_ref.dtype)

    return pl.pallas_call(
        body, name=name, grid=(nj, nm), out_shape=SDS((nj * parts, pr, pc), BF16),
        in_specs=([pl.BlockSpec((tm, ka), (lambda j, m: (m, j)) if a_tiled else (lambda j, m: (m, 0)))]
                  if len(a_parts) == 1 else [pl.BlockSpec((tm, p.shape[1]), lambda j, m: (m, 0)) for p in a_parts])
        + [pl.BlockSpec((tm, nb), (lambda j, m: (m, 0)) if a_tiled else (lambda j, m: (m, j)))]
        + [pl.BlockSpec(memory_space=pl.ANY)] * (after is not None),
        out_specs=pl.BlockSpec((parts, pr, pc), lambda j, m: (j, 0, 0)),
        scratch_shapes=[pltpu.VMEM((ka, nb), F32)],
        compiler_params=_params(("parallel", "arbitrary")),
    )(*a_parts, b, *([] if after is None else [after]))


def _final_loss(x, gain, target, name):
    s, d = x.shape
    tm = _tile(s, TOKEN_TILE)

    def body(x_ref, g_ref, t_ref, loss_ref, dx_ref, dg_ref):
        @pl.when(pl.program_id(0) == 0)
        def _():
            loss_ref[...] = jnp.zeros_like(loss_ref)
            dg_ref[...] = jnp.zeros_like(dg_ref)

        xv = x_ref[...]
        r = lax.rsqrt(jnp.mean(xv * xv, axis=-1, keepdims=True) + RMS_EPS)
        xn = xv * r
        err = xn * g_ref[...] - t_ref[...]
        loss_ref[...] += (0.5 / d) * jnp.sum(err * err)
        dy = err * (1.0 / d)
        dg_ref[...] += jnp.sum(dy * xn, axis=0, keepdims=True)
        dxh = dy * g_ref[...]
        dx_ref[...] = r * (dxh - xn * jnp.mean(dxh * xn, axis=-1, keepdims=True))

    return pl.pallas_call(
        body, name=name, grid=(s // tm,),
        out_shape=[SDS((8, 128), F32), SDS((s, d), F32), SDS((1, d), F32)],
        in_specs=[pl.BlockSpec((tm, d), lambda i: (i, 0)), pl.BlockSpec((1, d), lambda i: (0, 0)),
                  pl.BlockSpec((tm, d), lambda i: (i, 0))],
        out_specs=[pl.BlockSpec((8, 128), lambda i: (0, 0)), pl.BlockSpec((tm, d), lambda i: (i, 0)),
                   pl.BlockSpec((1, d), lambda i: (0, 0))],
        compiler_params=_params(("arbitrary",)),
    )(x, gain, target)


def _retention_tables(s):
    half = RET_HEAD_DIM // 2
    inv_freq = 1.0 / (RET_ROPE_BASE ** jnp.linspace(0.0, 1.0, half, dtype=F32))
    ang = jnp.arange(s, dtype=F32)[:, None] * inv_freq[None, :]
    cos, sin = jnp.cos(ang), jnp.sin(ang)
    cos_e = jnp.repeat(cos, 2, axis=-1)
    sin_s = jnp.stack([-sin, sin], axis=-1).reshape(s, RET_HEAD_DIM)
    log_g = jnp.log1p(-jnp.power(2.0, -5.0 - jnp.arange(RET_HEADS, dtype=F32)))
    pos = jnp.arange(CHUNK, dtype=F32)
    dmat = jnp.exp(jnp.abs(pos[:, None] - pos[None, :])[None] * log_g[:, None, None])
    qdec = jnp.exp((pos[None, :] + 1.0) * log_g[:, None])
    kdec = jnp.exp((CHUNK - 1.0 - pos[None, :]) * log_g[:, None])
    lam = jnp.exp(CHUNK * log_g)
    wide = (RET_HEADS, CHUNK, RET_HEAD_DIM)
    return dict(cos=cos_e, sin=sin_s, dmat=dmat,
                qdec=jnp.broadcast_to(qdec[:, :, None], wide),
                kdec=jnp.broadcast_to(kdec[:, :, None], wide),
                lam=jnp.broadcast_to(lam[:, None, None], (RET_HEADS, RET_HEAD_DIM, RET_HEAD_DIM)))


def _swap_pairs(t):
    lane = lax.broadcasted_iota(jnp.int32, t.shape, 1)
    return jnp.where(lane % 2 == 0, pltpu.roll(t, RET_HEAD_DIM - 1, 1), pltpu.roll(t, 1, 1))


def _head(h):
    return slice(h * RET_HEAD_DIM, (h + 1) * RET_HEAD_DIM)


def _ret_common_specs(tb, blk):
    zs = [pl.BlockSpec((tb, RET_WIDTH), functools.partial(lambda j, i: (blk(i), j), j)) for j in range(4)]
    tabs = [pl.BlockSpec((tb, RET_HEAD_DIM), lambda i: (blk(i), 0))] * 2
    consts = [pl.BlockSpec((1, RET_WIDTH), lambda i: (0, 0)),
              pl.BlockSpec((RET_HEADS, CHUNK, CHUNK), lambda i: (0, 0, 0)),
              pl.BlockSpec((RET_HEADS, CHUNK, RET_HEAD_DIM), lambda i: (0, 0, 0)),
              pl.BlockSpec((RET_HEADS, CHUNK, RET_HEAD_DIM), lambda i: (0, 0, 0)),
              pl.BlockSpec((RET_HEADS, RET_HEAD_DIM, RET_HEAD_DIM), lambda i: (0, 0, 0))]
    return zs + tabs + consts


def _ret_fwd(z, tabs, gn_gain, name):
    s = z.shape[0]
    tb = _tile(s, TOKEN_TILE)
    ncb = tb // CHUNK
    scale = RET_HEAD_DIM ** -0.5

    def body(q_ref, k_ref, v_ref, g_ref, cos_ref, sin_ref, gain_ref, dm_ref, qd_ref, kd_ref, lam_ref,
             o_ref, st_ref, ret_ref, s_scr, qr_scr, kr_scr):
        @pl.when(pl.program_id(0) == 0)
        def _():
            s_scr[...] = jnp.zeros_like(s_scr)

        cosv, sinv = cos_ref[...], sin_ref[...]
        for h in range(RET_HEADS):
            qh, kh = q_ref[:, _head(h)], k_ref[:, _head(h)]
            qr_scr[:, _head(h)] = qh * cosv + _swap_pairs(qh) * sinv
            kr_scr[:, _head(h)] = (kh * cosv + _swap_pairs(kh) * sinv) * scale

        def chunk(c, carry):
            rows = pl.ds(pl.multiple_of(c * CHUNK, CHUNK), CHUNK)
            for h in range(RET_HEADS):
                qc, kc, vc = qr_scr[rows, _head(h)], kr_scr[rows, _head(h)], v_ref[rows, _head(h)]
                a = _dot(_bf(qc), _bf(kc), NT) * dm_ref[h]
                st = s_scr[h]
                st_ref[c, h] = st
                o_ref[rows, _head(h)] = _dot(_bf(a), _bf(vc)) + _dot(_bf(qc * qd_ref[h]), _bf(st))
                s_scr[h] = st * lam_ref[h] + _dot(_bf(kc * kd_ref[h]), _bf(vc), TN)
            return carry

        lax.fori_loop(0, ncb, chunk, 0, unroll=4)
        for h in range(RET_HEADS):
            o = o_ref[:, _head(h)]
            mu = jnp.mean(o, axis=-1, keepdims=True)
            oc = o - mu
            y = oc * lax.rsqrt(jnp.mean(oc * oc, axis=-1, keepdims=True) + GN_EPS) * gain_ref[:, _head(h)]
            g = g_ref[:, _head(h)]
            ret_ref[:, _head(h)] = _bf(g / (1.0 + jnp.exp(-g)) * y)

    nc = s // CHUNK
    return pl.pallas_call(
        body, name=name, grid=(s // tb,),
        out_shape=[SDS((s, RET_WIDTH), F32), SDS((nc, RET_HEADS, RET_HEAD_DIM, RET_HEAD_DIM), F32),
                   SDS((s, RET_WIDTH), BF16)],
        in_specs=_ret_common_specs(tb, lambda i: i),
        out_specs=[pl.BlockSpec((tb, RET_WIDTH), lambda i: (i, 0)),
                   pl.BlockSpec((ncb, RET_HEADS, RET_HEAD_DIM, RET_HEAD_DIM), lambda i: (i, 0, 0, 0)),
                   pl.BlockSpec((tb, RET_WIDTH), lambda i: (i, 0))],
        scratch_shapes=[pltpu.VMEM((RET_HEADS, RET_HEAD_DIM, RET_HEAD_DIM), F32),
                        pltpu.VMEM((tb, RET_WIDTH), F32), pltpu.VMEM((tb, RET_WIDTH), F32)],
        compiler_params=_params(("arbitrary",)),
    )(z, z, z, z, tabs["cos"], tabs["sin"], gn_gain, tabs["dmat"], tabs["qdec"], tabs["kdec"], tabs["lam"])


def _ret_bwd(z, tabs, gn_gain, o_pre, states, du, name):
    s = z.shape[0]
    tb = _tile(s, TOKEN_TILE)
    ncb = tb // CHUNK
    nblk = s // tb
    scale = RET_HEAD_DIM ** -0.5
    rev = lambda i: nblk - 1 - i

    def body(q_ref, k_ref, v_ref, g_ref, cos_ref, sin_ref, gain_ref, dm_ref, qd_ref, kd_ref, lam_ref,
             o_ref, st_ref, dret_ref, dz_ref, dgain_ref, g_scr, qr_scr, kr_scr, do_scr, dq_scr, dk_scr):
        @pl.when(pl.program_id(0) == 0)
        def _():
            g_scr[...] = jnp.zeros_like(g_scr)
            dgain_ref[...] = jnp.zeros_like(dgain_ref)

        cosv, sinv = cos_ref[...], sin_ref[...]
        for h in range(RET_HEADS):
            hs = _head(h)
            qh, kh = q_ref[:, hs], k_ref[:, hs]
            qr_scr[:, hs] = qh * cosv + _swap_pairs(qh) * sinv
            kr_scr[:, hs] = (kh * cosv + _swap_pairs(kh) * sinv) * scale
            o = o_ref[:, hs]
            mu = jnp.mean(o, axis=-1, keepdims=True)
            oc = o - mu
            rstd = lax.rsqrt(jnp.mean(oc * oc, axis=-1, keepdims=True) + GN_EPS)
            yh = oc * rstd
            gain = gain_ref[:, hs]
            g = g_ref[:, hs]
            sg = 1.0 / (1.0 + jnp.exp(-g))
            dret = dret_ref[:, hs].astype(F32)
            dy = dret * (g * sg)
            dz_ref[:, 3 * RET_WIDTH + h * RET_HEAD_DIM:3 * RET_WIDTH + (h + 1) * RET_HEAD_DIM] = _bf(
                dret * (yh * gain) * (sg * (1.0 + g * (1.0 - sg))))
            dgain_ref[:, hs] += jnp.sum(dy * yh, axis=0, keepdims=True)
            dyh = dy * gain
            do_scr[:, hs] = rstd * (dyh - jnp.mean(dyh, axis=-1, keepdims=True)
                                    - yh * jnp.mean(dyh * yh, axis=-1, keepdims=True))

        def chunk(cc, carry):
            c = ncb - 1 - cc
            rows = pl.ds(pl.multiple_of(c * CHUNK, CHUNK), CHUNK)
            for h in range(RET_HEADS):
                hs = _head(h)
                qc, kc, vc, doc = _bf(qr_scr[rows, hs]), _bf(kr_scr[rows, hs]), _bf(v_ref[rows, hs]), _bf(do_scr[rows, hs])
                qdc, kdc = qd_ref[h], kd_ref[h]
                st, gs = _bf(st_ref[c, h]), g_scr[h]
                gsb = _bf(gs)
                dm = dm_ref[h]
                p = _bf(_dot(qc, kc, NT) * dm)
                da = _bf(_dot(doc, vc, NT) * dm)
                kt = _bf(kr_scr[rows, hs] * kdc)
                qt = _bf(qr_scr[rows, hs] * qdc)
                dz_ref[rows, 2 * RET_WIDTH + h * RET_HEAD_DIM:2 * RET_WIDTH + (h + 1) * RET_HEAD_DIM] = _bf(
                    _dot(p, doc, TN) + _dot(kt, gsb))
                dq_scr[rows, hs] = _dot(da, kc) + _dot(doc, st, NT) * qdc
                dk_scr[rows, hs] = _dot(da, qc, TN) + _dot(vc, gsb, NT) * kdc
                g_scr[h] = gs * lam_ref[h] + _dot(qt, doc, TN)
            return carry

        lax.fori_loop(0, ncb, chunk, 0, unroll=4)
        for h in range(RET_HEADS):
            hs = _head(h)
            dq, dk = dq_scr[:, hs], dk_scr[:, hs]
            dz_ref[:, h * RET_HEAD_DIM:(h + 1) * RET_HEAD_DIM] = _bf(dq * cosv - _swap_pairs(dq) * sinv)
            dz_ref[:, RET_WIDTH + h * RET_HEAD_DIM:RET_WIDTH + (h + 1) * RET_HEAD_DIM] = _bf(
                (dk * cosv - _swap_pairs(dk) * sinv) * scale)

    return pl.pallas_call(
        body, name=name, grid=(nblk,),
        out_shape=[SDS((s, AB_IN_WIDTH), BF16), SDS((1, RET_WIDTH), F32)],
        in_specs=_ret_common_specs(tb, rev)
        + [pl.BlockSpec((tb, RET_WIDTH), lambda i: (rev(i), 0)),
           pl.BlockSpec((ncb, RET_HEADS, RET_HEAD_DIM, RET_HEAD_DIM), lambda i: (rev(i), 0, 0, 0)),
           pl.BlockSpec((tb, RET_WIDTH), lambda i: (rev(i), 0))],
        out_specs=[pl.BlockSpec((tb, 4 * RET_WIDTH), lambda i: (rev(i), 0)),
                   pl.BlockSpec((1, RET_WIDTH), lambda i: (0, 0))],
        scratch_shapes=[pltpu.VMEM((RET_HEADS, RET_HEAD_DIM, RET_HEAD_DIM), F32)]
        + [pltpu.VMEM((tb, RET_WIDTH), F32)] * 5,
        compiler_params=_params(("arbitrary",)),
    )(z, z, z, z, tabs["cos"], tabs["sin"], gn_gain, tabs["dmat"], tabs["qdec"], tabs["kdec"], tabs["lam"],
      o_pre, states, du)


POOL_COL = 4 * RET_WIDTH // POOL_WIDTH


def _pooled(cur, prev, t0):
    tm = cur.shape[0]
    xx = jnp.concatenate([prev, cur], axis=0)
    sums = {1: xx}
    w = 1
    while w < POOL_WINDOWS[-1]:
        sums[2 * w] = sums[w] + pltpu.roll(sums[w], w, 0)
        w *= 2
    t = t0 + lax.broadcasted_iota(jnp.int32, (tm, 128), 0)
    outs = []
    for gi, w in enumerate(POOL_WINDOWS):
        cols = slice(gi * 128, (gi + 1) * 128)
        cnt = jnp.minimum(t + 1, w).astype(F32)
        outs.append(sums[w][POOL_HALO:, cols] / cnt - cur[:, cols])
    return outs


def _pool_fwd(z, w_pool, scale, name):
    s = z.shape[0]
    tm = _tile(s, TOKEN_TILE)
    hb = tm // POOL_HALO

    def body(p_ref, prev_ref, w_ref, sc_ref, o_ref):
        i = pl.program_id(0)
        prev = jnp.where(i > 0, prev_ref[...], 0.0)
        pooled = _pooled(p_ref[...], prev, i * tm)
        for gi in range(len(POOL_WINDOWS)):
            cols = slice(gi * 128, (gi + 1) * 128)
            o_ref[:, cols] = _bf(_dot(_bf(pooled[gi]), _bf(w_ref[gi])) * sc_ref[:, cols])

    return pl.pallas_call(
        body, name=name, grid=(s // tm,), out_shape=SDS((s, POOL_WIDTH), BF16),
        in_specs=[pl.BlockSpec((tm, POOL_WIDTH), lambda i: (i, POOL_COL)),
                  pl.BlockSpec((POOL_HALO, POOL_WIDTH), lambda i: (jnp.maximum(i * hb - 1, 0), POOL_COL)),
                  pl.BlockSpec(w_pool.shape, lambda i: (0, 0, 0)),
                  pl.BlockSpec((1, POOL_WIDTH), lambda i: (0, 0))],
        out_specs=pl.BlockSpec((tm, POOL_WIDTH), lambda i: (i, 0)),
        compiler_params=_params(("parallel",)),
    )(z, z, w_pool, scale)


def _pool_bwd(z, w_pool, scale, du, dz, name):
    s = z.shape[0]
    tm = _tile(s, TOKEN_TILE)
    hb = tm // POOL_HALO
    nblk = s // tm
    last_halo = s // POOL_HALO - 1

    def body(p_ref, prev_ref, w_ref, sc_ref, do_ref, don_ref, dz_ref, dp_ref, dw_ref, dsc_ref):
        i = pl.program_id(0)

        @pl.when(i == 0)
        def _():
            dw_ref[...] = jnp.zeros_like(dw_ref)
            dsc_ref[...] = jnp.zeros_like(dsc_ref)

        prev = jnp.where(i > 0, prev_ref[...], 0.0)
        pooled = _pooled(p_ref[...], prev, i * tm)
        dout = do_ref[...].astype(F32)
        dout_next = jnp.where(i < nblk - 1, don_ref[...].astype(F32), 0.0)
        sc = sc_ref[...]
        dmix = jnp.concatenate([dout * sc, dout_next * sc], axis=0)
        n = tm + POOL_HALO
        t = i * tm + lax.broadcasted_iota(jnp.int32, (n, 128), 0)
        for gi, w in enumerate(POOL_WINDOWS):
            cols = slice(gi * 128, (gi + 1) * 128)
            wg = _bf(w_ref[gi])
            pg = _bf(pooled[gi])
            dsc_ref[:, cols] += jnp.sum(dout[:, cols] * _dot(pg, wg), axis=0, keepdims=True)
            dw_ref[gi] += _dot(pg, _bf(dmix[:tm, cols]), TN)
            dpool = _dot(_bf(dmix[:, cols]), wg, NT)
            acc = dpool / jnp.minimum(t + 1, w).astype(F32)
            step = 1
            while step < w:
                acc = acc + pltpu.roll(acc, n - step, 0)
                step *= 2
            dp_ref[:, cols] = _bf(acc[:tm] - dpool[:tm])

    return pl.pallas_call(
        body, name=name, grid=(nblk,),
        out_shape=[SDS(dz.shape, BF16), SDS(w_pool.shape, F32), SDS((1, POOL_WIDTH), F32)],
        in_specs=[pl.BlockSpec((tm, POOL_WIDTH), lambda i: (i, POOL_COL)),
                  pl.BlockSpec((POOL_HALO, POOL_WIDTH), lambda i: (jnp.maximum(i * hb - 1, 0), POOL_COL)),
                  pl.BlockSpec(w_pool.shape, lambda i: (0, 0, 0)),
                  pl.BlockSpec((1, POOL_WIDTH), lambda i: (0, 0)),
                  pl.BlockSpec((tm, POOL_WIDTH), lambda i: (i, 1)),
                  pl.BlockSpec((POOL_HALO, POOL_WIDTH), lambda i: (jnp.minimum((i + 1) * hb, last_halo), 1)),
                  pl.BlockSpec(memory_space=pl.ANY)],
        out_specs=[pl.BlockSpec((tm, POOL_WIDTH), lambda i: (i, POOL_COL)),
                   pl.BlockSpec(w_pool.shape, lambda i: (0, 0, 0)),
                   pl.BlockSpec((1, POOL_WIDTH), lambda i: (0, 0))],
        input_output_aliases={6: 0},
        compiler_params=_params(("arbitrary",)),
    )(z, z, w_pool, scale, du, du, dz)


def _rel_onehot():
    r = lax.broadcasted_iota(jnp.int32, (REL_PAD, ATT_DIAG), 0)
    c = lax.broadcasted_iota(jnp.int32, (REL_PAD, ATT_DIAG), 1)
    rel = jnp.where(c < ATT_K_TILE, jnp.clip(LEFT_CHUNKS * CHUNK - c, -REL_CLIP, REL_CLIP) + REL_CLIP,
                    2 * REL_CLIP)
    return (rel == r).astype(BF16)


def _split3(v):
    hi = _bf(v)
    r1 = v - hi.astype(F32)
    mid = _bf(r1)
    return hi, mid, _bf(r1 - mid.astype(F32))


def _skew(v, sign):
    row = lax.broadcasted_iota(jnp.int32, v.shape, 0)
    bit = 1
    while bit < ATT_Q_TILE:
        shift = bit if sign > 0 else ATT_DIAG - bit
        v = jnp.where((row & bit) != 0, pltpu.roll(v, shift, 1), v)
        bit *= 2
    return v


def _attn_bias(rel_bias, name):
    def body(t_ref, o_ref):
        oh = _rel_onehot()
        base = sum(_dot(part, oh) for part in _split3(t_ref[0]))
        full = _skew(jnp.broadcast_to(base[0:1], (ATT_Q_TILE, ATT_DIAG)), +1)[:, :ATT_K_TILE]
        qc = lax.broadcasted_iota(jnp.int32, full.shape, 0) // CHUNK
        kc = lax.broadcasted_iota(jnp.int32, full.shape, 1) // CHUNK
        o_ref[0] = jnp.where((kc >= qc) & (kc <= qc + LEFT_CHUNKS), full, NEG_INF)

    t8 = jnp.broadcast_to(rel_bias[:, None, :], (ATT_HEADS, 8, REL_PAD))
    return pl.pallas_call(
        body, name=name, grid=(ATT_HEADS,), out_shape=SDS((ATT_HEADS, ATT_Q_TILE, ATT_K_TILE), F32),
        in_specs=[pl.BlockSpec((1, 8, REL_PAD), lambda h: (h, 0, 0))],
        out_specs=pl.BlockSpec((1, ATT_Q_TILE, ATT_K_TILE), lambda h: (h, 0, 0)),
        compiler_params=_params(("parallel",)),
    )(t8)


def _attn_dbias(dbias, name):
    def body(d_ref, o_ref):
        pad = jnp.zeros((ATT_Q_TILE, ATT_DIAG - ATT_K_TILE), F32)
        diag = _skew(jnp.concatenate([d_ref[0], pad], axis=1), -1)
        col = jnp.sum(diag, axis=0, keepdims=True)
        oh = _rel_onehot()
        col8 = jnp.broadcast_to(col, (8, ATT_DIAG))
        o_ref[0] = sum(_dot(part, oh, NT) for part in _split3(col8))

    out = pl.pallas_call(
        body, name=name, grid=(ATT_HEADS,), out_shape=SDS((ATT_HEADS, 8, REL_PAD), F32),
        in_specs=[pl.BlockSpec((1, ATT_Q_TILE, ATT_K_TILE), lambda h: (h, 0, 0))],
        out_specs=pl.BlockSpec((1, 8, REL_PAD), lambda h: (h, 0, 0)),
        compiler_params=_params(("parallel",)),
    )(dbias)
    return out[:, 0, :]


ATT_WIDTH = 128 * ATT_PAIRS
ATT_GROUPS = D_MODEL // ATT_WIDTH


def _attn_specs(nq):
    def tile(off, back):
        return pl.BlockSpec((ATT_Q_TILE, ATT_WIDTH),
                            lambda g, i: (jnp.maximum(jnp.minimum(i, nq - 1) - back, 0), off + g))

    backs = [ATT_BACK - b for b in range(ATT_BACK + 1)]
    return ([tile(0, 0)] + [tile(ATT_GROUPS, b) for b in backs] + [tile(2 * ATT_GROUPS, b) for b in backs]
            + [pl.BlockSpec((2 * ATT_PAIRS, ATT_Q_TILE, ATT_K_TILE), lambda g, i: (g, 0, 0))])


def _attn_weights(qh, k2, bias, i, masked):
    sc = _dot(qh, k2, NT) + bias
    if masked:
        kpos = (i - ATT_BACK) * ATT_Q_TILE + lax.broadcasted_iota(jnp.int32, sc.shape, 1)
        sc = jnp.where(kpos >= 0, sc, NEG_INF)
    e = jnp.exp(sc - jnp.max(sc, axis=-1, keepdims=True))
    return e, 1.0 / jnp.sum(e, axis=-1, keepdims=True)


def _first_head():
    return lax.broadcasted_iota(jnp.int32, (ATT_Q_TILE, 128), 1) < ATT_HEAD_DIM


def _pair_operands(q_ref, k_refs, v_refs, pp):
    cols = slice(pp * 128, (pp + 1) * 128)
    q2 = q_ref[:, cols] * ATT_HEAD_DIM ** -0.5
    k2 = jnp.concatenate([r[:, cols] for r in k_refs], axis=0)
    v2 = jnp.concatenate([r[:, cols] for r in v_refs], axis=0)
    return cols, q2, k2, v2


def _attn_fwd(z, bias, name):
    s = z.shape[0]
    nq = s // ATT_Q_TILE
    nt = ATT_BACK + 1

    def body(q_ref, *rest):
        k_refs, v_refs, (b_ref, o_ref) = rest[:nt], rest[nt:2 * nt], rest[2 * nt:]
        i = pl.program_id(1)
        first = _first_head()

        def compute(masked):
            for pp in range(ATT_PAIRS):
                cols, q2, k2, v2 = _pair_operands(q_ref, k_refs, v_refs, pp)
                outs = []
                for hh in range(2):
                    qh = jnp.where(first if hh == 0 else ~first, q2, 0)
                    e, inv = _attn_weights(qh, k2, b_ref[2 * pp + hh], i, masked)
                    outs.append(_dot(_bf(e), v2) * inv)
                o_ref[:, cols] = _bf(jnp.where(first, outs[0], outs[1]))

        pl.when(i < ATT_BACK)(lambda: compute(True))
        pl.when(i >= ATT_BACK)(lambda: compute(False))

    return pl.pallas_call(
        body, name=name, grid=(ATT_GROUPS, nq), out_shape=SDS((s, D_MODEL), BF16),
        in_specs=_attn_specs(nq),
        out_specs=pl.BlockSpec((ATT_Q_TILE, ATT_WIDTH), lambda g, i: (i, g)),
        compiler_params=_params(("parallel", "parallel")),
    )(*([z] * (1 + 2 * nt)), bias)


def _attn_bwd(z, bias, o, do, name):
    s = z.shape[0]
    nq = s // ATT_Q_TILE
    nt = ATT_BACK + 1

    def body(q_ref, *rest):
        k_refs, v_refs = rest[:nt], rest[nt:2 * nt]
        b_ref, o_ref, do_ref, dq_ref, dk_ref, dv_ref, db_ref, dk_acc, dv_acc = rest[2 * nt:]
        i = pl.program_id(1)
        first = _first_head()

        @pl.when(i == 0)
        def _():
            db_ref[...] = jnp.zeros_like(db_ref)
            dk_acc[...] = jnp.zeros_like(dk_acc)
            dv_acc[...] = jnp.zeros_like(dv_acc)

        def compute(masked):
            for pp in range(ATT_PAIRS):
                cols, q2, k2, v2 = _pair_operands(q_ref, k_refs, v_refs, pp)
                do2 = do_ref[:, cols].astype(F32)
                prod = do2 * o_ref[:, cols].astype(F32)
                dqs, dk, dv = [], None, None
                for hh in range(2):
                    mine = first if hh == 0 else ~first
                    qh = jnp.where(mine, q2, 0)
                    e, inv = _attn_weights(qh, k2, b_ref[2 * pp + hh], i, masked)
                    delta = jnp.sum(jnp.where(mine, prod, 0.0), axis=-1, keepdims=True) * inv
                    doh = _bf(jnp.where(mine, do2 * inv, 0.0))
                    ds = e * (_dot(doh, v2, NT) - delta)
                    db_ref[2 * pp + hh] += ds
                    dsb = _bf(ds)
                    dqs.append(_dot(dsb, k2))
                    dkh, dvh = _dot(dsb, qh, TN), _dot(_bf(e), doh, TN)
                    dk, dv = (dkh, dvh) if hh == 0 else (dk + dkh, dv + dvh)
                dq_ref[:, cols] = _bf(jnp.where(first, dqs[0], dqs[1]) * ATT_HEAD_DIM ** -0.5)
                for b in range(nt):
                    slot = (i + b + 1) % nt
                    rows = slice(b * ATT_Q_TILE, (b + 1) * ATT_Q_TILE)
                    if b < ATT_BACK:
                        dk_acc[slot, :, cols] += dk[rows]
                        dv_acc[slot, :, cols] += dv[rows]
                    else:
                        dk_acc[slot, :, cols] = dk[rows]
                        dv_acc[slot, :, cols] = dv[rows]

        pl.when(i < ATT_BACK)(lambda: compute(True))
        pl.when((i >= ATT_BACK) & (i < nq))(lambda: compute(False))
        done = (i + 1) % nt
        dk_ref[...] = _bf(dk_acc[done])
        dv_ref[...] = _bf(dv_acc[done])

    tile = pl.BlockSpec((ATT_Q_TILE, ATT_WIDTH), lambda g, i: (jnp.minimum(i, nq - 1), g))
    late = pl.BlockSpec((ATT_Q_TILE, ATT_WIDTH), lambda g, i: (jnp.maximum(i - ATT_BACK, 0), g))
    ring = pltpu.VMEM((nt, ATT_Q_TILE, ATT_WIDTH), F32)
    return pl.pallas_call(
        body, name=name, grid=(ATT_GROUPS, nq + ATT_BACK),
        out_shape=[SDS((s, D_MODEL), BF16)] * 3 + [SDS((ATT_HEADS, ATT_Q_TILE, ATT_K_TILE), F32)],
        in_specs=_attn_specs(nq) + [tile, tile],
        out_specs=[tile, late, late, pl.BlockSpec((2 * ATT_PAIRS, ATT_Q_TILE, ATT_K_TILE), lambda g, i: (g, 0, 0))],
        scratch_shapes=[ring, ring],
        compiler_params=_params(("parallel", "arbitrary")),
    )(*([z] * (1 + 2 * nt)), bias, o, do)


FWD_GROUPS = (
    (("ab_w_in", 0),),
    (("ab_w_out", 0), ("w_ffn_in", 0)),
    (("w_ffn_out", 0),),
    (("c_w_qkv", 0),),
    (("c_w_out", 0), ("w_ffn_in", 1), ("w_ffn_out", 1), ("ab_w_in", 1)),
    (("ab_w_out", 1), ("w_ffn_in", 2), ("w_ffn_out", 2), ("c_w_qkv", 1)),
    (("c_w_out", 1), ("w_ffn_in", 3), ("w_ffn_out", 3)),
)


def _local_step(x, target, small, comm):
    s = x.shape[0]
    tabs = _retention_tables(s)
    saved, w = [], comm.weight
    for layer in range(DEPTH):
        i = layer // 2
        sv = {"x0": x}
        g_mix = small["mix_norm"][layer:layer + 1]
        if layer % 2 == 0:
            sv["h1"], sv["z"] = _norm_mm(x, g_mix, w("ab_w_in", i, x), AB_IN_WIDTH, F32, False, "ab_in_fwd")
            gn = small["ab_gn_gain"][i:i + 1]
            sv["o_pre"], sv["states"], ret = _ret_fwd(sv["z"], tabs, gn, "ret_fwd")
            pool = _pool_fwd(sv["z"], small["ab_w_pool"][i], small["ab_pool_scale"][i:i + 1], "pool_fwd")
            sv["u"] = (ret, pool)
            x = _mm_res([ret, pool], w("ab_w_out", i, ret), x, "ab_out_fwd")
        else:
            sv["h1"], sv["z"] = _norm_mm(x, g_mix, w("c_w_qkv", i, x), 3 * D_MODEL // N_DEV, BF16, False, "qkv_fwd")
            rb = jnp.pad(small["c_rel_bias"][i], ((0, 0), (0, REL_PAD - N_REL)))
            sv["bias"] = _attn_bias(rb, "attn_bias")
            sv["o"] = _attn_fwd(sv["z"], sv["bias"], "attn_fwd")
            x = _mm_res([sv["o"]], w("c_w_out", i, sv["o"]), x, "c_out_fwd")
        sv["x1"] = x
        sv["h2"], sv["z1"], sv["a"] = _norm_mm(x, small["ffn_norm"][layer:layer + 1], w("w_ffn_in", layer, x),
                                               D_FF // N_DEV, BF16, True, "ffn_in_fwd")
        x = _mm_res([sv["a"]], w("w_ffn_out", layer, sv["a"]), x, "ffn_out_fwd")
        saved.append(sv)

    loss, dx, d_final = _final_loss(x, small["final_norm"][None, :], target, "final_loss")

    gs = {k: [None] * DEPTH for k in ("mix_norm", "ffn_norm")}
    for k in ("ab_gn_gain", "ab_w_pool", "ab_pool_scale", "c_rel_bias"):
        gs[k] = [None] * (DEPTH // 2)
    gs["final_norm"] = d_final[0]
    tok = jnp.zeros((), F32)
    for layer in reversed(range(DEPTH)):
        i = layer // 2
        sv = saved[layer]
        dz1 = _mm_nt_rows(dx, w("w_ffn_out", layer), sv["z1"], "ffn_out_bwd")
        gw = {("w_ffn_out", layer): _mm_tn(sv["a"], dx, 1024, D_MODEL, True, (0, 2), "ffn_out_dw"),
              ("w_ffn_in", layer): _mm_tn(sv["h2"], dz1, D_MODEL, 1024, False, (1, 2), "ffn_in_dw",
                                          comm.after() if layer == 0 else None)}
        dx, dg = _mm_nt_normbwd(dz1, w("w_ffn_in", layer), sv["x1"], small["ffn_norm"][layer:layer + 1] + tok, dx,
                                "ffn_in_bwd")
        gs["ffn_norm"][layer] = dg[0]
        if layer == 0:
            tok = comm.send(gw)
            gw = {}
        g_mix = small["mix_norm"][layer:layer + 1] + tok
        if layer % 2 == 0:
            du = _mm_nt_rows(dx, w("ab_w_out", i), None, "mix_out_bwd")
            gw["ab_w_out", i] = _mm_tn(sv["u"], dx, D_MODEL, D_MODEL, True, (0, N_DEV), "mix_out_dw")
            gn = small["ab_gn_gain"][i:i + 1]
            dz, dgn = _ret_bwd(sv["z"], tabs, gn, sv["o_pre"], sv["states"], du, "ret_bwd")
            dz, dwp, dsc = _pool_bwd(sv["z"], small["ab_w_pool"][i], small["ab_pool_scale"][i:i + 1], du, dz, "pool_bwd")
            gs["ab_gn_gain"][i], gs["ab_w_pool"][i], gs["ab_pool_scale"][i] = dgn[0], dwp, dsc[0]
            gw["ab_w_in", i] = _to_shard_major(_mm_tn(sv["h1"], dz, D_MODEL, AB_IN_WIDTH // 2, False, (1, 1), "ab_in_dw",
                                                      comm.after()))
            dx, dg = _mm_nt_normbwd(dz, w("ab_w_in", i), sv["x0"], g_mix, dx, "ab_in_bwd")
        else:
            do = _mm_nt_rows(dx, w("c_w_out", i), None, "mix_out_bwd")
            gw["c_w_out", i] = _mm_tn(sv["o"], dx, D_MODEL, D_MODEL, True, (0, N_DEV), "mix_out_dw")
            dq, dk, dv, dbias = _attn_bwd(sv["z"], sv["bias"], sv["o"], do, "attn_bwd")
            gs["c_rel_bias"][i] = _attn_dbias(dbias, "attn_dbias")[:, :N_REL]
            dz = jnp.concatenate([dq, dk, dv], axis=1)
            gw["c_w_qkv", i] = _mm_tn(sv["h1"], dz, D_MODEL, 768, False, (1, 2), "qkv_dw", comm.after())
            dx, dg = _mm_nt_normbwd(dz, w("c_w_qkv", i), sv["x0"], g_mix, dx, "qkv_bwd")
        gs["mix_norm"][layer] = dg[0]
        if layer > 0:
            tok = comm.send(gw)
    gsmall = {k: (jnp.stack(v) if isinstance(v, list) else v) for k, v in gs.items()}
    return loss, dx, gw, gsmall


BIG = ("w_ffn_in", "w_ffn_out", "ab_w_in", "ab_w_out", "c_w_qkv", "c_w_out")
SMALL = ("mix_norm", "ffn_norm", "ab_gn_gain", "ab_w_pool", "ab_pool_scale", "c_rel_bias", "final_norm")
N_PEERS = N_DEV - 1
FLIPS = [(fx, fy, fc) for fx in (0, 1) for fy in (0, 1) for fc in (0, 1)][1:]


def _peers():
    x, y, c = (lax.axis_index(a) for a in MESH_AXES)
    peers = []
    for fx, fy, fc in FLIPS:
        px, py, pc = (1 - x if fx else x), (1 - y if fy else y), (1 - c if fc else c)
        peers.append(((px, py, pc), 4 * px + 2 * py + pc))
    return 4 * x + 2 * y + c, peers


def _exchange(srcs, by_slot, name, collective_id):
    n = len(srcs)
    src_refs = [jax.new_ref(a, memory_space=pltpu.MemorySpace.HBM) for a in srcs]
    land_refs = [jax.empty_ref(SDS((N_DEV,) + (a.shape[1:] if slotted else a.shape), a.dtype),
                               memory_space=pltpu.MemorySpace.HBM) for a, slotted in zip(srcs, by_slot)]

    @pl.kernel(mesh=plsc.ScalarSubcoreMesh(axis_name="sequencer", num_cores=1), name=name,
               scratch_types=(pltpu.SemaphoreType.DMA((n * N_PEERS,)), pltpu.SemaphoreType.DMA((n * N_PEERS,)),
                              pltpu.SemaphoreType.DMA((n,))),
               compiler_params=pltpu.CompilerParams(collective_id=collective_id))
    def launch(send_sems, recv_sems, local_sems):
        me, peers = _peers()
        barrier = pltpu.get_barrier_semaphore()
        for pos, _ in peers:
            pl.semaphore_signal(barrier, inc=1, device_id=pos, device_id_type=pl.DeviceIdType.MESH)
        pl.semaphore_wait(barrier, N_PEERS)
        waits = []
        for k in range(n):
            own = pltpu.make_async_copy(src_refs[k].at[me] if by_slot[k] else src_refs[k], land_refs[k].at[me],
                                        local_sems.at[k])
            own.start()
            waits.append(own.wait)
            for rel, (pos, slot) in enumerate(peers):
                src = src_refs[k].at[slot] if by_slot[k] else src_refs[k]
                sems = dict(send_sem=send_sems.at[k * N_PEERS + rel], recv_sem=recv_sems.at[k * N_PEERS + rel],
                            device_id=pos, device_id_type=pl.DeviceIdType.MESH)
                send = pltpu.make_async_remote_copy(src_ref=src, dst_ref=land_refs[k].at[me], **sems)
                send.start()
                arrival = pltpu.make_async_remote_copy(src_ref=src, dst_ref=land_refs[k].at[slot], **sems)
                waits += [send.wait_send, arrival.wait_recv]
        for wait in waits:
            wait()

    launch()
    return [r[...] for r in land_refs]


def _cast_group(weights, keys, token, name):
    def body(*refs):
        n = len(keys)
        for i_ref, o_ref in zip(refs[:n], refs[n + 1:]):
            o_ref[...] = _bf(i_ref[...])

    def layer_spec(shape, l):
        return pl.BlockSpec((None,) + shape[1:], lambda i: (l, 0, 0))

    whole = lambda shape: pl.BlockSpec(shape, lambda i: (0, 0))
    ins = [weights[k] for k, _ in keys]
    return pl.pallas_call(
        body, name=name, grid=(1,), out_shape=[SDS(w.shape[1:], BF16) for w in ins],
        in_specs=[layer_spec(w.shape, l) for w, (_, l) in zip(ins, keys)] + [pl.BlockSpec(memory_space=pl.ANY)],
        out_specs=[whole(w.shape[1:]) for w in ins],
        compiler_params=_params(("arbitrary",)),
    )(*ins, token)


def _to_shard_major(g):
    nj, ka, nb = g.shape
    full = jnp.transpose(g, (1, 0, 2)).reshape(ka, N_DEV, nj * nb // N_DEV)
    return jnp.transpose(full, (1, 0, 2))


def _from_gathered(name, g):
    if name in ("w_ffn_out", "ab_w_out", "c_w_out"):
        return g.reshape(g.shape[0] * g.shape[1], g.shape[2])
    if name == "ab_w_in":
        return jnp.transpose(g, (1, 0, 2)).reshape(1, g.shape[1], N_DEV * g.shape[2])
    return g


class _Comm:
    def __init__(self, weights):
        self.weights_f32 = weights
        self.gathered = {}
        self.got = {}
        self.calls = 0
        self.ended = None
        self.opened = -1

    def _exchange(self, srcs, by_slot, name):
        self.calls += 1
        got = _exchange(srcs, by_slot, name, self.calls)
        self.ended = got[0][(0,) * got[0].ndim].astype(F32) * 0.0
        return got

    def _gather(self, group, at):
        keys = FWD_GROUPS[group]
        token = jnp.zeros((8, 128), F32) + at[(0,) * at.ndim].astype(F32) * 0.0 + (0.0 if self.ended is None else self.ended)
        shards = _cast_group(self.weights_f32, keys, token, "cast_%d" % group)
        got = self._exchange(shards, [False] * len(keys), "gather_%d" % group)
        self.gathered.update((k, _from_gathered(k[0], arr)) for k, arr in zip(keys, got))

    def weight(self, name, layer, at=None):
        if (name, layer) not in self.gathered:
            self._gather(0, at)
        group = next(g for g, keys in enumerate(FWD_GROUPS) if (name, layer) in keys)
        if group == self.opened + 1:
            self.opened = group
            if group + 1 < len(FWD_GROUPS):
                self._gather(group + 1, at)
        return self.gathered[name, layer]

    def after(self):
        return jnp.zeros((8, 128), F32) + self.ended

    def send(self, grads, shared=None):
        shared = shared or {}
        keys = list(grads) + list(shared)
        srcs = list(grads.values()) + list(shared.values())
        got = self._exchange(srcs, [True] * len(grads) + [False] * len(shared), "scatter_%d" % self.calls)
        self.got.update(zip(keys, got))
        return sum(g[0, 0, 0].astype(F32) * 0.0 for g in grads.values())

    def received(self):
        return self.got


def _adamw_math(g, w, m, v):
    m2 = ADAM_B1 * m + (1.0 - ADAM_B1) * g
    v2 = ADAM_B2 * v + (1.0 - ADAM_B2) * jnp.square(g)
    m_hat = m2 / (1.0 - ADAM_B1 ** ADAM_STEP)
    v_hat = v2 / (1.0 - ADAM_B2 ** ADAM_STEP)
    delta = -ADAM_LR * (m_hat / (jnp.sqrt(v_hat) + ADAM_EPS) + ADAM_WD * w)
    return delta, m2, v2


def _adamw(recv, w, m, v, name):
    nl, r, c = w.shape
    tr = _tile(r, 256)

    def body(*refs):
        g_refs = refs[:nl]
        w_ref, m_ref, v_ref, go_ref, d_ref, mo_ref, vo_ref = refs[nl:]
        for l in range(nl):
            @pl.when(pl.program_id(0) == l)
            def _():
                g = g_refs[l][0].astype(F32)
                for p in range(1, N_DEV):
                    g = g + g_refs[l][p].astype(F32)
                go_ref[...] = g
                d_ref[...], mo_ref[...], vo_ref[...] = _adamw_math(g, w_ref[...], m_ref[...], v_ref[...])

    def recv_spec(l):
        return pl.BlockSpec((N_DEV, tr, c), lambda layer, i: (0, jnp.where(layer == l, i, 0), 0))

    blk = pl.BlockSpec((None, tr, c), lambda l, i: (l, i, 0))
    return pl.pallas_call(
        body, name=name, grid=(nl, r // tr), out_shape=[SDS(w.shape, F32)] * 4,
        in_specs=[recv_spec(l) for l in range(nl)] + [blk, blk, blk],
        out_specs=[blk] * 4,
        compiler_params=_params(("arbitrary", "arbitrary")),
    )(*recv, w, m, v)


def _adamw_small(recv, loss_parts, w, m, v, name):
    n = len(w)

    def total(ref):
        t = ref[0]
        for p in range(1, N_DEV):
            t = t + ref[p]
        return t

    def body(*refs):
        g_refs, loss_ref = refs[:n], refs[n]
        w_refs, m_refs, v_refs = refs[n + 1:2 * n + 1], refs[2 * n + 1:3 * n + 1], refs[3 * n + 1:4 * n + 1]
        outs = refs[4 * n + 1:]
        for i in range(n):
            g = total(g_refs[i])
            outs[4 * i][...] = g
            outs[4 * i + 1][...], outs[4 * i + 2][...], outs[4 * i + 3][...] = _adamw_math(
                g, w_refs[i][...], m_refs[i][...], v_refs[i][...])
        outs[4 * n][...] = total(loss_ref)

    out_shape = [SDS(p.shape, F32) for p in w for _ in range(4)] + [SDS(loss_parts.shape[1:], F32)]
    outs = pl.pallas_call(body, name=name, out_shape=out_shape,
                          compiler_params=_params(None))(*recv, loss_parts, *w, *m, *v)
    return [outs[4 * i:4 * i + 4] for i in range(n)], outs[-1]


def kernel(x, mix_norm, ffn_norm, w_ffn_in, w_ffn_out, ab_w_in, ab_gn_gain, ab_w_pool, ab_pool_scale, ab_w_out, c_w_qkv, c_rel_bias, c_w_out, final_norm, loss_target, m_mix_norm, m_ffn_norm, m_w_ffn_in, m_w_ffn_out, m_ab_w_in, m_ab_gn_gain, m_ab_w_pool, m_ab_pool_scale, m_ab_w_out, m_c_w_qkv, m_c_rel_bias, m_c_w_out, m_final_norm, v_mix_norm, v_ffn_norm, v_w_ffn_in, v_w_ffn_out, v_ab_w_in, v_ab_gn_gain, v_ab_w_pool, v_ab_pool_scale, v_ab_w_out, v_c_w_qkv, v_c_rel_bias, v_c_w_out, v_final_norm):
    args = dict(locals())
    weights = {k: args[k] for k in BIG + SMALL}
    moments_m = {k: args["m_" + k] for k in BIG + SMALL}
    moments_v = {k: args["v_" + k] for k in BIG + SMALL}

    small = {k: weights[k] for k in SMALL}
    rows = lambda a: a.reshape(1, -1) if a.ndim == 1 else a

    comm = _Comm(weights)
    loss, dx, last_grads, gsmall = _local_step(x[0], loss_target[0], small, comm)
    comm.send(last_grads, {**{k: rows(gsmall[k]) for k in SMALL}, "loss": loss})
    recv = comm.received()

    outs = {}
    for k in BIG:
        layers = [recv[k, l] for l in range(weights[k].shape[0])]
        outs[k] = _adamw(layers, weights[k], moments_m[k], moments_v[k], "adamw_" + k)
    updated, total = _adamw_small([recv[k] for k in SMALL], recv["loss"], [rows(small[k]) for k in SMALL],
                                  [rows(moments_m[k]) for k in SMALL], [rows(moments_v[k]) for k in SMALL], "adamw_small")
    for k, parts in zip(SMALL, updated):
        outs[k] = [p.reshape(small[k].shape) for p in parts]

    order = SMALL[:2] + BIG[:2] + ("ab_w_in", "ab_gn_gain", "ab_w_pool", "ab_pool_scale", "ab_w_out",
                                   "c_w_qkv", "c_rel_bias", "c_w_out", "final_norm")
    result = [total[0, 0], dx[None]]
    for part in range(4):
        result += [outs[k][part] for k in order]
    return tuple(result)
```

```python
import functools

import jax
import jax.numpy as jnp
from jax import lax
from jax.experimental import pallas as pl
from jax.experimental.pallas import tpu as pltpu
from jax.experimental.pallas import tpu_sc as plsc

F32 = jnp.float32
BF16 = jnp.bfloat16
SDS = jax.ShapeDtypeStruct
MESH_AXES = ("x", "y", "c")
N_DEV = 8

D_MODEL = 1024
DEPTH = 4
CHUNK = 64
D_FF = 4 * D_MODEL
RMS_EPS = 1e-6
RET_WIDTH = 512
RET_HEADS = 4
RET_HEAD_DIM = 128
RET_ROPE_BASE = 10000.0
GN_EPS = 1e-5
POOL_WIDTH = 512
POOL_WINDOWS = (2, 4, 8, 16)
POOL_HALO = 16
AB_IN_WIDTH = 4 * RET_WIDTH + POOL_WIDTH
ATT_HEADS = 16
ATT_HEAD_DIM = 64
LEFT_CHUNKS = 8
REL_CLIP = 128
N_REL = 2 * REL_CLIP + 1
NEG_INF = -1e30

ADAM_LR = 0.001
ADAM_B1 = 0.9
ADAM_B2 = 0.999
ADAM_EPS = 1e-08
ADAM_WD = 0.01
ADAM_STEP = 10

TOKEN_TILE = 512
ATT_Q_TILE = 256
ATT_BACK = LEFT_CHUNKS * CHUNK // ATT_Q_TILE
ATT_K_TILE = (ATT_BACK + 1) * ATT_Q_TILE
ATT_PAIRS = 4
ATT_DIAG = 1024
REL_PAD = 384
VMEM_LIMIT_MB = 56

NT = (((1,), (1,)), ((), ()))
TN = (((0,), (0,)), ((), ()))


def _params(semantics, **kw):
    return pltpu.CompilerParams(dimension_semantics=semantics,
                                vmem_limit_bytes=VMEM_LIMIT_MB * 2 ** 20, **kw)


def _dot(a, b, dims=None):
    if dims is None:
        return jnp.dot(a, b, preferred_element_type=F32)
    return lax.dot_general(a, b, dims, preferred_element_type=F32)


def _bf(v):
    return v.astype(BF16)


def _tile(n, t):
    return min(n, t)


MXU_WIDTH = 256


def _mxu_group(nj, tn):
    return 2 if tn % MXU_WIDTH and (2 * tn) % MXU_WIDTH == 0 and nj % 2 == 0 else 1


def _w_tiles(w_ref, j, group):
    return w_ref[j] if group == 1 else jnp.concatenate([w_ref[j + t] for t in range(group)], axis=1)


def _w_cols(w_ref, j, group, c, width):
    return w_ref[j, :, c:c + width] if group == 1 else _w_tiles(w_ref, j, group)


def _norm_mm(x, gain, w, tn, z_dtype, relu2, name):
    s, d = x.shape
    nj = w.shape[0]
    tm = _tile(s, TOKEN_TILE)
    group = _mxu_group(nj, tn)

    def body(x_ref, g_ref, w_ref, h_ref, z_ref, *a_ref):
        xv = x_ref[...]
        r = lax.rsqrt(jnp.mean(xv * xv, axis=-1, keepdims=True) + RMS_EPS)
        h = _bf(xv * r * g_ref[...])
        h_ref[...] = h
        cw = tn if tn <= 512 else 512
        for j in range(0, nj, group):
            for c in range(0, tn, cw):
                z = _dot(h, _w_cols(w_ref, j, group, c, cw))
                cols = slice(j * tn + c, j * tn + c + group * cw)
                z_ref[:, cols] = z.astype(z_ref.dtype)
                if relu2:
                    a_ref[0][:, cols] = _bf(jnp.square(jnp.maximum(z, 0.0)))

    n = nj * tn
    out_shape = [SDS((s, d), BF16), SDS((s, n), z_dtype)]
    out_specs = [pl.BlockSpec((tm, d), lambda i: (i, 0)), pl.BlockSpec((tm, n), lambda i: (i, 0))]
    if relu2:
        out_shape.append(SDS((s, n), BF16))
        out_specs.append(pl.BlockSpec((tm, n), lambda i: (i, 0)))
    return pl.pallas_call(
        body, name=name, grid=(s // tm,), out_shape=out_shape,
        in_specs=[pl.BlockSpec((tm, d), lambda i: (i, 0)),
                  pl.BlockSpec((1, d), lambda i: (0, 0)),
                  pl.BlockSpec((nj, d, tn), lambda i: (0, 0, 0))],
        out_specs=out_specs,
        compiler_params=_params(("parallel",)),
    )(x, gain, w)


def _mm_res(parts, w, res, name):
    s, d = res.shape
    tm = _tile(s, TOKEN_TILE)
    widths = [p.shape[1] for p in parts]

    def body(*refs):
        a_refs = refs[:len(parts)]
        w_ref, res_ref, o_ref = refs[len(parts):]
        acc = res_ref[...]
        off = 0
        for a_ref, k in zip(a_refs, widths):
            acc = acc + _dot(a_ref[...], w_ref[off:off + k, :])
            off += k
        o_ref[...] = acc

    return pl.pallas_call(
        body, name=name, grid=(s // tm,), out_shape=SDS((s, d), F32),
        in_specs=[pl.BlockSpec((tm, k), lambda i: (i, 0)) for k in widths]
        + [pl.BlockSpec(w.shape, lambda i: (0, 0)), pl.BlockSpec((tm, d), lambda i: (i, 0))],
        out_specs=pl.BlockSpec((tm, d), lambda i: (i, 0)),
        compiler_params=_params(("parallel",)),
    )(*parts, w, res)


def _mm_nt_rows(dy, w, z, name):
    s, d = dy.shape
    k = w.shape[0]
    tm = _tile(s, TOKEN_TILE)
    tk = _tile(k, 1024)

    def body(dy_ref, w_ref, *rest):
        o_ref = rest[-1]
        dyb = _bf(dy_ref[...])
        for j in range(k // tk):
            cols = slice(j * tk, (j + 1) * tk)
            da = _dot(dyb, w_ref[cols, :], NT)
            if z is not None:
                da = da * (2.0 * jnp.maximum(rest[0][:, cols].astype(F32), 0.0))
            o_ref[:, cols] = _bf(da)

    in_specs = [pl.BlockSpec((tm, d), lambda i: (i, 0)), pl.BlockSpec((k, d), lambda i: (0, 0))]
    args = [dy, w]
    if z is not None:
        in_specs.append(pl.BlockSpec((tm, k), lambda i: (i, 0)))
        args.append(z)
    return pl.pallas_call(
        body, name=name, grid=(s // tm,), out_shape=SDS((s, k), BF16),
        in_specs=in_specs, out_specs=pl.BlockSpec((tm, k), lambda i: (i, 0)),
        compiler_params=_params(("parallel",)),
    )(*args)


def _mm_nt_normbwd(dz, w, x, gain, dres, name):
    s, d = x.shape
    nj, _, nc = w.shape
    tm = _tile(s, TOKEN_TILE)

    group = _mxu_group(nj, nc)

    halves = 2 if tm % 32 == 0 else 1

    def body(dz_ref, w_ref, x_ref, g_ref, dres_ref, dx_ref, dg_ref):
        @pl.when(pl.program_id(0) == 0)
        def _():
            dg_ref[...] = jnp.zeros_like(dg_ref)

        for half in range(halves):
            rows = slice(half * tm // halves, (half + 1) * tm // halves)
            dh = _dot(dz_ref[rows, 0:group * nc], _w_tiles(w_ref, 0, group), NT)
            for j in range(group, nj, group):
                dh = dh + _dot(dz_ref[rows, j * nc:(j + group) * nc], _w_tiles(w_ref, j, group), NT)
            xv = x_ref[rows, :]
            r = lax.rsqrt(jnp.mean(xv * xv, axis=-1, keepdims=True) + RMS_EPS)
            xn = xv * r
            dg_ref[...] += jnp.sum(dh * xn, axis=0, keepdims=True)
            dxh = dh * g_ref[...]
            dx_ref[rows, :] = dres_ref[rows, :] + r * (dxh - xn * jnp.mean(dxh * xn, axis=-1, keepdims=True))

    return pl.pallas_call(
        body, name=name, grid=(s // tm,), out_shape=[SDS((s, d), F32), SDS((1, d), F32)],
        in_specs=[pl.BlockSpec((tm, nj * nc), lambda i: (i, 0)),
                  pl.BlockSpec((nj, d, nc), lambda i: (0, 0, 0)),
                  pl.BlockSpec((tm, d), lambda i: (i, 0)),
                  pl.BlockSpec((1, d), lambda i: (0, 0)),
                  pl.BlockSpec((tm, d), lambda i: (i, 0))],
        out_specs=[pl.BlockSpec((tm, d), lambda i: (i, 0)), pl.BlockSpec((1, d), lambda i: (0, 0))],
        compiler_params=_params(("arbitrary",)),
    )(dz, w, x, gain, dres)


def _mm_tn(a, b, ka, nb, a_tiled, split, name, after=None):
    a_parts = list(a) if isinstance(a, (list, tuple)) else [a]
    s = a_parts[0].shape[0]
    tm = _tile(s, 2 * TOKEN_TILE)
    nm = s // tm
    nj = a_parts[0].shape[1] // ka if a_tiled and len(a_parts) == 1 else (1 if a_tiled else b.shape[1] // nb)
    axis, parts = split
    pr, pc = (ka // parts, nb) if axis == 0 else (ka, nb // parts)

    def body(*refs):
        a_refs, b_ref = refs[:len(a_parts)], refs[len(a_parts)]
        o_ref, acc = refs[-2:]
        m = pl.program_id(1)

        @pl.when(m == 0)
        def _():
            acc[...] = jnp.zeros_like(acc)

        av = a_refs[0][...] if len(a_refs) == 1 else jnp.concatenate([r[...] for r in a_refs], axis=1)
        acc[...] += _dot(_bf(av), _bf(b_ref[...]), TN)

        @pl.when(m == nm - 1)
        def _():
            for q in range(parts):
                piece = acc[q * pr:(q + 1) * pr, :] if axis == 0 else acc[:, q * pc:(q + 1) * pc]
                o_ref[q] = piece.astype(o_ref.dtype)

    return pl.pallas_call(
        body, name=name, grid=(nj, nm), out_shape=SDS((nj * parts, pr, pc), BF16),
        in_specs=([pl.BlockSpec((tm, ka), (lambda j, m: (m, j)) if a_tiled else (lambda j, m: (m, 0)))]
                  if len(a_parts) == 1 else [pl.BlockSpec((tm, p.shape[1]), lambda j, m: (m, 0)) for p in a_parts])
        + [pl.BlockSpec((tm, nb), (lambda j, m: (m, 0)) if a_tiled else (lambda j, m: (m, j)))]
        + [pl.BlockSpec(memory_space=pl.ANY)] * (after is not None),
        out_specs=pl.BlockSpec((parts, pr, pc), lambda j, m: (j, 0, 0)),
        scratch_shapes=[pltpu.VMEM((ka, nb), F32)],
        compiler_params=_params(("parallel", "arbitrary")),
    )(*a_parts, b, *([] if after is None else [after]))


def _final_loss(x, gain, target, name):
    s, d = x.shape
    tm = _tile(s, TOKEN_TILE)

    def body(x_ref, g_ref, t_ref, loss_ref, dx_ref, dg_ref):
        @pl.when(pl.program_id(0) == 0)
        def _():
            loss_ref[...] = jnp.zeros_like(loss_ref)
            dg_ref[...] = jnp.zeros_like(dg_ref)

        xv = x_ref[...]
        r = lax.rsqrt(jnp.mean(xv * xv, axis=-1, keepdims=True) + RMS_EPS)
        xn = xv * r
        err = xn * g_ref[...] - t_ref[...]
        loss_ref[...] += (0.5 / d) * jnp.sum(err * err)
        dy = err * (1.0 / d)
        dg_ref[...] += jnp.sum(dy * xn, axis=0, keepdims=True)
        dxh = dy * g_ref[...]
        dx_ref[...] = r * (dxh - xn * jnp.mean(dxh * xn, axis=-1, keepdims=True))

    return pl.pallas_call(
        body, name=name, grid=(s // tm,),
        out_shape=[SDS((8, 128), F32), SDS((s, d), F32), SDS((1, d), F32)],
        in_specs=[pl.BlockSpec((tm, d), lambda i: (i, 0)), pl.BlockSpec((1, d), lambda i: (0, 0)),
                  pl.BlockSpec((tm, d), lambda i: (i, 0))],
        out_specs=[pl.BlockSpec((8, 128), lambda i: (0, 0)), pl.BlockSpec((tm, d), lambda i: (i, 0)),
                   pl.BlockSpec((1, d), lambda i: (0, 0))],
        compiler_params=_params(("arbitrary",)),
    )(x, gain, target)


def _retention_tables(s):
    half = RET_HEAD_DIM // 2
    inv_freq = 1.0 / (RET_ROPE_BASE ** jnp.linspace(0.0, 1.0, half, dtype=F32))
    ang = jnp.arange(s, dtype=F32)[:, None] * inv_freq[None, :]
    cos, sin = jnp.cos(ang), jnp.sin(ang)
    cos_e = jnp.concatenate([cos, cos], axis=-1)
    sin_s = jnp.concatenate([-sin, sin], axis=-1)
    log_g = jnp.log1p(-jnp.power(2.0, -5.0 - jnp.arange(RET_HEADS, dtype=F32)))
    pos = jnp.arange(CHUNK, dtype=F32)
    dmat = jnp.exp(jnp.abs(pos[:, None] - pos[None, :])[None] * log_g[:, None, None])
    qdec = jnp.exp((pos[None, :] + 1.0) * log_g[:, None])
    kdec = jnp.exp((CHUNK - 1.0 - pos[None, :]) * log_g[:, None])
    lam = jnp.exp(CHUNK * log_g)
    wide = (RET_HEADS, CHUNK, RET_HEAD_DIM)
    return dict(cos=cos_e, sin=sin_s, dmat=dmat,
                qdec=jnp.broadcast_to(qdec[:, :, None], wide),
                kdec=jnp.broadcast_to(kdec[:, :, None], wide),
                lam=jnp.broadcast_to(lam[:, None, None], (RET_HEADS, RET_HEAD_DIM, RET_HEAD_DIM)))


def _swap_pairs(t):
    return pltpu.roll(t, RET_HEAD_DIM // 2, 1)


def _split_pairs(w, inverse=False):
    lead, nqk = w.shape[:-1], 2 * RET_WIDTH
    shape = (2 * RET_HEADS, 2, RET_HEAD_DIM // 2) if inverse else (2 * RET_HEADS, RET_HEAD_DIM // 2, 2)
    qk = jnp.swapaxes(w[..., :nqk].reshape(lead + shape), -1, -2).reshape(lead + (nqk,))
    return jnp.concatenate([qk, w[..., nqk:]], axis=-1)


def _head(h):
    return slice(h * RET_HEAD_DIM, (h + 1) * RET_HEAD_DIM)


def _ret_common_specs(tb, blk):
    zs = [pl.BlockSpec((tb, RET_WIDTH), functools.partial(lambda j, i: (blk(i), j), j)) for j in range(4)]
    tabs = [pl.BlockSpec((tb, RET_HEAD_DIM), lambda i: (blk(i), 0))] * 2
    consts = [pl.BlockSpec((1, RET_WIDTH), lambda i: (0, 0)),
              pl.BlockSpec((RET_HEADS, CHUNK, CHUNK), lambda i: (0, 0, 0)),
              pl.BlockSpec((RET_HEADS, CHUNK, RET_HEAD_DIM), lambda i: (0, 0, 0)),
              pl.BlockSpec((RET_HEADS, CHUNK, RET_HEAD_DIM), lambda i: (0, 0, 0)),
              pl.BlockSpec((RET_HEADS, RET_HEAD_DIM, RET_HEAD_DIM), lambda i: (0, 0, 0))]
    return zs + tabs + consts


def _ret_fwd(z, tabs, gn_gain, name):
    s = z.shape[0]
    tb = _tile(s, TOKEN_TILE)
    ncb = tb // CHUNK
    scale = RET_HEAD_DIM ** -0.5

    def body(q_ref, k_ref, v_ref, g_ref, cos_ref, sin_ref, gain_ref, dm_ref, qd_ref, kd_ref, lam_ref,
             o_ref, st_ref, ret_ref, s_scr, qr_scr, kr_scr):
        @pl.when(pl.program_id(0) == 0)
        def _():
            s_scr[...] = jnp.zeros_like(s_scr)

        cosv, sinv = cos_ref[...], sin_ref[...]
        for h in range(RET_HEADS):
            qh, kh = q_ref[:, _head(h)], k_ref[:, _head(h)]
            qr_scr[:, _head(h)] = qh * cosv + _swap_pairs(qh) * sinv
            kr_scr[:, _head(h)] = (kh * cosv + _swap_pairs(kh) * sinv) * scale

        def chunk(c, carry):
            rows = pl.ds(pl.multiple_of(c * CHUNK, CHUNK), CHUNK)
            for h in range(RET_HEADS):
                qc, kc, vc = qr_scr[rows, _head(h)], kr_scr[rows, _head(h)], v_ref[rows, _head(h)]
                a = _dot(_bf(qc), _bf(kc), NT) * dm_ref[h]
                st = s_scr[h]
                st_ref[c, h] = st
                o_ref[rows, _head(h)] = _dot(_bf(a), _bf(vc)) + _dot(_bf(qc * qd_ref[h]), _bf(st))
                s_scr[h] = st * lam_ref[h] + _dot(_bf(kc * kd_ref[h]), _bf(vc), TN)
            return carry

        lax.fori_loop(0, ncb, chunk, 0, unroll=4)
        for h in range(RET_HEADS):
            o = o_ref[:, _head(h)]
            mu = jnp.mean(o, axis=-1, keepdims=True)
            oc = o - mu
            y = oc * lax.rsqrt(jnp.mean(oc * oc, axis=-1, keepdims=True) + GN_EPS) * gain_ref[:, _head(h)]
            g = g_ref[:, _head(h)]
            ret_ref[:, _head(h)] = _bf(g / (1.0 + jnp.exp(-g)) * y)

    nc = s // CHUNK
    return pl.pallas_call(
        body, name=name, grid=(s // tb,),
        out_shape=[SDS((s, RET_WIDTH), F32), SDS((nc, RET_HEADS, RET_HEAD_DIM, RET_HEAD_DIM), F32),
                   SDS((s, RET_WIDTH), BF16)],
        in_specs=_ret_common_specs(tb, lambda i: i),
        out_specs=[pl.BlockSpec((tb, RET_WIDTH), lambda i: (i, 0)),
                   pl.BlockSpec((ncb, RET_HEADS, RET_HEAD_DIM, RET_HEAD_DIM), lambda i: (i, 0, 0, 0)),
                   pl.BlockSpec((tb, RET_WIDTH), lambda i: (i, 0))],
        scratch_shapes=[pltpu.VMEM((RET_HEADS, RET_HEAD_DIM, RET_HEAD_DIM), F32),
                        pltpu.VMEM((tb, RET_WIDTH), F32), pltpu.VMEM((tb, RET_WIDTH), F32)],
        compiler_params=_params(("arbitrary",)),
    )(z, z, z, z, tabs["cos"], tabs["sin"], gn_gain, tabs["dmat"], tabs["qdec"], tabs["kdec"], tabs["lam"])


def _ret_bwd(z, tabs, gn_gain, o_pre, states, du, name):
    s = z.shape[0]
    tb = _tile(s, TOKEN_TILE)
    ncb = tb // CHUNK
    nblk = s // tb
    scale = RET_HEAD_DIM ** -0.5
    rev = lambda i: nblk - 1 - i

    def body(q_ref, k_ref, v_ref, g_ref, cos_ref, sin_ref, gain_ref, dm_ref, qd_ref, kd_ref, lam_ref,
             o_ref, st_ref, dret_ref, dz_ref, dgain_ref, g_scr, qr_scr, kr_scr, do_scr, dq_scr, dk_scr):
        @pl.when(pl.program_id(0) == 0)
        def _():
            g_scr[...] = jnp.zeros_like(g_scr)
            dgain_ref[...] = jnp.zeros_like(dgain_ref)

        cosv, sinv = cos_ref[...], sin_ref[...]
        for h in range(RET_HEADS):
            hs = _head(h)
            qh, kh = q_ref[:, hs], k_ref[:, hs]
            qr_scr[:, hs] = qh * cosv + _swap_pairs(qh) * sinv
            kr_scr[:, hs] = (kh * cosv + _swap_pairs(kh) * sinv) * scale
            o = o_ref[:, hs]
            mu = jnp.mean(o, axis=-1, keepdims=True)
            oc = o - mu
            rstd = lax.rsqrt(jnp.mean(oc * oc, axis=-1, keepdims=True) + GN_EPS)
            yh = oc * rstd
            gain = gain_ref[:, hs]
            g = g_ref[:, hs]
            sg = 1.0 / (1.0 + jnp.exp(-g))
            dret = dret_ref[:, hs].astype(F32)
            dy = dret * (g * sg)
            dz_ref[:, 3 * RET_WIDTH + h * RET_HEAD_DIM:3 * RET_WIDTH + (h + 1) * RET_HEAD_DIM] = _bf(
                dret * (yh * gain) * (sg * (1.0 + g * (1.0 - sg))))
            dgain_ref[:, hs] += jnp.sum(dy * yh, axis=0, keepdims=True)
            dyh = dy * gain
            do_scr[:, hs] = rstd * (dyh - jnp.mean(dyh, axis=-1, keepdims=True)
                                    - yh * jnp.mean(dyh * yh, axis=-1, keepdims=True))

        def chunk(cc, carry):
            c = ncb - 1 - cc
            rows = pl.ds(pl.multiple_of(c * CHUNK, CHUNK), CHUNK)
            for h in range(RET_HEADS):
                hs = _head(h)
                qc, kc, vc, doc = _bf(qr_scr[rows, hs]), _bf(kr_scr[rows, hs]), _bf(v_ref[rows, hs]), _bf(do_scr[rows, hs])
                qdc, kdc = qd_ref[h], kd_ref[h]
                st, gs = _bf(st_ref[c, h]), g_scr[h]
                gsb = _bf(gs)
                dm = dm_ref[h]
                p = _bf(_dot(qc, kc, NT) * dm)
                da = _bf(_dot(doc, vc, NT) * dm)
                kt = _bf(kr_scr[rows, hs] * kdc)
                qt = _bf(qr_scr[rows, hs] * qdc)
                dz_ref[rows, 2 * RET_WIDTH + h * RET_HEAD_DIM:2 * RET_WIDTH + (h + 1) * RET_HEAD_DIM] = _bf(
                    _dot(p, doc, TN) + _dot(kt, gsb))
                dq_scr[rows, hs] = _dot(da, kc) + _dot(doc, st, NT) * qdc
                dk_scr[rows, hs] = _dot(da, qc, TN) + _dot(vc, gsb, NT) * kdc
                g_scr[h] = gs * lam_ref[h] + _dot(qt, doc, TN)
            return carry

        lax.fori_loop(0, ncb, chunk, 0, unroll=4)
        for h in range(RET_HEADS):
            hs = _head(h)
            dq, dk = dq_scr[:, hs], dk_scr[:, hs]
            dz_ref[:, h * RET_HEAD_DIM:(h + 1) * RET_HEAD_DIM] = _bf(dq * cosv - _swap_pairs(dq) * sinv)
            dz_ref[:, RET_WIDTH + h * RET_HEAD_DIM:RET_WIDTH + (h + 1) * RET_HEAD_DIM] = _bf(
                (dk * cosv - _swap_pairs(dk) * sinv) * scale)

    return pl.pallas_call(
        body, name=name, grid=(nblk,),
        out_shape=[SDS((s, AB_IN_WIDTH), BF16), SDS((1, RET_WIDTH), F32)],
        in_specs=_ret_common_specs(tb, rev)
        + [pl.BlockSpec((tb, RET_WIDTH), lambda i: (rev(i), 0)),
           pl.BlockSpec((ncb, RET_HEADS, RET_HEAD_DIM, RET_HEAD_DIM), lambda i: (rev(i), 0, 0, 0)),
           pl.BlockSpec((tb, RET_WIDTH), lambda i: (rev(i), 0))],
        out_specs=[pl.BlockSpec((tb, 4 * RET_WIDTH), lambda i: (rev(i), 0)),
                   pl.BlockSpec((1, RET_WIDTH), lambda i: (0, 0))],
        scratch_shapes=[pltpu.VMEM((RET_HEADS, RET_HEAD_DIM, RET_HEAD_DIM), F32)]
        + [pltpu.VMEM((tb, RET_WIDTH), F32)] * 5,
        compiler_params=_params(("arbitrary",)),
    )(z, z, z, z, tabs["cos"], tabs["sin"], gn_gain, tabs["dmat"], tabs["qdec"], tabs["kdec"], tabs["lam"],
      o_pre, states, du)


POOL_COL = 4 * RET_WIDTH // POOL_WIDTH


def _pooled(cur, prev, t0):
    tm = cur.shape[0]
    xx = jnp.concatenate([prev, cur], axis=0)
    sums = {1: xx}
    w = 1
    while w < POOL_WINDOWS[-1]:
        sums[2 * w] = sums[w] + pltpu.roll(sums[w], w, 0)
        w *= 2
    t = t0 + lax.broadcasted_iota(jnp.int32, (tm, 128), 0)
    outs = []
    for gi, w in enumerate(POOL_WINDOWS):
        cols = slice(gi * 128, (gi + 1) * 128)
        cnt = jnp.minimum(t + 1, w).astype(F32)
        outs.append(sums[w][POOL_HALO:, cols] / cnt - cur[:, cols])
    return outs


def _pool_fwd(z, w_pool, scale, name):
    s = z.shape[0]
    tm = _tile(s, TOKEN_TILE)
    hb = tm // POOL_HALO

    def body(p_ref, prev_ref, w_ref, sc_ref, o_ref):
        i = pl.program_id(0)
        prev = jnp.where(i > 0, prev_ref[...], 0.0)
        pooled = _pooled(p_ref[...], prev, i * tm)
        for gi in range(len(POOL_WINDOWS)):
            cols = slice(gi * 128, (gi + 1) * 128)
            o_ref[:, cols] = _bf(_dot(_bf(pooled[gi]), _bf(w_ref[gi])) * sc_ref[:, cols])

    return pl.pallas_call(
        body, name=name, grid=(s // tm,), out_shape=SDS((s, POOL_WIDTH), BF16),
        in_specs=[pl.BlockSpec((tm, POOL_WIDTH), lambda i: (i, POOL_COL)),
                  pl.BlockSpec((POOL_HALO, POOL_WIDTH), lambda i: (jnp.maximum(i * hb - 1, 0), POOL_COL)),
                  pl.BlockSpec(w_pool.shape, lambda i: (0, 0, 0)),
                  pl.BlockSpec((1, POOL_WIDTH), lambda i: (0, 0))],
        out_specs=pl.BlockSpec((tm, POOL_WIDTH), lambda i: (i, 0)),
        compiler_params=_params(("parallel",)),
    )(z, z, w_pool, scale)


def _pool_bwd(z, w_pool, scale, du, dz, name):
    s = z.shape[0]
    tm = _tile(s, TOKEN_TILE)
    hb = tm // POOL_HALO
    nblk = s // tm
    last_halo = s // POOL_HALO - 1

    def body(p_ref, prev_ref, w_ref, sc_ref, do_ref, don_ref, dz_ref, dp_ref, dw_ref, dsc_ref):
        i = pl.program_id(0)

        @pl.when(i == 0)
        def _():
            dw_ref[...] = jnp.zeros_like(dw_ref)
            dsc_ref[...] = jnp.zeros_like(dsc_ref)

        prev = jnp.where(i > 0, prev_ref[...], 0.0)
        pooled = _pooled(p_ref[...], prev, i * tm)
        dout = do_ref[...].astype(F32)
        dout_next = jnp.where(i < nblk - 1, don_ref[...].astype(F32), 0.0)
        sc = sc_ref[...]
        dmix = jnp.concatenate([dout * sc, dout_next * sc], axis=0)
        n = tm + POOL_HALO
        t = i * tm + lax.broadcasted_iota(jnp.int32, (n, 128), 0)
        for gi, w in enumerate(POOL_WINDOWS):
            cols = slice(gi * 128, (gi + 1) * 128)
            wg = _bf(w_ref[gi])
            pg = _bf(pooled[gi])
            dsc_ref[:, cols] += jnp.sum(dout[:, cols] * _dot(pg, wg), axis=0, keepdims=True)
            dw_ref[gi] += _dot(pg, _bf(dmix[:tm, cols]), TN)
            dpool = _dot(_bf(dmix[:, cols]), wg, NT)
            acc = dpool / jnp.minimum(t + 1, w).astype(F32)
            step = 1
            while step < w:
                acc = acc + pltpu.roll(acc, n - step, 0)
                step *= 2
            dp_ref[:, cols] = _bf(acc[:tm] - dpool[:tm])

    return pl.pallas_call(
        body, name=name, grid=(nblk,),
        out_shape=[SDS(dz.shape, BF16), SDS(w_pool.shape, F32), SDS((1, POOL_WIDTH), F32)],
        in_specs=[pl.BlockSpec((tm, POOL_WIDTH), lambda i: (i, POOL_COL)),
                  pl.BlockSpec((POOL_HALO, POOL_WIDTH), lambda i: (jnp.maximum(i * hb - 1, 0), POOL_COL)),
                  pl.BlockSpec(w_pool.shape, lambda i: (0, 0, 0)),
                  pl.BlockSpec((1, POOL_WIDTH), lambda i: (0, 0)),
                  pl.BlockSpec((tm, POOL_WIDTH), lambda i: (i, 1)),
                  pl.BlockSpec((POOL_HALO, POOL_WIDTH), lambda i: (jnp.minimum((i + 1) * hb, last_halo), 1)),
                  pl.BlockSpec(memory_space=pl.ANY)],
        out_specs=[pl.BlockSpec((tm, POOL_WIDTH), lambda i: (i, POOL_COL)),
                   pl.BlockSpec(w_pool.shape, lambda i: (0, 0, 0)),
                   pl.BlockSpec((1, POOL_WIDTH), lambda i: (0, 0))],
        input_output_aliases={6: 0},
        compiler_params=_params(("arbitrary",)),
    )(z, z, w_pool, scale, du, du, dz)


def _rel_onehot():
    r = lax.broadcasted_iota(jnp.int32, (REL_PAD, ATT_DIAG), 0)
    c = lax.broadcasted_iota(jnp.int32, (REL_PAD, ATT_DIAG), 1)
    rel = jnp.where(c < ATT_K_TILE, jnp.clip(LEFT_CHUNKS * CHUNK - c, -REL_CLIP, REL_CLIP) + REL_CLIP,
                    2 * REL_CLIP)
    return (rel == r).astype(BF16)


def _split3(v):
    hi = _bf(v)
    r1 = v - hi.astype(F32)
    mid = _bf(r1)
    return hi, mid, _bf(r1 - mid.astype(F32))


def _skew(v, sign):
    row = lax.broadcasted_iota(jnp.int32, v.shape, 0)
    bit = 1
    while bit < ATT_Q_TILE:
        shift = bit if sign > 0 else ATT_DIAG - bit
        v = jnp.where((row & bit) != 0, pltpu.roll(v, shift, 1), v)
        bit *= 2
    return v


def _attn_bias(rel_bias, name):
    def body(t_ref, o_ref):
        oh = _rel_onehot()
        base = sum(_dot(part, oh) for part in _split3(t_ref[0]))
        full = _skew(jnp.broadcast_to(base[0:1], (ATT_Q_TILE, ATT_DIAG)), +1)[:, :ATT_K_TILE]
        qc = lax.broadcasted_iota(jnp.int32, full.shape, 0) // CHUNK
        kc = lax.broadcasted_iota(jnp.int32, full.shape, 1) // CHUNK
        o_ref[0] = jnp.where((kc >= qc) & (kc <= qc + LEFT_CHUNKS), full, NEG_INF)

    t8 = jnp.broadcast_to(rel_bias[:, None, :], (ATT_HEADS, 8, REL_PAD))
    return pl.pallas_call(
        body, name=name, grid=(ATT_HEADS,), out_shape=SDS((ATT_HEADS, ATT_Q_TILE, ATT_K_TILE), F32),
        in_specs=[pl.BlockSpec((1, 8, REL_PAD), lambda h: (h, 0, 0))],
        out_specs=pl.BlockSpec((1, ATT_Q_TILE, ATT_K_TILE), lambda h: (h, 0, 0)),
        compiler_params=_params(("parallel",)),
    )(t8)


def _attn_dbias(dbias, name):
    def body(d_ref, o_ref):
        pad = jnp.zeros((ATT_Q_TILE, ATT_DIAG - ATT_K_TILE), F32)
        diag = _skew(jnp.concatenate([d_ref[0], pad], axis=1), -1)
        col = jnp.sum(diag, axis=0, keepdims=True)
        oh = _rel_onehot()
        col8 = jnp.broadcast_to(col, (8, ATT_DIAG))
        o_ref[0] = sum(_dot(part, oh, NT) for part in _split3(col8))

    out = pl.pallas_call(
        body, name=name, grid=(ATT_HEADS,), out_shape=SDS((ATT_HEADS, 8, REL_PAD), F32),
        in_specs=[pl.BlockSpec((1, ATT_Q_TILE, ATT_K_TILE), lambda h: (h, 0, 0))],
        out_specs=pl.BlockSpec((1, 8, REL_PAD), lambda h: (h, 0, 0)),
        compiler_params=_params(("parallel",)),
    )(dbias)
    return out[:, 0, :]


ATT_WIDTH = 128 * ATT_PAIRS
ATT_GROUPS = D_MODEL // ATT_WIDTH


def _attn_specs(nq):
    def tile(off, back):
        return pl.BlockSpec((ATT_Q_TILE, ATT_WIDTH),
                            lambda g, i: (jnp.maximum(jnp.minimum(i, nq - 1) - back, 0), off + g))

    backs = [ATT_BACK - b for b in range(ATT_BACK + 1)]
    return ([tile(0, 0)] + [tile(ATT_GROUPS, b) for b in backs] + [tile(2 * ATT_GROUPS, b) for b in backs]
            + [pl.BlockSpec((2 * ATT_PAIRS, ATT_Q_TILE, ATT_K_TILE), lambda g, i: (g, 0, 0))])


def _attn_weights(qh, k2, bias, i, masked):
    sc = _dot(qh, k2, NT) + bias
    if masked:
        kpos = (i - ATT_BACK) * ATT_Q_TILE + lax.broadcasted_iota(jnp.int32, sc.shape, 1)
        sc = jnp.where(kpos >= 0, sc, NEG_INF)
    e = jnp.exp(sc - jnp.max(sc, axis=-1, keepdims=True))
    return e, 1.0 / jnp.sum(e, axis=-1, keepdims=True)


def _first_head():
    return lax.broadcasted_iota(jnp.int32, (ATT_Q_TILE, 128), 1) < ATT_HEAD_DIM


def _pair_operands(q_ref, k_refs, v_refs, pp):
    cols = slice(pp * 128, (pp + 1) * 128)
    q2 = q_ref[:, cols] * ATT_HEAD_DIM ** -0.5
    k2 = jnp.concatenate([r[:, cols] for r in k_refs], axis=0)
    v2 = jnp.concatenate([r[:, cols] for r in v_refs], axis=0)
    return cols, q2, k2, v2


def _attn_fwd(z, bias, name):
    s = z.shape[0]
    nq = s // ATT_Q_TILE
    nt = ATT_BACK + 1

    def body(q_ref, *rest):
        k_refs, v_refs, (b_ref, o_ref) = rest[:nt], rest[nt:2 * nt], rest[2 * nt:]
        i = pl.program_id(1)
        first = _first_head()

        def compute(masked):
            for pp in range(ATT_PAIRS):
                cols, q2, k2, v2 = _pair_operands(q_ref, k_refs, v_refs, pp)
                outs = []
                for hh in range(2):
                    qh = jnp.where(first if hh == 0 else ~first, q2, 0)
                    e, inv = _attn_weights(qh, k2, b_ref[2 * pp + hh], i, masked)
                    outs.append(_dot(_bf(e), v2) * inv)
                o_ref[:, cols] = _bf(jnp.where(first, outs[0], outs[1]))

        pl.when(i < ATT_BACK)(lambda: compute(True))
        pl.when(i >= ATT_BACK)(lambda: compute(False))

    return pl.pallas_call(
        body, name=name, grid=(ATT_GROUPS, nq), out_shape=SDS((s, D_MODEL), BF16),
        in_specs=_attn_specs(nq),
        out_specs=pl.BlockSpec((ATT_Q_TILE, ATT_WIDTH), lambda g, i: (i, g)),
        compiler_params=_params(("parallel", "parallel")),
    )(*([z] * (1 + 2 * nt)), bias)


def _attn_bwd(z, bias, o, do, name):
    s = z.shape[0]
    nq = s // ATT_Q_TILE
    nt = ATT_BACK + 1

    def body(q_ref, *rest):
        k_refs, v_refs = rest[:nt], rest[nt:2 * nt]
        b_ref, o_ref, do_ref, dq_ref, dk_ref, dv_ref, db_ref, dk_acc, dv_acc = rest[2 * nt:]
        i = pl.program_id(1)
        first = _first_head()

        @pl.when(i == 0)
        def _():
            db_ref[...] = jnp.zeros_like(db_ref)
            dk_acc[...] = jnp.zeros_like(dk_acc)
            dv_acc[...] = jnp.zeros_like(dv_acc)

        def compute(masked):
            for pp in range(ATT_PAIRS):
                cols, q2, k2, v2 = _pair_operands(q_ref, k_refs, v_refs, pp)
                do2 = do_ref[:, cols].astype(F32)
                prod = do2 * o_ref[:, cols].astype(F32)
                dqs, dk, dv = [], None, None
                for hh in range(2):
                    mine = first if hh == 0 else ~first
                    qh = jnp.where(mine, q2, 0)
                    e, inv = _attn_weights(qh, k2, b_ref[2 * pp + hh], i, masked)
                    delta = jnp.sum(jnp.where(mine, prod, 0.0), axis=-1, keepdims=True) * inv
                    doh = _bf(jnp.where(mine, do2 * inv, 0.0))
                    ds = e * (_dot(doh, v2, NT) - delta)
                    db_ref[2 * pp + hh] += ds
                    dsb = _bf(ds)
                    dqs.append(_dot(dsb, k2))
                    dkh, dvh = _dot(dsb, qh, TN), _dot(_bf(e), doh, TN)
                    dk, dv = (dkh, dvh) if hh == 0 else (dk + dkh, dv + dvh)
                dq_ref[:, cols] = _bf(jnp.where(first, dqs[0], dqs[1]) * ATT_HEAD_DIM ** -0.5)
                for b in range(nt):
                    slot = (i + b + 1) % nt
                    rows = slice(b * ATT_Q_TILE, (b + 1) * ATT_Q_TILE)
                    if b < ATT_BACK:
                        dk_acc[slot, :, cols] += dk[rows]
                        dv_acc[slot, :, cols] += dv[rows]
                    else:
                        dk_acc[slot, :, cols] = dk[rows]
                        dv_acc[slot, :, cols] = dv[rows]

        pl.when(i < ATT_BACK)(lambda: compute(True))
        pl.when((i >= ATT_BACK) & (i < nq))(lambda: compute(False))
        done = (i + 1) % nt
        dk_ref[...] = _bf(dk_acc[done])
        dv_ref[...] = _bf(dv_acc[done])

    tile = pl.BlockSpec((ATT_Q_TILE, ATT_WIDTH), lambda g, i: (jnp.minimum(i, nq - 1), g))
    late = pl.BlockSpec((ATT_Q_TILE, ATT_WIDTH), lambda g, i: (jnp.maximum(i - ATT_BACK, 0), g))
    ring = pltpu.VMEM((nt, ATT_Q_TILE, ATT_WIDTH), F32)
    return pl.pallas_call(
        body, name=name, grid=(ATT_GROUPS, nq + ATT_BACK),
        out_shape=[SDS((s, D_MODEL), BF16)] * 3 + [SDS((ATT_HEADS, ATT_Q_TILE, ATT_K_TILE), F32)],
        in_specs=_attn_specs(nq) + [tile, tile],
        out_specs=[tile, late, late, pl.BlockSpec((2 * ATT_PAIRS, ATT_Q_TILE, ATT_K_TILE), lambda g, i: (g, 0, 0))],
        scratch_shapes=[ring, ring],
        compiler_params=_params(("parallel", "arbitrary")),
    )(*([z] * (1 + 2 * nt)), bias, o, do)


FWD_GROUPS = (
    (("ab_w_in", 0),),
    (("ab_w_out", 0), ("w_ffn_in", 0)),
    (("w_ffn_out", 0),),
    (("c_w_qkv", 0),),
    (("c_w_out", 0), ("w_ffn_in", 1), ("w_ffn_out", 1), ("ab_w_in", 1)),
    (("ab_w_out", 1), ("w_ffn_in", 2), ("w_ffn_out", 2), ("c_w_qkv", 1)),
    (("c_w_out", 1), ("w_ffn_in", 3), ("w_ffn_out", 3)),
)


def _local_step(x, target, small, comm):
    s = x.shape[0]
    tabs = _retention_tables(s)
    saved, w = [], comm.weight
    for layer in range(DEPTH):
        i = layer // 2
        sv = {"x0": x}
        g_mix = small["mix_norm"][layer:layer + 1]
        if layer % 2 == 0:
            at = (x,) + tuple(tabs.values()) if layer == 0 else x
            sv["h1"], sv["z"] = _norm_mm(x, g_mix, w("ab_w_in", i, at), AB_IN_WIDTH, F32, False, "ab_in_fwd")
            gn = small["ab_gn_gain"][i:i + 1]
            sv["o_pre"], sv["states"], ret = _ret_fwd(sv["z"], tabs, gn, "ret_fwd")
            pool = _pool_fwd(sv["z"], small["ab_w_pool"][i], small["ab_pool_scale"][i:i + 1], "pool_fwd")
            sv["u"] = (ret, pool)
            x = _mm_res([ret, pool], w("ab_w_out", i, ret), x, "ab_out_fwd")
        else:
            sv["h1"], sv["z"] = _norm_mm(x, g_mix, w("c_w_qkv", i, x), 3 * D_MODEL // N_DEV, BF16, False, "qkv_fwd")
            rb = jnp.pad(small["c_rel_bias"][i], ((0, 0), (0, REL_PAD - N_REL)))
            sv["bias"] = _attn_bias(rb, "attn_bias")
            sv["o"] = _attn_fwd(sv["z"], sv["bias"], "attn_fwd")
            x = _mm_res([sv["o"]], w("c_w_out", i, sv["o"]), x, "c_out_fwd")
        sv["x1"] = x
        sv["h2"], sv["z1"], sv["a"] = _norm_mm(x, small["ffn_norm"][layer:layer + 1], w("w_ffn_in", layer, x),
                                               D_FF // N_DEV, BF16, True, "ffn_in_fwd")
        x = _mm_res([sv["a"]], w("w_ffn_out", layer, sv["a"]), x, "ffn_out_fwd")
        saved.append(sv)

    loss, dx, d_final = _final_loss(x, small["final_norm"][None, :], target, "final_loss")

    gs = {k: [None] * DEPTH for k in ("mix_norm", "ffn_norm")}
    for k in ("ab_gn_gain", "ab_w_pool", "ab_pool_scale", "c_rel_bias"):
        gs[k] = [None] * (DEPTH // 2)
    gs["final_norm"] = d_final[0]
    tok = jnp.zeros((), F32)
    for layer in reversed(range(DEPTH)):
        i = layer // 2
        sv = saved[layer]
        dz1 = _mm_nt_rows(dx, w("w_ffn_out", layer), sv["z1"], "ffn_out_bwd")
        gw = {("w_ffn_out", layer): _mm_tn(sv["a"], dx, 1024, D_MODEL, True, (0, 2), "ffn_out_dw"),
              ("w_ffn_in", layer): _mm_tn(sv["h2"], dz1, D_MODEL, 1024, False, (1, 2), "ffn_in_dw",
                                          comm.after() if layer == 0 else None)}
        dx, dg = _mm_nt_normbwd(dz1, w("w_ffn_in", layer), sv["x1"], small["ffn_norm"][layer:layer + 1] + tok, dx,
                                "ffn_in_bwd")
        gs["ffn_norm"][layer] = dg[0]
        if layer == 0:
            tok = comm.send(gw)
            gw = {}
        g_mix = small["mix_norm"][layer:layer + 1] + tok
        if layer % 2 == 0:
            du = _mm_nt_rows(dx, w("ab_w_out", i), None, "mix_out_bwd")
            gw["ab_w_out", i] = _mm_tn(sv["u"], dx, D_MODEL, D_MODEL, True, (0, N_DEV), "mix_out_dw")
            gn = small["ab_gn_gain"][i:i + 1]
            dz, dgn = _ret_bwd(sv["z"], tabs, gn, sv["o_pre"], sv["states"], du, "ret_bwd")
            dz, dwp, dsc = _pool_bwd(sv["z"], small["ab_w_pool"][i], small["ab_pool_scale"][i:i + 1], du, dz, "pool_bwd")
            gs["ab_gn_gain"][i], gs["ab_w_pool"][i], gs["ab_pool_scale"][i] = dgn[0], dwp, dsc[0]
            gw["ab_w_in", i] = _to_shard_major(_mm_tn(sv["h1"], dz, D_MODEL, AB_IN_WIDTH // 2, False, (1, 1), "ab_in_dw",
                                                      comm.after()))
            dx, dg = _mm_nt_normbwd(dz, w("ab_w_in", i), sv["x0"], g_mix, dx, "ab_in_bwd")
        else:
            do = _mm_nt_rows(dx, w("c_w_out", i), None, "mix_out_bwd")
            gw["c_w_out", i] = _mm_tn(sv["o"], dx, D_MODEL, D_MODEL, True, (0, N_DEV), "mix_out_dw")
            dq, dk, dv, dbias = _attn_bwd(sv["z"], sv["bias"], sv["o"], do, "attn_bwd")
            gs["c_rel_bias"][i] = _attn_dbias(dbias, "attn_dbias")[:, :N_REL]
            dz = jnp.concatenate([dq, dk, dv], axis=1)
            gw["c_w_qkv", i] = _mm_tn(sv["h1"], dz, D_MODEL, 768, False, (1, 2), "qkv_dw", comm.after())
            dx, dg = _mm_nt_normbwd(dz, w("c_w_qkv", i), sv["x0"], g_mix, dx, "qkv_bwd")
        gs["mix_norm"][layer] = dg[0]
        if layer > 0:
            tok = comm.send(gw)
    gsmall = {k: (jnp.stack(v) if isinstance(v, list) else v) for k, v in gs.items()}
    return loss, dx, gw, gsmall


BIG = ("w_ffn_in", "w_ffn_out", "ab_w_in", "ab_w_out", "c_w_qkv", "c_w_out")
SMALL = ("mix_norm", "ffn_norm", "ab_gn_gain", "ab_w_pool", "ab_pool_scale", "c_rel_bias", "final_norm")
N_PEERS = N_DEV - 1
FLIPS = [(fx, fy, fc) for fx in (0, 1) for fy in (0, 1) for fc in (0, 1)][1:]


def _peers():
    x, y, c = (lax.axis_index(a) for a in MESH_AXES)
    peers = []
    for fx, fy, fc in FLIPS:
        px, py, pc = (1 - x if fx else x), (1 - y if fy else y), (1 - c if fc else c)
        peers.append(((px, py, pc), 4 * px + 2 * py + pc))
    return 4 * x + 2 * y + c, peers


def _exchange(srcs, by_slot, name, collective_id):
    n = len(srcs)
    src_refs = [jax.new_ref(a, memory_space=pltpu.MemorySpace.HBM) for a in srcs]
    land_refs = [jax.empty_ref(SDS((N_DEV,) + (a.shape[1:] if slotted else a.shape), a.dtype),
                               memory_space=pltpu.MemorySpace.HBM) for a, slotted in zip(srcs, by_slot)]

    @pl.kernel(mesh=plsc.ScalarSubcoreMesh(axis_name="sequencer", num_cores=1), name=name,
               scratch_types=(pltpu.SemaphoreType.DMA((n * N_PEERS,)), pltpu.SemaphoreType.DMA((n * N_PEERS,)),
                              pltpu.SemaphoreType.DMA((n,))),
               compiler_params=pltpu.CompilerParams(collective_id=collective_id))
    def launch(send_sems, recv_sems, local_sems):
        me, peers = _peers()
        barrier = pltpu.get_barrier_semaphore()
        for pos, _ in peers:
            pl.semaphore_signal(barrier, inc=1, device_id=pos, device_id_type=pl.DeviceIdType.MESH)
        pl.semaphore_wait(barrier, N_PEERS)
        waits = []
        for k in range(n):
            own = pltpu.make_async_copy(src_refs[k].at[me] if by_slot[k] else src_refs[k], land_refs[k].at[me],
                                        local_sems.at[k])
            own.start()
            waits.append(own.wait)
            for rel, (pos, slot) in enumerate(peers):
                src = src_refs[k].at[slot] if by_slot[k] else src_refs[k]
                sems = dict(send_sem=send_sems.at[k * N_PEERS + rel], recv_sem=recv_sems.at[k * N_PEERS + rel],
                            device_id=pos, device_id_type=pl.DeviceIdType.MESH)
                send = pltpu.make_async_remote_copy(src_ref=src, dst_ref=land_refs[k].at[me], **sems)
                send.start()
                arrival = pltpu.make_async_remote_copy(src_ref=src, dst_ref=land_refs[k].at[slot], **sems)
                waits += [send.wait_send, arrival.wait_recv]
        for wait in waits:
            wait()

    launch()
    return [r[...] for r in land_refs]


def _cast_group(weights, keys, token, name):
    def body(*refs):
        n = len(keys)
        for i_ref, o_ref in zip(refs[:n], refs[n + 1:]):
            o_ref[...] = _bf(i_ref[...])

    def layer_spec(shape, l):
        return pl.BlockSpec((None,) + shape[1:], lambda i: (l, 0, 0))

    whole = lambda shape: pl.BlockSpec(shape, lambda i: (0, 0))
    ins = [weights[k] for k, _ in keys]
    return pl.pallas_call(
        body, name=name, grid=(1,), out_shape=[SDS(w.shape[1:], BF16) for w in ins],
        in_specs=[layer_spec(w.shape, l) for w, (_, l) in zip(ins, keys)] + [pl.BlockSpec(memory_space=pl.ANY)],
        out_specs=[whole(w.shape[1:]) for w in ins],
        compiler_params=_params(("arbitrary",)),
    )(*ins, token)


def _to_shard_major(g):
    nj, ka, nb = g.shape
    full = _split_pairs(jnp.transpose(g, (1, 0, 2)).reshape(ka, nj * nb), inverse=True)
    return jnp.transpose(full.reshape(ka, N_DEV, nj * nb // N_DEV), (1, 0, 2))


def _from_gathered(name, g):
    if name in ("w_ffn_out", "ab_w_out", "c_w_out"):
        return g.reshape(g.shape[0] * g.shape[1], g.shape[2])
    if name == "ab_w_in":
        return _split_pairs(jnp.transpose(g, (1, 0, 2)).reshape(1, g.shape[1], N_DEV * g.shape[2]))
    return g


class _Comm:
    def __init__(self, weights):
        self.weights_f32 = weights
        self.gathered = {}
        self.got = {}
        self.calls = 0
        self.ended = None
        self.opened = -1

    def _exchange(self, srcs, by_slot, name):
        self.calls += 1
        got = _exchange(srcs, by_slot, name, self.calls)
        self.ended = got[0][(0,) * got[0].ndim].astype(F32) * 0.0
        return got

    def _gather(self, group, at):
        keys = FWD_GROUPS[group]
        token = jnp.zeros((8, 128), F32) + (0.0 if self.ended is None else self.ended)
        for a in at if isinstance(at, tuple) else (at,):
            token = token + a[(0,) * a.ndim].astype(F32) * 0.0
        shards = _cast_group(self.weights_f32, keys, token, "cast_%d" % group)
        got = self._exchange(shards, [False] * len(keys), "gather_%d" % group)
        self.gathered.update((k, _from_gathered(k[0], arr)) for k, arr in zip(keys, got))

    def weight(self, name, layer, at=None):
        if (name, layer) not in self.gathered:
            self._gather(0, at[0] if isinstance(at, tuple) else at)
        group = next(g for g, keys in enumerate(FWD_GROUPS) if (name, layer) in keys)
        if group == self.opened + 1:
            self.opened = group
            if group + 1 < len(FWD_GROUPS):
                self._gather(group + 1, at)
        return self.gathered[name, layer]

    def after(self):
        return jnp.zeros((8, 128), F32) + self.ended

    def send(self, grads, shared=None):
        shared = shared or {}
        keys = list(grads) + list(shared)
        srcs = list(grads.values()) + list(shared.values())
        got = self._exchange(srcs, [True] * len(grads) + [False] * len(shared), "scatter_%d" % self.calls)
        self.got.update(zip(keys, got))
        return sum(g[0, 0, 0].astype(F32) * 0.0 for g in grads.values())

    def received(self):
        return self.got


def _adamw_math(g, w, m, v):
    m2 = ADAM_B1 * m + (1.0 - ADAM_B1) * g
    v2 = ADAM_B2 * v + (1.0 - ADAM_B2) * jnp.square(g)
    m_hat = m2 / (1.0 - ADAM_B1 ** ADAM_STEP)
    v_hat = v2 / (1.0 - ADAM_B2 ** ADAM_STEP)
    delta = -ADAM_LR * (m_hat / (jnp.sqrt(v_hat) + ADAM_EPS) + ADAM_WD * w)
    return delta, m2, v2


def _adamw(recv, w, m, v, name):
    nl, r, c = w.shape
    tr = _tile(r, 256)

    def body(*refs):
        g_refs = refs[:nl]
        w_ref, m_ref, v_ref, go_ref, d_ref, mo_ref, vo_ref = refs[nl:]
        for l in range(nl):
            @pl.when(pl.program_id(0) == l)
            def _():
                g = g_refs[l][0].astype(F32)
                for p in range(1, N_DEV):
                    g = g + g_refs[l][p].astype(F32)
                go_ref[...] = g
                d_ref[...], mo_ref[...], vo_ref[...] = _adamw_math(g, w_ref[...], m_ref[...], v_ref[...])

    def recv_spec(l):
        return pl.BlockSpec((N_DEV, tr, c), lambda layer, i: (0, jnp.where(layer == l, i, 0), 0))

    blk = pl.BlockSpec((None, tr, c), lambda l, i: (l, i, 0))
    return pl.pallas_call(
        body, name=name, grid=(nl, r // tr), out_shape=[SDS(w.shape, F32)] * 4,
        in_specs=[recv_spec(l) for l in range(nl)] + [blk, blk, blk],
        out_specs=[blk] * 4,
        compiler_params=_params(("arbitrary", "arbitrary")),
    )(*recv, w, m, v)


def _adamw_small(recv, loss_parts, w, m, v, name):
    n = len(w)

    def total(ref):
        t = ref[0]
        for p in range(1, N_DEV):
            t = t + ref[p]
        return t

    def body(*refs):
        g_refs, loss_ref = refs[:n], refs[n]
        w_refs, m_refs, v_refs = refs[n + 1:2 * n + 1], refs[2 * n + 1:3 * n + 1], refs[3 * n + 1:4 * n + 1]
        outs = refs[4 * n + 1:]
        for i in range(n):
            g = total(g_refs[i])
            outs[4 * i][...] = g
            outs[4 * i + 1][...], outs[4 * i + 2][...], outs[4 * i + 3][...] = _adamw_math(
                g, w_refs[i][...], m_refs[i][...], v_refs[i][...])
        outs[4 * n][...] = total(loss_ref)

    out_shape = [SDS(p.shape, F32) for p in w for _ in range(4)] + [SDS(loss_parts.shape[1:], F32)]
    outs = pl.pallas_call(body, name=name, out_shape=out_shape,
                          compiler_params=_params(None))(*recv, loss_parts, *w, *m, *v)
    return [outs[4 * i:4 * i + 4] for i in range(n)], outs[-1]


def kernel(x, mix_norm, ffn_norm, w_ffn_in, w_ffn_out, ab_w_in, ab_gn_gain, ab_w_pool, ab_pool_scale, ab_w_out, c_w_qkv, c_rel_bias, c_w_out, final_norm, loss_target, m_mix_norm, m_ffn_norm, m_w_ffn_in, m_w_ffn_out, m_ab_w_in, m_ab_gn_gain, m_ab_w_pool, m_ab_pool_scale, m_ab_w_out, m_c_w_qkv, m_c_rel_bias, m_c_w_out, m_final_norm, v_mix_norm, v_ffn_norm, v_w_ffn_in, v_w_ffn_out, v_ab_w_in, v_ab_gn_gain, v_ab_w_pool, v_ab_pool_scale, v_ab_w_out, v_c_w_qkv, v_c_rel_bias, v_c_w_out, v_final_norm):
    args = dict(locals())
    weights = {k: args[k] for k in BIG + SMALL}
    moments_m = {k: args["m_" + k] for k in BIG + SMALL}
    moments_v = {k: args["v_" + k] for k in BIG + SMALL}

    small = {k: weights[k] for k in SMALL}
    rows = lambda a: a.reshape(1, -1) if a.ndim == 1 else a

    comm = _Comm(weights)
    loss, dx, last_grads, gsmall = _local_step(x[0], loss_target[0], small, comm)
    comm.send(last_grads, {**{k: rows(gsmall[k]) for k in SMALL}, "loss": loss})
    recv = comm.received()

    outs = {}
    for k in BIG:
        layers = [recv[k, l] for l in range(weights[k].shape[0])]
        outs[k] = _adamw(layers, weights[k], moments_m[k], moments_v[k], "adamw_" + k)
    updated, total = _adamw_small([recv[k] for k in SMALL], recv["loss"], [rows(small[k]) for k in SMALL],
                                  [rows(moments_m[k]) for k in SMALL], [rows(moments_v[k]) for k in SMALL], "adamw_small")
    for k, parts in zip(SMALL, updated):
        outs[k] = [p.reshape(small[k].shape) for p in parts]

    order = SMALL[:2] + BIG[:2] + ("ab_w_in", "ab_gn_gain", "ab_w_pool", "ab_pool_scale", "ab_w_out",
                                   "c_w_qkv", "c_rel_bias", "c_w_out", "final_norm")
    result = [total[0, 0], dx[None]]
    for part in range(4):
        result += [outs[k][part] for k in order]
    return tuple(result)
```

```python
import functools

import jax
import jax.numpy as jnp
from jax import lax
from jax.experimental import pallas as pl
from jax.experimental.pallas import tpu as pltpu
from jax.experimental.pallas import tpu_sc as plsc

F32 = jnp.float32
BF16 = jnp.bfloat16
SDS = jax.ShapeDtypeStruct
MESH_AXES = ("x", "y", "c")
N_DEV = 8

D_MODEL = 1024
DEPTH = 4
CHUNK = 64
D_FF = 4 * D_MODEL
RMS_EPS = 1e-6
RET_WIDTH = 512
RET_HEADS = 4
RET_HEAD_DIM = 128
RET_ROPE_BASE = 10000.0
GN_EPS = 1e-5
POOL_WIDTH = 512
POOL_WINDOWS = (2, 4, 8, 16)
POOL_HALO = 16
AB_IN_WIDTH = 4 * RET_WIDTH + POOL_WIDTH
ATT_HEADS = 16
ATT_HEAD_DIM = 64
LEFT_CHUNKS = 8
REL_CLIP = 128
N_REL = 2 * REL_CLIP + 1
NEG_INF = -1e30

ADAM_LR = 0.001
ADAM_B1 = 0.9
ADAM_B2 = 0.999
ADAM_EPS = 1e-08
ADAM_WD = 0.01
ADAM_STEP = 10

TOKEN_TILE = 512
ATT_Q_TILE = 256
ATT_BACK = LEFT_CHUNKS * CHUNK // ATT_Q_TILE
ATT_K_TILE = (ATT_BACK + 1) * ATT_Q_TILE
ATT_PAIRS = 4
ATT_DIAG = 1024
REL_PAD = 384
VMEM_LIMIT_MB = 56

NT = (((1,), (1,)), ((), ()))
TN = (((0,), (0,)), ((), ()))


def _params(semantics, **kw):
    return pltpu.CompilerParams(dimension_semantics=semantics,
                                vmem_limit_bytes=VMEM_LIMIT_MB * 2 ** 20, **kw)


def _dot(a, b, dims=None):
    if dims is None:
        return jnp.dot(a, b, preferred_element_type=F32)
    return lax.dot_general(a, b, dims, preferred_element_type=F32)


def _bf(v):
    return v.astype(BF16)


def _tile(n, t):
    return min(n, t)


MXU_WIDTH = 256


def _mxu_group(nj, tn):
    return 2 if tn % MXU_WIDTH and (2 * tn) % MXU_WIDTH == 0 and nj % 2 == 0 else 1


def _w_tiles(w_ref, j, group):
    return w_ref[j] if group == 1 else jnp.concatenate([w_ref[j + t] for t in range(group)], axis=1)


def _w_cols(w_ref, j, group, c, width):
    return w_ref[j, :, c:c + width] if group == 1 else _w_tiles(w_ref, j, group)


def _norm_mm(x, gain, w, tn, z_dtype, relu2, name):
    s, d = x.shape
    nj = w.shape[0]
    tm = _tile(s, TOKEN_TILE)
    group = _mxu_group(nj, tn)

    def body(x_ref, g_ref, w_ref, h_ref, z_ref, *a_ref):
        xv = x_ref[...]
        r = lax.rsqrt(jnp.mean(xv * xv, axis=-1, keepdims=True) + RMS_EPS)
        h = _bf(xv * r * g_ref[...])
        h_ref[...] = h
        cw = tn if tn <= 512 else 512
        for j in range(0, nj, group):
            for c in range(0, tn, cw):
                z = _dot(h, _w_cols(w_ref, j, group, c, cw))
                cols = slice(j * tn + c, j * tn + c + group * cw)
                z_ref[:, cols] = z.astype(z_ref.dtype)
                if relu2:
                    a_ref[0][:, cols] = _bf(jnp.square(jnp.maximum(z, 0.0)))

    n = nj * tn
    out_shape = [SDS((s, d), BF16), SDS((s, n), z_dtype)]
    out_specs = [pl.BlockSpec((tm, d), lambda i: (i, 0)), pl.BlockSpec((tm, n), lambda i: (i, 0))]
    if relu2:
        out_shape.append(SDS((s, n), BF16))
        out_specs.append(pl.BlockSpec((tm, n), lambda i: (i, 0)))
    return pl.pallas_call(
        body, name=name, grid=(s // tm,), out_shape=out_shape,
        in_specs=[pl.BlockSpec((tm, d), lambda i: (i, 0)),
                  pl.BlockSpec((1, d), lambda i: (0, 0)),
                  pl.BlockSpec((nj, d, tn), lambda i: (0, 0, 0))],
        out_specs=out_specs,
        compiler_params=_params(("parallel",)),
    )(x, gain, w)


def _mm_res(parts, w, res, name):
    s, d = res.shape
    tm = _tile(s, TOKEN_TILE)
    widths = [p.shape[1] for p in parts]

    def body(*refs):
        a_refs = refs[:len(parts)]
        w_ref, res_ref, o_ref = refs[len(parts):]
        acc = res_ref[...]
        off = 0
        for a_ref, k in zip(a_refs, widths):
            acc = acc + _dot(a_ref[...], w_ref[off:off + k, :])
            off += k
        o_ref[...] = acc

    return pl.pallas_call(
        body, name=name, grid=(s // tm,), out_shape=SDS((s, d), F32),
        in_specs=[pl.BlockSpec((tm, k), lambda i: (i, 0)) for k in widths]
        + [pl.BlockSpec(w.shape, lambda i: (0, 0)), pl.BlockSpec((tm, d), lambda i: (i, 0))],
        out_specs=pl.BlockSpec((tm, d), lambda i: (i, 0)),
        compiler_params=_params(("parallel",)),
    )(*parts, w, res)


def _mm_nt_rows(dy, w, z, name):
    s, d = dy.shape
    k = w.shape[0]
    tm = _tile(s, TOKEN_TILE)
    tk = _tile(k, 1024)

    def body(dy_ref, w_ref, *rest):
        o_ref = rest[-1]
        dyb = _bf(dy_ref[...])
        for j in range(k // tk):
            cols = slice(j * tk, (j + 1) * tk)
            da = _dot(dyb, w_ref[cols, :], NT)
            if z is not None:
                da = da * (2.0 * jnp.maximum(rest[0][:, cols].astype(F32), 0.0))
            o_ref[:, cols] = _bf(da)

    in_specs = [pl.BlockSpec((tm, d), lambda i: (i, 0)), pl.BlockSpec((k, d), lambda i: (0, 0))]
    args = [dy, w]
    if z is not None:
        in_specs.append(pl.BlockSpec((tm, k), lambda i: (i, 0)))
        args.append(z)
    return pl.pallas_call(
        body, name=name, grid=(s // tm,), out_shape=SDS((s, k), BF16),
        in_specs=in_specs, out_specs=pl.BlockSpec((tm, k), lambda i: (i, 0)),
        compiler_params=_params(("parallel",)),
    )(*args)


def _w_range(w_ref, c0, c1):
    nc = w_ref.shape[2]
    pieces, c = [], c0
    while c < c1:
        j = c // nc
        hi = min(nc, c1 - j * nc)
        pieces.append(w_ref[j, :, c - j * nc:hi])
        c = j * nc + hi
    return pieces[0] if len(pieces) == 1 else jnp.concatenate(pieces, axis=1)


def _mm_nt_normbwd(dz, w, x, gain, dres, name):
    parts = list(dz) if isinstance(dz, (list, tuple)) else [dz]
    widths = [p.shape[1] for p in parts]
    s, d = x.shape
    tm = _tile(s, TOKEN_TILE)
    chunk = 2 * MXU_WIDTH
    halves = 2 if tm % 32 == 0 else 1

    def body(*refs):
        dz_refs = refs[:len(parts)]
        w_ref, x_ref, g_ref, dres_ref, dx_ref, dg_ref = refs[len(parts):]

        @pl.when(pl.program_id(0) == 0)
        def _():
            dg_ref[...] = jnp.zeros_like(dg_ref)

        for half in range(halves):
            rows = slice(half * tm // halves, (half + 1) * tm // halves)
            dh, base = None, 0
            for dz_ref, width in zip(dz_refs, widths):
                for c in range(0, width, chunk):
                    term = _dot(dz_ref[rows, c:c + chunk], _w_range(w_ref, base + c, base + c + chunk), NT)
                    dh = term if dh is None else dh + term
                base += width
            xv = x_ref[rows, :]
            r = lax.rsqrt(jnp.mean(xv * xv, axis=-1, keepdims=True) + RMS_EPS)
            xn = xv * r
            dg_ref[...] += jnp.sum(dh * xn, axis=0, keepdims=True)
            dxh = dh * g_ref[...]
            dx_ref[rows, :] = dres_ref[rows, :] + r * (dxh - xn * jnp.mean(dxh * xn, axis=-1, keepdims=True))

    return pl.pallas_call(
        body, name=name, grid=(s // tm,), out_shape=[SDS((s, d), F32), SDS((1, d), F32)],
        in_specs=[pl.BlockSpec((tm, width), lambda i: (i, 0)) for width in widths]
        + [pl.BlockSpec(w.shape, lambda i: (0, 0, 0)),
           pl.BlockSpec((tm, d), lambda i: (i, 0)),
           pl.BlockSpec((1, d), lambda i: (0, 0)),
           pl.BlockSpec((tm, d), lambda i: (i, 0))],
        out_specs=[pl.BlockSpec((tm, d), lambda i: (i, 0)), pl.BlockSpec((1, d), lambda i: (0, 0))],
        compiler_params=_params(("arbitrary",)),
    )(*parts, w, x, gain, dres)


def _mm_tn(a, b, ka, nb, a_tiled, split, name, after=None):
    a_parts = list(a) if isinstance(a, (list, tuple)) else [a]
    s = a_parts[0].shape[0]
    tm = _tile(s, 4 * TOKEN_TILE)
    nm = s // tm
    nj = a_parts[0].shape[1] // ka if a_tiled and len(a_parts) == 1 else (1 if a_tiled else b.shape[1] // nb)
    axis, parts = split
    pr, pc = (ka // parts, nb) if axis == 0 else (ka, nb // parts)

    def body(*refs):
        a_refs, b_ref = refs[:len(a_parts)], refs[len(a_parts)]
        o_ref, acc = refs[-2:]
        m = pl.program_id(1)

        @pl.when(m == 0)
        def _():
            acc[...] = jnp.zeros_like(acc)

        av = a_refs[0][...] if len(a_refs) == 1 else jnp.concatenate([r[...] for r in a_refs], axis=1)
        acc[...] += _dot(_bf(av), _bf(b_ref[...]), TN)

        @pl.when(m == nm - 1)
        def _():
            for q in range(parts):
                piece = acc[q * pr:(q + 1) * pr, :] if axis == 0 else acc[:, q * pc:(q + 1) * pc]
                o_ref[q] = piece.astype(o_ref.dtype)

    return pl.pallas_call(
        body, name=name, grid=(nj, nm), out_shape=SDS((nj * parts, pr, pc), BF16),
        in_specs=([pl.BlockSpec((tm, ka), (lambda j, m: (m, j)) if a_tiled else (lambda j, m: (m, 0)))]
                  if len(a_parts) == 1 else [pl.BlockSpec((tm, p.shape[1]), lambda j, m: (m, 0)) for p in a_parts])
        + [pl.BlockSpec((tm, nb), (lambda j, m: (m, 0)) if a_tiled else (lambda j, m: (m, j)))]
        + [pl.BlockSpec(memory_space=pl.ANY)] * (after is not None),
        out_specs=pl.BlockSpec((parts, pr, pc), lambda j, m: (j, 0, 0)),
        scratch_shapes=[pltpu.VMEM((ka, nb), F32)],
        compiler_params=_params(("parallel", "arbitrary")),
    )(*a_parts, b, *([] if after is None else [after]))


def _final_loss(x, gain, target, name):
    s, d = x.shape
    tm = _tile(s, TOKEN_TILE)

    def body(x_ref, g_ref, t_ref, loss_ref, dx_ref, dg_ref):
        @pl.when(pl.program_id(0) == 0)
        def _():
            loss_ref[...] = jnp.zeros_like(loss_ref)
            dg_ref[...] = jnp.zeros_like(dg_ref)

        xv = x_ref[...]
        r = lax.rsqrt(jnp.mean(xv * xv, axis=-1, keepdims=True) + RMS_EPS)
        xn = xv * r
        err = xn * g_ref[...] - t_ref[...]
        loss_ref[...] += (0.5 / d) * jnp.sum(err * err)
        dy = err * (1.0 / d)
        dg_ref[...] += jnp.sum(dy * xn, axis=0, keepdims=True)
        dxh = dy * g_ref[...]
        dx_ref[...] = r * (dxh - xn * jnp.mean(dxh * xn, axis=-1, keepdims=True))

    return pl.pallas_call(
        body, name=name, grid=(s // tm,),
        out_shape=[SDS((8, 128), F32), SDS((s, d), F32), SDS((1, d), F32)],
        in_specs=[pl.BlockSpec((tm, d), lambda i: (i, 0)), pl.BlockSpec((1, d), lambda i: (0, 0)),
                  pl.BlockSpec((tm, d), lambda i: (i, 0))],
        out_specs=[pl.BlockSpec((8, 128), lambda i: (0, 0)), pl.BlockSpec((tm, d), lambda i: (i, 0)),
                   pl.BlockSpec((1, d), lambda i: (0, 0))],
        compiler_params=_params(("arbitrary",)),
    )(x, gain, target)


def _retention_tables(s):
    half = RET_HEAD_DIM // 2
    inv_freq = 1.0 / (RET_ROPE_BASE ** jnp.linspace(0.0, 1.0, half, dtype=F32))
    ang = jnp.arange(s, dtype=F32)[:, None] * inv_freq[None, :]
    cos, sin = jnp.cos(ang), jnp.sin(ang)
    cos_e = jnp.concatenate([cos, cos], axis=-1)
    sin_s = jnp.concatenate([-sin, sin], axis=-1)
    log_g = jnp.log1p(-jnp.power(2.0, -5.0 - jnp.arange(RET_HEADS, dtype=F32)))
    pos = jnp.arange(CHUNK, dtype=F32)
    dmat = jnp.exp(jnp.abs(pos[:, None] - pos[None, :])[None] * log_g[:, None, None])
    qdec = jnp.exp((pos[None, :] + 1.0) * log_g[:, None])
    kdec = jnp.exp((CHUNK - 1.0 - pos[None, :]) * log_g[:, None])
    lam = jnp.exp(CHUNK * log_g)
    wide = (RET_HEADS, CHUNK, RET_HEAD_DIM)
    return dict(cos=cos_e, sin=sin_s, dmat=dmat,
                qdec=jnp.broadcast_to(qdec[:, :, None], wide),
                kdec=jnp.broadcast_to(kdec[:, :, None], wide),
                lam=jnp.broadcast_to(lam[:, None, None], (RET_HEADS, RET_HEAD_DIM, RET_HEAD_DIM)))


def _swap_pairs(t):
    return pltpu.roll(t, RET_HEAD_DIM // 2, 1)


def _split_pairs(w, inverse=False):
    lead, nqk = w.shape[:-1], 2 * RET_WIDTH
    shape = (2 * RET_HEADS, 2, RET_HEAD_DIM // 2) if inverse else (2 * RET_HEADS, RET_HEAD_DIM // 2, 2)
    qk = jnp.swapaxes(w[..., :nqk].reshape(lead + shape), -1, -2).reshape(lead + (nqk,))
    return jnp.concatenate([qk, w[..., nqk:]], axis=-1)


def _head(h):
    return slice(h * RET_HEAD_DIM, (h + 1) * RET_HEAD_DIM)


def _ret_common_specs(tb, blk):
    zs = [pl.BlockSpec((tb, RET_WIDTH), functools.partial(lambda j, i: (blk(i), j), j)) for j in range(4)]
    tabs = [pl.BlockSpec((tb, RET_HEAD_DIM), lambda i: (blk(i), 0))] * 2
    consts = [pl.BlockSpec((1, RET_WIDTH), lambda i: (0, 0)),
              pl.BlockSpec((RET_HEADS, CHUNK, CHUNK), lambda i: (0, 0, 0)),
              pl.BlockSpec((RET_HEADS, CHUNK, RET_HEAD_DIM), lambda i: (0, 0, 0)),
              pl.BlockSpec((RET_HEADS, CHUNK, RET_HEAD_DIM), lambda i: (0, 0, 0)),
              pl.BlockSpec((RET_HEADS, RET_HEAD_DIM, RET_HEAD_DIM), lambda i: (0, 0, 0))]
    return zs + tabs + consts


def _ret_fwd(z, tabs, gn_gain, name):
    s = z.shape[0]
    tb = _tile(s, TOKEN_TILE)
    ncb = tb // CHUNK
    scale = RET_HEAD_DIM ** -0.5

    def body(q_ref, k_ref, v_ref, g_ref, cos_ref, sin_ref, gain_ref, dm_ref, qd_ref, kd_ref, lam_ref,
             o_ref, st_ref, ret_ref, s_scr, qr_scr, kr_scr):
        @pl.when(pl.program_id(0) == 0)
        def _():
            s_scr[...] = jnp.zeros_like(s_scr)

        cosv, sinv = cos_ref[...], sin_ref[...]
        for h in range(RET_HEADS):
            qh, kh = q_ref[:, _head(h)], k_ref[:, _head(h)]
            qr_scr[:, _head(h)] = qh * cosv + _swap_pairs(qh) * sinv
            kr_scr[:, _head(h)] = (kh * cosv + _swap_pairs(kh) * sinv) * scale

        def chunk(c, carry):
            rows = pl.ds(pl.multiple_of(c * CHUNK, CHUNK), CHUNK)
            for h in range(RET_HEADS):
                qc, kc, vc = qr_scr[rows, _head(h)], kr_scr[rows, _head(h)], v_ref[rows, _head(h)]
                a = _dot(_bf(qc), _bf(kc), NT) * dm_ref[h]
                st = s_scr[h]
                st_ref[c, h] = st
                o_ref[rows, _head(h)] = _dot(_bf(a), _bf(vc)) + _dot(_bf(qc * qd_ref[h]), _bf(st))
                s_scr[h] = st * lam_ref[h] + _dot(_bf(kc * kd_ref[h]), _bf(vc), TN)
            return carry

        lax.fori_loop(0, ncb, chunk, 0, unroll=4)
        for h in range(RET_HEADS):
            o = o_ref[:, _head(h)]
            mu = jnp.mean(o, axis=-1, keepdims=True)
            oc = o - mu
            y = oc * lax.rsqrt(jnp.mean(oc * oc, axis=-1, keepdims=True) + GN_EPS) * gain_ref[:, _head(h)]
            g = g_ref[:, _head(h)]
            ret_ref[:, _head(h)] = _bf(g / (1.0 + jnp.exp(-g)) * y)

    nc = s // CHUNK
    return pl.pallas_call(
        body, name=name, grid=(s // tb,),
        out_shape=[SDS((s, RET_WIDTH), F32), SDS((nc, RET_HEADS, RET_HEAD_DIM, RET_HEAD_DIM), F32),
                   SDS((s, RET_WIDTH), BF16)],
        in_specs=_ret_common_specs(tb, lambda i: i),
        out_specs=[pl.BlockSpec((tb, RET_WIDTH), lambda i: (i, 0)),
                   pl.BlockSpec((ncb, RET_HEADS, RET_HEAD_DIM, RET_HEAD_DIM), lambda i: (i, 0, 0, 0)),
                   pl.BlockSpec((tb, RET_WIDTH), lambda i: (i, 0))],
        scratch_shapes=[pltpu.VMEM((RET_HEADS, RET_HEAD_DIM, RET_HEAD_DIM), F32),
                        pltpu.VMEM((tb, RET_WIDTH), F32), pltpu.VMEM((tb, RET_WIDTH), F32)],
        compiler_params=_params(("arbitrary",)),
    )(z, z, z, z, tabs["cos"], tabs["sin"], gn_gain, tabs["dmat"], tabs["qdec"], tabs["kdec"], tabs["lam"])


def _ret_bwd(z, tabs, gn_gain, o_pre, states, du, name):
    s = z.shape[0]
    tb = _tile(s, TOKEN_TILE)
    ncb = tb // CHUNK
    nblk = s // tb
    scale = RET_HEAD_DIM ** -0.5
    rev = lambda i: nblk - 1 - i

    def body(q_ref, k_ref, v_ref, g_ref, cos_ref, sin_ref, gain_ref, dm_ref, qd_ref, kd_ref, lam_ref,
             o_ref, st_ref, dret_ref, dz_ref, dgain_ref, g_scr, qr_scr, kr_scr, do_scr, dq_scr, dk_scr):
        @pl.when(pl.program_id(0) == 0)
        def _():
            g_scr[...] = jnp.zeros_like(g_scr)
            dgain_ref[...] = jnp.zeros_like(dgain_ref)

        cosv, sinv = cos_ref[...], sin_ref[...]
        for h in range(RET_HEADS):
            hs = _head(h)
            qh, kh = q_ref[:, hs], k_ref[:, hs]
            qr_scr[:, hs] = qh * cosv + _swap_pairs(qh) * sinv
            kr_scr[:, hs] = (kh * cosv + _swap_pairs(kh) * sinv) * scale
            o = o_ref[:, hs]
            mu = jnp.mean(o, axis=-1, keepdims=True)
            oc = o - mu
            rstd = lax.rsqrt(jnp.mean(oc * oc, axis=-1, keepdims=True) + GN_EPS)
            yh = oc * rstd
            gain = gain_ref[:, hs]
            g = g_ref[:, hs]
            sg = 1.0 / (1.0 + jnp.exp(-g))
            dret = dret_ref[:, hs].astype(F32)
            dy = dret * (g * sg)
            dz_ref[:, 3 * RET_WIDTH + h * RET_HEAD_DIM:3 * RET_WIDTH + (h + 1) * RET_HEAD_DIM] = _bf(
                dret * (yh * gain) * (sg * (1.0 + g * (1.0 - sg))))
            dgain_ref[:, hs] += jnp.sum(dy * yh, axis=0, keepdims=True)
            dyh = dy * gain
            do_scr[:, hs] = rstd * (dyh - jnp.mean(dyh, axis=-1, keepdims=True)
                                    - yh * jnp.mean(dyh * yh, axis=-1, keepdims=True))

        def chunk(cc, carry):
            c = ncb - 1 - cc
            rows = pl.ds(pl.multiple_of(c * CHUNK, CHUNK), CHUNK)
            for h in range(RET_HEADS):
                hs = _head(h)
                qc, kc, vc, doc = _bf(qr_scr[rows, hs]), _bf(kr_scr[rows, hs]), _bf(v_ref[rows, hs]), _bf(do_scr[rows, hs])
                qdc, kdc = qd_ref[h], kd_ref[h]
                st, gs = _bf(st_ref[c, h]), g_scr[h]
                gsb = _bf(gs)
                dm = dm_ref[h]
                p = _bf(_dot(qc, kc, NT) * dm)
                da = _bf(_dot(doc, vc, NT) * dm)
                kt = _bf(kr_scr[rows, hs] * kdc)
                qt = _bf(qr_scr[rows, hs] * qdc)
                dz_ref[rows, 2 * RET_WIDTH + h * RET_HEAD_DIM:2 * RET_WIDTH + (h + 1) * RET_HEAD_DIM] = _bf(
                    _dot(p, doc, TN) + _dot(kt, gsb))
                dq_scr[rows, hs] = _dot(da, kc) + _dot(doc, st, NT) * qdc
                dk_scr[rows, hs] = _dot(da, qc, TN) + _dot(vc, gsb, NT) * kdc
                g_scr[h] = gs * lam_ref[h] + _dot(qt, doc, TN)
            return carry

        lax.fori_loop(0, ncb, chunk, 0, unroll=4)
        for h in range(RET_HEADS):
            hs = _head(h)
            dq, dk = dq_scr[:, hs], dk_scr[:, hs]
            dz_ref[:, h * RET_HEAD_DIM:(h + 1) * RET_HEAD_DIM] = _bf(dq * cosv - _swap_pairs(dq) * sinv)
            dz_ref[:, RET_WIDTH + h * RET_HEAD_DIM:RET_WIDTH + (h + 1) * RET_HEAD_DIM] = _bf(
                (dk * cosv - _swap_pairs(dk) * sinv) * scale)

    return pl.pallas_call(
        body, name=name, grid=(nblk,),
        out_shape=[SDS((s, AB_IN_WIDTH), BF16), SDS((1, RET_WIDTH), F32)],
        in_specs=_ret_common_specs(tb, rev)
        + [pl.BlockSpec((tb, RET_WIDTH), lambda i: (rev(i), 0)),
           pl.BlockSpec((ncb, RET_HEADS, RET_HEAD_DIM, RET_HEAD_DIM), lambda i: (rev(i), 0, 0, 0)),
           pl.BlockSpec((tb, RET_WIDTH), lambda i: (rev(i), 0))],
        out_specs=[pl.BlockSpec((tb, 4 * RET_WIDTH), lambda i: (rev(i), 0)),
                   pl.BlockSpec((1, RET_WIDTH), lambda i: (0, 0))],
        scratch_shapes=[pltpu.VMEM((RET_HEADS, RET_HEAD_DIM, RET_HEAD_DIM), F32)]
        + [pltpu.VMEM((tb, RET_WIDTH), F32)] * 5,
        compiler_params=_params(("arbitrary",)),
    )(z, z, z, z, tabs["cos"], tabs["sin"], gn_gain, tabs["dmat"], tabs["qdec"], tabs["kdec"], tabs["lam"],
      o_pre, states, du)


POOL_COL = 4 * RET_WIDTH // POOL_WIDTH


def _pooled(cur, prev, t0):
    tm = cur.shape[0]
    xx = jnp.concatenate([prev, cur], axis=0)
    sums = {1: xx}
    w = 1
    while w < POOL_WINDOWS[-1]:
        sums[2 * w] = sums[w] + pltpu.roll(sums[w], w, 0)
        w *= 2
    t = t0 + lax.broadcasted_iota(jnp.int32, (tm, 128), 0)
    outs = []
    for gi, w in enumerate(POOL_WINDOWS):
        cols = slice(gi * 128, (gi + 1) * 128)
        cnt = jnp.minimum(t + 1, w).astype(F32)
        outs.append(sums[w][POOL_HALO:, cols] / cnt - cur[:, cols])
    return outs


def _pool_fwd(z, w_pool, scale, name):
    s = z.shape[0]
    tm = _tile(s, TOKEN_TILE)
    hb = tm // POOL_HALO

    def body(p_ref, prev_ref, w_ref, sc_ref, o_ref):
        i = pl.program_id(0)
        prev = jnp.where(i > 0, prev_ref[...], 0.0)
        pooled = _pooled(p_ref[...], prev, i * tm)
        for gi in range(len(POOL_WINDOWS)):
            cols = slice(gi * 128, (gi + 1) * 128)
            o_ref[:, cols] = _bf(_dot(_bf(pooled[gi]), _bf(w_ref[gi])) * sc_ref[:, cols])

    return pl.pallas_call(
        body, name=name, grid=(s // tm,), out_shape=SDS((s, POOL_WIDTH), BF16),
        in_specs=[pl.BlockSpec((tm, POOL_WIDTH), lambda i: (i, POOL_COL)),
                  pl.BlockSpec((POOL_HALO, POOL_WIDTH), lambda i: (jnp.maximum(i * hb - 1, 0), POOL_COL)),
                  pl.BlockSpec(w_pool.shape, lambda i: (0, 0, 0)),
                  pl.BlockSpec((1, POOL_WIDTH), lambda i: (0, 0))],
        out_specs=pl.BlockSpec((tm, POOL_WIDTH), lambda i: (i, 0)),
        compiler_params=_params(("parallel",)),
    )(z, z, w_pool, scale)


def _pool_bwd(z, w_pool, scale, du, dz, name):
    s = z.shape[0]
    tm = _tile(s, TOKEN_TILE)
    hb = tm // POOL_HALO
    nblk = s // tm
    last_halo = s // POOL_HALO - 1

    def body(p_ref, prev_ref, w_ref, sc_ref, do_ref, don_ref, dz_ref, dp_ref, dw_ref, dsc_ref):
        i = pl.program_id(0)

        @pl.when(i == 0)
        def _():
            dw_ref[...] = jnp.zeros_like(dw_ref)
            dsc_ref[...] = jnp.zeros_like(dsc_ref)

        prev = jnp.where(i > 0, prev_ref[...], 0.0)
        pooled = _pooled(p_ref[...], prev, i * tm)
        dout = do_ref[...].astype(F32)
        dout_next = jnp.where(i < nblk - 1, don_ref[...].astype(F32), 0.0)
        sc = sc_ref[...]
        dmix = jnp.concatenate([dout * sc, dout_next * sc], axis=0)
        n = tm + POOL_HALO
        t = i * tm + lax.broadcasted_iota(jnp.int32, (n, 128), 0)
        for gi, w in enumerate(POOL_WINDOWS):
            cols = slice(gi * 128, (gi + 1) * 128)
            wg = _bf(w_ref[gi])
            pg = _bf(pooled[gi])
            dsc_ref[:, cols] += jnp.sum(dout[:, cols] * _dot(pg, wg), axis=0, keepdims=True)
            dw_ref[gi] += _dot(pg, _bf(dmix[:tm, cols]), TN)
            dpool = _dot(_bf(dmix[:, cols]), wg, NT)
            acc = dpool / jnp.minimum(t + 1, w).astype(F32)
            step = 1
            while step < w:
                acc = acc + pltpu.roll(acc, n - step, 0)
                step *= 2
            dp_ref[:, cols] = _bf(acc[:tm] - dpool[:tm])

    return pl.pallas_call(
        body, name=name, grid=(nblk,),
        out_shape=[SDS(dz.shape, BF16), SDS(w_pool.shape, F32), SDS((1, POOL_WIDTH), F32)],
        in_specs=[pl.BlockSpec((tm, POOL_WIDTH), lambda i: (i, POOL_COL)),
                  pl.BlockSpec((POOL_HALO, POOL_WIDTH), lambda i: (jnp.maximum(i * hb - 1, 0), POOL_COL)),
                  pl.BlockSpec(w_pool.shape, lambda i: (0, 0, 0)),
                  pl.BlockSpec((1, POOL_WIDTH), lambda i: (0, 0)),
                  pl.BlockSpec((tm, POOL_WIDTH), lambda i: (i, 1)),
                  pl.BlockSpec((POOL_HALO, POOL_WIDTH), lambda i: (jnp.minimum((i + 1) * hb, last_halo), 1)),
                  pl.BlockSpec(memory_space=pl.ANY)],
        out_specs=[pl.BlockSpec((tm, POOL_WIDTH), lambda i: (i, POOL_COL)),
                   pl.BlockSpec(w_pool.shape, lambda i: (0, 0, 0)),
                   pl.BlockSpec((1, POOL_WIDTH), lambda i: (0, 0))],
        input_output_aliases={6: 0},
        compiler_params=_params(("arbitrary",)),
    )(z, z, w_pool, scale, du, du, dz)


def _rel_onehot():
    r = lax.broadcasted_iota(jnp.int32, (REL_PAD, ATT_DIAG), 0)
    c = lax.broadcasted_iota(jnp.int32, (REL_PAD, ATT_DIAG), 1)
    rel = jnp.where(c < ATT_K_TILE, jnp.clip(LEFT_CHUNKS * CHUNK - c, -REL_CLIP, REL_CLIP) + REL_CLIP,
                    2 * REL_CLIP)
    return (rel == r).astype(BF16)


def _split3(v):
    hi = _bf(v)
    r1 = v - hi.astype(F32)
    mid = _bf(r1)
    return hi, mid, _bf(r1 - mid.astype(F32))


def _skew(v, sign):
    row = lax.broadcasted_iota(jnp.int32, v.shape, 0)
    bit = 1
    while bit < ATT_Q_TILE:
        shift = bit if sign > 0 else ATT_DIAG - bit
        v = jnp.where((row & bit) != 0, pltpu.roll(v, shift, 1), v)
        bit *= 2
    return v


def _attn_bias(rel_bias, name):
    def body(t_ref, o_ref):
        oh = _rel_onehot()
        base = sum(_dot(part, oh) for part in _split3(t_ref[0]))
        full = _skew(jnp.broadcast_to(base[0:1], (ATT_Q_TILE, ATT_DIAG)), +1)[:, :ATT_K_TILE]
        qc = lax.broadcasted_iota(jnp.int32, full.shape, 0) // CHUNK
        kc = lax.broadcasted_iota(jnp.int32, full.shape, 1) // CHUNK
        o_ref[0] = jnp.where((kc >= qc) & (kc <= qc + LEFT_CHUNKS), full, NEG_INF)

    t8 = jnp.broadcast_to(rel_bias[:, None, :], (ATT_HEADS, 8, REL_PAD))
    return pl.pallas_call(
        body, name=name, grid=(ATT_HEADS,), out_shape=SDS((ATT_HEADS, ATT_Q_TILE, ATT_K_TILE), F32),
        in_specs=[pl.BlockSpec((1, 8, REL_PAD), lambda h: (h, 0, 0))],
        out_specs=pl.BlockSpec((1, ATT_Q_TILE, ATT_K_TILE), lambda h: (h, 0, 0)),
        compiler_params=_params(("parallel",)),
    )(t8)


def _attn_dbias(dbias, name):
    def body(d_ref, o_ref):
        pad = jnp.zeros((ATT_Q_TILE, ATT_DIAG - ATT_K_TILE), F32)
        diag = _skew(jnp.concatenate([d_ref[0], pad], axis=1), -1)
        col = jnp.sum(diag, axis=0, keepdims=True)
        oh = _rel_onehot()
        col8 = jnp.broadcast_to(col, (8, ATT_DIAG))
        o_ref[0] = sum(_dot(part, oh, NT) for part in _split3(col8))

    out = pl.pallas_call(
        body, name=name, grid=(ATT_HEADS,), out_shape=SDS((ATT_HEADS, 8, REL_PAD), F32),
        in_specs=[pl.BlockSpec((1, ATT_Q_TILE, ATT_K_TILE), lambda h: (h, 0, 0))],
        out_specs=pl.BlockSpec((1, 8, REL_PAD), lambda h: (h, 0, 0)),
        compiler_params=_params(("parallel",)),
    )(dbias)
    return out[:, 0, :]


ATT_WIDTH = 128 * ATT_PAIRS
ATT_GROUPS = D_MODEL // ATT_WIDTH


def _attn_specs(nq):
    def tile(off, back):
        return pl.BlockSpec((ATT_Q_TILE, ATT_WIDTH),
                            lambda g, i: (jnp.maximum(jnp.minimum(i, nq - 1) - back, 0), off + g))

    backs = [ATT_BACK - b for b in range(ATT_BACK + 1)]
    return ([tile(0, 0)] + [tile(ATT_GROUPS, b) for b in backs] + [tile(2 * ATT_GROUPS, b) for b in backs]
            + [pl.BlockSpec((2 * ATT_PAIRS, ATT_Q_TILE, ATT_K_TILE), lambda g, i: (g, 0, 0))])


def _attn_weights(qh, k2, bias, i, masked):
    sc = _dot(qh, k2, NT) + bias
    if masked:
        kpos = (i - ATT_BACK) * ATT_Q_TILE + lax.broadcasted_iota(jnp.int32, sc.shape, 1)
        sc = jnp.where(kpos >= 0, sc, NEG_INF)
    e = jnp.exp(sc - jnp.max(sc, axis=-1, keepdims=True))
    return e, 1.0 / jnp.sum(e, axis=-1, keepdims=True)


def _first_head():
    return lax.broadcasted_iota(jnp.int32, (ATT_Q_TILE, 128), 1) < ATT_HEAD_DIM


def _pair_operands(q_ref, k_refs, v_refs, pp):
    cols = slice(pp * 128, (pp + 1) * 128)
    q2 = q_ref[:, cols] * ATT_HEAD_DIM ** -0.5
    k2 = jnp.concatenate([r[:, cols] for r in k_refs], axis=0)
    v2 = jnp.concatenate([r[:, cols] for r in v_refs], axis=0)
    return cols, q2, k2, v2


def _attn_fwd(z, bias, name):
    s = z.shape[0]
    nq = s // ATT_Q_TILE
    nt = ATT_BACK + 1

    def body(q_ref, *rest):
        k_refs, v_refs, (b_ref, o_ref) = rest[:nt], rest[nt:2 * nt], rest[2 * nt:]
        i = pl.program_id(1)
        first = _first_head()

        def compute(masked):
            for pp in range(ATT_PAIRS):
                cols, q2, k2, v2 = _pair_operands(q_ref, k_refs, v_refs, pp)
                outs = []
                for hh in range(2):
                    qh = jnp.where(first if hh == 0 else ~first, q2, 0)
                    e, inv = _attn_weights(qh, k2, b_ref[2 * pp + hh], i, masked)
                    outs.append(_dot(_bf(e), v2) * inv)
                o_ref[:, cols] = _bf(jnp.where(first, outs[0], outs[1]))

        pl.when(i < ATT_BACK)(lambda: compute(True))
        pl.when(i >= ATT_BACK)(lambda: compute(False))

    return pl.pallas_call(
        body, name=name, grid=(ATT_GROUPS, nq), out_shape=SDS((s, D_MODEL), BF16),
        in_specs=_attn_specs(nq),
        out_specs=pl.BlockSpec((ATT_Q_TILE, ATT_WIDTH), lambda g, i: (i, g)),
        compiler_params=_params(("parallel", "parallel")),
    )(*([z] * (1 + 2 * nt)), bias)


def _attn_bwd(z, bias, o, do, name):
    s = z.shape[0]
    nq = s // ATT_Q_TILE
    nt = ATT_BACK + 1

    def body(q_ref, *rest):
        k_refs, v_refs = rest[:nt], rest[nt:2 * nt]
        b_ref, o_ref, do_ref, dq_ref, dk_ref, dv_ref, db_ref, dk_acc, dv_acc = rest[2 * nt:]
        i = pl.program_id(1)
        first = _first_head()

        @pl.when(i == 0)
        def _():
            db_ref[...] = jnp.zeros_like(db_ref)
            dk_acc[...] = jnp.zeros_like(dk_acc)
            dv_acc[...] = jnp.zeros_like(dv_acc)

        def compute(masked):
            for pp in range(ATT_PAIRS):
                cols, q2, k2, v2 = _pair_operands(q_ref, k_refs, v_refs, pp)
                do2 = do_ref[:, cols].astype(F32)
                prod = do2 * o_ref[:, cols].astype(F32)
                dqs, dk, dv = [], None, None
                for hh in range(2):
                    mine = first if hh == 0 else ~first
                    qh = jnp.where(mine, q2, 0)
                    e, inv = _attn_weights(qh, k2, b_ref[2 * pp + hh], i, masked)
                    delta = jnp.sum(jnp.where(mine, prod, 0.0), axis=-1, keepdims=True) * inv
                    doh = _bf(jnp.where(mine, do2 * inv, 0.0))
                    ds = e * (_dot(doh, v2, NT) - delta)
                    db_ref[2 * pp + hh] += ds
                    dsb = _bf(ds)
                    dqs.append(_dot(dsb, k2))
                    dkh, dvh = _dot(dsb, qh, TN), _dot(_bf(e), doh, TN)
                    dk, dv = (dkh, dvh) if hh == 0 else (dk + dkh, dv + dvh)
                dq_ref[:, cols] = _bf(jnp.where(first, dqs[0], dqs[1]) * ATT_HEAD_DIM ** -0.5)
                for b in range(nt):
                    slot = (i + b + 1) % nt
                    rows = slice(b * ATT_Q_TILE, (b + 1) * ATT_Q_TILE)
                    if b < ATT_BACK:
                        dk_acc[slot, :, cols] += dk[rows]
                        dv_acc[slot, :, cols] += dv[rows]
                    else:
                        dk_acc[slot, :, cols] = dk[rows]
                        dv_acc[slot, :, cols] = dv[rows]

        pl.when(i < ATT_BACK)(lambda: compute(True))
        pl.when((i >= ATT_BACK) & (i < nq))(lambda: compute(False))
        done = (i + 1) % nt
        dk_ref[...] = _bf(dk_acc[done])
        dv_ref[...] = _bf(dv_acc[done])

    tile = pl.BlockSpec((ATT_Q_TILE, ATT_WIDTH), lambda g, i: (jnp.minimum(i, nq - 1), g))
    late = pl.BlockSpec((ATT_Q_TILE, ATT_WIDTH), lambda g, i: (jnp.maximum(i - ATT_BACK, 0), g))
    ring = pltpu.VMEM((nt, ATT_Q_TILE, ATT_WIDTH), F32)
    return pl.pallas_call(
        body, name=name, grid=(ATT_GROUPS, nq + ATT_BACK),
        out_shape=[SDS((s, D_MODEL), BF16)] * 3 + [SDS((ATT_HEADS, ATT_Q_TILE, ATT_K_TILE), F32)],
        in_specs=_attn_specs(nq) + [tile, tile],
        out_specs=[tile, late, late, pl.BlockSpec((2 * ATT_PAIRS, ATT_Q_TILE, ATT_K_TILE), lambda g, i: (g, 0, 0))],
        scratch_shapes=[ring, ring],
        compiler_params=_params(("parallel", "arbitrary")),
    )(*([z] * (1 + 2 * nt)), bias, o, do)


FWD_GROUPS = (
    (("ab_w_in", 0),),
    (("ab_w_out", 0), ("w_ffn_in", 0)),
    (("w_ffn_out", 0),),
    (("c_w_qkv", 0),),
    (("c_w_out", 0), ("w_ffn_in", 1), ("w_ffn_out", 1), ("ab_w_in", 1)),
    (("ab_w_out", 1), ("w_ffn_in", 2), ("w_ffn_out", 2), ("c_w_qkv", 1)),
    (("c_w_out", 1), ("w_ffn_in", 3), ("w_ffn_out", 3)),
)


def _local_step(x, target, small, comm):
    s = x.shape[0]
    tabs = _retention_tables(s)
    saved, w = [], comm.weight
    for layer in range(DEPTH):
        i = layer // 2
        sv = {"x0": x}
        g_mix = small["mix_norm"][layer:layer + 1]
        if layer % 2 == 0:
            at = (x,) + tuple(tabs.values()) if layer == 0 else x
            sv["h1"], sv["z"] = _norm_mm(x, g_mix, w("ab_w_in", i, at), AB_IN_WIDTH, F32, False, "ab_in_fwd")
            gn = small["ab_gn_gain"][i:i + 1]
            sv["o_pre"], sv["states"], ret = _ret_fwd(sv["z"], tabs, gn, "ret_fwd")
            pool = _pool_fwd(sv["z"], small["ab_w_pool"][i], small["ab_pool_scale"][i:i + 1], "pool_fwd")
            sv["u"] = (ret, pool)
            x = _mm_res([ret, pool], w("ab_w_out", i, ret), x, "ab_out_fwd")
        else:
            sv["h1"], sv["z"] = _norm_mm(x, g_mix, w("c_w_qkv", i, x), 3 * D_MODEL // N_DEV, BF16, False, "qkv_fwd")
            rb = jnp.pad(small["c_rel_bias"][i], ((0, 0), (0, REL_PAD - N_REL)))
            sv["bias"] = _attn_bias(rb, "attn_bias")
            sv["o"] = _attn_fwd(sv["z"], sv["bias"], "attn_fwd")
            x = _mm_res([sv["o"]], w("c_w_out", i, sv["o"]), x, "c_out_fwd")
        sv["x1"] = x
        sv["h2"], sv["z1"], sv["a"] = _norm_mm(x, small["ffn_norm"][layer:layer + 1], w("w_ffn_in", layer, x),
                                               D_FF // N_DEV, BF16, True, "ffn_in_fwd")
        x = _mm_res([sv["a"]], w("w_ffn_out", layer, sv["a"]), x, "ffn_out_fwd")
        saved.append(sv)

    loss, dx, d_final = _final_loss(x, small["final_norm"][None, :], target, "final_loss")

    gs = {k: [None] * DEPTH for k in ("mix_norm", "ffn_norm")}
    for k in ("ab_gn_gain", "ab_w_pool", "ab_pool_scale", "c_rel_bias"):
        gs[k] = [None] * (DEPTH // 2)
    gs["final_norm"] = d_final[0]
    tok = jnp.zeros((), F32)
    for layer in reversed(range(DEPTH)):
        i = layer // 2
        sv = saved[layer]
        dz1 = _mm_nt_rows(dx, w("w_ffn_out", layer), sv["z1"], "ffn_out_bwd")
        gw = {("w_ffn_out", layer): _mm_tn(sv["a"], dx, 1024, D_MODEL, True, (0, 2), "ffn_out_dw"),
              ("w_ffn_in", layer): _mm_tn(sv["h2"], dz1, D_MODEL, 1024, False, (1, 2), "ffn_in_dw",
                                          comm.after() if layer == 0 else None)}
        dx, dg = _mm_nt_normbwd(dz1, w("w_ffn_in", layer), sv["x1"], small["ffn_norm"][layer:layer + 1] + tok, dx,
                                "ffn_in_bwd")
        gs["ffn_norm"][layer] = dg[0]
        if layer == 0:
            tok = comm.send(gw)
            gw = {}
        g_mix = small["mix_norm"][layer:layer + 1] + tok
        if layer % 2 == 0:
            du = _mm_nt_rows(dx, w("ab_w_out", i), None, "mix_out_bwd")
            gw["ab_w_out", i] = _mm_tn(sv["u"], dx, D_MODEL, D_MODEL, True, (0, N_DEV), "mix_out_dw")
            gn = small["ab_gn_gain"][i:i + 1]
            dz, dgn = _ret_bwd(sv["z"], tabs, gn, sv["o_pre"], sv["states"], du, "ret_bwd")
            dz, dwp, dsc = _pool_bwd(sv["z"], small["ab_w_pool"][i], small["ab_pool_scale"][i:i + 1], du, dz, "pool_bwd")
            gs["ab_gn_gain"][i], gs["ab_w_pool"][i], gs["ab_pool_scale"][i] = dgn[0], dwp, dsc[0]
            gw["ab_w_in", i] = _to_shard_major(_mm_tn(sv["h1"], dz, D_MODEL, AB_IN_WIDTH // 2, False, (1, 1), "ab_in_dw",
                                                      comm.after()), pairs_split=True)
            dx, dg = _mm_nt_normbwd(dz, w("ab_w_in", i), sv["x0"], g_mix, dx, "ab_in_bwd")
        else:
            do = _mm_nt_rows(dx, w("c_w_out", i), None, "mix_out_bwd")
            gw["c_w_out", i] = _mm_tn(sv["o"], dx, D_MODEL, D_MODEL, True, (0, N_DEV), "mix_out_dw")
            dq, dk, dv, dbias = _attn_bwd(sv["z"], sv["bias"], sv["o"], do, "attn_bwd")
            gs["c_rel_bias"][i] = _attn_dbias(dbias, "attn_dbias")[:, :N_REL]
            tiles = [_mm_tn(sv["h1"], part, D_MODEL, D_MODEL, False, (1, 1), "qkv_dw", after)
                     for part, after in ((dq, None), (dk, None), (dv, comm.after()))]
            gw["c_w_qkv", i] = _to_shard_major(jnp.concatenate(tiles, axis=0))
            dx, dg = _mm_nt_normbwd([dq, dk, dv], w("c_w_qkv", i), sv["x0"], g_mix, dx, "qkv_bwd")
        gs["mix_norm"][layer] = dg[0]
        if layer > 0:
            tok = comm.send(gw)
    gsmall = {k: (jnp.stack(v) if isinstance(v, list) else v) for k, v in gs.items()}
    return loss, dx, gw, gsmall


BIG = ("w_ffn_in", "w_ffn_out", "ab_w_in", "ab_w_out", "c_w_qkv", "c_w_out")
SMALL = ("mix_norm", "ffn_norm", "ab_gn_gain", "ab_w_pool", "ab_pool_scale", "c_rel_bias", "final_norm")
N_PEERS = N_DEV - 1
FLIPS = [(fx, fy, fc) for fx in (0, 1) for fy in (0, 1) for fc in (0, 1)][1:]


def _peers():
    x, y, c = (lax.axis_index(a) for a in MESH_AXES)
    peers = []
    for fx, fy, fc in FLIPS:
        px, py, pc = (1 - x if fx else x), (1 - y if fy else y), (1 - c if fc else c)
        peers.append(((px, py, pc), 4 * px + 2 * py + pc))
    return 4 * x + 2 * y + c, peers


def _exchange(srcs, by_slot, name, collective_id):
    n = len(srcs)
    src_refs = [jax.new_ref(a, memory_space=pltpu.MemorySpace.HBM) for a in srcs]
    land_refs = [jax.empty_ref(SDS((N_DEV,) + (a.shape[1:] if slotted else a.shape), a.dtype),
                               memory_space=pltpu.MemorySpace.HBM) for a, slotted in zip(srcs, by_slot)]

    @pl.kernel(mesh=plsc.ScalarSubcoreMesh(axis_name="sequencer", num_cores=1), name=name,
               scratch_types=(pltpu.SemaphoreType.DMA((n * N_PEERS,)), pltpu.SemaphoreType.DMA((n * N_PEERS,)),
                              pltpu.SemaphoreType.DMA((n,))),
               compiler_params=pltpu.CompilerParams(collective_id=collective_id))
    def launch(send_sems, recv_sems, local_sems):
        me, peers = _peers()
        barrier = pltpu.get_barrier_semaphore()
        for pos, _ in peers:
            pl.semaphore_signal(barrier, inc=1, device_id=pos, device_id_type=pl.DeviceIdType.MESH)
        pl.semaphore_wait(barrier, N_PEERS)
        waits = []
        for k in range(n):
            own = pltpu.make_async_copy(src_refs[k].at[me] if by_slot[k] else src_refs[k], land_refs[k].at[me],
                                        local_sems.at[k])
            own.start()
            waits.append(own.wait)
            for rel, (pos, slot) in enumerate(peers):
                src = src_refs[k].at[slot] if by_slot[k] else src_refs[k]
                sems = dict(send_sem=send_sems.at[k * N_PEERS + rel], recv_sem=recv_sems.at[k * N_PEERS + rel],
                            device_id=pos, device_id_type=pl.DeviceIdType.MESH)
                send = pltpu.make_async_remote_copy(src_ref=src, dst_ref=land_refs[k].at[me], **sems)
                send.start()
                arrival = pltpu.make_async_remote_copy(src_ref=src, dst_ref=land_refs[k].at[slot], **sems)
                waits += [send.wait_send, arrival.wait_recv]
        for wait in waits:
            wait()

    launch()
    return [r[...] for r in land_refs]


def _cast_group(weights, keys, token, name):
    def body(*refs):
        n = len(keys)
        for i_ref, o_ref in zip(refs[:n], refs[n + 1:]):
            o_ref[...] = _bf(i_ref[...])

    def layer_spec(shape, l):
        return pl.BlockSpec((None,) + shape[1:], lambda i: (l, 0, 0))

    whole = lambda shape: pl.BlockSpec(shape, lambda i: (0, 0))
    ins = [weights[k] for k, _ in keys]
    return pl.pallas_call(
        body, name=name, grid=(1,), out_shape=[SDS(w.shape[1:], BF16) for w in ins],
        in_specs=[layer_spec(w.shape, l) for w, (_, l) in zip(ins, keys)] + [pl.BlockSpec(memory_space=pl.ANY)],
        out_specs=[whole(w.shape[1:]) for w in ins],
        compiler_params=_params(("arbitrary",)),
    )(*ins, token)


def _to_shard_major(g, pairs_split=False):
    nj, ka, nb = g.shape
    full = jnp.transpose(g, (1, 0, 2)).reshape(ka, nj * nb)
    if pairs_split:
        full = _split_pairs(full, inverse=True)
    return jnp.transpose(full.reshape(ka, N_DEV, nj * nb // N_DEV), (1, 0, 2))


def _from_gathered(name, g):
    if name in ("w_ffn_out", "ab_w_out", "c_w_out"):
        return g.reshape(g.shape[0] * g.shape[1], g.shape[2])
    if name == "ab_w_in":
        return _split_pairs(jnp.transpose(g, (1, 0, 2)).reshape(1, g.shape[1], N_DEV * g.shape[2]))
    return g


class _Comm:
    def __init__(self, weights):
        self.weights_f32 = weights
        self.gathered = {}
        self.got = {}
        self.calls = 0
        self.ended = None
        self.opened = -1

    def _exchange(self, srcs, by_slot, name):
        self.calls += 1
        got = _exchange(srcs, by_slot, name, self.calls)
        self.ended = got[0][(0,) * got[0].ndim].astype(F32) * 0.0
        return got

    def _gather(self, group, at):
        keys = FWD_GROUPS[group]
        token = jnp.zeros((8, 128), F32) + (0.0 if self.ended is None else self.ended)
        for a in at if isinstance(at, tuple) else (at,):
            token = token + a[(0,) * a.ndim].astype(F32) * 0.0
        shards = _cast_group(self.weights_f32, keys, token, "cast_%d" % group)
        got = self._exchange(shards, [False] * len(keys), "gather_%d" % group)
        self.gathered.update((k, _from_gathered(k[0], arr)) for k, arr in zip(keys, got))

    def weight(self, name, layer, at=None):
        if (name, layer) not in self.gathered:
            self._gather(0, at[0] if isinstance(at, tuple) else at)
        group = next(g for g, keys in enumerate(FWD_GROUPS) if (name, layer) in keys)
        if group == self.opened + 1:
            self.opened = group
            if group + 1 < len(FWD_GROUPS):
                self._gather(group + 1, at)
        return self.gathered[name, layer]

    def after(self):
        return jnp.zeros((8, 128), F32) + self.ended

    def send(self, grads, shared=None):
        shared = shared or {}
        keys = list(grads) + list(shared)
        srcs = list(grads.values()) + list(shared.values())
        got = self._exchange(srcs, [True] * len(grads) + [False] * len(shared), "scatter_%d" % self.calls)
        self.got.update(zip(keys, got))
        return sum(g[0, 0, 0].astype(F32) * 0.0 for g in grads.values())

    def received(self):
        return self.got


def _adamw_math(g, w, m, v):
    m2 = ADAM_B1 * m + (1.0 - ADAM_B1) * g
    v2 = ADAM_B2 * v + (1.0 - ADAM_B2) * jnp.square(g)
    m_hat = m2 / (1.0 - ADAM_B1 ** ADAM_STEP)
    v_hat = v2 / (1.0 - ADAM_B2 ** ADAM_STEP)
    delta = -ADAM_LR * (m_hat / (jnp.sqrt(v_hat) + ADAM_EPS) + ADAM_WD * w)
    return delta, m2, v2


def _adamw(recv, w, m, v, name):
    nl, r, c = w.shape
    tr = _tile(r, 256)

    def body(*refs):
        g_refs = refs[:nl]
        w_ref, m_ref, v_ref, go_ref, d_ref, mo_ref, vo_ref = refs[nl:]
        for l in range(nl):
            @pl.when(pl.program_id(0) == l)
            def _():
                g = g_refs[l][0].astype(F32)
                for p in range(1, N_DEV):
                    g = g + g_refs[l][p].astype(F32)
                go_ref[...] = g
                d_ref[...], mo_ref[...], vo_ref[...] = _adamw_math(g, w_ref[...], m_ref[...], v_ref[...])

    def recv_spec(l):
        return pl.BlockSpec((N_DEV, tr, c), lambda layer, i: (0, jnp.where(layer == l, i, 0), 0))

    blk = pl.BlockSpec((None, tr, c), lambda l, i: (l, i, 0))
    return pl.pallas_call(
        body, name=name, grid=(nl, r // tr), out_shape=[SDS(w.shape, F32)] * 4,
        in_specs=[recv_spec(l) for l in range(nl)] + [blk, blk, blk],
        out_specs=[blk] * 4,
        compiler_params=_params(("arbitrary", "arbitrary")),
    )(*recv, w, m, v)


def _adamw_small(recv, loss_parts, w, m, v, name):
    n = len(w)

    def total(ref):
        t = ref[0]
        for p in range(1, N_DEV):
            t = t + ref[p]
        return t

    def body(*refs):
        g_refs, loss_ref = refs[:n], refs[n]
        w_refs, m_refs, v_refs = refs[n + 1:2 * n + 1], refs[2 * n + 1:3 * n + 1], refs[3 * n + 1:4 * n + 1]
        outs = refs[4 * n + 1:]
        for i in range(n):
            g = total(g_refs[i])
            outs[4 * i][...] = g
            outs[4 * i + 1][...], outs[4 * i + 2][...], outs[4 * i + 3][...] = _adamw_math(
                g, w_refs[i][...], m_refs[i][...], v_refs[i][...])
        outs[4 * n][...] = total(loss_ref)

    out_shape = [SDS(p.shape, F32) for p in w for _ in range(4)] + [SDS(loss_parts.shape[1:], F32)]
    outs = pl.pallas_call(body, name=name, out_shape=out_shape,
                          compiler_params=_params(None))(*recv, loss_parts, *w, *m, *v)
    return [outs[4 * i:4 * i + 4] for i in range(n)], outs[-1]


def kernel(x, mix_norm, ffn_norm, w_ffn_in, w_ffn_out, ab_w_in, ab_gn_gain, ab_w_pool, ab_pool_scale, ab_w_out, c_w_qkv, c_rel_bias, c_w_out, final_norm, loss_target, m_mix_norm, m_ffn_norm, m_w_ffn_in, m_w_ffn_out, m_ab_w_in, m_ab_gn_gain, m_ab_w_pool, m_ab_pool_scale, m_ab_w_out, m_c_w_qkv, m_c_rel_bias, m_c_w_out, m_final_norm, v_mix_norm, v_ffn_norm, v_w_ffn_in, v_w_ffn_out, v_ab_w_in, v_ab_gn_gain, v_ab_w_pool, v_ab_pool_scale, v_ab_w_out, v_c_w_qkv, v_c_rel_bias, v_c_w_out, v_final_norm):
    args = dict(locals())
    weights = {k: args[k] for k in BIG + SMALL}
    moments_m = {k: args["m_" + k] for k in BIG + SMALL}
    moments_v = {k: args["v_" + k] for k in BIG + SMALL}

    small = {k: weights[k] for k in SMALL}
    rows = lambda a: a.reshape(1, -1) if a.ndim == 1 else a

    comm = _Comm(weights)
    loss, dx, last_grads, gsmall = _local_step(x[0], loss_target[0], small, comm)
    comm.send(last_grads, {**{k: rows(gsmall[k]) for k in SMALL}, "loss": loss})
    recv = comm.received()

    outs = {}
    for k in BIG:
        layers = [recv[k, l] for l in range(weights[k].shape[0])]
        outs[k] = _adamw(layers, weights[k], moments_m[k], moments_v[k], "adamw_" + k)
    updated, total = _adamw_small([recv[k] for k in SMALL], recv["loss"], [rows(small[k]) for k in SMALL],
                                  [rows(moments_m[k]) for k in SMALL], [rows(moments_v[k]) for k in SMALL], "adamw_small")
    for k, parts in zip(SMALL, updated):
        outs[k] = [p.reshape(small[k].shape) for p in parts]

    order = SMALL[:2] + BIG[:2] + ("ab_w_in", "ab_gn_gain", "ab_w_pool", "ab_pool_scale", "ab_w_out",
                                   "c_w_qkv", "c_rel_bias", "c_w_out", "final_norm")
    result = [total[0, 0], dx[None]]
    for part in range(4):
        result += [outs[k][part] for k in order]
    return tuple(result)
```

```python
import functools

import jax
import jax.numpy as jnp
from jax import lax
from jax.experimental import pallas as pl
from jax.experimental.pallas import tpu as pltpu
from jax.experimental.pallas import tpu_sc as plsc

F32 = jnp.float32
BF16 = jnp.bfloat16
SDS = jax.ShapeDtypeStruct
MESH_AXES = ("x", "y", "c")
N_DEV = 8

D_MODEL = 1024
DEPTH = 4
CHUNK = 64
D_FF = 4 * D_MODEL
RMS_EPS = 1e-6
RET_WIDTH = 512
RET_HEADS = 4
RET_HEAD_DIM = 128
RET_ROPE_BASE = 10000.0
GN_EPS = 1e-5
POOL_WIDTH = 512
POOL_WINDOWS = (2, 4, 8, 16)
POOL_HALO = 16
AB_IN_WIDTH = 4 * RET_WIDTH + POOL_WIDTH
ATT_HEADS = 16
ATT_HEAD_DIM = 64
LEFT_CHUNKS = 8
REL_CLIP = 128
N_REL = 2 * REL_CLIP + 1
NEG_INF = -1e30

ADAM_LR = 0.001
ADAM_B1 = 0.9
ADAM_B2 = 0.999
ADAM_EPS = 1e-08
ADAM_WD = 0.01
ADAM_STEP = 10

TOKEN_TILE = 512
ATT_Q_TILE = 256
ATT_BACK = LEFT_CHUNKS * CHUNK // ATT_Q_TILE
ATT_K_TILE = (ATT_BACK + 1) * ATT_Q_TILE
ATT_PAIRS = 4
ATT_DIAG = 1024
REL_PAD = 384
VMEM_LIMIT_MB = 56

NT = (((1,), (1,)), ((), ()))
TN = (((0,), (0,)), ((), ()))


def _params(semantics, **kw):
    return pltpu.CompilerParams(dimension_semantics=semantics,
                                vmem_limit_bytes=VMEM_LIMIT_MB * 2 ** 20, **kw)


def _dot(a, b, dims=None):
    if dims is None:
        return jnp.dot(a, b, preferred_element_type=F32)
    return lax.dot_general(a, b, dims, preferred_element_type=F32)


def _bf(v):
    return v.astype(BF16)


def _tile(n, t):
    return min(n, t)


FOLLOW = pl.BlockSpec(memory_space=pl.ANY)


def _follow(after):
    return [] if after is None else list(after) if isinstance(after, (list, tuple)) else [after]


MXU_WIDTH = 256


def _mxu_group(nj, tn):
    return 2 if tn % MXU_WIDTH and (2 * tn) % MXU_WIDTH == 0 and nj % 2 == 0 else 1


def _w_tiles(w_ref, j, group):
    return w_ref[j] if group == 1 else jnp.concatenate([w_ref[j + t] for t in range(group)], axis=1)


def _w_cols(w_ref, j, group, c, width):
    return w_ref[j, :, c:c + width] if group == 1 else _w_tiles(w_ref, j, group)


def _norm_mm(x, gain, w, tn, z_dtype, relu2, name):
    s, d = x.shape
    nj = w.shape[0]
    tm = _tile(s, TOKEN_TILE)
    group = _mxu_group(nj, tn)

    def body(x_ref, g_ref, w_ref, h_ref, z_ref, *a_ref):
        xv = x_ref[...]
        r = lax.rsqrt(jnp.mean(xv * xv, axis=-1, keepdims=True) + RMS_EPS)
        h = _bf(xv * r * g_ref[...])
        h_ref[...] = h
        cw = tn if tn <= 512 else 512
        for j in range(0, nj, group):
            for c in range(0, tn, cw):
                z = _dot(h, _w_cols(w_ref, j, group, c, cw))
                cols = slice(j * tn + c, j * tn + c + group * cw)
                z_ref[:, cols] = z.astype(z_ref.dtype)
                if relu2:
                    a_ref[0][:, cols] = _bf(jnp.square(jnp.maximum(z, 0.0)))

    n = nj * tn
    out_shape = [SDS((s, d), BF16), SDS((s, n), z_dtype)]
    out_specs = [pl.BlockSpec((tm, d), lambda i: (i, 0)), pl.BlockSpec((tm, n), lambda i: (i, 0))]
    if relu2:
        out_shape.append(SDS((s, n), BF16))
        out_specs.append(pl.BlockSpec((tm, n), lambda i: (i, 0)))
    return pl.pallas_call(
        body, name=name, grid=(s // tm,), out_shape=out_shape,
        in_specs=[pl.BlockSpec((tm, d), lambda i: (i, 0)),
                  pl.BlockSpec((1, d), lambda i: (0, 0)),
                  pl.BlockSpec((nj, d, tn), lambda i: (0, 0, 0))],
        out_specs=out_specs,
        compiler_params=_params(("parallel",)),
    )(x, gain, w)


def _mm_res(parts, w, res, name):
    s, d = res.shape
    tm = _tile(s, TOKEN_TILE)
    widths = [p.shape[1] for p in parts]

    def body(*refs):
        a_refs = refs[:len(parts)]
        w_ref, res_ref, o_ref = refs[len(parts):]
        acc = res_ref[...]
        off = 0
        for a_ref, k in zip(a_refs, widths):
            acc = acc + _dot(a_ref[...], w_ref[off:off + k, :])
            off += k
        o_ref[...] = acc

    return pl.pallas_call(
        body, name=name, grid=(s // tm,), out_shape=SDS((s, d), F32),
        in_specs=[pl.BlockSpec((tm, k), lambda i: (i, 0)) for k in widths]
        + [pl.BlockSpec(w.shape, lambda i: (0, 0)), pl.BlockSpec((tm, d), lambda i: (i, 0))],
        out_specs=pl.BlockSpec((tm, d), lambda i: (i, 0)),
        compiler_params=_params(("parallel",)),
    )(*parts, w, res)


def _mm_nt_rows(dy, w, z, name):
    s, d = dy.shape
    k = w.shape[0]
    tm = _tile(s, TOKEN_TILE)
    tk = _tile(k, 1024)

    def body(dy_ref, w_ref, *rest):
        o_ref = rest[-1]
        dyb = _bf(dy_ref[...])
        for j in range(k // tk):
            cols = slice(j * tk, (j + 1) * tk)
            da = _dot(dyb, w_ref[cols, :], NT)
            if z is not None:
                da = da * (2.0 * jnp.maximum(rest[0][:, cols].astype(F32), 0.0))
            o_ref[:, cols] = _bf(da)

    in_specs = [pl.BlockSpec((tm, d), lambda i: (i, 0)), pl.BlockSpec((k, d), lambda i: (0, 0))]
    args = [dy, w]
    if z is not None:
        in_specs.append(pl.BlockSpec((tm, k), lambda i: (i, 0)))
        args.append(z)
    return pl.pallas_call(
        body, name=name, grid=(s // tm,), out_shape=SDS((s, k), BF16),
        in_specs=in_specs, out_specs=pl.BlockSpec((tm, k), lambda i: (i, 0)),
        compiler_params=_params(("parallel",)),
    )(*args)


def _w_range(w_ref, c0, c1):
    nc = w_ref.shape[2]
    pieces, c = [], c0
    while c < c1:
        j = c // nc
        hi = min(nc, c1 - j * nc)
        pieces.append(w_ref[j, :, c - j * nc:hi])
        c = j * nc + hi
    return pieces[0] if len(pieces) == 1 else jnp.concatenate(pieces, axis=1)


def _mm_nt_normbwd(dz, w, x, gain, dres, name, after=None):
    parts, after = list(dz) if isinstance(dz, (list, tuple)) else [dz], _follow(after)
    widths = [p.shape[1] for p in parts]
    s, d = x.shape
    tm = _tile(s, TOKEN_TILE)
    chunk = 2 * MXU_WIDTH
    halves = 2 if tm % 32 == 0 else 1

    def body(*refs):
        dz_refs = refs[:len(parts)]
        w_ref, x_ref, g_ref, dres_ref = refs[len(parts):len(parts) + 4]
        dx_ref, dg_ref = refs[-2:]

        @pl.when(pl.program_id(0) == 0)
        def _():
            dg_ref[...] = jnp.zeros_like(dg_ref)

        for half in range(halves):
            rows = slice(half * tm // halves, (half + 1) * tm // halves)
            dh, base = None, 0
            for dz_ref, width in zip(dz_refs, widths):
                for c in range(0, width, chunk):
                    term = _dot(dz_ref[rows, c:c + chunk], _w_range(w_ref, base + c, base + c + chunk), NT)
                    dh = term if dh is None else dh + term
                base += width
            xv = x_ref[rows, :]
            r = lax.rsqrt(jnp.mean(xv * xv, axis=-1, keepdims=True) + RMS_EPS)
            xn = xv * r
            dg_ref[...] += jnp.sum(dh * xn, axis=0, keepdims=True)
            dxh = dh * g_ref[...]
            dx_ref[rows, :] = dres_ref[rows, :] + r * (dxh - xn * jnp.mean(dxh * xn, axis=-1, keepdims=True))

    return pl.pallas_call(
        body, name=name, grid=(s // tm,), out_shape=[SDS((s, d), F32), SDS((1, d), F32)],
        in_specs=[pl.BlockSpec((tm, width), lambda i: (i, 0)) for width in widths]
        + [pl.BlockSpec(w.shape, lambda i: (0, 0, 0)),
           pl.BlockSpec((tm, d), lambda i: (i, 0)),
           pl.BlockSpec((1, d), lambda i: (0, 0)),
           pl.BlockSpec((tm, d), lambda i: (i, 0))] + [FOLLOW] * len(after),
        out_specs=[pl.BlockSpec((tm, d), lambda i: (i, 0)), pl.BlockSpec((1, d), lambda i: (0, 0))],
        compiler_params=_params(("arbitrary",)),
    )(*parts, w, x, gain, dres, *after)


def _mm_tn(a, b, ka, nb, a_tiled, split, name, after=None):
    a_parts, after = list(a) if isinstance(a, (list, tuple)) else [a], _follow(after)
    s = a_parts[0].shape[0]
    tm = _tile(s, 4 * TOKEN_TILE)
    nm = s // tm
    nj = a_parts[0].shape[1] // ka if a_tiled and len(a_parts) == 1 else (1 if a_tiled else b.shape[1] // nb)
    axis, parts = split
    pr, pc = (ka // parts, nb) if axis == 0 else (ka, nb // parts)

    def body(*refs):
        a_refs, b_ref = refs[:len(a_parts)], refs[len(a_parts)]
        o_ref, acc = refs[-2:]
        m = pl.program_id(1)

        @pl.when(m == 0)
        def _():
            acc[...] = jnp.zeros_like(acc)

        av = a_refs[0][...] if len(a_refs) == 1 else jnp.concatenate([r[...] for r in a_refs], axis=1)
        acc[...] += _dot(_bf(av), _bf(b_ref[...]), TN)

        @pl.when(m == nm - 1)
        def _():
            for q in range(parts):
                piece = acc[q * pr:(q + 1) * pr, :] if axis == 0 else acc[:, q * pc:(q + 1) * pc]
                o_ref[q] = piece.astype(o_ref.dtype)

    return pl.pallas_call(
        body, name=name, grid=(nj, nm), out_shape=SDS((nj * parts, pr, pc), BF16),
        in_specs=([pl.BlockSpec((tm, ka), (lambda j, m: (m, j)) if a_tiled else (lambda j, m: (m, 0)))]
                  if len(a_parts) == 1 else [pl.BlockSpec((tm, p.shape[1]), lambda j, m: (m, 0)) for p in a_parts])
        + [pl.BlockSpec((tm, nb), (lambda j, m: (m, 0)) if a_tiled else (lambda j, m: (m, j)))]
        + [FOLLOW] * len(after),
        out_specs=pl.BlockSpec((parts, pr, pc), lambda j, m: (j, 0, 0)),
        scratch_shapes=[pltpu.VMEM((ka, nb), F32)],
        compiler_params=_params(("parallel", "arbitrary")),
    )(*a_parts, b, *after)


def _final_loss(x, gain, target, name):
    s, d = x.shape
    tm = _tile(s, TOKEN_TILE)

    def body(x_ref, g_ref, t_ref, loss_ref, dx_ref, dg_ref):
        @pl.when(pl.program_id(0) == 0)
        def _():
            loss_ref[...] = jnp.zeros_like(loss_ref)
            dg_ref[...] = jnp.zeros_like(dg_ref)

        xv = x_ref[...]
        r = lax.rsqrt(jnp.mean(xv * xv, axis=-1, keepdims=True) + RMS_EPS)
        xn = xv * r
        err = xn * g_ref[...] - t_ref[...]
        loss_ref[...] += (0.5 / d) * jnp.sum(err * err)
        dy = err * (1.0 / d)
        dg_ref[...] += jnp.sum(dy * xn, axis=0, keepdims=True)
        dxh = dy * g_ref[...]
        dx_ref[...] = r * (dxh - xn * jnp.mean(dxh * xn, axis=-1, keepdims=True))

    return pl.pallas_call(
        body, name=name, grid=(s // tm,),
        out_shape=[SDS((8, 128), F32), SDS((s, d), F32), SDS((1, d), F32)],
        in_specs=[pl.BlockSpec((tm, d), lambda i: (i, 0)), pl.BlockSpec((1, d), lambda i: (0, 0)),
                  pl.BlockSpec((tm, d), lambda i: (i, 0))],
        out_specs=[pl.BlockSpec((8, 128), lambda i: (0, 0)), pl.BlockSpec((tm, d), lambda i: (i, 0)),
                   pl.BlockSpec((1, d), lambda i: (0, 0))],
        compiler_params=_params(("arbitrary",)),
    )(x, gain, target)


def _retention_tables(s):
    half = RET_HEAD_DIM // 2
    inv_freq = 1.0 / (RET_ROPE_BASE ** jnp.linspace(0.0, 1.0, half, dtype=F32))
    ang = jnp.arange(s, dtype=F32)[:, None] * inv_freq[None, :]
    cos, sin = jnp.cos(ang), jnp.sin(ang)
    cos_e = jnp.concatenate([cos, cos], axis=-1)
    sin_s = jnp.concatenate([-sin, sin], axis=-1)
    log_g = jnp.log1p(-jnp.power(2.0, -5.0 - jnp.arange(RET_HEADS, dtype=F32)))
    pos = jnp.arange(CHUNK, dtype=F32)
    dmat = jnp.exp(jnp.abs(pos[:, None] - pos[None, :])[None] * log_g[:, None, None])
    qdec = jnp.exp((pos[None, :] + 1.0) * log_g[:, None])
    kdec = jnp.exp((CHUNK - 1.0 - pos[None, :]) * log_g[:, None])
    lam = jnp.exp(CHUNK * log_g)
    wide = (RET_HEADS, CHUNK, RET_HEAD_DIM)
    return dict(cos=cos_e, sin=sin_s, dmat=dmat,
                qdec=jnp.broadcast_to(qdec[:, :, None], wide),
                kdec=jnp.broadcast_to(kdec[:, :, None], wide),
                lam=jnp.broadcast_to(lam[:, None, None], (RET_HEADS, RET_HEAD_DIM, RET_HEAD_DIM)))


def _swap_pairs(t):
    return pltpu.roll(t, RET_HEAD_DIM // 2, 1)


def _split_pairs(w, inverse=False):
    lead, nqk = w.shape[:-1], 2 * RET_WIDTH
    shape = (2 * RET_HEADS, 2, RET_HEAD_DIM // 2) if inverse else (2 * RET_HEADS, RET_HEAD_DIM // 2, 2)
    qk = jnp.swapaxes(w[..., :nqk].reshape(lead + shape), -1, -2).reshape(lead + (nqk,))
    return jnp.concatenate([qk, w[..., nqk:]], axis=-1)


def _head(h):
    return slice(h * RET_HEAD_DIM, (h + 1) * RET_HEAD_DIM)


def _ret_common_specs(tb, blk):
    zs = [pl.BlockSpec((tb, RET_WIDTH), functools.partial(lambda j, i: (blk(i), j), j)) for j in range(4)]
    tabs = [pl.BlockSpec((tb, RET_HEAD_DIM), lambda i: (blk(i), 0))] * 2
    consts = [pl.BlockSpec((1, RET_WIDTH), lambda i: (0, 0)),
              pl.BlockSpec((RET_HEADS, CHUNK, CHUNK), lambda i: (0, 0, 0)),
              pl.BlockSpec((RET_HEADS, CHUNK, RET_HEAD_DIM), lambda i: (0, 0, 0)),
              pl.BlockSpec((RET_HEADS, CHUNK, RET_HEAD_DIM), lambda i: (0, 0, 0)),
              pl.BlockSpec((RET_HEADS, RET_HEAD_DIM, RET_HEAD_DIM), lambda i: (0, 0, 0))]
    return zs + tabs + consts


def _ret_fwd(z, tabs, gn_gain, name):
    s = z.shape[0]
    tb = _tile(s, TOKEN_TILE)
    ncb = tb // CHUNK
    scale = RET_HEAD_DIM ** -0.5

    def body(q_ref, k_ref, v_ref, g_ref, cos_ref, sin_ref, gain_ref, dm_ref, qd_ref, kd_ref, lam_ref,
             o_ref, st_ref, ret_ref, s_scr, qr_scr, kr_scr):
        @pl.when(pl.program_id(0) == 0)
        def _():
            s_scr[...] = jnp.zeros_like(s_scr)

        cosv, sinv = cos_ref[...], sin_ref[...]
        for h in range(RET_HEADS):
            qh, kh = q_ref[:, _head(h)], k_ref[:, _head(h)]
            qr_scr[:, _head(h)] = qh * cosv + _swap_pairs(qh) * sinv
            kr_scr[:, _head(h)] = (kh * cosv + _swap_pairs(kh) * sinv) * scale

        def chunk(c, carry):
            rows = pl.ds(pl.multiple_of(c * CHUNK, CHUNK), CHUNK)
            for h in range(RET_HEADS):
                qc, kc, vc = qr_scr[rows, _head(h)], kr_scr[rows, _head(h)], v_ref[rows, _head(h)]
                a = _dot(_bf(qc), _bf(kc), NT) * dm_ref[h]
                st = s_scr[h]
                st_ref[c, h] = st
                o_ref[rows, _head(h)] = _dot(_bf(a), _bf(vc)) + _dot(_bf(qc * qd_ref[h]), _bf(st))
                s_scr[h] = st * lam_ref[h] + _dot(_bf(kc * kd_ref[h]), _bf(vc), TN)
            return carry

        lax.fori_loop(0, ncb, chunk, 0, unroll=4)
        for h in range(RET_HEADS):
            o = o_ref[:, _head(h)]
            mu = jnp.mean(o, axis=-1, keepdims=True)
            oc = o - mu
            y = oc * lax.rsqrt(jnp.mean(oc * oc, axis=-1, keepdims=True) + GN_EPS) * gain_ref[:, _head(h)]
            g = g_ref[:, _head(h)]
            ret_ref[:, _head(h)] = _bf(g / (1.0 + jnp.exp(-g)) * y)

    nc = s // CHUNK
    return pl.pallas_call(
        body, name=name, grid=(s // tb,),
        out_shape=[SDS((s, RET_WIDTH), F32), SDS((nc, RET_HEADS, RET_HEAD_DIM, RET_HEAD_DIM), F32),
                   SDS((s, RET_WIDTH), BF16)],
        in_specs=_ret_common_specs(tb, lambda i: i),
        out_specs=[pl.BlockSpec((tb, RET_WIDTH), lambda i: (i, 0)),
                   pl.BlockSpec((ncb, RET_HEADS, RET_HEAD_DIM, RET_HEAD_DIM), lambda i: (i, 0, 0, 0)),
                   pl.BlockSpec((tb, RET_WIDTH), lambda i: (i, 0))],
        scratch_shapes=[pltpu.VMEM((RET_HEADS, RET_HEAD_DIM, RET_HEAD_DIM), F32),
                        pltpu.VMEM((tb, RET_WIDTH), F32), pltpu.VMEM((tb, RET_WIDTH), F32)],
        compiler_params=_params(("arbitrary",)),
    )(z, z, z, z, tabs["cos"], tabs["sin"], gn_gain, tabs["dmat"], tabs["qdec"], tabs["kdec"], tabs["lam"])


def _ret_bwd(z, tabs, gn_gain, o_pre, states, du, name):
    s = z.shape[0]
    tb = _tile(s, TOKEN_TILE)
    ncb = tb // CHUNK
    nblk = s // tb
    scale = RET_HEAD_DIM ** -0.5
    rev = lambda i: nblk - 1 - i

    def body(q_ref, k_ref, v_ref, g_ref, cos_ref, sin_ref, gain_ref, dm_ref, qd_ref, kd_ref, lam_ref,
             o_ref, st_ref, dret_ref, dz_ref, dgain_ref, g_scr, qr_scr, kr_scr, do_scr, dq_scr, dk_scr):
        @pl.when(pl.program_id(0) == 0)
        def _():
            g_scr[...] = jnp.zeros_like(g_scr)
            dgain_ref[...] = jnp.zeros_like(dgain_ref)

        cosv, sinv = cos_ref[...], sin_ref[...]
        for h in range(RET_HEADS):
            hs = _head(h)
            qh, kh = q_ref[:, hs], k_ref[:, hs]
            qr_scr[:, hs] = qh * cosv + _swap_pairs(qh) * sinv
            kr_scr[:, hs] = (kh * cosv + _swap_pairs(kh) * sinv) * scale
            o = o_ref[:, hs]
            mu = jnp.mean(o, axis=-1, keepdims=True)
            oc = o - mu
            rstd = lax.rsqrt(jnp.mean(oc * oc, axis=-1, keepdims=True) + GN_EPS)
            yh = oc * rstd
            gain = gain_ref[:, hs]
            g = g_ref[:, hs]
            sg = 1.0 / (1.0 + jnp.exp(-g))
            dret = dret_ref[:, hs].astype(F32)
            dy = dret * (g * sg)
            dz_ref[:, 3 * RET_WIDTH + h * RET_HEAD_DIM:3 * RET_WIDTH + (h + 1) * RET_HEAD_DIM] = _bf(
                dret * (yh * gain) * (sg * (1.0 + g * (1.0 - sg))))
            dgain_ref[:, hs] += jnp.sum(dy * yh, axis=0, keepdims=True)
            dyh = dy * gain
            do_scr[:, hs] = rstd * (dyh - jnp.mean(dyh, axis=-1, keepdims=True)
                                    - yh * jnp.mean(dyh * yh, axis=-1, keepdims=True))

        def chunk(cc, carry):
            c = ncb - 1 - cc
            rows = pl.ds(pl.multiple_of(c * CHUNK, CHUNK), CHUNK)
            for h in range(RET_HEADS):
                hs = _head(h)
                qc, kc, vc, doc = _bf(qr_scr[rows, hs]), _bf(kr_scr[rows, hs]), _bf(v_ref[rows, hs]), _bf(do_scr[rows, hs])
                qdc, kdc = qd_ref[h], kd_ref[h]
                st, gs = _bf(st_ref[c, h]), g_scr[h]
                gsb = _bf(gs)
                dm = dm_ref[h]
                p = _bf(_dot(qc, kc, NT) * dm)
                da = _bf(_dot(doc, vc, NT) * dm)
                kt = _bf(kr_scr[rows, hs] * kdc)
                qt = _bf(qr_scr[rows, hs] * qdc)
                dz_ref[rows, 2 * RET_WIDTH + h * RET_HEAD_DIM:2 * RET_WIDTH + (h + 1) * RET_HEAD_DIM] = _bf(
                    _dot(p, doc, TN) + _dot(kt, gsb))
                dq_scr[rows, hs] = _dot(da, kc) + _dot(doc, st, NT) * qdc
                dk_scr[rows, hs] = _dot(da, qc, TN) + _dot(vc, gsb, NT) * kdc
                g_scr[h] = gs * lam_ref[h] + _dot(qt, doc, TN)
            return carry

        lax.fori_loop(0, ncb, chunk, 0, unroll=4)
        for h in range(RET_HEADS):
            hs = _head(h)
            dq, dk = dq_scr[:, hs], dk_scr[:, hs]
            dz_ref[:, h * RET_HEAD_DIM:(h + 1) * RET_HEAD_DIM] = _bf(dq * cosv - _swap_pairs(dq) * sinv)
            dz_ref[:, RET_WIDTH + h * RET_HEAD_DIM:RET_WIDTH + (h + 1) * RET_HEAD_DIM] = _bf(
                (dk * cosv - _swap_pairs(dk) * sinv) * scale)

    return pl.pallas_call(
        body, name=name, grid=(nblk,),
        out_shape=[SDS((s, AB_IN_WIDTH), BF16), SDS((1, RET_WIDTH), F32)],
        in_specs=_ret_common_specs(tb, rev)
        + [pl.BlockSpec((tb, RET_WIDTH), lambda i: (rev(i), 0)),
           pl.BlockSpec((ncb, RET_HEADS, RET_HEAD_DIM, RET_HEAD_DIM), lambda i: (rev(i), 0, 0, 0)),
           pl.BlockSpec((tb, RET_WIDTH), lambda i: (rev(i), 0))],
        out_specs=[pl.BlockSpec((tb, 4 * RET_WIDTH), lambda i: (rev(i), 0)),
                   pl.BlockSpec((1, RET_WIDTH), lambda i: (0, 0))],
        scratch_shapes=[pltpu.VMEM((RET_HEADS, RET_HEAD_DIM, RET_HEAD_DIM), F32)]
        + [pltpu.VMEM((tb, RET_WIDTH), F32)] * 5,
        compiler_params=_params(("arbitrary",)),
    )(z, z, z, z, tabs["cos"], tabs["sin"], gn_gain, tabs["dmat"], tabs["qdec"], tabs["kdec"], tabs["lam"],
      o_pre, states, du)


POOL_COL = 4 * RET_WIDTH // POOL_WIDTH


def _pooled(cur, prev, t0):
    tm = cur.shape[0]
    xx = jnp.concatenate([prev, cur], axis=0)
    sums = {1: xx}
    w = 1
    while w < POOL_WINDOWS[-1]:
        sums[2 * w] = sums[w] + pltpu.roll(sums[w], w, 0)
        w *= 2
    t = t0 + lax.broadcasted_iota(jnp.int32, (tm, 128), 0)
    outs = []
    for gi, w in enumerate(POOL_WINDOWS):
        cols = slice(gi * 128, (gi + 1) * 128)
        cnt = jnp.minimum(t + 1, w).astype(F32)
        outs.append(sums[w][POOL_HALO:, cols] / cnt - cur[:, cols])
    return outs


def _pool_fwd(z, w_pool, scale, name):
    s = z.shape[0]
    tm = _tile(s, TOKEN_TILE)
    hb = tm // POOL_HALO

    def body(p_ref, prev_ref, w_ref, sc_ref, o_ref):
        i = pl.program_id(0)
        prev = jnp.where(i > 0, prev_ref[...], 0.0)
        pooled = _pooled(p_ref[...], prev, i * tm)
        for gi in range(len(POOL_WINDOWS)):
            cols = slice(gi * 128, (gi + 1) * 128)
            o_ref[:, cols] = _bf(_dot(_bf(pooled[gi]), _bf(w_ref[gi])) * sc_ref[:, cols])

    return pl.pallas_call(
        body, name=name, grid=(s // tm,), out_shape=SDS((s, POOL_WIDTH), BF16),
        in_specs=[pl.BlockSpec((tm, POOL_WIDTH), lambda i: (i, POOL_COL)),
                  pl.BlockSpec((POOL_HALO, POOL_WIDTH), lambda i: (jnp.maximum(i * hb - 1, 0), POOL_COL)),
                  pl.BlockSpec(w_pool.shape, lambda i: (0, 0, 0)),
                  pl.BlockSpec((1, POOL_WIDTH), lambda i: (0, 0))],
        out_specs=pl.BlockSpec((tm, POOL_WIDTH), lambda i: (i, 0)),
        compiler_params=_params(("parallel",)),
    )(z, z, w_pool, scale)


def _pool_bwd(z, w_pool, scale, du, dz, name):
    s = z.shape[0]
    tm = _tile(s, TOKEN_TILE)
    hb = tm // POOL_HALO
    nblk = s // tm
    last_halo = s // POOL_HALO - 1

    def body(p_ref, prev_ref, w_ref, sc_ref, do_ref, don_ref, dz_ref, dp_ref, dw_ref, dsc_ref):
        i = pl.program_id(0)

        @pl.when(i == 0)
        def _():
            dw_ref[...] = jnp.zeros_like(dw_ref)
            dsc_ref[...] = jnp.zeros_like(dsc_ref)

        prev = jnp.where(i > 0, prev_ref[...], 0.0)
        pooled = _pooled(p_ref[...], prev, i * tm)
        dout = do_ref[...].astype(F32)
        dout_next = jnp.where(i < nblk - 1, don_ref[...].astype(F32), 0.0)
        sc = sc_ref[...]
        dmix = jnp.concatenate([dout * sc, dout_next * sc], axis=0)
        n = tm + POOL_HALO
        t = i * tm + lax.broadcasted_iota(jnp.int32, (n, 128), 0)
        for gi, w in enumerate(POOL_WINDOWS):
            cols = slice(gi * 128, (gi + 1) * 128)
            wg = _bf(w_ref[gi])
            pg = _bf(pooled[gi])
            dsc_ref[:, cols] += jnp.sum(dout[:, cols] * _dot(pg, wg), axis=0, keepdims=True)
            dw_ref[gi] += _dot(pg, _bf(dmix[:tm, cols]), TN)
            dpool = _dot(_bf(dmix[:, cols]), wg, NT)
            acc = dpool / jnp.minimum(t + 1, w).astype(F32)
            step = 1
            while step < w:
                acc = acc + pltpu.roll(acc, n - step, 0)
                step *= 2
            dp_ref[:, cols] = _bf(acc[:tm] - dpool[:tm])

    return pl.pallas_call(
        body, name=name, grid=(nblk,),
        out_shape=[SDS(dz.shape, BF16), SDS(w_pool.shape, F32), SDS((1, POOL_WIDTH), F32)],
        in_specs=[pl.BlockSpec((tm, POOL_WIDTH), lambda i: (i, POOL_COL)),
                  pl.BlockSpec((POOL_HALO, POOL_WIDTH), lambda i: (jnp.maximum(i * hb - 1, 0), POOL_COL)),
                  pl.BlockSpec(w_pool.shape, lambda i: (0, 0, 0)),
                  pl.BlockSpec((1, POOL_WIDTH), lambda i: (0, 0)),
                  pl.BlockSpec((tm, POOL_WIDTH), lambda i: (i, 1)),
                  pl.BlockSpec((POOL_HALO, POOL_WIDTH), lambda i: (jnp.minimum((i + 1) * hb, last_halo), 1)),
                  pl.BlockSpec(memory_space=pl.ANY)],
        out_specs=[pl.BlockSpec((tm, POOL_WIDTH), lambda i: (i, POOL_COL)),
                   pl.BlockSpec(w_pool.shape, lambda i: (0, 0, 0)),
                   pl.BlockSpec((1, POOL_WIDTH), lambda i: (0, 0))],
        input_output_aliases={6: 0},
        compiler_params=_params(("arbitrary",)),
    )(z, z, w_pool, scale, du, du, dz)


def _rel_onehot():
    r = lax.broadcasted_iota(jnp.int32, (REL_PAD, ATT_DIAG), 0)
    c = lax.broadcasted_iota(jnp.int32, (REL_PAD, ATT_DIAG), 1)
    rel = jnp.where(c < ATT_K_TILE, jnp.clip(LEFT_CHUNKS * CHUNK - c, -REL_CLIP, REL_CLIP) + REL_CLIP,
                    2 * REL_CLIP)
    return (rel == r).astype(BF16)


def _split3(v):
    hi = _bf(v)
    r1 = v - hi.astype(F32)
    mid = _bf(r1)
    return hi, mid, _bf(r1 - mid.astype(F32))


def _skew(v, sign):
    row = lax.broadcasted_iota(jnp.int32, v.shape, 0)
    bit = 1
    while bit < ATT_Q_TILE:
        shift = bit if sign > 0 else ATT_DIAG - bit
        v = jnp.where((row & bit) != 0, pltpu.roll(v, shift, 1), v)
        bit *= 2
    return v


def _attn_bias(rel_bias, name):
    def body(t_ref, o_ref):
        oh = _rel_onehot()
        base = sum(_dot(part, oh) for part in _split3(t_ref[0]))
        full = _skew(jnp.broadcast_to(base[0:1], (ATT_Q_TILE, ATT_DIAG)), +1)[:, :ATT_K_TILE]
        qc = lax.broadcasted_iota(jnp.int32, full.shape, 0) // CHUNK
        kc = lax.broadcasted_iota(jnp.int32, full.shape, 1) // CHUNK
        o_ref[0] = jnp.where((kc >= qc) & (kc <= qc + LEFT_CHUNKS), full, NEG_INF)

    t8 = jnp.broadcast_to(rel_bias[:, None, :], (ATT_HEADS, 8, REL_PAD))
    return pl.pallas_call(
        body, name=name, grid=(ATT_HEADS,), out_shape=SDS((ATT_HEADS, ATT_Q_TILE, ATT_K_TILE), F32),
        in_specs=[pl.BlockSpec((1, 8, REL_PAD), lambda h: (h, 0, 0))],
        out_specs=pl.BlockSpec((1, ATT_Q_TILE, ATT_K_TILE), lambda h: (h, 0, 0)),
        compiler_params=_params(("parallel",)),
    )(t8)


def _attn_dbias(dbias, name):
    def body(d_ref, o_ref):
        pad = jnp.zeros((ATT_Q_TILE, ATT_DIAG - ATT_K_TILE), F32)
        diag = _skew(jnp.concatenate([d_ref[0], pad], axis=1), -1)
        col = jnp.sum(diag, axis=0, keepdims=True)
        oh = _rel_onehot()
        col8 = jnp.broadcast_to(col, (8, ATT_DIAG))
        o_ref[0] = sum(_dot(part, oh, NT) for part in _split3(col8))

    out = pl.pallas_call(
        body, name=name, grid=(ATT_HEADS,), out_shape=SDS((ATT_HEADS, 8, REL_PAD), F32),
        in_specs=[pl.BlockSpec((1, ATT_Q_TILE, ATT_K_TILE), lambda h: (h, 0, 0))],
        out_specs=pl.BlockSpec((1, 8, REL_PAD), lambda h: (h, 0, 0)),
        compiler_params=_params(("parallel",)),
    )(dbias)
    return out[:, 0, :]


ATT_WIDTH = 128 * ATT_PAIRS
ATT_GROUPS = D_MODEL // ATT_WIDTH


def _attn_specs(nq):
    def tile(off, back):
        return pl.BlockSpec((ATT_Q_TILE, ATT_WIDTH),
                            lambda g, i: (jnp.maximum(jnp.minimum(i, nq - 1) - back, 0), off + g))

    backs = [ATT_BACK - b for b in range(ATT_BACK + 1)]
    return ([tile(0, 0)] + [tile(ATT_GROUPS, b) for b in backs] + [tile(2 * ATT_GROUPS, b) for b in backs]
            + [pl.BlockSpec((2 * ATT_PAIRS, ATT_Q_TILE, ATT_K_TILE), lambda g, i: (g, 0, 0))])


def _attn_weights(qh, k2, bias, i, masked):
    sc = _dot(qh, k2, NT) + bias
    if masked:
        kpos = (i - ATT_BACK) * ATT_Q_TILE + lax.broadcasted_iota(jnp.int32, sc.shape, 1)
        sc = jnp.where(kpos >= 0, sc, NEG_INF)
    e = jnp.exp(sc - jnp.max(sc, axis=-1, keepdims=True))
    return e, 1.0 / jnp.sum(e, axis=-1, keepdims=True)


def _first_head():
    return lax.broadcasted_iota(jnp.int32, (ATT_Q_TILE, 128), 1) < ATT_HEAD_DIM


def _pair_operands(q_ref, k_refs, v_refs, pp):
    cols = slice(pp * 128, (pp + 1) * 128)
    q2 = q_ref[:, cols] * ATT_HEAD_DIM ** -0.5
    k2 = jnp.concatenate([r[:, cols] for r in k_refs], axis=0)
    v2 = jnp.concatenate([r[:, cols] for r in v_refs], axis=0)
    return cols, q2, k2, v2


def _attn_fwd(z, bias, name):
    s = z.shape[0]
    nq = s // ATT_Q_TILE
    nt = ATT_BACK + 1

    def body(q_ref, *rest):
        k_refs, v_refs, (b_ref, o_ref) = rest[:nt], rest[nt:2 * nt], rest[2 * nt:]
        i = pl.program_id(1)
        first = _first_head()

        def compute(masked):
            for pp in range(ATT_PAIRS):
                cols, q2, k2, v2 = _pair_operands(q_ref, k_refs, v_refs, pp)
                outs = []
                for hh in range(2):
                    qh = jnp.where(first if hh == 0 else ~first, q2, 0)
                    e, inv = _attn_weights(qh, k2, b_ref[2 * pp + hh], i, masked)
                    outs.append(_dot(_bf(e), v2) * inv)
                o_ref[:, cols] = _bf(jnp.where(first, outs[0], outs[1]))

        pl.when(i < ATT_BACK)(lambda: compute(True))
        pl.when(i >= ATT_BACK)(lambda: compute(False))

    return pl.pallas_call(
        body, name=name, grid=(ATT_GROUPS, nq), out_shape=SDS((s, D_MODEL), BF16),
        in_specs=_attn_specs(nq),
        out_specs=pl.BlockSpec((ATT_Q_TILE, ATT_WIDTH), lambda g, i: (i, g)),
        compiler_params=_params(("parallel", "parallel")),
    )(*([z] * (1 + 2 * nt)), bias)


def _attn_bwd(z, bias, o, do, name):
    s = z.shape[0]
    nq = s // ATT_Q_TILE
    nt = ATT_BACK + 1

    def body(q_ref, *rest):
        k_refs, v_refs = rest[:nt], rest[nt:2 * nt]
        b_ref, o_ref, do_ref, dq_ref, dk_ref, dv_ref, db_ref, dk_acc, dv_acc = rest[2 * nt:]
        i = pl.program_id(1)
        first = _first_head()

        @pl.when(i == 0)
        def _():
            db_ref[...] = jnp.zeros_like(db_ref)
            dk_acc[...] = jnp.zeros_like(dk_acc)
            dv_acc[...] = jnp.zeros_like(dv_acc)

        def compute(masked):
            for pp in range(ATT_PAIRS):
                cols, q2, k2, v2 = _pair_operands(q_ref, k_refs, v_refs, pp)
                do2 = do_ref[:, cols].astype(F32)
                prod = do2 * o_ref[:, cols].astype(F32)
                dqs, dk, dv = [], None, None
                for hh in range(2):
                    mine = first if hh == 0 else ~first
                    qh = jnp.where(mine, q2, 0)
                    e, inv = _attn_weights(qh, k2, b_ref[2 * pp + hh], i, masked)
                    delta = jnp.sum(jnp.where(mine, prod, 0.0), axis=-1, keepdims=True) * inv
                    doh = _bf(jnp.where(mine, do2 * inv, 0.0))
                    ds = e * (_dot(doh, v2, NT) - delta)
                    db_ref[2 * pp + hh] += ds
                    dsb = _bf(ds)
                    dqs.append(_dot(dsb, k2))
                    dkh, dvh = _dot(dsb, qh, TN), _dot(_bf(e), doh, TN)
                    dk, dv = (dkh, dvh) if hh == 0 else (dk + dkh, dv + dvh)
                dq_ref[:, cols] = _bf(jnp.where(first, dqs[0], dqs[1]) * ATT_HEAD_DIM ** -0.5)
                for b in range(nt):
                    slot = (i + b + 1) % nt
                    rows = slice(b * ATT_Q_TILE, (b + 1) * ATT_Q_TILE)
                    if b < ATT_BACK:
                        dk_acc[slot, :, cols] += dk[rows]
                        dv_acc[slot, :, cols] += dv[rows]
                    else:
                        dk_acc[slot, :, cols] = dk[rows]
                        dv_acc[slot, :, cols] = dv[rows]

        pl.when(i < ATT_BACK)(lambda: compute(True))
        pl.when((i >= ATT_BACK) & (i < nq))(lambda: compute(False))
        done = (i + 1) % nt
        dk_ref[...] = _bf(dk_acc[done])
        dv_ref[...] = _bf(dv_acc[done])

    tile = pl.BlockSpec((ATT_Q_TILE, ATT_WIDTH), lambda g, i: (jnp.minimum(i, nq - 1), g))
    late = pl.BlockSpec((ATT_Q_TILE, ATT_WIDTH), lambda g, i: (jnp.maximum(i - ATT_BACK, 0), g))
    ring = pltpu.VMEM((nt, ATT_Q_TILE, ATT_WIDTH), F32)
    return pl.pallas_call(
        body, name=name, grid=(ATT_GROUPS, nq + ATT_BACK),
        out_shape=[SDS((s, D_MODEL), BF16)] * 3 + [SDS((ATT_HEADS, ATT_Q_TILE, ATT_K_TILE), F32)],
        in_specs=_attn_specs(nq) + [tile, tile],
        out_specs=[tile, late, late, pl.BlockSpec((2 * ATT_PAIRS, ATT_Q_TILE, ATT_K_TILE), lambda g, i: (g, 0, 0))],
        scratch_shapes=[ring, ring],
        compiler_params=_params(("parallel", "arbitrary")),
    )(*([z] * (1 + 2 * nt)), bias, o, do)


FWD_GROUPS = (
    (("ab_w_in", 0),),
    (("ab_w_out", 0), ("w_ffn_in", 0)),
    (("w_ffn_out", 0),),
    (("c_w_qkv", 0),),
    (("c_w_out", 0), ("w_ffn_in", 1), ("w_ffn_out", 1), ("ab_w_in", 1)),
    (("ab_w_out", 1), ("w_ffn_in", 2), ("w_ffn_out", 2), ("c_w_qkv", 1)),
    (("c_w_out", 1), ("w_ffn_in", 3), ("w_ffn_out", 3)),
)


def _local_step(x, target, small, comm):
    s = x.shape[0]
    tabs = _retention_tables(s)
    saved, w = [], comm.weight
    for layer in range(DEPTH):
        i = layer // 2
        sv = {"x0": x}
        g_mix = small["mix_norm"][layer:layer + 1]
        if layer % 2 == 0:
            at = (x,) + tuple(tabs.values()) if layer == 0 else x
            sv["h1"], sv["z"] = _norm_mm(x, g_mix, w("ab_w_in", i, at), AB_IN_WIDTH, F32, False, "ab_in_fwd")
            gn = small["ab_gn_gain"][i:i + 1]
            sv["o_pre"], sv["states"], ret = _ret_fwd(sv["z"], tabs, gn, "ret_fwd")
            pool = _pool_fwd(sv["z"], small["ab_w_pool"][i], small["ab_pool_scale"][i:i + 1], "pool_fwd")
            sv["u"] = (ret, pool)
            x = _mm_res([ret, pool], w("ab_w_out", i, ret), x, "ab_out_fwd")
        else:
            sv["h1"], sv["z"] = _norm_mm(x, g_mix, w("c_w_qkv", i, x), 3 * D_MODEL // N_DEV, BF16, False, "qkv_fwd")
            rb = jnp.pad(small["c_rel_bias"][i], ((0, 0), (0, REL_PAD - N_REL)))
            sv["bias"] = _attn_bias(rb, "attn_bias")
            sv["o"] = _attn_fwd(sv["z"], sv["bias"], "attn_fwd")
            x = _mm_res([sv["o"]], w("c_w_out", i, sv["o"]), x, "c_out_fwd")
        sv["x1"] = x
        sv["h2"], sv["z1"], sv["a"] = _norm_mm(x, small["ffn_norm"][layer:layer + 1], w("w_ffn_in", layer, x),
                                               D_FF // N_DEV, BF16, True, "ffn_in_fwd")
        x = _mm_res([sv["a"]], w("w_ffn_out", layer, sv["a"]), x, "ffn_out_fwd")
        saved.append(sv)

    loss, dx, d_final = _final_loss(x, small["final_norm"][None, :], target, "final_loss")

    gs = {k: [None] * DEPTH for k in ("mix_norm", "ffn_norm")}
    for k in ("ab_gn_gain", "ab_w_pool", "ab_pool_scale", "c_rel_bias"):
        gs[k] = [None] * (DEPTH // 2)
    gs["final_norm"] = d_final[0]
    sent = None
    for layer in reversed(range(DEPTH)):
        i = layer // 2
        sv = saved[layer]
        dz1 = _mm_nt_rows(dx, w("w_ffn_out", layer), sv["z1"], "ffn_out_bwd")
        gw = {("w_ffn_out", layer): _mm_tn(sv["a"], dx, 1024, D_MODEL, True, (0, 2), "ffn_out_dw"),
              ("w_ffn_in", layer): _mm_tn(sv["h2"], dz1, D_MODEL, 1024, False, (1, 2), "ffn_in_dw",
                                          comm.after() if layer == 0 else None)}
        dx, dg = _mm_nt_normbwd(dz1, w("w_ffn_in", layer), sv["x1"], small["ffn_norm"][layer:layer + 1], dx,
                                "ffn_in_bwd", sent)
        gs["ffn_norm"][layer] = dg[0]
        sent = None
        g_mix = small["mix_norm"][layer:layer + 1]
        if layer % 2 == 0:
            du = _mm_nt_rows(dx, w("ab_w_out", i), None, "mix_out_bwd")
            gw["ab_w_out", i] = _mm_tn(sv["u"], dx, D_MODEL, D_MODEL, True, (0, N_DEV), "mix_out_dw")
            if layer == 0:
                sent, gw = comm.send(gw), {}
            gn = small["ab_gn_gain"][i:i + 1]
            dz, dgn = _ret_bwd(sv["z"], tabs, gn, sv["o_pre"], sv["states"], du, "ret_bwd")
            dz, dwp, dsc = _pool_bwd(sv["z"], small["ab_w_pool"][i], small["ab_pool_scale"][i:i + 1], du, dz, "pool_bwd")
            gs["ab_gn_gain"][i], gs["ab_w_pool"][i], gs["ab_pool_scale"][i] = dgn[0], dwp, dsc[0]
            gw["ab_w_in", i] = _to_shard_major(_mm_tn(sv["h1"], dz, D_MODEL, AB_IN_WIDTH // 2, False, (1, 1), "ab_in_dw",
                                                      comm.after()), pairs_split=True)
            dx, dg = _mm_nt_normbwd(dz, w("ab_w_in", i), sv["x0"], g_mix, dx, "ab_in_bwd", sent)
        else:
            do = _mm_nt_rows(dx, w("c_w_out", i), None, "mix_out_bwd")
            gw["c_w_out", i] = _mm_tn(sv["o"], dx, D_MODEL, D_MODEL, True, (0, N_DEV), "mix_out_dw")
            dq, dk, dv, dbias = _attn_bwd(sv["z"], sv["bias"], sv["o"], do, "attn_bwd")
            gs["c_rel_bias"][i] = _attn_dbias(dbias, "attn_dbias")[:, :N_REL]
            tiles = [_mm_tn(sv["h1"], part, D_MODEL, D_MODEL, False, (1, 1), "qkv_dw", after)
                     for part, after in ((dq, None), (dk, None), (dv, comm.after()))]
            gw["c_w_qkv", i] = _to_shard_major(jnp.concatenate(tiles, axis=0))
            dx, dg = _mm_nt_normbwd([dq, dk, dv], w("c_w_qkv", i), sv["x0"], g_mix, dx, "qkv_bwd", sent)
        gs["mix_norm"][layer] = dg[0]
        if layer > 0:
            sent = comm.send(gw)
    gsmall = {k: (jnp.stack(v) if isinstance(v, list) else v) for k, v in gs.items()}
    return loss, dx, gw, gsmall


BIG = ("w_ffn_in", "w_ffn_out", "ab_w_in", "ab_w_out", "c_w_qkv", "c_w_out")
SMALL = ("mix_norm", "ffn_norm", "ab_gn_gain", "ab_w_pool", "ab_pool_scale", "c_rel_bias", "final_norm")
N_PEERS = N_DEV - 1
FLIPS = [(fx, fy, fc) for fx in (0, 1) for fy in (0, 1) for fc in (0, 1)][1:]


def _peers():
    x, y, c = (lax.axis_index(a) for a in MESH_AXES)
    peers = []
    for fx, fy, fc in FLIPS:
        px, py, pc = (1 - x if fx else x), (1 - y if fy else y), (1 - c if fc else c)
        peers.append(((px, py, pc), 4 * px + 2 * py + pc))
    return 4 * x + 2 * y + c, peers


def _exchange(srcs, by_slot, name, collective_id):
    n = len(srcs)
    src_refs = [jax.new_ref(a, memory_space=pltpu.MemorySpace.HBM) for a in srcs]
    land_refs = [jax.empty_ref(SDS((N_DEV,) + (a.shape[1:] if slotted else a.shape), a.dtype),
                               memory_space=pltpu.MemorySpace.HBM) for a, slotted in zip(srcs, by_slot)]

    @pl.kernel(mesh=plsc.ScalarSubcoreMesh(axis_name="sequencer", num_cores=1), name=name,
               scratch_types=(pltpu.SemaphoreType.DMA((n * N_PEERS,)), pltpu.SemaphoreType.DMA((n * N_PEERS,)),
                              pltpu.SemaphoreType.DMA((n,))),
               compiler_params=pltpu.CompilerParams(collective_id=collective_id))
    def launch(send_sems, recv_sems, local_sems):
        me, peers = _peers()
        barrier = pltpu.get_barrier_semaphore()
        for pos, _ in peers:
            pl.semaphore_signal(barrier, inc=1, device_id=pos, device_id_type=pl.DeviceIdType.MESH)
        pl.semaphore_wait(barrier, N_PEERS)
        waits = []
        for k in range(n):
            own = pltpu.make_async_copy(src_refs[k].at[me] if by_slot[k] else src_refs[k], land_refs[k].at[me],
                                        local_sems.at[k])
            own.start()
            waits.append(own.wait)
            for rel, (pos, slot) in enumerate(peers):
                src = src_refs[k].at[slot] if by_slot[k] else src_refs[k]
                sems = dict(send_sem=send_sems.at[k * N_PEERS + rel], recv_sem=recv_sems.at[k * N_PEERS + rel],
                            device_id=pos, device_id_type=pl.DeviceIdType.MESH)
                send = pltpu.make_async_remote_copy(src_ref=src, dst_ref=land_refs[k].at[me], **sems)
                send.start()
                arrival = pltpu.make_async_remote_copy(src_ref=src, dst_ref=land_refs[k].at[slot], **sems)
                waits += [send.wait_send, arrival.wait_recv]
        for wait in waits:
            wait()

    launch()
    return [r[...] for r in land_refs]


def _cast_group(weights, keys, after, name):
    after = _follow(after)

    def body(*refs):
        n = len(keys)
        for i_ref, o_ref in zip(refs[:n], refs[n + len(after):]):
            o_ref[...] = _bf(i_ref[...])

    def layer_spec(shape, l):
        return pl.BlockSpec((None,) + shape[1:], lambda i: (l, 0, 0))

    whole = lambda shape: pl.BlockSpec(shape, lambda i: (0, 0))
    ins = [weights[k] for k, _ in keys]
    return pl.pallas_call(
        body, name=name, grid=(1,), out_shape=[SDS(w.shape[1:], BF16) for w in ins],
        in_specs=[layer_spec(w.shape, l) for w, (_, l) in zip(ins, keys)] + [FOLLOW] * len(after),
        out_specs=[whole(w.shape[1:]) for w in ins],
        compiler_params=_params(("arbitrary",)),
    )(*ins, *after)


def _to_shard_major(g, pairs_split=False):
    nj, ka, nb = g.shape
    full = jnp.transpose(g, (1, 0, 2)).reshape(ka, nj * nb)
    if pairs_split:
        full = _split_pairs(full, inverse=True)
    return jnp.transpose(full.reshape(ka, N_DEV, nj * nb // N_DEV), (1, 0, 2))


def _from_gathered(name, g):
    if name in ("w_ffn_out", "ab_w_out", "c_w_out"):
        return g.reshape(g.shape[0] * g.shape[1], g.shape[2])
    if name == "ab_w_in":
        return _split_pairs(jnp.transpose(g, (1, 0, 2)).reshape(1, g.shape[1], N_DEV * g.shape[2]))
    return g


class _Comm:
    def __init__(self, weights):
        self.weights_f32 = weights
        self.gathered = {}
        self.got = {}
        self.calls = 0
        self.ended = []
        self.opened = -1

    def _exchange(self, srcs, by_slot, name):
        self.calls += 1
        got = _exchange(srcs, by_slot, name, self.calls)
        self.ended = got[:1]
        return got

    def _gather(self, group, at):
        keys = FWD_GROUPS[group]
        shards = _cast_group(self.weights_f32, keys, self.ended + _follow(at), "cast_%d" % group)
        got = self._exchange(shards, [False] * len(keys), "gather_%d" % group)
        self.gathered.update((k, _from_gathered(k[0], arr)) for k, arr in zip(keys, got))

    def weight(self, name, layer, at=None):
        if (name, layer) not in self.gathered:
            self._gather(0, at[0] if isinstance(at, tuple) else at)
        group = next(g for g, keys in enumerate(FWD_GROUPS) if (name, layer) in keys)
        if group == self.opened + 1:
            self.opened = group
            if group + 1 < len(FWD_GROUPS):
                self._gather(group + 1, at)
        return self.gathered[name, layer]

    def after(self):
        return self.ended

    def send(self, grads, shared=None):
        shared = shared or {}
        keys = list(grads) + list(shared)
        srcs = list(grads.values()) + list(shared.values())
        got = self._exchange(srcs, [True] * len(grads) + [False] * len(shared), "scatter_%d" % self.calls)
        self.got.update(zip(keys, got))
        return list(grads.values())

    def received(self):
        return self.got


def _adamw_math(g, w, m, v):
    m2 = ADAM_B1 * m + (1.0 - ADAM_B1) * g
    v2 = ADAM_B2 * v + (1.0 - ADAM_B2) * jnp.square(g)
    m_hat = m2 / (1.0 - ADAM_B1 ** ADAM_STEP)
    v_hat = v2 / (1.0 - ADAM_B2 ** ADAM_STEP)
    delta = -ADAM_LR * (m_hat / (jnp.sqrt(v_hat) + ADAM_EPS) + ADAM_WD * w)
    return delta, m2, v2


def _adamw(recv, w, m, v, name):
    nl, r, c = w.shape
    tr = _tile(r, 256)

    def body(*refs):
        g_refs = refs[:nl]
        w_ref, m_ref, v_ref, go_ref, d_ref, mo_ref, vo_ref = refs[nl:]
        for l in range(nl):
            @pl.when(pl.program_id(0) == l)
            def _():
                g = g_refs[l][0].astype(F32)
                for p in range(1, N_DEV):
                    g = g + g_refs[l][p].astype(F32)
                go_ref[...] = g
                d_ref[...], mo_ref[...], vo_ref[...] = _adamw_math(g, w_ref[...], m_ref[...], v_ref[...])

    def recv_spec(l):
        return pl.BlockSpec((N_DEV, tr, c), lambda layer, i: (0, jnp.where(layer == l, i, 0), 0))

    blk = pl.BlockSpec((None, tr, c), lambda l, i: (l, i, 0))
    return pl.pallas_call(
        body, name=name, grid=(nl, r // tr), out_shape=[SDS(w.shape, F32)] * 4,
        in_specs=[recv_spec(l) for l in range(nl)] + [blk, blk, blk],
        out_specs=[blk] * 4,
        compiler_params=_params(("arbitrary", "arbitrary")),
    )(*recv, w, m, v)


def _adamw_small(recv, loss_parts, w, m, v, name):
    n = len(w)

    def total(ref):
        t = ref[0]
        for p in range(1, N_DEV):
            t = t + ref[p]
        return t

    def body(*refs):
        g_refs, loss_ref = refs[:n], refs[n]
        w_refs, m_refs, v_refs = refs[n + 1:2 * n + 1], refs[2 * n + 1:3 * n + 1], refs[3 * n + 1:4 * n + 1]
        outs = refs[4 * n + 1:]
        for i in range(n):
            g = total(g_refs[i])
            outs[4 * i][...] = g
            outs[4 * i + 1][...], outs[4 * i + 2][...], outs[4 * i + 3][...] = _adamw_math(
                g, w_refs[i][...], m_refs[i][...], v_refs[i][...])
        outs[4 * n][...] = total(loss_ref)

    out_shape = [SDS(p.shape, F32) for p in w for _ in range(4)] + [SDS(loss_parts.shape[1:], F32)]
    outs = pl.pallas_call(body, name=name, out_shape=out_shape,
                          compiler_params=_params(None))(*recv, loss_parts, *w, *m, *v)
    return [outs[4 * i:4 * i + 4] for i in range(n)], outs[-1]


def kernel(x, mix_norm, ffn_norm, w_ffn_in, w_ffn_out, ab_w_in, ab_gn_gain, ab_w_pool, ab_pool_scale, ab_w_out, c_w_qkv, c_rel_bias, c_w_out, final_norm, loss_target, m_mix_norm, m_ffn_norm, m_w_ffn_in, m_w_ffn_out, m_ab_w_in, m_ab_gn_gain, m_ab_w_pool, m_ab_pool_scale, m_ab_w_out, m_c_w_qkv, m_c_rel_bias, m_c_w_out, m_final_norm, v_mix_norm, v_ffn_norm, v_w_ffn_in, v_w_ffn_out, v_ab_w_in, v_ab_gn_gain, v_ab_w_pool, v_ab_pool_scale, v_ab_w_out, v_c_w_qkv, v_c_rel_bias, v_c_w_out, v_final_norm):
    args = dict(locals())
    weights = {k: args[k] for k in BIG + SMALL}
    moments_m = {k: args["m_" + k] for k in BIG + SMALL}
    moments_v = {k: args["v_" + k] for k in BIG + SMALL}

    small = {k: weights[k] for k in SMALL}
    rows = lambda a: a.reshape(1, -1) if a.ndim == 1 else a

    comm = _Comm(weights)
    loss, dx, last_grads, gsmall = _local_step(x[0], loss_target[0], small, comm)
    comm.send(last_grads, {**{k: rows(gsmall[k]) for k in SMALL}, "loss": loss})
    recv = comm.received()

    outs = {}
    for k in BIG:
        layers = [recv[k, l] for l in range(weights[k].shape[0])]
        outs[k] = _adamw(layers, weights[k], moments_m[k], moments_v[k], "adamw_" + k)
    updated, total = _adamw_small([recv[k] for k in SMALL], recv["loss"], [rows(small[k]) for k in SMALL],
                                  [rows(moments_m[k]) for k in SMALL], [rows(moments_v[k]) for k in SMALL], "adamw_small")
    for k, parts in zip(SMALL, updated):
        outs[k] = [p.reshape(small[k].shape) for p in parts]

    order = SMALL[:2] + BIG[:2] + ("ab_w_in", "ab_gn_gain", "ab_w_pool", "ab_pool_scale", "ab_w_out",
                                   "c_w_qkv", "c_rel_bias", "c_w_out", "final_norm")
    result = [total[0, 0], dx[None]]
    for part in range(4):
        result += [outs[k][part] for k in order]
    return tuple(result)
```

```python
import functools

import jax
import jax.numpy as jnp
from jax import lax
from jax.experimental import pallas as pl
from jax.experimental.pallas import tpu as pltpu
from jax.experimental.pallas import tpu_sc as plsc

F32 = jnp.float32
BF16 = jnp.bfloat16
SDS = jax.ShapeDtypeStruct
MESH_AXES = ("x", "y", "c")
N_DEV = 8

D_MODEL = 1024
DEPTH = 4
CHUNK = 64
D_FF = 4 * D_MODEL
RMS_EPS = 1e-6
RET_WIDTH = 512
RET_HEADS = 4
RET_HEAD_DIM = 128
RET_ROPE_BASE = 10000.0
GN_EPS = 1e-5
POOL_WIDTH = 512
POOL_WINDOWS = (2, 4, 8, 16)
POOL_HALO = 16
AB_IN_WIDTH = 4 * RET_WIDTH + POOL_WIDTH
ATT_HEADS = 16
ATT_HEAD_DIM = 64
LEFT_CHUNKS = 8
REL_CLIP = 128
N_REL = 2 * REL_CLIP + 1
NEG_INF = -1e30

ADAM_LR = 0.001
ADAM_B1 = 0.9
ADAM_B2 = 0.999
ADAM_EPS = 1e-08
ADAM_WD = 0.01
ADAM_STEP = 10

TOKEN_TILE = 512
ATT_Q_TILE = 256
ATT_BACK = LEFT_CHUNKS * CHUNK // ATT_Q_TILE
ATT_K_TILE = (ATT_BACK + 1) * ATT_Q_TILE
ATT_PAIRS = 4
ATT_DIAG = 1024
REL_PAD = 384
VMEM_LIMIT_MB = 56

NT = (((1,), (1,)), ((), ()))
TN = (((0,), (0,)), ((), ()))


def _params(semantics, **kw):
    return pltpu.CompilerParams(dimension_semantics=semantics,
                                vmem_limit_bytes=VMEM_LIMIT_MB * 2 ** 20, **kw)


def _dot(a, b, dims=None):
    if dims is None:
        return jnp.dot(a, b, preferred_element_type=F32)
    return lax.dot_general(a, b, dims, preferred_element_type=F32)


def _bf(v):
    return v.astype(BF16)


def _tile(n, t):
    return min(n, t)


FOLLOW = pl.BlockSpec(memory_space=pl.ANY)


def _follow(after):
    return [] if after is None else list(after) if isinstance(after, (list, tuple)) else [after]


MXU_WIDTH = 256


def _mxu_group(nj, tn):
    return 2 if tn % MXU_WIDTH and (2 * tn) % MXU_WIDTH == 0 and nj % 2 == 0 else 1


def _w_tiles(w_ref, j, group):
    return w_ref[j] if group == 1 else jnp.concatenate([w_ref[j + t] for t in range(group)], axis=1)


def _w_cols(w_ref, j, group, c, width):
    return w_ref[j, :, c:c + width] if group == 1 else _w_tiles(w_ref, j, group)


def _norm_mm(x, gain, w, tn, z_dtype, relu2, name, pre=None):
    s, d = x.shape
    nj = w.shape[0]
    tm = _tile(s, TOKEN_TILE)
    group = _mxu_group(nj, tn)
    pre_parts, w_pre = pre if pre else ([], None)
    widths = [p.shape[1] for p in pre_parts]

    def body(*refs):
        p_refs = refs[:len(pre_parts)]
        refs = refs[len(pre_parts):]
        if pre:
            wp_ref, x_ref, g_ref, w_ref, x1_ref, h_ref, z_ref, *a_ref = refs
            xv, off = x_ref[...], 0
            for p_ref, k in zip(p_refs, widths):
                xv = xv + _dot(p_ref[...], wp_ref[off:off + k, :])
                off += k
            x1_ref[...] = xv
        else:
            x_ref, g_ref, w_ref, h_ref, z_ref, *a_ref = refs
            xv = x_ref[...]
        r = lax.rsqrt(jnp.mean(xv * xv, axis=-1, keepdims=True) + RMS_EPS)
        h = _bf(xv * r * g_ref[...])
        h_ref[...] = h
        cw = tn if tn <= 512 else 512
        for j in range(0, nj, group):
            for c in range(0, tn, cw):
                z = _dot(h, _w_cols(w_ref, j, group, c, cw))
                cols = slice(j * tn + c, j * tn + c + group * cw)
                z_ref[:, cols] = z.astype(z_ref.dtype)
                if relu2:
                    a_ref[0][:, cols] = _bf(jnp.square(jnp.maximum(z, 0.0)))

    n = nj * tn
    rows = lambda width: pl.BlockSpec((tm, width), lambda i: (i, 0))
    out_shape = [SDS((s, d), F32)] * bool(pre) + [SDS((s, d), BF16), SDS((s, n), z_dtype)] + [SDS((s, n), BF16)] * relu2
    out_specs = [rows(d)] * bool(pre) + [rows(d), rows(n)] + [rows(n)] * relu2
    return pl.pallas_call(
        body, name=name, grid=(s // tm,), out_shape=out_shape,
        in_specs=[rows(k) for k in widths] + ([pl.BlockSpec(w_pre.shape, lambda i: (0, 0))] if pre else [])
        + [rows(d), pl.BlockSpec((1, d), lambda i: (0, 0)), pl.BlockSpec((nj, d, tn), lambda i: (0, 0, 0))],
        out_specs=out_specs,
        compiler_params=_params(("parallel",)),
    )(*pre_parts, *([w_pre] if pre else []), x, gain, w)


def _mm_res(parts, w, res, name):
    s, d = res.shape
    tm = _tile(s, TOKEN_TILE)
    widths = [p.shape[1] for p in parts]

    def body(*refs):
        a_refs = refs[:len(parts)]
        w_ref, res_ref, o_ref = refs[len(parts):]
        acc = res_ref[...]
        off = 0
        for a_ref, k in zip(a_refs, widths):
            acc = acc + _dot(a_ref[...], w_ref[off:off + k, :])
            off += k
        o_ref[...] = acc

    return pl.pallas_call(
        body, name=name, grid=(s // tm,), out_shape=SDS((s, d), F32),
        in_specs=[pl.BlockSpec((tm, k), lambda i: (i, 0)) for k in widths]
        + [pl.BlockSpec(w.shape, lambda i: (0, 0)), pl.BlockSpec((tm, d), lambda i: (i, 0))],
        out_specs=pl.BlockSpec((tm, d), lambda i: (i, 0)),
        compiler_params=_params(("parallel",)),
    )(*parts, w, res)


def _mm_nt_rows(dy, w, z, name):
    s, d = dy.shape
    k = w.shape[0]
    tm = _tile(s, TOKEN_TILE)
    tk = _tile(k, 1024)

    def body(dy_ref, w_ref, *rest):
        o_ref = rest[-1]
        dyb = _bf(dy_ref[...])
        for j in range(k // tk):
            cols = slice(j * tk, (j + 1) * tk)
            da = _dot(dyb, w_ref[cols, :], NT)
            if z is not None:
                da = da * (2.0 * jnp.maximum(rest[0][:, cols].astype(F32), 0.0))
            o_ref[:, cols] = _bf(da)

    in_specs = [pl.BlockSpec((tm, d), lambda i: (i, 0)), pl.BlockSpec((k, d), lambda i: (0, 0))]
    args = [dy, w]
    if z is not None:
        in_specs.append(pl.BlockSpec((tm, k), lambda i: (i, 0)))
        args.append(z)
    return pl.pallas_call(
        body, name=name, grid=(s // tm,), out_shape=SDS((s, k), BF16),
        in_specs=in_specs, out_specs=pl.BlockSpec((tm, k), lambda i: (i, 0)),
        compiler_params=_params(("parallel",)),
    )(*args)


def _w_range(w_ref, c0, c1):
    nc = w_ref.shape[2]
    pieces, c = [], c0
    while c < c1:
        j = c // nc
        hi = min(nc, c1 - j * nc)
        pieces.append(w_ref[j, :, c - j * nc:hi])
        c = j * nc + hi
    return pieces[0] if len(pieces) == 1 else jnp.concatenate(pieces, axis=1)


def _mm_nt_normbwd(dz, w, x, gain, dres, name, after=None, w_post=None):
    parts, after = list(dz) if isinstance(dz, (list, tuple)) else [dz], _follow(after)
    widths = [p.shape[1] for p in parts]
    s, d = x.shape
    tm = _tile(s, TOKEN_TILE)
    chunk = 2 * MXU_WIDTH
    halves = 2 if tm % 32 == 0 else 1

    def body(*refs):
        dz_refs = refs[:len(parts)]
        w_ref, x_ref, g_ref, dres_ref = refs[len(parts):len(parts) + 4]
        dx_ref, dg_ref = refs[-2 - has_post:][:2]

        @pl.when(pl.program_id(0) == 0)
        def _():
            dg_ref[...] = jnp.zeros_like(dg_ref)

        for half in range(halves):
            rows = slice(half * tm // halves, (half + 1) * tm // halves)
            dh, base = None, 0
            for dz_ref, width in zip(dz_refs, widths):
                for c in range(0, width, chunk):
                    term = _dot(dz_ref[rows, c:c + chunk], _w_range(w_ref, base + c, base + c + chunk), NT)
                    dh = term if dh is None else dh + term
                base += width
            xv = x_ref[rows, :]
            r = lax.rsqrt(jnp.mean(xv * xv, axis=-1, keepdims=True) + RMS_EPS)
            xn = xv * r
            dg_ref[...] += jnp.sum(dh * xn, axis=0, keepdims=True)
            dxh = dh * g_ref[...]
            dx = dres_ref[rows, :] + r * (dxh - xn * jnp.mean(dxh * xn, axis=-1, keepdims=True))
            dx_ref[rows, :] = dx
            if has_post:
                refs[-1][rows, :] = _bf(_dot(_bf(dx), refs[len(parts) + 4][...], NT))

    has_post = w_post is not None
    post_in = [pl.BlockSpec(w_post.shape, lambda i: (0, 0))] if has_post else []
    post_out = [pl.BlockSpec((tm, w_post.shape[0]), lambda i: (i, 0))] if has_post else []
    return pl.pallas_call(
        body, name=name, grid=(s // tm,),
        out_shape=[SDS((s, d), F32), SDS((1, d), F32)] + ([SDS((s, w_post.shape[0]), BF16)] if has_post else []),
        in_specs=[pl.BlockSpec((tm, width), lambda i: (i, 0)) for width in widths]
        + [pl.BlockSpec(w.shape, lambda i: (0, 0, 0)),
           pl.BlockSpec((tm, d), lambda i: (i, 0)),
           pl.BlockSpec((1, d), lambda i: (0, 0)),
           pl.BlockSpec((tm, d), lambda i: (i, 0))] + post_in + [FOLLOW] * len(after),
        out_specs=[pl.BlockSpec((tm, d), lambda i: (i, 0)), pl.BlockSpec((1, d), lambda i: (0, 0))] + post_out,
        compiler_params=_params(("arbitrary",)),
    )(*parts, w, x, gain, dres, *([w_post] if has_post else []), *after)


def _mm_tn(a, b, ka, nb, a_tiled, split, name, after=None):
    a_parts, after = list(a) if isinstance(a, (list, tuple)) else [a], _follow(after)
    s = a_parts[0].shape[0]
    tm = _tile(s, 4 * TOKEN_TILE)
    nm = s // tm
    nj = a_parts[0].shape[1] // ka if a_tiled and len(a_parts) == 1 else (1 if a_tiled else b.shape[1] // nb)
    axis, parts = split
    pr, pc = (ka // parts, nb) if axis == 0 else (ka, nb // parts)

    def body(*refs):
        a_refs, b_ref = refs[:len(a_parts)], refs[len(a_parts)]
        o_ref, acc = refs[-2:]
        m = pl.program_id(1)

        @pl.when(m == 0)
        def _():
            acc[...] = jnp.zeros_like(acc)

        av = a_refs[0][...] if len(a_refs) == 1 else jnp.concatenate([r[...] for r in a_refs], axis=1)
        acc[...] += _dot(_bf(av), _bf(b_ref[...]), TN)

        @pl.when(m == nm - 1)
        def _():
            for q in range(parts):
                piece = acc[q * pr:(q + 1) * pr, :] if axis == 0 else acc[:, q * pc:(q + 1) * pc]
                o_ref[q] = piece.astype(o_ref.dtype)

    return pl.pallas_call(
        body, name=name, grid=(nj, nm), out_shape=SDS((nj * parts, pr, pc), BF16),
        in_specs=([pl.BlockSpec((tm, ka), (lambda j, m: (m, j)) if a_tiled else (lambda j, m: (m, 0)))]
                  if len(a_parts) == 1 else [pl.BlockSpec((tm, p.shape[1]), lambda j, m: (m, 0)) for p in a_parts])
        + [pl.BlockSpec((tm, nb), (lambda j, m: (m, 0)) if a_tiled else (lambda j, m: (m, j)))]
        + [FOLLOW] * len(after),
        out_specs=pl.BlockSpec((parts, pr, pc), lambda j, m: (j, 0, 0)),
        scratch_shapes=[pltpu.VMEM((ka, nb), F32)],
        compiler_params=_params(("parallel", "arbitrary")),
    )(*a_parts, b, *after)


def _final_loss(x, gain, target, name):
    s, d = x.shape
    tm = _tile(s, TOKEN_TILE)

    def body(x_ref, g_ref, t_ref, loss_ref, dx_ref, dg_ref):
        @pl.when(pl.program_id(0) == 0)
        def _():
            loss_ref[...] = jnp.zeros_like(loss_ref)
            dg_ref[...] = jnp.zeros_like(dg_ref)

        xv = x_ref[...]
        r = lax.rsqrt(jnp.mean(xv * xv, axis=-1, keepdims=True) + RMS_EPS)
        xn = xv * r
        err = xn * g_ref[...] - t_ref[...]
        loss_ref[...] += (0.5 / d) * jnp.sum(err * err)
        dy = err * (1.0 / d)
        dg_ref[...] += jnp.sum(dy * xn, axis=0, keepdims=True)
        dxh = dy * g_ref[...]
        dx_ref[...] = r * (dxh - xn * jnp.mean(dxh * xn, axis=-1, keepdims=True))

    return pl.pallas_call(
        body, name=name, grid=(s // tm,),
        out_shape=[SDS((8, 128), F32), SDS((s, d), F32), SDS((1, d), F32)],
        in_specs=[pl.BlockSpec((tm, d), lambda i: (i, 0)), pl.BlockSpec((1, d), lambda i: (0, 0)),
                  pl.BlockSpec((tm, d), lambda i: (i, 0))],
        out_specs=[pl.BlockSpec((8, 128), lambda i: (0, 0)), pl.BlockSpec((tm, d), lambda i: (i, 0)),
                   pl.BlockSpec((1, d), lambda i: (0, 0))],
        compiler_params=_params(("arbitrary",)),
    )(x, gain, target)


def _retention_tables(s):
    half = RET_HEAD_DIM // 2
    inv_freq = 1.0 / (RET_ROPE_BASE ** jnp.linspace(0.0, 1.0, half, dtype=F32))
    ang = jnp.arange(s, dtype=F32)[:, None] * inv_freq[None, :]
    cos, sin = jnp.cos(ang), jnp.sin(ang)
    cos_e = jnp.concatenate([cos, cos], axis=-1)
    sin_s = jnp.concatenate([-sin, sin], axis=-1)
    log_g = jnp.log1p(-jnp.power(2.0, -5.0 - jnp.arange(RET_HEADS, dtype=F32)))
    pos = jnp.arange(CHUNK, dtype=F32)
    dmat = jnp.exp(jnp.abs(pos[:, None] - pos[None, :])[None] * log_g[:, None, None])
    qdec = jnp.exp((pos[None, :] + 1.0) * log_g[:, None])
    kdec = jnp.exp((CHUNK - 1.0 - pos[None, :]) * log_g[:, None])
    lam = jnp.exp(CHUNK * log_g)
    wide = (RET_HEADS, CHUNK, RET_HEAD_DIM)
    return dict(cos=cos_e, sin=sin_s, dmat=dmat,
                qdec=jnp.broadcast_to(qdec[:, :, None], wide),
                kdec=jnp.broadcast_to(kdec[:, :, None], wide),
                lam=jnp.broadcast_to(lam[:, None, None], (RET_HEADS, RET_HEAD_DIM, RET_HEAD_DIM)))


def _swap_pairs(t):
    return pltpu.roll(t, RET_HEAD_DIM // 2, 1)


def _split_pairs(w, inverse=False):
    lead, nqk = w.shape[:-1], 2 * RET_WIDTH
    shape = (2 * RET_HEADS, 2, RET_HEAD_DIM // 2) if inverse else (2 * RET_HEADS, RET_HEAD_DIM // 2, 2)
    qk = jnp.swapaxes(w[..., :nqk].reshape(lead + shape), -1, -2).reshape(lead + (nqk,))
    return jnp.concatenate([qk, w[..., nqk:]], axis=-1)


def _head(h):
    return slice(h * RET_HEAD_DIM, (h + 1) * RET_HEAD_DIM)


def _ret_common_specs(tb, blk):
    zs = [pl.BlockSpec((tb, RET_WIDTH), functools.partial(lambda j, i: (blk(i), j), j)) for j in range(4)]
    tabs = [pl.BlockSpec((tb, RET_HEAD_DIM), lambda i: (blk(i), 0))] * 2
    consts = [pl.BlockSpec((1, RET_WIDTH), lambda i: (0, 0)),
              pl.BlockSpec((RET_HEADS, CHUNK, CHUNK), lambda i: (0, 0, 0)),
              pl.BlockSpec((RET_HEADS, CHUNK, RET_HEAD_DIM), lambda i: (0, 0, 0)),
              pl.BlockSpec((RET_HEADS, CHUNK, RET_HEAD_DIM), lambda i: (0, 0, 0)),
              pl.BlockSpec((RET_HEADS, RET_HEAD_DIM, RET_HEAD_DIM), lambda i: (0, 0, 0))]
    return zs + tabs + consts


def _ret_fwd(z, tabs, gn_gain, name):
    s = z.shape[0]
    tb = _tile(s, TOKEN_TILE)
    ncb = tb // CHUNK
    scale = RET_HEAD_DIM ** -0.5

    def body(q_ref, k_ref, v_ref, g_ref, cos_ref, sin_ref, gain_ref, dm_ref, qd_ref, kd_ref, lam_ref,
             o_ref, st_ref, ret_ref, s_scr, qr_scr, kr_scr):
        @pl.when(pl.program_id(0) == 0)
        def _():
            s_scr[...] = jnp.zeros_like(s_scr)

        cosv, sinv = cos_ref[...], sin_ref[...]
        for h in range(RET_HEADS):
            qh, kh = q_ref[:, _head(h)], k_ref[:, _head(h)]
            qr_scr[:, _head(h)] = qh * cosv + _swap_pairs(qh) * sinv
            kr_scr[:, _head(h)] = (kh * cosv + _swap_pairs(kh) * sinv) * scale

        def chunk(c, carry):
            rows = pl.ds(pl.multiple_of(c * CHUNK, CHUNK), CHUNK)
            for h in range(RET_HEADS):
                qc, kc, vc = qr_scr[rows, _head(h)], kr_scr[rows, _head(h)], v_ref[rows, _head(h)]
                a = _dot(_bf(qc), _bf(kc), NT) * dm_ref[h]
                st = s_scr[h]
                st_ref[c, h] = st
                o_ref[rows, _head(h)] = _dot(_bf(a), _bf(vc)) + _dot(_bf(qc * qd_ref[h]), _bf(st))
                s_scr[h] = st * lam_ref[h] + _dot(_bf(kc * kd_ref[h]), _bf(vc), TN)
            return carry

        lax.fori_loop(0, ncb, chunk, 0, unroll=4)
        for h in range(RET_HEADS):
            o = o_ref[:, _head(h)]
            mu = jnp.mean(o, axis=-1, keepdims=True)
            oc = o - mu
            y = oc * lax.rsqrt(jnp.mean(oc * oc, axis=-1, keepdims=True) + GN_EPS) * gain_ref[:, _head(h)]
            g = g_ref[:, _head(h)]
            ret_ref[:, _head(h)] = _bf(g / (1.0 + jnp.exp(-g)) * y)

    nc = s // CHUNK
    return pl.pallas_call(
        body, name=name, grid=(s // tb,),
        out_shape=[SDS((s, RET_WIDTH), F32), SDS((nc, RET_HEADS, RET_HEAD_DIM, RET_HEAD_DIM), F32),
                   SDS((s, RET_WIDTH), BF16)],
        in_specs=_ret_common_specs(tb, lambda i: i),
        out_specs=[pl.BlockSpec((tb, RET_WIDTH), lambda i: (i, 0)),
                   pl.BlockSpec((ncb, RET_HEADS, RET_HEAD_DIM, RET_HEAD_DIM), lambda i: (i, 0, 0, 0)),
                   pl.BlockSpec((tb, RET_WIDTH), lambda i: (i, 0))],
        scratch_shapes=[pltpu.VMEM((RET_HEADS, RET_HEAD_DIM, RET_HEAD_DIM), F32),
                        pltpu.VMEM((tb, RET_WIDTH), F32), pltpu.VMEM((tb, RET_WIDTH), F32)],
        compiler_params=_params(("arbitrary",)),
    )(z, z, z, z, tabs["cos"], tabs["sin"], gn_gain, tabs["dmat"], tabs["qdec"], tabs["kdec"], tabs["lam"])


def _ret_bwd(z, tabs, gn_gain, o_pre, states, du, name):
    s = z.shape[0]
    tb = _tile(s, TOKEN_TILE)
    ncb = tb // CHUNK
    nblk = s // tb
    scale = RET_HEAD_DIM ** -0.5
    rev = lambda i: nblk - 1 - i

    def body(q_ref, k_ref, v_ref, g_ref, cos_ref, sin_ref, gain_ref, dm_ref, qd_ref, kd_ref, lam_ref,
             o_ref, st_ref, dret_ref, dz_ref, dgain_ref, g_scr, qr_scr, kr_scr, do_scr, dq_scr, dk_scr):
        @pl.when(pl.program_id(0) == 0)
        def _():
            g_scr[...] = jnp.zeros_like(g_scr)
            dgain_ref[...] = jnp.zeros_like(dgain_ref)

        cosv, sinv = cos_ref[...], sin_ref[...]
        for h in range(RET_HEADS):
            hs = _head(h)
            qh, kh = q_ref[:, hs], k_ref[:, hs]
            qr_scr[:, hs] = qh * cosv + _swap_pairs(qh) * sinv
            kr_scr[:, hs] = (kh * cosv + _swap_pairs(kh) * sinv) * scale
            o = o_ref[:, hs]
            mu = jnp.mean(o, axis=-1, keepdims=True)
            oc = o - mu
            rstd = lax.rsqrt(jnp.mean(oc * oc, axis=-1, keepdims=True) + GN_EPS)
            yh = oc * rstd
            gain = gain_ref[:, hs]
            g = g_ref[:, hs]
            sg = 1.0 / (1.0 + jnp.exp(-g))
            dret = dret_ref[:, hs].astype(F32)
            dy = dret * (g * sg)
            dz_ref[:, 3 * RET_WIDTH + h * RET_HEAD_DIM:3 * RET_WIDTH + (h + 1) * RET_HEAD_DIM] = _bf(
                dret * (yh * gain) * (sg * (1.0 + g * (1.0 - sg))))
            dgain_ref[:, hs] += jnp.sum(dy * yh, axis=0, keepdims=True)
            dyh = dy * gain
            do_scr[:, hs] = rstd * (dyh - jnp.mean(dyh, axis=-1, keepdims=True)
                                    - yh * jnp.mean(dyh * yh, axis=-1, keepdims=True))

        def chunk(cc, carry):
            c = ncb - 1 - cc
            rows = pl.ds(pl.multiple_of(c * CHUNK, CHUNK), CHUNK)
            for h in range(RET_HEADS):
                hs = _head(h)
                qc, kc, vc, doc = _bf(qr_scr[rows, hs]), _bf(kr_scr[rows, hs]), _bf(v_ref[rows, hs]), _bf(do_scr[rows, hs])
                qdc, kdc = qd_ref[h], kd_ref[h]
                st, gs = _bf(st_ref[c, h]), g_scr[h]
                gsb = _bf(gs)
                dm = dm_ref[h]
                p = _bf(_dot(qc, kc, NT) * dm)
                da = _bf(_dot(doc, vc, NT) * dm)
                kt = _bf(kr_scr[rows, hs] * kdc)
                qt = _bf(qr_scr[rows, hs] * qdc)
                dz_ref[rows, 2 * RET_WIDTH + h * RET_HEAD_DIM:2 * RET_WIDTH + (h + 1) * RET_HEAD_DIM] = _bf(
                    _dot(p, doc, TN) + _dot(kt, gsb))
                dq_scr[rows, hs] = _dot(da, kc) + _dot(doc, st, NT) * qdc
                dk_scr[rows, hs] = _dot(da, qc, TN) + _dot(vc, gsb, NT) * kdc
                g_scr[h] = gs * lam_ref[h] + _dot(qt, doc, TN)
            return carry

        lax.fori_loop(0, ncb, chunk, 0, unroll=4)
        for h in range(RET_HEADS):
            hs = _head(h)
            dq, dk = dq_scr[:, hs], dk_scr[:, hs]
            dz_ref[:, h * RET_HEAD_DIM:(h + 1) * RET_HEAD_DIM] = _bf(dq * cosv - _swap_pairs(dq) * sinv)
            dz_ref[:, RET_WIDTH + h * RET_HEAD_DIM:RET_WIDTH + (h + 1) * RET_HEAD_DIM] = _bf(
                (dk * cosv - _swap_pairs(dk) * sinv) * scale)

    return pl.pallas_call(
        body, name=name, grid=(nblk,),
        out_shape=[SDS((s, AB_IN_WIDTH), BF16), SDS((1, RET_WIDTH), F32)],
        in_specs=_ret_common_specs(tb, rev)
        + [pl.BlockSpec((tb, RET_WIDTH), lambda i: (rev(i), 0)),
           pl.BlockSpec((ncb, RET_HEADS, RET_HEAD_DIM, RET_HEAD_DIM), lambda i: (rev(i), 0, 0, 0)),
           pl.BlockSpec((tb, RET_WIDTH), lambda i: (rev(i), 0))],
        out_specs=[pl.BlockSpec((tb, 4 * RET_WIDTH), lambda i: (rev(i), 0)),
                   pl.BlockSpec((1, RET_WIDTH), lambda i: (0, 0))],
        scratch_shapes=[pltpu.VMEM((RET_HEADS, RET_HEAD_DIM, RET_HEAD_DIM), F32)]
        + [pltpu.VMEM((tb, RET_WIDTH), F32)] * 5,
        compiler_params=_params(("arbitrary",)),
    )(z, z, z, z, tabs["cos"], tabs["sin"], gn_gain, tabs["dmat"], tabs["qdec"], tabs["kdec"], tabs["lam"],
      o_pre, states, du)


POOL_COL = 4 * RET_WIDTH // POOL_WIDTH


def _pooled(cur, prev, t0):
    tm = cur.shape[0]
    xx = jnp.concatenate([prev, cur], axis=0)
    sums = {1: xx}
    w = 1
    while w < POOL_WINDOWS[-1]:
        sums[2 * w] = sums[w] + pltpu.roll(sums[w], w, 0)
        w *= 2
    t = t0 + lax.broadcasted_iota(jnp.int32, (tm, 128), 0)
    outs = []
    for gi, w in enumerate(POOL_WINDOWS):
        cols = slice(gi * 128, (gi + 1) * 128)
        cnt = jnp.minimum(t + 1, w).astype(F32)
        outs.append(sums[w][POOL_HALO:, cols] / cnt - cur[:, cols])
    return outs


def _pool_fwd(z, w_pool, scale, name):
    s = z.shape[0]
    tm = _tile(s, TOKEN_TILE)
    hb = tm // POOL_HALO

    def body(p_ref, prev_ref, w_ref, sc_ref, o_ref):
        i = pl.program_id(0)
        prev = jnp.where(i > 0, prev_ref[...], 0.0)
        pooled = _pooled(p_ref[...], prev, i * tm)
        for gi in range(len(POOL_WINDOWS)):
            cols = slice(gi * 128, (gi + 1) * 128)
            o_ref[:, cols] = _bf(_dot(_bf(pooled[gi]), _bf(w_ref[gi])) * sc_ref[:, cols])

    return pl.pallas_call(
        body, name=name, grid=(s // tm,), out_shape=SDS((s, POOL_WIDTH), BF16),
        in_specs=[pl.BlockSpec((tm, POOL_WIDTH), lambda i: (i, POOL_COL)),
                  pl.BlockSpec((POOL_HALO, POOL_WIDTH), lambda i: (jnp.maximum(i * hb - 1, 0), POOL_COL)),
                  pl.BlockSpec(w_pool.shape, lambda i: (0, 0, 0)),
                  pl.BlockSpec((1, POOL_WIDTH), lambda i: (0, 0))],
        out_specs=pl.BlockSpec((tm, POOL_WIDTH), lambda i: (i, 0)),
        compiler_params=_params(("parallel",)),
    )(z, z, w_pool, scale)


def _pool_bwd(z, w_pool, scale, du, dz, name):
    s = z.shape[0]
    tm = _tile(s, TOKEN_TILE)
    hb = tm // POOL_HALO
    nblk = s // tm
    last_halo = s // POOL_HALO - 1

    def body(p_ref, prev_ref, w_ref, sc_ref, do_ref, don_ref, dz_ref, dp_ref, dw_ref, dsc_ref):
        i = pl.program_id(0)

        @pl.when(i == 0)
        def _():
            dw_ref[...] = jnp.zeros_like(dw_ref)
            dsc_ref[...] = jnp.zeros_like(dsc_ref)

        prev = jnp.where(i > 0, prev_ref[...], 0.0)
        pooled = _pooled(p_ref[...], prev, i * tm)
        dout = do_ref[...].astype(F32)
        dout_next = jnp.where(i < nblk - 1, don_ref[...].astype(F32), 0.0)
        sc = sc_ref[...]
        dmix = jnp.concatenate([dout * sc, dout_next * sc], axis=0)
        n = tm + POOL_HALO
        t = i * tm + lax.broadcasted_iota(jnp.int32, (n, 128), 0)
        for gi, w in enumerate(POOL_WINDOWS):
            cols = slice(gi * 128, (gi + 1) * 128)
            wg = _bf(w_ref[gi])
            pg = _bf(pooled[gi])
            dsc_ref[:, cols] += jnp.sum(dout[:, cols] * _dot(pg, wg), axis=0, keepdims=True)
            dw_ref[gi] += _dot(pg, _bf(dmix[:tm, cols]), TN)
            dpool = _dot(_bf(dmix[:, cols]), wg, NT)
            acc = dpool / jnp.minimum(t + 1, w).astype(F32)
            step = 1
            while step < w:
                acc = acc + pltpu.roll(acc, n - step, 0)
                step *= 2
            dp_ref[:, cols] = _bf(acc[:tm] - dpool[:tm])

    return pl.pallas_call(
        body, name=name, grid=(nblk,),
        out_shape=[SDS(dz.shape, BF16), SDS(w_pool.shape, F32), SDS((1, POOL_WIDTH), F32)],
        in_specs=[pl.BlockSpec((tm, POOL_WIDTH), lambda i: (i, POOL_COL)),
                  pl.BlockSpec((POOL_HALO, POOL_WIDTH), lambda i: (jnp.maximum(i * hb - 1, 0), POOL_COL)),
                  pl.BlockSpec(w_pool.shape, lambda i: (0, 0, 0)),
                  pl.BlockSpec((1, POOL_WIDTH), lambda i: (0, 0)),
                  pl.BlockSpec((tm, POOL_WIDTH), lambda i: (i, 1)),
                  pl.BlockSpec((POOL_HALO, POOL_WIDTH), lambda i: (jnp.minimum((i + 1) * hb, last_halo), 1)),
                  pl.BlockSpec(memory_space=pl.ANY)],
        out_specs=[pl.BlockSpec((tm, POOL_WIDTH), lambda i: (i, POOL_COL)),
                   pl.BlockSpec(w_pool.shape, lambda i: (0, 0, 0)),
                   pl.BlockSpec((1, POOL_WIDTH), lambda i: (0, 0))],
        input_output_aliases={6: 0},
        compiler_params=_params(("arbitrary",)),
    )(z, z, w_pool, scale, du, du, dz)


def _rel_onehot():
    r = lax.broadcasted_iota(jnp.int32, (REL_PAD, ATT_DIAG), 0)
    c = lax.broadcasted_iota(jnp.int32, (REL_PAD, ATT_DIAG), 1)
    rel = jnp.where(c < ATT_K_TILE, jnp.clip(LEFT_CHUNKS * CHUNK - c, -REL_CLIP, REL_CLIP) + REL_CLIP,
                    2 * REL_CLIP)
    return (rel == r).astype(BF16)


def _split3(v):
    hi = _bf(v)
    r1 = v - hi.astype(F32)
    mid = _bf(r1)
    return hi, mid, _bf(r1 - mid.astype(F32))


def _skew(v, sign):
    row = lax.broadcasted_iota(jnp.int32, v.shape, 0)
    bit = 1
    while bit < ATT_Q_TILE:
        shift = bit if sign > 0 else ATT_DIAG - bit
        v = jnp.where((row & bit) != 0, pltpu.roll(v, shift, 1), v)
        bit *= 2
    return v


def _attn_bias(rel_bias, name):
    def body(t_ref, o_ref):
        oh = _rel_onehot()
        base = sum(_dot(part, oh) for part in _split3(t_ref[0]))
        full = _skew(jnp.broadcast_to(base[0:1], (ATT_Q_TILE, ATT_DIAG)), +1)[:, :ATT_K_TILE]
        qc = lax.broadcasted_iota(jnp.int32, full.shape, 0) // CHUNK
        kc = lax.broadcasted_iota(jnp.int32, full.shape, 1) // CHUNK
        o_ref[0] = jnp.where((kc >= qc) & (kc <= qc + LEFT_CHUNKS), full, NEG_INF)

    t8 = jnp.broadcast_to(rel_bias[:, None, :], (ATT_HEADS, 8, REL_PAD))
    return pl.pallas_call(
        body, name=name, grid=(ATT_HEADS,), out_shape=SDS((ATT_HEADS, ATT_Q_TILE, ATT_K_TILE), F32),
        in_specs=[pl.BlockSpec((1, 8, REL_PAD), lambda h: (h, 0, 0))],
        out_specs=pl.BlockSpec((1, ATT_Q_TILE, ATT_K_TILE), lambda h: (h, 0, 0)),
        compiler_params=_params(("parallel",)),
    )(t8)


def _attn_dbias(dbias, name):
    def body(d_ref, o_ref):
        pad = jnp.zeros((ATT_Q_TILE, ATT_DIAG - ATT_K_TILE), F32)
        diag = _skew(jnp.concatenate([d_ref[0], pad], axis=1), -1)
        col = jnp.sum(diag, axis=0, keepdims=True)
        oh = _rel_onehot()
        col8 = jnp.broadcast_to(col, (8, ATT_DIAG))
        o_ref[0] = sum(_dot(part, oh, NT) for part in _split3(col8))

    out = pl.pallas_call(
        body, name=name, grid=(ATT_HEADS,), out_shape=SDS((ATT_HEADS, 8, REL_PAD), F32),
        in_specs=[pl.BlockSpec((1, ATT_Q_TILE, ATT_K_TILE), lambda h: (h, 0, 0))],
        out_specs=pl.BlockSpec((1, 8, REL_PAD), lambda h: (h, 0, 0)),
        compiler_params=_params(("parallel",)),
    )(dbias)
    return out[:, 0, :]


ATT_WIDTH = 128 * ATT_PAIRS
ATT_GROUPS = D_MODEL // ATT_WIDTH


def _attn_specs(nq):
    def tile(off, back):
        return pl.BlockSpec((ATT_Q_TILE, ATT_WIDTH),
                            lambda g, i: (jnp.maximum(jnp.minimum(i, nq - 1) - back, 0), off + g))

    backs = [ATT_BACK - b for b in range(ATT_BACK + 1)]
    return ([tile(0, 0)] + [tile(ATT_GROUPS, b) for b in backs] + [tile(2 * ATT_GROUPS, b) for b in backs]
            + [pl.BlockSpec((2 * ATT_PAIRS, ATT_Q_TILE, ATT_K_TILE), lambda g, i: (g, 0, 0))])


def _attn_weights(qh, k2, bias, i, masked):
    sc = _dot(qh, k2, NT) + bias
    if masked:
        kpos = (i - ATT_BACK) * ATT_Q_TILE + lax.broadcasted_iota(jnp.int32, sc.shape, 1)
        sc = jnp.where(kpos >= 0, sc, NEG_INF)
    e = jnp.exp(sc - jnp.max(sc, axis=-1, keepdims=True))
    return e, 1.0 / jnp.sum(e, axis=-1, keepdims=True)


def _first_head():
    return lax.broadcasted_iota(jnp.int32, (ATT_Q_TILE, 128), 1) < ATT_HEAD_DIM


def _pair_operands(q_ref, k_refs, v_refs, pp):
    cols = slice(pp * 128, (pp + 1) * 128)
    q2 = q_ref[:, cols] * ATT_HEAD_DIM ** -0.5
    k2 = jnp.concatenate([r[:, cols] for r in k_refs], axis=0)
    v2 = jnp.concatenate([r[:, cols] for r in v_refs], axis=0)
    return cols, q2, k2, v2


def _attn_fwd(z, bias, name):
    s = z.shape[0]
    nq = s // ATT_Q_TILE
    nt = ATT_BACK + 1

    def body(q_ref, *rest):
        k_refs, v_refs, (b_ref, o_ref) = rest[:nt], rest[nt:2 * nt], rest[2 * nt:]
        i = pl.program_id(1)
        first = _first_head()

        def compute(masked):
            for pp in range(ATT_PAIRS):
                cols, q2, k2, v2 = _pair_operands(q_ref, k_refs, v_refs, pp)
                outs = []
                for hh in range(2):
                    qh = jnp.where(first if hh == 0 else ~first, q2, 0)
                    e, inv = _attn_weights(qh, k2, b_ref[2 * pp + hh], i, masked)
                    outs.append(_dot(_bf(e), v2) * inv)
                o_ref[:, cols] = _bf(jnp.where(first, outs[0], outs[1]))

        pl.when(i < ATT_BACK)(lambda: compute(True))
        pl.when(i >= ATT_BACK)(lambda: compute(False))

    return pl.pallas_call(
        body, name=name, grid=(ATT_GROUPS, nq), out_shape=SDS((s, D_MODEL), BF16),
        in_specs=_attn_specs(nq),
        out_specs=pl.BlockSpec((ATT_Q_TILE, ATT_WIDTH), lambda g, i: (i, g)),
        compiler_params=_params(("parallel", "parallel")),
    )(*([z] * (1 + 2 * nt)), bias)


def _attn_bwd(z, bias, o, do, name):
    s = z.shape[0]
    nq = s // ATT_Q_TILE
    nt = ATT_BACK + 1

    def body(q_ref, *rest):
        k_refs, v_refs = rest[:nt], rest[nt:2 * nt]
        b_ref, o_ref, do_ref, dq_ref, dk_ref, dv_ref, db_ref, dk_acc, dv_acc = rest[2 * nt:]
        i = pl.program_id(1)
        first = _first_head()

        @pl.when(i == 0)
        def _():
            db_ref[...] = jnp.zeros_like(db_ref)
            dk_acc[...] = jnp.zeros_like(dk_acc)
            dv_acc[...] = jnp.zeros_like(dv_acc)

        def compute(masked):
            for pp in range(ATT_PAIRS):
                cols, q2, k2, v2 = _pair_operands(q_ref, k_refs, v_refs, pp)
                do2 = do_ref[:, cols].astype(F32)
                prod = do2 * o_ref[:, cols].astype(F32)
                dqs, dk, dv = [], None, None
                for hh in range(2):
                    mine = first if hh == 0 else ~first
                    qh = jnp.where(mine, q2, 0)
                    e, inv = _attn_weights(qh, k2, b_ref[2 * pp + hh], i, masked)
                    delta = jnp.sum(jnp.where(mine, prod, 0.0), axis=-1, keepdims=True) * inv
                    doh = _bf(jnp.where(mine, do2 * inv, 0.0))
                    ds = e * (_dot(doh, v2, NT) - delta)
                    db_ref[2 * pp + hh] += ds
                    dsb = _bf(ds)
                    dqs.append(_dot(dsb, k2))
                    dkh, dvh = _dot(dsb, qh, TN), _dot(_bf(e), doh, TN)
                    dk, dv = (dkh, dvh) if hh == 0 else (dk + dkh, dv + dvh)
                dq_ref[:, cols] = _bf(jnp.where(first, dqs[0], dqs[1]) * ATT_HEAD_DIM ** -0.5)
                for b in range(nt):
                    slot = (i + b + 1) % nt
                    rows = slice(b * ATT_Q_TILE, (b + 1) * ATT_Q_TILE)
                    if b < ATT_BACK:
                        dk_acc[slot, :, cols] += dk[rows]
                        dv_acc[slot, :, cols] += dv[rows]
                    else:
                        dk_acc[slot, :, cols] = dk[rows]
                        dv_acc[slot, :, cols] = dv[rows]

        pl.when(i < ATT_BACK)(lambda: compute(True))
        pl.when((i >= ATT_BACK) & (i < nq))(lambda: compute(False))
        done = (i + 1) % nt
        dk_ref[...] = _bf(dk_acc[done])
        dv_ref[...] = _bf(dv_acc[done])

    tile = pl.BlockSpec((ATT_Q_TILE, ATT_WIDTH), lambda g, i: (jnp.minimum(i, nq - 1), g))
    late = pl.BlockSpec((ATT_Q_TILE, ATT_WIDTH), lambda g, i: (jnp.maximum(i - ATT_BACK, 0), g))
    ring = pltpu.VMEM((nt, ATT_Q_TILE, ATT_WIDTH), F32)
    return pl.pallas_call(
        body, name=name, grid=(ATT_GROUPS, nq + ATT_BACK),
        out_shape=[SDS((s, D_MODEL), BF16)] * 3 + [SDS((ATT_HEADS, ATT_Q_TILE, ATT_K_TILE), F32)],
        in_specs=_attn_specs(nq) + [tile, tile],
        out_specs=[tile, late, late, pl.BlockSpec((2 * ATT_PAIRS, ATT_Q_TILE, ATT_K_TILE), lambda g, i: (g, 0, 0))],
        scratch_shapes=[ring, ring],
        compiler_params=_params(("parallel", "arbitrary")),
    )(*([z] * (1 + 2 * nt)), bias, o, do)


FWD_GROUPS = (
    (("ab_w_in", 0),),
    (("ab_w_out", 0), ("w_ffn_in", 0)),
    (("w_ffn_out", 0),),
    (("c_w_qkv", 0),),
    (("c_w_out", 0), ("w_ffn_in", 1), ("w_ffn_out", 1), ("ab_w_in", 1)),
    (("ab_w_out", 1), ("w_ffn_in", 2), ("w_ffn_out", 2), ("c_w_qkv", 1)),
    (("c_w_out", 1), ("w_ffn_in", 3), ("w_ffn_out", 3)),
)


def _local_step(x, target, small, comm):
    s = x.shape[0]
    tabs = _retention_tables(s)
    saved, w = [], comm.weight
    for layer in range(DEPTH):
        i = layer // 2
        sv = {"x0": x}
        g_mix = small["mix_norm"][layer:layer + 1]
        if layer % 2 == 0:
            at = (x,) + tuple(tabs.values()) if layer == 0 else x
            sv["h1"], sv["z"] = _norm_mm(x, g_mix, w("ab_w_in", i, at), AB_IN_WIDTH, F32, False, "ab_in_fwd")
            gn = small["ab_gn_gain"][i:i + 1]
            sv["o_pre"], sv["states"], ret = _ret_fwd(sv["z"], tabs, gn, "ret_fwd")
            pool = _pool_fwd(sv["z"], small["ab_w_pool"][i], small["ab_pool_scale"][i:i + 1], "pool_fwd")
            sv["u"] = (ret, pool)
            mixed = ([ret, pool], w("ab_w_out", i, ret))
        else:
            sv["h1"], sv["z"] = _norm_mm(x, g_mix, w("c_w_qkv", i, x), 3 * D_MODEL // N_DEV, BF16, False, "qkv_fwd")
            rb = jnp.pad(small["c_rel_bias"][i], ((0, 0), (0, REL_PAD - N_REL)))
            sv["bias"] = _attn_bias(rb, "attn_bias")
            sv["o"] = _attn_fwd(sv["z"], sv["bias"], "attn_fwd")
            mixed = ([sv["o"]], w("c_w_out", i, sv["o"]))
        x, sv["h2"], sv["z1"], sv["a"] = _norm_mm(x, small["ffn_norm"][layer:layer + 1], w("w_ffn_in", layer, mixed[0][0]),
                                                  D_FF // N_DEV, BF16, True, "ffn_in_fwd", mixed)
        sv["x1"] = x
        x = _mm_res([sv["a"]], w("w_ffn_out", layer, sv["a"]), x, "ffn_out_fwd")
        saved.append(sv)

    loss, dx, d_final = _final_loss(x, small["final_norm"][None, :], target, "final_loss")

    gs = {k: [None] * DEPTH for k in ("mix_norm", "ffn_norm")}
    for k in ("ab_gn_gain", "ab_w_pool", "ab_pool_scale", "c_rel_bias"):
        gs[k] = [None] * (DEPTH // 2)
    gs["final_norm"] = d_final[0]
    sent = None
    for layer in reversed(range(DEPTH)):
        i = layer // 2
        sv = saved[layer]
        dz1 = _mm_nt_rows(dx, w("w_ffn_out", layer), sv["z1"], "ffn_out_bwd")
        gw = {("w_ffn_out", layer): _mm_tn(sv["a"], dx, 1024, D_MODEL, True, (0, 2), "ffn_out_dw"),
              ("w_ffn_in", layer): _mm_tn(sv["h2"], dz1, D_MODEL, 1024, False, (1, 2), "ffn_in_dw",
                                          comm.after() if layer == 0 else None)}
        w_out = w("ab_w_out" if layer % 2 == 0 else "c_w_out", i)
        dx, dg, du = _mm_nt_normbwd(dz1, w("w_ffn_in", layer), sv["x1"], small["ffn_norm"][layer:layer + 1], dx,
                                    "ffn_in_bwd", sent, w_out)
        gs["ffn_norm"][layer] = dg[0]
        sent = None
        g_mix = small["mix_norm"][layer:layer + 1]
        if layer % 2 == 0:
            gw["ab_w_out", i] = _mm_tn(sv["u"], dx, D_MODEL, D_MODEL, True, (0, N_DEV), "mix_out_dw")
            if layer == 0:
                sent, gw = comm.send(gw), {}
            gn = small["ab_gn_gain"][i:i + 1]
            dz, dgn = _ret_bwd(sv["z"], tabs, gn, sv["o_pre"], sv["states"], du, "ret_bwd")
            dz, dwp, dsc = _pool_bwd(sv["z"], small["ab_w_pool"][i], small["ab_pool_scale"][i:i + 1], du, dz, "pool_bwd")
            gs["ab_gn_gain"][i], gs["ab_w_pool"][i], gs["ab_pool_scale"][i] = dgn[0], dwp, dsc[0]
            gw["ab_w_in", i] = _to_shard_major(_mm_tn(sv["h1"], dz, D_MODEL, AB_IN_WIDTH // 2, False, (1, 1), "ab_in_dw",
                                                      comm.after()), pairs_split=True)
            dx, dg = _mm_nt_normbwd(dz, w("ab_w_in", i), sv["x0"], g_mix, dx, "ab_in_bwd", sent)
        else:
            do = du
            gw["c_w_out", i] = _mm_tn(sv["o"], dx, D_MODEL, D_MODEL, True, (0, N_DEV), "mix_out_dw")
            dq, dk, dv, dbias = _attn_bwd(sv["z"], sv["bias"], sv["o"], do, "attn_bwd")
            gs["c_rel_bias"][i] = _attn_dbias(dbias, "attn_dbias")[:, :N_REL]
            tiles = [_mm_tn(sv["h1"], part, D_MODEL, D_MODEL, False, (1, 1), "qkv_dw", after)
                     for part, after in ((dq, None), (dk, None), (dv, comm.after()))]
            gw["c_w_qkv", i] = _to_shard_major(jnp.concatenate(tiles, axis=0))
            dx, dg = _mm_nt_normbwd([dq, dk, dv], w("c_w_qkv", i), sv["x0"], g_mix, dx, "qkv_bwd", sent)
        gs["mix_norm"][layer] = dg[0]
        if layer > 0:
            sent = comm.send(gw)
    gsmall = {k: (jnp.stack(v) if isinstance(v, list) else v) for k, v in gs.items()}
    return loss, dx, gw, gsmall


BIG = ("w_ffn_in", "w_ffn_out", "ab_w_in", "ab_w_out", "c_w_qkv", "c_w_out")
SMALL = ("mix_norm", "ffn_norm", "ab_gn_gain", "ab_w_pool", "ab_pool_scale", "c_rel_bias", "final_norm")
N_PEERS = N_DEV - 1
FLIPS = [(fx, fy, fc) for fx in (0, 1) for fy in (0, 1) for fc in (0, 1)][1:]


def _peers():
    x, y, c = (lax.axis_index(a) for a in MESH_AXES)
    peers = []
    for fx, fy, fc in FLIPS:
        px, py, pc = (1 - x if fx else x), (1 - y if fy else y), (1 - c if fc else c)
        peers.append(((px, py, pc), 4 * px + 2 * py + pc))
    return 4 * x + 2 * y + c, peers


def _exchange(srcs, by_slot, name, collective_id):
    n = len(srcs)
    src_refs = [jax.new_ref(a, memory_space=pltpu.MemorySpace.HBM) for a in srcs]
    land_refs = [jax.empty_ref(SDS((N_DEV,) + (a.shape[1:] if slotted else a.shape), a.dtype),
                               memory_space=pltpu.MemorySpace.HBM) for a, slotted in zip(srcs, by_slot)]

    @pl.kernel(mesh=plsc.ScalarSubcoreMesh(axis_name="sequencer", num_cores=1), name=name,
               scratch_types=(pltpu.SemaphoreType.DMA((n * N_PEERS,)), pltpu.SemaphoreType.DMA((n * N_PEERS,)),
                              pltpu.SemaphoreType.DMA((n,))),
               compiler_params=pltpu.CompilerParams(collective_id=collective_id))
    def launch(send_sems, recv_sems, local_sems):
        me, peers = _peers()
        barrier = pltpu.get_barrier_semaphore()
        for pos, _ in peers:
            pl.semaphore_signal(barrier, inc=1, device_id=pos, device_id_type=pl.DeviceIdType.MESH)
        pl.semaphore_wait(barrier, N_PEERS)
        waits = []
        for k in range(n):
            own = pltpu.make_async_copy(src_refs[k].at[me] if by_slot[k] else src_refs[k], land_refs[k].at[me],
                                        local_sems.at[k])
            own.start()
            waits.append(own.wait)
            for rel, (pos, slot) in enumerate(peers):
                src = src_refs[k].at[slot] if by_slot[k] else src_refs[k]
                sems = dict(send_sem=send_sems.at[k * N_PEERS + rel], recv_sem=recv_sems.at[k * N_PEERS + rel],
                            device_id=pos, device_id_type=pl.DeviceIdType.MESH)
                send = pltpu.make_async_remote_copy(src_ref=src, dst_ref=land_refs[k].at[me], **sems)
                send.start()
                arrival = pltpu.make_async_remote_copy(src_ref=src, dst_ref=land_refs[k].at[slot], **sems)
                waits += [send.wait_send, arrival.wait_recv]
        for wait in waits:
            wait()

    launch()
    return [r[...] for r in land_refs]


def _cast_group(weights, keys, after, name):
    after = _follow(after)

    def body(*refs):
        n = len(keys)
        for i_ref, o_ref in zip(refs[:n], refs[n + len(after):]):
            o_ref[...] = _bf(i_ref[...])

    def layer_spec(shape, l):
        return pl.BlockSpec((None,) + shape[1:], lambda i: (l, 0, 0))

    whole = lambda shape: pl.BlockSpec(shape, lambda i: (0, 0))
    ins = [weights[k] for k, _ in keys]
    return pl.pallas_call(
        body, name=name, grid=(1,), out_shape=[SDS(w.shape[1:], BF16) for w in ins],
        in_specs=[layer_spec(w.shape, l) for w, (_, l) in zip(ins, keys)] + [FOLLOW] * len(after),
        out_specs=[whole(w.shape[1:]) for w in ins],
        compiler_params=_params(("arbitrary",)),
    )(*ins, *after)


def _to_shard_major(g, pairs_split=False):
    nj, ka, nb = g.shape
    full = jnp.transpose(g, (1, 0, 2)).reshape(ka, nj * nb)
    if pairs_split:
        full = _split_pairs(full, inverse=True)
    return jnp.transpose(full.reshape(ka, N_DEV, nj * nb // N_DEV), (1, 0, 2))


def _from_gathered(name, g):
    if name in ("w_ffn_out", "ab_w_out", "c_w_out"):
        return g.reshape(g.shape[0] * g.shape[1], g.shape[2])
    if name == "ab_w_in":
        return _split_pairs(jnp.transpose(g, (1, 0, 2)).reshape(1, g.shape[1], N_DEV * g.shape[2]))
    return g


class _Comm:
    def __init__(self, weights):
        self.weights_f32 = weights
        self.gathered = {}
        self.got = {}
        self.calls = 0
        self.ended = []
        self.opened = -1

    def _exchange(self, srcs, by_slot, name):
        self.calls += 1
        got = _exchange(srcs, by_slot, name, self.calls)
        self.ended = got[:1]
        return got

    def _gather(self, group, at):
        keys = FWD_GROUPS[group]
        shards = _cast_group(self.weights_f32, keys, self.ended + _follow(at), "cast_%d" % group)
        got = self._exchange(shards, [False] * len(keys), "gather_%d" % group)
        self.gathered.update((k, _from_gathered(k[0], arr)) for k, arr in zip(keys, got))

    def weight(self, name, layer, at=None):
        if (name, layer) not in self.gathered:
            self._gather(0, at[0] if isinstance(at, tuple) else at)
        group = next(g for g, keys in enumerate(FWD_GROUPS) if (name, layer) in keys)
        if group == self.opened + 1:
            self.opened = group
            if group + 1 < len(FWD_GROUPS):
                self._gather(group + 1, at)
        return self.gathered[name, layer]

    def after(self):
        return self.ended

    def send(self, grads, shared=None):
        shared = shared or {}
        keys = list(grads) + list(shared)
        srcs = list(grads.values()) + list(shared.values())
        got = self._exchange(srcs, [True] * len(grads) + [False] * len(shared), "scatter_%d" % self.calls)
        self.got.update(zip(keys, got))
        return list(grads.values())

    def received(self):
        return self.got


def _adamw_math(g, w, m, v):
    m2 = ADAM_B1 * m + (1.0 - ADAM_B1) * g
    v2 = ADAM_B2 * v + (1.0 - ADAM_B2) * jnp.square(g)
    m_hat = m2 / (1.0 - ADAM_B1 ** ADAM_STEP)
    v_hat = v2 / (1.0 - ADAM_B2 ** ADAM_STEP)
    delta = -ADAM_LR * (m_hat / (jnp.sqrt(v_hat) + ADAM_EPS) + ADAM_WD * w)
    return delta, m2, v2


def _adamw(recv, w, m, v, name):
    nl, r, c = w.shape
    tr = _tile(r, 256)

    def body(*refs):
        g_refs = refs[:nl]
        w_ref, m_ref, v_ref, go_ref, d_ref, mo_ref, vo_ref = refs[nl:]
        for l in range(nl):
            @pl.when(pl.program_id(0) == l)
            def _():
                g = g_refs[l][0].astype(F32)
                for p in range(1, N_DEV):
                    g = g + g_refs[l][p].astype(F32)
                go_ref[...] = g
                d_ref[...], mo_ref[...], vo_ref[...] = _adamw_math(g, w_ref[...], m_ref[...], v_ref[...])

    def recv_spec(l):
        return pl.BlockSpec((N_DEV, tr, c), lambda layer, i: (0, jnp.where(layer == l, i, 0), 0))

    blk = pl.BlockSpec((None, tr, c), lambda l, i: (l, i, 0))
    return pl.pallas_call(
        body, name=name, grid=(nl, r // tr), out_shape=[SDS(w.shape, F32)] * 4,
        in_specs=[recv_spec(l) for l in range(nl)] + [blk, blk, blk],
        out_specs=[blk] * 4,
        compiler_params=_params(("arbitrary", "arbitrary")),
    )(*recv, w, m, v)


def _adamw_small(recv, loss_parts, w, m, v, name):
    n = len(w)

    def total(ref):
        t = ref[0]
        for p in range(1, N_DEV):
            t = t + ref[p]
        return t

    def body(*refs):
        g_refs, loss_ref = refs[:n], refs[n]
        w_refs, m_refs, v_refs = refs[n + 1:2 * n + 1], refs[2 * n + 1:3 * n + 1], refs[3 * n + 1:4 * n + 1]
        outs = refs[4 * n + 1:]
        for i in range(n):
            g = total(g_refs[i])
            outs[4 * i][...] = g
            outs[4 * i + 1][...], outs[4 * i + 2][...], outs[4 * i + 3][...] = _adamw_math(
                g, w_refs[i][...], m_refs[i][...], v_refs[i][...])
        outs[4 * n][...] = total(loss_ref)

    out_shape = [SDS(p.shape, F32) for p in w for _ in range(4)] + [SDS(loss_parts.shape[1:], F32)]
    outs = pl.pallas_call(body, name=name, out_shape=out_shape,
                          compiler_params=_params(None))(*recv, loss_parts, *w, *m, *v)
    return [outs[4 * i:4 * i + 4] for i in range(n)], outs[-1]


def kernel(x, mix_norm, ffn_norm, w_ffn_in, w_ffn_out, ab_w_in, ab_gn_gain, ab_w_pool, ab_pool_scale, ab_w_out, c_w_qkv, c_rel_bias, c_w_out, final_norm, loss_target, m_mix_norm, m_ffn_norm, m_w_ffn_in, m_w_ffn_out, m_ab_w_in, m_ab_gn_gain, m_ab_w_pool, m_ab_pool_scale, m_ab_w_out, m_c_w_qkv, m_c_rel_bias, m_c_w_out, m_final_norm, v_mix_norm, v_ffn_norm, v_w_ffn_in, v_w_ffn_out, v_ab_w_in, v_ab_gn_gain, v_ab_w_pool, v_ab_pool_scale, v_ab_w_out, v_c_w_qkv, v_c_rel_bias, v_c_w_out, v_final_norm):
    args = dict(locals())
    weights = {k: args[k] for k in BIG + SMALL}
    moments_m = {k: args["m_" + k] for k in BIG + SMALL}
    moments_v = {k: args["v_" + k] for k in BIG + SMALL}

    small = {k: weights[k] for k in SMALL}
    rows = lambda a: a.reshape(1, -1) if a.ndim == 1 else a

    comm = _Comm(weights)
    loss, dx, last_grads, gsmall = _local_step(x[0], loss_target[0], small, comm)
    comm.send(last_grads, {**{k: rows(gsmall[k]) for k in SMALL}, "loss": loss})
    recv = comm.received()

    outs = {}
    for k in BIG:
        layers = [recv[k, l] for l in range(weights[k].shape[0])]
        outs[k] = _adamw(layers, weights[k], moments_m[k], moments_v[k], "adamw_" + k)
    updated, total = _adamw_small([recv[k] for k in SMALL], recv["loss"], [rows(small[k]) for k in SMALL],
                                  [rows(moments_m[k]) for k in SMALL], [rows(moments_v[k]) for k in SMALL], "adamw_small")
    for k, parts in zip(SMALL, updated):
        outs[k] = [p.reshape(small[k].shape) for p in parts]

    order = SMALL[:2] + BIG[:2] + ("ab_w_in", "ab_gn_gain", "ab_w_pool", "ab_pool_scale", "ab_w_out",
                                   "c_w_qkv", "c_rel_bias", "c_w_out", "final_norm")
    result = [total[0, 0], dx[None]]
    for part in range(4):
        result += [outs[k][part] for k in order]
    return tuple(result)
```

```python
import functools

import jax
import jax.numpy as jnp
from jax import lax
from jax.experimental import pallas as pl
from jax.experimental.pallas import tpu as pltpu
from jax.experimental.pallas import tpu_sc as plsc

F32 = jnp.float32
BF16 = jnp.bfloat16
SDS = jax.ShapeDtypeStruct
MESH_AXES = ("x", "y", "c")
N_DEV = 8

D_MODEL = 1024
DEPTH = 4
CHUNK = 64
D_FF = 4 * D_MODEL
RMS_EPS = 1e-6
RET_WIDTH = 512
RET_HEADS = 4
RET_HEAD_DIM = 128
RET_ROPE_BASE = 10000.0
GN_EPS = 1e-5
POOL_WIDTH = 512
POOL_WINDOWS = (2, 4, 8, 16)
POOL_HALO = 16
AB_IN_WIDTH = 4 * RET_WIDTH + POOL_WIDTH
ATT_HEADS = 16
ATT_HEAD_DIM = 64
LEFT_CHUNKS = 8
REL_CLIP = 128
N_REL = 2 * REL_CLIP + 1
NEG_INF = -1e30

ADAM_LR = 0.001
ADAM_B1 = 0.9
ADAM_B2 = 0.999
ADAM_EPS = 1e-08
ADAM_WD = 0.01
ADAM_STEP = 10

TOKEN_TILE = 512
ATT_Q_TILE = 256
ATT_BACK = LEFT_CHUNKS * CHUNK // ATT_Q_TILE
ATT_K_TILE = (ATT_BACK + 1) * ATT_Q_TILE
ATT_PAIRS = 4
ATT_DIAG = 1024
REL_PAD = 384
VMEM_LIMIT_MB = 56

NT = (((1,), (1,)), ((), ()))
TN = (((0,), (0,)), ((), ()))


def _params(semantics, **kw):
    return pltpu.CompilerParams(dimension_semantics=semantics,
                                vmem_limit_bytes=VMEM_LIMIT_MB * 2 ** 20, **kw)


def _dot(a, b, dims=None):
    if dims is None:
        return jnp.dot(a, b, preferred_element_type=F32)
    return lax.dot_general(a, b, dims, preferred_element_type=F32)


def _bf(v):
    return v.astype(BF16)


def _tile(n, t):
    return min(n, t)


FOLLOW = pl.BlockSpec(memory_space=pl.ANY)


def _follow(after):
    return [] if after is None else list(after) if isinstance(after, (list, tuple)) else [after]


MXU_WIDTH = 256


def _mxu_group(nj, tn):
    return 2 if tn % MXU_WIDTH and (2 * tn) % MXU_WIDTH == 0 and nj % 2 == 0 else 1


def _w_tiles(w_ref, j, group):
    return w_ref[j] if group == 1 else jnp.concatenate([w_ref[j + t] for t in range(group)], axis=1)


def _w_cols(w_ref, j, group, c, width):
    return w_ref[j, :, c:c + width] if group == 1 else _w_tiles(w_ref, j, group)


def _norm_mm(x, gain, w, tn, z_dtype, relu2, name, pre=None):
    s, d = x.shape
    nj = w.shape[0]
    tm = _tile(s, TOKEN_TILE)
    group = _mxu_group(nj, tn)
    pre_parts, w_pre = pre if pre else ([], None)
    widths = [p.shape[1] for p in pre_parts]

    def body(*refs):
        p_refs = refs[:len(pre_parts)]
        refs = refs[len(pre_parts):]
        if pre:
            wp_ref, x_ref, g_ref, w_ref, x1_ref, h_ref, z_ref, *a_ref = refs
            xv, off = x_ref[...], 0
            for p_ref, k in zip(p_refs, widths):
                xv = xv + _dot(p_ref[...], wp_ref[off:off + k, :])
                off += k
            x1_ref[...] = xv
        else:
            x_ref, g_ref, w_ref, h_ref, z_ref, *a_ref = refs
            xv = x_ref[...]
        r = lax.rsqrt(jnp.mean(xv * xv, axis=-1, keepdims=True) + RMS_EPS)
        h = _bf(xv * r * g_ref[...])
        h_ref[...] = h
        cw = tn if tn <= 512 else 512
        for j in range(0, nj, group):
            for c in range(0, tn, cw):
                z = _dot(h, _w_cols(w_ref, j, group, c, cw))
                cols = slice(j * tn + c, j * tn + c + group * cw)
                z_ref[:, cols] = z.astype(z_ref.dtype)
                if relu2:
                    a_ref[0][:, cols] = _bf(jnp.square(jnp.maximum(z, 0.0)))

    n = nj * tn
    rows = lambda width: pl.BlockSpec((tm, width), lambda i: (i, 0))
    out_shape = [SDS((s, d), F32)] * bool(pre) + [SDS((s, d), BF16), SDS((s, n), z_dtype)] + [SDS((s, n), BF16)] * relu2
    out_specs = [rows(d)] * bool(pre) + [rows(d), rows(n)] + [rows(n)] * relu2
    return pl.pallas_call(
        body, name=name, grid=(s // tm,), out_shape=out_shape,
        in_specs=[rows(k) for k in widths] + ([pl.BlockSpec(w_pre.shape, lambda i: (0, 0))] if pre else [])
        + [rows(d), pl.BlockSpec((1, d), lambda i: (0, 0)), pl.BlockSpec((nj, d, tn), lambda i: (0, 0, 0))],
        out_specs=out_specs,
        compiler_params=_params(("parallel",)),
    )(*pre_parts, *([w_pre] if pre else []), x, gain, w)


def _mm_res(parts, w, res, name):
    s, d = res.shape
    tm = _tile(s, TOKEN_TILE)
    widths = [p.shape[1] for p in parts]

    def body(*refs):
        a_refs = refs[:len(parts)]
        w_ref, res_ref, o_ref = refs[len(parts):]
        acc = res_ref[...]
        off = 0
        for a_ref, k in zip(a_refs, widths):
            acc = acc + _dot(a_ref[...], w_ref[off:off + k, :])
            off += k
        o_ref[...] = acc

    return pl.pallas_call(
        body, name=name, grid=(s // tm,), out_shape=SDS((s, d), F32),
        in_specs=[pl.BlockSpec((tm, k), lambda i: (i, 0)) for k in widths]
        + [pl.BlockSpec(w.shape, lambda i: (0, 0)), pl.BlockSpec((tm, d), lambda i: (i, 0))],
        out_specs=pl.BlockSpec((tm, d), lambda i: (i, 0)),
        compiler_params=_params(("parallel",)),
    )(*parts, w, res)


def _mm_nt_rows(dy, w, z, name):
    s, d = dy.shape
    k = w.shape[0]
    tm = _tile(s, TOKEN_TILE)
    tk = _tile(k, 1024)

    def body(dy_ref, w_ref, *rest):
        o_ref = rest[-1]
        dyb = _bf(dy_ref[...])
        for j in range(k // tk):
            cols = slice(j * tk, (j + 1) * tk)
            da = _dot(dyb, w_ref[cols, :], NT)
            if z is not None:
                da = da * (2.0 * jnp.maximum(rest[0][:, cols].astype(F32), 0.0))
            o_ref[:, cols] = _bf(da)

    in_specs = [pl.BlockSpec((tm, d), lambda i: (i, 0)), pl.BlockSpec((k, d), lambda i: (0, 0))]
    args = [dy, w]
    if z is not None:
        in_specs.append(pl.BlockSpec((tm, k), lambda i: (i, 0)))
        args.append(z)
    return pl.pallas_call(
        body, name=name, grid=(s // tm,), out_shape=SDS((s, k), BF16),
        in_specs=in_specs, out_specs=pl.BlockSpec((tm, k), lambda i: (i, 0)),
        compiler_params=_params(("parallel",)),
    )(*args)


def _w_range(w_ref, c0, c1):
    nc = w_ref.shape[2]
    pieces, c = [], c0
    while c < c1:
        j = c // nc
        hi = min(nc, c1 - j * nc)
        pieces.append(w_ref[j, :, c - j * nc:hi])
        c = j * nc + hi
    return pieces[0] if len(pieces) == 1 else jnp.concatenate(pieces, axis=1)


def _mm_nt_normbwd(dz, w, x, gain, dres, name, after=None, w_post=None):
    parts, after = list(dz) if isinstance(dz, (list, tuple)) else [dz], _follow(after)
    widths = [p.shape[1] for p in parts]
    s, d = x.shape
    tm = _tile(s, TOKEN_TILE)
    chunk = 2 * MXU_WIDTH
    halves = 2 if tm % 32 == 0 else 1

    def body(*refs):
        dz_refs = refs[:len(parts)]
        w_ref, x_ref, g_ref, dres_ref = refs[len(parts):len(parts) + 4]
        dx_ref, dg_ref = refs[-2 - has_post:][:2]

        @pl.when(pl.program_id(0) == 0)
        def _():
            dg_ref[...] = jnp.zeros_like(dg_ref)

        for half in range(halves):
            rows = slice(half * tm // halves, (half + 1) * tm // halves)
            dh, base = None, 0
            for dz_ref, width in zip(dz_refs, widths):
                for c in range(0, width, chunk):
                    term = _dot(dz_ref[rows, c:c + chunk], _w_range(w_ref, base + c, base + c + chunk), NT)
                    dh = term if dh is None else dh + term
                base += width
            xv = x_ref[rows, :]
            r = lax.rsqrt(jnp.mean(xv * xv, axis=-1, keepdims=True) + RMS_EPS)
            xn = xv * r
            dg_ref[...] += jnp.sum(dh * xn, axis=0, keepdims=True)
            dxh = dh * g_ref[...]
            dx = dres_ref[rows, :] + r * (dxh - xn * jnp.mean(dxh * xn, axis=-1, keepdims=True))
            dx_ref[rows, :] = dx
            if has_post:
                refs[-1][rows, :] = _bf(_dot(_bf(dx), refs[len(parts) + 4][...], NT))

    has_post = w_post is not None
    post_in = [pl.BlockSpec(w_post.shape, lambda i: (0, 0))] if has_post else []
    post_out = [pl.BlockSpec((tm, w_post.shape[0]), lambda i: (i, 0))] if has_post else []
    return pl.pallas_call(
        body, name=name, grid=(s // tm,),
        out_shape=[SDS((s, d), F32), SDS((1, d), F32)] + ([SDS((s, w_post.shape[0]), BF16)] if has_post else []),
        in_specs=[pl.BlockSpec((tm, width), lambda i: (i, 0)) for width in widths]
        + [pl.BlockSpec(w.shape, lambda i: (0, 0, 0)),
           pl.BlockSpec((tm, d), lambda i: (i, 0)),
           pl.BlockSpec((1, d), lambda i: (0, 0)),
           pl.BlockSpec((tm, d), lambda i: (i, 0))] + post_in + [FOLLOW] * len(after),
        out_specs=[pl.BlockSpec((tm, d), lambda i: (i, 0)), pl.BlockSpec((1, d), lambda i: (0, 0))] + post_out,
        compiler_params=_params(("arbitrary",)),
    )(*parts, w, x, gain, dres, *([w_post] if has_post else []), *after)


def _mm_tn(a, b, ka, nb, a_tiled, split, name, after=None):
    a_parts, after = list(a) if isinstance(a, (list, tuple)) else [a], _follow(after)
    s = a_parts[0].shape[0]
    tm = _tile(s, 4 * TOKEN_TILE)
    nm = s // tm
    nj = a_parts[0].shape[1] // ka if a_tiled and len(a_parts) == 1 else (1 if a_tiled else b.shape[1] // nb)
    axis, parts = split
    pr, pc = (ka // parts, nb) if axis == 0 else (ka, nb // parts)

    def body(*refs):
        a_refs, b_ref = refs[:len(a_parts)], refs[len(a_parts)]
        o_ref, acc = refs[-2:]
        m = pl.program_id(1)

        @pl.when(m == 0)
        def _():
            acc[...] = jnp.zeros_like(acc)

        av = a_refs[0][...] if len(a_refs) == 1 else jnp.concatenate([r[...] for r in a_refs], axis=1)
        acc[...] += _dot(_bf(av), _bf(b_ref[...]), TN)

        @pl.when(m == nm - 1)
        def _():
            for q in range(parts):
                piece = acc[q * pr:(q + 1) * pr, :] if axis == 0 else acc[:, q * pc:(q + 1) * pc]
                o_ref[q] = piece.astype(o_ref.dtype)

    return pl.pallas_call(
        body, name=name, grid=(nj, nm), out_shape=SDS((nj * parts, pr, pc), BF16),
        in_specs=([pl.BlockSpec((tm, ka), (lambda j, m: (m, j)) if a_tiled else (lambda j, m: (m, 0)))]
                  if len(a_parts) == 1 else [pl.BlockSpec((tm, p.shape[1]), lambda j, m: (m, 0)) for p in a_parts])
        + [pl.BlockSpec((tm, nb), (lambda j, m: (m, 0)) if a_tiled else (lambda j, m: (m, j)))]
        + [FOLLOW] * len(after),
        out_specs=pl.BlockSpec((parts, pr, pc), lambda j, m: (j, 0, 0)),
        scratch_shapes=[pltpu.VMEM((ka, nb), F32)],
        compiler_params=_params(("parallel", "arbitrary")),
    )(*a_parts, b, *after)


def _final_loss(x, gain, target, name):
    s, d = x.shape
    tm = _tile(s, TOKEN_TILE)

    def body(x_ref, g_ref, t_ref, loss_ref, dx_ref, dg_ref):
        @pl.when(pl.program_id(0) == 0)
        def _():
            loss_ref[...] = jnp.zeros_like(loss_ref)
            dg_ref[...] = jnp.zeros_like(dg_ref)

        xv = x_ref[...]
        r = lax.rsqrt(jnp.mean(xv * xv, axis=-1, keepdims=True) + RMS_EPS)
        xn = xv * r
        err = xn * g_ref[...] - t_ref[...]
        loss_ref[...] += (0.5 / d) * jnp.sum(err * err)
        dy = err * (1.0 / d)
        dg_ref[...] += jnp.sum(dy * xn, axis=0, keepdims=True)
        dxh = dy * g_ref[...]
        dx_ref[...] = r * (dxh - xn * jnp.mean(dxh * xn, axis=-1, keepdims=True))

    return pl.pallas_call(
        body, name=name, grid=(s // tm,),
        out_shape=[SDS((8, 128), F32), SDS((s, d), F32), SDS((1, d), F32)],
        in_specs=[pl.BlockSpec((tm, d), lambda i: (i, 0)), pl.BlockSpec((1, d), lambda i: (0, 0)),
                  pl.BlockSpec((tm, d), lambda i: (i, 0))],
        out_specs=[pl.BlockSpec((8, 128), lambda i: (0, 0)), pl.BlockSpec((tm, d), lambda i: (i, 0)),
                   pl.BlockSpec((1, d), lambda i: (0, 0))],
        compiler_params=_params(("arbitrary",)),
    )(x, gain, target)


def _retention_tables(s):
    half = RET_HEAD_DIM // 2
    inv_freq = 1.0 / (RET_ROPE_BASE ** jnp.linspace(0.0, 1.0, half, dtype=F32))
    ang = jnp.arange(s, dtype=F32)[:, None] * inv_freq[None, :]
    cos, sin = jnp.cos(ang), jnp.sin(ang)
    cos_e = jnp.concatenate([cos, cos], axis=-1)
    sin_s = jnp.concatenate([-sin, sin], axis=-1)
    log_g = jnp.log1p(-jnp.power(2.0, -5.0 - jnp.arange(RET_HEADS, dtype=F32)))
    pos = jnp.arange(CHUNK, dtype=F32)
    dmat = jnp.exp(jnp.abs(pos[:, None] - pos[None, :])[None] * log_g[:, None, None])
    qdec = jnp.exp((pos[None, :] + 1.0) * log_g[:, None])
    kdec = jnp.exp((CHUNK - 1.0 - pos[None, :]) * log_g[:, None])
    lam = jnp.exp(CHUNK * log_g)
    wide = (RET_HEADS, CHUNK, RET_HEAD_DIM)
    return dict(cos=cos_e, sin=sin_s, dmat=dmat,
                qdec=jnp.broadcast_to(qdec[:, :, None], wide),
                kdec=jnp.broadcast_to(kdec[:, :, None], wide),
                lam=jnp.broadcast_to(lam[:, None, None], (RET_HEADS, RET_HEAD_DIM, RET_HEAD_DIM)))


def _swap_pairs(t):
    return pltpu.roll(t, RET_HEAD_DIM // 2, 1)


def _split_pairs(w, inverse=False):
    lead, nqk = w.shape[:-1], 2 * RET_WIDTH
    shape = (2 * RET_HEADS, 2, RET_HEAD_DIM // 2) if inverse else (2 * RET_HEADS, RET_HEAD_DIM // 2, 2)
    qk = jnp.swapaxes(w[..., :nqk].reshape(lead + shape), -1, -2).reshape(lead + (nqk,))
    return jnp.concatenate([qk, w[..., nqk:]], axis=-1)


def _head(h):
    return slice(h * RET_HEAD_DIM, (h + 1) * RET_HEAD_DIM)


def _ret_common_specs(tb, blk):
    zs = [pl.BlockSpec((tb, RET_WIDTH), functools.partial(lambda j, i: (blk(i), j), j)) for j in range(4)]
    tabs = [pl.BlockSpec((tb, RET_HEAD_DIM), lambda i: (blk(i), 0))] * 2
    consts = [pl.BlockSpec((1, RET_WIDTH), lambda i: (0, 0)),
              pl.BlockSpec((RET_HEADS, CHUNK, CHUNK), lambda i: (0, 0, 0)),
              pl.BlockSpec((RET_HEADS, CHUNK, RET_HEAD_DIM), lambda i: (0, 0, 0)),
              pl.BlockSpec((RET_HEADS, CHUNK, RET_HEAD_DIM), lambda i: (0, 0, 0)),
              pl.BlockSpec((RET_HEADS, RET_HEAD_DIM, RET_HEAD_DIM), lambda i: (0, 0, 0))]
    return zs + tabs + consts


def _ret_fwd(z, tabs, gn_gain, name):
    s = z.shape[0]
    tb = _tile(s, TOKEN_TILE)
    ncb = tb // CHUNK
    scale = RET_HEAD_DIM ** -0.5

    def body(q_ref, k_ref, v_ref, g_ref, cos_ref, sin_ref, gain_ref, dm_ref, qd_ref, kd_ref, lam_ref,
             o_ref, st_ref, ret_ref, s_scr, qr_scr, kr_scr):
        @pl.when(pl.program_id(0) == 0)
        def _():
            s_scr[...] = jnp.zeros_like(s_scr)

        cosv, sinv = cos_ref[...], sin_ref[...]
        for h in range(RET_HEADS):
            qh, kh = q_ref[:, _head(h)], k_ref[:, _head(h)]
            qr_scr[:, _head(h)] = qh * cosv + _swap_pairs(qh) * sinv
            kr_scr[:, _head(h)] = (kh * cosv + _swap_pairs(kh) * sinv) * scale

        def chunk(c, carry):
            rows = pl.ds(pl.multiple_of(c * CHUNK, CHUNK), CHUNK)
            for h in range(RET_HEADS):
                qc, kc, vc = qr_scr[rows, _head(h)], kr_scr[rows, _head(h)], v_ref[rows, _head(h)]
                a = _dot(_bf(qc), _bf(kc), NT) * dm_ref[h]
                st = s_scr[h]
                st_ref[c, h] = st
                o_ref[rows, _head(h)] = _dot(_bf(a), _bf(vc)) + _dot(_bf(qc * qd_ref[h]), _bf(st))
                s_scr[h] = st * lam_ref[h] + _dot(_bf(kc * kd_ref[h]), _bf(vc), TN)
            return carry

        lax.fori_loop(0, ncb, chunk, 0, unroll=True)
        for h in range(RET_HEADS):
            o = o_ref[:, _head(h)]
            mu = jnp.mean(o, axis=-1, keepdims=True)
            oc = o - mu
            y = oc * lax.rsqrt(jnp.mean(oc * oc, axis=-1, keepdims=True) + GN_EPS) * gain_ref[:, _head(h)]
            g = g_ref[:, _head(h)]
            ret_ref[:, _head(h)] = _bf(g / (1.0 + jnp.exp(-g)) * y)

    nc = s // CHUNK
    return pl.pallas_call(
        body, name=name, grid=(s // tb,),
        out_shape=[SDS((s, RET_WIDTH), F32), SDS((nc, RET_HEADS, RET_HEAD_DIM, RET_HEAD_DIM), F32),
                   SDS((s, RET_WIDTH), BF16)],
        in_specs=_ret_common_specs(tb, lambda i: i),
        out_specs=[pl.BlockSpec((tb, RET_WIDTH), lambda i: (i, 0)),
                   pl.BlockSpec((ncb, RET_HEADS, RET_HEAD_DIM, RET_HEAD_DIM), lambda i: (i, 0, 0, 0)),
                   pl.BlockSpec((tb, RET_WIDTH), lambda i: (i, 0))],
        scratch_shapes=[pltpu.VMEM((RET_HEADS, RET_HEAD_DIM, RET_HEAD_DIM), F32),
                        pltpu.VMEM((tb, RET_WIDTH), F32), pltpu.VMEM((tb, RET_WIDTH), F32)],
        compiler_params=_params(("arbitrary",)),
    )(z, z, z, z, tabs["cos"], tabs["sin"], gn_gain, tabs["dmat"], tabs["qdec"], tabs["kdec"], tabs["lam"])


def _ret_bwd(z, tabs, gn_gain, o_pre, states, du, name):
    s = z.shape[0]
    tb = _tile(s, TOKEN_TILE)
    ncb = tb // CHUNK
    nblk = s // tb
    scale = RET_HEAD_DIM ** -0.5
    rev = lambda i: nblk - 1 - i

    def body(q_ref, k_ref, v_ref, g_ref, cos_ref, sin_ref, gain_ref, dm_ref, qd_ref, kd_ref, lam_ref,
             o_ref, st_ref, dret_ref, dz_ref, dgain_ref, g_scr, qr_scr, kr_scr, do_scr, dq_scr, dk_scr):
        @pl.when(pl.program_id(0) == 0)
        def _():
            g_scr[...] = jnp.zeros_like(g_scr)
            dgain_ref[...] = jnp.zeros_like(dgain_ref)

        cosv, sinv = cos_ref[...], sin_ref[...]
        for h in range(RET_HEADS):
            hs = _head(h)
            qh, kh = q_ref[:, hs], k_ref[:, hs]
            qr_scr[:, hs] = qh * cosv + _swap_pairs(qh) * sinv
            kr_scr[:, hs] = (kh * cosv + _swap_pairs(kh) * sinv) * scale
            o = o_ref[:, hs]
            mu = jnp.mean(o, axis=-1, keepdims=True)
            oc = o - mu
            rstd = lax.rsqrt(jnp.mean(oc * oc, axis=-1, keepdims=True) + GN_EPS)
            yh = oc * rstd
            gain = gain_ref[:, hs]
            g = g_ref[:, hs]
            sg = 1.0 / (1.0 + jnp.exp(-g))
            dret = dret_ref[:, hs].astype(F32)
            dy = dret * (g * sg)
            dz_ref[:, 3 * RET_WIDTH + h * RET_HEAD_DIM:3 * RET_WIDTH + (h + 1) * RET_HEAD_DIM] = _bf(
                dret * (yh * gain) * (sg * (1.0 + g * (1.0 - sg))))
            dgain_ref[:, hs] += jnp.sum(dy * yh, axis=0, keepdims=True)
            dyh = dy * gain
            do_scr[:, hs] = rstd * (dyh - jnp.mean(dyh, axis=-1, keepdims=True)
                                    - yh * jnp.mean(dyh * yh, axis=-1, keepdims=True))

        def chunk(cc, carry):
            c = ncb - 1 - cc
            rows = pl.ds(pl.multiple_of(c * CHUNK, CHUNK), CHUNK)
            for h in range(RET_HEADS):
                hs = _head(h)
                qc, kc, vc, doc = _bf(qr_scr[rows, hs]), _bf(kr_scr[rows, hs]), _bf(v_ref[rows, hs]), _bf(do_scr[rows, hs])
                qdc, kdc = qd_ref[h], kd_ref[h]
                st, gs = _bf(st_ref[c, h]), g_scr[h]
                gsb = _bf(gs)
                dm = dm_ref[h]
                p = _bf(_dot(qc, kc, NT) * dm)
                da = _bf(_dot(doc, vc, NT) * dm)
                kt = _bf(kr_scr[rows, hs] * kdc)
                qt = _bf(qr_scr[rows, hs] * qdc)
                dz_ref[rows, 2 * RET_WIDTH + h * RET_HEAD_DIM:2 * RET_WIDTH + (h + 1) * RET_HEAD_DIM] = _bf(
                    _dot(p, doc, TN) + _dot(kt, gsb))
                dq_scr[rows, hs] = _dot(da, kc) + _dot(doc, st, NT) * qdc
                dk_scr[rows, hs] = _dot(da, qc, TN) + _dot(vc, gsb, NT) * kdc
                g_scr[h] = gs * lam_ref[h] + _dot(qt, doc, TN)
            return carry

        lax.fori_loop(0, ncb, chunk, 0, unroll=True)
        for h in range(RET_HEADS):
            hs = _head(h)
            dq, dk = dq_scr[:, hs], dk_scr[:, hs]
            dz_ref[:, h * RET_HEAD_DIM:(h + 1) * RET_HEAD_DIM] = _bf(dq * cosv - _swap_pairs(dq) * sinv)
            dz_ref[:, RET_WIDTH + h * RET_HEAD_DIM:RET_WIDTH + (h + 1) * RET_HEAD_DIM] = _bf(
                (dk * cosv - _swap_pairs(dk) * sinv) * scale)

    return pl.pallas_call(
        body, name=name, grid=(nblk,),
        out_shape=[SDS((s, AB_IN_WIDTH), BF16), SDS((1, RET_WIDTH), F32)],
        in_specs=_ret_common_specs(tb, rev)
        + [pl.BlockSpec((tb, RET_WIDTH), lambda i: (rev(i), 0)),
           pl.BlockSpec((ncb, RET_HEADS, RET_HEAD_DIM, RET_HEAD_DIM), lambda i: (rev(i), 0, 0, 0)),
           pl.BlockSpec((tb, RET_WIDTH), lambda i: (rev(i), 0))],
        out_specs=[pl.BlockSpec((tb, 4 * RET_WIDTH), lambda i: (rev(i), 0)),
                   pl.BlockSpec((1, RET_WIDTH), lambda i: (0, 0))],
        scratch_shapes=[pltpu.VMEM((RET_HEADS, RET_HEAD_DIM, RET_HEAD_DIM), F32)]
        + [pltpu.VMEM((tb, RET_WIDTH), F32)] * 5,
        compiler_params=_params(("arbitrary",)),
    )(z, z, z, z, tabs["cos"], tabs["sin"], gn_gain, tabs["dmat"], tabs["qdec"], tabs["kdec"], tabs["lam"],
      o_pre, states, du)


POOL_COL = 4 * RET_WIDTH // POOL_WIDTH


def _pooled(cur, prev, t0):
    tm = cur.shape[0]
    xx = jnp.concatenate([prev, cur], axis=0)
    sums = {1: xx}
    w = 1
    while w < POOL_WINDOWS[-1]:
        sums[2 * w] = sums[w] + pltpu.roll(sums[w], w, 0)
        w *= 2
    t = t0 + lax.broadcasted_iota(jnp.int32, (tm, 128), 0)
    outs = []
    for gi, w in enumerate(POOL_WINDOWS):
        cols = slice(gi * 128, (gi + 1) * 128)
        cnt = jnp.minimum(t + 1, w).astype(F32)
        outs.append(sums[w][POOL_HALO:, cols] / cnt - cur[:, cols])
    return outs


def _pool_fwd(z, w_pool, scale, name):
    s = z.shape[0]
    tm = _tile(s, TOKEN_TILE)
    hb = tm // POOL_HALO

    def body(p_ref, prev_ref, w_ref, sc_ref, o_ref):
        i = pl.program_id(0)
        prev = jnp.where(i > 0, prev_ref[...], 0.0)
        pooled = _pooled(p_ref[...], prev, i * tm)
        for gi in range(len(POOL_WINDOWS)):
            cols = slice(gi * 128, (gi + 1) * 128)
            o_ref[:, cols] = _bf(_dot(_bf(pooled[gi]), _bf(w_ref[gi])) * sc_ref[:, cols])

    return pl.pallas_call(
        body, name=name, grid=(s // tm,), out_shape=SDS((s, POOL_WIDTH), BF16),
        in_specs=[pl.BlockSpec((tm, POOL_WIDTH), lambda i: (i, POOL_COL)),
                  pl.BlockSpec((POOL_HALO, POOL_WIDTH), lambda i: (jnp.maximum(i * hb - 1, 0), POOL_COL)),
                  pl.BlockSpec(w_pool.shape, lambda i: (0, 0, 0)),
                  pl.BlockSpec((1, POOL_WIDTH), lambda i: (0, 0))],
        out_specs=pl.BlockSpec((tm, POOL_WIDTH), lambda i: (i, 0)),
        compiler_params=_params(("parallel",)),
    )(z, z, w_pool, scale)


def _pool_bwd(z, w_pool, scale, du, dz, name):
    s = z.shape[0]
    tm = _tile(s, TOKEN_TILE)
    hb = tm // POOL_HALO
    nblk = s // tm
    last_halo = s // POOL_HALO - 1

    def body(p_ref, prev_ref, w_ref, sc_ref, do_ref, don_ref, dz_ref, dp_ref, dw_ref, dsc_ref):
        i = pl.program_id(0)

        @pl.when(i == 0)
        def _():
            dw_ref[...] = jnp.zeros_like(dw_ref)
            dsc_ref[...] = jnp.zeros_like(dsc_ref)

        prev = jnp.where(i > 0, prev_ref[...], 0.0)
        pooled = _pooled(p_ref[...], prev, i * tm)
        dout = do_ref[...].astype(F32)
        dout_next = jnp.where(i < nblk - 1, don_ref[...].astype(F32), 0.0)
        sc = sc_ref[...]
        dmix = jnp.concatenate([dout * sc, dout_next * sc], axis=0)
        n = tm + POOL_HALO
        t = i * tm + lax.broadcasted_iota(jnp.int32, (n, 128), 0)
        for gi, w in enumerate(POOL_WINDOWS):
            cols = slice(gi * 128, (gi + 1) * 128)
            wg = _bf(w_ref[gi])
            pg = _bf(pooled[gi])
            dsc_ref[:, cols] += jnp.sum(dout[:, cols] * _dot(pg, wg), axis=0, keepdims=True)
            dw_ref[gi] += _dot(pg, _bf(dmix[:tm, cols]), TN)
            dpool = _dot(_bf(dmix[:, cols]), wg, NT)
            acc = dpool / jnp.minimum(t + 1, w).astype(F32)
            step = 1
            while step < w:
                acc = acc + pltpu.roll(acc, n - step, 0)
                step *= 2
            dp_ref[:, cols] = _bf(acc[:tm] - dpool[:tm])

    return pl.pallas_call(
        body, name=name, grid=(nblk,),
        out_shape=[SDS(dz.shape, BF16), SDS(w_pool.shape, F32), SDS((1, POOL_WIDTH), F32)],
        in_specs=[pl.BlockSpec((tm, POOL_WIDTH), lambda i: (i, POOL_COL)),
                  pl.BlockSpec((POOL_HALO, POOL_WIDTH), lambda i: (jnp.maximum(i * hb - 1, 0), POOL_COL)),
                  pl.BlockSpec(w_pool.shape, lambda i: (0, 0, 0)),
                  pl.BlockSpec((1, POOL_WIDTH), lambda i: (0, 0)),
                  pl.BlockSpec((tm, POOL_WIDTH), lambda i: (i, 1)),
                  pl.BlockSpec((POOL_HALO, POOL_WIDTH), lambda i: (jnp.minimum((i + 1) * hb, last_halo), 1)),
                  pl.BlockSpec(memory_space=pl.ANY)],
        out_specs=[pl.BlockSpec((tm, POOL_WIDTH), lambda i: (i, POOL_COL)),
                   pl.BlockSpec(w_pool.shape, lambda i: (0, 0, 0)),
                   pl.BlockSpec((1, POOL_WIDTH), lambda i: (0, 0))],
        input_output_aliases={6: 0},
        compiler_params=_params(("arbitrary",)),
    )(z, z, w_pool, scale, du, du, dz)


def _rel_onehot():
    r = lax.broadcasted_iota(jnp.int32, (REL_PAD, ATT_DIAG), 0)
    c = lax.broadcasted_iota(jnp.int32, (REL_PAD, ATT_DIAG), 1)
    rel = jnp.where(c < ATT_K_TILE, jnp.clip(LEFT_CHUNKS * CHUNK - c, -REL_CLIP, REL_CLIP) + REL_CLIP,
                    2 * REL_CLIP)
    return (rel == r).astype(BF16)


def _split3(v):
    hi = _bf(v)
    r1 = v - hi.astype(F32)
    mid = _bf(r1)
    return hi, mid, _bf(r1 - mid.astype(F32))


def _skew(v, sign):
    row = lax.broadcasted_iota(jnp.int32, v.shape, 0)
    bit = 1
    while bit < ATT_Q_TILE:
        shift = bit if sign > 0 else ATT_DIAG - bit
        v = jnp.where((row & bit) != 0, pltpu.roll(v, shift, 1), v)
        bit *= 2
    return v


def _attn_bias(rel_bias, name):
    def body(t_ref, o_ref):
        oh = _rel_onehot()
        base = sum(_dot(part, oh) for part in _split3(t_ref[0]))
        full = _skew(jnp.broadcast_to(base[0:1], (ATT_Q_TILE, ATT_DIAG)), +1)[:, :ATT_K_TILE]
        qc = lax.broadcasted_iota(jnp.int32, full.shape, 0) // CHUNK
        kc = lax.broadcasted_iota(jnp.int32, full.shape, 1) // CHUNK
        o_ref[0] = jnp.where((kc >= qc) & (kc <= qc + LEFT_CHUNKS), full, NEG_INF)

    t8 = jnp.broadcast_to(rel_bias[:, None, :], (ATT_HEADS, 8, REL_PAD))
    return pl.pallas_call(
        body, name=name, grid=(ATT_HEADS,), out_shape=SDS((ATT_HEADS, ATT_Q_TILE, ATT_K_TILE), F32),
        in_specs=[pl.BlockSpec((1, 8, REL_PAD), lambda h: (h, 0, 0))],
        out_specs=pl.BlockSpec((1, ATT_Q_TILE, ATT_K_TILE), lambda h: (h, 0, 0)),
        compiler_params=_params(("parallel",)),
    )(t8)


def _attn_dbias(dbias, name):
    def body(d_ref, o_ref):
        pad = jnp.zeros((ATT_Q_TILE, ATT_DIAG - ATT_K_TILE), F32)
        diag = _skew(jnp.concatenate([d_ref[0], pad], axis=1), -1)
        col = jnp.sum(diag, axis=0, keepdims=True)
        oh = _rel_onehot()
        col8 = jnp.broadcast_to(col, (8, ATT_DIAG))
        o_ref[0] = sum(_dot(part, oh, NT) for part in _split3(col8))

    out = pl.pallas_call(
        body, name=name, grid=(ATT_HEADS,), out_shape=SDS((ATT_HEADS, 8, REL_PAD), F32),
        in_specs=[pl.BlockSpec((1, ATT_Q_TILE, ATT_K_TILE), lambda h: (h, 0, 0))],
        out_specs=pl.BlockSpec((1, 8, REL_PAD), lambda h: (h, 0, 0)),
        compiler_params=_params(("parallel",)),
    )(dbias)
    return out[:, 0, :]


ATT_WIDTH = 128 * ATT_PAIRS
ATT_GROUPS = D_MODEL // ATT_WIDTH


def _attn_specs(nq):
    def tile(off, back):
        return pl.BlockSpec((ATT_Q_TILE, ATT_WIDTH),
                            lambda g, i: (jnp.maximum(jnp.minimum(i, nq - 1) - back, 0), off + g))

    backs = [ATT_BACK - b for b in range(ATT_BACK + 1)]
    return ([tile(0, 0)] + [tile(ATT_GROUPS, b) for b in backs] + [tile(2 * ATT_GROUPS, b) for b in backs]
            + [pl.BlockSpec((2 * ATT_PAIRS, ATT_Q_TILE, ATT_K_TILE), lambda g, i: (g, 0, 0))])


def _attn_weights(qh, k2, bias, i, masked):
    sc = _dot(qh, k2, NT) + bias
    if masked:
        kpos = (i - ATT_BACK) * ATT_Q_TILE + lax.broadcasted_iota(jnp.int32, sc.shape, 1)
        sc = jnp.where(kpos >= 0, sc, NEG_INF)
    e = jnp.exp(sc - jnp.max(sc, axis=-1, keepdims=True))
    return e, 1.0 / jnp.sum(e, axis=-1, keepdims=True)


def _first_head():
    return lax.broadcasted_iota(jnp.int32, (ATT_Q_TILE, 128), 1) < ATT_HEAD_DIM


def _pair_operands(q_ref, k_refs, v_refs, pp):
    cols = slice(pp * 128, (pp + 1) * 128)
    q2 = q_ref[:, cols] * ATT_HEAD_DIM ** -0.5
    k2 = jnp.concatenate([r[:, cols] for r in k_refs], axis=0)
    v2 = jnp.concatenate([r[:, cols] for r in v_refs], axis=0)
    return cols, q2, k2, v2


def _attn_fwd(z, bias, name):
    s = z.shape[0]
    nq = s // ATT_Q_TILE
    nt = ATT_BACK + 1

    def body(q_ref, *rest):
        k_refs, v_refs, (b_ref, o_ref) = rest[:nt], rest[nt:2 * nt], rest[2 * nt:]
        i = pl.program_id(1)
        first = _first_head()

        def compute(masked):
            for pp in range(ATT_PAIRS):
                cols, q2, k2, v2 = _pair_operands(q_ref, k_refs, v_refs, pp)
                outs = []
                for hh in range(2):
                    qh = jnp.where(first if hh == 0 else ~first, q2, 0)
                    e, inv = _attn_weights(qh, k2, b_ref[2 * pp + hh], i, masked)
                    outs.append(_dot(_bf(e), v2) * inv)
                o_ref[:, cols] = _bf(jnp.where(first, outs[0], outs[1]))

        pl.when(i < ATT_BACK)(lambda: compute(True))
        pl.when(i >= ATT_BACK)(lambda: compute(False))

    return pl.pallas_call(
        body, name=name, grid=(ATT_GROUPS, nq), out_shape=SDS((s, D_MODEL), BF16),
        in_specs=_attn_specs(nq),
        out_specs=pl.BlockSpec((ATT_Q_TILE, ATT_WIDTH), lambda g, i: (i, g)),
        compiler_params=_params(("parallel", "parallel")),
    )(*([z] * (1 + 2 * nt)), bias)


def _attn_bwd(z, bias, o, do, name):
    s = z.shape[0]
    nq = s // ATT_Q_TILE
    nt = ATT_BACK + 1

    def body(q_ref, *rest):
        k_refs, v_refs = rest[:nt], rest[nt:2 * nt]
        b_ref, o_ref, do_ref, dq_ref, dk_ref, dv_ref, db_ref, dk_acc, dv_acc = rest[2 * nt:]
        i = pl.program_id(1)
        first = _first_head()

        @pl.when(i == 0)
        def _():
            db_ref[...] = jnp.zeros_like(db_ref)
            dk_acc[...] = jnp.zeros_like(dk_acc)
            dv_acc[...] = jnp.zeros_like(dv_acc)

        def compute(masked):
            for pp in range(ATT_PAIRS):
                cols, q2, k2, v2 = _pair_operands(q_ref, k_refs, v_refs, pp)
                do2 = do_ref[:, cols].astype(F32)
                prod = do2 * o_ref[:, cols].astype(F32)
                dqs, dk, dv = [], None, None
                for hh in range(2):
                    mine = first if hh == 0 else ~first
                    qh = jnp.where(mine, q2, 0)
                    e, inv = _attn_weights(qh, k2, b_ref[2 * pp + hh], i, masked)
                    delta = jnp.sum(jnp.where(mine, prod, 0.0), axis=-1, keepdims=True) * inv
                    doh = _bf(jnp.where(mine, do2 * inv, 0.0))
                    ds = e * (_dot(doh, v2, NT) - delta)
                    db_ref[2 * pp + hh] += ds
                    dsb = _bf(ds)
                    dqs.append(_dot(dsb, k2))
                    dkh, dvh = _dot(dsb, qh, TN), _dot(_bf(e), doh, TN)
                    dk, dv = (dkh, dvh) if hh == 0 else (dk + dkh, dv + dvh)
                dq_ref[:, cols] = _bf(jnp.where(first, dqs[0], dqs[1]) * ATT_HEAD_DIM ** -0.5)
                for b in range(nt):
                    slot = (i + b + 1) % nt
                    rows = slice(b * ATT_Q_TILE, (b + 1) * ATT_Q_TILE)
                    if b < ATT_BACK:
                        dk_acc[slot, :, cols] += dk[rows]
                        dv_acc[slot, :, cols] += dv[rows]
                    else:
                        dk_acc[slot, :, cols] = dk[rows]
                        dv_acc[slot, :, cols] = dv[rows]

        pl.when(i < ATT_BACK)(lambda: compute(True))
        pl.when((i >= ATT_BACK) & (i < nq))(lambda: compute(False))
        done = (i + 1) % nt
        dk_ref[...] = _bf(dk_acc[done])
        dv_ref[...] = _bf(dv_acc[done])

    tile = pl.BlockSpec((ATT_Q_TILE, ATT_WIDTH), lambda g, i: (jnp.minimum(i, nq - 1), g))
    late = pl.BlockSpec((ATT_Q_TILE, ATT_WIDTH), lambda g, i: (jnp.maximum(i - ATT_BACK, 0), g))
    ring = pltpu.VMEM((nt, ATT_Q_TILE, ATT_WIDTH), F32)
    return pl.pallas_call(
        body, name=name, grid=(ATT_GROUPS, nq + ATT_BACK),
        out_shape=[SDS((s, D_MODEL), BF16)] * 3 + [SDS((ATT_HEADS, ATT_Q_TILE, ATT_K_TILE), F32)],
        in_specs=_attn_specs(nq) + [tile, tile],
        out_specs=[tile, late, late, pl.BlockSpec((2 * ATT_PAIRS, ATT_Q_TILE, ATT_K_TILE), lambda g, i: (g, 0, 0))],
        scratch_shapes=[ring, ring],
        compiler_params=_params(("parallel", "arbitrary")),
    )(*([z] * (1 + 2 * nt)), bias, o, do)


FWD_GROUPS = (
    (("ab_w_in", 0),),
    (("ab_w_out", 0), ("w_ffn_in", 0)),
    (("w_ffn_out", 0),),
    (("c_w_qkv", 0),),
    (("c_w_out", 0), ("w_ffn_in", 1), ("w_ffn_out", 1), ("ab_w_in", 1)),
    (("ab_w_out", 1), ("w_ffn_in", 2), ("w_ffn_out", 2), ("c_w_qkv", 1)),
    (("c_w_out", 1), ("w_ffn_in", 3), ("w_ffn_out", 3)),
)


def _local_step(x, target, small, comm):
    s = x.shape[0]
    tabs = _retention_tables(s)
    saved, w = [], comm.weight
    for layer in range(DEPTH):
        i = layer // 2
        sv = {"x0": x}
        g_mix = small["mix_norm"][layer:layer + 1]
        if layer % 2 == 0:
            at = (x,) + tuple(tabs.values()) if layer == 0 else x
            sv["h1"], sv["z"] = _norm_mm(x, g_mix, w("ab_w_in", i, at), AB_IN_WIDTH, F32, False, "ab_in_fwd")
            gn = small["ab_gn_gain"][i:i + 1]
            sv["o_pre"], sv["states"], ret = _ret_fwd(sv["z"], tabs, gn, "ret_fwd")
            pool = _pool_fwd(sv["z"], small["ab_w_pool"][i], small["ab_pool_scale"][i:i + 1], "pool_fwd")
            sv["u"] = (ret, pool)
            mixed = ([ret, pool], w("ab_w_out", i, ret))
        else:
            sv["h1"], sv["z"] = _norm_mm(x, g_mix, w("c_w_qkv", i, x), 3 * D_MODEL // N_DEV, BF16, False, "qkv_fwd")
            rb = jnp.pad(small["c_rel_bias"][i], ((0, 0), (0, REL_PAD - N_REL)))
            sv["bias"] = _attn_bias(rb, "attn_bias")
            sv["o"] = _attn_fwd(sv["z"], sv["bias"], "attn_fwd")
            mixed = ([sv["o"]], w("c_w_out", i, sv["o"]))
        x, sv["h2"], sv["z1"], sv["a"] = _norm_mm(x, small["ffn_norm"][layer:layer + 1], w("w_ffn_in", layer, mixed[0][0]),
                                                  D_FF // N_DEV, BF16, True, "ffn_in_fwd", mixed)
        sv["x1"] = x
        x = _mm_res([sv["a"]], w("w_ffn_out", layer, sv["a"]), x, "ffn_out_fwd")
        saved.append(sv)

    loss, dx, d_final = _final_loss(x, small["final_norm"][None, :], target, "final_loss")

    gs = {k: [None] * DEPTH for k in ("mix_norm", "ffn_norm")}
    for k in ("ab_gn_gain", "ab_w_pool", "ab_pool_scale", "c_rel_bias"):
        gs[k] = [None] * (DEPTH // 2)
    gs["final_norm"] = d_final[0]
    sent = None
    for layer in reversed(range(DEPTH)):
        i = layer // 2
        sv = saved[layer]
        dz1 = _mm_nt_rows(dx, w("w_ffn_out", layer), sv["z1"], "ffn_out_bwd")
        gw = {("w_ffn_out", layer): _mm_tn(sv["a"], dx, 1024, D_MODEL, True, (0, 2), "ffn_out_dw"),
              ("w_ffn_in", layer): _mm_tn(sv["h2"], dz1, D_MODEL, 1024, False, (1, 2), "ffn_in_dw",
                                          comm.after() if layer == 0 else None)}
        w_out = w("ab_w_out" if layer % 2 == 0 else "c_w_out", i)
        dx, dg, du = _mm_nt_normbwd(dz1, w("w_ffn_in", layer), sv["x1"], small["ffn_norm"][layer:layer + 1], dx,
                                    "ffn_in_bwd", sent, w_out)
        gs["ffn_norm"][layer] = dg[0]
        sent = None
        g_mix = small["mix_norm"][layer:layer + 1]
        if layer % 2 == 0:
            gw["ab_w_out", i] = _mm_tn(sv["u"], dx, D_MODEL, D_MODEL, True, (0, N_DEV), "mix_out_dw")
            if layer == 0:
                sent, gw = comm.send(gw), {}
            gn = small["ab_gn_gain"][i:i + 1]
            dz, dgn = _ret_bwd(sv["z"], tabs, gn, sv["o_pre"], sv["states"], du, "ret_bwd")
            dz, dwp, dsc = _pool_bwd(sv["z"], small["ab_w_pool"][i], small["ab_pool_scale"][i:i + 1], du, dz, "pool_bwd")
            gs["ab_gn_gain"][i], gs["ab_w_pool"][i], gs["ab_pool_scale"][i] = dgn[0], dwp, dsc[0]
            gw["ab_w_in", i] = _to_shard_major(_mm_tn(sv["h1"], dz, D_MODEL, AB_IN_WIDTH // 2, False, (1, 1), "ab_in_dw",
                                                      comm.after()), pairs_split=True)
            dx, dg = _mm_nt_normbwd(dz, w("ab_w_in", i), sv["x0"], g_mix, dx, "ab_in_bwd", sent)
        else:
            do = du
            gw["c_w_out", i] = _mm_tn(sv["o"], dx, D_MODEL, D_MODEL, True, (0, N_DEV), "mix_out_dw")
            dq, dk, dv, dbias = _attn_bwd(sv["z"], sv["bias"], sv["o"], do, "attn_bwd")
            gs["c_rel_bias"][i] = _attn_dbias(dbias, "attn_dbias")[:, :N_REL]
            tiles = [_mm_tn(sv["h1"], part, D_MODEL, D_MODEL, False, (1, 1), "qkv_dw", after)
                     for part, after in ((dq, None), (dk, None), (dv, comm.after()))]
            gw["c_w_qkv", i] = _to_shard_major(jnp.concatenate(tiles, axis=0))
            dx, dg = _mm_nt_normbwd([dq, dk, dv], w("c_w_qkv", i), sv["x0"], g_mix, dx, "qkv_bwd", sent)
        gs["mix_norm"][layer] = dg[0]
        if layer > 0:
            sent = comm.send(gw)
    gsmall = {k: (jnp.stack(v) if isinstance(v, list) else v) for k, v in gs.items()}
    return loss, dx, gw, gsmall


BIG = ("w_ffn_in", "w_ffn_out", "ab_w_in", "ab_w_out", "c_w_qkv", "c_w_out")
SMALL = ("mix_norm", "ffn_norm", "ab_gn_gain", "ab_w_pool", "ab_pool_scale", "c_rel_bias", "final_norm")
N_PEERS = N_DEV - 1
FLIPS = [(fx, fy, fc) for fx in (0, 1) for fy in (0, 1) for fc in (0, 1)][1:]


def _peers():
    x, y, c = (lax.axis_index(a) for a in MESH_AXES)
    peers = []
    for fx, fy, fc in FLIPS:
        px, py, pc = (1 - x if fx else x), (1 - y if fy else y), (1 - c if fc else c)
        peers.append(((px, py, pc), 4 * px + 2 * py + pc))
    return 4 * x + 2 * y + c, peers


def _exchange(srcs, by_slot, name, collective_id):
    n = len(srcs)
    src_refs = [jax.new_ref(a, memory_space=pltpu.MemorySpace.HBM) for a in srcs]
    land_refs = [jax.empty_ref(SDS((N_DEV,) + (a.shape[1:] if slotted else a.shape), a.dtype),
                               memory_space=pltpu.MemorySpace.HBM) for a, slotted in zip(srcs, by_slot)]

    @pl.kernel(mesh=plsc.ScalarSubcoreMesh(axis_name="sequencer", num_cores=1), name=name,
               scratch_types=(pltpu.SemaphoreType.DMA((n * N_PEERS,)), pltpu.SemaphoreType.DMA((n * N_PEERS,)),
                              pltpu.SemaphoreType.DMA((n,))),
               compiler_params=pltpu.CompilerParams(collective_id=collective_id))
    def launch(send_sems, recv_sems, local_sems):
        me, peers = _peers()
        barrier = pltpu.get_barrier_semaphore()
        for pos, _ in peers:
            pl.semaphore_signal(barrier, inc=1, device_id=pos, device_id_type=pl.DeviceIdType.MESH)
        pl.semaphore_wait(barrier, N_PEERS)
        waits = []
        for k in range(n):
            own = pltpu.make_async_copy(src_refs[k].at[me] if by_slot[k] else src_refs[k], land_refs[k].at[me],
                                        local_sems.at[k])
            own.start()
            waits.append(own.wait)
            for rel, (pos, slot) in enumerate(peers):
                src = src_refs[k].at[slot] if by_slot[k] else src_refs[k]
                sems = dict(send_sem=send_sems.at[k * N_PEERS + rel], recv_sem=recv_sems.at[k * N_PEERS + rel],
                            device_id=pos, device_id_type=pl.DeviceIdType.MESH)
                send = pltpu.make_async_remote_copy(src_ref=src, dst_ref=land_refs[k].at[me], **sems)
                send.start()
                arrival = pltpu.make_async_remote_copy(src_ref=src, dst_ref=land_refs[k].at[slot], **sems)
                waits += [send.wait_send, arrival.wait_recv]
        for wait in waits:
            wait()

    launch()
    return [r[...] for r in land_refs]


def _cast_group(weights, keys, after, name):
    after = _follow(after)

    def body(*refs):
        n = len(keys)
        for i_ref, o_ref in zip(refs[:n], refs[n + len(after):]):
            o_ref[...] = _bf(i_ref[...])

    def layer_spec(shape, l):
        return pl.BlockSpec((None,) + shape[1:], lambda i: (l, 0, 0))

    whole = lambda shape: pl.BlockSpec(shape, lambda i: (0, 0))
    ins = [weights[k] for k, _ in keys]
    return pl.pallas_call(
        body, name=name, grid=(1,), out_shape=[SDS(w.shape[1:], BF16) for w in ins],
        in_specs=[layer_spec(w.shape, l) for w, (_, l) in zip(ins, keys)] + [FOLLOW] * len(after),
        out_specs=[whole(w.shape[1:]) for w in ins],
        compiler_params=_params(("arbitrary",)),
    )(*ins, *after)


def _to_shard_major(g, pairs_split=False):
    nj, ka, nb = g.shape
    full = jnp.transpose(g, (1, 0, 2)).reshape(ka, nj * nb)
    if pairs_split:
        full = _split_pairs(full, inverse=True)
    return jnp.transpose(full.reshape(ka, N_DEV, nj * nb // N_DEV), (1, 0, 2))


def _from_gathered(name, g):
    if name in ("w_ffn_out", "ab_w_out", "c_w_out"):
        return g.reshape(g.shape[0] * g.shape[1], g.shape[2])
    if name == "ab_w_in":
        return _split_pairs(jnp.transpose(g, (1, 0, 2)).reshape(1, g.shape[1], N_DEV * g.shape[2]))
    return g


class _Comm:
    def __init__(self, weights):
        self.weights_f32 = weights
        self.gathered = {}
        self.got = {}
        self.calls = 0
        self.ended = []
        self.opened = -1

    def _exchange(self, srcs, by_slot, name):
        self.calls += 1
        got = _exchange(srcs, by_slot, name, self.calls)
        self.ended = got[:1]
        return got

    def _gather(self, group, at):
        keys = FWD_GROUPS[group]
        shards = _cast_group(self.weights_f32, keys, self.ended + _follow(at), "cast_%d" % group)
        got = self._exchange(shards, [False] * len(keys), "gather_%d" % group)
        self.gathered.update((k, _from_gathered(k[0], arr)) for k, arr in zip(keys, got))

    def weight(self, name, layer, at=None):
        if (name, layer) not in self.gathered:
            self._gather(0, at[0] if isinstance(at, tuple) else at)
        group = next(g for g, keys in enumerate(FWD_GROUPS) if (name, layer) in keys)
        if group == self.opened + 1:
            self.opened = group
            if group + 1 < len(FWD_GROUPS):
                self._gather(group + 1, at)
        return self.gathered[name, layer]

    def after(self):
        return self.ended

    def send(self, grads, shared=None):
        shared = shared or {}
        keys = list(grads) + list(shared)
        srcs = list(grads.values()) + list(shared.values())
        got = self._exchange(srcs, [True] * len(grads) + [False] * len(shared), "scatter_%d" % self.calls)
        self.got.update(zip(keys, got))
        return list(grads.values())

    def received(self):
        return self.got


def _adamw_math(g, w, m, v):
    m2 = ADAM_B1 * m + (1.0 - ADAM_B1) * g
    v2 = ADAM_B2 * v + (1.0 - ADAM_B2) * jnp.square(g)
    m_hat = m2 / (1.0 - ADAM_B1 ** ADAM_STEP)
    v_hat = v2 / (1.0 - ADAM_B2 ** ADAM_STEP)
    delta = -ADAM_LR * (m_hat / (jnp.sqrt(v_hat) + ADAM_EPS) + ADAM_WD * w)
    return delta, m2, v2


def _adamw(recv, w, m, v, name):
    nl, r, c = w.shape
    tr = _tile(r, 256)

    def body(*refs):
        g_refs = refs[:nl]
        w_ref, m_ref, v_ref, go_ref, d_ref, mo_ref, vo_ref = refs[nl:]
        for l in range(nl):
            @pl.when(pl.program_id(0) == l)
            def _():
                g = g_refs[l][0].astype(F32)
                for p in range(1, N_DEV):
                    g = g + g_refs[l][p].astype(F32)
                go_ref[...] = g
                d_ref[...], mo_ref[...], vo_ref[...] = _adamw_math(g, w_ref[...], m_ref[...], v_ref[...])

    def recv_spec(l):
        return pl.BlockSpec((N_DEV, tr, c), lambda layer, i: (0, jnp.where(layer == l, i, 0), 0))

    blk = pl.BlockSpec((None, tr, c), lambda l, i: (l, i, 0))
    return pl.pallas_call(
        body, name=name, grid=(nl, r // tr), out_shape=[SDS(w.shape, F32)] * 4,
        in_specs=[recv_spec(l) for l in range(nl)] + [blk, blk, blk],
        out_specs=[blk] * 4,
        compiler_params=_params(("arbitrary", "arbitrary")),
    )(*recv, w, m, v)


def _adamw_small(recv, loss_parts, w, m, v, name):
    n = len(w)

    def total(ref):
        t = ref[0]
        for p in range(1, N_DEV):
            t = t + ref[p]
        return t

    def body(*refs):
        g_refs, loss_ref = refs[:n], refs[n]
        w_refs, m_refs, v_refs = refs[n + 1:2 * n + 1], refs[2 * n + 1:3 * n + 1], refs[3 * n + 1:4 * n + 1]
        outs = refs[4 * n + 1:]
        for i in range(n):
            g = total(g_refs[i])
            outs[4 * i][...] = g
            outs[4 * i + 1][...], outs[4 * i + 2][...], outs[4 * i + 3][...] = _adamw_math(
                g, w_refs[i][...], m_refs[i][...], v_refs[i][...])
        outs[4 * n][...] = total(loss_ref)

    out_shape = [SDS(p.shape, F32) for p in w for _ in range(4)] + [SDS(loss_parts.shape[1:], F32)]
    outs = pl.pallas_call(body, name=name, out_shape=out_shape,
                          compiler_params=_params(None))(*recv, loss_parts, *w, *m, *v)
    return [outs[4 * i:4 * i + 4] for i in range(n)], outs[-1]


def kernel(x, mix_norm, ffn_norm, w_ffn_in, w_ffn_out, ab_w_in, ab_gn_gain, ab_w_pool, ab_pool_scale, ab_w_out, c_w_qkv, c_rel_bias, c_w_out, final_norm, loss_target, m_mix_norm, m_ffn_norm, m_w_ffn_in, m_w_ffn_out, m_ab_w_in, m_ab_gn_gain, m_ab_w_pool, m_ab_pool_scale, m_ab_w_out, m_c_w_qkv, m_c_rel_bias, m_c_w_out, m_final_norm, v_mix_norm, v_ffn_norm, v_w_ffn_in, v_w_ffn_out, v_ab_w_in, v_ab_gn_gain, v_ab_w_pool, v_ab_pool_scale, v_ab_w_out, v_c_w_qkv, v_c_rel_bias, v_c_w_out, v_final_norm):
    args = dict(locals())
    weights = {k: args[k] for k in BIG + SMALL}
    moments_m = {k: args["m_" + k] for k in BIG + SMALL}
    moments_v = {k: args["v_" + k] for k in BIG + SMALL}

    small = {k: weights[k] for k in SMALL}
    rows = lambda a: a.reshape(1, -1) if a.ndim == 1 else a

    comm = _Comm(weights)
    loss, dx, last_grads, gsmall = _local_step(x[0], loss_target[0], small, comm)
    comm.send(last_grads, {**{k: rows(gsmall[k]) for k in SMALL}, "loss": loss})
    recv = comm.received()

    outs = {}
    for k in BIG:
        layers = [recv[k, l] for l in range(weights[k].shape[0])]
        outs[k] = _adamw(layers, weights[k], moments_m[k], moments_v[k], "adamw_" + k)
    updated, total = _adamw_small([recv[k] for k in SMALL], recv["loss"], [rows(small[k]) for k in SMALL],
                                  [rows(moments_m[k]) for k in SMALL], [rows(moments_v[k]) for k in SMALL], "adamw_small")
    for k, parts in zip(SMALL, updated):
        outs[k] = [p.reshape(small[k].shape) for p in parts]

    order = SMALL[:2] + BIG[:2] + ("ab_w_in", "ab_gn_gain", "ab_w_pool", "ab_pool_scale", "ab_w_out",
                                   "c_w_qkv", "c_rel_bias", "c_w_out", "final_norm")
    result = [total[0, 0], dx[None]]
    for part in range(4):
        result += [outs[k][part] for k in order]
    return tuple(result)
```

```python
import functools

import jax
import jax.numpy as jnp
from jax import lax
from jax.experimental import pallas as pl
from jax.experimental.pallas import tpu as pltpu
from jax.experimental.pallas import tpu_sc as plsc

F32 = jnp.float32
BF16 = jnp.bfloat16
SDS = jax.ShapeDtypeStruct
MESH_AXES = ("x", "y", "c")
N_DEV = 8

D_MODEL = 1024
DEPTH = 4
CHUNK = 64
D_FF = 4 * D_MODEL
RMS_EPS = 1e-6
RET_WIDTH = 512
RET_HEADS = 4
RET_HEAD_DIM = 128
RET_ROPE_BASE = 10000.0
GN_EPS = 1e-5
POOL_WIDTH = 512
POOL_WINDOWS = (2, 4, 8, 16)
POOL_HALO = 16
AB_IN_WIDTH = 4 * RET_WIDTH + POOL_WIDTH
ATT_HEADS = 16
ATT_HEAD_DIM = 64
LEFT_CHUNKS = 8
REL_CLIP = 128
N_REL = 2 * REL_CLIP + 1
NEG_INF = -1e30

ADAM_LR = 0.001
ADAM_B1 = 0.9
ADAM_B2 = 0.999
ADAM_EPS = 1e-08
ADAM_WD = 0.01
ADAM_STEP = 10

TOKEN_TILE = 512
ATT_Q_TILE = 256
ATT_BACK = LEFT_CHUNKS * CHUNK // ATT_Q_TILE
ATT_K_TILE = (ATT_BACK + 1) * ATT_Q_TILE
ATT_PAIRS = 4
ATT_DIAG = 1024
REL_PAD = 384
VMEM_LIMIT_MB = 56

NT = (((1,), (1,)), ((), ()))
TN = (((0,), (0,)), ((), ()))


def _params(semantics, **kw):
    return pltpu.CompilerParams(dimension_semantics=semantics,
                                vmem_limit_bytes=VMEM_LIMIT_MB * 2 ** 20, **kw)


def _dot(a, b, dims=None):
    if dims is None:
        return jnp.dot(a, b, preferred_element_type=F32)
    return lax.dot_general(a, b, dims, preferred_element_type=F32)


def _bf(v):
    return v.astype(BF16)


def _tile(n, t):
    return min(n, t)


FOLLOW = pl.BlockSpec(memory_space=pl.ANY)


def _follow(after):
    return [] if after is None else list(after) if isinstance(after, (list, tuple)) else [after]


MXU_WIDTH = 256


def _mxu_group(nj, tn):
    return 2 if tn % MXU_WIDTH and (2 * tn) % MXU_WIDTH == 0 and nj % 2 == 0 else 1


def _w_tiles(w_ref, j, group):
    return w_ref[j] if group == 1 else jnp.concatenate([w_ref[j + t] for t in range(group)], axis=1)


def _w_cols(w_ref, j, group, c, width):
    return w_ref[j, :, c:c + width] if group == 1 else _w_tiles(w_ref, j, group)


def _norm_mm(x, gain, w, tn, z_dtype, relu2, name, pre=None):
    s, d = x.shape
    nj = w.shape[0]
    tm = _tile(s, TOKEN_TILE)
    group = _mxu_group(nj, tn)
    pre_parts, w_pre = pre if pre else ([], None)
    widths = [p.shape[1] for p in pre_parts]

    def body(*refs):
        p_refs = refs[:len(pre_parts)]
        refs = refs[len(pre_parts):]
        if pre:
            wp_ref, x_ref, g_ref, w_ref, x1_ref, h_ref, z_ref, *a_ref = refs
            xv, off = x_ref[...], 0
            for p_ref, k in zip(p_refs, widths):
                xv = xv + _dot(p_ref[...], wp_ref[off:off + k, :])
                off += k
            x1_ref[...] = xv
        else:
            x_ref, g_ref, w_ref, h_ref, z_ref, *a_ref = refs
            xv = x_ref[...]
        r = lax.rsqrt(jnp.mean(xv * xv, axis=-1, keepdims=True) + RMS_EPS)
        h = _bf(xv * r * g_ref[...])
        h_ref[...] = h
        cw = tn if tn <= 512 else 512
        for j in range(0, nj, group):
            for c in range(0, tn, cw):
                z = _dot(h, _w_cols(w_ref, j, group, c, cw))
                cols = slice(j * tn + c, j * tn + c + group * cw)
                z_ref[:, cols] = z.astype(z_ref.dtype)
                if relu2:
                    a_ref[0][:, cols] = _bf(jnp.square(jnp.maximum(z, 0.0)))

    n = nj * tn
    rows = lambda width: pl.BlockSpec((tm, width), lambda i: (i, 0))
    out_shape = [SDS((s, d), F32)] * bool(pre) + [SDS((s, d), BF16), SDS((s, n), z_dtype)] + [SDS((s, n), BF16)] * relu2
    out_specs = [rows(d)] * bool(pre) + [rows(d), rows(n)] + [rows(n)] * relu2
    return pl.pallas_call(
        body, name=name, grid=(s // tm,), out_shape=out_shape,
        in_specs=[rows(k) for k in widths] + ([pl.BlockSpec(w_pre.shape, lambda i: (0, 0))] if pre else [])
        + [rows(d), pl.BlockSpec((1, d), lambda i: (0, 0)), pl.BlockSpec((nj, d, tn), lambda i: (0, 0, 0))],
        out_specs=out_specs,
        compiler_params=_params(("parallel",)),
    )(*pre_parts, *([w_pre] if pre else []), x, gain, w)


def _mm_res(parts, w, res, name):
    s, d = res.shape
    tm = _tile(s, TOKEN_TILE)
    widths = [p.shape[1] for p in parts]

    def body(*refs):
        a_refs = refs[:len(parts)]
        w_ref, res_ref, o_ref = refs[len(parts):]
        acc = res_ref[...]
        off = 0
        for a_ref, k in zip(a_refs, widths):
            acc = acc + _dot(a_ref[...], w_ref[off:off + k, :])
            off += k
        o_ref[...] = acc

    return pl.pallas_call(
        body, name=name, grid=(s // tm,), out_shape=SDS((s, d), F32),
        in_specs=[pl.BlockSpec((tm, k), lambda i: (i, 0)) for k in widths]
        + [pl.BlockSpec(w.shape, lambda i: (0, 0)), pl.BlockSpec((tm, d), lambda i: (i, 0))],
        out_specs=pl.BlockSpec((tm, d), lambda i: (i, 0)),
        compiler_params=_params(("parallel",)),
    )(*parts, w, res)


def _mm_nt_rows(dy, w, z, name):
    s, d = dy.shape
    k = w.shape[0]
    tm = _tile(s, TOKEN_TILE)
    tk = _tile(k, 1024)

    def body(dy_ref, w_ref, *rest):
        o_ref = rest[-1]
        dyb = _bf(dy_ref[...])
        for j in range(k // tk):
            cols = slice(j * tk, (j + 1) * tk)
            da = _dot(dyb, w_ref[cols, :], NT)
            if z is not None:
                da = da * (2.0 * jnp.maximum(rest[0][:, cols].astype(F32), 0.0))
            o_ref[:, cols] = _bf(da)

    in_specs = [pl.BlockSpec((tm, d), lambda i: (i, 0)), pl.BlockSpec((k, d), lambda i: (0, 0))]
    args = [dy, w]
    if z is not None:
        in_specs.append(pl.BlockSpec((tm, k), lambda i: (i, 0)))
        args.append(z)
    return pl.pallas_call(
        body, name=name, grid=(s // tm,), out_shape=SDS((s, k), BF16),
        in_specs=in_specs, out_specs=pl.BlockSpec((tm, k), lambda i: (i, 0)),
        compiler_params=_params(("parallel",)),
    )(*args)


def _w_range(w_ref, c0, c1):
    nc = w_ref.shape[2]
    pieces, c = [], c0
    while c < c1:
        j = c // nc
        hi = min(nc, c1 - j * nc)
        pieces.append(w_ref[j, :, c - j * nc:hi])
        c = j * nc + hi
    return pieces[0] if len(pieces) == 1 else jnp.concatenate(pieces, axis=1)


def _mm_nt_normbwd(dz, w, x, gain, dres, name, after=None, w_post=None):
    parts, after = list(dz) if isinstance(dz, (list, tuple)) else [dz], _follow(after)
    widths = [p.shape[1] for p in parts]
    s, d = x.shape
    tm = _tile(s, TOKEN_TILE)
    chunk = 2 * MXU_WIDTH
    halves = 2 if tm % 32 == 0 else 1

    def body(*refs):
        dz_refs = refs[:len(parts)]
        w_ref, x_ref, g_ref, dres_ref = refs[len(parts):len(parts) + 4]
        dx_ref, dg_ref = refs[-2 - has_post:][:2]

        @pl.when(pl.program_id(0) == 0)
        def _():
            dg_ref[...] = jnp.zeros_like(dg_ref)

        for half in range(halves):
            rows = slice(half * tm // halves, (half + 1) * tm // halves)
            dh, base = None, 0
            for dz_ref, width in zip(dz_refs, widths):
                for c in range(0, width, chunk):
                    term = _dot(dz_ref[rows, c:c + chunk], _w_range(w_ref, base + c, base + c + chunk), NT)
                    dh = term if dh is None else dh + term
                base += width
            xv = x_ref[rows, :]
            r = lax.rsqrt(jnp.mean(xv * xv, axis=-1, keepdims=True) + RMS_EPS)
            xn = xv * r
            dg_ref[...] += jnp.sum(dh * xn, axis=0, keepdims=True)
            dxh = dh * g_ref[...]
            dx = dres_ref[rows, :] + r * (dxh - xn * jnp.mean(dxh * xn, axis=-1, keepdims=True))
            dx_ref[rows, :] = dx
            if has_post:
                refs[-1][rows, :] = _bf(_dot(_bf(dx), refs[len(parts) + 4][...], NT))

    has_post = w_post is not None
    post_in = [pl.BlockSpec(w_post.shape, lambda i: (0, 0))] if has_post else []
    post_out = [pl.BlockSpec((tm, w_post.shape[0]), lambda i: (i, 0))] if has_post else []
    return pl.pallas_call(
        body, name=name, grid=(s // tm,),
        out_shape=[SDS((s, d), F32), SDS((1, d), F32)] + ([SDS((s, w_post.shape[0]), BF16)] if has_post else []),
        in_specs=[pl.BlockSpec((tm, width), lambda i: (i, 0)) for width in widths]
        + [pl.BlockSpec(w.shape, lambda i: (0, 0, 0)),
           pl.BlockSpec((tm, d), lambda i: (i, 0)),
           pl.BlockSpec((1, d), lambda i: (0, 0)),
           pl.BlockSpec((tm, d), lambda i: (i, 0))] + post_in + [FOLLOW] * len(after),
        out_specs=[pl.BlockSpec((tm, d), lambda i: (i, 0)), pl.BlockSpec((1, d), lambda i: (0, 0))] + post_out,
        compiler_params=_params(("arbitrary",)),
    )(*parts, w, x, gain, dres, *([w_post] if has_post else []), *after)


def _mm_tn(a, b, ka, nb, a_tiled, split, name, after=None):
    a_parts, after = list(a) if isinstance(a, (list, tuple)) else [a], _follow(after)
    s = a_parts[0].shape[0]
    tm = _tile(s, 4 * TOKEN_TILE)
    nm = s // tm
    nj = a_parts[0].shape[1] // ka if a_tiled and len(a_parts) == 1 else (1 if a_tiled else b.shape[1] // nb)
    axis, parts = split
    pr, pc = (ka // parts, nb) if axis == 0 else (ka, nb // parts)

    def body(*refs):
        a_refs, b_ref = refs[:len(a_parts)], refs[len(a_parts)]
        o_ref, acc = refs[-2:]
        m = pl.program_id(1)

        @pl.when(m == 0)
        def _():
            acc[...] = jnp.zeros_like(acc)

        av = a_refs[0][...] if len(a_refs) == 1 else jnp.concatenate([r[...] for r in a_refs], axis=1)
        acc[...] += _dot(_bf(av), _bf(b_ref[...]), TN)

        @pl.when(m == nm - 1)
        def _():
            for q in range(parts):
                piece = acc[q * pr:(q + 1) * pr, :] if axis == 0 else acc[:, q * pc:(q + 1) * pc]
                o_ref[q] = piece.astype(o_ref.dtype)

    return pl.pallas_call(
        body, name=name, grid=(nj, nm), out_shape=SDS((nj * parts, pr, pc), BF16),
        in_specs=([pl.BlockSpec((tm, ka), (lambda j, m: (m, j)) if a_tiled else (lambda j, m: (m, 0)))]
                  if len(a_parts) == 1 else [pl.BlockSpec((tm, p.shape[1]), lambda j, m: (m, 0)) for p in a_parts])
        + [pl.BlockSpec((tm, nb), (lambda j, m: (m, 0)) if a_tiled else (lambda j, m: (m, j)))]
        + [FOLLOW] * len(after),
        out_specs=pl.BlockSpec((parts, pr, pc), lambda j, m: (j, 0, 0)),
        scratch_shapes=[pltpu.VMEM((ka, nb), F32)],
        compiler_params=_params(("parallel", "arbitrary")),
    )(*a_parts, b, *after)


def _final_loss(x, gain, target, name):
    s, d = x.shape
    tm = _tile(s, TOKEN_TILE)

    def body(x_ref, g_ref, t_ref, loss_ref, dx_ref, dg_ref):
        @pl.when(pl.program_id(0) == 0)
        def _():
            loss_ref[...] = jnp.zeros_like(loss_ref)
            dg_ref[...] = jnp.zeros_like(dg_ref)

        xv = x_ref[...]
        r = lax.rsqrt(jnp.mean(xv * xv, axis=-1, keepdims=True) + RMS_EPS)
        xn = xv * r
        err = xn * g_ref[...] - t_ref[...]
        loss_ref[...] += (0.5 / d) * jnp.sum(err * err)
        dy = err * (1.0 / d)
        dg_ref[...] += jnp.sum(dy * xn, axis=0, keepdims=True)
        dxh = dy * g_ref[...]
        dx_ref[...] = r * (dxh - xn * jnp.mean(dxh * xn, axis=-1, keepdims=True))

    return pl.pallas_call(
        body, name=name, grid=(s // tm,),
        out_shape=[SDS((8, 128), F32), SDS((s, d), F32), SDS((1, d), F32)],
        in_specs=[pl.BlockSpec((tm, d), lambda i: (i, 0)), pl.BlockSpec((1, d), lambda i: (0, 0)),
                  pl.BlockSpec((tm, d), lambda i: (i, 0))],
        out_specs=[pl.BlockSpec((8, 128), lambda i: (0, 0)), pl.BlockSpec((tm, d), lambda i: (i, 0)),
                   pl.BlockSpec((1, d), lambda i: (0, 0))],
        compiler_params=_params(("arbitrary",)),
    )(x, gain, target)


def _retention_tables(s):
    half = RET_HEAD_DIM // 2
    inv_freq = 1.0 / (RET_ROPE_BASE ** jnp.linspace(0.0, 1.0, half, dtype=F32))
    ang = jnp.arange(s, dtype=F32)[:, None] * inv_freq[None, :]
    cos, sin = jnp.cos(ang), jnp.sin(ang)
    cos_e = jnp.concatenate([cos, cos], axis=-1)
    sin_s = jnp.concatenate([-sin, sin], axis=-1)
    log_g = jnp.log1p(-jnp.power(2.0, -5.0 - jnp.arange(RET_HEADS, dtype=F32)))
    pos = jnp.arange(CHUNK, dtype=F32)
    dmat = jnp.exp(jnp.abs(pos[:, None] - pos[None, :])[None] * log_g[:, None, None])
    qdec = jnp.exp((pos[None, :] + 1.0) * log_g[:, None])
    kdec = jnp.exp((CHUNK - 1.0 - pos[None, :]) * log_g[:, None])
    lam = jnp.exp(CHUNK * log_g)
    wide = (RET_HEADS, CHUNK, RET_HEAD_DIM)
    return dict(cos=cos_e, sin=sin_s, dmat=dmat,
                qdec=jnp.broadcast_to(qdec[:, :, None], wide),
                kdec=jnp.broadcast_to(kdec[:, :, None], wide),
                lam=jnp.broadcast_to(lam[:, None, None], (RET_HEADS, RET_HEAD_DIM, RET_HEAD_DIM)))


def _swap_pairs(t):
    return pltpu.roll(t, RET_HEAD_DIM // 2, 1)


def _split_pairs(w, inverse=False):
    lead, nqk = w.shape[:-1], 2 * RET_WIDTH
    shape = (2 * RET_HEADS, 2, RET_HEAD_DIM // 2) if inverse else (2 * RET_HEADS, RET_HEAD_DIM // 2, 2)
    qk = jnp.swapaxes(w[..., :nqk].reshape(lead + shape), -1, -2).reshape(lead + (nqk,))
    return jnp.concatenate([qk, w[..., nqk:]], axis=-1)


def _head(h):
    return slice(h * RET_HEAD_DIM, (h + 1) * RET_HEAD_DIM)


def _ret_common_specs(tb, blk):
    zs = [pl.BlockSpec((tb, RET_WIDTH), functools.partial(lambda j, i: (blk(i), j), j)) for j in range(4)]
    tabs = [pl.BlockSpec((tb, RET_HEAD_DIM), lambda i: (blk(i), 0))] * 2
    consts = [pl.BlockSpec((1, RET_WIDTH), lambda i: (0, 0)),
              pl.BlockSpec((RET_HEADS, CHUNK, CHUNK), lambda i: (0, 0, 0)),
              pl.BlockSpec((RET_HEADS, CHUNK, RET_HEAD_DIM), lambda i: (0, 0, 0)),
              pl.BlockSpec((RET_HEADS, CHUNK, RET_HEAD_DIM), lambda i: (0, 0, 0)),
              pl.BlockSpec((RET_HEADS, RET_HEAD_DIM, RET_HEAD_DIM), lambda i: (0, 0, 0))]
    return zs + tabs + consts


def _ret_fwd(z, tabs, gn_gain, name):
    s = z.shape[0]
    tb = _tile(s, TOKEN_TILE)
    ncb = tb // CHUNK
    scale = RET_HEAD_DIM ** -0.5

    def body(q_ref, k_ref, v_ref, g_ref, cos_ref, sin_ref, gain_ref, dm_ref, qd_ref, kd_ref, lam_ref,
             o_ref, st_ref, ret_ref, s_scr, qr_scr, kr_scr):
        @pl.when(pl.program_id(0) == 0)
        def _():
            s_scr[...] = jnp.zeros_like(s_scr)

        cosv, sinv = cos_ref[...], sin_ref[...]
        for h in range(RET_HEADS):
            qh, kh = q_ref[:, _head(h)], k_ref[:, _head(h)]
            qr_scr[:, _head(h)] = qh * cosv + _swap_pairs(qh) * sinv
            kr_scr[:, _head(h)] = (kh * cosv + _swap_pairs(kh) * sinv) * scale

        def chunk(c, carry):
            rows = pl.ds(pl.multiple_of(c * CHUNK, CHUNK), CHUNK)
            for h in range(RET_HEADS):
                qc, kc, vc = qr_scr[rows, _head(h)], kr_scr[rows, _head(h)], v_ref[rows, _head(h)]
                a = _dot(_bf(qc), _bf(kc), NT) * dm_ref[h]
                st = s_scr[h]
                st_ref[c, h] = st
                o_ref[rows, _head(h)] = _dot(_bf(a), _bf(vc)) + _dot(_bf(qc * qd_ref[h]), _bf(st))
                s_scr[h] = st * lam_ref[h] + _dot(_bf(kc * kd_ref[h]), _bf(vc), TN)
            return carry

        lax.fori_loop(0, ncb, chunk, 0, unroll=True)
        for h in range(RET_HEADS):
            o = o_ref[:, _head(h)]
            mu = jnp.mean(o, axis=-1, keepdims=True)
            oc = o - mu
            y = oc * lax.rsqrt(jnp.mean(oc * oc, axis=-1, keepdims=True) + GN_EPS) * gain_ref[:, _head(h)]
            g = g_ref[:, _head(h)]
            ret_ref[:, _head(h)] = _bf(g / (1.0 + jnp.exp(-g)) * y)

    nc = s // CHUNK
    return pl.pallas_call(
        body, name=name, grid=(s // tb,),
        out_shape=[SDS((s, RET_WIDTH), F32), SDS((nc, RET_HEADS, RET_HEAD_DIM, RET_HEAD_DIM), F32),
                   SDS((s, RET_WIDTH), BF16)],
        in_specs=_ret_common_specs(tb, lambda i: i),
        out_specs=[pl.BlockSpec((tb, RET_WIDTH), lambda i: (i, 0)),
                   pl.BlockSpec((ncb, RET_HEADS, RET_HEAD_DIM, RET_HEAD_DIM), lambda i: (i, 0, 0, 0)),
                   pl.BlockSpec((tb, RET_WIDTH), lambda i: (i, 0))],
        scratch_shapes=[pltpu.VMEM((RET_HEADS, RET_HEAD_DIM, RET_HEAD_DIM), F32),
                        pltpu.VMEM((tb, RET_WIDTH), F32), pltpu.VMEM((tb, RET_WIDTH), F32)],
        compiler_params=_params(("arbitrary",)),
    )(z, z, z, z, tabs["cos"], tabs["sin"], gn_gain, tabs["dmat"], tabs["qdec"], tabs["kdec"], tabs["lam"])


def _ret_bwd(z, tabs, gn_gain, o_pre, states, du, name):
    s = z.shape[0]
    tb = _tile(s, TOKEN_TILE)
    ncb = tb // CHUNK
    nblk = s // tb
    scale = RET_HEAD_DIM ** -0.5
    rev = lambda i: nblk - 1 - i

    def body(q_ref, k_ref, v_ref, g_ref, cos_ref, sin_ref, gain_ref, dm_ref, qd_ref, kd_ref, lam_ref,
             o_ref, st_ref, dret_ref, dz_ref, dgain_ref, g_scr, qr_scr, kr_scr, do_scr, dq_scr, dk_scr):
        @pl.when(pl.program_id(0) == 0)
        def _():
            g_scr[...] = jnp.zeros_like(g_scr)
            dgain_ref[...] = jnp.zeros_like(dgain_ref)

        cosv, sinv = cos_ref[...], sin_ref[...]
        for h in range(RET_HEADS):
            hs = _head(h)
            qh, kh = q_ref[:, hs], k_ref[:, hs]
            qr_scr[:, hs] = qh * cosv + _swap_pairs(qh) * sinv
            kr_scr[:, hs] = (kh * cosv + _swap_pairs(kh) * sinv) * scale
            o = o_ref[:, hs]
            mu = jnp.mean(o, axis=-1, keepdims=True)
            oc = o - mu
            rstd = lax.rsqrt(jnp.mean(oc * oc, axis=-1, keepdims=True) + GN_EPS)
            yh = oc * rstd
            gain = gain_ref[:, hs]
            g = g_ref[:, hs]
            sg = 1.0 / (1.0 + jnp.exp(-g))
            dret = dret_ref[:, hs].astype(F32)
            dy = dret * (g * sg)
            dz_ref[:, 3 * RET_WIDTH + h * RET_HEAD_DIM:3 * RET_WIDTH + (h + 1) * RET_HEAD_DIM] = _bf(
                dret * (yh * gain) * (sg * (1.0 + g * (1.0 - sg))))
            dgain_ref[:, hs] += jnp.sum(dy * yh, axis=0, keepdims=True)
            dyh = dy * gain
            do_scr[:, hs] = rstd * (dyh - jnp.mean(dyh, axis=-1, keepdims=True)
                                    - yh * jnp.mean(dyh * yh, axis=-1, keepdims=True))

        def chunk(cc, carry):
            c = ncb - 1 - cc
            rows = pl.ds(pl.multiple_of(c * CHUNK, CHUNK), CHUNK)
            for h in range(RET_HEADS):
                hs = _head(h)
                qc, kc, vc, doc = _bf(qr_scr[rows, hs]), _bf(kr_scr[rows, hs]), _bf(v_ref[rows, hs]), _bf(do_scr[rows, hs])
                qdc, kdc = qd_ref[h], kd_ref[h]
                st, gs = _bf(st_ref[c, h]), g_scr[h]
                gsb = _bf(gs)
                dm = dm_ref[h]
                p = _bf(_dot(qc, kc, NT) * dm)
                da = _bf(_dot(doc, vc, NT) * dm)
                kt = _bf(kr_scr[rows, hs] * kdc)
                qt = _bf(qr_scr[rows, hs] * qdc)
                dz_ref[rows, 2 * RET_WIDTH + h * RET_HEAD_DIM:2 * RET_WIDTH + (h + 1) * RET_HEAD_DIM] = _bf(
                    _dot(p, doc, TN) + _dot(kt, gsb))
                dq_scr[rows, hs] = _dot(da, kc) + _dot(doc, st, NT) * qdc
                dk_scr[rows, hs] = _dot(da, qc, TN) + _dot(vc, gsb, NT) * kdc
                g_scr[h] = gs * lam_ref[h] + _dot(qt, doc, TN)
            return carry

        lax.fori_loop(0, ncb, chunk, 0, unroll=True)
        for h in range(RET_HEADS):
            hs = _head(h)
            dq, dk = dq_scr[:, hs], dk_scr[:, hs]
            dz_ref[:, h * RET_HEAD_DIM:(h + 1) * RET_HEAD_DIM] = _bf(dq * cosv - _swap_pairs(dq) * sinv)
            dz_ref[:, RET_WIDTH + h * RET_HEAD_DIM:RET_WIDTH + (h + 1) * RET_HEAD_DIM] = _bf(
                (dk * cosv - _swap_pairs(dk) * sinv) * scale)

    return pl.pallas_call(
        body, name=name, grid=(nblk,),
        out_shape=[SDS((s, AB_IN_WIDTH), BF16), SDS((1, RET_WIDTH), F32)],
        in_specs=_ret_common_specs(tb, rev)
        + [pl.BlockSpec((tb, RET_WIDTH), lambda i: (rev(i), 0)),
           pl.BlockSpec((ncb, RET_HEADS, RET_HEAD_DIM, RET_HEAD_DIM), lambda i: (rev(i), 0, 0, 0)),
           pl.BlockSpec((tb, RET_WIDTH), lambda i: (rev(i), 0))],
        out_specs=[pl.BlockSpec((tb, 4 * RET_WIDTH), lambda i: (rev(i), 0)),
                   pl.BlockSpec((1, RET_WIDTH), lambda i: (0, 0))],
        scratch_shapes=[pltpu.VMEM((RET_HEADS, RET_HEAD_DIM, RET_HEAD_DIM), F32)]
        + [pltpu.VMEM((tb, RET_WIDTH), F32)] * 5,
        compiler_params=_params(("arbitrary",)),
    )(z, z, z, z, tabs["cos"], tabs["sin"], gn_gain, tabs["dmat"], tabs["qdec"], tabs["kdec"], tabs["lam"],
      o_pre, states, du)


POOL_COL = 4 * RET_WIDTH // POOL_WIDTH


def _pooled(cur, prev, t0):
    tm = cur.shape[0]
    xx = jnp.concatenate([prev, cur], axis=0)
    sums = {1: xx}
    w = 1
    while w < POOL_WINDOWS[-1]:
        sums[2 * w] = sums[w] + pltpu.roll(sums[w], w, 0)
        w *= 2
    t = t0 + lax.broadcasted_iota(jnp.int32, (tm, 128), 0)
    outs = []
    for gi, w in enumerate(POOL_WINDOWS):
        cols = slice(gi * 128, (gi + 1) * 128)
        cnt = jnp.minimum(t + 1, w).astype(F32)
        outs.append(sums[w][POOL_HALO:, cols] / cnt - cur[:, cols])
    return outs


def _pool_fwd(z, w_pool, scale, name):
    s = z.shape[0]
    tm = _tile(s, TOKEN_TILE)
    hb = tm // POOL_HALO

    def body(p_ref, prev_ref, w_ref, sc_ref, o_ref):
        i = pl.program_id(0)
        prev = jnp.where(i > 0, prev_ref[...], 0.0)
        pooled = _pooled(p_ref[...], prev, i * tm)
        for gi in range(len(POOL_WINDOWS)):
            cols = slice(gi * 128, (gi + 1) * 128)
            o_ref[:, cols] = _bf(_dot(_bf(pooled[gi]), _bf(w_ref[gi])) * sc_ref[:, cols])

    return pl.pallas_call(
        body, name=name, grid=(s // tm,), out_shape=SDS((s, POOL_WIDTH), BF16),
        in_specs=[pl.BlockSpec((tm, POOL_WIDTH), lambda i: (i, POOL_COL)),
                  pl.BlockSpec((POOL_HALO, POOL_WIDTH), lambda i: (jnp.maximum(i * hb - 1, 0), POOL_COL)),
                  pl.BlockSpec(w_pool.shape, lambda i: (0, 0, 0)),
                  pl.BlockSpec((1, POOL_WIDTH), lambda i: (0, 0))],
        out_specs=pl.BlockSpec((tm, POOL_WIDTH), lambda i: (i, 0)),
        compiler_params=_params(("parallel",)),
    )(z, z, w_pool, scale)


def _pool_bwd(z, w_pool, scale, du, dz, name):
    s = z.shape[0]
    tm = _tile(s, TOKEN_TILE)
    hb = tm // POOL_HALO
    nblk = s // tm
    last_halo = s // POOL_HALO - 1

    def body(p_ref, prev_ref, w_ref, sc_ref, do_ref, don_ref, dz_ref, dp_ref, dw_ref, dsc_ref):
        i = pl.program_id(0)

        @pl.when(i == 0)
        def _():
            dw_ref[...] = jnp.zeros_like(dw_ref)
            dsc_ref[...] = jnp.zeros_like(dsc_ref)

        prev = jnp.where(i > 0, prev_ref[...], 0.0)
        pooled = _pooled(p_ref[...], prev, i * tm)
        dout = do_ref[...].astype(F32)
        dout_next = jnp.where(i < nblk - 1, don_ref[...].astype(F32), 0.0)
        sc = sc_ref[...]
        dmix = jnp.concatenate([dout * sc, dout_next * sc], axis=0)
        n = tm + POOL_HALO
        t = i * tm + lax.broadcasted_iota(jnp.int32, (n, 128), 0)
        for gi, w in enumerate(POOL_WINDOWS):
            cols = slice(gi * 128, (gi + 1) * 128)
            wg = _bf(w_ref[gi])
            pg = _bf(pooled[gi])
            dsc_ref[:, cols] += jnp.sum(dout[:, cols] * _dot(pg, wg), axis=0, keepdims=True)
            dw_ref[gi] += _dot(pg, _bf(dmix[:tm, cols]), TN)
            dpool = _dot(_bf(dmix[:, cols]), wg, NT)
            acc = dpool / jnp.minimum(t + 1, w).astype(F32)
            step = 1
            while step < w:
                acc = acc + pltpu.roll(acc, n - step, 0)
                step *= 2
            dp_ref[:, cols] = _bf(acc[:tm] - dpool[:tm])

    return pl.pallas_call(
        body, name=name, grid=(nblk,),
        out_shape=[SDS(dz.shape, BF16), SDS(w_pool.shape, F32), SDS((1, POOL_WIDTH), F32)],
        in_specs=[pl.BlockSpec((tm, POOL_WIDTH), lambda i: (i, POOL_COL)),
                  pl.BlockSpec((POOL_HALO, POOL_WIDTH), lambda i: (jnp.maximum(i * hb - 1, 0), POOL_COL)),
                  pl.BlockSpec(w_pool.shape, lambda i: (0, 0, 0)),
                  pl.BlockSpec((1, POOL_WIDTH), lambda i: (0, 0)),
                  pl.BlockSpec((tm, POOL_WIDTH), lambda i: (i, 1)),
                  pl.BlockSpec((POOL_HALO, POOL_WIDTH), lambda i: (jnp.minimum((i + 1) * hb, last_halo), 1)),
                  pl.BlockSpec(memory_space=pl.ANY)],
        out_specs=[pl.BlockSpec((tm, POOL_WIDTH), lambda i: (i, POOL_COL)),
                   pl.BlockSpec(w_pool.shape, lambda i: (0, 0, 0)),
                   pl.BlockSpec((1, POOL_WIDTH), lambda i: (0, 0))],
        input_output_aliases={6: 0},
        compiler_params=_params(("arbitrary",)),
    )(z, z, w_pool, scale, du, du, dz)


def _rel_onehot(offset=0):
    r = lax.broadcasted_iota(jnp.int32, (REL_PAD, ATT_DIAG), 0)
    c = lax.broadcasted_iota(jnp.int32, (REL_PAD, ATT_DIAG), 1) - offset
    rel = jnp.where((c >= 0) & (c < ATT_K_TILE), jnp.clip(LEFT_CHUNKS * CHUNK - c, -REL_CLIP, REL_CLIP) + REL_CLIP,
                    2 * REL_CLIP)
    return (rel == r).astype(BF16)


def _split3(v):
    hi = _bf(v)
    r1 = v - hi.astype(F32)
    mid = _bf(r1)
    return hi, mid, _bf(r1 - mid.astype(F32))


def _skew(v):
    return pltpu.roll(v, 0, 1, stride=1, stride_axis=0)


def _attn_bias(rel_bias, name):
    def body(t_ref, o_ref):
        oh = _rel_onehot()
        base = sum(_dot(part, oh) for part in _split3(t_ref[0]))
        full = _skew(jnp.broadcast_to(base[0:1], (ATT_Q_TILE, ATT_DIAG)))[:, :ATT_K_TILE]
        qc = lax.broadcasted_iota(jnp.int32, full.shape, 0) // CHUNK
        kc = lax.broadcasted_iota(jnp.int32, full.shape, 1) // CHUNK
        o_ref[0] = jnp.where((kc >= qc) & (kc <= qc + LEFT_CHUNKS), full, NEG_INF)

    t8 = jnp.broadcast_to(rel_bias[:, None, :], (ATT_HEADS, 8, REL_PAD))
    return pl.pallas_call(
        body, name=name, grid=(ATT_HEADS,), out_shape=SDS((ATT_HEADS, ATT_Q_TILE, ATT_K_TILE), F32),
        in_specs=[pl.BlockSpec((1, 8, REL_PAD), lambda h: (h, 0, 0))],
        out_specs=pl.BlockSpec((1, ATT_Q_TILE, ATT_K_TILE), lambda h: (h, 0, 0)),
        compiler_params=_params(("parallel",)),
    )(t8)


def _attn_dbias(dbias, name):
    def body(d_ref, o_ref):
        pad = jnp.zeros((ATT_Q_TILE, ATT_DIAG - ATT_K_TILE), F32)
        row = lax.broadcasted_iota(jnp.int32, (ATT_Q_TILE, ATT_Q_TILE), 0)
        lane = lax.broadcasted_iota(jnp.int32, (ATT_Q_TILE, ATT_Q_TILE), 1)
        flip = (row + lane == ATT_Q_TILE - 1).astype(BF16)
        upside = sum(_dot(flip, part) for part in _split3(jnp.concatenate([d_ref[0], pad], axis=1)))
        col = jnp.sum(_skew(upside), axis=0, keepdims=True)
        oh = _rel_onehot(ATT_Q_TILE - 1)
        col8 = jnp.broadcast_to(col, (8, ATT_DIAG))
        o_ref[0] = sum(_dot(part, oh, NT) for part in _split3(col8))

    out = pl.pallas_call(
        body, name=name, grid=(ATT_HEADS,), out_shape=SDS((ATT_HEADS, 8, REL_PAD), F32),
        in_specs=[pl.BlockSpec((1, ATT_Q_TILE, ATT_K_TILE), lambda h: (h, 0, 0))],
        out_specs=pl.BlockSpec((1, 8, REL_PAD), lambda h: (h, 0, 0)),
        compiler_params=_params(("parallel",)),
    )(dbias)
    return out[:, 0, :]


ATT_WIDTH = 128 * ATT_PAIRS
ATT_GROUPS = D_MODEL // ATT_WIDTH


def _attn_specs(nq):
    def tile(off, back):
        return pl.BlockSpec((ATT_Q_TILE, ATT_WIDTH),
                            lambda g, i: (jnp.maximum(jnp.minimum(i, nq - 1) - back, 0), off + g))

    backs = [ATT_BACK - b for b in range(ATT_BACK + 1)]
    return ([tile(0, 0)] + [tile(ATT_GROUPS, b) for b in backs] + [tile(2 * ATT_GROUPS, b) for b in backs]
            + [pl.BlockSpec((2 * ATT_PAIRS, ATT_Q_TILE, ATT_K_TILE), lambda g, i: (g, 0, 0))])


def _attn_weights(qh, k2, bias, i, masked):
    sc = _dot(qh, k2, NT) + bias
    if masked:
        kpos = (i - ATT_BACK) * ATT_Q_TILE + lax.broadcasted_iota(jnp.int32, sc.shape, 1)
        sc = jnp.where(kpos >= 0, sc, NEG_INF)
    e = jnp.exp(sc - jnp.max(sc, axis=-1, keepdims=True))
    return e, 1.0 / jnp.sum(e, axis=-1, keepdims=True)


def _first_head():
    return lax.broadcasted_iota(jnp.int32, (ATT_Q_TILE, 128), 1) < ATT_HEAD_DIM


def _pair_operands(q_ref, k_refs, v_refs, pp):
    cols = slice(pp * 128, (pp + 1) * 128)
    q2 = q_ref[:, cols] * ATT_HEAD_DIM ** -0.5
    k2 = jnp.concatenate([r[:, cols] for r in k_refs], axis=0)
    v2 = jnp.concatenate([r[:, cols] for r in v_refs], axis=0)
    return cols, q2, k2, v2


def _attn_fwd(z, bias, name):
    s = z.shape[0]
    nq = s // ATT_Q_TILE
    nt = ATT_BACK + 1

    def body(q_ref, *rest):
        k_refs, v_refs, (b_ref, o_ref) = rest[:nt], rest[nt:2 * nt], rest[2 * nt:]
        i = pl.program_id(1)
        first = _first_head()

        def compute(masked):
            for pp in range(ATT_PAIRS):
                cols, q2, k2, v2 = _pair_operands(q_ref, k_refs, v_refs, pp)
                outs = []
                for hh in range(2):
                    qh = jnp.where(first if hh == 0 else ~first, q2, 0)
                    e, inv = _attn_weights(qh, k2, b_ref[2 * pp + hh], i, masked)
                    outs.append(_dot(_bf(e), v2) * inv)
                o_ref[:, cols] = _bf(jnp.where(first, outs[0], outs[1]))

        pl.when(i < ATT_BACK)(lambda: compute(True))
        pl.when(i >= ATT_BACK)(lambda: compute(False))

    return pl.pallas_call(
        body, name=name, grid=(ATT_GROUPS, nq), out_shape=SDS((s, D_MODEL), BF16),
        in_specs=_attn_specs(nq),
        out_specs=pl.BlockSpec((ATT_Q_TILE, ATT_WIDTH), lambda g, i: (i, g)),
        compiler_params=_params(("parallel", "parallel")),
    )(*([z] * (1 + 2 * nt)), bias)


def _attn_bwd(z, bias, o, do, name):
    s = z.shape[0]
    nq = s // ATT_Q_TILE
    nt = ATT_BACK + 1

    def body(q_ref, *rest):
        k_refs, v_refs = rest[:nt], rest[nt:2 * nt]
        b_ref, o_ref, do_ref, dq_ref, dk_ref, dv_ref, db_ref, dk_acc, dv_acc = rest[2 * nt:]
        i = pl.program_id(1)
        first = _first_head()

        @pl.when(i == 0)
        def _():
            db_ref[...] = jnp.zeros_like(db_ref)
            dk_acc[...] = jnp.zeros_like(dk_acc)
            dv_acc[...] = jnp.zeros_like(dv_acc)

        def compute(masked):
            for pp in range(ATT_PAIRS):
                cols, q2, k2, v2 = _pair_operands(q_ref, k_refs, v_refs, pp)
                do2 = do_ref[:, cols].astype(F32)
                prod = do2 * o_ref[:, cols].astype(F32)
                dqs, dk, dv = [], None, None
                for hh in range(2):
                    mine = first if hh == 0 else ~first
                    qh = jnp.where(mine, q2, 0)
                    e, inv = _attn_weights(qh, k2, b_ref[2 * pp + hh], i, masked)
                    delta = jnp.sum(jnp.where(mine, prod, 0.0), axis=-1, keepdims=True) * inv
                    doh = _bf(jnp.where(mine, do2 * inv, 0.0))
                    ds = e * (_dot(doh, v2, NT) - delta)
                    db_ref[2 * pp + hh] += ds
                    dsb = _bf(ds)
                    dqs.append(_dot(dsb, k2))
                    dkh, dvh = _dot(dsb, qh, TN), _dot(_bf(e), doh, TN)
                    dk, dv = (dkh, dvh) if hh == 0 else (dk + dkh, dv + dvh)
                dq_ref[:, cols] = _bf(jnp.where(first, dqs[0], dqs[1]) * ATT_HEAD_DIM ** -0.5)
                for b in range(nt):
                    slot = (i + b + 1) % nt
                    rows = slice(b * ATT_Q_TILE, (b + 1) * ATT_Q_TILE)
                    if b < ATT_BACK:
                        dk_acc[slot, :, cols] += dk[rows]
                        dv_acc[slot, :, cols] += dv[rows]
                    else:
                        dk_acc[slot, :, cols] = dk[rows]
                        dv_acc[slot, :, cols] = dv[rows]

        pl.when(i < ATT_BACK)(lambda: compute(True))
        pl.when((i >= ATT_BACK) & (i < nq))(lambda: compute(False))
        done = (i + 1) % nt
        dk_ref[...] = _bf(dk_acc[done])
        dv_ref[...] = _bf(dv_acc[done])

    tile = pl.BlockSpec((ATT_Q_TILE, ATT_WIDTH), lambda g, i: (jnp.minimum(i, nq - 1), g))
    late = pl.BlockSpec((ATT_Q_TILE, ATT_WIDTH), lambda g, i: (jnp.maximum(i - ATT_BACK, 0), g))
    ring = pltpu.VMEM((nt, ATT_Q_TILE, ATT_WIDTH), F32)
    return pl.pallas_call(
        body, name=name, grid=(ATT_GROUPS, nq + ATT_BACK),
        out_shape=[SDS((s, D_MODEL), BF16)] * 3 + [SDS((ATT_HEADS, ATT_Q_TILE, ATT_K_TILE), F32)],
        in_specs=_attn_specs(nq) + [tile, tile],
        out_specs=[tile, late, late, pl.BlockSpec((2 * ATT_PAIRS, ATT_Q_TILE, ATT_K_TILE), lambda g, i: (g, 0, 0))],
        scratch_shapes=[ring, ring],
        compiler_params=_params(("parallel", "arbitrary")),
    )(*([z] * (1 + 2 * nt)), bias, o, do)


FWD_GROUPS = (
    (("ab_w_in", 0),),
    (("ab_w_out", 0), ("w_ffn_in", 0)),
    (("w_ffn_out", 0),),
    (("c_w_qkv", 0),),
    (("c_w_out", 0), ("w_ffn_in", 1), ("w_ffn_out", 1), ("ab_w_in", 1)),
    (("ab_w_out", 1), ("w_ffn_in", 2), ("w_ffn_out", 2), ("c_w_qkv", 1)),
    (("c_w_out", 1), ("w_ffn_in", 3), ("w_ffn_out", 3)),
)


def _local_step(x, target, small, comm):
    s = x.shape[0]
    tabs = _retention_tables(s)
    saved, w = [], comm.weight
    for layer in range(DEPTH):
        i = layer // 2
        sv = {"x0": x}
        g_mix = small["mix_norm"][layer:layer + 1]
        if layer % 2 == 0:
            at = (x,) + tuple(tabs.values()) if layer == 0 else x
            sv["h1"], sv["z"] = _norm_mm(x, g_mix, w("ab_w_in", i, at), AB_IN_WIDTH, F32, False, "ab_in_fwd")
            gn = small["ab_gn_gain"][i:i + 1]
            sv["o_pre"], sv["states"], ret = _ret_fwd(sv["z"], tabs, gn, "ret_fwd")
            pool = _pool_fwd(sv["z"], small["ab_w_pool"][i], small["ab_pool_scale"][i:i + 1], "pool_fwd")
            sv["u"] = (ret, pool)
            mixed = ([ret, pool], w("ab_w_out", i, ret))
        else:
            sv["h1"], sv["z"] = _norm_mm(x, g_mix, w("c_w_qkv", i, x), 3 * D_MODEL // N_DEV, BF16, False, "qkv_fwd")
            rb = jnp.pad(small["c_rel_bias"][i], ((0, 0), (0, REL_PAD - N_REL)))
            sv["bias"] = _attn_bias(rb, "attn_bias")
            sv["o"] = _attn_fwd(sv["z"], sv["bias"], "attn_fwd")
            mixed = ([sv["o"]], w("c_w_out", i, sv["o"]))
        x, sv["h2"], sv["z1"], sv["a"] = _norm_mm(x, small["ffn_norm"][layer:layer + 1], w("w_ffn_in", layer, mixed[0][0]),
                                                  D_FF // N_DEV, BF16, True, "ffn_in_fwd", mixed)
        sv["x1"] = x
        x = _mm_res([sv["a"]], w("w_ffn_out", layer, sv["a"]), x, "ffn_out_fwd")
        saved.append(sv)

    loss, dx, d_final = _final_loss(x, small["final_norm"][None, :], target, "final_loss")

    gs = {k: [None] * DEPTH for k in ("mix_norm", "ffn_norm")}
    for k in ("ab_gn_gain", "ab_w_pool", "ab_pool_scale", "c_rel_bias"):
        gs[k] = [None] * (DEPTH // 2)
    gs["final_norm"] = d_final[0]
    sent = None
    for layer in reversed(range(DEPTH)):
        i = layer // 2
        sv = saved[layer]
        dz1 = _mm_nt_rows(dx, w("w_ffn_out", layer), sv["z1"], "ffn_out_bwd")
        gw = {("w_ffn_out", layer): _mm_tn(sv["a"], dx, 1024, D_MODEL, True, (0, 2), "ffn_out_dw"),
              ("w_ffn_in", layer): _mm_tn(sv["h2"], dz1, D_MODEL, 1024, False, (1, 2), "ffn_in_dw",
                                          comm.after() if layer == 0 else None)}
        w_out = w("ab_w_out" if layer % 2 == 0 else "c_w_out", i)
        dx, dg, du = _mm_nt_normbwd(dz1, w("w_ffn_in", layer), sv["x1"], small["ffn_norm"][layer:layer + 1], dx,
                                    "ffn_in_bwd", sent, w_out)
        gs["ffn_norm"][layer] = dg[0]
        sent = None
        g_mix = small["mix_norm"][layer:layer + 1]
        if layer % 2 == 0:
            gw["ab_w_out", i] = _mm_tn(sv["u"], dx, D_MODEL, D_MODEL, True, (0, N_DEV), "mix_out_dw")
            if layer == 0:
                sent, gw = comm.send(gw), {}
            gn = small["ab_gn_gain"][i:i + 1]
            dz, dgn = _ret_bwd(sv["z"], tabs, gn, sv["o_pre"], sv["states"], du, "ret_bwd")
            dz, dwp, dsc = _pool_bwd(sv["z"], small["ab_w_pool"][i], small["ab_pool_scale"][i:i + 1], du, dz, "pool_bwd")
            gs["ab_gn_gain"][i], gs["ab_w_pool"][i], gs["ab_pool_scale"][i] = dgn[0], dwp, dsc[0]
            gw["ab_w_in", i] = _to_shard_major(_mm_tn(sv["h1"], dz, D_MODEL, AB_IN_WIDTH // 2, False, (1, 1), "ab_in_dw",
                                                      comm.after()), pairs_split=True)
            dx, dg = _mm_nt_normbwd(dz, w("ab_w_in", i), sv["x0"], g_mix, dx, "ab_in_bwd", sent)
        else:
            do = du
            gw["c_w_out", i] = _mm_tn(sv["o"], dx, D_MODEL, D_MODEL, True, (0, N_DEV), "mix_out_dw")
            dq, dk, dv, dbias = _attn_bwd(sv["z"], sv["bias"], sv["o"], do, "attn_bwd")
            gs["c_rel_bias"][i] = _attn_dbias(dbias, "attn_dbias")[:, :N_REL]
            tiles = [_mm_tn(sv["h1"], part, D_MODEL, D_MODEL, False, (1, 1), "qkv_dw", after)
                     for part, after in ((dq, None), (dk, None), (dv, comm.after()))]
            gw["c_w_qkv", i] = _to_shard_major(jnp.concatenate(tiles, axis=0))
            dx, dg = _mm_nt_normbwd([dq, dk, dv], w("c_w_qkv", i), sv["x0"], g_mix, dx, "qkv_bwd", sent)
        gs["mix_norm"][layer] = dg[0]
        if layer > 0:
            sent = comm.send(gw)
    gsmall = {k: (jnp.stack(v) if isinstance(v, list) else v) for k, v in gs.items()}
    return loss, dx, gw, gsmall


BIG = ("w_ffn_in", "w_ffn_out", "ab_w_in", "ab_w_out", "c_w_qkv", "c_w_out")
SMALL = ("mix_norm", "ffn_norm", "ab_gn_gain", "ab_w_pool", "ab_pool_scale", "c_rel_bias", "final_norm")
N_PEERS = N_DEV - 1
FLIPS = [(fx, fy, fc) for fx in (0, 1) for fy in (0, 1) for fc in (0, 1)][1:]


def _peers():
    x, y, c = (lax.axis_index(a) for a in MESH_AXES)
    peers = []
    for fx, fy, fc in FLIPS:
        px, py, pc = (1 - x if fx else x), (1 - y if fy else y), (1 - c if fc else c)
        peers.append(((px, py, pc), 4 * px + 2 * py + pc))
    return 4 * x + 2 * y + c, peers


def _exchange(srcs, by_slot, name, collective_id):
    n = len(srcs)
    src_refs = [jax.new_ref(a, memory_space=pltpu.MemorySpace.HBM) for a in srcs]
    land_refs = [jax.empty_ref(SDS((N_DEV,) + (a.shape[1:] if slotted else a.shape), a.dtype),
                               memory_space=pltpu.MemorySpace.HBM) for a, slotted in zip(srcs, by_slot)]

    @pl.kernel(mesh=plsc.ScalarSubcoreMesh(axis_name="sequencer", num_cores=1), name=name,
               scratch_types=(pltpu.SemaphoreType.DMA((n * N_PEERS,)), pltpu.SemaphoreType.DMA((n * N_PEERS,)),
                              pltpu.SemaphoreType.DMA((n,))),
               compiler_params=pltpu.CompilerParams(collective_id=collective_id))
    def launch(send_sems, recv_sems, local_sems):
        me, peers = _peers()
        barrier = pltpu.get_barrier_semaphore()
        for pos, _ in peers:
            pl.semaphore_signal(barrier, inc=1, device_id=pos, device_id_type=pl.DeviceIdType.MESH)
        pl.semaphore_wait(barrier, N_PEERS)
        waits = []
        for k in range(n):
            own = pltpu.make_async_copy(src_refs[k].at[me] if by_slot[k] else src_refs[k], land_refs[k].at[me],
                                        local_sems.at[k])
            own.start()
            waits.append(own.wait)
            for rel, (pos, slot) in enumerate(peers):
                src = src_refs[k].at[slot] if by_slot[k] else src_refs[k]
                sems = dict(send_sem=send_sems.at[k * N_PEERS + rel], recv_sem=recv_sems.at[k * N_PEERS + rel],
                            device_id=pos, device_id_type=pl.DeviceIdType.MESH)
                send = pltpu.make_async_remote_copy(src_ref=src, dst_ref=land_refs[k].at[me], **sems)
                send.start()
                arrival = pltpu.make_async_remote_copy(src_ref=src, dst_ref=land_refs[k].at[slot], **sems)
                waits += [send.wait_send, arrival.wait_recv]
        for wait in waits:
            wait()

    launch()
    return [r[...] for r in land_refs]


def _cast_group(weights, keys, after, name):
    after = _follow(after)

    def body(*refs):
        n = len(keys)
        for i_ref, o_ref in zip(refs[:n], refs[n + len(after):]):
            o_ref[...] = _bf(i_ref[...])

    def layer_spec(shape, l):
        return pl.BlockSpec((None,) + shape[1:], lambda i: (l, 0, 0))

    whole = lambda shape: pl.BlockSpec(shape, lambda i: (0, 0))
    ins = [weights[k] for k, _ in keys]
    return pl.pallas_call(
        body, name=name, grid=(1,), out_shape=[SDS(w.shape[1:], BF16) for w in ins],
        in_specs=[layer_spec(w.shape, l) for w, (_, l) in zip(ins, keys)] + [FOLLOW] * len(after),
        out_specs=[whole(w.shape[1:]) for w in ins],
        compiler_params=_params(("arbitrary",)),
    )(*ins, *after)


def _to_shard_major(g, pairs_split=False):
    nj, ka, nb = g.shape
    full = jnp.transpose(g, (1, 0, 2)).reshape(ka, nj * nb)
    if pairs_split:
        full = _split_pairs(full, inverse=True)
    return jnp.transpose(full.reshape(ka, N_DEV, nj * nb // N_DEV), (1, 0, 2))


def _from_gathered(name, g):
    if name in ("w_ffn_out", "ab_w_out", "c_w_out"):
        return g.reshape(g.shape[0] * g.shape[1], g.shape[2])
    if name == "ab_w_in":
        return _split_pairs(jnp.transpose(g, (1, 0, 2)).reshape(1, g.shape[1], N_DEV * g.shape[2]))
    return g


class _Comm:
    def __init__(self, weights):
        self.weights_f32 = weights
        self.gathered = {}
        self.got = {}
        self.calls = 0
        self.ended = []
        self.opened = -1

    def _exchange(self, srcs, by_slot, name):
        self.calls += 1
        got = _exchange(srcs, by_slot, name, self.calls)
        self.ended = got[:1]
        return got

    def _gather(self, group, at):
        keys = FWD_GROUPS[group]
        shards = _cast_group(self.weights_f32, keys, self.ended + _follow(at), "cast_%d" % group)
        got = self._exchange(shards, [False] * len(keys), "gather_%d" % group)
        self.gathered.update((k, _from_gathered(k[0], arr)) for k, arr in zip(keys, got))

    def weight(self, name, layer, at=None):
        if (name, layer) not in self.gathered:
            self._gather(0, at[0] if isinstance(at, tuple) else at)
        group = next(g for g, keys in enumerate(FWD_GROUPS) if (name, layer) in keys)
        if group == self.opened + 1:
            self.opened = group
            if group + 1 < len(FWD_GROUPS):
                self._gather(group + 1, at)
        return self.gathered[name, layer]

    def after(self):
        return self.ended

    def send(self, grads, shared=None):
        shared = shared or {}
        keys = list(grads) + list(shared)
        srcs = list(grads.values()) + list(shared.values())
        got = self._exchange(srcs, [True] * len(grads) + [False] * len(shared), "scatter_%d" % self.calls)
        self.got.update(zip(keys, got))
        return list(grads.values())

    def received(self):
        return self.got


def _adamw_math(g, w, m, v):
    m2 = ADAM_B1 * m + (1.0 - ADAM_B1) * g
    v2 = ADAM_B2 * v + (1.0 - ADAM_B2) * jnp.square(g)
    m_hat = m2 / (1.0 - ADAM_B1 ** ADAM_STEP)
    v_hat = v2 / (1.0 - ADAM_B2 ** ADAM_STEP)
    delta = -ADAM_LR * (m_hat / (jnp.sqrt(v_hat) + ADAM_EPS) + ADAM_WD * w)
    return delta, m2, v2


def _adamw(recv, w, m, v, name):
    nl, r, c = w.shape
    tr = _tile(r, 256)

    def body(*refs):
        g_refs = refs[:nl]
        w_ref, m_ref, v_ref, go_ref, d_ref, mo_ref, vo_ref = refs[nl:]
        for l in range(nl):
            @pl.when(pl.program_id(0) == l)
            def _():
                g = g_refs[l][0].astype(F32)
                for p in range(1, N_DEV):
                    g = g + g_refs[l][p].astype(F32)
                go_ref[...] = g
                d_ref[...], mo_ref[...], vo_ref[...] = _adamw_math(g, w_ref[...], m_ref[...], v_ref[...])

    def recv_spec(l):
        return pl.BlockSpec((N_DEV, tr, c), lambda layer, i: (0, jnp.where(layer == l, i, 0), 0))

    blk = pl.BlockSpec((None, tr, c), lambda l, i: (l, i, 0))
    return pl.pallas_call(
        body, name=name, grid=(nl, r // tr), out_shape=[SDS(w.shape, F32)] * 4,
        in_specs=[recv_spec(l) for l in range(nl)] + [blk, blk, blk],
        out_specs=[blk] * 4,
        compiler_params=_params(("arbitrary", "arbitrary")),
    )(*recv, w, m, v)


def _adamw_small(recv, loss_parts, w, m, v, name):
    n = len(w)

    def total(ref):
        t = ref[0]
        for p in range(1, N_DEV):
            t = t + ref[p]
        return t

    def body(*refs):
        g_refs, loss_ref = refs[:n], refs[n]
        w_refs, m_refs, v_refs = refs[n + 1:2 * n + 1], refs[2 * n + 1:3 * n + 1], refs[3 * n + 1:4 * n + 1]
        outs = refs[4 * n + 1:]
        for i in range(n):
            g = total(g_refs[i])
            outs[4 * i][...] = g
            outs[4 * i + 1][...], outs[4 * i + 2][...], outs[4 * i + 3][...] = _adamw_math(
                g, w_refs[i][...], m_refs[i][...], v_refs[i][...])
        outs[4 * n][...] = total(loss_ref)

    out_shape = [SDS(p.shape, F32) for p in w for _ in range(4)] + [SDS(loss_parts.shape[1:], F32)]
    outs = pl.pallas_call(body, name=name, out_shape=out_shape,
                          compiler_params=_params(None))(*recv, loss_parts, *w, *m, *v)
    return [outs[4 * i:4 * i + 4] for i in range(n)], outs[-1]


def kernel(x, mix_norm, ffn_norm, w_ffn_in, w_ffn_out, ab_w_in, ab_gn_gain, ab_w_pool, ab_pool_scale, ab_w_out, c_w_qkv, c_rel_bias, c_w_out, final_norm, loss_target, m_mix_norm, m_ffn_norm, m_w_ffn_in, m_w_ffn_out, m_ab_w_in, m_ab_gn_gain, m_ab_w_pool, m_ab_pool_scale, m_ab_w_out, m_c_w_qkv, m_c_rel_bias, m_c_w_out, m_final_norm, v_mix_norm, v_ffn_norm, v_w_ffn_in, v_w_ffn_out, v_ab_w_in, v_ab_gn_gain, v_ab_w_pool, v_ab_pool_scale, v_ab_w_out, v_c_w_qkv, v_c_rel_bias, v_c_w_out, v_final_norm):
    args = dict(locals())
    weights = {k: args[k] for k in BIG + SMALL}
    moments_m = {k: args["m_" + k] for k in BIG + SMALL}
    moments_v = {k: args["v_" + k] for k in BIG + SMALL}

    small = {k: weights[k] for k in SMALL}
    rows = lambda a: a.reshape(1, -1) if a.ndim == 1 else a

    comm = _Comm(weights)
    loss, dx, last_grads, gsmall = _local_step(x[0], loss_target[0], small, comm)
    comm.send(last_grads, {**{k: rows(gsmall[k]) for k in SMALL}, "loss": loss})
    recv = comm.received()

    outs = {}
    for k in BIG:
        layers = [recv[k, l] for l in range(weights[k].shape[0])]
        outs[k] = _adamw(layers, weights[k], moments_m[k], moments_v[k], "adamw_" + k)
    updated, total = _adamw_small([recv[k] for k in SMALL], recv["loss"], [rows(small[k]) for k in SMALL],
                                  [rows(moments_m[k]) for k in SMALL], [rows(moments_v[k]) for k in SMALL], "adamw_small")
    for k, parts in zip(SMALL, updated):
        outs[k] = [p.reshape(small[k].shape) for p in parts]

    order = SMALL[:2] + BIG[:2] + ("ab_w_in", "ab_gn_gain", "ab_w_pool", "ab_pool_scale", "ab_w_out",
                                   "c_w_qkv", "c_rel_bias", "c_w_out", "final_norm")
    result = [total[0, 0], dx[None]]
    for part in range(4):
        result += [outs[k][part] for k in order]
    return tuple(result)
```

```python
import functools

import jax
import jax.numpy as jnp
from jax import lax
from jax.experimental import pallas as pl
from jax.experimental.pallas import tpu as pltpu
from jax.experimental.pallas import tpu_sc as plsc

F32 = jnp.float32
BF16 = jnp.bfloat16
SDS = jax.ShapeDtypeStruct
MESH_AXES = ("x", "y", "c")
N_DEV = 8

D_MODEL = 1024
DEPTH = 4
CHUNK = 64
D_FF = 4 * D_MODEL
RMS_EPS = 1e-6
RET_WIDTH = 512
RET_HEADS = 4
RET_HEAD_DIM = 128
RET_ROPE_BASE = 10000.0
GN_EPS = 1e-5
POOL_WIDTH = 512
POOL_WINDOWS = (2, 4, 8, 16)
POOL_HALO = 16
AB_IN_WIDTH = 4 * RET_WIDTH + POOL_WIDTH
ATT_HEADS = 16
ATT_HEAD_DIM = 64
LEFT_CHUNKS = 8
REL_CLIP = 128
N_REL = 2 * REL_CLIP + 1
NEG_INF = -1e30

ADAM_LR = 0.001
ADAM_B1 = 0.9
ADAM_B2 = 0.999
ADAM_EPS = 1e-08
ADAM_WD = 0.01
ADAM_STEP = 10

TOKEN_TILE = 512
ATT_Q_TILE = 256
ATT_BACK = LEFT_CHUNKS * CHUNK // ATT_Q_TILE
ATT_K_TILE = (ATT_BACK + 1) * ATT_Q_TILE
ATT_PAIRS = 4
ATT_DIAG = 1024
REL_PAD = 384
VMEM_LIMIT_MB = 56

NT = (((1,), (1,)), ((), ()))
TN = (((0,), (0,)), ((), ()))


def _params(semantics, **kw):
    return pltpu.CompilerParams(dimension_semantics=semantics,
                                vmem_limit_bytes=VMEM_LIMIT_MB * 2 ** 20, **kw)


def _dot(a, b, dims=None):
    if dims is None:
        return jnp.dot(a, b, preferred_element_type=F32)
    return lax.dot_general(a, b, dims, preferred_element_type=F32)


def _bf(v):
    return v.astype(BF16)


def _tile(n, t):
    return min(n, t)


FOLLOW = pl.BlockSpec(memory_space=pl.ANY)


def _follow(after):
    return [] if after is None else list(after) if isinstance(after, (list, tuple)) else [after]


MXU_WIDTH = 256


def _mxu_group(nj, tn):
    return 2 if tn % MXU_WIDTH and (2 * tn) % MXU_WIDTH == 0 and nj % 2 == 0 else 1


def _w_tiles(w_ref, j, group):
    return w_ref[j] if group == 1 else jnp.concatenate([w_ref[j + t] for t in range(group)], axis=1)


def _w_cols(w_ref, j, group, c, width):
    return w_ref[j, :, c:c + width] if group == 1 else _w_tiles(w_ref, j, group)


def _norm_mm(x, gain, w, tn, z_dtype, relu2, name, pre=None):
    s, d = x.shape
    nj = w.shape[0]
    tm = _tile(s, TOKEN_TILE)
    group = _mxu_group(nj, tn)
    pre_parts, w_pre = pre if pre else ([], None)
    widths = [p.shape[1] for p in pre_parts]

    def body(*refs):
        p_refs = refs[:len(pre_parts)]
        refs = refs[len(pre_parts):]
        if pre:
            wp_ref, x_ref, g_ref, w_ref, x1_ref, h_ref, z_ref, *a_ref = refs
            xv, off = x_ref[...], 0
            for p_ref, k in zip(p_refs, widths):
                xv = xv + _dot(p_ref[...], wp_ref[off:off + k, :])
                off += k
            x1_ref[...] = xv
        else:
            x_ref, g_ref, w_ref, h_ref, z_ref, *a_ref = refs
            xv = x_ref[...]
        r = lax.rsqrt(jnp.mean(xv * xv, axis=-1, keepdims=True) + RMS_EPS)
        h = _bf(xv * r * g_ref[...])
        h_ref[...] = h
        cw = tn if tn <= 512 else 512
        for j in range(0, nj, group):
            for c in range(0, tn, cw):
                z = _dot(h, _w_cols(w_ref, j, group, c, cw))
                cols = slice(j * tn + c, j * tn + c + group * cw)
                z_ref[:, cols] = z.astype(z_ref.dtype)
                if relu2:
                    a_ref[0][:, cols] = _bf(jnp.square(jnp.maximum(z, 0.0)))

    n = nj * tn
    rows = lambda width: pl.BlockSpec((tm, width), lambda i: (i, 0))
    out_shape = [SDS((s, d), F32)] * bool(pre) + [SDS((s, d), BF16), SDS((s, n), z_dtype)] + [SDS((s, n), BF16)] * relu2
    out_specs = [rows(d)] * bool(pre) + [rows(d), rows(n)] + [rows(n)] * relu2
    return pl.pallas_call(
        body, name=name, grid=(s // tm,), out_shape=out_shape,
        in_specs=[rows(k) for k in widths] + ([pl.BlockSpec(w_pre.shape, lambda i: (0, 0))] if pre else [])
        + [rows(d), pl.BlockSpec((1, d), lambda i: (0, 0)), pl.BlockSpec((nj, d, tn), lambda i: (0, 0, 0))],
        out_specs=out_specs,
        compiler_params=_params(("parallel",)),
    )(*pre_parts, *([w_pre] if pre else []), x, gain, w)


def _mm_res(parts, w, res, name):
    s, d = res.shape
    tm = _tile(s, TOKEN_TILE)
    widths = [p.shape[1] for p in parts]

    def body(*refs):
        a_refs = refs[:len(parts)]
        w_ref, res_ref, o_ref = refs[len(parts):]
        acc = res_ref[...]
        off = 0
        for a_ref, k in zip(a_refs, widths):
            acc = acc + _dot(a_ref[...], w_ref[off:off + k, :])
            off += k
        o_ref[...] = acc

    return pl.pallas_call(
        body, name=name, grid=(s // tm,), out_shape=SDS((s, d), F32),
        in_specs=[pl.BlockSpec((tm, k), lambda i: (i, 0)) for k in widths]
        + [pl.BlockSpec(w.shape, lambda i: (0, 0)), pl.BlockSpec((tm, d), lambda i: (i, 0))],
        out_specs=pl.BlockSpec((tm, d), lambda i: (i, 0)),
        compiler_params=_params(("parallel",)),
    )(*parts, w, res)


def _mm_nt_rows(dy, w, z, name):
    s, d = dy.shape
    k = w.shape[0]
    tm = _tile(s, TOKEN_TILE)
    tk = _tile(k, 1024)

    def body(dy_ref, w_ref, *rest):
        o_ref = rest[-1]
        dyb = _bf(dy_ref[...])
        for j in range(k // tk):
            cols = slice(j * tk, (j + 1) * tk)
            da = _dot(dyb, w_ref[cols, :], NT)
            if z is not None:
                da = da * (2.0 * jnp.maximum(rest[0][:, cols].astype(F32), 0.0))
            o_ref[:, cols] = _bf(da)

    in_specs = [pl.BlockSpec((tm, d), lambda i: (i, 0)), pl.BlockSpec((k, d), lambda i: (0, 0))]
    args = [dy, w]
    if z is not None:
        in_specs.append(pl.BlockSpec((tm, k), lambda i: (i, 0)))
        args.append(z)
    return pl.pallas_call(
        body, name=name, grid=(s // tm,), out_shape=SDS((s, k), BF16),
        in_specs=in_specs, out_specs=pl.BlockSpec((tm, k), lambda i: (i, 0)),
        compiler_params=_params(("parallel",)),
    )(*args)


def _w_range(w_ref, c0, c1):
    nc = w_ref.shape[2]
    pieces, c = [], c0
    while c < c1:
        j = c // nc
        hi = min(nc, c1 - j * nc)
        pieces.append(w_ref[j, :, c - j * nc:hi])
        c = j * nc + hi
    return pieces[0] if len(pieces) == 1 else jnp.concatenate(pieces, axis=1)


def _mm_nt_normbwd(dz, w, x, gain, dres, name, after=None, w_post=None):
    parts, after = list(dz) if isinstance(dz, (list, tuple)) else [dz], _follow(after)
    widths = [p.shape[1] for p in parts]
    s, d = x.shape
    tm = _tile(s, TOKEN_TILE)
    chunk = 2 * MXU_WIDTH
    halves = 2 if tm % 32 == 0 else 1

    def body(*refs):
        dz_refs = refs[:len(parts)]
        w_ref, x_ref, g_ref, dres_ref = refs[len(parts):len(parts) + 4]
        dx_ref, dg_ref = refs[-2 - has_post:][:2]

        @pl.when(pl.program_id(0) == 0)
        def _():
            dg_ref[...] = jnp.zeros_like(dg_ref)

        for half in range(halves):
            rows = slice(half * tm // halves, (half + 1) * tm // halves)
            dh, base = None, 0
            for dz_ref, width in zip(dz_refs, widths):
                for c in range(0, width, chunk):
                    term = _dot(dz_ref[rows, c:c + chunk], _w_range(w_ref, base + c, base + c + chunk), NT)
                    dh = term if dh is None else dh + term
                base += width
            xv = x_ref[rows, :]
            r = lax.rsqrt(jnp.mean(xv * xv, axis=-1, keepdims=True) + RMS_EPS)
            xn = xv * r
            dg_ref[...] += jnp.sum(dh * xn, axis=0, keepdims=True)
            dxh = dh * g_ref[...]
            dx = dres_ref[rows, :] + r * (dxh - xn * jnp.mean(dxh * xn, axis=-1, keepdims=True))
            dx_ref[rows, :] = dx
            if has_post:
                refs[-1][rows, :] = _bf(_dot(_bf(dx), refs[len(parts) + 4][...], NT))

    has_post = w_post is not None
    post_in = [pl.BlockSpec(w_post.shape, lambda i: (0, 0))] if has_post else []
    post_out = [pl.BlockSpec((tm, w_post.shape[0]), lambda i: (i, 0))] if has_post else []
    return pl.pallas_call(
        body, name=name, grid=(s // tm,),
        out_shape=[SDS((s, d), F32), SDS((1, d), F32)] + ([SDS((s, w_post.shape[0]), BF16)] if has_post else []),
        in_specs=[pl.BlockSpec((tm, width), lambda i: (i, 0)) for width in widths]
        + [pl.BlockSpec(w.shape, lambda i: (0, 0, 0)),
           pl.BlockSpec((tm, d), lambda i: (i, 0)),
           pl.BlockSpec((1, d), lambda i: (0, 0)),
           pl.BlockSpec((tm, d), lambda i: (i, 0))] + post_in + [FOLLOW] * len(after),
        out_specs=[pl.BlockSpec((tm, d), lambda i: (i, 0)), pl.BlockSpec((1, d), lambda i: (0, 0))] + post_out,
        compiler_params=_params(("arbitrary",)),
    )(*parts, w, x, gain, dres, *([w_post] if has_post else []), *after)


def _mm_tn(a, b, ka, nb, a_tiled, split, name, after=None):
    a_parts, after = list(a) if isinstance(a, (list, tuple)) else [a], _follow(after)
    s = a_parts[0].shape[0]
    tm = _tile(s, 4 * TOKEN_TILE)
    nm = s // tm
    nj = a_parts[0].shape[1] // ka if a_tiled and len(a_parts) == 1 else (1 if a_tiled else b.shape[1] // nb)
    axis, parts = split
    pr, pc = (ka // parts, nb) if axis == 0 else (ka, nb // parts)

    def body(*refs):
        a_refs, b_ref = refs[:len(a_parts)], refs[len(a_parts)]
        o_ref, acc = refs[-2:]
        m = pl.program_id(1)

        @pl.when(m == 0)
        def _():
            acc[...] = jnp.zeros_like(acc)

        av = a_refs[0][...] if len(a_refs) == 1 else jnp.concatenate([r[...] for r in a_refs], axis=1)
        acc[...] += _dot(_bf(av), _bf(b_ref[...]), TN)

        @pl.when(m == nm - 1)
        def _():
            for q in range(parts):
                piece = acc[q * pr:(q + 1) * pr, :] if axis == 0 else acc[:, q * pc:(q + 1) * pc]
                o_ref[q] = piece.astype(o_ref.dtype)

    return pl.pallas_call(
        body, name=name, grid=(nj, nm), out_shape=SDS((nj * parts, pr, pc), BF16),
        in_specs=([pl.BlockSpec((tm, ka), (lambda j, m: (m, j)) if a_tiled else (lambda j, m: (m, 0)))]
                  if len(a_parts) == 1 else [pl.BlockSpec((tm, p.shape[1]), lambda j, m: (m, 0)) for p in a_parts])
        + [pl.BlockSpec((tm, nb), (lambda j, m: (m, 0)) if a_tiled else (lambda j, m: (m, j)))]
        + [FOLLOW] * len(after),
        out_specs=pl.BlockSpec((parts, pr, pc), lambda j, m: (j, 0, 0)),
        scratch_shapes=[pltpu.VMEM((ka, nb), F32)],
        compiler_params=_params(("parallel", "arbitrary")),
    )(*a_parts, b, *after)


def _final_loss(x, gain, target, name):
    s, d = x.shape
    tm = _tile(s, TOKEN_TILE)

    def body(x_ref, g_ref, t_ref, loss_ref, dx_ref, dg_ref):
        @pl.when(pl.program_id(0) == 0)
        def _():
            loss_ref[...] = jnp.zeros_like(loss_ref)
            dg_ref[...] = jnp.zeros_like(dg_ref)

        xv = x_ref[...]
        r = lax.rsqrt(jnp.mean(xv * xv, axis=-1, keepdims=True) + RMS_EPS)
        xn = xv * r
        err = xn * g_ref[...] - t_ref[...]
        loss_ref[...] += (0.5 / d) * jnp.sum(err * err)
        dy = err * (1.0 / d)
        dg_ref[...] += jnp.sum(dy * xn, axis=0, keepdims=True)
        dxh = dy * g_ref[...]
        dx_ref[...] = r * (dxh - xn * jnp.mean(dxh * xn, axis=-1, keepdims=True))

    return pl.pallas_call(
        body, name=name, grid=(s // tm,),
        out_shape=[SDS((8, 128), F32), SDS((s, d), F32), SDS((1, d), F32)],
        in_specs=[pl.BlockSpec((tm, d), lambda i: (i, 0)), pl.BlockSpec((1, d), lambda i: (0, 0)),
                  pl.BlockSpec((tm, d), lambda i: (i, 0))],
        out_specs=[pl.BlockSpec((8, 128), lambda i: (0, 0)), pl.BlockSpec((tm, d), lambda i: (i, 0)),
                   pl.BlockSpec((1, d), lambda i: (0, 0))],
        compiler_params=_params(("arbitrary",)),
    )(x, gain, target)


def _retention_tables(s):
    half = RET_HEAD_DIM // 2
    inv_freq = 1.0 / (RET_ROPE_BASE ** jnp.linspace(0.0, 1.0, half, dtype=F32))
    ang = jnp.arange(s, dtype=F32)[:, None] * inv_freq[None, :]
    cos, sin = jnp.cos(ang), jnp.sin(ang)
    cos_e = jnp.concatenate([cos, cos], axis=-1)
    sin_s = jnp.concatenate([-sin, sin], axis=-1)
    log_g = jnp.log1p(-jnp.power(2.0, -5.0 - jnp.arange(RET_HEADS, dtype=F32)))
    pos = jnp.arange(CHUNK, dtype=F32)
    dmat = jnp.exp(jnp.abs(pos[:, None] - pos[None, :])[None] * log_g[:, None, None])
    qdec = jnp.exp((pos[None, :] + 1.0) * log_g[:, None])
    kdec = jnp.exp((CHUNK - 1.0 - pos[None, :]) * log_g[:, None])
    lam = jnp.exp(CHUNK * log_g)
    wide = (RET_HEADS, CHUNK, RET_HEAD_DIM)
    return dict(cos=cos_e, sin=sin_s, dmat=dmat,
                qdec=jnp.broadcast_to(qdec[:, :, None], wide),
                kdec=jnp.broadcast_to(kdec[:, :, None], wide),
                lam=jnp.broadcast_to(lam[:, None, None], (RET_HEADS, RET_HEAD_DIM, RET_HEAD_DIM)))


def _swap_pairs(t):
    return pltpu.roll(t, RET_HEAD_DIM // 2, 1)


def _split_pairs(w, inverse=False):
    lead, nqk = w.shape[:-1], 2 * RET_WIDTH
    shape = (2 * RET_HEADS, 2, RET_HEAD_DIM // 2) if inverse else (2 * RET_HEADS, RET_HEAD_DIM // 2, 2)
    qk = jnp.swapaxes(w[..., :nqk].reshape(lead + shape), -1, -2).reshape(lead + (nqk,))
    return jnp.concatenate([qk, w[..., nqk:]], axis=-1)


def _head(h):
    return slice(h * RET_HEAD_DIM, (h + 1) * RET_HEAD_DIM)


def _ret_common_specs(tb, blk, rotated=False):
    zs = [pl.BlockSpec((tb, RET_WIDTH), functools.partial(lambda j, i: (blk(i), j), 0 if rotated and j < 2 else j))
          for j in range(4)]
    tabs = [pl.BlockSpec((tb, RET_HEAD_DIM), lambda i: (blk(i), 0))] * 2
    consts = [pl.BlockSpec((1, RET_WIDTH), lambda i: (0, 0)),
              pl.BlockSpec((RET_HEADS, CHUNK, CHUNK), lambda i: (0, 0, 0)),
              pl.BlockSpec((RET_HEADS, CHUNK, RET_HEAD_DIM), lambda i: (0, 0, 0)),
              pl.BlockSpec((RET_HEADS, CHUNK, RET_HEAD_DIM), lambda i: (0, 0, 0)),
              pl.BlockSpec((RET_HEADS, RET_HEAD_DIM, RET_HEAD_DIM), lambda i: (0, 0, 0))]
    return zs + tabs + consts


def _ret_fwd(z, tabs, gn_gain, name):
    s = z.shape[0]
    tb = _tile(s, TOKEN_TILE)
    ncb = tb // CHUNK
    scale = RET_HEAD_DIM ** -0.5

    def body(q_ref, k_ref, v_ref, g_ref, cos_ref, sin_ref, gain_ref, dm_ref, qd_ref, kd_ref, lam_ref,
             o_ref, st_ref, ret_ref, qr_scr, kr_scr, s_scr):
        @pl.when(pl.program_id(0) == 0)
        def _():
            s_scr[...] = jnp.zeros_like(s_scr)

        cosv, sinv = cos_ref[...], sin_ref[...]
        for h in range(RET_HEADS):
            qh, kh = q_ref[:, _head(h)], k_ref[:, _head(h)]
            qr_scr[:, _head(h)] = qh * cosv + _swap_pairs(qh) * sinv
            kr_scr[:, _head(h)] = (kh * cosv + _swap_pairs(kh) * sinv) * scale

        def chunk(c, carry):
            rows = pl.ds(pl.multiple_of(c * CHUNK, CHUNK), CHUNK)
            for h in range(RET_HEADS):
                qc, kc, vc = qr_scr[rows, _head(h)], kr_scr[rows, _head(h)], v_ref[rows, _head(h)]
                a = _dot(_bf(qc), _bf(kc), NT) * dm_ref[h]
                st = s_scr[h]
                st_ref[c, h] = st
                o_ref[rows, _head(h)] = _dot(_bf(a), _bf(vc)) + _dot(_bf(qc * qd_ref[h]), _bf(st))
                s_scr[h] = st * lam_ref[h] + _dot(_bf(kc * kd_ref[h]), _bf(vc), TN)
            return carry

        lax.fori_loop(0, ncb, chunk, 0, unroll=True)
        for h in range(RET_HEADS):
            o = o_ref[:, _head(h)]
            mu = jnp.mean(o, axis=-1, keepdims=True)
            oc = o - mu
            y = oc * lax.rsqrt(jnp.mean(oc * oc, axis=-1, keepdims=True) + GN_EPS) * gain_ref[:, _head(h)]
            g = g_ref[:, _head(h)]
            ret_ref[:, _head(h)] = _bf(g / (1.0 + jnp.exp(-g)) * y)

    nc = s // CHUNK
    return pl.pallas_call(
        body, name=name, grid=(s // tb,),
        out_shape=[SDS((s, RET_WIDTH), F32), SDS((nc, RET_HEADS, RET_HEAD_DIM, RET_HEAD_DIM), F32),
                   SDS((s, RET_WIDTH), BF16), SDS((s, RET_WIDTH), F32), SDS((s, RET_WIDTH), F32)],
        in_specs=_ret_common_specs(tb, lambda i: i),
        out_specs=[pl.BlockSpec((tb, RET_WIDTH), lambda i: (i, 0)),
                   pl.BlockSpec((ncb, RET_HEADS, RET_HEAD_DIM, RET_HEAD_DIM), lambda i: (i, 0, 0, 0))]
        + [pl.BlockSpec((tb, RET_WIDTH), lambda i: (i, 0))] * 3,
        scratch_shapes=[pltpu.VMEM((RET_HEADS, RET_HEAD_DIM, RET_HEAD_DIM), F32)],
        compiler_params=_params(("arbitrary",)),
    )(z, z, z, z, tabs["cos"], tabs["sin"], gn_gain, tabs["dmat"], tabs["qdec"], tabs["kdec"], tabs["lam"])


def _ret_bwd(z, qr, kr, tabs, gn_gain, o_pre, states, du, name):
    s = z.shape[0]
    tb = _tile(s, TOKEN_TILE)
    ncb = tb // CHUNK
    nblk = s // tb
    scale = RET_HEAD_DIM ** -0.5
    rev = lambda i: nblk - 1 - i

    def body(q_ref, k_ref, v_ref, g_ref, cos_ref, sin_ref, gain_ref, dm_ref, qd_ref, kd_ref, lam_ref,
             o_ref, st_ref, dret_ref, dz_ref, dgain_ref, g_scr, do_scr, dq_scr, dk_scr):
        @pl.when(pl.program_id(0) == 0)
        def _():
            g_scr[...] = jnp.zeros_like(g_scr)
            dgain_ref[...] = jnp.zeros_like(dgain_ref)

        cosv, sinv = cos_ref[...], sin_ref[...]
        for h in range(RET_HEADS):
            hs = _head(h)
            o = o_ref[:, hs]
            mu = jnp.mean(o, axis=-1, keepdims=True)
            oc = o - mu
            rstd = lax.rsqrt(jnp.mean(oc * oc, axis=-1, keepdims=True) + GN_EPS)
            yh = oc * rstd
            gain = gain_ref[:, hs]
            g = g_ref[:, hs]
            sg = 1.0 / (1.0 + jnp.exp(-g))
            dret = dret_ref[:, hs].astype(F32)
            dy = dret * (g * sg)
            dz_ref[:, 3 * RET_WIDTH + h * RET_HEAD_DIM:3 * RET_WIDTH + (h + 1) * RET_HEAD_DIM] = _bf(
                dret * (yh * gain) * (sg * (1.0 + g * (1.0 - sg))))
            dgain_ref[:, hs] += jnp.sum(dy * yh, axis=0, keepdims=True)
            dyh = dy * gain
            do_scr[:, hs] = rstd * (dyh - jnp.mean(dyh, axis=-1, keepdims=True)
                                    - yh * jnp.mean(dyh * yh, axis=-1, keepdims=True))

        def chunk(cc, carry):
            c = ncb - 1 - cc
            rows = pl.ds(pl.multiple_of(c * CHUNK, CHUNK), CHUNK)
            for h in range(RET_HEADS):
                hs = _head(h)
                qc, kc, vc, doc = _bf(q_ref[rows, hs]), _bf(k_ref[rows, hs]), _bf(v_ref[rows, hs]), _bf(do_scr[rows, hs])
                qdc, kdc = qd_ref[h], kd_ref[h]
                st, gs = _bf(st_ref[c, h]), g_scr[h]
                gsb = _bf(gs)
                dm = dm_ref[h]
                p = _bf(_dot(qc, kc, NT) * dm)
                da = _bf(_dot(doc, vc, NT) * dm)
                kt = _bf(k_ref[rows, hs] * kdc)
                qt = _bf(q_ref[rows, hs] * qdc)
                dz_ref[rows, 2 * RET_WIDTH + h * RET_HEAD_DIM:2 * RET_WIDTH + (h + 1) * RET_HEAD_DIM] = _bf(
                    _dot(p, doc, TN) + _dot(kt, gsb))
                dq_scr[rows, hs] = _dot(da, kc) + _dot(doc, st, NT) * qdc
                dk_scr[rows, hs] = _dot(da, qc, TN) + _dot(vc, gsb, NT) * kdc
                g_scr[h] = gs * lam_ref[h] + _dot(qt, doc, TN)
            return carry

        lax.fori_loop(0, ncb, chunk, 0, unroll=True)
        for h in range(RET_HEADS):
            hs = _head(h)
            dq, dk = dq_scr[:, hs], dk_scr[:, hs]
            dz_ref[:, h * RET_HEAD_DIM:(h + 1) * RET_HEAD_DIM] = _bf(dq * cosv - _swap_pairs(dq) * sinv)
            dz_ref[:, RET_WIDTH + h * RET_HEAD_DIM:RET_WIDTH + (h + 1) * RET_HEAD_DIM] = _bf(
                (dk * cosv - _swap_pairs(dk) * sinv) * scale)

    return pl.pallas_call(
        body, name=name, grid=(nblk,),
        out_shape=[SDS((s, AB_IN_WIDTH), BF16), SDS((1, RET_WIDTH), F32)],
        in_specs=_ret_common_specs(tb, rev, rotated=True)
        + [pl.BlockSpec((tb, RET_WIDTH), lambda i: (rev(i), 0)),
           pl.BlockSpec((ncb, RET_HEADS, RET_HEAD_DIM, RET_HEAD_DIM), lambda i: (rev(i), 0, 0, 0)),
           pl.BlockSpec((tb, RET_WIDTH), lambda i: (rev(i), 0))],
        out_specs=[pl.BlockSpec((tb, 4 * RET_WIDTH), lambda i: (rev(i), 0)),
                   pl.BlockSpec((1, RET_WIDTH), lambda i: (0, 0))],
        scratch_shapes=[pltpu.VMEM((RET_HEADS, RET_HEAD_DIM, RET_HEAD_DIM), F32)]
        + [pltpu.VMEM((tb, RET_WIDTH), F32)] * 3,
        compiler_params=_params(("arbitrary",)),
    )(qr, kr, z, z, tabs["cos"], tabs["sin"], gn_gain, tabs["dmat"], tabs["qdec"], tabs["kdec"], tabs["lam"],
      o_pre, states, du)


POOL_COL = 4 * RET_WIDTH // POOL_WIDTH


def _pooled(cur, prev, t0):
    tm = cur.shape[0]
    xx = jnp.concatenate([prev, cur], axis=0)
    sums = {1: xx}
    w = 1
    while w < POOL_WINDOWS[-1]:
        sums[2 * w] = sums[w] + pltpu.roll(sums[w], w, 0)
        w *= 2
    t = t0 + lax.broadcasted_iota(jnp.int32, (tm, 128), 0)
    outs = []
    for gi, w in enumerate(POOL_WINDOWS):
        cols = slice(gi * 128, (gi + 1) * 128)
        cnt = jnp.minimum(t + 1, w).astype(F32)
        outs.append(sums[w][POOL_HALO:, cols] / cnt - cur[:, cols])
    return outs


def _pool_fwd(z, w_pool, scale, name):
    s = z.shape[0]
    tm = _tile(s, TOKEN_TILE)
    hb = tm // POOL_HALO

    def body(p_ref, prev_ref, w_ref, sc_ref, o_ref):
        i = pl.program_id(0)
        prev = jnp.where(i > 0, prev_ref[...], 0.0)
        pooled = _pooled(p_ref[...], prev, i * tm)
        for gi in range(len(POOL_WINDOWS)):
            cols = slice(gi * 128, (gi + 1) * 128)
            o_ref[:, cols] = _bf(_dot(_bf(pooled[gi]), _bf(w_ref[gi])) * sc_ref[:, cols])

    return pl.pallas_call(
        body, name=name, grid=(s // tm,), out_shape=SDS((s, POOL_WIDTH), BF16),
        in_specs=[pl.BlockSpec((tm, POOL_WIDTH), lambda i: (i, POOL_COL)),
                  pl.BlockSpec((POOL_HALO, POOL_WIDTH), lambda i: (jnp.maximum(i * hb - 1, 0), POOL_COL)),
                  pl.BlockSpec(w_pool.shape, lambda i: (0, 0, 0)),
                  pl.BlockSpec((1, POOL_WIDTH), lambda i: (0, 0))],
        out_specs=pl.BlockSpec((tm, POOL_WIDTH), lambda i: (i, 0)),
        compiler_params=_params(("parallel",)),
    )(z, z, w_pool, scale)


def _pool_bwd(z, w_pool, scale, du, dz, name):
    s = z.shape[0]
    tm = _tile(s, TOKEN_TILE)
    hb = tm // POOL_HALO
    nblk = s // tm
    last_halo = s // POOL_HALO - 1

    def body(p_ref, prev_ref, w_ref, sc_ref, do_ref, don_ref, dz_ref, dp_ref, dw_ref, dsc_ref):
        i = pl.program_id(0)

        @pl.when(i == 0)
        def _():
            dw_ref[...] = jnp.zeros_like(dw_ref)
            dsc_ref[...] = jnp.zeros_like(dsc_ref)

        prev = jnp.where(i > 0, prev_ref[...], 0.0)
        pooled = _pooled(p_ref[...], prev, i * tm)
        dout = do_ref[...].astype(F32)
        dout_next = jnp.where(i < nblk - 1, don_ref[...].astype(F32), 0.0)
        sc = sc_ref[...]
        dmix = jnp.concatenate([dout * sc, dout_next * sc], axis=0)
        n = tm + POOL_HALO
        t = i * tm + lax.broadcasted_iota(jnp.int32, (n, 128), 0)
        for gi, w in enumerate(POOL_WINDOWS):
            cols = slice(gi * 128, (gi + 1) * 128)
            wg = _bf(w_ref[gi])
            pg = _bf(pooled[gi])
            dsc_ref[:, cols] += jnp.sum(dout[:, cols] * _dot(pg, wg), axis=0, keepdims=True)
            dw_ref[gi] += _dot(pg, _bf(dmix[:tm, cols]), TN)
            dpool = _dot(_bf(dmix[:, cols]), wg, NT)
            acc = dpool / jnp.minimum(t + 1, w).astype(F32)
            step = 1
            while step < w:
                acc = acc + pltpu.roll(acc, n - step, 0)
                step *= 2
            dp_ref[:, cols] = _bf(acc[:tm] - dpool[:tm])

    return pl.pallas_call(
        body, name=name, grid=(nblk,),
        out_shape=[SDS(dz.shape, BF16), SDS(w_pool.shape, F32), SDS((1, POOL_WIDTH), F32)],
        in_specs=[pl.BlockSpec((tm, POOL_WIDTH), lambda i: (i, POOL_COL)),
                  pl.BlockSpec((POOL_HALO, POOL_WIDTH), lambda i: (jnp.maximum(i * hb - 1, 0), POOL_COL)),
                  pl.BlockSpec(w_pool.shape, lambda i: (0, 0, 0)),
                  pl.BlockSpec((1, POOL_WIDTH), lambda i: (0, 0)),
                  pl.BlockSpec((tm, POOL_WIDTH), lambda i: (i, 1)),
                  pl.BlockSpec((POOL_HALO, POOL_WIDTH), lambda i: (jnp.minimum((i + 1) * hb, last_halo), 1)),
                  pl.BlockSpec(memory_space=pl.ANY)],
        out_specs=[pl.BlockSpec((tm, POOL_WIDTH), lambda i: (i, POOL_COL)),
                   pl.BlockSpec(w_pool.shape, lambda i: (0, 0, 0)),
                   pl.BlockSpec((1, POOL_WIDTH), lambda i: (0, 0))],
        input_output_aliases={6: 0},
        compiler_params=_params(("arbitrary",)),
    )(z, z, w_pool, scale, du, du, dz)


def _rel_onehot(offset=0):
    r = lax.broadcasted_iota(jnp.int32, (REL_PAD, ATT_DIAG), 0)
    c = lax.broadcasted_iota(jnp.int32, (REL_PAD, ATT_DIAG), 1) - offset
    rel = jnp.where((c >= 0) & (c < ATT_K_TILE), jnp.clip(LEFT_CHUNKS * CHUNK - c, -REL_CLIP, REL_CLIP) + REL_CLIP,
                    2 * REL_CLIP)
    return (rel == r).astype(BF16)


def _split3(v):
    hi = _bf(v)
    r1 = v - hi.astype(F32)
    mid = _bf(r1)
    return hi, mid, _bf(r1 - mid.astype(F32))


def _skew(v):
    return pltpu.roll(v, 0, 1, stride=1, stride_axis=0)


def _attn_bias(rel_bias, name):
    def body(t_ref, o_ref):
        oh = _rel_onehot()
        base = sum(_dot(part, oh) for part in _split3(t_ref[0]))
        full = _skew(jnp.broadcast_to(base[0:1], (ATT_Q_TILE, ATT_DIAG)))[:, :ATT_K_TILE]
        qc = lax.broadcasted_iota(jnp.int32, full.shape, 0) // CHUNK
        kc = lax.broadcasted_iota(jnp.int32, full.shape, 1) // CHUNK
        o_ref[0] = jnp.where((kc >= qc) & (kc <= qc + LEFT_CHUNKS), full, NEG_INF)

    t8 = jnp.broadcast_to(rel_bias[:, None, :], (ATT_HEADS, 8, REL_PAD))
    return pl.pallas_call(
        body, name=name, grid=(ATT_HEADS,), out_shape=SDS((ATT_HEADS, ATT_Q_TILE, ATT_K_TILE), F32),
        in_specs=[pl.BlockSpec((1, 8, REL_PAD), lambda h: (h, 0, 0))],
        out_specs=pl.BlockSpec((1, ATT_Q_TILE, ATT_K_TILE), lambda h: (h, 0, 0)),
        compiler_params=_params(("parallel",)),
    )(t8)


def _attn_dbias(dbias, name):
    def body(d_ref, o_ref):
        pad = jnp.zeros((ATT_Q_TILE, ATT_DIAG - ATT_K_TILE), F32)
        row = lax.broadcasted_iota(jnp.int32, (ATT_Q_TILE, ATT_Q_TILE), 0)
        lane = lax.broadcasted_iota(jnp.int32, (ATT_Q_TILE, ATT_Q_TILE), 1)
        flip = (row + lane == ATT_Q_TILE - 1).astype(BF16)
        upside = sum(_dot(flip, part) for part in _split3(jnp.concatenate([d_ref[0], pad], axis=1)))
        col = jnp.sum(_skew(upside), axis=0, keepdims=True)
        oh = _rel_onehot(ATT_Q_TILE - 1)
        col8 = jnp.broadcast_to(col, (8, ATT_DIAG))
        o_ref[0] = sum(_dot(part, oh, NT) for part in _split3(col8))

    out = pl.pallas_call(
        body, name=name, grid=(ATT_HEADS,), out_shape=SDS((ATT_HEADS, 8, REL_PAD), F32),
        in_specs=[pl.BlockSpec((1, ATT_Q_TILE, ATT_K_TILE), lambda h: (h, 0, 0))],
        out_specs=pl.BlockSpec((1, 8, REL_PAD), lambda h: (h, 0, 0)),
        compiler_params=_params(("parallel",)),
    )(dbias)
    return out[:, 0, :]


ATT_WIDTH = 128 * ATT_PAIRS
ATT_GROUPS = D_MODEL // ATT_WIDTH


def _attn_specs(nq):
    def tile(off, back):
        return pl.BlockSpec((ATT_Q_TILE, ATT_WIDTH),
                            lambda g, i: (jnp.maximum(jnp.minimum(i, nq - 1) - back, 0), off + g))

    backs = [ATT_BACK - b for b in range(ATT_BACK + 1)]
    return ([tile(0, 0)] + [tile(ATT_GROUPS, b) for b in backs] + [tile(2 * ATT_GROUPS, b) for b in backs]
            + [pl.BlockSpec((2 * ATT_PAIRS, ATT_Q_TILE, ATT_K_TILE), lambda g, i: (g, 0, 0))])


def _attn_weights(qh, k2, bias, i, masked):
    sc = _dot(qh, k2, NT) + bias
    if masked:
        kpos = (i - ATT_BACK) * ATT_Q_TILE + lax.broadcasted_iota(jnp.int32, sc.shape, 1)
        sc = jnp.where(kpos >= 0, sc, NEG_INF)
    e = jnp.exp(sc - jnp.max(sc, axis=-1, keepdims=True))
    return e, 1.0 / jnp.sum(e, axis=-1, keepdims=True)


def _first_head():
    return lax.broadcasted_iota(jnp.int32, (ATT_Q_TILE, 128), 1) < ATT_HEAD_DIM


def _pair_operands(q_ref, k_refs, v_refs, pp):
    cols = slice(pp * 128, (pp + 1) * 128)
    q2 = q_ref[:, cols] * ATT_HEAD_DIM ** -0.5
    k2 = jnp.concatenate([r[:, cols] for r in k_refs], axis=0)
    v2 = jnp.concatenate([r[:, cols] for r in v_refs], axis=0)
    return cols, q2, k2, v2


def _attn_fwd(z, bias, name):
    s = z.shape[0]
    nq = s // ATT_Q_TILE
    nt = ATT_BACK + 1

    def body(q_ref, *rest):
        k_refs, v_refs, (b_ref, o_ref) = rest[:nt], rest[nt:2 * nt], rest[2 * nt:]
        i = pl.program_id(1)
        first = _first_head()

        def compute(masked):
            for pp in range(ATT_PAIRS):
                cols, q2, k2, v2 = _pair_operands(q_ref, k_refs, v_refs, pp)
                outs = []
                for hh in range(2):
                    qh = jnp.where(first if hh == 0 else ~first, q2, 0)
                    e, inv = _attn_weights(qh, k2, b_ref[2 * pp + hh], i, masked)
                    outs.append(_dot(_bf(e), v2) * inv)
                o_ref[:, cols] = _bf(jnp.where(first, outs[0], outs[1]))

        pl.when(i < ATT_BACK)(lambda: compute(True))
        pl.when(i >= ATT_BACK)(lambda: compute(False))

    return pl.pallas_call(
        body, name=name, grid=(ATT_GROUPS, nq), out_shape=SDS((s, D_MODEL), BF16),
        in_specs=_attn_specs(nq),
        out_specs=pl.BlockSpec((ATT_Q_TILE, ATT_WIDTH), lambda g, i: (i, g)),
        compiler_params=_params(("parallel", "parallel")),
    )(*([z] * (1 + 2 * nt)), bias)


def _attn_bwd(z, bias, o, do, name):
    s = z.shape[0]
    nq = s // ATT_Q_TILE
    nt = ATT_BACK + 1

    def body(q_ref, *rest):
        k_refs, v_refs = rest[:nt], rest[nt:2 * nt]
        b_ref, o_ref, do_ref, dq_ref, dk_ref, dv_ref, db_ref, dk_acc, dv_acc = rest[2 * nt:]
        i = pl.program_id(1)
        first = _first_head()

        @pl.when(i == 0)
        def _():
            db_ref[...] = jnp.zeros_like(db_ref)
            dk_acc[...] = jnp.zeros_like(dk_acc)
            dv_acc[...] = jnp.zeros_like(dv_acc)

        def compute(masked):
            for pp in range(ATT_PAIRS):
                cols, q2, k2, v2 = _pair_operands(q_ref, k_refs, v_refs, pp)
                do2 = do_ref[:, cols].astype(F32)
                prod = do2 * o_ref[:, cols].astype(F32)
                dqs, dk, dv = [], None, None
                for hh in range(2):
                    mine = first if hh == 0 else ~first
                    qh = jnp.where(mine, q2, 0)
                    e, inv = _attn_weights(qh, k2, b_ref[2 * pp + hh], i, masked)
                    delta = jnp.sum(jnp.where(mine, prod, 0.0), axis=-1, keepdims=True) * inv
                    doh = _bf(jnp.where(mine, do2 * inv, 0.0))
                    ds = e * (_dot(doh, v2, NT) - delta)
                    db_ref[2 * pp + hh] += ds
                    dsb = _bf(ds)
                    dqs.append(_dot(dsb, k2))
                    dkh, dvh = _dot(dsb, qh, TN), _dot(_bf(e), doh, TN)
                    dk, dv = (dkh, dvh) if hh == 0 else (dk + dkh, dv + dvh)
                dq_ref[:, cols] = _bf(jnp.where(first, dqs[0], dqs[1]) * ATT_HEAD_DIM ** -0.5)
                for b in range(nt):
                    slot = (i + b + 1) % nt
                    rows = slice(b * ATT_Q_TILE, (b + 1) * ATT_Q_TILE)
                    if b < ATT_BACK:
                        dk_acc[slot, :, cols] += dk[rows]
                        dv_acc[slot, :, cols] += dv[rows]
                    else:
                        dk_acc[slot, :, cols] = dk[rows]
                        dv_acc[slot, :, cols] = dv[rows]

        pl.when(i < ATT_BACK)(lambda: compute(True))
        pl.when((i >= ATT_BACK) & (i < nq))(lambda: compute(False))
        done = (i + 1) % nt
        dk_ref[...] = _bf(dk_acc[done])
        dv_ref[...] = _bf(dv_acc[done])

    tile = pl.BlockSpec((ATT_Q_TILE, ATT_WIDTH), lambda g, i: (jnp.minimum(i, nq - 1), g))
    late = pl.BlockSpec((ATT_Q_TILE, ATT_WIDTH), lambda g, i: (jnp.maximum(i - ATT_BACK, 0), g))
    ring = pltpu.VMEM((nt, ATT_Q_TILE, ATT_WIDTH), F32)
    return pl.pallas_call(
        body, name=name, grid=(ATT_GROUPS, nq + ATT_BACK),
        out_shape=[SDS((s, D_MODEL), BF16)] * 3 + [SDS((ATT_HEADS, ATT_Q_TILE, ATT_K_TILE), F32)],
        in_specs=_attn_specs(nq) + [tile, tile],
        out_specs=[tile, late, late, pl.BlockSpec((2 * ATT_PAIRS, ATT_Q_TILE, ATT_K_TILE), lambda g, i: (g, 0, 0))],
        scratch_shapes=[ring, ring],
        compiler_params=_params(("parallel", "arbitrary")),
    )(*([z] * (1 + 2 * nt)), bias, o, do)


FWD_GROUPS = (
    (("ab_w_in", 0),),
    (("ab_w_out", 0), ("w_ffn_in", 0)),
    (("w_ffn_out", 0),),
    (("c_w_qkv", 0),),
    (("c_w_out", 0), ("w_ffn_in", 1), ("w_ffn_out", 1), ("ab_w_in", 1)),
    (("ab_w_out", 1), ("w_ffn_in", 2), ("w_ffn_out", 2), ("c_w_qkv", 1)),
    (("c_w_out", 1), ("w_ffn_in", 3), ("w_ffn_out", 3)),
)


def _local_step(x, target, small, comm):
    s = x.shape[0]
    tabs = _retention_tables(s)
    saved, w = [], comm.weight
    for layer in range(DEPTH):
        i = layer // 2
        sv = {"x0": x}
        g_mix = small["mix_norm"][layer:layer + 1]
        if layer % 2 == 0:
            at = (x,) + tuple(tabs.values()) if layer == 0 else x
            sv["h1"], sv["z"] = _norm_mm(x, g_mix, w("ab_w_in", i, at), AB_IN_WIDTH, F32, False, "ab_in_fwd")
            gn = small["ab_gn_gain"][i:i + 1]
            sv["o_pre"], sv["states"], ret, sv["qr"], sv["kr"] = _ret_fwd(sv["z"], tabs, gn, "ret_fwd")
            pool = _pool_fwd(sv["z"], small["ab_w_pool"][i], small["ab_pool_scale"][i:i + 1], "pool_fwd")
            sv["u"] = (ret, pool)
            mixed = ([ret, pool], w("ab_w_out", i, ret))
        else:
            sv["h1"], sv["z"] = _norm_mm(x, g_mix, w("c_w_qkv", i, x), 3 * D_MODEL // N_DEV, BF16, False, "qkv_fwd")
            rb = jnp.pad(small["c_rel_bias"][i], ((0, 0), (0, REL_PAD - N_REL)))
            sv["bias"] = _attn_bias(rb, "attn_bias")
            sv["o"] = _attn_fwd(sv["z"], sv["bias"], "attn_fwd")
            mixed = ([sv["o"]], w("c_w_out", i, sv["o"]))
        x, sv["h2"], sv["z1"], sv["a"] = _norm_mm(x, small["ffn_norm"][layer:layer + 1], w("w_ffn_in", layer, mixed[0][0]),
                                                  D_FF // N_DEV, BF16, True, "ffn_in_fwd", mixed)
        sv["x1"] = x
        x = _mm_res([sv["a"]], w("w_ffn_out", layer, sv["a"]), x, "ffn_out_fwd")
        saved.append(sv)

    loss, dx, d_final = _final_loss(x, small["final_norm"][None, :], target, "final_loss")

    gs = {k: [None] * DEPTH for k in ("mix_norm", "ffn_norm")}
    for k in ("ab_gn_gain", "ab_w_pool", "ab_pool_scale", "c_rel_bias"):
        gs[k] = [None] * (DEPTH // 2)
    gs["final_norm"] = d_final[0]
    sent = None
    for layer in reversed(range(DEPTH)):
        i = layer // 2
        sv = saved[layer]
        dz1 = _mm_nt_rows(dx, w("w_ffn_out", layer), sv["z1"], "ffn_out_bwd")
        gw = {("w_ffn_out", layer): _mm_tn(sv["a"], dx, 1024, D_MODEL, True, (0, 2), "ffn_out_dw"),
              ("w_ffn_in", layer): _mm_tn(sv["h2"], dz1, D_MODEL, 1024, False, (1, 2), "ffn_in_dw",
                                          comm.after() if layer == 0 else None)}
        w_out = w("ab_w_out" if layer % 2 == 0 else "c_w_out", i)
        dx, dg, du = _mm_nt_normbwd(dz1, w("w_ffn_in", layer), sv["x1"], small["ffn_norm"][layer:layer + 1], dx,
                                    "ffn_in_bwd", sent, w_out)
        gs["ffn_norm"][layer] = dg[0]
        sent = None
        g_mix = small["mix_norm"][layer:layer + 1]
        if layer % 2 == 0:
            gw["ab_w_out", i] = _mm_tn(sv["u"], dx, D_MODEL, D_MODEL, True, (0, N_DEV), "mix_out_dw")
            if layer == 0:
                sent, gw = comm.send(gw), {}
            gn = small["ab_gn_gain"][i:i + 1]
            dz, dgn = _ret_bwd(sv["z"], sv["qr"], sv["kr"], tabs, gn, sv["o_pre"], sv["states"], du, "ret_bwd")
            dz, dwp, dsc = _pool_bwd(sv["z"], small["ab_w_pool"][i], small["ab_pool_scale"][i:i + 1], du, dz, "pool_bwd")
            gs["ab_gn_gain"][i], gs["ab_w_pool"][i], gs["ab_pool_scale"][i] = dgn[0], dwp, dsc[0]
            gw["ab_w_in", i] = _to_shard_major(_mm_tn(sv["h1"], dz, D_MODEL, AB_IN_WIDTH // 2, False, (1, 1), "ab_in_dw",
                                                      comm.after()), pairs_split=True)
            dx, dg = _mm_nt_normbwd(dz, w("ab_w_in", i), sv["x0"], g_mix, dx, "ab_in_bwd", sent)
        else:
            do = du
            gw["c_w_out", i] = _mm_tn(sv["o"], dx, D_MODEL, D_MODEL, True, (0, N_DEV), "mix_out_dw")
            dq, dk, dv, dbias = _attn_bwd(sv["z"], sv["bias"], sv["o"], do, "attn_bwd")
            gs["c_rel_bias"][i] = _attn_dbias(dbias, "attn_dbias")[:, :N_REL]
            tiles = [_mm_tn(sv["h1"], part, D_MODEL, D_MODEL, False, (1, 1), "qkv_dw", after)
                     for part, after in ((dq, None), (dk, None), (dv, comm.after()))]
            gw["c_w_qkv", i] = _to_shard_major(jnp.concatenate(tiles, axis=0))
            dx, dg = _mm_nt_normbwd([dq, dk, dv], w("c_w_qkv", i), sv["x0"], g_mix, dx, "qkv_bwd", sent)
        gs["mix_norm"][layer] = dg[0]
        if layer > 0:
            sent = comm.send(gw)
    gsmall = {k: (jnp.stack(v) if isinstance(v, list) else v) for k, v in gs.items()}
    return loss, dx, gw, gsmall


BIG = ("w_ffn_in", "w_ffn_out", "ab_w_in", "ab_w_out", "c_w_qkv", "c_w_out")
SMALL = ("mix_norm", "ffn_norm", "ab_gn_gain", "ab_w_pool", "ab_pool_scale", "c_rel_bias", "final_norm")
N_PEERS = N_DEV - 1
FLIPS = [(fx, fy, fc) for fx in (0, 1) for fy in (0, 1) for fc in (0, 1)][1:]


def _peers():
    x, y, c = (lax.axis_index(a) for a in MESH_AXES)
    peers = []
    for fx, fy, fc in FLIPS:
        px, py, pc = (1 - x if fx else x), (1 - y if fy else y), (1 - c if fc else c)
        peers.append(((px, py, pc), 4 * px + 2 * py + pc))
    return 4 * x + 2 * y + c, peers


def _exchange(srcs, by_slot, name, collective_id):
    n = len(srcs)
    src_refs = [jax.new_ref(a, memory_space=pltpu.MemorySpace.HBM) for a in srcs]
    land_refs = [jax.empty_ref(SDS((N_DEV,) + (a.shape[1:] if slotted else a.shape), a.dtype),
                               memory_space=pltpu.MemorySpace.HBM) for a, slotted in zip(srcs, by_slot)]

    @pl.kernel(mesh=plsc.ScalarSubcoreMesh(axis_name="sequencer", num_cores=1), name=name,
               scratch_types=(pltpu.SemaphoreType.DMA((n * N_PEERS,)), pltpu.SemaphoreType.DMA((n * N_PEERS,)),
                              pltpu.SemaphoreType.DMA((n,))),
               compiler_params=pltpu.CompilerParams(collective_id=collective_id))
    def launch(send_sems, recv_sems, local_sems):
        me, peers = _peers()
        barrier = pltpu.get_barrier_semaphore()
        for pos, _ in peers:
            pl.semaphore_signal(barrier, inc=1, device_id=pos, device_id_type=pl.DeviceIdType.MESH)
        pl.semaphore_wait(barrier, N_PEERS)
        waits = []
        for k in range(n):
            own = pltpu.make_async_copy(src_refs[k].at[me] if by_slot[k] else src_refs[k], land_refs[k].at[me],
                                        local_sems.at[k])
            own.start()
            waits.append(own.wait)
            for rel, (pos, slot) in enumerate(peers):
                src = src_refs[k].at[slot] if by_slot[k] else src_refs[k]
                sems = dict(send_sem=send_sems.at[k * N_PEERS + rel], recv_sem=recv_sems.at[k * N_PEERS + rel],
                            device_id=pos, device_id_type=pl.DeviceIdType.MESH)
                send = pltpu.make_async_remote_copy(src_ref=src, dst_ref=land_refs[k].at[me], **sems)
                send.start()
                arrival = pltpu.make_async_remote_copy(src_ref=src, dst_ref=land_refs[k].at[slot], **sems)
                waits += [send.wait_send, arrival.wait_recv]
        for wait in waits:
            wait()

    launch()
    return [r[...] for r in land_refs]


def _cast_group(weights, keys, after, name):
    after = _follow(after)

    def body(*refs):
        n = len(keys)
        for i_ref, o_ref in zip(refs[:n], refs[n + len(after):]):
            o_ref[...] = _bf(i_ref[...])

    def layer_spec(shape, l):
        return pl.BlockSpec((None,) + shape[1:], lambda i: (l, 0, 0))

    whole = lambda shape: pl.BlockSpec(shape, lambda i: (0, 0))
    ins = [weights[k] for k, _ in keys]
    return pl.pallas_call(
        body, name=name, grid=(1,), out_shape=[SDS(w.shape[1:], BF16) for w in ins],
        in_specs=[layer_spec(w.shape, l) for w, (_, l) in zip(ins, keys)] + [FOLLOW] * len(after),
        out_specs=[whole(w.shape[1:]) for w in ins],
        compiler_params=_params(("arbitrary",)),
    )(*ins, *after)


def _to_shard_major(g, pairs_split=False):
    nj, ka, nb = g.shape
    full = jnp.transpose(g, (1, 0, 2)).reshape(ka, nj * nb)
    if pairs_split:
        full = _split_pairs(full, inverse=True)
    return jnp.transpose(full.reshape(ka, N_DEV, nj * nb // N_DEV), (1, 0, 2))


def _from_gathered(name, g):
    if name in ("w_ffn_out", "ab_w_out", "c_w_out"):
        return g.reshape(g.shape[0] * g.shape[1], g.shape[2])
    if name == "ab_w_in":
        return _split_pairs(jnp.transpose(g, (1, 0, 2)).reshape(1, g.shape[1], N_DEV * g.shape[2]))
    return g


class _Comm:
    def __init__(self, weights):
        self.weights_f32 = weights
        self.gathered = {}
        self.got = {}
        self.calls = 0
        self.ended = []
        self.opened = -1

    def _exchange(self, srcs, by_slot, name):
        self.calls += 1
        got = _exchange(srcs, by_slot, name, self.calls)
        self.ended = got[:1]
        return got

    def _gather(self, group, at):
        keys = FWD_GROUPS[group]
        shards = _cast_group(self.weights_f32, keys, self.ended + _follow(at), "cast_%d" % group)
        got = self._exchange(shards, [False] * len(keys), "gather_%d" % group)
        self.gathered.update((k, _from_gathered(k[0], arr)) for k, arr in zip(keys, got))

    def weight(self, name, layer, at=None):
        if (name, layer) not in self.gathered:
            self._gather(0, at[0] if isinstance(at, tuple) else at)
        group = next(g for g, keys in enumerate(FWD_GROUPS) if (name, layer) in keys)
        if group == self.opened + 1:
            self.opened = group
            if group + 1 < len(FWD_GROUPS):
                self._gather(group + 1, at)
        return self.gathered[name, layer]

    def after(self):
        return self.ended

    def send(self, grads, shared=None):
        shared = shared or {}
        keys = list(grads) + list(shared)
        srcs = list(grads.values()) + list(shared.values())
        got = self._exchange(srcs, [True] * len(grads) + [False] * len(shared), "scatter_%d" % self.calls)
        self.got.update(zip(keys, got))
        return list(grads.values())

    def received(self):
        return self.got


def _adamw_math(g, w, m, v):
    m2 = ADAM_B1 * m + (1.0 - ADAM_B1) * g
    v2 = ADAM_B2 * v + (1.0 - ADAM_B2) * jnp.square(g)
    m_hat = m2 / (1.0 - ADAM_B1 ** ADAM_STEP)
    v_hat = v2 / (1.0 - ADAM_B2 ** ADAM_STEP)
    delta = -ADAM_LR * (m_hat / (jnp.sqrt(v_hat) + ADAM_EPS) + ADAM_WD * w)
    return delta, m2, v2


def _adamw(recv, w, m, v, name):
    nl, r, c = w.shape
    tr = _tile(r, 256)

    def body(*refs):
        g_refs = refs[:nl]
        w_ref, m_ref, v_ref, go_ref, d_ref, mo_ref, vo_ref = refs[nl:]
        for l in range(nl):
            @pl.when(pl.program_id(0) == l)
            def _():
                g = g_refs[l][0].astype(F32)
                for p in range(1, N_DEV):
                    g = g + g_refs[l][p].astype(F32)
                go_ref[...] = g
                d_ref[...], mo_ref[...], vo_ref[...] = _adamw_math(g, w_ref[...], m_ref[...], v_ref[...])

    def recv_spec(l):
        return pl.BlockSpec((N_DEV, tr, c), lambda layer, i: (0, jnp.where(layer == l, i, 0), 0))

    blk = pl.BlockSpec((None, tr, c), lambda l, i: (l, i, 0))
    return pl.pallas_call(
        body, name=name, grid=(nl, r // tr), out_shape=[SDS(w.shape, F32)] * 4,
        in_specs=[recv_spec(l) for l in range(nl)] + [blk, blk, blk],
        out_specs=[blk] * 4,
        compiler_params=_params(("arbitrary", "arbitrary")),
    )(*recv, w, m, v)


def _adamw_small(recv, loss_parts, w, m, v, name):
    n = len(w)

    def total(ref):
        t = ref[0]
        for p in range(1, N_DEV):
            t = t + ref[p]
        return t

    def body(*refs):
        g_refs, loss_ref = refs[:n], refs[n]
        w_refs, m_refs, v_refs = refs[n + 1:2 * n + 1], refs[2 * n + 1:3 * n + 1], refs[3 * n + 1:4 * n + 1]
        outs = refs[4 * n + 1:]
        for i in range(n):
            g = total(g_refs[i])
            outs[4 * i][...] = g
            outs[4 * i + 1][...], outs[4 * i + 2][...], outs[4 * i + 3][...] = _adamw_math(
                g, w_refs[i][...], m_refs[i][...], v_refs[i][...])
        outs[4 * n][...] = total(loss_ref)

    out_shape = [SDS(p.shape, F32) for p in w for _ in range(4)] + [SDS(loss_parts.shape[1:], F32)]
    outs = pl.pallas_call(body, name=name, out_shape=out_shape,
                          compiler_params=_params(None))(*recv, loss_parts, *w, *m, *v)
    return [outs[4 * i:4 * i + 4] for i in range(n)], outs[-1]


def kernel(x, mix_norm, ffn_norm, w_ffn_in, w_ffn_out, ab_w_in, ab_gn_gain, ab_w_pool, ab_pool_scale, ab_w_out, c_w_qkv, c_rel_bias, c_w_out, final_norm, loss_target, m_mix_norm, m_ffn_norm, m_w_ffn_in, m_w_ffn_out, m_ab_w_in, m_ab_gn_gain, m_ab_w_pool, m_ab_pool_scale, m_ab_w_out, m_c_w_qkv, m_c_rel_bias, m_c_w_out, m_final_norm, v_mix_norm, v_ffn_norm, v_w_ffn_in, v_w_ffn_out, v_ab_w_in, v_ab_gn_gain, v_ab_w_pool, v_ab_pool_scale, v_ab_w_out, v_c_w_qkv, v_c_rel_bias, v_c_w_out, v_final_norm):
    args = dict(locals())
    weights = {k: args[k] for k in BIG + SMALL}
    moments_m = {k: args["m_" + k] for k in BIG + SMALL}
    moments_v = {k: args["v_" + k] for k in BIG + SMALL}

    small = {k: weights[k] for k in SMALL}
    rows = lambda a: a.reshape(1, -1) if a.ndim == 1 else a

    comm = _Comm(weights)
    loss, dx, last_grads, gsmall = _local_step(x[0], loss_target[0], small, comm)
    comm.send(last_grads, {**{k: rows(gsmall[k]) for k in SMALL}, "loss": loss})
    recv = comm.received()

    outs = {}
    for k in BIG:
        layers = [recv[k, l] for l in range(weights[k].shape[0])]
        outs[k] = _adamw(layers, weights[k], moments_m[k], moments_v[k], "adamw_" + k)
    updated, total = _adamw_small([recv[k] for k in SMALL], recv["loss"], [rows(small[k]) for k in SMALL],
                                  [rows(moments_m[k]) for k in SMALL], [rows(moments_v[k]) for k in SMALL], "adamw_small")
    for k, parts in zip(SMALL, updated):
        outs[k] = [p.reshape(small[k].shape) for p in parts]

    order = SMALL[:2] + BIG[:2] + ("ab_w_in", "ab_gn_gain", "ab_w_pool", "ab_pool_scale", "ab_w_out",
                                   "c_w_qkv", "c_rel_bias", "c_w_out", "final_norm")
    result = [total[0, 0], dx[None]]
    for part in range(4):
        result += [outs[k][part] for k in order]
    return tuple(result)
```

```python
import functools

import jax
import jax.numpy as jnp
from jax import lax
from jax.experimental import pallas as pl
from jax.experimental.pallas import tpu as pltpu
from jax.experimental.pallas import tpu_sc as plsc

F32 = jnp.float32
BF16 = jnp.bfloat16
SDS = jax.ShapeDtypeStruct
MESH_AXES = ("x", "y", "c")
N_DEV = 8

D_MODEL = 1024
DEPTH = 4
CHUNK = 64
D_FF = 4 * D_MODEL
RMS_EPS = 1e-6
RET_WIDTH = 512
RET_HEADS = 4
RET_HEAD_DIM = 128
RET_ROPE_BASE = 10000.0
GN_EPS = 1e-5
POOL_WIDTH = 512
POOL_WINDOWS = (2, 4, 8, 16)
POOL_HALO = 16
AB_IN_WIDTH = 4 * RET_WIDTH + POOL_WIDTH
ATT_HEADS = 16
ATT_HEAD_DIM = 64
LEFT_CHUNKS = 8
REL_CLIP = 128
N_REL = 2 * REL_CLIP + 1
NEG_INF = -1e30

ADAM_LR = 0.001
ADAM_B1 = 0.9
ADAM_B2 = 0.999
ADAM_EPS = 1e-08
ADAM_WD = 0.01
ADAM_STEP = 10

TOKEN_TILE = 512
ATT_Q_TILE = 256
ATT_BACK = LEFT_CHUNKS * CHUNK // ATT_Q_TILE
ATT_K_TILE = (ATT_BACK + 1) * ATT_Q_TILE
ATT_PAIRS = 4
ATT_DIAG = 1024
REL_PAD = 384
VMEM_LIMIT_MB = 56

NT = (((1,), (1,)), ((), ()))
TN = (((0,), (0,)), ((), ()))


def _params(semantics, **kw):
    return pltpu.CompilerParams(dimension_semantics=semantics,
                                vmem_limit_bytes=VMEM_LIMIT_MB * 2 ** 20, **kw)


def _dot(a, b, dims=None):
    if dims is None:
        return jnp.dot(a, b, preferred_element_type=F32)
    return lax.dot_general(a, b, dims, preferred_element_type=F32)


def _bf(v):
    return v.astype(BF16)


def _tile(n, t):
    return min(n, t)


FOLLOW = pl.BlockSpec(memory_space=pl.ANY)


def _follow(after):
    return [] if after is None else list(after) if isinstance(after, (list, tuple)) else [after]


MXU_WIDTH = 256


def _mxu_group(nj, tn):
    return 2 if tn % MXU_WIDTH and (2 * tn) % MXU_WIDTH == 0 and nj % 2 == 0 else 1


def _w_tiles(w_ref, j, group):
    return w_ref[j] if group == 1 else jnp.concatenate([w_ref[j + t] for t in range(group)], axis=1)


def _w_cols(w_ref, j, group, c, width):
    return w_ref[j, :, c:c + width] if group == 1 else _w_tiles(w_ref, j, group)


def _norm_mm(x, gain, w, tn, z_dtype, relu2, name, pre=None):
    s, d = x.shape
    nj = w.shape[0]
    tm = _tile(s, TOKEN_TILE)
    group = _mxu_group(nj, tn)
    pre_parts, w_pre = pre if pre else ([], None)
    widths = [p.shape[1] for p in pre_parts]

    def body(*refs):
        p_refs = refs[:len(pre_parts)]
        refs = refs[len(pre_parts):]
        if pre:
            wp_ref, x_ref, g_ref, w_ref, x1_ref, h_ref, z_ref, *a_ref = refs
            xv, off = x_ref[...], 0
            for p_ref, k in zip(p_refs, widths):
                xv = xv + _dot(p_ref[...], wp_ref[off:off + k, :])
                off += k
            x1_ref[...] = xv
        else:
            x_ref, g_ref, w_ref, h_ref, z_ref, *a_ref = refs
            xv = x_ref[...]
        r = lax.rsqrt(jnp.mean(xv * xv, axis=-1, keepdims=True) + RMS_EPS)
        h = _bf(xv * r * g_ref[...])
        h_ref[...] = h
        cw = tn if tn <= 512 else 512
        for j in range(0, nj, group):
            for c in range(0, tn, cw):
                z = _dot(h, _w_cols(w_ref, j, group, c, cw))
                cols = slice(j * tn + c, j * tn + c + group * cw)
                z_ref[:, cols] = z.astype(z_ref.dtype)
                if relu2:
                    a_ref[0][:, cols] = _bf(jnp.square(jnp.maximum(z, 0.0)))

    n = nj * tn
    rows = lambda width: pl.BlockSpec((tm, width), lambda i: (i, 0))
    out_shape = [SDS((s, d), F32)] * bool(pre) + [SDS((s, d), BF16), SDS((s, n), z_dtype)] + [SDS((s, n), BF16)] * relu2
    out_specs = [rows(d)] * bool(pre) + [rows(d), rows(n)] + [rows(n)] * relu2
    return pl.pallas_call(
        body, name=name, grid=(s // tm,), out_shape=out_shape,
        in_specs=[rows(k) for k in widths] + ([pl.BlockSpec(w_pre.shape, lambda i: (0, 0))] if pre else [])
        + [rows(d), pl.BlockSpec((1, d), lambda i: (0, 0)), pl.BlockSpec((nj, d, tn), lambda i: (0, 0, 0))],
        out_specs=out_specs,
        compiler_params=_params(("parallel",)),
    )(*pre_parts, *([w_pre] if pre else []), x, gain, w)


def _mm_res(parts, w, res, name):
    s, d = res.shape
    tm = _tile(s, TOKEN_TILE)
    widths = [p.shape[1] for p in parts]

    def body(*refs):
        a_refs = refs[:len(parts)]
        w_ref, res_ref, o_ref = refs[len(parts):]
        acc = res_ref[...]
        off = 0
        for a_ref, k in zip(a_refs, widths):
            acc = acc + _dot(a_ref[...], w_ref[off:off + k, :])
            off += k
        o_ref[...] = acc

    return pl.pallas_call(
        body, name=name, grid=(s // tm,), out_shape=SDS((s, d), F32),
        in_specs=[pl.BlockSpec((tm, k), lambda i: (i, 0)) for k in widths]
        + [pl.BlockSpec(w.shape, lambda i: (0, 0)), pl.BlockSpec((tm, d), lambda i: (i, 0))],
        out_specs=pl.BlockSpec((tm, d), lambda i: (i, 0)),
        compiler_params=_params(("parallel",)),
    )(*parts, w, res)


def _mm_nt_rows(dy, w, z, name):
    s, d = dy.shape
    k = w.shape[0]
    tm = _tile(s, TOKEN_TILE)
    tk = _tile(k, 1024)

    def body(dy_ref, w_ref, *rest):
        o_ref = rest[-1]
        dyb = _bf(dy_ref[...])
        for j in range(k // tk):
            cols = slice(j * tk, (j + 1) * tk)
            da = _dot(dyb, w_ref[cols, :], NT)
            if z is not None:
                da = da * (2.0 * jnp.maximum(rest[0][:, cols].astype(F32), 0.0))
            o_ref[:, cols] = _bf(da)

    in_specs = [pl.BlockSpec((tm, d), lambda i: (i, 0)), pl.BlockSpec((k, d), lambda i: (0, 0))]
    args = [dy, w]
    if z is not None:
        in_specs.append(pl.BlockSpec((tm, k), lambda i: (i, 0)))
        args.append(z)
    return pl.pallas_call(
        body, name=name, grid=(s // tm,), out_shape=SDS((s, k), BF16),
        in_specs=in_specs, out_specs=pl.BlockSpec((tm, k), lambda i: (i, 0)),
        compiler_params=_params(("parallel",)),
    )(*args)


def _w_range(w_ref, c0, c1):
    nc = w_ref.shape[2]
    pieces, c = [], c0
    while c < c1:
        j = c // nc
        hi = min(nc, c1 - j * nc)
        pieces.append(w_ref[j, :, c - j * nc:hi])
        c = j * nc + hi
    return pieces[0] if len(pieces) == 1 else jnp.concatenate(pieces, axis=1)


def _mm_nt_normbwd(dz, w, x, gain, dres, name, after=None, w_post=None):
    parts, after = list(dz) if isinstance(dz, (list, tuple)) else [dz], _follow(after)
    widths = [p.shape[1] for p in parts]
    s, d = x.shape
    tm = _tile(s, TOKEN_TILE)
    chunk = 2 * MXU_WIDTH
    halves = 2 if tm % 32 == 0 else 1

    def body(*refs):
        dz_refs = refs[:len(parts)]
        w_ref, x_ref, g_ref, dres_ref = refs[len(parts):len(parts) + 4]
        dx_ref, dg_ref = refs[-2 - has_post:][:2]

        @pl.when(pl.program_id(0) == 0)
        def _():
            dg_ref[...] = jnp.zeros_like(dg_ref)

        for half in range(halves):
            rows = slice(half * tm // halves, (half + 1) * tm // halves)
            dh, base = None, 0
            for dz_ref, width in zip(dz_refs, widths):
                for c in range(0, width, chunk):
                    term = _dot(dz_ref[rows, c:c + chunk], _w_range(w_ref, base + c, base + c + chunk), NT)
                    dh = term if dh is None else dh + term
                base += width
            xv = x_ref[rows, :]
            r = lax.rsqrt(jnp.mean(xv * xv, axis=-1, keepdims=True) + RMS_EPS)
            xn = xv * r
            dg_ref[...] += jnp.sum(dh * xn, axis=0, keepdims=True)
            dxh = dh * g_ref[...]
            dx = dres_ref[rows, :] + r * (dxh - xn * jnp.mean(dxh * xn, axis=-1, keepdims=True))
            dx_ref[rows, :] = dx
            if has_post:
                refs[-1][rows, :] = _bf(_dot(_bf(dx), refs[len(parts) + 4][...], NT))

    has_post = w_post is not None
    post_in = [pl.BlockSpec(w_post.shape, lambda i: (0, 0))] if has_post else []
    post_out = [pl.BlockSpec((tm, w_post.shape[0]), lambda i: (i, 0))] if has_post else []
    return pl.pallas_call(
        body, name=name, grid=(s // tm,),
        out_shape=[SDS((s, d), F32), SDS((1, d), F32)] + ([SDS((s, w_post.shape[0]), BF16)] if has_post else []),
        in_specs=[pl.BlockSpec((tm, width), lambda i: (i, 0)) for width in widths]
        + [pl.BlockSpec(w.shape, lambda i: (0, 0, 0)),
           pl.BlockSpec((tm, d), lambda i: (i, 0)),
           pl.BlockSpec((1, d), lambda i: (0, 0)),
           pl.BlockSpec((tm, d), lambda i: (i, 0))] + post_in + [FOLLOW] * len(after),
        out_specs=[pl.BlockSpec((tm, d), lambda i: (i, 0)), pl.BlockSpec((1, d), lambda i: (0, 0))] + post_out,
        compiler_params=_params(("arbitrary",)),
    )(*parts, w, x, gain, dres, *([w_post] if has_post else []), *after)


def _mm_tn(a, b, ka, nb, a_tiled, split, name, after=None):
    a_parts, after = list(a) if isinstance(a, (list, tuple)) else [a], _follow(after)
    s = a_parts[0].shape[0]
    tm = _tile(s, 4 * TOKEN_TILE)
    nm = s // tm
    nj = a_parts[0].shape[1] // ka if a_tiled and len(a_parts) == 1 else (1 if a_tiled else b.shape[1] // nb)
    axis, parts = split
    pr, pc = (ka // parts, nb) if axis == 0 else (ka, nb // parts)

    def body(*refs):
        a_refs, b_ref = refs[:len(a_parts)], refs[len(a_parts)]
        o_ref, acc = refs[-2:]
        m = pl.program_id(1)

        @pl.when(m == 0)
        def _():
            acc[...] = jnp.zeros_like(acc)

        av = a_refs[0][...] if len(a_refs) == 1 else jnp.concatenate([r[...] for r in a_refs], axis=1)
        acc[...] += _dot(_bf(av), _bf(b_ref[...]), TN)

        @pl.when(m == nm - 1)
        def _():
            for q in range(parts):
                piece = acc[q * pr:(q + 1) * pr, :] if axis == 0 else acc[:, q * pc:(q + 1) * pc]
                o_ref[q] = piece.astype(o_ref.dtype)

    return pl.pallas_call(
        body, name=name, grid=(nj, nm), out_shape=SDS((nj * parts, pr, pc), BF16),
        in_specs=([pl.BlockSpec((tm, ka), (lambda j, m: (m, j)) if a_tiled else (lambda j, m: (m, 0)))]
                  if len(a_parts) == 1 else [pl.BlockSpec((tm, p.shape[1]), lambda j, m: (m, 0)) for p in a_parts])
        + [pl.BlockSpec((tm, nb), (lambda j, m: (m, 0)) if a_tiled else (lambda j, m: (m, j)))]
        + [FOLLOW] * len(after),
        out_specs=pl.BlockSpec((parts, pr, pc), lambda j, m: (j, 0, 0)),
        scratch_shapes=[pltpu.VMEM((ka, nb), F32)],
        compiler_params=_params(("parallel", "arbitrary")),
    )(*a_parts, b, *after)


def _final_loss(x, gain, target, name):
    s, d = x.shape
    tm = _tile(s, TOKEN_TILE)

    def body(x_ref, g_ref, t_ref, loss_ref, dx_ref, dg_ref):
        @pl.when(pl.program_id(0) == 0)
        def _():
            loss_ref[...] = jnp.zeros_like(loss_ref)
            dg_ref[...] = jnp.zeros_like(dg_ref)

        xv = x_ref[...]
        r = lax.rsqrt(jnp.mean(xv * xv, axis=-1, keepdims=True) + RMS_EPS)
        xn = xv * r
        err = xn * g_ref[...] - t_ref[...]
        loss_ref[...] += (0.5 / d) * jnp.sum(err * err)
        dy = err * (1.0 / d)
        dg_ref[...] += jnp.sum(dy * xn, axis=0, keepdims=True)
        dxh = dy * g_ref[...]
        dx_ref[...] = r * (dxh - xn * jnp.mean(dxh * xn, axis=-1, keepdims=True))

    return pl.pallas_call(
        body, name=name, grid=(s // tm,),
        out_shape=[SDS((8, 128), F32), SDS((s, d), F32), SDS((1, d), F32)],
        in_specs=[pl.BlockSpec((tm, d), lambda i: (i, 0)), pl.BlockSpec((1, d), lambda i: (0, 0)),
                  pl.BlockSpec((tm, d), lambda i: (i, 0))],
        out_specs=[pl.BlockSpec((8, 128), lambda i: (0, 0)), pl.BlockSpec((tm, d), lambda i: (i, 0)),
                   pl.BlockSpec((1, d), lambda i: (0, 0))],
        compiler_params=_params(("arbitrary",)),
    )(x, gain, target)


def _retention_tables(s):
    half = RET_HEAD_DIM // 2
    inv_freq = 1.0 / (RET_ROPE_BASE ** jnp.linspace(0.0, 1.0, half, dtype=F32))
    ang = jnp.arange(s, dtype=F32)[:, None] * inv_freq[None, :]
    cos, sin = jnp.cos(ang), jnp.sin(ang)
    cos_e = jnp.concatenate([cos, cos], axis=-1)
    sin_s = jnp.concatenate([-sin, sin], axis=-1)
    log_g = jnp.log1p(-jnp.power(2.0, -5.0 - jnp.arange(RET_HEADS, dtype=F32)))
    pos = jnp.arange(CHUNK, dtype=F32)
    dmat = jnp.exp(jnp.abs(pos[:, None] - pos[None, :])[None] * log_g[:, None, None])
    qdec = jnp.exp((pos[None, :] + 1.0) * log_g[:, None])
    kdec = jnp.exp((CHUNK - 1.0 - pos[None, :]) * log_g[:, None])
    lam = jnp.exp(CHUNK * log_g)
    wide = (RET_HEADS, CHUNK, RET_HEAD_DIM)
    return dict(cos=cos_e, sin=sin_s, dmat=dmat,
                qdec=jnp.broadcast_to(qdec[:, :, None], wide),
                kdec=jnp.broadcast_to(kdec[:, :, None], wide),
                lam=jnp.broadcast_to(lam[:, None, None], (RET_HEADS, RET_HEAD_DIM, RET_HEAD_DIM)))


def _swap_pairs(t):
    return pltpu.roll(t, RET_HEAD_DIM // 2, 1)


def _split_pairs(w, inverse=False):
    lead, nqk = w.shape[:-1], 2 * RET_WIDTH
    shape = (2 * RET_HEADS, 2, RET_HEAD_DIM // 2) if inverse else (2 * RET_HEADS, RET_HEAD_DIM // 2, 2)
    qk = jnp.swapaxes(w[..., :nqk].reshape(lead + shape), -1, -2).reshape(lead + (nqk,))
    return jnp.concatenate([qk, w[..., nqk:]], axis=-1)


def _head(h):
    return slice(h * RET_HEAD_DIM, (h + 1) * RET_HEAD_DIM)


def _ret_common_specs(tb, blk):
    zs = [pl.BlockSpec((tb, RET_WIDTH), functools.partial(lambda j, i: (blk(i), j), j)) for j in range(4)]
    tabs = [pl.BlockSpec((tb, RET_HEAD_DIM), lambda i: (blk(i), 0))] * 2
    consts = [pl.BlockSpec((1, RET_WIDTH), lambda i: (0, 0)),
              pl.BlockSpec((RET_HEADS, CHUNK, CHUNK), lambda i: (0, 0, 0)),
              pl.BlockSpec((RET_HEADS, CHUNK, RET_HEAD_DIM), lambda i: (0, 0, 0)),
              pl.BlockSpec((RET_HEADS, CHUNK, RET_HEAD_DIM), lambda i: (0, 0, 0)),
              pl.BlockSpec((RET_HEADS, RET_HEAD_DIM, RET_HEAD_DIM), lambda i: (0, 0, 0))]
    return zs + tabs + consts


def _ret_fwd(z, tabs, gn_gain, name):
    s = z.shape[0]
    tb = _tile(s, TOKEN_TILE)
    ncb = tb // CHUNK
    scale = RET_HEAD_DIM ** -0.5

    def body(q_ref, k_ref, v_ref, g_ref, cos_ref, sin_ref, gain_ref, dm_ref, qd_ref, kd_ref, lam_ref,
             o_ref, st_ref, ret_ref, s_scr, qr_scr, kr_scr):
        @pl.when(pl.program_id(0) == 0)
        def _():
            s_scr[...] = jnp.zeros_like(s_scr)

        cosv, sinv = cos_ref[...], sin_ref[...]
        for h in range(RET_HEADS):
            qh, kh = q_ref[:, _head(h)], k_ref[:, _head(h)]
            qr_scr[:, _head(h)] = qh * cosv + _swap_pairs(qh) * sinv
            kr_scr[:, _head(h)] = (kh * cosv + _swap_pairs(kh) * sinv) * scale

        def chunk(c, carry):
            rows = pl.ds(pl.multiple_of(c * CHUNK, CHUNK), CHUNK)
            for h in range(RET_HEADS):
                qc, kc, vc = qr_scr[rows, _head(h)], kr_scr[rows, _head(h)], v_ref[rows, _head(h)]
                a = _dot(_bf(qc), _bf(kc), NT) * dm_ref[h]
                st = s_scr[h]
                st_ref[c, h] = st
                o_ref[rows, _head(h)] = _dot(_bf(a), _bf(vc)) + _dot(_bf(qc * qd_ref[h]), _bf(st))
                s_scr[h] = st * lam_ref[h] + _dot(_bf(kc * kd_ref[h]), _bf(vc), TN)
            return carry

        lax.fori_loop(0, ncb, chunk, 0, unroll=True)
        for h in range(RET_HEADS):
            o = o_ref[:, _head(h)]
            mu = jnp.mean(o, axis=-1, keepdims=True)
            oc = o - mu
            y = oc * lax.rsqrt(jnp.mean(oc * oc, axis=-1, keepdims=True) + GN_EPS) * gain_ref[:, _head(h)]
            g = g_ref[:, _head(h)]
            ret_ref[:, _head(h)] = _bf(g / (1.0 + jnp.exp(-g)) * y)

    nc = s // CHUNK
    return pl.pallas_call(
        body, name=name, grid=(s // tb,),
        out_shape=[SDS((s, RET_WIDTH), F32), SDS((nc, RET_HEADS, RET_HEAD_DIM, RET_HEAD_DIM), F32),
                   SDS((s, RET_WIDTH), BF16)],
        in_specs=_ret_common_specs(tb, lambda i: i),
        out_specs=[pl.BlockSpec((tb, RET_WIDTH), lambda i: (i, 0)),
                   pl.BlockSpec((ncb, RET_HEADS, RET_HEAD_DIM, RET_HEAD_DIM), lambda i: (i, 0, 0, 0)),
                   pl.BlockSpec((tb, RET_WIDTH), lambda i: (i, 0))],
        scratch_shapes=[pltpu.VMEM((RET_HEADS, RET_HEAD_DIM, RET_HEAD_DIM), F32),
                        pltpu.VMEM((tb, RET_WIDTH), F32), pltpu.VMEM((tb, RET_WIDTH), F32)],
        compiler_params=_params(("arbitrary",)),
    )(z, z, z, z, tabs["cos"], tabs["sin"], gn_gain, tabs["dmat"], tabs["qdec"], tabs["kdec"], tabs["lam"])


def _ret_bwd(z, tabs, gn_gain, o_pre, states, du, name):
    s = z.shape[0]
    tb = _tile(s, TOKEN_TILE)
    ncb = tb // CHUNK
    nblk = s // tb
    scale = RET_HEAD_DIM ** -0.5
    rev = lambda i: nblk - 1 - i

    def body(q_ref, k_ref, v_ref, g_ref, cos_ref, sin_ref, gain_ref, dm_ref, qd_ref, kd_ref, lam_ref,
             o_ref, st_ref, dret_ref, dz_ref, dgain_ref, g_scr, qr_scr, kr_scr, do_scr, dq_scr, dk_scr):
        @pl.when(pl.program_id(0) == 0)
        def _():
            g_scr[...] = jnp.zeros_like(g_scr)
            dgain_ref[...] = jnp.zeros_like(dgain_ref)

        cosv, sinv = cos_ref[...], sin_ref[...]
        for h in range(RET_HEADS):
            hs = _head(h)
            qh, kh = q_ref[:, hs], k_ref[:, hs]
            qr_scr[:, hs] = qh * cosv + _swap_pairs(qh) * sinv
            kr_scr[:, hs] = (kh * cosv + _swap_pairs(kh) * sinv) * scale
            o = o_ref[:, hs]
            mu = jnp.mean(o, axis=-1, keepdims=True)
            oc = o - mu
            rstd = lax.rsqrt(jnp.mean(oc * oc, axis=-1, keepdims=True) + GN_EPS)
            yh = oc * rstd
            gain = gain_ref[:, hs]
            g = g_ref[:, hs]
            sg = 1.0 / (1.0 + jnp.exp(-g))
            dret = dret_ref[:, hs].astype(F32)
            dy = dret * (g * sg)
            dz_ref[:, 3 * RET_WIDTH + h * RET_HEAD_DIM:3 * RET_WIDTH + (h + 1) * RET_HEAD_DIM] = _bf(
                dret * (yh * gain) * (sg * (1.0 + g * (1.0 - sg))))
            dgain_ref[:, hs] += jnp.sum(dy * yh, axis=0, keepdims=True)
            dyh = dy * gain
            do_scr[:, hs] = rstd * (dyh - jnp.mean(dyh, axis=-1, keepdims=True)
                                    - yh * jnp.mean(dyh * yh, axis=-1, keepdims=True))

        def chunk(cc, carry):
            c = ncb - 1 - cc
            rows = pl.ds(pl.multiple_of(c * CHUNK, CHUNK), CHUNK)
            for h in range(RET_HEADS):
                hs = _head(h)
                qc, kc, vc, doc = _bf(qr_scr[rows, hs]), _bf(kr_scr[rows, hs]), _bf(v_ref[rows, hs]), _bf(do_scr[rows, hs])
                qdc, kdc = qd_ref[h], kd_ref[h]
                st, gs = _bf(st_ref[c, h]), g_scr[h]
                gsb = _bf(gs)
                dm = dm_ref[h]
                p = _bf(_dot(qc, kc, NT) * dm)
                da = _bf(_dot(doc, vc, NT) * dm)
                kt = _bf(kr_scr[rows, hs] * kdc)
                qt = _bf(qr_scr[rows, hs] * qdc)
                dz_ref[rows, 2 * RET_WIDTH + h * RET_HEAD_DIM:2 * RET_WIDTH + (h + 1) * RET_HEAD_DIM] = _bf(
                    _dot(p, doc, TN) + _dot(kt, gsb))
                dq_scr[rows, hs] = _dot(da, kc) + _dot(doc, st, NT) * qdc
                dk_scr[rows, hs] = _dot(da, qc, TN) + _dot(vc, gsb, NT) * kdc
                g_scr[h] = gs * lam_ref[h] + _dot(qt, doc, TN)
            return carry

        lax.fori_loop(0, ncb, chunk, 0, unroll=True)
        for h in range(RET_HEADS):
            hs = _head(h)
            dq, dk = dq_scr[:, hs], dk_scr[:, hs]
            dz_ref[:, h * RET_HEAD_DIM:(h + 1) * RET_HEAD_DIM] = _bf(dq * cosv - _swap_pairs(dq) * sinv)
            dz_ref[:, RET_WIDTH + h * RET_HEAD_DIM:RET_WIDTH + (h + 1) * RET_HEAD_DIM] = _bf(
                (dk * cosv - _swap_pairs(dk) * sinv) * scale)

    return pl.pallas_call(
        body, name=name, grid=(nblk,),
        out_shape=[SDS((s, AB_IN_WIDTH), BF16), SDS((1, RET_WIDTH), F32)],
        in_specs=_ret_common_specs(tb, rev)
        + [pl.BlockSpec((tb, RET_WIDTH), lambda i: (rev(i), 0)),
           pl.BlockSpec((ncb, RET_HEADS, RET_HEAD_DIM, RET_HEAD_DIM), lambda i: (rev(i), 0, 0, 0)),
           pl.BlockSpec((tb, RET_WIDTH), lambda i: (rev(i), 0))],
        out_specs=[pl.BlockSpec((tb, 4 * RET_WIDTH), lambda i: (rev(i), 0)),
                   pl.BlockSpec((1, RET_WIDTH), lambda i: (0, 0))],
        scratch_shapes=[pltpu.VMEM((RET_HEADS, RET_HEAD_DIM, RET_HEAD_DIM), F32)]
        + [pltpu.VMEM((tb, RET_WIDTH), F32)] * 5,
        compiler_params=_params(("arbitrary",)),
    )(z, z, z, z, tabs["cos"], tabs["sin"], gn_gain, tabs["dmat"], tabs["qdec"], tabs["kdec"], tabs["lam"],
      o_pre, states, du)


POOL_COL = 4 * RET_WIDTH // POOL_WIDTH


def _pooled(cur, prev, t0):
    tm = cur.shape[0]
    xx = jnp.concatenate([prev, cur], axis=0)
    sums = {1: xx}
    w = 1
    while w < POOL_WINDOWS[-1]:
        sums[2 * w] = sums[w] + pltpu.roll(sums[w], w, 0)
        w *= 2
    t = t0 + lax.broadcasted_iota(jnp.int32, (tm, 128), 0)
    outs = []
    for gi, w in enumerate(POOL_WINDOWS):
        cols = slice(gi * 128, (gi + 1) * 128)
        cnt = jnp.minimum(t + 1, w).astype(F32)
        outs.append(sums[w][POOL_HALO:, cols] / cnt - cur[:, cols])
    return outs


def _pool_fwd(z, w_pool, scale, name):
    s = z.shape[0]
    tm = _tile(s, TOKEN_TILE)
    hb = tm // POOL_HALO

    def body(p_ref, prev_ref, w_ref, sc_ref, o_ref):
        i = pl.program_id(0)
        prev = jnp.where(i > 0, prev_ref[...], 0.0)
        pooled = _pooled(p_ref[...], prev, i * tm)
        for gi in range(len(POOL_WINDOWS)):
            cols = slice(gi * 128, (gi + 1) * 128)
            o_ref[:, cols] = _bf(_dot(_bf(pooled[gi]), _bf(w_ref[gi])) * sc_ref[:, cols])

    return pl.pallas_call(
        body, name=name, grid=(s // tm,), out_shape=SDS((s, POOL_WIDTH), BF16),
        in_specs=[pl.BlockSpec((tm, POOL_WIDTH), lambda i: (i, POOL_COL)),
                  pl.BlockSpec((POOL_HALO, POOL_WIDTH), lambda i: (jnp.maximum(i * hb - 1, 0), POOL_COL)),
                  pl.BlockSpec(w_pool.shape, lambda i: (0, 0, 0)),
                  pl.BlockSpec((1, POOL_WIDTH), lambda i: (0, 0))],
        out_specs=pl.BlockSpec((tm, POOL_WIDTH), lambda i: (i, 0)),
        compiler_params=_params(("parallel",)),
    )(z, z, w_pool, scale)


def _pool_bwd(z, w_pool, scale, du, dz, name):
    s = z.shape[0]
    tm = _tile(s, TOKEN_TILE)
    hb = tm // POOL_HALO
    nblk = s // tm
    last_halo = s // POOL_HALO - 1

    def body(p_ref, prev_ref, w_ref, sc_ref, do_ref, don_ref, dz_ref, dp_ref, dw_ref, dsc_ref):
        i = pl.program_id(0)

        @pl.when(i == 0)
        def _():
            dw_ref[...] = jnp.zeros_like(dw_ref)
            dsc_ref[...] = jnp.zeros_like(dsc_ref)

        prev = jnp.where(i > 0, prev_ref[...], 0.0)
        pooled = _pooled(p_ref[...], prev, i * tm)
        dout = do_ref[...].astype(F32)
        dout_next = jnp.where(i < nblk - 1, don_ref[...].astype(F32), 0.0)
        sc = sc_ref[...]
        dmix = jnp.concatenate([dout * sc, dout_next * sc], axis=0)
        n = tm + POOL_HALO
        t = i * tm + lax.broadcasted_iota(jnp.int32, (n, 128), 0)
        for gi, w in enumerate(POOL_WINDOWS):
            cols = slice(gi * 128, (gi + 1) * 128)
            wg = _bf(w_ref[gi])
            pg = _bf(pooled[gi])
            dsc_ref[:, cols] += jnp.sum(dout[:, cols] * _dot(pg, wg), axis=0, keepdims=True)
            dw_ref[gi] += _dot(pg, _bf(dmix[:tm, cols]), TN)
            dpool = _dot(_bf(dmix[:, cols]), wg, NT)
            acc = dpool / jnp.minimum(t + 1, w).astype(F32)
            step = 1
            while step < w:
                acc = acc + pltpu.roll(acc, n - step, 0)
                step *= 2
            dp_ref[:, cols] = _bf(acc[:tm] - dpool[:tm])

    return pl.pallas_call(
        body, name=name, grid=(nblk,),
        out_shape=[SDS(dz.shape, BF16), SDS(w_pool.shape, F32), SDS((1, POOL_WIDTH), F32)],
        in_specs=[pl.BlockSpec((tm, POOL_WIDTH), lambda i: (i, POOL_COL)),
                  pl.BlockSpec((POOL_HALO, POOL_WIDTH), lambda i: (jnp.maximum(i * hb - 1, 0), POOL_COL)),
                  pl.BlockSpec(w_pool.shape, lambda i: (0, 0, 0)),
                  pl.BlockSpec((1, POOL_WIDTH), lambda i: (0, 0)),
                  pl.BlockSpec((tm, POOL_WIDTH), lambda i: (i, 1)),
                  pl.BlockSpec((POOL_HALO, POOL_WIDTH), lambda i: (jnp.minimum((i + 1) * hb, last_halo), 1)),
                  pl.BlockSpec(memory_space=pl.ANY)],
        out_specs=[pl.BlockSpec((tm, POOL_WIDTH), lambda i: (i, POOL_COL)),
                   pl.BlockSpec(w_pool.shape, lambda i: (0, 0, 0)),
                   pl.BlockSpec((1, POOL_WIDTH), lambda i: (0, 0))],
        input_output_aliases={6: 0},
        compiler_params=_params(("arbitrary",)),
    )(z, z, w_pool, scale, du, du, dz)


def _rel_onehot(offset=0):
    r = lax.broadcasted_iota(jnp.int32, (REL_PAD, ATT_DIAG), 0)
    c = lax.broadcasted_iota(jnp.int32, (REL_PAD, ATT_DIAG), 1) - offset
    rel = jnp.where((c >= 0) & (c < ATT_K_TILE), jnp.clip(LEFT_CHUNKS * CHUNK - c, -REL_CLIP, REL_CLIP) + REL_CLIP,
                    2 * REL_CLIP)
    return (rel == r).astype(BF16)


def _split3(v):
    hi = _bf(v)
    r1 = v - hi.astype(F32)
    mid = _bf(r1)
    return hi, mid, _bf(r1 - mid.astype(F32))


def _skew(v):
    return pltpu.roll(v, 0, 1, stride=1, stride_axis=0)


def _attn_bias(rel_bias, name):
    def body(t_ref, o_ref):
        oh = _rel_onehot()
        base = sum(_dot(part, oh) for part in _split3(t_ref[0]))
        full = _skew(jnp.broadcast_to(base[0:1], (ATT_Q_TILE, ATT_DIAG)))[:, :ATT_K_TILE]
        qc = lax.broadcasted_iota(jnp.int32, full.shape, 0) // CHUNK
        kc = lax.broadcasted_iota(jnp.int32, full.shape, 1) // CHUNK
        o_ref[0] = jnp.where((kc >= qc) & (kc <= qc + LEFT_CHUNKS), full, NEG_INF)

    t8 = jnp.broadcast_to(rel_bias[:, None, :], (ATT_HEADS, 8, REL_PAD))
    return pl.pallas_call(
        body, name=name, grid=(ATT_HEADS,), out_shape=SDS((ATT_HEADS, ATT_Q_TILE, ATT_K_TILE), F32),
        in_specs=[pl.BlockSpec((1, 8, REL_PAD), lambda h: (h, 0, 0))],
        out_specs=pl.BlockSpec((1, ATT_Q_TILE, ATT_K_TILE), lambda h: (h, 0, 0)),
        compiler_params=_params(("parallel",)),
    )(t8)


def _attn_dbias(dbias, name):
    def body(d_ref, o_ref):
        pad = jnp.zeros((ATT_Q_TILE, ATT_DIAG - ATT_K_TILE), F32)
        row = lax.broadcasted_iota(jnp.int32, (ATT_Q_TILE, ATT_Q_TILE), 0)
        lane = lax.broadcasted_iota(jnp.int32, (ATT_Q_TILE, ATT_Q_TILE), 1)
        flip = (row + lane == ATT_Q_TILE - 1).astype(BF16)
        upside = sum(_dot(flip, part) for part in _split3(jnp.concatenate([d_ref[0], pad], axis=1)))
        col = jnp.sum(_skew(upside), axis=0, keepdims=True)
        oh = _rel_onehot(ATT_Q_TILE - 1)
        col8 = jnp.broadcast_to(col, (8, ATT_DIAG))
        o_ref[0] = sum(_dot(part, oh, NT) for part in _split3(col8))

    out = pl.pallas_call(
        body, name=name, grid=(ATT_HEADS,), out_shape=SDS((ATT_HEADS, 8, REL_PAD), F32),
        in_specs=[pl.BlockSpec((1, ATT_Q_TILE, ATT_K_TILE), lambda h: (h, 0, 0))],
        out_specs=pl.BlockSpec((1, 8, REL_PAD), lambda h: (h, 0, 0)),
        compiler_params=_params(("parallel",)),
    )(dbias)
    return out[:, 0, :]


ATT_WIDTH = 128 * ATT_PAIRS
ATT_GROUPS = D_MODEL // ATT_WIDTH


def _attn_specs(nq):
    def tile(off, back):
        return pl.BlockSpec((ATT_Q_TILE, ATT_WIDTH),
                            lambda g, i: (jnp.maximum(jnp.minimum(i, nq - 1) - back, 0), off + g))

    backs = [ATT_BACK - b for b in range(ATT_BACK + 1)]
    return ([tile(0, 0)] + [tile(ATT_GROUPS, b) for b in backs] + [tile(2 * ATT_GROUPS, b) for b in backs]
            + [pl.BlockSpec((2 * ATT_PAIRS, ATT_Q_TILE, ATT_K_TILE), lambda g, i: (g, 0, 0))])


def _attn_weights(qh, k2, bias, i, masked):
    sc = _dot(qh, k2, NT) + bias
    if masked:
        kpos = (i - ATT_BACK) * ATT_Q_TILE + lax.broadcasted_iota(jnp.int32, sc.shape, 1)
        sc = jnp.where(kpos >= 0, sc, NEG_INF)
    e = jnp.exp(sc - jnp.max(sc, axis=-1, keepdims=True))
    return e, 1.0 / jnp.sum(e, axis=-1, keepdims=True)


def _first_head():
    return lax.broadcasted_iota(jnp.int32, (ATT_Q_TILE, 128), 1) < ATT_HEAD_DIM


def _pair_operands(q_ref, k_refs, v_refs, pp):
    cols = slice(pp * 128, (pp + 1) * 128)
    q2 = q_ref[:, cols] * ATT_HEAD_DIM ** -0.5
    k2 = jnp.concatenate([r[:, cols] for r in k_refs], axis=0)
    v2 = jnp.concatenate([r[:, cols] for r in v_refs], axis=0)
    return cols, q2, k2, v2


def _attn_fwd(z, bias, name):
    s = z.shape[0]
    nq = s // ATT_Q_TILE
    nt = ATT_BACK + 1

    def body(q_ref, *rest):
        k_refs, v_refs, (b_ref, o_ref) = rest[:nt], rest[nt:2 * nt], rest[2 * nt:]
        i = pl.program_id(1)
        first = _first_head()

        def compute(masked):
            for pp in range(ATT_PAIRS):
                cols, q2, k2, v2 = _pair_operands(q_ref, k_refs, v_refs, pp)
                outs = []
                for hh in range(2):
                    qh = jnp.where(first if hh == 0 else ~first, q2, 0)
                    e, inv = _attn_weights(qh, k2, b_ref[2 * pp + hh], i, masked)
                    outs.append(_dot(_bf(e), v2) * inv)
                o_ref[:, cols] = _bf(jnp.where(first, outs[0], outs[1]))

        pl.when(i < ATT_BACK)(lambda: compute(True))
        pl.when(i >= ATT_BACK)(lambda: compute(False))

    return pl.pallas_call(
        body, name=name, grid=(ATT_GROUPS, nq), out_shape=SDS((s, D_MODEL), BF16),
        in_specs=_attn_specs(nq),
        out_specs=pl.BlockSpec((ATT_Q_TILE, ATT_WIDTH), lambda g, i: (i, g)),
        compiler_params=_params(("parallel", "parallel")),
    )(*([z] * (1 + 2 * nt)), bias)


def _attn_bwd(z, bias, o, do, name):
    s = z.shape[0]
    nq = s // ATT_Q_TILE
    nt = ATT_BACK + 1

    def body(q_ref, *rest):
        k_refs, v_refs = rest[:nt], rest[nt:2 * nt]
        b_ref, o_ref, do_ref, dq_ref, dk_ref, dv_ref, db_ref, dk_acc, dv_acc = rest[2 * nt:]
        i = pl.program_id(1)
        first = _first_head()

        @pl.when(i == 0)
        def _():
            db_ref[...] = jnp.zeros_like(db_ref)
            dk_acc[...] = jnp.zeros_like(dk_acc)
            dv_acc[...] = jnp.zeros_like(dv_acc)

        def compute(masked):
            for pp in range(ATT_PAIRS):
                cols, q2, k2, v2 = _pair_operands(q_ref, k_refs, v_refs, pp)
                do2 = do_ref[:, cols].astype(F32)
                prod = do2 * o_ref[:, cols].astype(F32)
                dqs, dk, dv = [], None, None
                for hh in range(2):
                    mine = first if hh == 0 else ~first
                    qh = jnp.where(mine, q2, 0)
                    e, inv = _attn_weights(qh, k2, b_ref[2 * pp + hh], i, masked)
                    delta = jnp.sum(jnp.where(mine, prod, 0.0), axis=-1, keepdims=True) * inv
                    doh = _bf(jnp.where(mine, do2 * inv, 0.0))
                    ds = e * (_dot(doh, v2, NT) - delta)
                    db_ref[2 * pp + hh] += ds
                    dsb = _bf(ds)
                    dqs.append(_dot(dsb, k2))
                    dkh, dvh = _dot(dsb, qh, TN), _dot(_bf(e), doh, TN)
                    dk, dv = (dkh, dvh) if hh == 0 else (dk + dkh, dv + dvh)
                dq_ref[:, cols] = _bf(jnp.where(first, dqs[0], dqs[1]) * ATT_HEAD_DIM ** -0.5)
                for b in range(nt):
                    slot = (i + b + 1) % nt
                    rows = slice(b * ATT_Q_TILE, (b + 1) * ATT_Q_TILE)
                    if b < ATT_BACK:
                        dk_acc[slot, :, cols] += dk[rows]
                        dv_acc[slot, :, cols] += dv[rows]
                    else:
                        dk_acc[slot, :, cols] = dk[rows]
                        dv_acc[slot, :, cols] = dv[rows]

        pl.when(i < ATT_BACK)(lambda: compute(True))
        pl.when((i >= ATT_BACK) & (i < nq))(lambda: compute(False))
        done = (i + 1) % nt
        dk_ref[...] = _bf(dk_acc[done])
        dv_ref[...] = _bf(dv_acc[done])

    tile = pl.BlockSpec((ATT_Q_TILE, ATT_WIDTH), lambda g, i: (jnp.minimum(i, nq - 1), g))
    late = pl.BlockSpec((ATT_Q_TILE, ATT_WIDTH), lambda g, i: (jnp.maximum(i - ATT_BACK, 0), g))
    ring = pltpu.VMEM((nt, ATT_Q_TILE, ATT_WIDTH), F32)
    return pl.pallas_call(
        body, name=name, grid=(ATT_GROUPS, nq + ATT_BACK),
        out_shape=[SDS((s, D_MODEL), BF16)] * 3 + [SDS((ATT_HEADS, ATT_Q_TILE, ATT_K_TILE), F32)],
        in_specs=_attn_specs(nq) + [tile, tile],
        out_specs=[tile, late, late, pl.BlockSpec((2 * ATT_PAIRS, ATT_Q_TILE, ATT_K_TILE), lambda g, i: (g, 0, 0))],
        scratch_shapes=[ring, ring],
        compiler_params=_params(("parallel", "arbitrary")),
    )(*([z] * (1 + 2 * nt)), bias, o, do)


FWD_GROUPS = (
    (("ab_w_in", 0),),
    (("ab_w_out", 0), ("w_ffn_in", 0)),
    (("w_ffn_out", 0),),
    (("c_w_qkv", 0),),
    (("c_w_out", 0), ("w_ffn_in", 1), ("w_ffn_out", 1), ("ab_w_in", 1)),
    (("ab_w_out", 1), ("w_ffn_in", 2), ("w_ffn_out", 2), ("c_w_qkv", 1)),
    (("c_w_out", 1), ("w_ffn_in", 3), ("w_ffn_out", 3)),
)


def _local_step(x, target, small, comm):
    s = x.shape[0]
    tabs = _retention_tables(s)
    saved, w = [], comm.weight
    for layer in range(DEPTH):
        i = layer // 2
        sv = {"x0": x}
        g_mix = small["mix_norm"][layer:layer + 1]
        if layer % 2 == 0:
            at = (x,) + tuple(tabs.values()) if layer == 0 else x
            sv["h1"], sv["z"] = _norm_mm(x, g_mix, w("ab_w_in", i, at), AB_IN_WIDTH, F32, False, "ab_in_fwd")
            gn = small["ab_gn_gain"][i:i + 1]
            sv["o_pre"], sv["states"], ret = _ret_fwd(sv["z"], tabs, gn, "ret_fwd")
            pool = _pool_fwd(sv["z"], small["ab_w_pool"][i], small["ab_pool_scale"][i:i + 1], "pool_fwd")
            sv["u"] = (ret, pool)
            mixed = ([ret, pool], w("ab_w_out", i, ret))
        else:
            sv["h1"], sv["z"] = _norm_mm(x, g_mix, w("c_w_qkv", i, x), 3 * D_MODEL // N_DEV, BF16, False, "qkv_fwd")
            rb = jnp.pad(small["c_rel_bias"][i], ((0, 0), (0, REL_PAD - N_REL)))
            sv["bias"] = _attn_bias(rb, "attn_bias")
            sv["o"] = _attn_fwd(sv["z"], sv["bias"], "attn_fwd")
            mixed = ([sv["o"]], w("c_w_out", i, sv["o"]))
        x, sv["h2"], sv["z1"], sv["a"] = _norm_mm(x, small["ffn_norm"][layer:layer + 1], w("w_ffn_in", layer, mixed[0][0]),
                                                  D_FF // N_DEV, BF16, True, "ffn_in_fwd", mixed)
        sv["x1"] = x
        x = _mm_res([sv["a"]], w("w_ffn_out", layer, sv["a"]), x, "ffn_out_fwd")
        saved.append(sv)

    loss, dx, d_final = _final_loss(x, small["final_norm"][None, :], target, "final_loss")

    gs = {k: [None] * DEPTH for k in ("mix_norm", "ffn_norm")}
    for k in ("ab_gn_gain", "ab_w_pool", "ab_pool_scale", "c_rel_bias"):
        gs[k] = [None] * (DEPTH // 2)
    gs["final_norm"] = d_final[0]
    sent = None
    for layer in reversed(range(DEPTH)):
        i = layer // 2
        sv = saved[layer]
        dz1 = _mm_nt_rows(dx, w("w_ffn_out", layer), sv["z1"], "ffn_out_bwd")
        gw = {("w_ffn_out", layer): _mm_tn(sv["a"], dx, 1024, D_MODEL, True, (0, 2), "ffn_out_dw"),
              ("w_ffn_in", layer): _mm_tn(sv["h2"], dz1, D_MODEL, 1024, False, (1, 2), "ffn_in_dw",
                                          comm.after() if layer == 0 else None)}
        w_out = w("ab_w_out" if layer % 2 == 0 else "c_w_out", i)
        dx, dg, du = _mm_nt_normbwd(dz1, w("w_ffn_in", layer), sv["x1"], small["ffn_norm"][layer:layer + 1], dx,
                                    "ffn_in_bwd", sent, w_out)
        gs["ffn_norm"][layer] = dg[0]
        sent = None
        g_mix = small["mix_norm"][layer:layer + 1]
        if layer % 2 == 0:
            gw["ab_w_out", i] = _mm_tn(sv["u"], dx, D_MODEL, D_MODEL, True, (0, N_DEV), "mix_out_dw")
            if layer == 0:
                sent, gw = comm.send(gw), {}
            gn = small["ab_gn_gain"][i:i + 1]
            dz, dgn = _ret_bwd(sv["z"], tabs, gn, sv["o_pre"], sv["states"], du, "ret_bwd")
            dz, dwp, dsc = _pool_bwd(sv["z"], small["ab_w_pool"][i], small["ab_pool_scale"][i:i + 1], du, dz, "pool_bwd")
            gs["ab_gn_gain"][i], gs["ab_w_pool"][i], gs["ab_pool_scale"][i] = dgn[0], dwp, dsc[0]
            gw["ab_w_in", i] = _to_shard_major(_mm_tn(sv["h1"], dz, D_MODEL, AB_IN_WIDTH // 2, False, (1, 1), "ab_in_dw",
                                                      comm.after()), pairs_split=True)
            dx, dg = _mm_nt_normbwd(dz, w("ab_w_in", i), sv["x0"], g_mix, dx, "ab_in_bwd", sent)
        else:
            do = du
            gw["c_w_out", i] = _mm_tn(sv["o"], dx, D_MODEL, D_MODEL, True, (0, N_DEV), "mix_out_dw")
            dq, dk, dv, dbias = _attn_bwd(sv["z"], sv["bias"], sv["o"], do, "attn_bwd")
            gs["c_rel_bias"][i] = _attn_dbias(dbias, "attn_dbias")[:, :N_REL]
            tiles = [_mm_tn(sv["h1"], part, D_MODEL, D_MODEL, False, (1, 1), "qkv_dw", after)
                     for part, after in ((dq, None), (dk, None), (dv, comm.after()))]
            gw["c_w_qkv", i] = _to_shard_major(jnp.concatenate(tiles, axis=0))
            dx, dg = _mm_nt_normbwd([dq, dk, dv], w("c_w_qkv", i), sv["x0"], g_mix, dx, "qkv_bwd", sent)
        gs["mix_norm"][layer] = dg[0]
        if layer > 0:
            sent = comm.send(gw)
    gsmall = {k: (jnp.stack(v) if isinstance(v, list) else v) for k, v in gs.items()}
    return loss, dx, gw, gsmall


BIG = ("w_ffn_in", "w_ffn_out", "ab_w_in", "ab_w_out", "c_w_qkv", "c_w_out")
SMALL = ("mix_norm", "ffn_norm", "ab_gn_gain", "ab_w_pool", "ab_pool_scale", "c_rel_bias", "final_norm")
N_PEERS = N_DEV - 1
FLIPS = [(fx, fy, fc) for fx in (0, 1) for fy in (0, 1) for fc in (0, 1)][1:]


def _peers():
    x, y, c = (lax.axis_index(a) for a in MESH_AXES)
    peers = []
    for fx, fy, fc in FLIPS:
        px, py, pc = (1 - x if fx else x), (1 - y if fy else y), (1 - c if fc else c)
        peers.append(((px, py, pc), 4 * px + 2 * py + pc))
    return 4 * x + 2 * y + c, peers


def _exchange(srcs, by_slot, name, collective_id):
    n = len(srcs)
    src_refs = [jax.new_ref(a, memory_space=pltpu.MemorySpace.HBM) for a in srcs]
    land_refs = [jax.empty_ref(SDS((N_DEV,) + (a.shape[1:] if slotted else a.shape), a.dtype),
                               memory_space=pltpu.MemorySpace.HBM) for a, slotted in zip(srcs, by_slot)]

    @pl.kernel(mesh=plsc.ScalarSubcoreMesh(axis_name="sequencer", num_cores=1), name=name,
               scratch_types=(pltpu.SemaphoreType.DMA((n * N_PEERS,)), pltpu.SemaphoreType.DMA((n * N_PEERS,)),
                              pltpu.SemaphoreType.DMA((n,))),
               compiler_params=pltpu.CompilerParams(collective_id=collective_id))
    def launch(send_sems, recv_sems, local_sems):
        me, peers = _peers()
        barrier = pltpu.get_barrier_semaphore()
        for pos, _ in peers:
            pl.semaphore_signal(barrier, inc=1, device_id=pos, device_id_type=pl.DeviceIdType.MESH)
        pl.semaphore_wait(barrier, N_PEERS)
        waits = []
        for k in range(n):
            own = pltpu.make_async_copy(src_refs[k].at[me] if by_slot[k] else src_refs[k], land_refs[k].at[me],
                                        local_sems.at[k])
            own.start()
            waits.append(own.wait)
            for rel, (pos, slot) in enumerate(peers):
                src = src_refs[k].at[slot] if by_slot[k] else src_refs[k]
                sems = dict(send_sem=send_sems.at[k * N_PEERS + rel], recv_sem=recv_sems.at[k * N_PEERS + rel],
                            device_id=pos, device_id_type=pl.DeviceIdType.MESH)
                send = pltpu.make_async_remote_copy(src_ref=src, dst_ref=land_refs[k].at[me], **sems)
                send.start()
                arrival = pltpu.make_async_remote_copy(src_ref=src, dst_ref=land_refs[k].at[slot], **sems)
                waits += [send.wait_send, arrival.wait_recv]
        for wait in waits:
            wait()

    launch()
    return [r[...] for r in land_refs]


def _cast_group(weights, keys, after, name):
    after = _follow(after)

    def body(*refs):
        n = len(keys)
        for i_ref, o_ref in zip(refs[:n], refs[n + len(after):]):
            o_ref[...] = _bf(i_ref[...])

    def layer_spec(shape, l):
        return pl.BlockSpec((None,) + shape[1:], lambda i: (l, 0, 0))

    whole = lambda shape: pl.BlockSpec(shape, lambda i: (0, 0))
    ins = [weights[k] for k, _ in keys]
    return pl.pallas_call(
        body, name=name, grid=(1,), out_shape=[SDS(w.shape[1:], BF16) for w in ins],
        in_specs=[layer_spec(w.shape, l) for w, (_, l) in zip(ins, keys)] + [FOLLOW] * len(after),
        out_specs=[whole(w.shape[1:]) for w in ins],
        compiler_params=_params(("arbitrary",)),
    )(*ins, *after)


def _to_shard_major(g, pairs_split=False):
    nj, ka, nb = g.shape
    full = jnp.transpose(g, (1, 0, 2)).reshape(ka, nj * nb)
    if pairs_split:
        full = _split_pairs(full, inverse=True)
    return jnp.transpose(full.reshape(ka, N_DEV, nj * nb // N_DEV), (1, 0, 2))


def _from_gathered(name, g):
    if name in ("w_ffn_out", "ab_w_out", "c_w_out"):
        return g.reshape(g.shape[0] * g.shape[1], g.shape[2])
    if name == "ab_w_in":
        return _split_pairs(jnp.transpose(g, (1, 0, 2)).reshape(1, g.shape[1], N_DEV * g.shape[2]))
    return g


class _Comm:
    def __init__(self, weights):
        self.weights_f32 = weights
        self.gathered = {}
        self.got = {}
        self.calls = 0
        self.ended = []
        self.opened = -1

    def _exchange(self, srcs, by_slot, name):
        self.calls += 1
        got = _exchange(srcs, by_slot, name, self.calls)
        self.ended = got[:1]
        return got

    def _gather(self, group, at):
        keys = FWD_GROUPS[group]
        shards = _cast_group(self.weights_f32, keys, self.ended + _follow(at), "cast_%d" % group)
        got = self._exchange(shards, [False] * len(keys), "gather_%d" % group)
        self.gathered.update((k, _from_gathered(k[0], arr)) for k, arr in zip(keys, got))

    def weight(self, name, layer, at=None):
        if (name, layer) not in self.gathered:
            self._gather(0, at[0] if isinstance(at, tuple) else at)
        group = next(g for g, keys in enumerate(FWD_GROUPS) if (name, layer) in keys)
        if group == self.opened + 1:
            self.opened = group
            if group + 1 < len(FWD_GROUPS):
                self._gather(group + 1, at)
        return self.gathered[name, layer]

    def after(self):
        return self.ended

    def send(self, grads, shared=None):
        shared = shared or {}
        keys = list(grads) + list(shared)
        srcs = list(grads.values()) + list(shared.values())
        got = self._exchange(srcs, [True] * len(grads) + [False] * len(shared), "scatter_%d" % self.calls)
        self.got.update(zip(keys, got))
        return list(grads.values())

    def received(self):
        return self.got


def _adamw_math(g, w, m, v):
    m2 = ADAM_B1 * m + (1.0 - ADAM_B1) * g
    v2 = ADAM_B2 * v + (1.0 - ADAM_B2) * jnp.square(g)
    m_hat = m2 / (1.0 - ADAM_B1 ** ADAM_STEP)
    v_hat = v2 / (1.0 - ADAM_B2 ** ADAM_STEP)
    delta = -ADAM_LR * (m_hat / (jnp.sqrt(v_hat) + ADAM_EPS) + ADAM_WD * w)
    return delta, m2, v2


def _adamw(recv, w, m, v, name):
    nl, r, c = w.shape
    tr = _tile(r, 256)

    def body(*refs):
        g_refs = refs[:nl]
        w_ref, m_ref, v_ref, go_ref, d_ref, mo_ref, vo_ref = refs[nl:]
        for l in range(nl):
            @pl.when(pl.program_id(0) == l)
            def _():
                g = g_refs[l][0].astype(F32)
                for p in range(1, N_DEV):
                    g = g + g_refs[l][p].astype(F32)
                go_ref[...] = g
                d_ref[...], mo_ref[...], vo_ref[...] = _adamw_math(g, w_ref[...], m_ref[...], v_ref[...])

    def recv_spec(l):
        return pl.BlockSpec((N_DEV, tr, c), lambda layer, i: (0, jnp.where(layer == l, i, 0), 0))

    blk = pl.BlockSpec((None, tr, c), lambda l, i: (l, i, 0))
    return pl.pallas_call(
        body, name=name, grid=(nl, r // tr), out_shape=[SDS(w.shape, F32)] * 4,
        in_specs=[recv_spec(l) for l in range(nl)] + [blk, blk, blk],
        out_specs=[blk] * 4,
        compiler_params=_params(("arbitrary", "arbitrary")),
    )(*recv, w, m, v)


def _adamw_small(recv, loss_parts, w, m, v, name):
    n = len(w)

    def total(ref):
        t = ref[0].astype(F32)
        for p in range(1, N_DEV):
            t = t + ref[p].astype(F32)
        return t

    def body(*refs):
        g_refs, loss_ref = refs[:n], refs[n]
        w_refs, m_refs, v_refs = refs[n + 1:2 * n + 1], refs[2 * n + 1:3 * n + 1], refs[3 * n + 1:4 * n + 1]
        outs = refs[4 * n + 1:]
        for i in range(n):
            g = total(g_refs[i])
            outs[4 * i][...] = g
            outs[4 * i + 1][...], outs[4 * i + 2][...], outs[4 * i + 3][...] = _adamw_math(
                g, w_refs[i][...], m_refs[i][...], v_refs[i][...])
        outs[4 * n][...] = total(loss_ref)

    out_shape = [SDS(p.shape, F32) for p in w for _ in range(4)] + [SDS(loss_parts.shape[1:], F32)]
    outs = pl.pallas_call(body, name=name, out_shape=out_shape,
                          compiler_params=_params(None))(*recv, loss_parts, *w, *m, *v)
    return [outs[4 * i:4 * i + 4] for i in range(n)], outs[-1]


def kernel(x, mix_norm, ffn_norm, w_ffn_in, w_ffn_out, ab_w_in, ab_gn_gain, ab_w_pool, ab_pool_scale, ab_w_out, c_w_qkv, c_rel_bias, c_w_out, final_norm, loss_target, m_mix_norm, m_ffn_norm, m_w_ffn_in, m_w_ffn_out, m_ab_w_in, m_ab_gn_gain, m_ab_w_pool, m_ab_pool_scale, m_ab_w_out, m_c_w_qkv, m_c_rel_bias, m_c_w_out, m_final_norm, v_mix_norm, v_ffn_norm, v_w_ffn_in, v_w_ffn_out, v_ab_w_in, v_ab_gn_gain, v_ab_w_pool, v_ab_pool_scale, v_ab_w_out, v_c_w_qkv, v_c_rel_bias, v_c_w_out, v_final_norm):
    args = dict(locals())
    weights = {k: args[k] for k in BIG + SMALL}
    moments_m = {k: args["m_" + k] for k in BIG + SMALL}
    moments_v = {k: args["v_" + k] for k in BIG + SMALL}

    small = {k: weights[k] for k in SMALL}
    rows = lambda a: a.reshape(1, -1) if a.ndim == 1 else a

    comm = _Comm(weights)
    loss, dx, last_grads, gsmall = _local_step(x[0], loss_target[0], small, comm)
    travel = lambda k: rows(gsmall[k]).astype(BF16 if k == "ab_w_pool" else F32)
    comm.send(last_grads, {**{k: travel(k) for k in SMALL}, "loss": loss})
    recv = comm.received()

    outs = {}
    for k in BIG:
        layers = [recv[k, l] for l in range(weights[k].shape[0])]
        outs[k] = _adamw(layers, weights[k], moments_m[k], moments_v[k], "adamw_" + k)
    updated, total = _adamw_small([recv[k] for k in SMALL], recv["loss"], [rows(small[k]) for k in SMALL],
                                  [rows(moments_m[k]) for k in SMALL], [rows(moments_v[k]) for k in SMALL], "adamw_small")
    for k, parts in zip(SMALL, updated):
        outs[k] = [p.reshape(small[k].shape) for p in parts]

    order = SMALL[:2] + BIG[:2] + ("ab_w_in", "ab_gn_gain", "ab_w_pool", "ab_pool_scale", "ab_w_out",
                                   "c_w_qkv", "c_rel_bias", "c_w_out", "final_norm")
    result = [total[0, 0], dx[None]]
    for part in range(4):
        result += [outs[k][part] for k in order]
    return tuple(result)
```
